```python
import jax, jax.numpy as jnp
from jax import lax
import numpy as np

D_MODEL = 2048
BATCH = 1
SEQ = 16384
DEPTH = 1

NSA_HEADS = 16
NSA_GROUPS = 4
NSA_HPG = NSA_HEADS // NSA_GROUPS
NSA_DIM = 64
NSA_WIDTH = NSA_HEADS * NSA_DIM
KV_WIDTH = NSA_GROUPS * NSA_DIM
N_KV_STREAMS = 6
CMP_BLOCK = 32
CMP_STRIDE = 16
CMP_HIDDEN = 256
SEL_BLOCK = 64
SEL_TOPK = 16
WINDOW = 512
QUERY_BLOCK = 128
SGU_GROUPS = 8
SGU_GROUP_DIM = 128
SGU_WIDTH = SGU_GROUPS * SGU_GROUP_DIM
SGU_CHUNK = 128
N_BRANCHES = 2
MEM_TOKENS = 256
MEM_HEADS = 4
MEM_DIM = 128
MEM_WIDTH = MEM_HEADS * MEM_DIM
FFN_HIDDEN = (8 * D_MODEL + 3 * 256 - 1) // (3 * 256) * 256

NORM_EPS = 1e-6
MASK_VALUE = -1e30
FORCED_SCORE = 1e6
IN_SPLITS = (NSA_WIDTH, N_KV_STREAMS * KV_WIDTH, NSA_HEADS * 3, SGU_WIDTH, SGU_WIDTH, N_BRANCHES * D_MODEL)
IN_WIDTH = sum(IN_SPLITS)

kernel_name = "hybrid_nsa_sgu_gated_block"


def rmsnorm(x, g):
    xf = x.astype(jnp.float32)
    y = xf * lax.rsqrt(jnp.mean(xf * xf, axis=-1, keepdims=True) + NORM_EPS)
    return (y * g.astype(jnp.float32)).astype(x.dtype)


def layernorm(x, g, b):
    xf = x.astype(jnp.float32)
    mu = jnp.mean(xf, axis=-1, keepdims=True)
    var = jnp.mean(jnp.square(xf - mu), axis=-1, keepdims=True)
    y = (xf - mu) * lax.rsqrt(var + NORM_EPS)
    return (y * g.astype(jnp.float32) + b.astype(jnp.float32)).astype(x.dtype)


def masked_softmax(s, mask):
    s = jnp.where(mask, s.astype(jnp.float32), MASK_VALUE)
    p = jax.nn.softmax(s, axis=-1)
    return jnp.where(mask, p, 0.0)


def compress_blocks(kv, pe, w1, b1, w2):
    b, s = kv.shape[:2]
    nc = (s - CMP_BLOCK) // CMP_STRIDE + 1
    idx = jnp.arange(nc)[:, None] * CMP_STRIDE + jnp.arange(CMP_BLOCK)[None, :]
    blocks = kv[:, idx] + pe[None, None, :, None, :]
    blocks = blocks.transpose(0, 3, 1, 2, 4).reshape(b, NSA_GROUPS, nc, CMP_BLOCK * NSA_DIM)
    hid = jax.nn.gelu(blocks @ w1 + b1)
    return hid @ w2


def block_coverage(nc, nb):
    cs = np.arange(nc)[:, None] * CMP_STRIDE
    ss = np.arange(nb)[None, :] * SEL_BLOCK
    cov = np.clip(np.minimum(cs + CMP_BLOCK, ss + SEL_BLOCK) - np.maximum(cs, ss), 0, None) / CMP_BLOCK
    return jnp.asarray(cov, dtype=jnp.float32)


def nsa_attention(q, kc, vc, ks, vs, kw, vw, gates):
    b, s = q.shape[:2]
    nqb = s // QUERY_BLOCK
    nc = kc.shape[2]
    nb = s // SEL_BLOCK
    topk = min(SEL_TOPK, nb)
    cov = block_coverage(nc, nb)
    q_blocks = (q * (NSA_DIM ** -0.5)).reshape(b, nqb, QUERY_BLOCK, NSA_GROUPS, NSA_HPG, NSA_DIM)
    q_blocks = q_blocks.transpose(1, 0, 3, 4, 2, 5)
    ks_blocks = ks.transpose(0, 2, 1, 3).reshape(b, NSA_GROUPS, nb, SEL_BLOCK, NSA_DIM)
    vs_blocks = vs.transpose(0, 2, 1, 3).reshape(b, NSA_GROUPS, nb, SEL_BLOCK, NSA_DIM)
    pad = ((0, 0), (WINDOW, 0), (0, 0), (0, 0))
    kw_pad = jnp.pad(kw, pad).transpose(0, 2, 1, 3)
    vw_pad = jnp.pad(vw, pad).transpose(0, 2, 1, 3)
    cmp_end = jnp.arange(nc) * CMP_STRIDE + CMP_BLOCK - 1
    blk = jnp.arange(nb)
    b_idx = jnp.arange(b)[:, None, None, None]
    g_idx = jnp.arange(NSA_GROUPS)[None, :, None, None]
    sel_off = jnp.arange(SEL_BLOCK)
    win_off = jnp.arange(QUERY_BLOCK + WINDOW)

    def block_fn(args):
        qb, i = args
        start = i * QUERY_BLOCK
        t = start + jnp.arange(QUERY_BLOCK)
        s_c = jnp.einsum('bghqd,bgnd->bghqn', qb, kc)
        p_c = masked_softmax(s_c, cmp_end[None, :] <= t[:, None])
        o_c = jnp.einsum('bghqn,bgnd->bghqd', p_c.astype(vc.dtype), vc)
        imp = jnp.einsum('bghqn,nj->bgqj', p_c, cov)
        jt = (t // SEL_BLOCK)[:, None]
        forced = (blk == 0) | (blk == jt) | (blk == jt - 1)
        score = jnp.where(forced, FORCED_SCORE, jnp.where(blk <= jt, imp, -1.0))
        _, idx = lax.top_k(score, topk)
        k_sel = ks_blocks[b_idx, g_idx, idx].reshape(b, NSA_GROUPS, QUERY_BLOCK, topk * SEL_BLOCK, NSA_DIM)
        v_sel = vs_blocks[b_idx, g_idx, idx].reshape(b, NSA_GROUPS, QUERY_BLOCK, topk * SEL_BLOCK, NSA_DIM)
        pos = (idx[..., None] * SEL_BLOCK + sel_off).reshape(b, NSA_GROUPS, QUERY_BLOCK, topk * SEL_BLOCK)
        mask_s = pos <= t[:, None]
        s_s = jnp.einsum('bghqd,bgqkd->bghqk', qb, k_sel)
        p_s = masked_softmax(s_s, mask_s[:, :, None])
        o_s = jnp.einsum('bghqk,bgqkd->bghqd', p_s.astype(v_sel.dtype), v_sel)
        k_w = lax.dynamic_slice_in_dim(kw_pad, start, QUERY_BLOCK + WINDOW, axis=2)
        v_w = lax.dynamic_slice_in_dim(vw_pad, start, QUERY_BLOCK + WINDOW, axis=2)
        kpos = start - WINDOW + win_off
        mask_w = (kpos[None, :] <= t[:, None]) & (kpos[None, :] > t[:, None] - WINDOW) & (kpos[None, :] >= 0)
        s_w = jnp.einsum('bghqd,bgkd->bghqk', qb, k_w)
        p_w = masked_softmax(s_w, mask_w)
        o_w = jnp.einsum('bghqk,bgkd->bghqd', p_w.astype(v_w.dtype), v_w)
        return (o_c, o_s, o_w)

    o_c, o_s, o_w = lax.map(block_fn, (q_blocks, jnp.arange(nqb, dtype=jnp.int32)))

    def unblock(o):
        return o.transpose(1, 0, 4, 2, 3, 5).reshape(b, s, NSA_HEADS, NSA_DIM)

    out = gates[..., 0:1] * unblock(o_c) + gates[..., 1:2] * unblock(o_s) + gates[..., 2:3] * unblock(o_w)
    return out.reshape(b, s, NSA_WIDTH)


def spatial_gating(u, v, ln_g, ln_b, ws, bs):
    b, s, _ = v.shape
    v = layernorm(v, ln_g, ln_b)
    vc = v.reshape(b, s // SGU_CHUNK, SGU_CHUNK, SGU_GROUPS, SGU_GROUP_DIM)
    tri = jnp.tril(jnp.ones((SGU_CHUNK, SGU_CHUNK), dtype=bool))
    w = jnp.where(tri, ws, 0.0)
    mixed = jnp.einsum('gts,bcsgd->bctgd', w, vc) + bs.T[None, None, :, :, None]
    return u * mixed.reshape(b, s, SGU_WIDTH)


def setup_inputs(seed: int = 0) -> dict:
    key = jax.random.key(seed)
    ks = jax.random.split(key, 32)

    def nrm(k, shape, scale):
        return jax.random.normal(k, shape, jnp.float32) * scale

    def gain(k, shape):
        return 1.0 + 0.1 * jax.random.normal(k, shape, jnp.float32)

    L = DEPTH
    return {
        "x": nrm(ks[0], (BATCH, SEQ, D_MODEL), 1.0),
        "mem": nrm(ks[1], (BATCH, MEM_TOKENS, D_MODEL), 1.0),
        "norm_mix_g": gain(ks[2], (L, D_MODEL)),
        "w_in": nrm(ks[3], (L, D_MODEL, IN_WIDTH), D_MODEL ** -0.5),
        "cmp_pe_k": nrm(ks[4], (L, CMP_BLOCK, NSA_DIM), 0.5),
        "cmp_k_w1": nrm(ks[5], (L, CMP_BLOCK * NSA_DIM, CMP_HIDDEN), (CMP_BLOCK * NSA_DIM) ** -0.5),
        "cmp_k_b1": nrm(ks[6], (L, CMP_HIDDEN), 0.01),
        "cmp_k_w2": nrm(ks[7], (L, CMP_HIDDEN, NSA_DIM), CMP_HIDDEN ** -0.5),
        "cmp_pe_v": nrm(ks[8], (L, CMP_BLOCK, NSA_DIM), 0.5),
        "cmp_v_w1": nrm(ks[9], (L, CMP_BLOCK * NSA_DIM, CMP_HIDDEN), (CMP_BLOCK * NSA_DIM) ** -0.5),
        "cmp_v_b1": nrm(ks[10], (L, CMP_HIDDEN), 0.01),
        "cmp_v_w2": nrm(ks[11], (L, CMP_HIDDEN, NSA_DIM), CMP_HIDDEN ** -0.5),
        "sgu_ln_g": gain(ks[12], (L, SGU_WIDTH)),
        "sgu_ln_b": nrm(ks[13], (L, SGU_WIDTH), 0.01),
        "sgu_ws": nrm(ks[14], (L, SGU_GROUPS, SGU_CHUNK, SGU_CHUNK), SGU_CHUNK ** -0.5),
        "sgu_b": gain(ks[15], (L, SGU_GROUPS, SGU_CHUNK)),
        "w_proj_a": nrm(ks[16], (L, NSA_WIDTH, D_MODEL), NSA_WIDTH ** -0.5),
        "w_proj_b": nrm(ks[17], (L, SGU_WIDTH, D_MODEL), SGU_WIDTH ** -0.5),
        "w_mix_out": nrm(ks[18], (L, D_MODEL, D_MODEL), D_MODEL ** -0.5),
        "norm_cross_g": gain(ks[19], (L, D_MODEL)),
        "norm_mem_g": gain(ks[20], (L, D_MODEL)),
        "w_mq": nrm(ks[21], (L, D_MODEL, MEM_WIDTH), D_MODEL ** -0.5),
        "w_mkv": nrm(ks[22], (L, D_MODEL, 2 * MEM_WIDTH), D_MODEL ** -0.5),
        "w_mo": nrm(ks[23], (L, MEM_WIDTH, D_MODEL), MEM_WIDTH ** -0.5),
        "norm_ffn_g": gain(ks[24], (L, D_MODEL)),
        "w_ffn_in": nrm(ks[25], (L, D_MODEL, 2 * FFN_HIDDEN), D_MODEL ** -0.5),
        "w_ffn_out": nrm(ks[26], (L, FFN_HIDDEN, D_MODEL), FFN_HIDDEN ** -0.5),
        "norm_final_g": gain(ks[27], (D_MODEL,)),
    }


def reference(x, mem, norm_mix_g, w_in, cmp_pe_k, cmp_k_w1, cmp_k_b1, cmp_k_w2, cmp_pe_v, cmp_v_w1, cmp_v_b1, cmp_v_w2,
              sgu_ln_g, sgu_ln_b, sgu_ws, sgu_b, w_proj_a, w_proj_b, w_mix_out, norm_cross_g, norm_mem_g,
              w_mq, w_mkv, w_mo, norm_ffn_g, w_ffn_in, w_ffn_out, norm_final_g):
    b, s, _ = x.shape
    split_points = np.cumsum(IN_SPLITS)[:-1].tolist()
    for l in range(DEPTH):
        h = rmsnorm(x, norm_mix_g[l])
        proj = h @ w_in[l]
        q, kv, nsa_g, u, v, merge_g = jnp.split(proj, split_points, axis=-1)
        q = q.reshape(b, s, NSA_HEADS, NSA_DIM)
        kv = kv.reshape(b, s, N_KV_STREAMS, NSA_GROUPS, NSA_DIM)
        kc = compress_blocks(kv[:, :, 0], cmp_pe_k[l], cmp_k_w1[l], cmp_k_b1[l], cmp_k_w2[l])
        vc = compress_blocks(kv[:, :, 1], cmp_pe_v[l], cmp_v_w1[l], cmp_v_b1[l], cmp_v_w2[l])
        nsa_gates = jax.nn.sigmoid(nsa_g.reshape(b, s, NSA_HEADS, 3))
        o_a = nsa_attention(q, kc, vc, kv[:, :, 2], kv[:, :, 3], kv[:, :, 4], kv[:, :, 5], nsa_gates)
        o_b = spatial_gating(jax.nn.gelu(u), jax.nn.gelu(v), sgu_ln_g[l], sgu_ln_b[l], sgu_ws[l], sgu_b[l])
        g = jax.nn.sigmoid(merge_g).reshape(b, s, N_BRANCHES, D_MODEL)
        merged = g[:, :, 0] * (o_a @ w_proj_a[l]) + g[:, :, 1] * (o_b @ w_proj_b[l])
        x = x + merged @ w_mix_out[l]
        hx = rmsnorm(x, norm_cross_g[l])
        hm = rmsnorm(mem, norm_mem_g[l])
        mq = (hx @ w_mq[l]).reshape(b, s, MEM_HEADS, MEM_DIM) * (MEM_DIM ** -0.5)
        mk, mv = jnp.split(hm @ w_mkv[l], 2, axis=-1)
        mk = mk.reshape(b, -1, MEM_HEADS, MEM_DIM)
        mv = mv.reshape(b, -1, MEM_HEADS, MEM_DIM)
        sc = jnp.einsum('bshd,bmhd->bhsm', mq, mk).astype(jnp.float32)
        p = jax.nn.softmax(sc, axis=-1)
        mo = jnp.einsum('bhsm,bmhd->bshd', p.astype(mv.dtype), mv).reshape(b, s, MEM_WIDTH)
        x = x + mo @ w_mo[l]
        hf = rmsnorm(x, norm_ffn_g[l])
        gt, up = jnp.split(hf @ w_ffn_in[l], 2, axis=-1)
        x = x + (jax.nn.silu(gt) * up) @ w_ffn_out[l]
    return rmsnorm(x, norm_final_g)
```

```python
import functools

import numpy as np
import jax
import jax.numpy as jnp
from jax import lax
from jax.experimental import pallas as pl
from jax.experimental.pallas import tpu as pltpu

F32 = jnp.float32
BF16 = jnp.bfloat16

NORM_EPS = 1e-6
MASK_VALUE = -1e30
FORCED_SCORE = 1e6

NSA_HEADS = 16
NSA_GROUPS = 4
NSA_HPG = NSA_HEADS // NSA_GROUPS
NSA_DIM = 64
N_KV_STREAMS = 6
CMP_BLOCK = 32
CMP_STRIDE = 16
SEL_BLOCK = 64
SEL_TOPK = 16
WINDOW = 512
QUERY_BLOCK = 128
SGU_GROUPS = 8
SGU_CHUNK = 128
MEM_HEADS = 4
MEM_DIM = 128

LANES = 128
SEL_TILE = 512
VMEM_LIMIT = 56 * 1024 * 1024


def _params(*sem):
    return pltpu.CompilerParams(dimension_semantics=sem, vmem_limit_bytes=VMEM_LIMIT)


def _rms(x, g):
    return x * lax.rsqrt(jnp.mean(x * x, axis=-1, keepdims=True) + NORM_EPS) * g


def _dot(a, b):
    return jnp.dot(a, b, preferred_element_type=F32)


def _norm_matmul_kernel(x_ref, g_ref, w_ref, o_ref, h_ref):
    @pl.when(pl.program_id(1) == 0)
    def _():
        h_ref[...] = _rms(x_ref[...], g_ref[...]).astype(BF16)

    o_ref[...] = _dot(h_ref[...], w_ref[...]).astype(o_ref.dtype)


def _norm_matmul(x, g, w, out_dtype, tm, tn):
    s, d = x.shape
    n = w.shape[1]
    return pl.pallas_call(
        _norm_matmul_kernel,
        grid=(s // tm, n // tn),
        in_specs=[
            pl.BlockSpec((tm, d), lambda i, j: (i, 0)),
            pl.BlockSpec((1, d), lambda i, j: (0, 0)),
            pl.BlockSpec((d, tn), lambda i, j: (0, j)),
        ],
        out_specs=pl.BlockSpec((tm, tn), lambda i, j: (i, j)),
        out_shape=jax.ShapeDtypeStruct((s, n), out_dtype),
        scratch_shapes=[pltpu.VMEM((tm, d), BF16)],
        compiler_params=_params("parallel", "arbitrary"),
        name="norm_matmul",
    )(x, g.reshape(1, d), w)


def _compress_kernel(r_ref, w1_ref, pe_ref, b1_ref, w2_ref, o_ref):
    half = r_ref.shape[-1]
    r = r_ref[0, 0]
    bias = _dot(pe_ref[0].astype(BF16), w1_ref[0]) + b1_ref[0]
    top = _dot(r, w1_ref[0, :half, :])
    bot = _dot(r, w1_ref[0, half:, :])
    ncp = r.shape[0]
    hid = top + pltpu.roll(bot, ncp - 1, 0) + bias
    o_ref[0, 0] = _dot(jax.nn.gelu(hid).astype(BF16), w2_ref[0]).astype(o_ref.dtype)


def _compress(r, w1, pe, b1, w2):
    _, g, ncp, half = r.shape
    hidden = w1.shape[-1]
    dh = w2.shape[-1]
    return pl.pallas_call(
        _compress_kernel,
        grid=(2, g),
        in_specs=[
            pl.BlockSpec((1, 1, ncp, half), lambda s, gg: (s, gg, 0, 0)),
            pl.BlockSpec((1, 2 * half, hidden), lambda s, gg: (s, 0, 0)),
            pl.BlockSpec((1, 1, 2 * half), lambda s, gg: (s, 0, 0)),
            pl.BlockSpec((1, 1, hidden), lambda s, gg: (s, 0, 0)),
            pl.BlockSpec((1, hidden, dh), lambda s, gg: (s, 0, 0)),
        ],
        out_specs=pl.BlockSpec((1, 1, ncp, dh), lambda s, gg: (s, gg, 0, 0)),
        out_shape=jax.ShapeDtypeStruct((2, g, ncp, dh), BF16),
        compiler_params=_params("parallel", "parallel"),
        name="compress",
    )(r, w1, pe, b1, w2)


def _softmax_cols(s, mask):
    s = jnp.where(mask, s, MASK_VALUE)
    m = jnp.max(s, axis=0, keepdims=True)
    e = jnp.where(mask, jnp.exp(s - m), 0.0)
    l = jnp.sum(e, axis=0, keepdims=True)
    return e * jnp.where(l > 0.0, 1.0 / l, 0.0)


def _nsa_kernel(q_ref, gt_ref, kc_ref, vct_ref, covt_ref, ks_ref, vst_ref, kw_ref, vwt_ref, o_ref, sel_ref):
    g = pl.program_id(0)
    i = pl.program_id(1)
    qb = QUERY_BLOCK
    start = i * qb
    nb = sel_ref.shape[0]
    ncp = kc_ref.shape[1]
    t_row = start + lax.broadcasted_iota(jnp.int32, (1, qb), 1)

    qt = (q_ref[...].astype(F32) * (NSA_DIM ** -0.5)).T
    q_t = jnp.concatenate([qt[h * NSA_DIM:(h + 1) * NSA_DIM] for h in range(NSA_HPG)], axis=1).astype(BF16)

    sc = _dot(kc_ref[0], q_t)
    cmask = lax.broadcasted_iota(jnp.int32, (ncp, qb), 0) * CMP_STRIDE + (CMP_BLOCK - 1) <= t_row
    p_heads = [_softmax_cols(sc[:, h * qb:(h + 1) * qb], cmask) for h in range(NSA_HPG)]
    psum = p_heads[0]
    for p in p_heads[1:]:
        psum = psum + p
    o_cmp = _dot(vct_ref[0], jnp.concatenate([p.astype(BF16) for p in p_heads], axis=1))
    p_hi = psum.astype(BF16)
    p_lo = (psum - p_hi.astype(F32)).astype(BF16)
    imp = _dot(covt_ref[...], p_hi) + _dot(covt_ref[...], p_lo)

    blk = lax.broadcasted_iota(jnp.int32, (nb, qb), 0)
    jt = t_row // SEL_BLOCK
    forced = (blk == 0) | (blk == jt) | (blk == jt - 1)
    score = jnp.where(forced, FORCED_SCORE, jnp.where(blk <= jt, imp, -1.0))
    blk_f = blk.astype(F32)

    def pick(_, carry):
        score, bias = carry
        mx = jnp.max(score, axis=0, keepdims=True)
        idx = jnp.min(jnp.where(score == mx, blk_f, float(nb)), axis=0, keepdims=True)
        hit = blk_f == idx
        return jnp.where(hit, -2.0, score), jnp.where(hit, 0.0, bias)

    _, bias = lax.fori_loop(0, min(SEL_TOPK, nb), pick, (score, jnp.full((nb, qb), MASK_VALUE, F32)))
    sel_ref[...] = bias

    tk = SEL_TILE
    bpt = tk // SEL_BLOCK
    hq = NSA_HPG * qb

    def sel_tile(kt, carry, causal):
        m, l, acc = carry
        k0 = pl.multiple_of(kt * tk, tk)
        s = _dot(ks_ref[0, pl.ds(k0, tk), :], q_t)
        rows = [jnp.broadcast_to(sel_ref[pl.ds(kt * bpt + b, 1), :], (SEL_BLOCK, qb)) for b in range(bpt)]
        tile_bias = jnp.concatenate(rows, axis=0)
        if causal:
            kpos = k0 + lax.broadcasted_iota(jnp.int32, (tk, qb), 0)
            tile_bias = jnp.where(kpos <= t_row, tile_bias, MASK_VALUE)
        s = s + jnp.concatenate([tile_bias] * NSA_HPG, axis=1)
        m_new = jnp.maximum(m, jnp.max(s, axis=0, keepdims=True))
        alpha = jnp.exp(m - m_new)
        p = jnp.exp(s - m_new)
        l = alpha * l + jnp.sum(p, axis=0, keepdims=True)
        acc = alpha * acc + _dot(vst_ref[0, :, pl.ds(k0, tk)], p.astype(BF16))
        return m_new, l, acc

    last = (start + qb - 1) // tk
    init = (jnp.full((1, hq), MASK_VALUE, F32), jnp.zeros((1, hq), F32), jnp.zeros((NSA_DIM, hq), F32))
    carry = lax.fori_loop(0, last, functools.partial(sel_tile, causal=False), init)
    _, l_sel, acc_sel = sel_tile(last, carry, True)
    o_sel = acc_sel * (1.0 / l_sel)

    wk = WINDOW + qb
    k0w = pl.multiple_of(jnp.maximum(start - WINDOW, 0), qb)
    sw = _dot(kw_ref[0, pl.ds(k0w, wk), :], q_t)
    kpos = k0w + lax.broadcasted_iota(jnp.int32, (wk, qb), 0)
    wmask = (kpos <= t_row) & (kpos > t_row - WINDOW)
    p_win = jnp.concatenate([_softmax_cols(sw[:, h * qb:(h + 1) * qb], wmask).astype(BF16) for h in range(NSA_HPG)],
                            axis=1)
    o_win = _dot(vwt_ref[0, :, pl.ds(k0w, wk)], p_win)

    outs = []
    for h in range(NSA_HPG):
        row = g * (NSA_HPG * 3) + h * 3
        gc = jax.nn.sigmoid(gt_ref[pl.ds(row, 1), :])
        gs = jax.nn.sigmoid(gt_ref[pl.ds(row + 1, 1), :])
        gw = jax.nn.sigmoid(gt_ref[pl.ds(row + 2, 1), :])
        sl = slice(h * qb, (h + 1) * qb)
        outs.append(gc * o_cmp[:, sl] + gs * o_sel[:, sl] + gw * o_win[:, sl])
    o_ref[...] = jnp.concatenate(outs, axis=0).T.astype(o_ref.dtype)


def _nsa(q, gates_t, kc, vc_t, cov_t, ks, vs_t, kw, vw_t):
    s = q.shape[0]
    gdim = NSA_HPG * NSA_DIM
    ncp = kc.shape[1]
    nb = s // SEL_BLOCK
    per_group = lambda shape: pl.BlockSpec((1,) + shape, lambda g, i: (g, 0, 0))
    return pl.pallas_call(
        _nsa_kernel,
        grid=(NSA_GROUPS, s // QUERY_BLOCK),
        in_specs=[
            pl.BlockSpec((QUERY_BLOCK, gdim), lambda g, i: (i, g)),
            pl.BlockSpec((NSA_HEADS * 3, QUERY_BLOCK), lambda g, i: (0, i)),
            per_group((ncp, NSA_DIM)),
            per_group((NSA_DIM, ncp)),
            pl.BlockSpec((nb, ncp), lambda g, i: (0, 0)),
            per_group((s, NSA_DIM)),
            per_group((NSA_DIM, s)),
            per_group((s, NSA_DIM)),
            per_group((NSA_DIM, s)),
        ],
        out_specs=pl.BlockSpec((QUERY_BLOCK, gdim), lambda g, i: (i, g)),
        out_shape=jax.ShapeDtypeStruct((s, NSA_HEADS * NSA_DIM), BF16),
        scratch_shapes=[pltpu.VMEM((nb, QUERY_BLOCK), F32)],
        compiler_params=_params("parallel", "arbitrary"),
        name="nsa",
    )(q, gates_t, kc, vc_t, cov_t, ks, vs_t, kw, vw_t)


def _coverage_t(nc, ncp, nb):
    cs = np.arange(nc)[None, :] * CMP_STRIDE
    ss = np.arange(nb)[:, None] * SEL_BLOCK
    cov = np.clip(np.minimum(cs + CMP_BLOCK, ss + SEL_BLOCK) - np.maximum(cs, ss), 0, None) / CMP_BLOCK
    return jnp.asarray(np.pad(cov, ((0, 0), (0, ncp - nc))), dtype=BF16)


def _sgu_kernel(u_ref, v_ref, lng_ref, lnb_ref, ws_ref, bs_ref, o_ref):
    c = SGU_CHUNK
    tm = u_ref.shape[0]
    v = jax.nn.gelu(v_ref[...])
    mu = jnp.mean(v, axis=-1, keepdims=True)
    var = jnp.mean(jnp.square(v - mu), axis=-1, keepdims=True)
    vn = ((v - mu) * lax.rsqrt(var + NORM_EPS) * lng_ref[...] + lnb_ref[...]).astype(BF16)
    u = jax.nn.gelu(u_ref[...])
    tri = lax.broadcasted_iota(jnp.int32, (c, c), 0) >= lax.broadcasted_iota(jnp.int32, (c, c), 1)
    for g in range(SGU_GROUPS):
        w = jnp.where(tri, ws_ref[g], 0.0).astype(BF16)
        cols = slice(g * c, (g + 1) * c)
        rhs = jnp.concatenate([vn[k * c:(k + 1) * c, cols] for k in range(tm // c)], axis=1)
        mixed = _dot(w, rhs)
        for k in range(tm // c):
            rows = slice(k * c, (k + 1) * c)
            o_ref[rows, cols] = (u[rows, cols] * (mixed[:, rows] + bs_ref[g])).astype(o_ref.dtype)


def _sgu(proj, lng, lnb, ws, bs, tm):
    s = proj.shape[0]
    w = lng.shape[0]
    c = SGU_CHUNK
    bs_b = jnp.broadcast_to(bs[:, :, None], (SGU_GROUPS, c, c))
    return pl.pallas_call(
        _sgu_kernel,
        grid=(s // tm,),
        in_specs=[
            pl.BlockSpec((tm, w), lambda i: (i, 0)),
            pl.BlockSpec((tm, w), lambda i: (i, 1)),
            pl.BlockSpec((1, w), lambda i: (0, 0)),
            pl.BlockSpec((1, w), lambda i: (0, 0)),
            pl.BlockSpec((SGU_GROUPS, c, c), lambda i: (0, 0, 0)),
            pl.BlockSpec((SGU_GROUPS, c, c), lambda i: (0, 0, 0)),
        ],
        out_specs=pl.BlockSpec((tm, w), lambda i: (i, 0)),
        out_shape=jax.ShapeDtypeStruct((s, w), BF16),
        compiler_params=_params("parallel"),
        name="sgu",
    )(proj, proj, lng.reshape(1, w), lnb.reshape(1, w), ws, bs_b)


def _merge_kernel(oa_ref, ob_ref, ga_ref, gb_ref, x_ref, pa_ref, pb_ref, wo_ref, o_ref):
    a = _dot(oa_ref[...], pa_ref[...])
    b = _dot(ob_ref[...], pb_ref[...])
    merged = jax.nn.sigmoid(ga_ref[...]) * a + jax.nn.sigmoid(gb_ref[...]) * b
    o_ref[...] = x_ref[...] + _dot(merged.astype(BF16), wo_ref[...])


def _merge(o_a, o_b, proj, x, p_a, p_b, w_o, tm, gate_block):
    s, d = x.shape
    wa = o_a.shape[1]
    wb = o_b.shape[1]
    resident = lambda shape: pl.BlockSpec(shape, lambda i: (0, 0), pipeline_mode=pl.Buffered(1))
    return pl.pallas_call(
        _merge_kernel,
        grid=(s // tm,),
        in_specs=[
            pl.BlockSpec((tm, wa), lambda i: (i, 0)),
            pl.BlockSpec((tm, wb), lambda i: (i, 0)),
            pl.BlockSpec((tm, d), lambda i: (i, gate_block)),
            pl.BlockSpec((tm, d), lambda i: (i, gate_block + 1)),
            pl.BlockSpec((tm, d), lambda i: (i, 0)),
            resident((wa, d)),
            resident((wb, d)),
            resident((d, d)),
        ],
        out_specs=pl.BlockSpec((tm, d), lambda i: (i, 0)),
        out_shape=jax.ShapeDtypeStruct((s, d), F32),
        compiler_params=_params("parallel"),
        name="merge",
    )(o_a, o_b, proj, proj, x, p_a, p_b, w_o)


def _cross_kernel(x_ref, g_ref, wq_ref, mkt_ref, mv_ref, wo_ref, o_ref):
    x = x_ref[...]
    h = _rms(x, g_ref[...]).astype(BF16)
    mq = (_dot(h, wq_ref[...]) * (MEM_DIM ** -0.5)).astype(BF16)
    outs = []
    for hh in range(MEM_HEADS):
        s = _dot(mq[:, hh * MEM_DIM:(hh + 1) * MEM_DIM], mkt_ref[hh])
        e = jnp.exp(s - jnp.max(s, axis=-1, keepdims=True))
        p = e / jnp.sum(e, axis=-1, keepdims=True)
        outs.append(_dot(p.astype(BF16), mv_ref[hh]).astype(BF16))
    o_ref[...] = x + _dot(jnp.concatenate(outs, axis=1), wo_ref[...])


def _cross(x, g, w_q, mk_t, mv, w_o, tm):
    s, d = x.shape
    mw = w_q.shape[1]
    m = mv.shape[1]
    return pl.pallas_call(
        _cross_kernel,
        grid=(s // tm,),
        in_specs=[
            pl.BlockSpec((tm, d), lambda i: (i, 0)),
            pl.BlockSpec((1, d), lambda i: (0, 0)),
            pl.BlockSpec((d, mw), lambda i: (0, 0)),
            pl.BlockSpec((MEM_HEADS, MEM_DIM, m), lambda i: (0, 0, 0)),
            pl.BlockSpec((MEM_HEADS, m, MEM_DIM), lambda i: (0, 0, 0)),
            pl.BlockSpec((mw, d), lambda i: (0, 0)),
        ],
        out_specs=pl.BlockSpec((tm, d), lambda i: (i, 0)),
        out_shape=jax.ShapeDtypeStruct((s, d), F32),
        compiler_params=_params("parallel"),
        name="cross",
    )(x, g.reshape(1, d), w_q, mk_t, mv, w_o)


def _ffn_kernel(x_ref, g_ref, wg_ref, wu_ref, wo_ref, gf_ref, o_ref, h_ref, acc_ref, *, final_norm):
    j = pl.program_id(1)

    @pl.when(j == 0)
    def _():
        h_ref[...] = _rms(x_ref[...], g_ref[...]).astype(BF16)
        acc_ref[...] = jnp.zeros_like(acc_ref)

    h = h_ref[...]
    act = (jax.nn.silu(_dot(h, wg_ref[...])) * _dot(h, wu_ref[...])).astype(BF16)
    acc_ref[...] += _dot(act, wo_ref[...])

    @pl.when(j == pl.num_programs(1) - 1)
    def _():
        y = x_ref[...] + acc_ref[...]
        o_ref[...] = _rms(y, gf_ref[...]) if final_norm else y


def _ffn(x, g, w_in, w_out, g_final, final_norm, tm, th):
    s, d = x.shape
    hidden = w_out.shape[0]
    nh = hidden // th
    return pl.pallas_call(
        functools.partial(_ffn_kernel, final_norm=final_norm),
        grid=(s // tm, nh),
        in_specs=[
            pl.BlockSpec((tm, d), lambda i, j: (i, 0)),
            pl.BlockSpec((1, d), lambda i, j: (0, 0)),
            pl.BlockSpec((d, th), lambda i, j: (0, j)),
            pl.BlockSpec((d, th), lambda i, j: (0, j + nh)),
            pl.BlockSpec((th, d), lambda i, j: (j, 0)),
            pl.BlockSpec((1, d), lambda i, j: (0, 0)),
        ],
        out_specs=pl.BlockSpec((tm, d), lambda i, j: (i, 0)),
        out_shape=jax.ShapeDtypeStruct((s, d), F32),
        scratch_shapes=[pltpu.VMEM((tm, d), BF16), pltpu.VMEM((tm, d), F32)],
        compiler_params=_params("parallel", "arbitrary"),
        name="ffn",
    )(x, g.reshape(1, d), w_in, w_in, w_out, g_final.reshape(1, d))


def _layer(x, mem, norm_mix_g, w_in, cmp_pe_k, cmp_k_w1, cmp_k_b1, cmp_k_w2, cmp_pe_v, cmp_v_w1, cmp_v_b1, cmp_v_w2,
           sgu_ln_g, sgu_ln_b, sgu_ws, sgu_b, w_proj_a, w_proj_b, w_mix_out, norm_cross_g, norm_mem_g,
           w_mq, w_mkv, w_mo, norm_ffn_g):
    s, d = x.shape
    qw = NSA_HEADS * NSA_DIM
    kvw = NSA_GROUPS * NSA_DIM
    sguw = sgu_ln_g.shape[0]
    ngate = NSA_HEADS * 3
    o_kv = qw
    o_gate = o_kv + N_KV_STREAMS * kvw
    o_u = o_gate + ngate
    o_merge = o_u + 2 * sguw

    w_a = w_in[:, :o_gate].astype(BF16)
    gate_pad = 512
    w_b = jnp.concatenate(
        [w_in[:, o_u:], w_in[:, o_gate:o_u], jnp.zeros((d, gate_pad - ngate), F32)], axis=1).astype(BF16)
    tm_proj = min(1024, s)
    proj_a = _norm_matmul(x, norm_mix_g, w_a, BF16, tm_proj, 512)
    proj_b = _norm_matmul(x, norm_mix_g, w_b, F32, tm_proj, 512)

    q = proj_a[:, :qw]
    kv = proj_a[:, o_kv:].reshape(s, N_KV_STREAMS, NSA_GROUPS, NSA_DIM)
    gates_t = proj_b[:, 2 * sguw + 2 * d:2 * sguw + 2 * d + ngate].T

    ncp = s // CMP_STRIDE
    nc = (s - CMP_BLOCK) // CMP_STRIDE + 1
    r = kv[:, 0:2].reshape(ncp, CMP_STRIDE, 2, NSA_GROUPS, NSA_DIM).transpose(2, 3, 0, 1, 4)
    r = r.reshape(2, NSA_GROUPS, ncp, CMP_STRIDE * NSA_DIM)
    w1 = jnp.stack([cmp_k_w1, cmp_v_w1]).astype(BF16)
    pe = jnp.stack([cmp_pe_k.reshape(1, -1), cmp_pe_v.reshape(1, -1)])
    b1 = jnp.stack([cmp_k_b1.reshape(1, -1), cmp_v_b1.reshape(1, -1)])
    w2 = jnp.stack([cmp_k_w2, cmp_v_w2]).astype(BF16)
    cmp = _compress(r, w1, pe, b1, w2)
    kc = cmp[0]
    vc_t = cmp[1].transpose(0, 2, 1)

    ks = kv[:, 2].transpose(1, 0, 2)
    vs_t = kv[:, 3].transpose(1, 2, 0)
    kw = kv[:, 4].transpose(1, 0, 2)
    vw_t = kv[:, 5].transpose(1, 2, 0)
    cov_t = _coverage_t(nc, ncp, s // SEL_BLOCK)
    o_a = _nsa(q, gates_t, kc, vc_t, cov_t, ks, vs_t, kw, vw_t)

    o_b = _sgu(proj_b, sgu_ln_g, sgu_ln_b, sgu_ws, sgu_b, min(512, s))
    x = _merge(o_a, o_b, proj_b, x, w_proj_a.astype(BF16), w_proj_b.astype(BF16), w_mix_out.astype(BF16),
               min(256, s), (2 * sguw) // d)

    m = mem.shape[0]
    mw = MEM_HEADS * MEM_DIM
    mkv = _norm_matmul(mem, norm_mem_g, w_mkv.astype(BF16), F32, m, mw)
    mk_t = mkv[:, :mw].reshape(m, MEM_HEADS, MEM_DIM).transpose(1, 2, 0).astype(BF16)
    mv = mkv[:, mw:].reshape(m, MEM_HEADS, MEM_DIM).transpose(1, 0, 2).astype(BF16)
    x = _cross(x, norm_cross_g, w_mq.astype(BF16), mk_t, mv, w_mo.astype(BF16), min(512, s))
    return x


def kernel(x, mem, norm_mix_g, w_in, cmp_pe_k, cmp_k_w1, cmp_k_b1, cmp_k_w2, cmp_pe_v, cmp_v_w1, cmp_v_b1, cmp_v_w2, sgu_ln_g, sgu_ln_b, sgu_ws, sgu_b, w_proj_a, w_proj_b, w_mix_out, norm_cross_g, norm_mem_g, w_mq, w_mkv, w_mo, norm_ffn_g, w_ffn_in, w_ffn_out, norm_final_g):
    b, s, d = x.shape
    depth = w_in.shape[0]
    outs = []
    for bi in range(b):
        xb = x[bi]
        for l in range(depth):
            last = l == depth - 1
            xb = _layer(xb, mem[bi], norm_mix_g[l], w_in[l], cmp_pe_k[l], cmp_k_w1[l], cmp_k_b1[l], cmp_k_w2[l],
                        cmp_pe_v[l], cmp_v_w1[l], cmp_v_b1[l], cmp_v_w2[l], sgu_ln_g[l], sgu_ln_b[l], sgu_ws[l],
                        sgu_b[l], w_proj_a[l], w_proj_b[l], w_mix_out[l], norm_cross_g[l], norm_mem_g[l],
                        w_mq[l], w_mkv[l], w_mo[l], norm_ffn_g[l])
            xb = _ffn(xb, norm_ffn_g[l], w_ffn_in[l].astype(BF16), w_ffn_out[l].astype(BF16), norm_final_g, last,
                      min(512, s), 512)
        outs.append(xb)
    return jnp.stack(outs)
```

```python
import functools
import math

import numpy as np
import jax
import jax.numpy as jnp
from jax import lax
from jax.experimental import pallas as pl
from jax.experimental.pallas import tpu as pltpu

F32 = jnp.float32
BF16 = jnp.bfloat16

NORM_EPS = 1e-6
MASK_VALUE = -1e30
FORCED_SCORE = 1e6
LOG2E = math.log2(math.e)

NSA_HEADS = 16
NSA_GROUPS = 4
NSA_HPG = NSA_HEADS // NSA_GROUPS
NSA_DIM = 64
N_KV_STREAMS = 6
CMP_BLOCK = 32
CMP_STRIDE = 16
SEL_BLOCK = 64
SEL_TOPK = 16
WINDOW = 512
QUERY_BLOCK = 128
SGU_GROUPS = 8
SGU_CHUNK = 128
MEM_HEADS = 4
MEM_DIM = 128

LANES = 128
SEL_TILE = 512
SEL_BPT = SEL_TILE // SEL_BLOCK
BF16_ROWS = 16
VMEM_LIMIT = 56 * 1024 * 1024


def _params(*sem):
    return pltpu.CompilerParams(dimension_semantics=sem, vmem_limit_bytes=VMEM_LIMIT)


def _rms(x, g):
    return x * lax.rsqrt(jnp.mean(x * x, axis=-1, keepdims=True) + NORM_EPS) * g


def _dot(a, b):
    return jnp.dot(a, b, preferred_element_type=F32)


def _norm_matmul_kernel(x_ref, g_ref, w_ref, o_ref, h_ref):
    @pl.when(pl.program_id(1) == 0)
    def _():
        h_ref[...] = _rms(x_ref[...], g_ref[...]).astype(BF16)

    o_ref[...] = _dot(h_ref[...], w_ref[...]).astype(o_ref.dtype)


def _norm_matmul(x, g, w, out_dtype, tm, tn):
    s, d = x.shape
    n = w.shape[1]
    return pl.pallas_call(
        _norm_matmul_kernel,
        grid=(s // tm, n // tn),
        in_specs=[
            pl.BlockSpec((tm, d), lambda i, j: (i, 0)),
            pl.BlockSpec((1, d), lambda i, j: (0, 0)),
            pl.BlockSpec((d, tn), lambda i, j: (0, j)),
        ],
        out_specs=pl.BlockSpec((tm, tn), lambda i, j: (i, j)),
        out_shape=jax.ShapeDtypeStruct((s, n), out_dtype),
        scratch_shapes=[pltpu.VMEM((tm, d), BF16)],
        compiler_params=_params("parallel", "arbitrary"),
        name="norm_matmul",
    )(x, g.reshape(1, d), w)


def _compress_kernel(r_ref, w1_ref, pe_ref, b1_ref, w2_ref, o_ref):
    half = r_ref.shape[-1]
    r = r_ref[0, 0]
    bias = _dot(pe_ref[0].astype(BF16), w1_ref[0]) + b1_ref[0]
    top = _dot(r, w1_ref[0, :half, :])
    bot = _dot(r, w1_ref[0, half:, :])
    ncp = r.shape[0]
    hid = top + pltpu.roll(bot, ncp - 1, 0) + bias
    o_ref[0, 0] = _dot(jax.nn.gelu(hid).astype(BF16), w2_ref[0]).astype(o_ref.dtype)


def _compress(r, w1, pe, b1, w2):
    _, g, ncp, half = r.shape
    hidden = w1.shape[-1]
    dh = w2.shape[-1]
    return pl.pallas_call(
        _compress_kernel,
        grid=(2, g),
        in_specs=[
            pl.BlockSpec((1, 1, ncp, half), lambda s, gg: (s, gg, 0, 0)),
            pl.BlockSpec((1, 2 * half, hidden), lambda s, gg: (s, 0, 0)),
            pl.BlockSpec((1, 1, 2 * half), lambda s, gg: (s, 0, 0)),
            pl.BlockSpec((1, 1, hidden), lambda s, gg: (s, 0, 0)),
            pl.BlockSpec((1, hidden, dh), lambda s, gg: (s, 0, 0)),
        ],
        out_specs=pl.BlockSpec((1, 1, ncp, dh), lambda s, gg: (s, gg, 0, 0)),
        out_shape=jax.ShapeDtypeStruct((2, g, ncp, dh), BF16),
        compiler_params=_params("parallel", "parallel"),
        name="compress",
    )(r, w1, pe, b1, w2)


def _softmax2_cols(s, bias, valid):
    s = s + bias
    e = jnp.exp2(s - jnp.max(s, axis=0, keepdims=True))
    l = jnp.sum(e, axis=0, keepdims=True)
    return e * jnp.where(valid, 1.0 / l, 0.0)


def _nsa_kernel(q_ref, gt_ref, kc_ref, vct_ref, covt_ref, ks_ref, vst_ref, kw_ref, vwt_ref, o_ref,
                qa_ref, sel_ref, s_ref, acc_ref):
    g = pl.program_id(0)
    i = pl.program_id(1)
    qb = QUERY_BLOCK
    start = i * qb
    nb = sel_ref.shape[0]
    ncp = kc_ref.shape[1]
    hq = NSA_HPG * qb
    t_row = start + lax.broadcasted_iota(jnp.int32, (1, qb), 1)

    qt = (q_ref[...].astype(F32) * (NSA_DIM ** -0.5 * LOG2E)).T
    q_t = jnp.concatenate([qt[h * NSA_DIM:(h + 1) * NSA_DIM] for h in range(NSA_HPG)], axis=1).astype(BF16)
    qa_ref[:NSA_DIM, :] = q_t
    qa_ref[NSA_DIM:, :] = jnp.zeros((qa_ref.shape[0] - NSA_DIM, hq), BF16)

    sc = _dot(kc_ref[0], q_t)
    cmask = lax.broadcasted_iota(jnp.int32, (ncp, qb), 0) * CMP_STRIDE + (CMP_BLOCK - 1) <= t_row
    cbias = jnp.where(cmask, 0.0, MASK_VALUE)
    cvalid = t_row >= CMP_BLOCK - 1
    p_heads = [_softmax2_cols(sc[:, h * qb:(h + 1) * qb], cbias, cvalid) for h in range(NSA_HPG)]
    psum = p_heads[0]
    for p in p_heads[1:]:
        psum = psum + p
    o_cmp = _dot(vct_ref[0], jnp.concatenate([p.astype(BF16) for p in p_heads], axis=1))
    p_hi = psum.astype(BF16)
    p_lo = (psum - p_hi.astype(F32)).astype(BF16)
    imp = _dot(covt_ref[...], p_hi) + _dot(covt_ref[...], p_lo)

    blk = lax.broadcasted_iota(jnp.int32, (nb, qb), 0)
    jt = t_row // SEL_BLOCK
    forced = (blk == 0) | (blk == jt) | (blk == jt - 1)
    score = jnp.where(forced, FORCED_SCORE, jnp.where(blk <= jt, imp, -1.0))
    blk_f = blk.astype(F32)
    picked = -2.0

    def pick(_, score):
        mx = jnp.max(score, axis=0, keepdims=True)
        idx = jnp.min(jnp.where(score == mx, blk_f, float(nb)), axis=0, keepdims=True)
        return jnp.where(blk_f == idx, picked, score)

    score = lax.fori_loop(0, min(SEL_TOPK, nb), pick, score)
    sel_ref[...] = jnp.where((score == picked) & (blk <= jt), 0.0, MASK_VALUE)

    tk = SEL_TILE

    def scores(slot, kt):
        k0 = pl.multiple_of(kt * tk, tk)
        b8 = sel_ref[pl.ds(pl.multiple_of(kt * SEL_BPT, SEL_BPT), SEL_BPT), :]
        b16 = jnp.concatenate([b8, jnp.zeros((BF16_ROWS - SEL_BPT, qb), F32)], axis=0).astype(BF16)
        qa_ref[NSA_DIM:NSA_DIM + BF16_ROWS, :] = jnp.concatenate([b16] * NSA_HPG, axis=1)
        s = _dot(ks_ref[0, pl.ds(k0, tk), :], qa_ref[...])
        s_ref[slot] = s
        return jnp.max(s, axis=0, keepdims=True)

    def update(slot, kt, mt, m, causal):
        k0 = pl.multiple_of(kt * tk, tk)
        s = s_ref[slot]
        if causal:
            kpos = k0 + lax.broadcasted_iota(jnp.int32, (tk, qb), 0)
            s = s + jnp.concatenate([jnp.where(kpos <= t_row, 0.0, MASK_VALUE)] * NSA_HPG, axis=1)
            mt = jnp.max(s, axis=0, keepdims=True)
        m_new = jnp.maximum(m, mt)
        p = jnp.exp2(s - m_new).astype(BF16)
        acc_ref[...] = jnp.exp2(m - m_new) * acc_ref[...] + _dot(vst_ref[0, :, pl.ds(k0, tk)], p)
        return m_new

    def pair(pi, carry):
        mt_a, m = carry
        mt_b = scores(1, 2 * pi + 1)
        m = update(0, 2 * pi, mt_a, m, False)
        mt_a = scores(0, 2 * pi + 2)
        m = update(1, 2 * pi + 1, mt_b, m, False)
        return mt_a, m

    acc_ref[...] = jnp.zeros(acc_ref.shape, F32)
    last_pair = (start + qb - 1) // (2 * tk)
    mt_a, m = lax.fori_loop(0, last_pair, pair, (scores(0, 0), jnp.full((1, hq), MASK_VALUE, F32)))
    scores(1, 2 * last_pair + 1)
    m = update(0, 2 * last_pair, mt_a, m, True)
    update(1, 2 * last_pair + 1, None, m, True)
    o_sel = acc_ref[:NSA_DIM, :] * (1.0 / acc_ref[NSA_DIM:NSA_DIM + 1, :])

    wk = WINDOW + qb
    k0w = pl.multiple_of(jnp.maximum(start - WINDOW, 0), qb)
    sw = _dot(kw_ref[0, pl.ds(k0w, wk), :], q_t)
    kpos = k0w + lax.broadcasted_iota(jnp.int32, (wk, qb), 0)
    wbias = jnp.where((kpos <= t_row) & (kpos > t_row - WINDOW), 0.0, MASK_VALUE)
    p_win = jnp.concatenate(
        [_softmax2_cols(sw[:, h * qb:(h + 1) * qb], wbias, True).astype(BF16) for h in range(NSA_HPG)], axis=1)
    o_win = _dot(vwt_ref[0, :, pl.ds(k0w, wk)], p_win)

    outs = []
    for h in range(NSA_HPG):
        row = g * (NSA_HPG * 3) + h * 3
        gc = jax.nn.sigmoid(gt_ref[pl.ds(row, 1), :])
        gs = jax.nn.sigmoid(gt_ref[pl.ds(row + 1, 1), :])
        gw = jax.nn.sigmoid(gt_ref[pl.ds(row + 2, 1), :])
        sl = slice(h * qb, (h + 1) * qb)
        outs.append(gc * o_cmp[:, sl] + gs * o_sel[:, sl] + gw * o_win[:, sl])
    o_ref[...] = jnp.concatenate(outs, axis=0).T.astype(o_ref.dtype)


def _nsa(q, gates_t, kc, vc_t, cov_t, ks_aug, vs_aug_t, kw, vw_t):
    s = q.shape[0]
    assert s % (2 * SEL_TILE) == 0 and s >= WINDOW + QUERY_BLOCK
    gdim = NSA_HPG * NSA_DIM
    ncp = kc.shape[1]
    nb = s // SEL_BLOCK
    ka = ks_aug.shape[-1]
    va = vs_aug_t.shape[1]
    hq = NSA_HPG * QUERY_BLOCK
    per_group = lambda shape: pl.BlockSpec((1,) + shape, lambda g, i: (g, 0, 0))
    return pl.pallas_call(
        _nsa_kernel,
        grid=(NSA_GROUPS, s // QUERY_BLOCK),
        in_specs=[
            pl.BlockSpec((QUERY_BLOCK, gdim), lambda g, i: (i, g)),
            pl.BlockSpec((NSA_HEADS * 3, QUERY_BLOCK), lambda g, i: (0, i)),
            per_group((ncp, NSA_DIM)),
            per_group((NSA_DIM, ncp)),
            pl.BlockSpec((nb, ncp), lambda g, i: (0, 0)),
            per_group((s, ka)),
            per_group((va, s)),
            per_group((s, NSA_DIM)),
            per_group((NSA_DIM, s)),
        ],
        out_specs=pl.BlockSpec((QUERY_BLOCK, gdim), lambda g, i: (i, g)),
        out_shape=jax.ShapeDtypeStruct((s, NSA_HEADS * NSA_DIM), BF16),
        scratch_shapes=[pltpu.VMEM((ka, hq), BF16), pltpu.VMEM((nb, QUERY_BLOCK), F32),
                        pltpu.VMEM((2, SEL_TILE, hq), F32), pltpu.VMEM((va, hq), F32)],
        compiler_params=_params("parallel", "arbitrary"),
        name="nsa",
    )(q, gates_t, kc, vc_t, cov_t, ks_aug, vs_aug_t, kw, vw_t)


def _coverage_t(nc, ncp, nb):
    cs = np.arange(nc)[None, :] * CMP_STRIDE
    ss = np.arange(nb)[:, None] * SEL_BLOCK
    cov = np.clip(np.minimum(cs + CMP_BLOCK, ss + SEL_BLOCK) - np.maximum(cs, ss), 0, None) / CMP_BLOCK
    return jnp.asarray(np.pad(cov, ((0, 0), (0, ncp - nc))), dtype=BF16)


def _block_onehot(s):
    oh = np.zeros((s, NSA_DIM), np.float32)
    oh[np.arange(s), (np.arange(s) // SEL_BLOCK) % SEL_BPT] = 1.0
    return jnp.asarray(oh, dtype=BF16)


def _sgu_kernel(u_ref, v_ref, lng_ref, lnb_ref, ws_ref, bs_ref, o_ref):
    c = SGU_CHUNK
    tm = u_ref.shape[0]
    v = jax.nn.gelu(v_ref[...])
    mu = jnp.mean(v, axis=-1, keepdims=True)
    var = jnp.mean(jnp.square(v - mu), axis=-1, keepdims=True)
    vn = ((v - mu) * lax.rsqrt(var + NORM_EPS) * lng_ref[...] + lnb_ref[...]).astype(BF16)
    u = jax.nn.gelu(u_ref[...])
    tri = lax.broadcasted_iota(jnp.int32, (c, c), 0) >= lax.broadcasted_iota(jnp.int32, (c, c), 1)
    for g in range(SGU_GROUPS):
        w = jnp.where(tri, ws_ref[g], 0.0).astype(BF16)
        cols = slice(g * c, (g + 1) * c)
        rhs = jnp.concatenate([vn[k * c:(k + 1) * c, cols] for k in range(tm // c)], axis=1)
        mixed = _dot(w, rhs)
        for k in range(tm // c):
            rows = slice(k * c, (k + 1) * c)
            o_ref[rows, cols] = (u[rows, cols] * (mixed[:, rows] + bs_ref[g])).astype(o_ref.dtype)


def _sgu(proj, lng, lnb, ws, bs, tm):
    s = proj.shape[0]
    w = lng.shape[0]
    c = SGU_CHUNK
    bs_b = jnp.broadcast_to(bs[:, :, None], (SGU_GROUPS, c, c))
    return pl.pallas_call(
        _sgu_kernel,
        grid=(s // tm,),
        in_specs=[
            pl.BlockSpec((tm, w), lambda i: (i, 0)),
            pl.BlockSpec((tm, w), lambda i: (i, 1)),
            pl.BlockSpec((1, w), lambda i: (0, 0)),
            pl.BlockSpec((1, w), lambda i: (0, 0)),
            pl.BlockSpec((SGU_GROUPS, c, c), lambda i: (0, 0, 0)),
            pl.BlockSpec((SGU_GROUPS, c, c), lambda i: (0, 0, 0)),
        ],
        out_specs=pl.BlockSpec((tm, w), lambda i: (i, 0)),
        out_shape=jax.ShapeDtypeStruct((s, w), BF16),
        compiler_params=_params("parallel"),
        name="sgu",
    )(proj, proj, lng.reshape(1, w), lnb.reshape(1, w), ws, bs_b)


def _merge_kernel(oa_ref, ob_ref, ga_ref, gb_ref, x_ref, pa_ref, pb_ref, wo_ref, o_ref):
    a = _dot(oa_ref[...], pa_ref[...])
    b = _dot(ob_ref[...], pb_ref[...])
    merged = jax.nn.sigmoid(ga_ref[...]) * a + jax.nn.sigmoid(gb_ref[...]) * b
    o_ref[...] = x_ref[...] + _dot(merged.astype(BF16), wo_ref[...])


def _merge(o_a, o_b, proj, x, p_a, p_b, w_o, tm, gate_block):
    s, d = x.shape
    wa = o_a.shape[1]
    wb = o_b.shape[1]
    resident = lambda shape: pl.BlockSpec(shape, lambda i: (0, 0), pipeline_mode=pl.Buffered(1))
    return pl.pallas_call(
        _merge_kernel,
        grid=(s // tm,),
        in_specs=[
            pl.BlockSpec((tm, wa), lambda i: (i, 0)),
            pl.BlockSpec((tm, wb), lambda i: (i, 0)),
            pl.BlockSpec((tm, d), lambda i: (i, gate_block)),
            pl.BlockSpec((tm, d), lambda i: (i, gate_block + 1)),
            pl.BlockSpec((tm, d), lambda i: (i, 0)),
            resident((wa, d)),
            resident((wb, d)),
            resident((d, d)),
        ],
        out_specs=pl.BlockSpec((tm, d), lambda i: (i, 0)),
        out_shape=jax.ShapeDtypeStruct((s, d), F32),
        compiler_params=_params("parallel"),
        name="merge",
    )(o_a, o_b, proj, proj, x, p_a, p_b, w_o)


def _cross_kernel(x_ref, g_ref, wq_ref, mkt_ref, mv_ref, wo_ref, o_ref):
    x = x_ref[...]
    h = _rms(x, g_ref[...]).astype(BF16)
    mq = (_dot(h, wq_ref[...]) * (MEM_DIM ** -0.5)).astype(BF16)
    outs = []
    for hh in range(MEM_HEADS):
        s = _dot(mq[:, hh * MEM_DIM:(hh + 1) * MEM_DIM], mkt_ref[hh])
        e = jnp.exp(s - jnp.max(s, axis=-1, keepdims=True))
        p = e / jnp.sum(e, axis=-1, keepdims=True)
        outs.append(_dot(p.astype(BF16), mv_ref[hh]).astype(BF16))
    o_ref[...] = x + _dot(jnp.concatenate(outs, axis=1), wo_ref[...])


def _cross(x, g, w_q, mk_t, mv, w_o, tm):
    s, d = x.shape
    mw = w_q.shape[1]
    m = mv.shape[1]
    return pl.pallas_call(
        _cross_kernel,
        grid=(s // tm,),
        in_specs=[
            pl.BlockSpec((tm, d), lambda i: (i, 0)),
            pl.BlockSpec((1, d), lambda i: (0, 0)),
            pl.BlockSpec((d, mw), lambda i: (0, 0)),
            pl.BlockSpec((MEM_HEADS, MEM_DIM, m), lambda i: (0, 0, 0)),
            pl.BlockSpec((MEM_HEADS, m, MEM_DIM), lambda i: (0, 0, 0)),
            pl.BlockSpec((mw, d), lambda i: (0, 0)),
        ],
        out_specs=pl.BlockSpec((tm, d), lambda i: (i, 0)),
        out_shape=jax.ShapeDtypeStruct((s, d), F32),
        compiler_params=_params("parallel"),
        name="cross",
    )(x, g.reshape(1, d), w_q, mk_t, mv, w_o)


def _ffn_kernel(x_ref, g_ref, wg_ref, wu_ref, wo_ref, gf_ref, o_ref, h_ref, acc_ref, *, final_norm):
    j = pl.program_id(1)

    @pl.when(j == 0)
    def _():
        h_ref[...] = _rms(x_ref[...], g_ref[...]).astype(BF16)
        acc_ref[...] = jnp.zeros_like(acc_ref)

    h = h_ref[...]
    act = (jax.nn.silu(_dot(h, wg_ref[...])) * _dot(h, wu_ref[...])).astype(BF16)
    acc_ref[...] += _dot(act, wo_ref[...])

    @pl.when(j == pl.num_programs(1) - 1)
    def _():
        y = x_ref[...] + acc_ref[...]
        o_ref[...] = _rms(y, gf_ref[...]) if final_norm else y


def _ffn(x, g, w_in, w_out, g_final, final_norm, tm, th):
    s, d = x.shape
    hidden = w_out.shape[0]
    nh = hidden // th
    return pl.pallas_call(
        functools.partial(_ffn_kernel, final_norm=final_norm),
        grid=(s // tm, nh),
        in_specs=[
            pl.BlockSpec((tm, d), lambda i, j: (i, 0)),
            pl.BlockSpec((1, d), lambda i, j: (0, 0)),
            pl.BlockSpec((d, th), lambda i, j: (0, j)),
            pl.BlockSpec((d, th), lambda i, j: (0, j + nh)),
            pl.BlockSpec((th, d), lambda i, j: (j, 0)),
            pl.BlockSpec((1, d), lambda i, j: (0, 0)),
        ],
        out_specs=pl.BlockSpec((tm, d), lambda i, j: (i, 0)),
        out_shape=jax.ShapeDtypeStruct((s, d), F32),
        scratch_shapes=[pltpu.VMEM((tm, d), BF16), pltpu.VMEM((tm, d), F32)],
        compiler_params=_params("parallel", "arbitrary"),
        name="ffn",
    )(x, g.reshape(1, d), w_in, w_in, w_out, g_final.reshape(1, d))


def _layer(x, mem, norm_mix_g, w_in, cmp_pe_k, cmp_k_w1, cmp_k_b1, cmp_k_w2, cmp_pe_v, cmp_v_w1, cmp_v_b1, cmp_v_w2,
           sgu_ln_g, sgu_ln_b, sgu_ws, sgu_b, w_proj_a, w_proj_b, w_mix_out, norm_cross_g, norm_mem_g,
           w_mq, w_mkv, w_mo, norm_ffn_g):
    s, d = x.shape
    qw = NSA_HEADS * NSA_DIM
    kvw = NSA_GROUPS * NSA_DIM
    sguw = sgu_ln_g.shape[0]
    ngate = NSA_HEADS * 3
    o_kv = qw
    o_gate = o_kv + N_KV_STREAMS * kvw
    o_u = o_gate + ngate

    w_a = w_in[:, :o_gate].astype(BF16)
    gate_pad = 512
    w_b = jnp.concatenate(
        [w_in[:, o_u:], w_in[:, o_gate:o_u], jnp.zeros((d, gate_pad - ngate), F32)], axis=1).astype(BF16)
    tm_proj = min(1024, s)
    proj_a = _norm_matmul(x, norm_mix_g, w_a, BF16, tm_proj, 512)
    proj_b = _norm_matmul(x, norm_mix_g, w_b, F32, tm_proj, 512)

    q = proj_a[:, :qw]
    kv = proj_a[:, o_kv:].reshape(s, N_KV_STREAMS, NSA_GROUPS, NSA_DIM)
    gates_t = proj_b[:, 2 * sguw + 2 * d:2 * sguw + 2 * d + ngate].T

    ncp = s // CMP_STRIDE
    nc = (s - CMP_BLOCK) // CMP_STRIDE + 1
    r = kv[:, 0:2].reshape(ncp, CMP_STRIDE, 2, NSA_GROUPS, NSA_DIM).transpose(2, 3, 0, 1, 4)
    r = r.reshape(2, NSA_GROUPS, ncp, CMP_STRIDE * NSA_DIM)
    w1 = jnp.stack([cmp_k_w1, cmp_v_w1]).astype(BF16)
    pe = jnp.stack([cmp_pe_k.reshape(1, -1), cmp_pe_v.reshape(1, -1)])
    b1 = jnp.stack([cmp_k_b1.reshape(1, -1), cmp_v_b1.reshape(1, -1)])
    w2 = jnp.stack([cmp_k_w2, cmp_v_w2]).astype(BF16)
    cmp = _compress(r, w1, pe, b1, w2)
    kc = cmp[0]
    vc_t = cmp[1].transpose(0, 2, 1)

    ks = kv[:, 2].transpose(1, 0, 2)
    ks_aug = jnp.concatenate([ks, jnp.broadcast_to(_block_onehot(s), ks.shape)], axis=-1)
    vs_t = kv[:, 3].transpose(1, 2, 0)
    ones_row = jnp.zeros((NSA_GROUPS, BF16_ROWS, s), BF16).at[:, 0].set(1.0)
    vs_aug_t = jnp.concatenate([vs_t, ones_row], axis=1)
    kw = kv[:, 4].transpose(1, 0, 2)
    vw_t = kv[:, 5].transpose(1, 2, 0)
    cov_t = _coverage_t(nc, ncp, s // SEL_BLOCK)
    o_a = _nsa(q, gates_t, kc, vc_t, cov_t, ks_aug, vs_aug_t, kw, vw_t)

    o_b = _sgu(proj_b, sgu_ln_g, sgu_ln_b, sgu_ws, sgu_b, min(512, s))
    x = _merge(o_a, o_b, proj_b, x, w_proj_a.astype(BF16), w_proj_b.astype(BF16), w_mix_out.astype(BF16),
               min(256, s), (2 * sguw) // d)

    m = mem.shape[0]
    mw = MEM_HEADS * MEM_DIM
    mkv = _norm_matmul(mem, norm_mem_g, w_mkv.astype(BF16), F32, m, mw)
    mk_t = mkv[:, :mw].reshape(m, MEM_HEADS, MEM_DIM).transpose(1, 2, 0).astype(BF16)
    mv = mkv[:, mw:].reshape(m, MEM_HEADS, MEM_DIM).transpose(1, 0, 2).astype(BF16)
    x = _cross(x, norm_cross_g, w_mq.astype(BF16), mk_t, mv, w_mo.astype(BF16), min(512, s))
    return x


def kernel(x, mem, norm_mix_g, w_in, cmp_pe_k, cmp_k_w1, cmp_k_b1, cmp_k_w2, cmp_pe_v, cmp_v_w1, cmp_v_b1, cmp_v_w2, sgu_ln_g, sgu_ln_b, sgu_ws, sgu_b, w_proj_a, w_proj_b, w_mix_out, norm_cross_g, norm_mem_g, w_mq, w_mkv, w_mo, norm_ffn_g, w_ffn_in, w_ffn_out, norm_final_g):
    b, s, d = x.shape
    depth = w_in.shape[0]
    outs = []
    for bi in range(b):
        xb = x[bi]
        for l in range(depth):
            last = l == depth - 1
            xb = _layer(xb, mem[bi], norm_mix_g[l], w_in[l], cmp_pe_k[l], cmp_k_w1[l], cmp_k_b1[l], cmp_k_w2[l],
                        cmp_pe_v[l], cmp_v_w1[l], cmp_v_b1[l], cmp_v_w2[l], sgu_ln_g[l], sgu_ln_b[l], sgu_ws[l],
                        sgu_b[l], w_proj_a[l], w_proj_b[l], w_mix_out[l], norm_cross_g[l], norm_mem_g[l],
                        w_mq[l], w_mkv[l], w_mo[l], norm_ffn_g[l])
            xb = _ffn(xb, norm_ffn_g[l], w_ffn_in[l].astype(BF16), w_ffn_out[l].astype(BF16), norm_final_g, last,
                      min(512, s), 512)
        outs.append(xb)
    return jnp.stack(outs)
```

```python
import functools
import math

import numpy as np
import jax
import jax.numpy as jnp
from jax import lax
from jax.experimental import pallas as pl
from jax.experimental.pallas import tpu as pltpu

F32 = jnp.float32
BF16 = jnp.bfloat16

NORM_EPS = 1e-6
MASK_VALUE = -1e30
FORCED_SCORE = 1e6
LOG2E = math.log2(math.e)

NSA_HEADS = 16
NSA_GROUPS = 4
NSA_HPG = NSA_HEADS // NSA_GROUPS
NSA_DIM = 64
N_KV_STREAMS = 6
CMP_BLOCK = 32
CMP_STRIDE = 16
SEL_BLOCK = 64
SEL_TOPK = 16
WINDOW = 512
QUERY_BLOCK = 128
SGU_GROUPS = 8
SGU_CHUNK = 128
MEM_HEADS = 4
MEM_DIM = 128

LANES = 128
SEL_TILE = 512
SEL_BPT = SEL_TILE // SEL_BLOCK
NSA_BUCKETS = 4
BF16_ROWS = 16
VMEM_LIMIT = 56 * 1024 * 1024


def _params(*sem):
    return pltpu.CompilerParams(dimension_semantics=sem, vmem_limit_bytes=VMEM_LIMIT)


def _rms(x, g):
    return x * lax.rsqrt(jnp.mean(x * x, axis=-1, keepdims=True) + NORM_EPS) * g


def _dot(a, b):
    return jnp.dot(a, b, preferred_element_type=F32)


def _norm_matmul_kernel(x_ref, g_ref, w_ref, o_ref, h_ref):
    @pl.when(pl.program_id(1) == 0)
    def _():
        h_ref[...] = _rms(x_ref[...], g_ref[...]).astype(BF16)

    o_ref[...] = _dot(h_ref[...], w_ref[...]).astype(o_ref.dtype)


def _norm_matmul(x, g, w, out_dtype, tm, tn):
    s, d = x.shape
    n = w.shape[1]
    return pl.pallas_call(
        _norm_matmul_kernel,
        grid=(s // tm, n // tn),
        in_specs=[
            pl.BlockSpec((tm, d), lambda i, j: (i, 0)),
            pl.BlockSpec((1, d), lambda i, j: (0, 0)),
            pl.BlockSpec((d, tn), lambda i, j: (0, j)),
        ],
        out_specs=pl.BlockSpec((tm, tn), lambda i, j: (i, j)),
        out_shape=jax.ShapeDtypeStruct((s, n), out_dtype),
        scratch_shapes=[pltpu.VMEM((tm, d), BF16)],
        compiler_params=_params("parallel", "arbitrary"),
        name="norm_matmul",
    )(x, g.reshape(1, d), w)


def _compress_kernel(r_ref, w1_ref, pe_ref, b1_ref, w2_ref, o_ref):
    half = r_ref.shape[-1]
    r = r_ref[0, 0]
    bias = _dot(pe_ref[0].astype(BF16), w1_ref[0]) + b1_ref[0]
    top = _dot(r, w1_ref[0, :half, :])
    bot = _dot(r, w1_ref[0, half:, :])
    ncp = r.shape[0]
    hid = top + pltpu.roll(bot, ncp - 1, 0) + bias
    o_ref[0, 0] = _dot(jax.nn.gelu(hid).astype(BF16), w2_ref[0]).astype(o_ref.dtype)


def _compress(r, w1, pe, b1, w2):
    _, g, ncp, half = r.shape
    hidden = w1.shape[-1]
    dh = w2.shape[-1]
    return pl.pallas_call(
        _compress_kernel,
        grid=(2, g),
        in_specs=[
            pl.BlockSpec((1, 1, ncp, half), lambda s, gg: (s, gg, 0, 0)),
            pl.BlockSpec((1, 2 * half, hidden), lambda s, gg: (s, 0, 0)),
            pl.BlockSpec((1, 1, 2 * half), lambda s, gg: (s, 0, 0)),
            pl.BlockSpec((1, 1, hidden), lambda s, gg: (s, 0, 0)),
            pl.BlockSpec((1, hidden, dh), lambda s, gg: (s, 0, 0)),
        ],
        out_specs=pl.BlockSpec((1, 1, ncp, dh), lambda s, gg: (s, gg, 0, 0)),
        out_shape=jax.ShapeDtypeStruct((2, g, ncp, dh), BF16),
        compiler_params=_params("parallel", "parallel"),
        name="compress",
    )(r, w1, pe, b1, w2)


def _softmax2_cols(s, bias, valid):
    s = s + bias
    e = jnp.exp2(s - jnp.max(s, axis=0, keepdims=True))
    l = jnp.sum(e, axis=0, keepdims=True)
    return e * jnp.where(valid, 1.0 / l, 0.0)


def _nsa_kernel(q_ref, gt_ref, kc_ref, vct_ref, covt_ref, ks_ref, vst_ref, kw_ref, vwt_ref, o_ref,
                qa_ref, sel_ref, s_ref, acc_ref, ocmp_ref):
    g = pl.program_id(0)
    i = pl.program_id(1)
    qb = QUERY_BLOCK
    start = i * qb
    nb = sel_ref.shape[0]
    ncp = kc_ref.shape[1]
    s_len = ks_ref.shape[1]
    hq = NSA_HPG * qb
    t_row = start + lax.broadcasted_iota(jnp.int32, (1, qb), 1)

    qt = (q_ref[...].astype(F32) * (NSA_DIM ** -0.5 * LOG2E)).T
    q_t = jnp.concatenate([qt[h * NSA_DIM:(h + 1) * NSA_DIM] for h in range(NSA_HPG)], axis=1).astype(BF16)
    qa_ref[:NSA_DIM, :] = q_t
    qa_ref[NSA_DIM:, :] = jnp.zeros((qa_ref.shape[0] - NSA_DIM, hq), BF16)

    jt = t_row // SEL_BLOCK
    picked = -2.0

    def compressed_and_select(nck, nbk):
        sc = _dot(kc_ref[0, :nck, :], q_t)
        cmask = lax.broadcasted_iota(jnp.int32, (nck, qb), 0) * CMP_STRIDE + (CMP_BLOCK - 1) <= t_row
        cbias = jnp.where(cmask, 0.0, MASK_VALUE)
        cvalid = t_row >= CMP_BLOCK - 1
        p_heads = [_softmax2_cols(sc[:, h * qb:(h + 1) * qb], cbias, cvalid) for h in range(NSA_HPG)]
        psum = p_heads[0]
        for p in p_heads[1:]:
            psum = psum + p
        ocmp_ref[...] = _dot(vct_ref[0, :, :nck], jnp.concatenate([p.astype(BF16) for p in p_heads], axis=1))
        p_hi = psum.astype(BF16)
        p_lo = (psum - p_hi.astype(F32)).astype(BF16)
        cov = covt_ref[:nbk, :nck]
        imp = _dot(cov, p_hi) + _dot(cov, p_lo)

        blk = lax.broadcasted_iota(jnp.int32, (nbk, qb), 0)
        forced = (blk == 0) | (blk == jt) | (blk == jt - 1)
        score = jnp.where(forced, FORCED_SCORE, jnp.where(blk <= jt, imp, -1.0))
        blk_f = blk.astype(F32)

        def pick(_, score):
            mx = jnp.max(score, axis=0, keepdims=True)
            idx = jnp.min(jnp.where(score == mx, blk_f, float(nbk)), axis=0, keepdims=True)
            return jnp.where(blk_f == idx, picked, score)

        score = lax.fori_loop(0, min(SEL_TOPK, nbk), pick, score)
        sel_ref[:nbk, :] = jnp.where((score == picked) & (blk <= jt), 0.0, MASK_VALUE)

    bucket_ok = s_len % (NSA_BUCKETS * SEL_TILE) == 0 and nb // NSA_BUCKETS >= SEL_TOPK
    n_bucket = NSA_BUCKETS if bucket_ok else 1
    bucket = (start + qb - 1) // (s_len // n_bucket)
    for b in range(n_bucket):
        pl.when(bucket == b)(functools.partial(
            compressed_and_select, (b + 1) * (ncp // n_bucket), (b + 1) * (nb // n_bucket)))
    o_cmp = ocmp_ref[...]

    tk = SEL_TILE

    def scores(slot, kt):
        k0 = pl.multiple_of(kt * tk, tk)
        b8 = sel_ref[pl.ds(pl.multiple_of(kt * SEL_BPT, SEL_BPT), SEL_BPT), :]
        b16 = jnp.concatenate([b8, jnp.zeros((BF16_ROWS - SEL_BPT, qb), F32)], axis=0).astype(BF16)
        qa_ref[NSA_DIM:NSA_DIM + BF16_ROWS, :] = jnp.concatenate([b16] * NSA_HPG, axis=1)
        s = _dot(ks_ref[0, pl.ds(k0, tk), :], qa_ref[...])
        s_ref[slot] = s
        return jnp.max(s, axis=0, keepdims=True)

    def update(slot, kt, mt, m, causal):
        k0 = pl.multiple_of(kt * tk, tk)
        s = s_ref[slot]
        if causal:
            kpos = k0 + lax.broadcasted_iota(jnp.int32, (tk, qb), 0)
            s = s + jnp.concatenate([jnp.where(kpos <= t_row, 0.0, MASK_VALUE)] * NSA_HPG, axis=1)
            mt = jnp.max(s, axis=0, keepdims=True)
        m_new = jnp.maximum(m, mt)
        p = jnp.exp2(s - m_new).astype(BF16)
        acc_ref[...] = jnp.exp2(m - m_new) * acc_ref[...] + _dot(vst_ref[0, :, pl.ds(k0, tk)], p)
        return m_new

    def pair(pi, carry):
        mt_a, m = carry
        mt_b = scores(1, 2 * pi + 1)
        m = update(0, 2 * pi, mt_a, m, False)
        mt_a = scores(0, 2 * pi + 2)
        m = update(1, 2 * pi + 1, mt_b, m, False)
        return mt_a, m

    acc_ref[...] = jnp.zeros(acc_ref.shape, F32)
    last = (start + qb - 1) // tk
    mt_a, m = lax.fori_loop(0, last // 2, pair, (scores(0, 0), jnp.full((1, hq), MASK_VALUE, F32)))

    @pl.when(last % 2 == 0)
    def _():
        update(0, last, mt_a, m, True)

    @pl.when(last % 2 == 1)
    def _():
        mt_b = scores(1, last)
        update(1, last, mt_b, update(0, last - 1, mt_a, m, False), True)

    o_sel = acc_ref[:NSA_DIM, :] * (1.0 / acc_ref[NSA_DIM:NSA_DIM + 1, :])

    wk = WINDOW + qb
    k0w = pl.multiple_of(jnp.maximum(start - WINDOW, 0), qb)
    sw = _dot(kw_ref[0, pl.ds(k0w, wk), :], q_t)
    kpos = k0w + lax.broadcasted_iota(jnp.int32, (wk, qb), 0)
    wbias = jnp.where((kpos <= t_row) & (kpos > t_row - WINDOW), 0.0, MASK_VALUE)
    p_win = jnp.concatenate(
        [_softmax2_cols(sw[:, h * qb:(h + 1) * qb], wbias, True).astype(BF16) for h in range(NSA_HPG)], axis=1)
    o_win = _dot(vwt_ref[0, :, pl.ds(k0w, wk)], p_win)

    outs = []
    for h in range(NSA_HPG):
        row = g * (NSA_HPG * 3) + h * 3
        gc = jax.nn.sigmoid(gt_ref[pl.ds(row, 1), :])
        gs = jax.nn.sigmoid(gt_ref[pl.ds(row + 1, 1), :])
        gw = jax.nn.sigmoid(gt_ref[pl.ds(row + 2, 1), :])
        sl = slice(h * qb, (h + 1) * qb)
        outs.append(gc * o_cmp[:, sl] + gs * o_sel[:, sl] + gw * o_win[:, sl])
    o_ref[...] = jnp.concatenate(outs, axis=0).T.astype(o_ref.dtype)


def _nsa(q, gates_t, kc, vc_t, cov_t, ks_aug, vs_aug_t, kw, vw_t):
    s = q.shape[0]
    assert s % (2 * SEL_TILE) == 0 and s >= WINDOW + QUERY_BLOCK
    gdim = NSA_HPG * NSA_DIM
    ncp = kc.shape[1]
    nb = s // SEL_BLOCK
    ka = ks_aug.shape[-1]
    va = vs_aug_t.shape[1]
    hq = NSA_HPG * QUERY_BLOCK
    per_group = lambda shape: pl.BlockSpec((1,) + shape, lambda g, i: (g, 0, 0))
    return pl.pallas_call(
        _nsa_kernel,
        grid=(NSA_GROUPS, s // QUERY_BLOCK),
        in_specs=[
            pl.BlockSpec((QUERY_BLOCK, gdim), lambda g, i: (i, g)),
            pl.BlockSpec((NSA_HEADS * 3, QUERY_BLOCK), lambda g, i: (0, i)),
            per_group((ncp, NSA_DIM)),
            per_group((NSA_DIM, ncp)),
            pl.BlockSpec((nb, ncp), lambda g, i: (0, 0)),
            per_group((s, ka)),
            per_group((va, s)),
            per_group((s, NSA_DIM)),
            per_group((NSA_DIM, s)),
        ],
        out_specs=pl.BlockSpec((QUERY_BLOCK, gdim), lambda g, i: (i, g)),
        out_shape=jax.ShapeDtypeStruct((s, NSA_HEADS * NSA_DIM), BF16),
        scratch_shapes=[pltpu.VMEM((ka, hq), BF16), pltpu.VMEM((nb, QUERY_BLOCK), F32),
                        pltpu.VMEM((2, SEL_TILE, hq), F32), pltpu.VMEM((va, hq), F32),
                        pltpu.VMEM((NSA_DIM, hq), F32)],
        compiler_params=_params("parallel", "arbitrary"),
        name="nsa",
    )(q, gates_t, kc, vc_t, cov_t, ks_aug, vs_aug_t, kw, vw_t)


def _coverage_t(nc, ncp, nb):
    cs = np.arange(nc)[None, :] * CMP_STRIDE
    ss = np.arange(nb)[:, None] * SEL_BLOCK
    cov = np.clip(np.minimum(cs + CMP_BLOCK, ss + SEL_BLOCK) - np.maximum(cs, ss), 0, None) / CMP_BLOCK
    return jnp.asarray(np.pad(cov, ((0, 0), (0, ncp - nc))), dtype=BF16)


def _block_onehot(s):
    oh = np.zeros((s, NSA_DIM), np.float32)
    oh[np.arange(s), (np.arange(s) // SEL_BLOCK) % SEL_BPT] = 1.0
    return jnp.asarray(oh, dtype=BF16)


def _sgu_kernel(u_ref, v_ref, lng_ref, lnb_ref, ws_ref, bs_ref, o_ref):
    c = SGU_CHUNK
    tm = u_ref.shape[0]
    v = jax.nn.gelu(v_ref[...])
    mu = jnp.mean(v, axis=-1, keepdims=True)
    var = jnp.mean(jnp.square(v - mu), axis=-1, keepdims=True)
    vn = ((v - mu) * lax.rsqrt(var + NORM_EPS) * lng_ref[...] + lnb_ref[...]).astype(BF16)
    u = jax.nn.gelu(u_ref[...])
    tri = lax.broadcasted_iota(jnp.int32, (c, c), 0) >= lax.broadcasted_iota(jnp.int32, (c, c), 1)
    for g in range(SGU_GROUPS):
        w = jnp.where(tri, ws_ref[g], 0.0).astype(BF16)
        cols = slice(g * c, (g + 1) * c)
        rhs = jnp.concatenate([vn[k * c:(k + 1) * c, cols] for k in range(tm // c)], axis=1)
        mixed = _dot(w, rhs)
        for k in range(tm // c):
            rows = slice(k * c, (k + 1) * c)
            o_ref[rows, cols] = (u[rows, cols] * (mixed[:, rows] + bs_ref[g])).astype(o_ref.dtype)


def _sgu(proj, lng, lnb, ws, bs, tm):
    s = proj.shape[0]
    w = lng.shape[0]
    c = SGU_CHUNK
    bs_b = jnp.broadcast_to(bs[:, :, None], (SGU_GROUPS, c, c))
    return pl.pallas_call(
        _sgu_kernel,
        grid=(s // tm,),
        in_specs=[
            pl.BlockSpec((tm, w), lambda i: (i, 0)),
            pl.BlockSpec((tm, w), lambda i: (i, 1)),
            pl.BlockSpec((1, w), lambda i: (0, 0)),
            pl.BlockSpec((1, w), lambda i: (0, 0)),
            pl.BlockSpec((SGU_GROUPS, c, c), lambda i: (0, 0, 0)),
            pl.BlockSpec((SGU_GROUPS, c, c), lambda i: (0, 0, 0)),
        ],
        out_specs=pl.BlockSpec((tm, w), lambda i: (i, 0)),
        out_shape=jax.ShapeDtypeStruct((s, w), BF16),
        compiler_params=_params("parallel"),
        name="sgu",
    )(proj, proj, lng.reshape(1, w), lnb.reshape(1, w), ws, bs_b)


def _merge_kernel(oa_ref, ob_ref, ga_ref, gb_ref, x_ref, pa_ref, pb_ref, wo_ref, o_ref):
    a = _dot(oa_ref[...], pa_ref[...])
    b = _dot(ob_ref[...], pb_ref[...])
    merged = jax.nn.sigmoid(ga_ref[...]) * a + jax.nn.sigmoid(gb_ref[...]) * b
    o_ref[...] = x_ref[...] + _dot(merged.astype(BF16), wo_ref[...])


def _merge(o_a, o_b, proj, x, p_a, p_b, w_o, tm, gate_block):
    s, d = x.shape
    wa = o_a.shape[1]
    wb = o_b.shape[1]
    resident = lambda shape: pl.BlockSpec(shape, lambda i: (0, 0), pipeline_mode=pl.Buffered(1))
    return pl.pallas_call(
        _merge_kernel,
        grid=(s // tm,),
        in_specs=[
            pl.BlockSpec((tm, wa), lambda i: (i, 0)),
            pl.BlockSpec((tm, wb), lambda i: (i, 0)),
            pl.BlockSpec((tm, d), lambda i: (i, gate_block)),
            pl.BlockSpec((tm, d), lambda i: (i, gate_block + 1)),
            pl.BlockSpec((tm, d), lambda i: (i, 0)),
            resident((wa, d)),
            resident((wb, d)),
            resident((d, d)),
        ],
        out_specs=pl.BlockSpec((tm, d), lambda i: (i, 0)),
        out_shape=jax.ShapeDtypeStruct((s, d), F32),
        compiler_params=_params("parallel"),
        name="merge",
    )(o_a, o_b, proj, proj, x, p_a, p_b, w_o)


def _cross_kernel(x_ref, g_ref, wq_ref, mkt_ref, mv_ref, wo_ref, o_ref):
    x = x_ref[...]
    h = _rms(x, g_ref[...]).astype(BF16)
    mq = (_dot(h, wq_ref[...]) * (MEM_DIM ** -0.5)).astype(BF16)
    outs = []
    for hh in range(MEM_HEADS):
        s = _dot(mq[:, hh * MEM_DIM:(hh + 1) * MEM_DIM], mkt_ref[hh])
        e = jnp.exp(s - jnp.max(s, axis=-1, keepdims=True))
        p = e / jnp.sum(e, axis=-1, keepdims=True)
        outs.append(_dot(p.astype(BF16), mv_ref[hh]).astype(BF16))
    o_ref[...] = x + _dot(jnp.concatenate(outs, axis=1), wo_ref[...])


def _cross(x, g, w_q, mk_t, mv, w_o, tm):
    s, d = x.shape
    mw = w_q.shape[1]
    m = mv.shape[1]
    return pl.pallas_call(
        _cross_kernel,
        grid=(s // tm,),
        in_specs=[
            pl.BlockSpec((tm, d), lambda i: (i, 0)),
            pl.BlockSpec((1, d), lambda i: (0, 0)),
            pl.BlockSpec((d, mw), lambda i: (0, 0)),
            pl.BlockSpec((MEM_HEADS, MEM_DIM, m), lambda i: (0, 0, 0)),
            pl.BlockSpec((MEM_HEADS, m, MEM_DIM), lambda i: (0, 0, 0)),
            pl.BlockSpec((mw, d), lambda i: (0, 0)),
        ],
        out_specs=pl.BlockSpec((tm, d), lambda i: (i, 0)),
        out_shape=jax.ShapeDtypeStruct((s, d), F32),
        compiler_params=_params("parallel"),
        name="cross",
    )(x, g.reshape(1, d), w_q, mk_t, mv, w_o)


def _ffn_kernel(x_ref, g_ref, wg_ref, wu_ref, wo_ref, gf_ref, o_ref, h_ref, acc_ref, *, final_norm):
    j = pl.program_id(1)

    @pl.when(j == 0)
    def _():
        h_ref[...] = _rms(x_ref[...], g_ref[...]).astype(BF16)
        acc_ref[...] = jnp.zeros_like(acc_ref)

    h = h_ref[...]
    act = (jax.nn.silu(_dot(h, wg_ref[...])) * _dot(h, wu_ref[...])).astype(BF16)
    acc_ref[...] += _dot(act, wo_ref[...])

    @pl.when(j == pl.num_programs(1) - 1)
    def _():
        y = x_ref[...] + acc_ref[...]
        o_ref[...] = _rms(y, gf_ref[...]) if final_norm else y


def _ffn(x, g, w_in, w_out, g_final, final_norm, tm, th):
    s, d = x.shape
    hidden = w_out.shape[0]
    nh = hidden // th
    return pl.pallas_call(
        functools.partial(_ffn_kernel, final_norm=final_norm),
        grid=(s // tm, nh),
        in_specs=[
            pl.BlockSpec((tm, d), lambda i, j: (i, 0)),
            pl.BlockSpec((1, d), lambda i, j: (0, 0)),
            pl.BlockSpec((d, th), lambda i, j: (0, j)),
            pl.BlockSpec((d, th), lambda i, j: (0, j + nh)),
            pl.BlockSpec((th, d), lambda i, j: (j, 0)),
            pl.BlockSpec((1, d), lambda i, j: (0, 0)),
        ],
        out_specs=pl.BlockSpec((tm, d), lambda i, j: (i, 0)),
        out_shape=jax.ShapeDtypeStruct((s, d), F32),
        scratch_shapes=[pltpu.VMEM((tm, d), BF16), pltpu.VMEM((tm, d), F32)],
        compiler_params=_params("parallel", "arbitrary"),
        name="ffn",
    )(x, g.reshape(1, d), w_in, w_in, w_out, g_final.reshape(1, d))


def _layer(x, mem, norm_mix_g, w_in, cmp_pe_k, cmp_k_w1, cmp_k_b1, cmp_k_w2, cmp_pe_v, cmp_v_w1, cmp_v_b1, cmp_v_w2,
           sgu_ln_g, sgu_ln_b, sgu_ws, sgu_b, w_proj_a, w_proj_b, w_mix_out, norm_cross_g, norm_mem_g,
           w_mq, w_mkv, w_mo, norm_ffn_g):
    s, d = x.shape
    qw = NSA_HEADS * NSA_DIM
    kvw = NSA_GROUPS * NSA_DIM
    sguw = sgu_ln_g.shape[0]
    ngate = NSA_HEADS * 3
    o_kv = qw
    o_gate = o_kv + N_KV_STREAMS * kvw
    o_u = o_gate + ngate

    w_a = w_in[:, :o_gate].astype(BF16)
    gate_pad = 512
    w_b = jnp.concatenate(
        [w_in[:, o_u:], w_in[:, o_gate:o_u], jnp.zeros((d, gate_pad - ngate), F32)], axis=1).astype(BF16)
    tm_proj = min(1024, s)
    proj_a = _norm_matmul(x, norm_mix_g, w_a, BF16, tm_proj, 512)
    proj_b = _norm_matmul(x, norm_mix_g, w_b, F32, tm_proj, 512)

    q = proj_a[:, :qw]
    kv = proj_a[:, o_kv:].reshape(s, N_KV_STREAMS, NSA_GROUPS, NSA_DIM)
    gates_t = proj_b[:, 2 * sguw + 2 * d:2 * sguw + 2 * d + ngate].T

    ncp = s // CMP_STRIDE
    nc = (s - CMP_BLOCK) // CMP_STRIDE + 1
    r = kv[:, 0:2].reshape(ncp, CMP_STRIDE, 2, NSA_GROUPS, NSA_DIM).transpose(2, 3, 0, 1, 4)
    r = r.reshape(2, NSA_GROUPS, ncp, CMP_STRIDE * NSA_DIM)
    w1 = jnp.stack([cmp_k_w1, cmp_v_w1]).astype(BF16)
    pe = jnp.stack([cmp_pe_k.reshape(1, -1), cmp_pe_v.reshape(1, -1)])
    b1 = jnp.stack([cmp_k_b1.reshape(1, -1), cmp_v_b1.reshape(1, -1)])
    w2 = jnp.stack([cmp_k_w2, cmp_v_w2]).astype(BF16)
    cmp = _compress(r, w1, pe, b1, w2)
    kc = cmp[0]
    vc_t = cmp[1].transpose(0, 2, 1)

    ks = kv[:, 2].transpose(1, 0, 2)
    ks_aug = jnp.concatenate([ks, jnp.broadcast_to(_block_onehot(s), ks.shape)], axis=-1)
    vs_t = kv[:, 3].transpose(1, 2, 0)
    ones_row = jnp.zeros((NSA_GROUPS, BF16_ROWS, s), BF16).at[:, 0].set(1.0)
    vs_aug_t = jnp.concatenate([vs_t, ones_row], axis=1)
    kw = kv[:, 4].transpose(1, 0, 2)
    vw_t = kv[:, 5].transpose(1, 2, 0)
    cov_t = _coverage_t(nc, ncp, s // SEL_BLOCK)
    o_a = _nsa(q, gates_t, kc, vc_t, cov_t, ks_aug, vs_aug_t, kw, vw_t)

    o_b = _sgu(proj_b, sgu_ln_g, sgu_ln_b, sgu_ws, sgu_b, min(512, s))
    x = _merge(o_a, o_b, proj_b, x, w_proj_a.astype(BF16), w_proj_b.astype(BF16), w_mix_out.astype(BF16),
               min(256, s), (2 * sguw) // d)

    m = mem.shape[0]
    mw = MEM_HEADS * MEM_DIM
    mkv = _norm_matmul(mem, norm_mem_g, w_mkv.astype(BF16), F32, m, mw)
    mk_t = mkv[:, :mw].reshape(m, MEM_HEADS, MEM_DIM).transpose(1, 2, 0).astype(BF16)
    mv = mkv[:, mw:].reshape(m, MEM_HEADS, MEM_DIM).transpose(1, 0, 2).astype(BF16)
    x = _cross(x, norm_cross_g, w_mq.astype(BF16), mk_t, mv, w_mo.astype(BF16), min(512, s))
    return x


def kernel(x, mem, norm_mix_g, w_in, cmp_pe_k, cmp_k_w1, cmp_k_b1, cmp_k_w2, cmp_pe_v, cmp_v_w1, cmp_v_b1, cmp_v_w2, sgu_ln_g, sgu_ln_b, sgu_ws, sgu_b, w_proj_a, w_proj_b, w_mix_out, norm_cross_g, norm_mem_g, w_mq, w_mkv, w_mo, norm_ffn_g, w_ffn_in, w_ffn_out, norm_final_g):
    b, s, d = x.shape
    depth = w_in.shape[0]
    outs = []
    for bi in range(b):
        xb = x[bi]
        for l in range(depth):
            last = l == depth - 1
            xb = _layer(xb, mem[bi], norm_mix_g[l], w_in[l], cmp_pe_k[l], cmp_k_w1[l], cmp_k_b1[l], cmp_k_w2[l],
                        cmp_pe_v[l], cmp_v_w1[l], cmp_v_b1[l], cmp_v_w2[l], sgu_ln_g[l], sgu_ln_b[l], sgu_ws[l],
                        sgu_b[l], w_proj_a[l], w_proj_b[l], w_mix_out[l], norm_cross_g[l], norm_mem_g[l],
                        w_mq[l], w_mkv[l], w_mo[l], norm_ffn_g[l])
            xb = _ffn(xb, norm_ffn_g[l], w_ffn_in[l].astype(BF16), w_ffn_out[l].astype(BF16), norm_final_g, last,
                      min(512, s), 512)
        outs.append(xb)
    return jnp.stack(outs)
```

```python
import functools
import math

import numpy as np
import jax
import jax.numpy as jnp
from jax import lax
from jax.experimental import pallas as pl
from jax.experimental.pallas import tpu as pltpu

F32 = jnp.float32
BF16 = jnp.bfloat16

NORM_EPS = 1e-6
MASK_VALUE = -1e30
FORCED_SCORE = 1e6
LOG2E = math.log2(math.e)

NSA_HEADS = 16
NSA_GROUPS = 4
NSA_HPG = NSA_HEADS // NSA_GROUPS
NSA_DIM = 64
N_KV_STREAMS = 6
CMP_BLOCK = 32
CMP_STRIDE = 16
SEL_BLOCK = 64
SEL_TOPK = 16
WINDOW = 512
QUERY_BLOCK = 128
SGU_GROUPS = 8
SGU_CHUNK = 128
MEM_HEADS = 4
MEM_DIM = 128

LANES = 128
SEL_TILE = 512
SEL_BPT = SEL_TILE // SEL_BLOCK
NSA_BUCKETS = 4
BF16_ROWS = 16
VMEM_LIMIT = 56 * 1024 * 1024


def _params(*sem):
    return pltpu.CompilerParams(dimension_semantics=sem, vmem_limit_bytes=VMEM_LIMIT)


def _rms(x, g):
    return x * lax.rsqrt(jnp.mean(x * x, axis=-1, keepdims=True) + NORM_EPS) * g


def _dot(a, b):
    return jnp.dot(a, b, preferred_element_type=F32)


def _norm_matmul_kernel(x_ref, g_ref, w_ref, o_ref, h_ref):
    @pl.when(pl.program_id(1) == 0)
    def _():
        h_ref[...] = _rms(x_ref[...], g_ref[...]).astype(BF16)

    o_ref[...] = _dot(h_ref[...], w_ref[...]).astype(o_ref.dtype)


def _norm_matmul(x, g, w, out_dtype, tm, tn):
    s, d = x.shape
    n = w.shape[1]
    return pl.pallas_call(
        _norm_matmul_kernel,
        grid=(s // tm, n // tn),
        in_specs=[
            pl.BlockSpec((tm, d), lambda i, j: (i, 0)),
            pl.BlockSpec((1, d), lambda i, j: (0, 0)),
            pl.BlockSpec((d, tn), lambda i, j: (0, j)),
        ],
        out_specs=pl.BlockSpec((tm, tn), lambda i, j: (i, j)),
        out_shape=jax.ShapeDtypeStruct((s, n), out_dtype),
        scratch_shapes=[pltpu.VMEM((tm, d), BF16)],
        compiler_params=_params("parallel", "arbitrary"),
        name="norm_matmul",
    )(x, g.reshape(1, d), w)


PROJ_TN = 512


def _proj_attn_kernel(x_ref, g_ref, w_ref, nat_ref, kw_ref, ksa_ref, vsa_ref, vwt_ref, q_ref, h_ref):
    i = pl.program_id(0)
    j = pl.program_id(1)
    tm = x_ref.shape[0]
    dh = NSA_DIM
    ng = NSA_GROUPS

    @pl.when(j == 0)
    def _():
        h_ref[...] = _rms(x_ref[...], g_ref[...]).astype(BF16)

    res = _dot(h_ref[...], w_ref[...])

    def cols(c):
        return res[:, c * dh:(c + 1) * dh].astype(BF16)

    def cols_t(c):
        slab = res[:, (c // 2) * 2 * dh:(c // 2 + 1) * 2 * dh].T
        return slab[(c % 2) * dh:(c % 2 + 1) * dh].astype(BF16)

    @pl.when(j == 0)
    def _():
        for c in range(2 * ng):
            nat_ref[c // ng, c % ng] = cols(c)

    @pl.when(j == 1)
    def _():
        pos = i * tm + lax.broadcasted_iota(jnp.int32, (tm, dh), 0)
        lane = lax.broadcasted_iota(jnp.int32, (tm, dh), 1)
        onehot = jnp.where((pos // SEL_BLOCK) % SEL_BPT == lane, 1.0, 0.0).astype(BF16)
        for g in range(ng):
            kw_ref[g] = cols(g)
            ksa_ref[g, :, :dh] = cols(ng + g)
            ksa_ref[g, :, dh:] = onehot

    @pl.when(j == 2)
    def _():
        ones_row = jnp.where(lax.broadcasted_iota(jnp.int32, (BF16_ROWS, tm), 0) == 0, 1.0, 0.0).astype(BF16)
        for g in range(ng):
            vsa_ref[g, :dh, :] = cols_t(g)
            vsa_ref[g, dh:, :] = ones_row
            vwt_ref[g] = cols_t(ng + g)

    @pl.when(j >= 3)
    def _():
        q_ref[...] = res.astype(BF16)


def _proj_attn(x, g, w, tm):
    s, d = x.shape
    dh, ng, tn = NSA_DIM, NSA_GROUPS, PROJ_TN
    assert 2 * ng * dh == tn and w.shape[1] == 3 * tn + NSA_HEADS * dh
    nq = NSA_HEADS * dh // tn
    shapes = [
        jax.ShapeDtypeStruct((2, ng, s, dh), BF16),
        jax.ShapeDtypeStruct((ng, s, dh), BF16),
        jax.ShapeDtypeStruct((ng, s, 2 * dh), BF16),
        jax.ShapeDtypeStruct((ng, dh + BF16_ROWS, s), BF16),
        jax.ShapeDtypeStruct((ng, dh, s), BF16),
        jax.ShapeDtypeStruct((s, NSA_HEADS * dh), BF16),
    ]
    return pl.pallas_call(
        _proj_attn_kernel,
        grid=(s // tm, 3 + nq),
        in_specs=[
            pl.BlockSpec((tm, d), lambda i, j: (i, 0)),
            pl.BlockSpec((1, d), lambda i, j: (0, 0)),
            pl.BlockSpec((d, tn), lambda i, j: (0, j)),
        ],
        out_specs=[
            pl.BlockSpec((2, ng, tm, dh), lambda i, j: (0, 0, i, 0)),
            pl.BlockSpec((ng, tm, dh), lambda i, j: (0, i, 0)),
            pl.BlockSpec((ng, tm, 2 * dh), lambda i, j: (0, i, 0)),
            pl.BlockSpec((ng, dh + BF16_ROWS, tm), lambda i, j: (0, 0, i)),
            pl.BlockSpec((ng, dh, tm), lambda i, j: (0, 0, i)),
            pl.BlockSpec((tm, tn), lambda i, j: (i, jnp.maximum(j - 3, 0))),
        ],
        out_shape=shapes,
        scratch_shapes=[pltpu.VMEM((tm, d), BF16)],
        compiler_params=_params("parallel", "arbitrary"),
        name="proj_attn",
    )(x, g.reshape(1, d), w)


def _proj_gate_kernel(x_ref, g_ref, w_ref, o_ref, gt_ref, h_ref):
    j = pl.program_id(1)
    last = pl.num_programs(1) - 1

    @pl.when(j == 0)
    def _():
        h_ref[...] = _rms(x_ref[...], g_ref[...]).astype(BF16)

    res = _dot(h_ref[...], w_ref[...])

    @pl.when(j < last)
    def _():
        o_ref[...] = res

    @pl.when(j == last)
    def _():
        gt_ref[...] = res[:, :LANES].T


def _proj_gate(x, g, w, tm):
    s, d = x.shape
    tn = PROJ_TN
    n = w.shape[1] - tn
    return pl.pallas_call(
        _proj_gate_kernel,
        grid=(s // tm, n // tn + 1),
        in_specs=[
            pl.BlockSpec((tm, d), lambda i, j: (i, 0)),
            pl.BlockSpec((1, d), lambda i, j: (0, 0)),
            pl.BlockSpec((d, tn), lambda i, j: (0, j)),
        ],
        out_specs=[
            pl.BlockSpec((tm, tn), lambda i, j: (i, jnp.minimum(j, n // tn - 1))),
            pl.BlockSpec((LANES, tm), lambda i, j: (0, i)),
        ],
        out_shape=[jax.ShapeDtypeStruct((s, n), F32), jax.ShapeDtypeStruct((LANES, s), F32)],
        scratch_shapes=[pltpu.VMEM((tm, d), BF16)],
        compiler_params=_params("parallel", "arbitrary"),
        name="proj_gate",
    )(x, g.reshape(1, d), w)


def _compress_kernel(r_ref, w1_ref, pe_ref, b1_ref, w2_ref, o_ref, ot_ref):
    half = r_ref.shape[-1]
    r = r_ref[0, 0]
    bias = _dot(pe_ref[0].astype(BF16), w1_ref[0]) + b1_ref[0]
    top = _dot(r, w1_ref[0, :half, :])
    bot = _dot(r, w1_ref[0, half:, :])
    ncp = r.shape[0]
    hid = top + pltpu.roll(bot, ncp - 1, 0) + bias
    out = _dot(jax.nn.gelu(hid).astype(BF16), w2_ref[0])
    o_ref[0, 0] = out.astype(o_ref.dtype)
    dh = out.shape[1]
    out_t = jnp.concatenate([out, jnp.zeros((ncp, LANES - dh), F32)], axis=1).T
    ot_ref[0, 0] = out_t[:dh].astype(ot_ref.dtype)


def _compress(r, w1, pe, b1, w2):
    _, g, ncp, half = r.shape
    hidden = w1.shape[-1]
    dh = w2.shape[-1]
    return pl.pallas_call(
        _compress_kernel,
        grid=(2, g),
        in_specs=[
            pl.BlockSpec((1, 1, ncp, half), lambda s, gg: (s, gg, 0, 0)),
            pl.BlockSpec((1, 2 * half, hidden), lambda s, gg: (s, 0, 0)),
            pl.BlockSpec((1, 1, 2 * half), lambda s, gg: (s, 0, 0)),
            pl.BlockSpec((1, 1, hidden), lambda s, gg: (s, 0, 0)),
            pl.BlockSpec((1, hidden, dh), lambda s, gg: (s, 0, 0)),
        ],
        out_specs=[pl.BlockSpec((1, 1, ncp, dh), lambda s, gg: (s, gg, 0, 0)),
                   pl.BlockSpec((1, 1, dh, ncp), lambda s, gg: (s, gg, 0, 0))],
        out_shape=[jax.ShapeDtypeStruct((2, g, ncp, dh), BF16), jax.ShapeDtypeStruct((2, g, dh, ncp), BF16)],
        compiler_params=_params("parallel", "parallel"),
        name="compress",
    )(r, w1, pe, b1, w2)


def _softmax2_cols(s, bias, valid):
    s = s + bias
    e = jnp.exp2(s - jnp.max(s, axis=0, keepdims=True))
    l = jnp.sum(e, axis=0, keepdims=True)
    return e * jnp.where(valid, 1.0 / l, 0.0)


def _nsa_kernel(q_ref, gt_ref, kc_ref, vct_ref, covt_ref, ks_ref, vst_ref, kw_ref, vwt_ref, o_ref,
                qa_ref, sel_ref, s0_ref, s1_ref, acc_ref, ocmp_ref):
    s_refs = (s0_ref, s1_ref)
    g = pl.program_id(0)
    i = pl.program_id(1)
    qb = QUERY_BLOCK
    start = i * qb
    nb = sel_ref.shape[0]
    ncp = kc_ref.shape[1]
    s_len = ks_ref.shape[1]
    hq = NSA_HPG * qb
    t_row = start + lax.broadcasted_iota(jnp.int32, (1, qb), 1)

    qt = (q_ref[...].astype(F32) * (NSA_DIM ** -0.5 * LOG2E)).T
    q_t = jnp.concatenate([qt[h * NSA_DIM:(h + 1) * NSA_DIM] for h in range(NSA_HPG)], axis=1).astype(BF16)
    qa_ref[:NSA_DIM, :] = q_t
    qa_ref[NSA_DIM:, :] = jnp.zeros((qa_ref.shape[0] - NSA_DIM, hq), BF16)

    jt = t_row // SEL_BLOCK
    picked = -2.0

    def compressed_and_select(nck, nbk):
        sc = _dot(kc_ref[0, :nck, :], q_t)
        cmask = lax.broadcasted_iota(jnp.int32, (nck, qb), 0) * CMP_STRIDE + (CMP_BLOCK - 1) <= t_row
        cbias = jnp.where(cmask, 0.0, MASK_VALUE)
        cvalid = t_row >= CMP_BLOCK - 1
        p_heads = [_softmax2_cols(sc[:, h * qb:(h + 1) * qb], cbias, cvalid) for h in range(NSA_HPG)]
        psum = p_heads[0]
        for p in p_heads[1:]:
            psum = psum + p
        ocmp_ref[...] = _dot(vct_ref[0, :, :nck], jnp.concatenate([p.astype(BF16) for p in p_heads], axis=1))
        p_hi = psum.astype(BF16)
        p_lo = (psum - p_hi.astype(F32)).astype(BF16)
        cov = covt_ref[:nbk, :nck]
        imp = _dot(cov, p_hi) + _dot(cov, p_lo)

        blk = lax.broadcasted_iota(jnp.int32, (nbk, qb), 0)
        forced = (blk == 0) | (blk == jt) | (blk == jt - 1)
        score = jnp.where(forced, FORCED_SCORE, jnp.where(blk <= jt, imp, -1.0))
        blk_f = blk.astype(F32)

        def pick(_, score):
            mx = jnp.max(score, axis=0, keepdims=True)
            idx = jnp.min(jnp.where(score == mx, blk_f, float(nbk)), axis=0, keepdims=True)
            return jnp.where(blk_f == idx, picked, score)

        score = lax.fori_loop(0, min(SEL_TOPK, nbk), pick, score)
        sel_ref[:nbk, :] = jnp.where((score == picked) & (blk <= jt), 0.0, MASK_VALUE)

    bucket_ok = s_len % (NSA_BUCKETS * SEL_TILE) == 0 and nb // NSA_BUCKETS >= SEL_TOPK
    n_bucket = NSA_BUCKETS if bucket_ok else 1
    bucket = (start + qb - 1) // (s_len // n_bucket)
    for b in range(n_bucket):
        pl.when(bucket == b)(functools.partial(
            compressed_and_select, (b + 1) * (ncp // n_bucket), (b + 1) * (nb // n_bucket)))
    o_cmp = ocmp_ref[...]

    tk = SEL_TILE

    def scores(slot, kt):
        k0 = pl.multiple_of(kt * tk, tk)
        b8 = sel_ref[pl.ds(pl.multiple_of(kt * SEL_BPT, SEL_BPT), SEL_BPT), :]
        b16 = jnp.concatenate([b8, jnp.zeros((BF16_ROWS - SEL_BPT, qb), F32)], axis=0).astype(BF16)
        qa_ref[NSA_DIM:NSA_DIM + BF16_ROWS, :] = jnp.concatenate([b16] * NSA_HPG, axis=1)
        s = _dot(ks_ref[0, pl.ds(k0, tk), :], qa_ref[...])
        s_refs[slot][...] = s
        return jnp.max(s, axis=0, keepdims=True)

    def update(slot, kt, mt, m, causal):
        k0 = pl.multiple_of(kt * tk, tk)
        s = s_refs[slot][...]
        if causal:
            kpos = k0 + lax.broadcasted_iota(jnp.int32, (tk, qb), 0)
            s = s + jnp.concatenate([jnp.where(kpos <= t_row, 0.0, MASK_VALUE)] * NSA_HPG, axis=1)
            mt = jnp.max(s, axis=0, keepdims=True)
        m_new = jnp.maximum(m, mt)
        p = jnp.exp2(s - m_new).astype(BF16)
        acc_ref[...] = jnp.exp2(m - m_new) * acc_ref[...] + _dot(vst_ref[0, :, pl.ds(k0, tk)], p)
        return m_new

    def pair(pi, carry):
        mt_a, m = carry
        mt_b = scores(1, 2 * pi + 1)
        m = update(0, 2 * pi, mt_a, m, False)
        mt_a = scores(0, 2 * pi + 2)
        m = update(1, 2 * pi + 1, mt_b, m, False)
        return mt_a, m

    acc_ref[...] = jnp.zeros(acc_ref.shape, F32)
    last = (start + qb - 1) // tk
    mt_a, m = lax.fori_loop(0, last // 2, pair, (scores(0, 0), jnp.full((1, hq), MASK_VALUE, F32)))

    @pl.when(last % 2 == 0)
    def _():
        update(0, last, mt_a, m, True)

    @pl.when(last % 2 == 1)
    def _():
        mt_b = scores(1, last)
        update(1, last, mt_b, update(0, last - 1, mt_a, m, False), True)

    o_sel = acc_ref[:NSA_DIM, :] * (1.0 / acc_ref[NSA_DIM:NSA_DIM + 1, :])

    wk = WINDOW + qb
    k0w = pl.multiple_of(jnp.maximum(start - WINDOW, 0), qb)
    sw = _dot(kw_ref[0, pl.ds(k0w, wk), :], q_t)
    kpos = k0w + lax.broadcasted_iota(jnp.int32, (wk, qb), 0)
    wbias = jnp.where((kpos <= t_row) & (kpos > t_row - WINDOW), 0.0, MASK_VALUE)
    p_win = jnp.concatenate(
        [_softmax2_cols(sw[:, h * qb:(h + 1) * qb], wbias, True).astype(BF16) for h in range(NSA_HPG)], axis=1)
    o_win = _dot(vwt_ref[0, :, pl.ds(k0w, wk)], p_win)

    outs = []
    for h in range(NSA_HPG):
        row = g * (NSA_HPG * 3) + h * 3
        gc = jax.nn.sigmoid(gt_ref[pl.ds(row, 1), :])
        gs = jax.nn.sigmoid(gt_ref[pl.ds(row + 1, 1), :])
        gw = jax.nn.sigmoid(gt_ref[pl.ds(row + 2, 1), :])
        sl = slice(h * qb, (h + 1) * qb)
        outs.append(gc * o_cmp[:, sl] + gs * o_sel[:, sl] + gw * o_win[:, sl])
    o_ref[...] = jnp.concatenate(outs, axis=0).T.astype(o_ref.dtype)


def _nsa(q, gates_t, kc, vc_t, cov_t, ks_aug, vs_aug_t, kw, vw_t):
    s = q.shape[0]
    assert s % (2 * SEL_TILE) == 0 and s >= WINDOW + QUERY_BLOCK
    gdim = NSA_HPG * NSA_DIM
    ncp = kc.shape[1]
    nb = s // SEL_BLOCK
    ka = ks_aug.shape[-1]
    va = vs_aug_t.shape[1]
    hq = NSA_HPG * QUERY_BLOCK
    per_group = lambda shape: pl.BlockSpec((1,) + shape, lambda g, i: (g, 0, 0))
    return pl.pallas_call(
        _nsa_kernel,
        grid=(NSA_GROUPS, s // QUERY_BLOCK),
        in_specs=[
            pl.BlockSpec((QUERY_BLOCK, gdim), lambda g, i: (i, g)),
            pl.BlockSpec((NSA_HEADS * 3, QUERY_BLOCK), lambda g, i: (0, i)),
            per_group((ncp, NSA_DIM)),
            per_group((NSA_DIM, ncp)),
            pl.BlockSpec((nb, ncp), lambda g, i: (0, 0)),
            per_group((s, ka)),
            per_group((va, s)),
            per_group((s, NSA_DIM)),
            per_group((NSA_DIM, s)),
        ],
        out_specs=pl.BlockSpec((QUERY_BLOCK, gdim), lambda g, i: (i, g)),
        out_shape=jax.ShapeDtypeStruct((s, NSA_HEADS * NSA_DIM), BF16),
        scratch_shapes=[pltpu.VMEM((ka, hq), BF16), pltpu.VMEM((nb, QUERY_BLOCK), F32),
                        pltpu.VMEM((SEL_TILE, hq), F32), pltpu.VMEM((SEL_TILE, hq), F32),
                        pltpu.VMEM((va, hq), F32),
                        pltpu.VMEM((NSA_DIM, hq), F32)],
        compiler_params=_params("parallel", "arbitrary"),
        name="nsa",
    )(q, gates_t, kc, vc_t, cov_t, ks_aug, vs_aug_t, kw, vw_t)


def _coverage_t(nc, ncp, nb):
    cs = np.arange(nc)[None, :] * CMP_STRIDE
    ss = np.arange(nb)[:, None] * SEL_BLOCK
    cov = np.clip(np.minimum(cs + CMP_BLOCK, ss + SEL_BLOCK) - np.maximum(cs, ss), 0, None) / CMP_BLOCK
    return jnp.asarray(np.pad(cov, ((0, 0), (0, ncp - nc))), dtype=BF16)


def _sgu_kernel(u_ref, v_ref, lng_ref, lnb_ref, ws_ref, bs_ref, o_ref):
    c = SGU_CHUNK
    tm = u_ref.shape[0]
    v = jax.nn.gelu(v_ref[...])
    mu = jnp.mean(v, axis=-1, keepdims=True)
    var = jnp.mean(jnp.square(v - mu), axis=-1, keepdims=True)
    vn = ((v - mu) * lax.rsqrt(var + NORM_EPS) * lng_ref[...] + lnb_ref[...]).astype(BF16)
    u = jax.nn.gelu(u_ref[...])
    tri = lax.broadcasted_iota(jnp.int32, (c, c), 0) >= lax.broadcasted_iota(jnp.int32, (c, c), 1)
    for g in range(SGU_GROUPS):
        w = jnp.where(tri, ws_ref[g], 0.0).astype(BF16)
        cols = slice(g * c, (g + 1) * c)
        rhs = jnp.concatenate([vn[k * c:(k + 1) * c, cols] for k in range(tm // c)], axis=1)
        mixed = _dot(w, rhs)
        for k in range(tm // c):
            rows = slice(k * c, (k + 1) * c)
            o_ref[rows, cols] = (u[rows, cols] * (mixed[:, rows] + bs_ref[g])).astype(o_ref.dtype)


def _sgu(proj, lng, lnb, ws, bs, tm):
    s = proj.shape[0]
    w = lng.shape[0]
    c = SGU_CHUNK
    bs_b = jnp.broadcast_to(bs[:, :, None], (SGU_GROUPS, c, c))
    return pl.pallas_call(
        _sgu_kernel,
        grid=(s // tm,),
        in_specs=[
            pl.BlockSpec((tm, w), lambda i: (i, 0)),
            pl.BlockSpec((tm, w), lambda i: (i, 1)),
            pl.BlockSpec((1, w), lambda i: (0, 0)),
            pl.BlockSpec((1, w), lambda i: (0, 0)),
            pl.BlockSpec((SGU_GROUPS, c, c), lambda i: (0, 0, 0)),
            pl.BlockSpec((SGU_GROUPS, c, c), lambda i: (0, 0, 0)),
        ],
        out_specs=pl.BlockSpec((tm, w), lambda i: (i, 0)),
        out_shape=jax.ShapeDtypeStruct((s, w), BF16),
        compiler_params=_params("parallel"),
        name="sgu",
    )(proj, proj, lng.reshape(1, w), lnb.reshape(1, w), ws, bs_b)


def _merge_kernel(oa_ref, ob_ref, ga_ref, gb_ref, x_ref, pa_ref, pb_ref, wo_ref, o_ref):
    a = _dot(oa_ref[...], pa_ref[...])
    b = _dot(ob_ref[...], pb_ref[...])
    merged = jax.nn.sigmoid(ga_ref[...]) * a + jax.nn.sigmoid(gb_ref[...]) * b
    o_ref[...] = x_ref[...] + _dot(merged.astype(BF16), wo_ref[...])


def _merge(o_a, o_b, proj, x, p_a, p_b, w_o, tm, gate_block):
    s, d = x.shape
    wa = o_a.shape[1]
    wb = o_b.shape[1]
    resident = lambda shape: pl.BlockSpec(shape, lambda i: (0, 0), pipeline_mode=pl.Buffered(1))
    return pl.pallas_call(
        _merge_kernel,
        grid=(s // tm,),
        in_specs=[
            pl.BlockSpec((tm, wa), lambda i: (i, 0)),
            pl.BlockSpec((tm, wb), lambda i: (i, 0)),
            pl.BlockSpec((tm, d), lambda i: (i, gate_block)),
            pl.BlockSpec((tm, d), lambda i: (i, gate_block + 1)),
            pl.BlockSpec((tm, d), lambda i: (i, 0)),
            resident((wa, d)),
            resident((wb, d)),
            resident((d, d)),
        ],
        out_specs=pl.BlockSpec((tm, d), lambda i: (i, 0)),
        out_shape=jax.ShapeDtypeStruct((s, d), F32),
        compiler_params=_params("parallel"),
        name="merge",
    )(o_a, o_b, proj, proj, x, p_a, p_b, w_o)


def _cross_kernel(x_ref, g_ref, wq_ref, mkt_ref, mv_ref, wo_ref, o_ref):
    x = x_ref[...]
    h = _rms(x, g_ref[...]).astype(BF16)
    mq = (_dot(h, wq_ref[...]) * (MEM_DIM ** -0.5)).astype(BF16)
    outs = []
    for hh in range(MEM_HEADS):
        s = _dot(mq[:, hh * MEM_DIM:(hh + 1) * MEM_DIM], mkt_ref[hh])
        e = jnp.exp(s - jnp.max(s, axis=-1, keepdims=True))
        p = e / jnp.sum(e, axis=-1, keepdims=True)
        outs.append(_dot(p.astype(BF16), mv_ref[hh]).astype(BF16))
    o_ref[...] = x + _dot(jnp.concatenate(outs, axis=1), wo_ref[...])


def _cross(x, g, w_q, mk_t, mv, w_o, tm):
    s, d = x.shape
    mw = w_q.shape[1]
    m = mv.shape[1]
    return pl.pallas_call(
        _cross_kernel,
        grid=(s // tm,),
        in_specs=[
            pl.BlockSpec((tm, d), lambda i: (i, 0)),
            pl.BlockSpec((1, d), lambda i: (0, 0)),
            pl.BlockSpec((d, mw), lambda i: (0, 0)),
            pl.BlockSpec((MEM_HEADS, MEM_DIM, m), lambda i: (0, 0, 0)),
            pl.BlockSpec((MEM_HEADS, m, MEM_DIM), lambda i: (0, 0, 0)),
            pl.BlockSpec((mw, d), lambda i: (0, 0)),
        ],
        out_specs=pl.BlockSpec((tm, d), lambda i: (i, 0)),
        out_shape=jax.ShapeDtypeStruct((s, d), F32),
        compiler_params=_params("parallel"),
        name="cross",
    )(x, g.reshape(1, d), w_q, mk_t, mv, w_o)


def _ffn_kernel(x_ref, g_ref, wg_ref, wu_ref, wo_ref, gf_ref, o_ref, h_ref, acc_ref, *, final_norm):
    j = pl.program_id(1)

    @pl.when(j == 0)
    def _():
        h_ref[...] = _rms(x_ref[...], g_ref[...]).astype(BF16)
        acc_ref[...] = jnp.zeros_like(acc_ref)

    h = h_ref[...]
    act = (jax.nn.silu(_dot(h, wg_ref[...])) * _dot(h, wu_ref[...])).astype(BF16)
    acc_ref[...] += _dot(act, wo_ref[...])

    @pl.when(j == pl.num_programs(1) - 1)
    def _():
        y = x_ref[...] + acc_ref[...]
        o_ref[...] = _rms(y, gf_ref[...]) if final_norm else y


def _ffn(x, g, w_in, w_out, g_final, final_norm, tm, th):
    s, d = x.shape
    hidden = w_out.shape[0]
    nh = hidden // th
    return pl.pallas_call(
        functools.partial(_ffn_kernel, final_norm=final_norm),
        grid=(s // tm, nh),
        in_specs=[
            pl.BlockSpec((tm, d), lambda i, j: (i, 0)),
            pl.BlockSpec((1, d), lambda i, j: (0, 0)),
            pl.BlockSpec((d, th), lambda i, j: (0, j)),
            pl.BlockSpec((d, th), lambda i, j: (0, j + nh)),
            pl.BlockSpec((th, d), lambda i, j: (j, 0)),
            pl.BlockSpec((1, d), lambda i, j: (0, 0)),
        ],
        out_specs=pl.BlockSpec((tm, d), lambda i, j: (i, 0)),
        out_shape=jax.ShapeDtypeStruct((s, d), F32),
        scratch_shapes=[pltpu.VMEM((tm, d), BF16), pltpu.VMEM((tm, d), F32)],
        compiler_params=_params("parallel", "arbitrary"),
        name="ffn",
    )(x, g.reshape(1, d), w_in, w_in, w_out, g_final.reshape(1, d))


def _layer(x, mem, norm_mix_g, w_in, cmp_pe_k, cmp_k_w1, cmp_k_b1, cmp_k_w2, cmp_pe_v, cmp_v_w1, cmp_v_b1, cmp_v_w2,
           sgu_ln_g, sgu_ln_b, sgu_ws, sgu_b, w_proj_a, w_proj_b, w_mix_out, norm_cross_g, norm_mem_g,
           w_mq, w_mkv, w_mo, norm_ffn_g):
    s, d = x.shape
    qw = NSA_HEADS * NSA_DIM
    kvw = NSA_GROUPS * NSA_DIM
    sguw = sgu_ln_g.shape[0]
    ngate = NSA_HEADS * 3
    o_kv = qw
    o_gate = o_kv + N_KV_STREAMS * kvw
    o_u = o_gate + ngate

    stream = lambda st: w_in[:, o_kv + st * kvw:o_kv + (st + 1) * kvw]
    w_a = jnp.concatenate([stream(st) for st in (0, 1, 4, 2, 3, 5)] + [w_in[:, :qw]], axis=1).astype(BF16)
    w_b = jnp.concatenate(
        [w_in[:, o_u:], w_in[:, o_gate:o_u], jnp.zeros((d, PROJ_TN - ngate), F32)], axis=1).astype(BF16)
    tm_proj = min(1024, s)
    nat, kw, ks_aug, vs_aug_t, vw_t, q = _proj_attn(x, norm_mix_g, w_a, tm_proj)
    proj_b, gates_t = _proj_gate(x, norm_mix_g, w_b, tm_proj)

    ncp = s // CMP_STRIDE
    nc = (s - CMP_BLOCK) // CMP_STRIDE + 1
    r = nat.reshape(2, NSA_GROUPS, ncp, CMP_STRIDE * NSA_DIM)
    w1 = jnp.stack([cmp_k_w1, cmp_v_w1]).astype(BF16)
    pe = jnp.stack([cmp_pe_k.reshape(1, -1), cmp_pe_v.reshape(1, -1)])
    b1 = jnp.stack([cmp_k_b1.reshape(1, -1), cmp_v_b1.reshape(1, -1)])
    w2 = jnp.stack([cmp_k_w2, cmp_v_w2]).astype(BF16)
    cmp, cmp_t = _compress(r, w1, pe, b1, w2)
    cov_t = _coverage_t(nc, ncp, s // SEL_BLOCK)
    o_a = _nsa(q, gates_t, cmp[0], cmp_t[1], cov_t, ks_aug, vs_aug_t, kw, vw_t)

    o_b = _sgu(proj_b, sgu_ln_g, sgu_ln_b, sgu_ws, sgu_b, min(512, s))
    x = _merge(o_a, o_b, proj_b, x, w_proj_a.astype(BF16), w_proj_b.astype(BF16), w_mix_out.astype(BF16),
               min(256, s), (2 * sguw) // d)

    m = mem.shape[0]
    mw = MEM_HEADS * MEM_DIM
    mkv = _norm_matmul(mem, norm_mem_g, w_mkv.astype(BF16), F32, m, mw)
    mk_t = mkv[:, :mw].reshape(m, MEM_HEADS, MEM_DIM).transpose(1, 2, 0).astype(BF16)
    mv = mkv[:, mw:].reshape(m, MEM_HEADS, MEM_DIM).transpose(1, 0, 2).astype(BF16)
    x = _cross(x, norm_cross_g, w_mq.astype(BF16), mk_t, mv, w_mo.astype(BF16), min(512, s))
    return x


def kernel(x, mem, norm_mix_g, w_in, cmp_pe_k, cmp_k_w1, cmp_k_b1, cmp_k_w2, cmp_pe_v, cmp_v_w1, cmp_v_b1, cmp_v_w2, sgu_ln_g, sgu_ln_b, sgu_ws, sgu_b, w_proj_a, w_proj_b, w_mix_out, norm_cross_g, norm_mem_g, w_mq, w_mkv, w_mo, norm_ffn_g, w_ffn_in, w_ffn_out, norm_final_g):
    b, s, d = x.shape
    depth = w_in.shape[0]
    outs = []
    for bi in range(b):
        xb = x[bi]
        for l in range(depth):
            last = l == depth - 1
            xb = _layer(xb, mem[bi], norm_mix_g[l], w_in[l], cmp_pe_k[l], cmp_k_w1[l], cmp_k_b1[l], cmp_k_w2[l],
                        cmp_pe_v[l], cmp_v_w1[l], cmp_v_b1[l], cmp_v_w2[l], sgu_ln_g[l], sgu_ln_b[l], sgu_ws[l],
                        sgu_b[l], w_proj_a[l], w_proj_b[l], w_mix_out[l], norm_cross_g[l], norm_mem_g[l],
                        w_mq[l], w_mkv[l], w_mo[l], norm_ffn_g[l])
            xb = _ffn(xb, norm_ffn_g[l], w_ffn_in[l].astype(BF16), w_ffn_out[l].astype(BF16), norm_final_g, last,
                      min(512, s), 512)
        outs.append(xb)
    return jnp.stack(outs)
```

```python
import functools
import math

import numpy as np
import jax
import jax.numpy as jnp
from jax import lax
from jax.experimental import pallas as pl
from jax.experimental.pallas import tpu as pltpu

F32 = jnp.float32
BF16 = jnp.bfloat16

NORM_EPS = 1e-6
MASK_VALUE = -1e30
N_FORCED = 3
LOG2E = math.log2(math.e)

NSA_HEADS = 16
NSA_GROUPS = 4
NSA_HPG = NSA_HEADS // NSA_GROUPS
NSA_DIM = 64
N_KV_STREAMS = 6
CMP_BLOCK = 32
CMP_STRIDE = 16
SEL_BLOCK = 64
SEL_TOPK = 16
WINDOW = 512
QUERY_BLOCK = 128
SGU_GROUPS = 8
SGU_CHUNK = 128
MEM_HEADS = 4
MEM_DIM = 128

LANES = 128
SEL_TILE = 512
SEL_BPT = SEL_TILE // SEL_BLOCK
SEL_CHUNK = 256
NSA_BUCKETS = 4
BF16_ROWS = 16
VMEM_LIMIT = 56 * 1024 * 1024


def _params(*sem):
    return pltpu.CompilerParams(dimension_semantics=sem, vmem_limit_bytes=VMEM_LIMIT)


def _rms(x, g):
    return x * lax.rsqrt(jnp.mean(x * x, axis=-1, keepdims=True) + NORM_EPS) * g


def _dot(a, b):
    return jnp.dot(a, b, preferred_element_type=F32)


def _norm_matmul_kernel(x_ref, g_ref, w_ref, o_ref, h_ref):
    @pl.when(pl.program_id(1) == 0)
    def _():
        h_ref[...] = _rms(x_ref[...], g_ref[...]).astype(BF16)

    o_ref[...] = _dot(h_ref[...], w_ref[...]).astype(o_ref.dtype)


def _norm_matmul(x, g, w, out_dtype, tm, tn):
    s, d = x.shape
    n = w.shape[1]
    return pl.pallas_call(
        _norm_matmul_kernel,
        grid=(s // tm, n // tn),
        in_specs=[
            pl.BlockSpec((tm, d), lambda i, j: (i, 0)),
            pl.BlockSpec((1, d), lambda i, j: (0, 0)),
            pl.BlockSpec((d, tn), lambda i, j: (0, j)),
        ],
        out_specs=pl.BlockSpec((tm, tn), lambda i, j: (i, j)),
        out_shape=jax.ShapeDtypeStruct((s, n), out_dtype),
        scratch_shapes=[pltpu.VMEM((tm, d), BF16)],
        compiler_params=_params("parallel", "arbitrary"),
        name="norm_matmul",
    )(x, g.reshape(1, d), w)


PROJ_TN = 512


def _proj_attn_kernel(x_ref, g_ref, w_ref, nat_ref, kw_ref, ksa_ref, vsa_ref, vwt_ref, q_ref, h_ref):
    i = pl.program_id(0)
    j = pl.program_id(1)
    tm = x_ref.shape[0]
    dh = NSA_DIM
    ng = NSA_GROUPS

    @pl.when(j == 0)
    def _():
        h_ref[...] = _rms(x_ref[...], g_ref[...]).astype(BF16)

    res = _dot(h_ref[...], w_ref[...])

    def cols(c):
        return res[:, c * dh:(c + 1) * dh].astype(BF16)

    def cols_t(c):
        slab = res[:, (c // 2) * 2 * dh:(c // 2 + 1) * 2 * dh].T
        return slab[(c % 2) * dh:(c % 2 + 1) * dh].astype(BF16)

    @pl.when(j == 0)
    def _():
        for c in range(2 * ng):
            nat_ref[c // ng, c % ng] = cols(c)

    @pl.when(j == 1)
    def _():
        pos = i * tm + lax.broadcasted_iota(jnp.int32, (tm, dh), 0)
        lane = lax.broadcasted_iota(jnp.int32, (tm, dh), 1)
        onehot = jnp.where((pos // SEL_BLOCK) % SEL_BPT == lane, 1.0, 0.0).astype(BF16)
        for g in range(ng):
            kw_ref[g] = cols(g)
            ksa_ref[g, :, :dh] = cols(ng + g)
            ksa_ref[g, :, dh:] = onehot

    @pl.when(j == 2)
    def _():
        ones_row = jnp.where(lax.broadcasted_iota(jnp.int32, (BF16_ROWS, tm), 0) == 0, 1.0, 0.0).astype(BF16)
        for g in range(ng):
            vsa_ref[g, :dh, :] = cols_t(g)
            vsa_ref[g, dh:, :] = ones_row
            vwt_ref[g] = cols_t(ng + g)

    @pl.when(j >= 3)
    def _():
        q_ref[...] = res.astype(BF16)


def _proj_attn(x, g, w, tm):
    s, d = x.shape
    dh, ng, tn = NSA_DIM, NSA_GROUPS, PROJ_TN
    assert 2 * ng * dh == tn and w.shape[1] == 3 * tn + NSA_HEADS * dh
    nq = NSA_HEADS * dh // tn
    shapes = [
        jax.ShapeDtypeStruct((2, ng, s, dh), BF16),
        jax.ShapeDtypeStruct((ng, s, dh), BF16),
        jax.ShapeDtypeStruct((ng, s, 2 * dh), BF16),
        jax.ShapeDtypeStruct((ng, dh + BF16_ROWS, s), BF16),
        jax.ShapeDtypeStruct((ng, dh, s), BF16),
        jax.ShapeDtypeStruct((s, NSA_HEADS * dh), BF16),
    ]
    return pl.pallas_call(
        _proj_attn_kernel,
        grid=(s // tm, 3 + nq),
        in_specs=[
            pl.BlockSpec((tm, d), lambda i, j: (i, 0)),
            pl.BlockSpec((1, d), lambda i, j: (0, 0)),
            pl.BlockSpec((d, tn), lambda i, j: (0, j)),
        ],
        out_specs=[
            pl.BlockSpec((2, ng, tm, dh), lambda i, j: (0, 0, i, 0)),
            pl.BlockSpec((ng, tm, dh), lambda i, j: (0, i, 0)),
            pl.BlockSpec((ng, tm, 2 * dh), lambda i, j: (0, i, 0)),
            pl.BlockSpec((ng, dh + BF16_ROWS, tm), lambda i, j: (0, 0, i)),
            pl.BlockSpec((ng, dh, tm), lambda i, j: (0, 0, i)),
            pl.BlockSpec((tm, tn), lambda i, j: (i, jnp.maximum(j - 3, 0))),
        ],
        out_shape=shapes,
        scratch_shapes=[pltpu.VMEM((tm, d), BF16)],
        compiler_params=_params("parallel", "arbitrary"),
        name="proj_attn",
    )(x, g.reshape(1, d), w)


def _proj_gate_kernel(x_ref, g_ref, w_ref, o_ref, gt_ref, h_ref):
    j = pl.program_id(1)
    last = pl.num_programs(1) - 1

    @pl.when(j == 0)
    def _():
        h_ref[...] = _rms(x_ref[...], g_ref[...]).astype(BF16)

    res = _dot(h_ref[...], w_ref[...])

    @pl.when(j < last)
    def _():
        o_ref[...] = res

    @pl.when(j == last)
    def _():
        gt_ref[...] = res[:, :LANES].T


def _proj_gate(x, g, w, tm):
    s, d = x.shape
    tn = PROJ_TN
    n = w.shape[1] - tn
    return pl.pallas_call(
        _proj_gate_kernel,
        grid=(s // tm, n // tn + 1),
        in_specs=[
            pl.BlockSpec((tm, d), lambda i, j: (i, 0)),
            pl.BlockSpec((1, d), lambda i, j: (0, 0)),
            pl.BlockSpec((d, tn), lambda i, j: (0, j)),
        ],
        out_specs=[
            pl.BlockSpec((tm, tn), lambda i, j: (i, jnp.minimum(j, n // tn - 1))),
            pl.BlockSpec((LANES, tm), lambda i, j: (0, i)),
        ],
        out_shape=[jax.ShapeDtypeStruct((s, n), F32), jax.ShapeDtypeStruct((LANES, s), F32)],
        scratch_shapes=[pltpu.VMEM((tm, d), BF16)],
        compiler_params=_params("parallel", "arbitrary"),
        name="proj_gate",
    )(x, g.reshape(1, d), w)


def _compress_kernel(r_ref, w1_ref, pe_ref, b1_ref, w2_ref, o_ref, ot_ref):
    half = r_ref.shape[-1]
    r = r_ref[0, 0]
    bias = _dot(pe_ref[0].astype(BF16), w1_ref[0]) + b1_ref[0]
    top = _dot(r, w1_ref[0, :half, :])
    bot = _dot(r, w1_ref[0, half:, :])
    ncp = r.shape[0]
    hid = top + pltpu.roll(bot, ncp - 1, 0) + bias
    out = _dot(jax.nn.gelu(hid).astype(BF16), w2_ref[0])
    o_ref[0, 0] = out.astype(o_ref.dtype)
    dh = out.shape[1]
    out_t = jnp.concatenate([out, jnp.zeros((ncp, LANES - dh), F32)], axis=1).T
    ot_ref[0, 0] = out_t[:dh].astype(ot_ref.dtype)


def _compress(r, w1, pe, b1, w2):
    _, g, ncp, half = r.shape
    hidden = w1.shape[-1]
    dh = w2.shape[-1]
    return pl.pallas_call(
        _compress_kernel,
        grid=(2, g),
        in_specs=[
            pl.BlockSpec((1, 1, ncp, half), lambda s, gg: (s, gg, 0, 0)),
            pl.BlockSpec((1, 2 * half, hidden), lambda s, gg: (s, 0, 0)),
            pl.BlockSpec((1, 1, 2 * half), lambda s, gg: (s, 0, 0)),
            pl.BlockSpec((1, 1, hidden), lambda s, gg: (s, 0, 0)),
            pl.BlockSpec((1, hidden, dh), lambda s, gg: (s, 0, 0)),
        ],
        out_specs=[pl.BlockSpec((1, 1, ncp, dh), lambda s, gg: (s, gg, 0, 0)),
                   pl.BlockSpec((1, 1, dh, ncp), lambda s, gg: (s, gg, 0, 0))],
        out_shape=[jax.ShapeDtypeStruct((2, g, ncp, dh), BF16), jax.ShapeDtypeStruct((2, g, dh, ncp), BF16)],
        compiler_params=_params("parallel", "parallel"),
        name="compress",
    )(r, w1, pe, b1, w2)


def _softmax2_cols(s, bias, valid):
    s = s + bias
    e = jnp.exp2(s - jnp.max(s, axis=0, keepdims=True))
    l = jnp.sum(e, axis=0, keepdims=True)
    return e * jnp.where(valid, 1.0 / l, 0.0)


def _nsa_kernel(q_ref, gt_ref, kc_ref, vct_ref, covt_ref, ks_ref, vst_ref, kw_ref, vwt_ref, o_ref,
                qa_ref, sel_ref, s0_ref, s1_ref, acc_ref, part_ref):
    s_refs = (s0_ref, s1_ref)
    g = pl.program_id(0)
    i = pl.program_id(1)
    qb = QUERY_BLOCK
    start = i * qb
    nb = sel_ref.shape[0]
    ncp = kc_ref.shape[1]
    s_len = ks_ref.shape[1]
    hq = NSA_HPG * qb
    t_row = start + lax.broadcasted_iota(jnp.int32, (1, qb), 1)

    qt = (q_ref[...].astype(F32) * (NSA_DIM ** -0.5 * LOG2E)).T
    q_t = jnp.concatenate([qt[h * NSA_DIM:(h + 1) * NSA_DIM] for h in range(NSA_HPG)], axis=1).astype(BF16)
    qa_ref[:NSA_DIM, :] = q_t
    qa_ref[NSA_DIM:, :] = jnp.zeros((qa_ref.shape[0] - NSA_DIM, hq), BF16)

    jt = t_row // SEL_BLOCK
    picked = -2.0

    def gate(branch):
        rows = [gt_ref[pl.ds(g * (NSA_HPG * 3) + h * 3 + branch, 1), :] for h in range(NSA_HPG)]
        return jax.nn.sigmoid(jnp.concatenate(rows, axis=1))

    def window():
        wk = WINDOW + qb
        k0w = pl.multiple_of(jnp.maximum(start - WINDOW, 0), qb)
        sw = _dot(kw_ref[0, pl.ds(k0w, wk), :], q_t)
        kpos = k0w + lax.broadcasted_iota(jnp.int32, (wk, qb), 0)
        wbias = jnp.where((kpos <= t_row) & (kpos > t_row - WINDOW), 0.0, MASK_VALUE)
        p_win = jnp.concatenate(
            [_softmax2_cols(sw[:, h * qb:(h + 1) * qb], wbias, True).astype(BF16) for h in range(NSA_HPG)], axis=1)
        return _dot(vwt_ref[0, :, pl.ds(k0w, wk)], p_win)

    def compressed_and_select(nck, nbk):
        sc = _dot(kc_ref[0, :nck, :], q_t)
        cmask = lax.broadcasted_iota(jnp.int32, (nck, qb), 0) * CMP_STRIDE + (CMP_BLOCK - 1) <= t_row
        cbias = jnp.where(cmask, 0.0, MASK_VALUE)
        cvalid = t_row >= CMP_BLOCK - 1
        p_heads = [_softmax2_cols(sc[:, h * qb:(h + 1) * qb], cbias, cvalid) for h in range(NSA_HPG)]
        psum = p_heads[0]
        for p in p_heads[1:]:
            psum = psum + p
        o_cmp = _dot(vct_ref[0, :, :nck], jnp.concatenate([p.astype(BF16) for p in p_heads], axis=1))
        part_ref[...] = gate(0) * o_cmp + gate(2) * window()
        p_hi = psum.astype(BF16)
        p_lo = (psum - p_hi.astype(F32)).astype(BF16)
        cov = covt_ref[:nbk, :nck]
        imp = _dot(cov, p_hi) + _dot(cov, p_lo)

        blk = lax.broadcasted_iota(jnp.int32, (nbk, qb), 0)
        forced = (blk == 0) | (blk == jt) | (blk == jt - 1)
        score = jnp.where(forced, picked, jnp.where(blk <= jt, imp, -1.0))
        blk_f = blk.astype(F32)

        def pick(_, score):
            mx = jnp.max(score, axis=0, keepdims=True)
            idx = jnp.min(jnp.where(score == mx, blk_f, float(nbk)), axis=0, keepdims=True)
            return jnp.where(blk_f == idx, picked, score)

        score = lax.fori_loop(0, min(SEL_TOPK, nbk) - N_FORCED, pick, score)
        sel_ref[:nbk, :] = jnp.where((score == picked) & (blk <= jt), 0.0, MASK_VALUE)

    bucket_ok = s_len % (NSA_BUCKETS * SEL_TILE) == 0 and nb // NSA_BUCKETS >= SEL_TOPK
    n_bucket = NSA_BUCKETS if bucket_ok else 1
    bucket = (start + qb - 1) // (s_len // n_bucket)
    for b in range(n_bucket):
        pl.when(bucket == b)(functools.partial(
            compressed_and_select, (b + 1) * (ncp // n_bucket), (b + 1) * (nb // n_bucket)))

    tk = SEL_TILE

    def scores(slot, kt):
        k0 = pl.multiple_of(kt * tk, tk)
        b8 = sel_ref[pl.ds(pl.multiple_of(kt * SEL_BPT, SEL_BPT), SEL_BPT), :]
        b16 = jnp.concatenate([b8, jnp.zeros((BF16_ROWS - SEL_BPT, qb), F32)], axis=0).astype(BF16)
        qa_ref[NSA_DIM:NSA_DIM + BF16_ROWS, :] = jnp.concatenate([b16] * NSA_HPG, axis=1)
        s = _dot(ks_ref[0, pl.ds(k0, tk), :], qa_ref[...])
        s_refs[slot][...] = s
        return jnp.max(s, axis=0, keepdims=True)

    def update(slot, kt, mt, m, causal):
        k0 = pl.multiple_of(kt * tk, tk)
        s = s_refs[slot][...]
        if causal:
            kpos = k0 + lax.broadcasted_iota(jnp.int32, (tk, qb), 0)
            s = s + jnp.concatenate([jnp.where(kpos <= t_row, 0.0, MASK_VALUE)] * NSA_HPG, axis=1)
            mt = jnp.max(s, axis=0, keepdims=True)
        m_new = jnp.maximum(m, mt)
        p = jnp.exp2(s - m_new).astype(BF16)
        acc_ref[...] = jnp.exp2(m - m_new) * acc_ref[...] + _dot(vst_ref[0, :, pl.ds(k0, tk)], p)
        return m_new

    def fused(cur, kt_cur, mt_cur, m, nxt, kt_nxt):
        k0c = pl.multiple_of(kt_cur * tk, tk)
        k0n = pl.multiple_of(kt_nxt * tk, tk)
        b8 = sel_ref[pl.ds(pl.multiple_of(kt_nxt * SEL_BPT, SEL_BPT), SEL_BPT), :]
        b16 = jnp.concatenate([b8, jnp.zeros((BF16_ROWS - SEL_BPT, qb), F32)], axis=0).astype(BF16)
        qa_ref[NSA_DIM:NSA_DIM + BF16_ROWS, :] = jnp.concatenate([b16] * NSA_HPG, axis=1)
        m_new = jnp.maximum(m, mt_cur)
        mt_nxt, pv = None, None
        for c in range(tk // SEL_CHUNK):
            rows = pl.ds(c * SEL_CHUNK, SEL_CHUNK)
            s_n = _dot(ks_ref[0, pl.ds(k0n + c * SEL_CHUNK, SEL_CHUNK), :], qa_ref[...])
            s_refs[nxt][rows, :] = s_n
            mt_c = jnp.max(s_n, axis=0, keepdims=True)
            mt_nxt = mt_c if mt_nxt is None else jnp.maximum(mt_nxt, mt_c)
            p = jnp.exp2(s_refs[cur][rows, :] - m_new).astype(BF16)
            pv_c = _dot(vst_ref[0, :, pl.ds(k0c + c * SEL_CHUNK, SEL_CHUNK)], p)
            pv = pv_c if pv is None else pv + pv_c
        acc_ref[...] = jnp.exp2(m - m_new) * acc_ref[...] + pv
        return mt_nxt, m_new

    def pair(pi, carry):
        mt_a, m = carry
        mt_b, m = fused(0, 2 * pi, mt_a, m, 1, 2 * pi + 1)
        mt_a, m = fused(1, 2 * pi + 1, mt_b, m, 0, 2 * pi + 2)
        return mt_a, m

    acc_ref[...] = jnp.zeros(acc_ref.shape, F32)
    last = (start + qb - 1) // tk
    mt_a, m = lax.fori_loop(0, last // 2, pair, (scores(0, 0), jnp.full((1, hq), MASK_VALUE, F32)))

    @pl.when(last % 2 == 0)
    def _():
        update(0, last, mt_a, m, True)

    @pl.when(last % 2 == 1)
    def _():
        mt_b = scores(1, last)
        update(1, last, mt_b, update(0, last - 1, mt_a, m, False), True)

    o_sel = acc_ref[:NSA_DIM, :] * (1.0 / acc_ref[NSA_DIM:NSA_DIM + 1, :])

    out_t = part_ref[...] + gate(1) * o_sel
    outs = [out_t[:, h * qb:(h + 1) * qb] for h in range(NSA_HPG)]
    o_ref[...] = jnp.concatenate(outs, axis=0).T.astype(o_ref.dtype)


def _nsa(q, gates_t, kc, vc_t, cov_t, ks_aug, vs_aug_t, kw, vw_t):
    s = q.shape[0]
    assert s % (2 * SEL_TILE) == 0 and s >= WINDOW + QUERY_BLOCK
    gdim = NSA_HPG * NSA_DIM
    ncp = kc.shape[1]
    nb = s // SEL_BLOCK
    ka = ks_aug.shape[-1]
    va = vs_aug_t.shape[1]
    hq = NSA_HPG * QUERY_BLOCK
    per_group = lambda shape: pl.BlockSpec((1,) + shape, lambda g, i: (g, 0, 0))
    return pl.pallas_call(
        _nsa_kernel,
        grid=(NSA_GROUPS, s // QUERY_BLOCK),
        in_specs=[
            pl.BlockSpec((QUERY_BLOCK, gdim), lambda g, i: (i, g)),
            pl.BlockSpec((NSA_HEADS * 3, QUERY_BLOCK), lambda g, i: (0, i)),
            per_group((ncp, NSA_DIM)),
            per_group((NSA_DIM, ncp)),
            pl.BlockSpec((nb, ncp), lambda g, i: (0, 0)),
            per_group((s, ka)),
            per_group((va, s)),
            per_group((s, NSA_DIM)),
            per_group((NSA_DIM, s)),
        ],
        out_specs=pl.BlockSpec((QUERY_BLOCK, gdim), lambda g, i: (i, g)),
        out_shape=jax.ShapeDtypeStruct((s, NSA_HEADS * NSA_DIM), BF16),
        scratch_shapes=[pltpu.VMEM((ka, hq), BF16), pltpu.VMEM((nb, QUERY_BLOCK), F32),
                        pltpu.VMEM((SEL_TILE, hq), F32), pltpu.VMEM((SEL_TILE, hq), F32),
                        pltpu.VMEM((va, hq), F32),
                        pltpu.VMEM((NSA_DIM, hq), F32)],
        compiler_params=_params("parallel", "arbitrary"),
        name="nsa",
    )(q, gates_t, kc, vc_t, cov_t, ks_aug, vs_aug_t, kw, vw_t)


def _coverage_t(nc, ncp, nb):
    cs = np.arange(nc)[None, :] * CMP_STRIDE
    ss = np.arange(nb)[:, None] * SEL_BLOCK
    cov = np.clip(np.minimum(cs + CMP_BLOCK, ss + SEL_BLOCK) - np.maximum(cs, ss), 0, None) / CMP_BLOCK
    return jnp.asarray(np.pad(cov, ((0, 0), (0, ncp - nc))), dtype=BF16)


def _sgu_kernel(u_ref, v_ref, lng_ref, lnb_ref, ws_ref, bs_ref, o_ref):
    c = SGU_CHUNK
    tm = u_ref.shape[0]
    v = jax.nn.gelu(v_ref[...])
    mu = jnp.mean(v, axis=-1, keepdims=True)
    var = jnp.mean(jnp.square(v - mu), axis=-1, keepdims=True)
    vn = ((v - mu) * lax.rsqrt(var + NORM_EPS) * lng_ref[...] + lnb_ref[...]).astype(BF16)
    u = jax.nn.gelu(u_ref[...])
    tri = lax.broadcasted_iota(jnp.int32, (c, c), 0) >= lax.broadcasted_iota(jnp.int32, (c, c), 1)
    for g in range(SGU_GROUPS):
        w = jnp.where(tri, ws_ref[g], 0.0).astype(BF16)
        cols = slice(g * c, (g + 1) * c)
        rhs = jnp.concatenate([vn[k * c:(k + 1) * c, cols] for k in range(tm // c)], axis=1)
        mixed = _dot(w, rhs)
        for k in range(tm // c):
            rows = slice(k * c, (k + 1) * c)
            o_ref[rows, cols] = (u[rows, cols] * (mixed[:, rows] + bs_ref[g])).astype(o_ref.dtype)


def _sgu(proj, lng, lnb, ws, bs, tm):
    s = proj.shape[0]
    w = lng.shape[0]
    c = SGU_CHUNK
    bs_b = jnp.broadcast_to(bs[:, :, None], (SGU_GROUPS, c, c))
    return pl.pallas_call(
        _sgu_kernel,
        grid=(s // tm,),
        in_specs=[
            pl.BlockSpec((tm, w), lambda i: (i, 0)),
            pl.BlockSpec((tm, w), lambda i: (i, 1)),
            pl.BlockSpec((1, w), lambda i: (0, 0)),
            pl.BlockSpec((1, w), lambda i: (0, 0)),
            pl.BlockSpec((SGU_GROUPS, c, c), lambda i: (0, 0, 0)),
            pl.BlockSpec((SGU_GROUPS, c, c), lambda i: (0, 0, 0)),
        ],
        out_specs=pl.BlockSpec((tm, w), lambda i: (i, 0)),
        out_shape=jax.ShapeDtypeStruct((s, w), BF16),
        compiler_params=_params("parallel"),
        name="sgu",
    )(proj, proj, lng.reshape(1, w), lnb.reshape(1, w), ws, bs_b)


def _merge_kernel(oa_ref, ob_ref, ga_ref, gb_ref, x_ref, pa_ref, pb_ref, wo_ref, o_ref):
    a = _dot(oa_ref[...], pa_ref[...])
    b = _dot(ob_ref[...], pb_ref[...])
    merged = jax.nn.sigmoid(ga_ref[...]) * a + jax.nn.sigmoid(gb_ref[...]) * b
    o_ref[...] = x_ref[...] + _dot(merged.astype(BF16), wo_ref[...])


def _merge(o_a, o_b, proj, x, p_a, p_b, w_o, tm, gate_block):
    s, d = x.shape
    wa = o_a.shape[1]
    wb = o_b.shape[1]
    resident = lambda shape: pl.BlockSpec(shape, lambda i: (0, 0), pipeline_mode=pl.Buffered(1))
    return pl.pallas_call(
        _merge_kernel,
        grid=(s // tm,),
        in_specs=[
            pl.BlockSpec((tm, wa), lambda i: (i, 0)),
            pl.BlockSpec((tm, wb), lambda i: (i, 0)),
            pl.BlockSpec((tm, d), lambda i: (i, gate_block)),
            pl.BlockSpec((tm, d), lambda i: (i, gate_block + 1)),
            pl.BlockSpec((tm, d), lambda i: (i, 0)),
            resident((wa, d)),
            resident((wb, d)),
            resident((d, d)),
        ],
        out_specs=pl.BlockSpec((tm, d), lambda i: (i, 0)),
        out_shape=jax.ShapeDtypeStruct((s, d), F32),
        compiler_params=_params("parallel"),
        name="merge",
    )(o_a, o_b, proj, proj, x, p_a, p_b, w_o)


def _cross_kernel(x_ref, g_ref, wq_ref, mkt_ref, mv_ref, wo_ref, o_ref):
    x = x_ref[...]
    h = _rms(x, g_ref[...]).astype(BF16)
    mq = (_dot(h, wq_ref[...]) * (MEM_DIM ** -0.5)).astype(BF16)
    outs = []
    for hh in range(MEM_HEADS):
        s = _dot(mq[:, hh * MEM_DIM:(hh + 1) * MEM_DIM], mkt_ref[hh])
        e = jnp.exp(s - jnp.max(s, axis=-1, keepdims=True))
        p = e / jnp.sum(e, axis=-1, keepdims=True)
        outs.append(_dot(p.astype(BF16), mv_ref[hh]).astype(BF16))
    o_ref[...] = x + _dot(jnp.concatenate(outs, axis=1), wo_ref[...])


def _cross(x, g, w_q, mk_t, mv, w_o, tm):
    s, d = x.shape
    mw = w_q.shape[1]
    m = mv.shape[1]
    return pl.pallas_call(
        _cross_kernel,
        grid=(s // tm,),
        in_specs=[
            pl.BlockSpec((tm, d), lambda i: (i, 0)),
            pl.BlockSpec((1, d), lambda i: (0, 0)),
            pl.BlockSpec((d, mw), lambda i: (0, 0)),
            pl.BlockSpec((MEM_HEADS, MEM_DIM, m), lambda i: (0, 0, 0)),
            pl.BlockSpec((MEM_HEADS, m, MEM_DIM), lambda i: (0, 0, 0)),
            pl.BlockSpec((mw, d), lambda i: (0, 0)),
        ],
        out_specs=pl.BlockSpec((tm, d), lambda i: (i, 0)),
        out_shape=jax.ShapeDtypeStruct((s, d), F32),
        compiler_params=_params("parallel"),
        name="cross",
    )(x, g.reshape(1, d), w_q, mk_t, mv, w_o)


def _ffn_kernel(x_ref, g_ref, wg_ref, wu_ref, wo_ref, gf_ref, o_ref, h_ref, acc_ref, *, final_norm):
    j = pl.program_id(1)

    @pl.when(j == 0)
    def _():
        h_ref[...] = _rms(x_ref[...], g_ref[...]).astype(BF16)
        acc_ref[...] = jnp.zeros_like(acc_ref)

    h = h_ref[...]
    act = (jax.nn.silu(_dot(h, wg_ref[...])) * _dot(h, wu_ref[...])).astype(BF16)
    acc_ref[...] += _dot(act, wo_ref[...])

    @pl.when(j == pl.num_programs(1) - 1)
    def _():
        y = x_ref[...] + acc_ref[...]
        o_ref[...] = _rms(y, gf_ref[...]) if final_norm else y


def _ffn(x, g, w_in, w_out, g_final, final_norm, tm, th):
    s, d = x.shape
    hidden = w_out.shape[0]
    nh = hidden // th
    return pl.pallas_call(
        functools.partial(_ffn_kernel, final_norm=final_norm),
        grid=(s // tm, nh),
        in_specs=[
            pl.BlockSpec((tm, d), lambda i, j: (i, 0)),
            pl.BlockSpec((1, d), lambda i, j: (0, 0)),
            pl.BlockSpec((d, th), lambda i, j: (0, j)),
            pl.BlockSpec((d, th), lambda i, j: (0, j + nh)),
            pl.BlockSpec((th, d), lambda i, j: (j, 0)),
            pl.BlockSpec((1, d), lambda i, j: (0, 0)),
        ],
        out_specs=pl.BlockSpec((tm, d), lambda i, j: (i, 0)),
        out_shape=jax.ShapeDtypeStruct((s, d), F32),
        scratch_shapes=[pltpu.VMEM((tm, d), BF16), pltpu.VMEM((tm, d), F32)],
        compiler_params=_params("parallel", "arbitrary"),
        name="ffn",
    )(x, g.reshape(1, d), w_in, w_in, w_out, g_final.reshape(1, d))


def _layer(x, mem, norm_mix_g, w_in, cmp_pe_k, cmp_k_w1, cmp_k_b1, cmp_k_w2, cmp_pe_v, cmp_v_w1, cmp_v_b1, cmp_v_w2,
           sgu_ln_g, sgu_ln_b, sgu_ws, sgu_b, w_proj_a, w_proj_b, w_mix_out, norm_cross_g, norm_mem_g,
           w_mq, w_mkv, w_mo, norm_ffn_g):
    s, d = x.shape
    qw = NSA_HEADS * NSA_DIM
    kvw = NSA_GROUPS * NSA_DIM
    sguw = sgu_ln_g.shape[0]
    ngate = NSA_HEADS * 3
    o_kv = qw
    o_gate = o_kv + N_KV_STREAMS * kvw
    o_u = o_gate + ngate

    w_bf = w_in.astype(BF16)
    stream = lambda st: w_bf[:, o_kv + st * kvw:o_kv + (st + 1) * kvw]
    w_a = jnp.concatenate([stream(st) for st in (0, 1, 4, 2, 3, 5)] + [w_bf[:, :qw]], axis=1)
    w_b = jnp.pad(w_bf[:, o_u:], ((0, 0), (0, PROJ_TN)))
    w_b = lax.dynamic_update_slice(w_b, w_bf[:, o_gate:o_u], (0, w_in.shape[1] - o_u))
    tm_proj = min(1024, s)
    nat, kw, ks_aug, vs_aug_t, vw_t, q = _proj_attn(x, norm_mix_g, w_a, tm_proj)
    proj_b, gates_t = _proj_gate(x, norm_mix_g, w_b, tm_proj)

    ncp = s // CMP_STRIDE
    nc = (s - CMP_BLOCK) // CMP_STRIDE + 1
    r = nat.reshape(2, NSA_GROUPS, ncp, CMP_STRIDE * NSA_DIM)
    w1 = jnp.stack([cmp_k_w1, cmp_v_w1]).astype(BF16)
    pe = jnp.stack([cmp_pe_k.reshape(1, -1), cmp_pe_v.reshape(1, -1)])
    b1 = jnp.stack([cmp_k_b1.reshape(1, -1), cmp_v_b1.reshape(1, -1)])
    w2 = jnp.stack([cmp_k_w2, cmp_v_w2]).astype(BF16)
    cmp, cmp_t = _compress(r, w1, pe, b1, w2)
    cov_t = _coverage_t(nc, ncp, s // SEL_BLOCK)
    o_a = _nsa(q, gates_t, cmp[0], cmp_t[1], cov_t, ks_aug, vs_aug_t, kw, vw_t)

    o_b = _sgu(proj_b, sgu_ln_g, sgu_ln_b, sgu_ws, sgu_b, min(512, s))
    x = _merge(o_a, o_b, proj_b, x, w_proj_a.astype(BF16), w_proj_b.astype(BF16), w_mix_out.astype(BF16),
               min(256, s), (2 * sguw) // d)

    m = mem.shape[0]
    mw = MEM_HEADS * MEM_DIM
    mkv = _norm_matmul(mem, norm_mem_g, w_mkv.astype(BF16), F32, m, mw)
    mk_t = mkv[:, :mw].reshape(m, MEM_HEADS, MEM_DIM).transpose(1, 2, 0).astype(BF16)
    mv = mkv[:, mw:].reshape(m, MEM_HEADS, MEM_DIM).transpose(1, 0, 2).astype(BF16)
    x = _cross(x, norm_cross_g, w_mq.astype(BF16), mk_t, mv, w_mo.astype(BF16), min(512, s))
    return x


def kernel(x, mem, norm_mix_g, w_in, cmp_pe_k, cmp_k_w1, cmp_k_b1, cmp_k_w2, cmp_pe_v, cmp_v_w1, cmp_v_b1, cmp_v_w2, sgu_ln_g, sgu_ln_b, sgu_ws, sgu_b, w_proj_a, w_proj_b, w_mix_out, norm_cross_g, norm_mem_g, w_mq, w_mkv, w_mo, norm_ffn_g, w_ffn_in, w_ffn_out, norm_final_g):
    b, s, d = x.shape
    depth = w_in.shape[0]
    outs = []
    for bi in range(b):
        xb = x[bi]
        for l in range(depth):
            last = l == depth - 1
            xb = _layer(xb, mem[bi], norm_mix_g[l], w_in[l], cmp_pe_k[l], cmp_k_w1[l], cmp_k_b1[l], cmp_k_w2[l],
                        cmp_pe_v[l], cmp_v_w1[l], cmp_v_b1[l], cmp_v_w2[l], sgu_ln_g[l], sgu_ln_b[l], sgu_ws[l],
                        sgu_b[l], w_proj_a[l], w_proj_b[l], w_mix_out[l], norm_cross_g[l], norm_mem_g[l],
                        w_mq[l], w_mkv[l], w_mo[l], norm_ffn_g[l])
            xb = _ffn(xb, norm_ffn_g[l], w_ffn_in[l].astype(BF16), w_ffn_out[l].astype(BF16), norm_final_g, last,
                      min(512, s), 512)
        outs.append(xb)
    return jnp.stack(outs)
```

```python
import functools
import math

import numpy as np
import jax
import jax.numpy as jnp
from jax import lax
from jax.experimental import pallas as pl
from jax.experimental.pallas import tpu as pltpu

F32 = jnp.float32
BF16 = jnp.bfloat16

NORM_EPS = 1e-6
MASK_VALUE = -1e30
N_FORCED = 3
LOG2E = math.log2(math.e)

NSA_HEADS = 16
NSA_GROUPS = 4
NSA_HPG = NSA_HEADS // NSA_GROUPS
NSA_DIM = 64
N_KV_STREAMS = 6
CMP_BLOCK = 32
CMP_STRIDE = 16
SEL_BLOCK = 64
SEL_TOPK = 16
WINDOW = 512
QUERY_BLOCK = 128
SGU_GROUPS = 8
SGU_CHUNK = 128
MEM_HEADS = 4
MEM_DIM = 128

LANES = 128
SEL_TILE = 512
SEL_BPT = SEL_TILE // SEL_BLOCK
SEL_CHUNK = 256
SEL_UNROLL = 8
NSA_BUCKETS = 4
BF16_ROWS = 16
VMEM_LIMIT = 56 * 1024 * 1024


def _params(*sem):
    return pltpu.CompilerParams(dimension_semantics=sem, vmem_limit_bytes=VMEM_LIMIT)


def _rms(x, g):
    return x * lax.rsqrt(jnp.mean(x * x, axis=-1, keepdims=True) + NORM_EPS) * g


def _dot(a, b):
    return jnp.dot(a, b, preferred_element_type=F32)


def _norm_matmul_kernel(x_ref, g_ref, w_ref, o_ref, h_ref):
    @pl.when(pl.program_id(1) == 0)
    def _():
        h_ref[...] = _rms(x_ref[...], g_ref[...]).astype(BF16)

    o_ref[...] = _dot(h_ref[...], w_ref[...]).astype(o_ref.dtype)


def _norm_matmul(x, g, w, out_dtype, tm, tn):
    s, d = x.shape
    n = w.shape[1]
    return pl.pallas_call(
        _norm_matmul_kernel,
        grid=(s // tm, n // tn),
        in_specs=[
            pl.BlockSpec((tm, d), lambda i, j: (i, 0)),
            pl.BlockSpec((1, d), lambda i, j: (0, 0)),
            pl.BlockSpec((d, tn), lambda i, j: (0, j)),
        ],
        out_specs=pl.BlockSpec((tm, tn), lambda i, j: (i, j)),
        out_shape=jax.ShapeDtypeStruct((s, n), out_dtype),
        scratch_shapes=[pltpu.VMEM((tm, d), BF16)],
        compiler_params=_params("parallel", "arbitrary"),
        name="norm_matmul",
    )(x, g.reshape(1, d), w)


PROJ_TN = 512


def _proj_attn_kernel(x_ref, g_ref, w_ref, nat_ref, kw_ref, ksa_ref, vsa_ref, vwt_ref, q_ref, h_ref):
    i = pl.program_id(0)
    j = pl.program_id(1)
    tm = x_ref.shape[0]
    dh = NSA_DIM
    ng = NSA_GROUPS

    @pl.when(j == 0)
    def _():
        h_ref[...] = _rms(x_ref[...], g_ref[...]).astype(BF16)

    res = _dot(h_ref[...], w_ref[...])

    def cols(c):
        return res[:, c * dh:(c + 1) * dh].astype(BF16)

    def cols_t(c):
        slab = res[:, (c // 2) * 2 * dh:(c // 2 + 1) * 2 * dh].T
        return slab[(c % 2) * dh:(c % 2 + 1) * dh].astype(BF16)

    @pl.when(j == 0)
    def _():
        for c in range(2 * ng):
            nat_ref[c // ng, c % ng] = cols(c)

    @pl.when(j == 1)
    def _():
        pos = i * tm + lax.broadcasted_iota(jnp.int32, (tm, dh), 0)
        lane = lax.broadcasted_iota(jnp.int32, (tm, dh), 1)
        onehot = jnp.where((pos // SEL_BLOCK) % SEL_BPT == lane, 1.0, 0.0).astype(BF16)
        for g in range(ng):
            kw_ref[g] = cols(g)
            ksa_ref[g, :, :dh] = cols(ng + g)
            ksa_ref[g, :, dh:] = onehot

    @pl.when(j == 2)
    def _():
        ones_row = jnp.where(lax.broadcasted_iota(jnp.int32, (BF16_ROWS, tm), 0) == 0, 1.0, 0.0).astype(BF16)
        for g in range(ng):
            vsa_ref[g, :dh, :] = cols_t(g)
            vsa_ref[g, dh:, :] = ones_row
            vwt_ref[g] = cols_t(ng + g)

    @pl.when(j >= 3)
    def _():
        q_ref[...] = res.astype(BF16)


def _proj_attn(x, g, w, tm):
    s, d = x.shape
    dh, ng, tn = NSA_DIM, NSA_GROUPS, PROJ_TN
    assert 2 * ng * dh == tn and w.shape[1] == 3 * tn + NSA_HEADS * dh
    nq = NSA_HEADS * dh // tn
    shapes = [
        jax.ShapeDtypeStruct((2, ng, s, dh), BF16),
        jax.ShapeDtypeStruct((ng, s, dh), BF16),
        jax.ShapeDtypeStruct((ng, s, 2 * dh), BF16),
        jax.ShapeDtypeStruct((ng, dh + BF16_ROWS, s), BF16),
        jax.ShapeDtypeStruct((ng, dh, s), BF16),
        jax.ShapeDtypeStruct((s, NSA_HEADS * dh), BF16),
    ]
    return pl.pallas_call(
        _proj_attn_kernel,
        grid=(s // tm, 3 + nq),
        in_specs=[
            pl.BlockSpec((tm, d), lambda i, j: (i, 0)),
            pl.BlockSpec((1, d), lambda i, j: (0, 0)),
            pl.BlockSpec((d, tn), lambda i, j: (0, j)),
        ],
        out_specs=[
            pl.BlockSpec((2, ng, tm, dh), lambda i, j: (0, 0, i, 0)),
            pl.BlockSpec((ng, tm, dh), lambda i, j: (0, i, 0)),
            pl.BlockSpec((ng, tm, 2 * dh), lambda i, j: (0, i, 0)),
            pl.BlockSpec((ng, dh + BF16_ROWS, tm), lambda i, j: (0, 0, i)),
            pl.BlockSpec((ng, dh, tm), lambda i, j: (0, 0, i)),
            pl.BlockSpec((tm, tn), lambda i, j: (i, jnp.maximum(j - 3, 0))),
        ],
        out_shape=shapes,
        scratch_shapes=[pltpu.VMEM((tm, d), BF16)],
        compiler_params=_params("parallel", "arbitrary"),
        name="proj_attn",
    )(x, g.reshape(1, d), w)


def _proj_gate_kernel(x_ref, g_ref, w_ref, o_ref, gt_ref, h_ref):
    j = pl.program_id(1)
    last = pl.num_programs(1) - 1

    @pl.when(j == 0)
    def _():
        h_ref[...] = _rms(x_ref[...], g_ref[...]).astype(BF16)

    res = _dot(h_ref[...], w_ref[...])

    @pl.when(j < last)
    def _():
        o_ref[...] = res

    @pl.when(j == last)
    def _():
        gt_ref[...] = res[:, :LANES].T


def _proj_gate(x, g, w, tm):
    s, d = x.shape
    tn = PROJ_TN
    n = w.shape[1] - tn
    return pl.pallas_call(
        _proj_gate_kernel,
        grid=(s // tm, n // tn + 1),
        in_specs=[
            pl.BlockSpec((tm, d), lambda i, j: (i, 0)),
            pl.BlockSpec((1, d), lambda i, j: (0, 0)),
            pl.BlockSpec((d, tn), lambda i, j: (0, j)),
        ],
        out_specs=[
            pl.BlockSpec((tm, tn), lambda i, j: (i, jnp.minimum(j, n // tn - 1))),
            pl.BlockSpec((LANES, tm), lambda i, j: (0, i)),
        ],
        out_shape=[jax.ShapeDtypeStruct((s, n), F32), jax.ShapeDtypeStruct((LANES, s), F32)],
        scratch_shapes=[pltpu.VMEM((tm, d), BF16)],
        compiler_params=_params("parallel", "arbitrary"),
        name="proj_gate",
    )(x, g.reshape(1, d), w)


def _compress_kernel(r_ref, w1_ref, pe_ref, b1_ref, w2_ref, o_ref, ot_ref):
    half = r_ref.shape[-1]
    r = r_ref[0, 0]
    bias = _dot(pe_ref[0].astype(BF16), w1_ref[0]) + b1_ref[0]
    top = _dot(r, w1_ref[0, :half, :])
    bot = _dot(r, w1_ref[0, half:, :])
    ncp = r.shape[0]
    hid = top + pltpu.roll(bot, ncp - 1, 0) + bias
    out = _dot(jax.nn.gelu(hid).astype(BF16), w2_ref[0])
    o_ref[0, 0] = out.astype(o_ref.dtype)
    dh = out.shape[1]
    out_t = jnp.concatenate([out, jnp.zeros((ncp, LANES - dh), F32)], axis=1).T
    ot_ref[0, 0] = out_t[:dh].astype(ot_ref.dtype)


def _compress(r, w1, pe, b1, w2):
    _, g, ncp, half = r.shape
    hidden = w1.shape[-1]
    dh = w2.shape[-1]
    return pl.pallas_call(
        _compress_kernel,
        grid=(2, g),
        in_specs=[
            pl.BlockSpec((1, 1, ncp, half), lambda s, gg: (s, gg, 0, 0)),
            pl.BlockSpec((1, 2 * half, hidden), lambda s, gg: (s, 0, 0)),
            pl.BlockSpec((1, 1, 2 * half), lambda s, gg: (s, 0, 0)),
            pl.BlockSpec((1, 1, hidden), lambda s, gg: (s, 0, 0)),
            pl.BlockSpec((1, hidden, dh), lambda s, gg: (s, 0, 0)),
        ],
        out_specs=[pl.BlockSpec((1, 1, ncp, dh), lambda s, gg: (s, gg, 0, 0)),
                   pl.BlockSpec((1, 1, dh, ncp), lambda s, gg: (s, gg, 0, 0))],
        out_shape=[jax.ShapeDtypeStruct((2, g, ncp, dh), BF16), jax.ShapeDtypeStruct((2, g, dh, ncp), BF16)],
        compiler_params=_params("parallel", "parallel"),
        name="compress",
    )(r, w1, pe, b1, w2)


def _softmax2_cols(s, bias, valid):
    s = s + bias
    e = jnp.exp2(s - jnp.max(s, axis=0, keepdims=True))
    l = jnp.sum(e, axis=0, keepdims=True)
    return e * jnp.where(valid, 1.0 / l, 0.0)


def _nsa_kernel(q_ref, gt_ref, kc_ref, vct_ref, covt_ref, ks_ref, vst_ref, kw_ref, vwt_ref, o_ref,
                qa_ref, sel_ref, s0_ref, s1_ref, acc_ref, part_ref):
    s_refs = (s0_ref, s1_ref)
    g = pl.program_id(0)
    i = pl.program_id(1)
    qb = QUERY_BLOCK
    start = i * qb
    nb = sel_ref.shape[0]
    ncp = kc_ref.shape[1]
    s_len = ks_ref.shape[1]
    hq = NSA_HPG * qb
    t_row = start + lax.broadcasted_iota(jnp.int32, (1, qb), 1)

    qt = (q_ref[...].astype(F32) * (NSA_DIM ** -0.5 * LOG2E)).T
    q_t = jnp.concatenate([qt[h * NSA_DIM:(h + 1) * NSA_DIM] for h in range(NSA_HPG)], axis=1).astype(BF16)
    qa_ref[:NSA_DIM, :] = q_t
    qa_ref[NSA_DIM:, :] = jnp.zeros((qa_ref.shape[0] - NSA_DIM, hq), BF16)

    jt = t_row // SEL_BLOCK
    picked = -2.0

    def gate(branch):
        rows = [gt_ref[pl.ds(g * (NSA_HPG * 3) + h * 3 + branch, 1), :] for h in range(NSA_HPG)]
        return jax.nn.sigmoid(jnp.concatenate(rows, axis=1))

    def window():
        wk = WINDOW + qb
        k0w = pl.multiple_of(jnp.maximum(start - WINDOW, 0), qb)
        sw = _dot(kw_ref[0, pl.ds(k0w, wk), :], q_t)
        kpos = k0w + lax.broadcasted_iota(jnp.int32, (wk, qb), 0)
        wbias = jnp.where((kpos <= t_row) & (kpos > t_row - WINDOW), 0.0, MASK_VALUE)
        p_win = jnp.concatenate(
            [_softmax2_cols(sw[:, h * qb:(h + 1) * qb], wbias, True).astype(BF16) for h in range(NSA_HPG)], axis=1)
        return _dot(vwt_ref[0, :, pl.ds(k0w, wk)], p_win)

    def compressed_and_select(nck, nbk):
        sc = _dot(kc_ref[0, :nck, :], q_t)
        cmask = lax.broadcasted_iota(jnp.int32, (nck, qb), 0) * CMP_STRIDE + (CMP_BLOCK - 1) <= t_row
        cbias = jnp.where(cmask, 0.0, MASK_VALUE)
        cvalid = t_row >= CMP_BLOCK - 1
        p_heads = [_softmax2_cols(sc[:, h * qb:(h + 1) * qb], cbias, cvalid) for h in range(NSA_HPG)]
        psum = p_heads[0]
        for p in p_heads[1:]:
            psum = psum + p
        o_cmp = _dot(vct_ref[0, :, :nck], jnp.concatenate([p.astype(BF16) for p in p_heads], axis=1))
        part_ref[...] = gate(0) * o_cmp + gate(2) * window()
        p_hi = psum.astype(BF16)
        p_lo = (psum - p_hi.astype(F32)).astype(BF16)
        cov = covt_ref[:nbk, :nck]
        imp = _dot(cov, p_hi) + _dot(cov, p_lo)

        blk = lax.broadcasted_iota(jnp.int32, (nbk, qb), 0)
        forced = (blk == 0) | (blk == jt) | (blk == jt - 1)
        score = jnp.where(forced, picked, jnp.where(blk <= jt, imp, -1.0))
        blk_f = blk.astype(F32)

        def pick(_, score):
            mx = jnp.max(score, axis=0, keepdims=True)
            idx = jnp.min(jnp.where(score == mx, blk_f, float(nbk)), axis=0, keepdims=True)
            return jnp.where(blk_f == idx, picked, score)

        score = lax.fori_loop(0, min(SEL_TOPK, nbk) - N_FORCED, pick, score)
        sel_ref[:nbk, :] = jnp.where((score == picked) & (blk <= jt), 0.0, MASK_VALUE)

    bucket_ok = s_len % (NSA_BUCKETS * SEL_TILE) == 0 and nb // NSA_BUCKETS >= SEL_TOPK
    n_bucket = NSA_BUCKETS if bucket_ok else 1
    bucket = (start + qb - 1) // (s_len // n_bucket)
    for b in range(n_bucket):
        pl.when(bucket == b)(functools.partial(
            compressed_and_select, (b + 1) * (ncp // n_bucket), (b + 1) * (nb // n_bucket)))

    tk = SEL_TILE

    def scores(slot, kt):
        k0 = pl.multiple_of(kt * tk, tk)
        b8 = sel_ref[pl.ds(pl.multiple_of(kt * SEL_BPT, SEL_BPT), SEL_BPT), :]
        b16 = jnp.concatenate([b8, jnp.zeros((BF16_ROWS - SEL_BPT, qb), F32)], axis=0).astype(BF16)
        qa_ref[NSA_DIM:NSA_DIM + BF16_ROWS, :] = jnp.concatenate([b16] * NSA_HPG, axis=1)
        s = _dot(ks_ref[0, pl.ds(k0, tk), :], qa_ref[...])
        s_refs[slot][...] = s
        return jnp.max(s, axis=0, keepdims=True)

    def update(slot, kt, mt, m, causal):
        k0 = pl.multiple_of(kt * tk, tk)
        s = s_refs[slot][...]
        if causal:
            kpos = k0 + lax.broadcasted_iota(jnp.int32, (tk, qb), 0)
            s = s + jnp.concatenate([jnp.where(kpos <= t_row, 0.0, MASK_VALUE)] * NSA_HPG, axis=1)
            mt = jnp.max(s, axis=0, keepdims=True)
        m_new = jnp.maximum(m, mt)
        p = jnp.exp2(s - m_new).astype(BF16)
        acc_ref[...] = jnp.exp2(m - m_new) * acc_ref[...] + _dot(vst_ref[0, :, pl.ds(k0, tk)], p)
        return m_new

    def fused(cur, kt_cur, mt_cur, m, nxt, kt_nxt):
        k0c = pl.multiple_of(kt_cur * tk, tk)
        k0n = pl.multiple_of(kt_nxt * tk, tk)
        b8 = sel_ref[pl.ds(pl.multiple_of(kt_nxt * SEL_BPT, SEL_BPT), SEL_BPT), :]
        b16 = jnp.concatenate([b8, jnp.zeros((BF16_ROWS - SEL_BPT, qb), F32)], axis=0).astype(BF16)
        qa_ref[NSA_DIM:NSA_DIM + BF16_ROWS, :] = jnp.concatenate([b16] * NSA_HPG, axis=1)
        m_new = jnp.maximum(m, mt_cur)
        mt_nxt, pv = None, None
        for c in range(tk // SEL_CHUNK):
            rows = pl.ds(c * SEL_CHUNK, SEL_CHUNK)
            s_n = _dot(ks_ref[0, pl.ds(k0n + c * SEL_CHUNK, SEL_CHUNK), :], qa_ref[...])
            s_refs[nxt][rows, :] = s_n
            mt_c = jnp.max(s_n, axis=0, keepdims=True)
            mt_nxt = mt_c if mt_nxt is None else jnp.maximum(mt_nxt, mt_c)
            p = jnp.exp2(s_refs[cur][rows, :] - m_new).astype(BF16)
            pv_c = _dot(vst_ref[0, :, pl.ds(k0c + c * SEL_CHUNK, SEL_CHUNK)], p)
            pv = pv_c if pv is None else pv + pv_c
        acc_ref[...] = jnp.exp2(m - m_new) * acc_ref[...] + pv
        return mt_nxt, m_new

    def run(first, n_tiles, carry):
        mt, m = carry
        for k in range(n_tiles):
            mt, m = fused(k % 2, first + k, mt, m, (k + 1) % 2, first + k + 1)
        return mt, m

    acc_ref[...] = jnp.zeros(acc_ref.shape, F32)
    last = (start + qb - 1) // tk
    unroll = SEL_UNROLL
    carry = (scores(0, 0), jnp.full((1, hq), MASK_VALUE, F32))
    n_long = last // unroll
    carry = lax.fori_loop(0, n_long, lambda it, c: run(it * unroll, unroll, c), carry)
    n_short = (last - n_long * unroll) // 2
    mt_a, m = lax.fori_loop(0, n_short, lambda it, c: run(n_long * unroll + 2 * it, 2, c), carry)

    @pl.when(last % 2 == 0)
    def _():
        update(0, last, mt_a, m, True)

    @pl.when(last % 2 == 1)
    def _():
        mt_b = scores(1, last)
        update(1, last, mt_b, update(0, last - 1, mt_a, m, False), True)

    o_sel = acc_ref[:NSA_DIM, :] * (1.0 / acc_ref[NSA_DIM:NSA_DIM + 1, :])

    out_t = part_ref[...] + gate(1) * o_sel
    outs = [out_t[:, h * qb:(h + 1) * qb] for h in range(NSA_HPG)]
    o_ref[...] = jnp.concatenate(outs, axis=0).T.astype(o_ref.dtype)


def _nsa(q, gates_t, kc, vc_t, cov_t, ks_aug, vs_aug_t, kw, vw_t):
    s = q.shape[0]
    assert s % (2 * SEL_TILE) == 0 and s >= WINDOW + QUERY_BLOCK
    gdim = NSA_HPG * NSA_DIM
    ncp = kc.shape[1]
    nb = s // SEL_BLOCK
    ka = ks_aug.shape[-1]
    va = vs_aug_t.shape[1]
    hq = NSA_HPG * QUERY_BLOCK
    per_group = lambda shape: pl.BlockSpec((1,) + shape, lambda g, i: (g, 0, 0))
    return pl.pallas_call(
        _nsa_kernel,
        grid=(NSA_GROUPS, s // QUERY_BLOCK),
        in_specs=[
            pl.BlockSpec((QUERY_BLOCK, gdim), lambda g, i: (i, g)),
            pl.BlockSpec((NSA_HEADS * 3, QUERY_BLOCK), lambda g, i: (0, i)),
            per_group((ncp, NSA_DIM)),
            per_group((NSA_DIM, ncp)),
            pl.BlockSpec((nb, ncp), lambda g, i: (0, 0)),
            per_group((s, ka)),
            per_group((va, s)),
            per_group((s, NSA_DIM)),
            per_group((NSA_DIM, s)),
        ],
        out_specs=pl.BlockSpec((QUERY_BLOCK, gdim), lambda g, i: (i, g)),
        out_shape=jax.ShapeDtypeStruct((s, NSA_HEADS * NSA_DIM), BF16),
        scratch_shapes=[pltpu.VMEM((ka, hq), BF16), pltpu.VMEM((nb, QUERY_BLOCK), F32),
                        pltpu.VMEM((SEL_TILE, hq), F32), pltpu.VMEM((SEL_TILE, hq), F32),
                        pltpu.VMEM((va, hq), F32),
                        pltpu.VMEM((NSA_DIM, hq), F32)],
        compiler_params=_params("parallel", "arbitrary"),
        name="nsa",
    )(q, gates_t, kc, vc_t, cov_t, ks_aug, vs_aug_t, kw, vw_t)


def _coverage_t(nc, ncp, nb):
    cs = np.arange(nc)[None, :] * CMP_STRIDE
    ss = np.arange(nb)[:, None] * SEL_BLOCK
    cov = np.clip(np.minimum(cs + CMP_BLOCK, ss + SEL_BLOCK) - np.maximum(cs, ss), 0, None) / CMP_BLOCK
    return jnp.asarray(np.pad(cov, ((0, 0), (0, ncp - nc))), dtype=BF16)


def _sgu_kernel(u_ref, v_ref, lng_ref, lnb_ref, ws_ref, bs_ref, o_ref):
    c = SGU_CHUNK
    tm = u_ref.shape[0]
    v = jax.nn.gelu(v_ref[...])
    mu = jnp.mean(v, axis=-1, keepdims=True)
    var = jnp.mean(jnp.square(v - mu), axis=-1, keepdims=True)
    vn = ((v - mu) * lax.rsqrt(var + NORM_EPS) * lng_ref[...] + lnb_ref[...]).astype(BF16)
    u = jax.nn.gelu(u_ref[...])
    tri = lax.broadcasted_iota(jnp.int32, (c, c), 0) >= lax.broadcasted_iota(jnp.int32, (c, c), 1)
    for g in range(SGU_GROUPS):
        w = jnp.where(tri, ws_ref[g], 0.0).astype(BF16)
        cols = slice(g * c, (g + 1) * c)
        rhs = jnp.concatenate([vn[k * c:(k + 1) * c, cols] for k in range(tm // c)], axis=1)
        mixed = _dot(w, rhs)
        for k in range(tm // c):
            rows = slice(k * c, (k + 1) * c)
            o_ref[rows, cols] = (u[rows, cols] * (mixed[:, rows] + bs_ref[g])).astype(o_ref.dtype)


def _sgu(proj, lng, lnb, ws, bs, tm):
    s = proj.shape[0]
    w = lng.shape[0]
    c = SGU_CHUNK
    bs_b = jnp.broadcast_to(bs[:, :, None], (SGU_GROUPS, c, c))
    return pl.pallas_call(
        _sgu_kernel,
        grid=(s // tm,),
        in_specs=[
            pl.BlockSpec((tm, w), lambda i: (i, 0)),
            pl.BlockSpec((tm, w), lambda i: (i, 1)),
            pl.BlockSpec((1, w), lambda i: (0, 0)),
            pl.BlockSpec((1, w), lambda i: (0, 0)),
            pl.BlockSpec((SGU_GROUPS, c, c), lambda i: (0, 0, 0)),
            pl.BlockSpec((SGU_GROUPS, c, c), lambda i: (0, 0, 0)),
        ],
        out_specs=pl.BlockSpec((tm, w), lambda i: (i, 0)),
        out_shape=jax.ShapeDtypeStruct((s, w), BF16),
        compiler_params=_params("parallel"),
        name="sgu",
    )(proj, proj, lng.reshape(1, w), lnb.reshape(1, w), ws, bs_b)


def _merge_kernel(oa_ref, ob_ref, ga_ref, gb_ref, x_ref, pa_ref, pb_ref, wo_ref, o_ref):
    a = _dot(oa_ref[...], pa_ref[...])
    b = _dot(ob_ref[...], pb_ref[...])
    merged = jax.nn.sigmoid(ga_ref[...]) * a + jax.nn.sigmoid(gb_ref[...]) * b
    o_ref[...] = x_ref[...] + _dot(merged.astype(BF16), wo_ref[...])


def _merge(o_a, o_b, proj, x, p_a, p_b, w_o, tm, gate_block):
    s, d = x.shape
    wa = o_a.shape[1]
    wb = o_b.shape[1]
    resident = lambda shape: pl.BlockSpec(shape, lambda i: (0, 0), pipeline_mode=pl.Buffered(1))
    return pl.pallas_call(
        _merge_kernel,
        grid=(s // tm,),
        in_specs=[
            pl.BlockSpec((tm, wa), lambda i: (i, 0)),
            pl.BlockSpec((tm, wb), lambda i: (i, 0)),
            pl.BlockSpec((tm, d), lambda i: (i, gate_block)),
            pl.BlockSpec((tm, d), lambda i: (i, gate_block + 1)),
            pl.BlockSpec((tm, d), lambda i: (i, 0)),
            resident((wa, d)),
            resident((wb, d)),
            resident((d, d)),
        ],
        out_specs=pl.BlockSpec((tm, d), lambda i: (i, 0)),
        out_shape=jax.ShapeDtypeStruct((s, d), F32),
        compiler_params=_params("parallel"),
        name="merge",
    )(o_a, o_b, proj, proj, x, p_a, p_b, w_o)


def _cross_kernel(x_ref, g_ref, wq_ref, mkt_ref, mv_ref, wo_ref, o_ref):
    x = x_ref[...]
    h = _rms(x, g_ref[...]).astype(BF16)
    mq = (_dot(h, wq_ref[...]) * (MEM_DIM ** -0.5)).astype(BF16)
    outs = []
    for hh in range(MEM_HEADS):
        s = _dot(mq[:, hh * MEM_DIM:(hh + 1) * MEM_DIM], mkt_ref[hh])
        e = jnp.exp(s - jnp.max(s, axis=-1, keepdims=True))
        p = e / jnp.sum(e, axis=-1, keepdims=True)
        outs.append(_dot(p.astype(BF16), mv_ref[hh]).astype(BF16))
    o_ref[...] = x + _dot(jnp.concatenate(outs, axis=1), wo_ref[...])


def _cross(x, g, w_q, mk_t, mv, w_o, tm):
    s, d = x.shape
    mw = w_q.shape[1]
    m = mv.shape[1]
    return pl.pallas_call(
        _cross_kernel,
        grid=(s // tm,),
        in_specs=[
            pl.BlockSpec((tm, d), lambda i: (i, 0)),
            pl.BlockSpec((1, d), lambda i: (0, 0)),
            pl.BlockSpec((d, mw), lambda i: (0, 0)),
            pl.BlockSpec((MEM_HEADS, MEM_DIM, m), lambda i: (0, 0, 0)),
            pl.BlockSpec((MEM_HEADS, m, MEM_DIM), lambda i: (0, 0, 0)),
            pl.BlockSpec((mw, d), lambda i: (0, 0)),
        ],
        out_specs=pl.BlockSpec((tm, d), lambda i: (i, 0)),
        out_shape=jax.ShapeDtypeStruct((s, d), F32),
        compiler_params=_params("parallel"),
        name="cross",
    )(x, g.reshape(1, d), w_q, mk_t, mv, w_o)


def _ffn_kernel(x_ref, g_ref, wg_ref, wu_ref, wo_ref, gf_ref, o_ref, h_ref, acc_ref, *, final_norm):
    j = pl.program_id(1)

    @pl.when(j == 0)
    def _():
        h_ref[...] = _rms(x_ref[...], g_ref[...]).astype(BF16)
        acc_ref[...] = jnp.zeros_like(acc_ref)

    h = h_ref[...]
    act = (jax.nn.silu(_dot(h, wg_ref[...])) * _dot(h, wu_ref[...])).astype(BF16)
    acc_ref[...] += _dot(act, wo_ref[...])

    @pl.when(j == pl.num_programs(1) - 1)
    def _():
        y = x_ref[...] + acc_ref[...]
        o_ref[...] = _rms(y, gf_ref[...]) if final_norm else y


def _ffn(x, g, w_in, w_out, g_final, final_norm, tm, th):
    s, d = x.shape
    hidden = w_out.shape[0]
    nh = hidden // th
    return pl.pallas_call(
        functools.partial(_ffn_kernel, final_norm=final_norm),
        grid=(s // tm, nh),
        in_specs=[
            pl.BlockSpec((tm, d), lambda i, j: (i, 0)),
            pl.BlockSpec((1, d), lambda i, j: (0, 0)),
            pl.BlockSpec((d, th), lambda i, j: (0, j)),
            pl.BlockSpec((d, th), lambda i, j: (0, j + nh)),
            pl.BlockSpec((th, d), lambda i, j: (j, 0)),
            pl.BlockSpec((1, d), lambda i, j: (0, 0)),
        ],
        out_specs=pl.BlockSpec((tm, d), lambda i, j: (i, 0)),
        out_shape=jax.ShapeDtypeStruct((s, d), F32),
        scratch_shapes=[pltpu.VMEM((tm, d), BF16), pltpu.VMEM((tm, d), F32)],
        compiler_params=_params("parallel", "arbitrary"),
        name="ffn",
    )(x, g.reshape(1, d), w_in, w_in, w_out, g_final.reshape(1, d))


def _layer(x, mem, norm_mix_g, w_in, cmp_pe_k, cmp_k_w1, cmp_k_b1, cmp_k_w2, cmp_pe_v, cmp_v_w1, cmp_v_b1, cmp_v_w2,
           sgu_ln_g, sgu_ln_b, sgu_ws, sgu_b, w_proj_a, w_proj_b, w_mix_out, norm_cross_g, norm_mem_g,
           w_mq, w_mkv, w_mo, norm_ffn_g):
    s, d = x.shape
    qw = NSA_HEADS * NSA_DIM
    kvw = NSA_GROUPS * NSA_DIM
    sguw = sgu_ln_g.shape[0]
    ngate = NSA_HEADS * 3
    o_kv = qw
    o_gate = o_kv + N_KV_STREAMS * kvw
    o_u = o_gate + ngate

    w_bf = w_in.astype(BF16)
    stream = lambda st: w_bf[:, o_kv + st * kvw:o_kv + (st + 1) * kvw]
    w_a = jnp.concatenate([stream(st) for st in (0, 1, 4, 2, 3, 5)] + [w_bf[:, :qw]], axis=1)
    w_b = jnp.pad(w_bf[:, o_u:], ((0, 0), (0, PROJ_TN)))
    w_b = lax.dynamic_update_slice(w_b, w_bf[:, o_gate:o_u], (0, w_in.shape[1] - o_u))
    tm_proj = min(1024, s)
    nat, kw, ks_aug, vs_aug_t, vw_t, q = _proj_attn(x, norm_mix_g, w_a, tm_proj)
    proj_b, gates_t = _proj_gate(x, norm_mix_g, w_b, tm_proj)

    ncp = s // CMP_STRIDE
    nc = (s - CMP_BLOCK) // CMP_STRIDE + 1
    r = nat.reshape(2, NSA_GROUPS, ncp, CMP_STRIDE * NSA_DIM)
    w1 = jnp.stack([cmp_k_w1, cmp_v_w1]).astype(BF16)
    pe = jnp.stack([cmp_pe_k.reshape(1, -1), cmp_pe_v.reshape(1, -1)])
    b1 = jnp.stack([cmp_k_b1.reshape(1, -1), cmp_v_b1.reshape(1, -1)])
    w2 = jnp.stack([cmp_k_w2, cmp_v_w2]).astype(BF16)
    cmp, cmp_t = _compress(r, w1, pe, b1, w2)
    cov_t = _coverage_t(nc, ncp, s // SEL_BLOCK)
    o_a = _nsa(q, gates_t, cmp[0], cmp_t[1], cov_t, ks_aug, vs_aug_t, kw, vw_t)

    o_b = _sgu(proj_b, sgu_ln_g, sgu_ln_b, sgu_ws, sgu_b, min(512, s))
    x = _merge(o_a, o_b, proj_b, x, w_proj_a.astype(BF16), w_proj_b.astype(BF16), w_mix_out.astype(BF16),
               min(256, s), (2 * sguw) // d)

    m = mem.shape[0]
    mw = MEM_HEADS * MEM_DIM
    mkv = _norm_matmul(mem, norm_mem_g, w_mkv.astype(BF16), F32, m, mw)
    mk_t = mkv[:, :mw].reshape(m, MEM_HEADS, MEM_DIM).transpose(1, 2, 0).astype(BF16)
    mv = mkv[:, mw:].reshape(m, MEM_HEADS, MEM_DIM).transpose(1, 0, 2).astype(BF16)
    x = _cross(x, norm_cross_g, w_mq.astype(BF16), mk_t, mv, w_mo.astype(BF16), min(512, s))
    return x


def kernel(x, mem, norm_mix_g, w_in, cmp_pe_k, cmp_k_w1, cmp_k_b1, cmp_k_w2, cmp_pe_v, cmp_v_w1, cmp_v_b1, cmp_v_w2, sgu_ln_g, sgu_ln_b, sgu_ws, sgu_b, w_proj_a, w_proj_b, w_mix_out, norm_cross_g, norm_mem_g, w_mq, w_mkv, w_mo, norm_ffn_g, w_ffn_in, w_ffn_out, norm_final_g):
    b, s, d = x.shape
    depth = w_in.shape[0]
    outs = []
    for bi in range(b):
        xb = x[bi]
        for l in range(depth):
            last = l == depth - 1
            xb = _layer(xb, mem[bi], norm_mix_g[l], w_in[l], cmp_pe_k[l], cmp_k_w1[l], cmp_k_b1[l], cmp_k_w2[l],
                        cmp_pe_v[l], cmp_v_w1[l], cmp_v_b1[l], cmp_v_w2[l], sgu_ln_g[l], sgu_ln_b[l], sgu_ws[l],
                        sgu_b[l], w_proj_a[l], w_proj_b[l], w_mix_out[l], norm_cross_g[l], norm_mem_g[l],
                        w_mq[l], w_mkv[l], w_mo[l], norm_ffn_g[l])
            xb = _ffn(xb, norm_ffn_g[l], w_ffn_in[l].astype(BF16), w_ffn_out[l].astype(BF16), norm_final_g, last,
                      min(512, s), 512)
        outs.append(xb)
    return jnp.stack(outs)
```

```python
import functools
import math

import jax
import jax.numpy as jnp
from jax import lax
from jax.experimental import pallas as pl
from jax.experimental.pallas import tpu as pltpu

F32 = jnp.float32
BF16 = jnp.bfloat16

NORM_EPS = 1e-6
MASK_VALUE = -1e30
N_FORCED = 3
LOG2E = math.log2(math.e)

NSA_HEADS = 16
NSA_GROUPS = 4
NSA_HPG = NSA_HEADS // NSA_GROUPS
NSA_DIM = 64
N_KV_STREAMS = 6
CMP_BLOCK = 32
CMP_STRIDE = 16
SEL_BLOCK = 64
SEL_TOPK = 16
WINDOW = 512
QUERY_BLOCK = 128
SGU_GROUPS = 8
SGU_CHUNK = 128
MEM_HEADS = 4
MEM_DIM = 128

LANES = 128
SUBLANES = 8
COV_RATIO = SEL_BLOCK // CMP_STRIDE
COV_LEAD = CMP_BLOCK // CMP_STRIDE - 1
COV_BAND = tuple(
    max(min((k - COV_LEAD) * CMP_STRIDE + CMP_BLOCK, SEL_BLOCK) - max((k - COV_LEAD) * CMP_STRIDE, 0), 0) / CMP_BLOCK
    for k in range(COV_RATIO + COV_LEAD))
assert SEL_BLOCK % CMP_STRIDE == 0 and CMP_BLOCK % CMP_STRIDE == 0 and COV_LEAD <= SUBLANES
SEL_TILE = 512
SEL_BPT = SEL_TILE // SEL_BLOCK
SEL_CHUNK = 256
SEL_UNROLL = 8
NSA_BUCKETS = 4
BF16_ROWS = 16
VMEM_LIMIT = 56 * 1024 * 1024


def _params(*sem):
    return pltpu.CompilerParams(dimension_semantics=sem, vmem_limit_bytes=VMEM_LIMIT)


def _rms(x, g):
    return x * lax.rsqrt(jnp.mean(x * x, axis=-1, keepdims=True) + NORM_EPS) * g


def _dot(a, b):
    return jnp.dot(a, b, preferred_element_type=F32)


def _ones_row(n):
    return jnp.where(lax.broadcasted_iota(jnp.int32, (BF16_ROWS, n), 0) == 0, 1.0, 0.0).astype(BF16)


def _norm_matmul_kernel(x_ref, g_ref, w_ref, o_ref, h_ref):
    @pl.when(pl.program_id(1) == 0)
    def _():
        h_ref[...] = _rms(x_ref[...], g_ref[...]).astype(BF16)

    o_ref[...] = _dot(h_ref[...], w_ref[...]).astype(o_ref.dtype)


def _norm_matmul(x, g, w, out_dtype, tm, tn):
    s, d = x.shape
    n = w.shape[1]
    return pl.pallas_call(
        _norm_matmul_kernel,
        grid=(s // tm, n // tn),
        in_specs=[
            pl.BlockSpec((tm, d), lambda i, j: (i, 0)),
            pl.BlockSpec((1, d), lambda i, j: (0, 0)),
            pl.BlockSpec((d, tn), lambda i, j: (0, j)),
        ],
        out_specs=pl.BlockSpec((tm, tn), lambda i, j: (i, j)),
        out_shape=jax.ShapeDtypeStruct((s, n), out_dtype),
        scratch_shapes=[pltpu.VMEM((tm, d), BF16)],
        compiler_params=_params("parallel", "arbitrary"),
        name="norm_matmul",
    )(x, g.reshape(1, d), w)


PROJ_TN = 512


def _proj_attn_kernel(x_ref, g_ref, w_ref, nat_ref, kw_ref, ksa_ref, vsa_ref, vwa_ref, q_ref, h_ref):
    i = pl.program_id(0)
    j = pl.program_id(1)
    tm = x_ref.shape[0]
    dh = NSA_DIM
    ng = NSA_GROUPS

    @pl.when(j == 0)
    def _():
        h_ref[...] = _rms(x_ref[...], g_ref[...]).astype(BF16)

    res = _dot(h_ref[...], w_ref[...])

    def cols(c):
        return res[:, c * dh:(c + 1) * dh].astype(BF16)

    def cols_t(c):
        slab = res[:, (c // 2) * 2 * dh:(c // 2 + 1) * 2 * dh].T
        return slab[(c % 2) * dh:(c % 2 + 1) * dh].astype(BF16)

    @pl.when(j == 0)
    def _():
        for c in range(2 * ng):
            nat_ref[c // ng, c % ng] = cols(c)

    @pl.when(j == 1)
    def _():
        pos = i * tm + lax.broadcasted_iota(jnp.int32, (tm, dh), 0)
        lane = lax.broadcasted_iota(jnp.int32, (tm, dh), 1)
        onehot = jnp.where((pos // SEL_BLOCK) % SEL_BPT == lane, 1.0, 0.0).astype(BF16)
        for g in range(ng):
            ksa_ref[g, :, :dh] = cols(g)
            ksa_ref[g, :, dh:] = onehot
            vsa_ref[g, :dh, :] = cols_t(ng + g)
            vsa_ref[g, dh:, :] = _ones_row(tm)

    @pl.when(j == 2)
    def _():
        for g in range(ng):
            kw_ref[g] = cols(g)
            vwa_ref[g, :dh, :] = cols_t(ng + g)
            vwa_ref[g, dh:, :] = _ones_row(tm)

    @pl.when(j >= 3)
    def _():
        q_ref[...] = res.astype(BF16)


def _proj_attn(x, g, w, tm):
    s, d = x.shape
    dh, ng, tn = NSA_DIM, NSA_GROUPS, PROJ_TN
    assert 2 * ng * dh == tn and w.shape[1] == 3 * tn + NSA_HEADS * dh
    nq = NSA_HEADS * dh // tn
    nj = 3 + nq
    shapes = [
        jax.ShapeDtypeStruct((2, ng, s, dh), BF16),
        jax.ShapeDtypeStruct((ng, s, dh), BF16),
        jax.ShapeDtypeStruct((ng, s, 2 * dh), BF16),
        jax.ShapeDtypeStruct((ng, dh + BF16_ROWS, s), BF16),
        jax.ShapeDtypeStruct((ng, dh + BF16_ROWS, s), BF16),
        jax.ShapeDtypeStruct((s, NSA_HEADS * dh), BF16),
    ]
    return pl.pallas_call(
        _proj_attn_kernel,
        grid=(s // tm, nj),
        in_specs=[
            pl.BlockSpec((tm, d), lambda i, j: (i, 0)),
            pl.BlockSpec((1, d), lambda i, j: (0, 0)),
            pl.BlockSpec((d, tn), lambda i, j: (0, (j + nq) % nj)),
        ],
        out_specs=[
            pl.BlockSpec((2, ng, tm, dh), lambda i, j: (0, 0, i, 0)),
            pl.BlockSpec((ng, tm, dh), lambda i, j: (0, i, 0)),
            pl.BlockSpec((ng, tm, 2 * dh), lambda i, j: (0, i, 0)),
            pl.BlockSpec((ng, dh + BF16_ROWS, tm), lambda i, j: (0, 0, i)),
            pl.BlockSpec((ng, dh + BF16_ROWS, tm), lambda i, j: (0, 0, i)),
            pl.BlockSpec((tm, tn), lambda i, j: (i, jnp.maximum(j - 3, 0))),
        ],
        out_shape=shapes,
        scratch_shapes=[pltpu.VMEM((tm, d), BF16)],
        compiler_params=_params("parallel", "arbitrary"),
        name="proj_attn",
    )(x, g.reshape(1, d), w)


def _proj_gate_kernel(x_ref, g_ref, w_ref, o_ref, gt_ref, h_ref):
    j = pl.program_id(1)
    last = pl.num_programs(1) - 1

    @pl.when(j == 0)
    def _():
        h_ref[...] = _rms(x_ref[...], g_ref[...]).astype(BF16)

    res = _dot(h_ref[...], w_ref[...])

    @pl.when(j < last)
    def _():
        o_ref[...] = res

    @pl.when(j == last)
    def _():
        gt_ref[...] = res[:, :LANES].T


def _proj_gate(x, g, w, tm):
    s, d = x.shape
    tn = PROJ_TN
    n = w.shape[1] - tn
    return pl.pallas_call(
        _proj_gate_kernel,
        grid=(s // tm, n // tn + 1),
        in_specs=[
            pl.BlockSpec((tm, d), lambda i, j: (i, 0)),
            pl.BlockSpec((1, d), lambda i, j: (0, 0)),
            pl.BlockSpec((d, tn), lambda i, j: (0, j)),
        ],
        out_specs=[
            pl.BlockSpec((tm, tn), lambda i, j: (i, jnp.minimum(j, n // tn - 1))),
            pl.BlockSpec((LANES, tm), lambda i, j: (0, i)),
        ],
        out_shape=[jax.ShapeDtypeStruct((s, n), F32), jax.ShapeDtypeStruct((LANES, s), F32)],
        scratch_shapes=[pltpu.VMEM((tm, d), BF16)],
        compiler_params=_params("parallel", "arbitrary"),
        name="proj_gate",
    )(x, g.reshape(1, d), w)


def _compress_kernel(r_ref, w1_ref, pe_ref, b1_ref, w2_ref, o_ref, ot_ref):
    half = r_ref.shape[-1]
    r = r_ref[0, 0]
    bias = _dot(pe_ref[0].astype(BF16), w1_ref[0]) + b1_ref[0]
    top = _dot(r, w1_ref[0, :half, :])
    bot = _dot(r, w1_ref[0, half:, :])
    ncp = r.shape[0]
    hid = top + pltpu.roll(bot, ncp - 1, 0) + bias
    out = _dot(jax.nn.gelu(hid).astype(BF16), w2_ref[0])
    o_ref[0, 0] = out.astype(o_ref.dtype)
    dh = out.shape[1]
    out_t = jnp.concatenate([out, jnp.zeros((ncp, LANES - dh), F32)], axis=1).T
    ot_ref[0, 0, :dh, :] = out_t[:dh].astype(ot_ref.dtype)
    ot_ref[0, 0, dh:, :] = _ones_row(ncp)


def _compress(r, w1, pe, b1, w2):
    _, g, ncp, half = r.shape
    hidden = w1.shape[-1]
    dh = w2.shape[-1]
    return pl.pallas_call(
        _compress_kernel,
        grid=(2, g),
        in_specs=[
            pl.BlockSpec((1, 1, ncp, half), lambda s, gg: (s, gg, 0, 0)),
            pl.BlockSpec((1, 2 * half, hidden), lambda s, gg: (s, 0, 0)),
            pl.BlockSpec((1, 1, 2 * half), lambda s, gg: (s, 0, 0)),
            pl.BlockSpec((1, 1, hidden), lambda s, gg: (s, 0, 0)),
            pl.BlockSpec((1, hidden, dh), lambda s, gg: (s, 0, 0)),
        ],
        out_specs=[pl.BlockSpec((1, 1, ncp, dh), lambda s, gg: (s, gg, 0, 0)),
                   pl.BlockSpec((1, 1, dh + BF16_ROWS, ncp), lambda s, gg: (s, gg, 0, 0))],
        out_shape=[jax.ShapeDtypeStruct((2, g, ncp, dh), BF16),
                   jax.ShapeDtypeStruct((2, g, dh + BF16_ROWS, ncp), BF16)],
        compiler_params=_params("parallel", "parallel"),
        name="compress",
    )(r, w1, pe, b1, w2)


def _nsa_kernel(q_ref, gt_ref, kc_ref, vct_ref, ks_ref, vst_ref, kw_ref, vwt_ref, o_ref,
                qa_ref, sel_ref, s0_ref, s1_ref, acc_ref, part_ref, pp_ref):
    s_refs = (s0_ref, s1_ref)
    g = pl.program_id(0)
    i = pl.program_id(1)
    qb = QUERY_BLOCK
    start = i * qb
    nb = sel_ref.shape[0]
    ncp = kc_ref.shape[1]
    s_len = ks_ref.shape[1]
    hq = NSA_HPG * qb
    t_row = start + lax.broadcasted_iota(jnp.int32, (1, qb), 1)

    qt = (q_ref[...].astype(F32) * (NSA_DIM ** -0.5 * LOG2E)).T
    q_t = jnp.concatenate([qt[h * NSA_DIM:(h + 1) * NSA_DIM] for h in range(NSA_HPG)], axis=1).astype(BF16)
    qa_ref[:NSA_DIM, :] = q_t
    qa_ref[NSA_DIM:, :] = jnp.zeros((qa_ref.shape[0] - NSA_DIM, hq), BF16)

    jt = t_row // SEL_BLOCK
    picked = -2.0

    def gate(branch):
        rows = [gt_ref[pl.ds(g * (NSA_HPG * 3) + h * 3 + branch, 1), :] for h in range(NSA_HPG)]
        return jax.nn.sigmoid(jnp.concatenate(rows, axis=1))

    def exps(s, bias):
        out = []
        for h in range(NSA_HPG):
            sh = s[:, h * qb:(h + 1) * qb] + bias
            out.append(jnp.exp2(sh - jnp.max(sh, axis=0, keepdims=True)))
        return out

    def compressed_and_select(nck, nbk):
        wk = WINDOW + qb
        k0w = pl.multiple_of(jnp.maximum(start - WINDOW, 0), qb)
        sc = _dot(kc_ref[0, :nck, :], q_t)
        sw = _dot(kw_ref[0, pl.ds(k0w, wk), :], q_t)
        cmask = lax.broadcasted_iota(jnp.int32, (nck, qb), 0) * CMP_STRIDE + (CMP_BLOCK - 1) <= t_row
        e_cmp = exps(sc, jnp.where(cmask, 0.0, MASK_VALUE))
        o_cmp = _dot(vct_ref[0, :, :nck], jnp.concatenate([e.astype(BF16) for e in e_cmp], axis=1))
        kpos = k0w + lax.broadcasted_iota(jnp.int32, (wk, qb), 0)
        e_win = exps(sw, jnp.where((kpos <= t_row) & (kpos > t_row - WINDOW), 0.0, MASK_VALUE))
        o_win = _dot(vwt_ref[0, :, pl.ds(k0w, wk)], jnp.concatenate([e.astype(BF16) for e in e_win], axis=1))
        inv_cmp = jnp.where(jnp.concatenate([t_row >= CMP_BLOCK - 1] * NSA_HPG, axis=1),
                            1.0 / o_cmp[NSA_DIM:NSA_DIM + 1, :], 0.0)
        psum = e_cmp[0] * inv_cmp[:, :qb]
        for h in range(1, NSA_HPG):
            psum = psum + e_cmp[h] * inv_cmp[:, h * qb:(h + 1) * qb]
        part_ref[...] = (gate(0) * inv_cmp * o_cmp[:NSA_DIM]
                         + gate(2) * (1.0 / o_win[NSA_DIM:NSA_DIM + 1, :]) * o_win[:NSA_DIM])
        pp_ref[:SUBLANES, :] = jnp.zeros((SUBLANES, qb), F32)
        pp_ref[SUBLANES:SUBLANES + nck, :] = psum
        imp = None
        for k, w in enumerate(COV_BAND):
            tap = pp_ref[pl.ds(SUBLANES - COV_LEAD + k, nbk, stride=COV_RATIO), :]
            tap = tap if w == 1.0 else w * tap
            imp = tap if imp is None else imp + tap

        blk = lax.broadcasted_iota(jnp.int32, (nbk, qb), 0)
        forced = (blk == 0) | (blk == jt) | (blk == jt - 1)
        score = jnp.where(forced, picked, jnp.where(blk <= jt, imp, -1.0))
        blk_f = blk.astype(F32)

        def pick(_, score):
            mx = jnp.max(score, axis=0, keepdims=True)
            idx = jnp.min(jnp.where(score == mx, blk_f, float(nbk)), axis=0, keepdims=True)
            return jnp.where(blk_f == idx, picked, score)

        score = lax.fori_loop(0, min(SEL_TOPK, nbk) - N_FORCED, pick, score)
        sel_ref[:nbk, :] = jnp.where((score == picked) & (blk <= jt), 0.0, MASK_VALUE)

    bucket_ok = s_len % (NSA_BUCKETS * SEL_TILE) == 0 and nb // NSA_BUCKETS >= SEL_TOPK
    n_bucket = NSA_BUCKETS if bucket_ok else 1
    bucket = (start + qb - 1) // (s_len // n_bucket)
    for b in range(n_bucket):
        pl.when(bucket == b)(functools.partial(
            compressed_and_select, (b + 1) * (ncp // n_bucket), (b + 1) * (nb // n_bucket)))

    tk = SEL_TILE

    def scores(slot, kt):
        k0 = pl.multiple_of(kt * tk, tk)
        b8 = sel_ref[pl.ds(pl.multiple_of(kt * SEL_BPT, SEL_BPT), SEL_BPT), :]
        b16 = jnp.concatenate([b8, jnp.zeros((BF16_ROWS - SEL_BPT, qb), F32)], axis=0).astype(BF16)
        qa_ref[NSA_DIM:NSA_DIM + BF16_ROWS, :] = jnp.concatenate([b16] * NSA_HPG, axis=1)
        s = _dot(ks_ref[0, pl.ds(k0, tk), :], qa_ref[...])
        s_refs[slot][...] = s
        return jnp.max(s, axis=0, keepdims=True)

    def update(slot, kt, mt, m, causal):
        k0 = pl.multiple_of(kt * tk, tk)
        s = s_refs[slot][...]
        if causal:
            kpos = k0 + lax.broadcasted_iota(jnp.int32, (tk, qb), 0)
            s = s + jnp.concatenate([jnp.where(kpos <= t_row, 0.0, MASK_VALUE)] * NSA_HPG, axis=1)
            mt = jnp.max(s, axis=0, keepdims=True)
        m_new = jnp.maximum(m, mt)
        p = jnp.exp2(s - m_new).astype(BF16)
        acc_ref[...] = jnp.exp2(m - m_new) * acc_ref[...] + _dot(vst_ref[0, :, pl.ds(k0, tk)], p)
        return m_new

    def fused(cur, kt_cur, mt_cur, m, nxt, kt_nxt):
        k0c = pl.multiple_of(kt_cur * tk, tk)
        k0n = pl.multiple_of(kt_nxt * tk, tk)
        b8 = sel_ref[pl.ds(pl.multiple_of(kt_nxt * SEL_BPT, SEL_BPT), SEL_BPT), :]
        b16 = jnp.concatenate([b8, jnp.zeros((BF16_ROWS - SEL_BPT, qb), F32)], axis=0).astype(BF16)
        qa_ref[NSA_DIM:NSA_DIM + BF16_ROWS, :] = jnp.concatenate([b16] * NSA_HPG, axis=1)
        m_new = jnp.maximum(m, mt_cur)
        mt_nxt, pv = None, None
        for c in range(tk // SEL_CHUNK):
            rows = pl.ds(c * SEL_CHUNK, SEL_CHUNK)
            s_n = _dot(ks_ref[0, pl.ds(k0n + c * SEL_CHUNK, SEL_CHUNK), :], qa_ref[...])
            s_refs[nxt][rows, :] = s_n
            mt_c = jnp.max(s_n, axis=0, keepdims=True)
            mt_nxt = mt_c if mt_nxt is None else jnp.maximum(mt_nxt, mt_c)
            p = jnp.exp2(s_refs[cur][rows, :] - m_new).astype(BF16)
            pv_c = _dot(vst_ref[0, :, pl.ds(k0c + c * SEL_CHUNK, SEL_CHUNK)], p)
            pv = pv_c if pv is None else pv + pv_c
        acc_ref[...] = jnp.exp2(m - m_new) * acc_ref[...] + pv
        return mt_nxt, m_new

    def run(first, n_tiles, carry):
        mt, m = carry
        for k in range(n_tiles):
            mt, m = fused(k % 2, first + k, mt, m, (k + 1) % 2, first + k + 1)
        return mt, m

    acc_ref[...] = jnp.zeros(acc_ref.shape, F32)
    last = (start + qb - 1) // tk
    unroll = SEL_UNROLL
    carry = (scores(0, 0), jnp.full((1, hq), MASK_VALUE, F32))
    n_long = last // unroll
    carry = lax.fori_loop(0, n_long, lambda it, c: run(it * unroll, unroll, c), carry)
    n_short = (last - n_long * unroll) // 2
    mt_a, m = lax.fori_loop(0, n_short, lambda it, c: run(n_long * unroll + 2 * it, 2, c), carry)

    @pl.when(last % 2 == 0)
    def _():
        update(0, last, mt_a, m, True)

    @pl.when(last % 2 == 1)
    def _():
        mt_b = scores(1, last)
        update(1, last, mt_b, update(0, last - 1, mt_a, m, False), True)

    o_sel = acc_ref[:NSA_DIM, :] * (1.0 / acc_ref[NSA_DIM:NSA_DIM + 1, :])

    out_t = part_ref[...] + gate(1) * o_sel
    outs = [out_t[:, h * qb:(h + 1) * qb] for h in range(NSA_HPG)]
    o_ref[...] = jnp.concatenate(outs, axis=0).T.astype(o_ref.dtype)


def _nsa(q, gates_t, kc, vc_aug_t, ks_aug, vs_aug_t, kw, vw_aug_t):
    s = q.shape[0]
    assert s % (2 * SEL_TILE) == 0 and s >= WINDOW + QUERY_BLOCK
    gdim = NSA_HPG * NSA_DIM
    ncp = kc.shape[1]
    assert ncp * CMP_STRIDE == s and s // SEL_BLOCK * COV_RATIO == ncp
    nb = s // SEL_BLOCK
    ka = ks_aug.shape[-1]
    va = vs_aug_t.shape[1]
    hq = NSA_HPG * QUERY_BLOCK
    per_group = lambda shape: pl.BlockSpec((1,) + shape, lambda g, i: (g, 0, 0))
    return pl.pallas_call(
        _nsa_kernel,
        grid=(NSA_GROUPS, s // QUERY_BLOCK),
        in_specs=[
            pl.BlockSpec((QUERY_BLOCK, gdim), lambda g, i: (i, g)),
            pl.BlockSpec((NSA_HEADS * 3, QUERY_BLOCK), lambda g, i: (0, i)),
            per_group((ncp, NSA_DIM)),
            per_group((va, ncp)),
            per_group((s, ka)),
            per_group((va, s)),
            per_group((s, NSA_DIM)),
            per_group((va, s)),
        ],
        out_specs=pl.BlockSpec((QUERY_BLOCK, gdim), lambda g, i: (i, g)),
        out_shape=jax.ShapeDtypeStruct((s, NSA_HEADS * NSA_DIM), BF16),
        scratch_shapes=[pltpu.VMEM((ka, hq), BF16), pltpu.VMEM((nb, QUERY_BLOCK), F32),
                        pltpu.VMEM((SEL_TILE, hq), F32), pltpu.VMEM((SEL_TILE, hq), F32),
                        pltpu.VMEM((va, hq), F32),
                        pltpu.VMEM((NSA_DIM, hq), F32),
                        pltpu.VMEM((SUBLANES + ncp, QUERY_BLOCK), F32)],
        compiler_params=_params("parallel", "arbitrary"),
        name="nsa",
    )(q, gates_t, kc, vc_aug_t, ks_aug, vs_aug_t, kw, vw_aug_t)


def _sgu_kernel(u_ref, v_ref, lng_ref, lnb_ref, ws_ref, bs_ref, o_ref):
    c = SGU_CHUNK
    tm = u_ref.shape[0]
    v = jax.nn.gelu(v_ref[...])
    mu = jnp.mean(v, axis=-1, keepdims=True)
    var = jnp.mean(jnp.square(v - mu), axis=-1, keepdims=True)
    vn = ((v - mu) * lax.rsqrt(var + NORM_EPS) * lng_ref[...] + lnb_ref[...]).astype(BF16)
    u = jax.nn.gelu(u_ref[...])
    tri = lax.broadcasted_iota(jnp.int32, (c, c), 0) >= lax.broadcasted_iota(jnp.int32, (c, c), 1)
    for g in range(SGU_GROUPS):
        w = jnp.where(tri, ws_ref[g], 0.0).astype(BF16)
        cols = slice(g * c, (g + 1) * c)
        rhs = jnp.concatenate([vn[k * c:(k + 1) * c, cols] for k in range(tm // c)], axis=1)
        mixed = _dot(w, rhs)
        for k in range(tm // c):
            rows = slice(k * c, (k + 1) * c)
            o_ref[rows, cols] = (u[rows, cols] * (mixed[:, rows] + bs_ref[g])).astype(o_ref.dtype)


def _sgu(proj, lng, lnb, ws, bs, tm):
    s = proj.shape[0]
    w = lng.shape[0]
    c = SGU_CHUNK
    bs_b = jnp.broadcast_to(bs[:, :, None], (SGU_GROUPS, c, c))
    return pl.pallas_call(
        _sgu_kernel,
        grid=(s // tm,),
        in_specs=[
            pl.BlockSpec((tm, w), lambda i: (i, 0)),
            pl.BlockSpec((tm, w), lambda i: (i, 1)),
            pl.BlockSpec((1, w), lambda i: (0, 0)),
            pl.BlockSpec((1, w), lambda i: (0, 0)),
            pl.BlockSpec((SGU_GROUPS, c, c), lambda i: (0, 0, 0)),
            pl.BlockSpec((SGU_GROUPS, c, c), lambda i: (0, 0, 0)),
        ],
        out_specs=pl.BlockSpec((tm, w), lambda i: (i, 0)),
        out_shape=jax.ShapeDtypeStruct((s, w), BF16),
        compiler_params=_params("parallel"),
        name="sgu",
    )(proj, proj, lng.reshape(1, w), lnb.reshape(1, w), ws, bs_b)


def _merge_kernel(oa_ref, ob_ref, ga_ref, gb_ref, x_ref, pa_ref, pb_ref, wo_ref, o_ref):
    a = _dot(oa_ref[...], pa_ref[...])
    b = _dot(ob_ref[...], pb_ref[...])
    merged = jax.nn.sigmoid(ga_ref[...]) * a + jax.nn.sigmoid(gb_ref[...]) * b
    o_ref[...] = x_ref[...] + _dot(merged.astype(BF16), wo_ref[...])


def _merge(o_a, o_b, proj, x, p_a, p_b, w_o, tm, gate_block):
    s, d = x.shape
    wa = o_a.shape[1]
    wb = o_b.shape[1]
    resident = lambda shape: pl.BlockSpec(shape, lambda i: (0, 0), pipeline_mode=pl.Buffered(1))
    return pl.pallas_call(
        _merge_kernel,
        grid=(s // tm,),
        in_specs=[
            pl.BlockSpec((tm, wa), lambda i: (i, 0)),
            pl.BlockSpec((tm, wb), lambda i: (i, 0)),
            pl.BlockSpec((tm, d), lambda i: (i, gate_block)),
            pl.BlockSpec((tm, d), lambda i: (i, gate_block + 1)),
            pl.BlockSpec((tm, d), lambda i: (i, 0)),
            resident((wa, d)),
            resident((wb, d)),
            resident((d, d)),
        ],
        out_specs=pl.BlockSpec((tm, d), lambda i: (i, 0)),
        out_shape=jax.ShapeDtypeStruct((s, d), F32),
        compiler_params=_params("parallel"),
        name="merge",
    )(o_a, o_b, proj, proj, x, p_a, p_b, w_o)


def _cross_kernel(x_ref, g_ref, wq_ref, mkt_ref, mv_ref, wo_ref, o_ref):
    x = x_ref[...]
    h = _rms(x, g_ref[...]).astype(BF16)
    mq = (_dot(h, wq_ref[...]) * (MEM_DIM ** -0.5)).astype(BF16)
    outs = []
    for hh in range(MEM_HEADS):
        s = _dot(mq[:, hh * MEM_DIM:(hh + 1) * MEM_DIM], mkt_ref[hh])
        e = jnp.exp(s - jnp.max(s, axis=-1, keepdims=True))
        p = e / jnp.sum(e, axis=-1, keepdims=True)
        outs.append(_dot(p.astype(BF16), mv_ref[hh]).astype(BF16))
    o_ref[...] = x + _dot(jnp.concatenate(outs, axis=1), wo_ref[...])


def _cross(x, g, w_q, mk_t, mv, w_o, tm):
    s, d = x.shape
    mw = w_q.shape[1]
    m = mv.shape[1]
    return pl.pallas_call(
        _cross_kernel,
        grid=(s // tm,),
        in_specs=[
            pl.BlockSpec((tm, d), lambda i: (i, 0)),
            pl.BlockSpec((1, d), lambda i: (0, 0)),
            pl.BlockSpec((d, mw), lambda i: (0, 0)),
            pl.BlockSpec((MEM_HEADS, MEM_DIM, m), lambda i: (0, 0, 0)),
            pl.BlockSpec((MEM_HEADS, m, MEM_DIM), lambda i: (0, 0, 0)),
            pl.BlockSpec((mw, d), lambda i: (0, 0)),
        ],
        out_specs=pl.BlockSpec((tm, d), lambda i: (i, 0)),
        out_shape=jax.ShapeDtypeStruct((s, d), F32),
        compiler_params=_params("parallel"),
        name="cross",
    )(x, g.reshape(1, d), w_q, mk_t, mv, w_o)


def _ffn_kernel(x_ref, g_ref, wg_ref, wu_ref, wo_ref, gf_ref, o_ref, h_ref, acc_ref, *, final_norm):
    j = pl.program_id(1)

    @pl.when(j == 0)
    def _():
        h_ref[...] = _rms(x_ref[...], g_ref[...]).astype(BF16)
        acc_ref[...] = jnp.zeros_like(acc_ref)

    h = h_ref[...]
    act = (jax.nn.silu(_dot(h, wg_ref[...])) * _dot(h, wu_ref[...])).astype(BF16)
    acc_ref[...] += _dot(act, wo_ref[...])

    @pl.when(j == pl.num_programs(1) - 1)
    def _():
        y = x_ref[...] + acc_ref[...]
        o_ref[...] = _rms(y, gf_ref[...]) if final_norm else y


def _ffn(x, g, w_in, w_out, g_final, final_norm, tm, th):
    s, d = x.shape
    hidden = w_out.shape[0]
    nh = hidden // th
    return pl.pallas_call(
        functools.partial(_ffn_kernel, final_norm=final_norm),
        grid=(s // tm, nh),
        in_specs=[
            pl.BlockSpec((tm, d), lambda i, j: (i, 0)),
            pl.BlockSpec((1, d), lambda i, j: (0, 0)),
            pl.BlockSpec((d, th), lambda i, j: (0, j)),
            pl.BlockSpec((d, th), lambda i, j: (0, j + nh)),
            pl.BlockSpec((th, d), lambda i, j: (j, 0)),
            pl.BlockSpec((1, d), lambda i, j: (0, 0)),
        ],
        out_specs=pl.BlockSpec((tm, d), lambda i, j: (i, 0)),
        out_shape=jax.ShapeDtypeStruct((s, d), F32),
        scratch_shapes=[pltpu.VMEM((tm, d), BF16), pltpu.VMEM((tm, d), F32)],
        compiler_params=_params("parallel", "arbitrary"),
        name="ffn",
    )(x, g.reshape(1, d), w_in, w_in, w_out, g_final.reshape(1, d))


def _layer(x, mem, norm_mix_g, w_in, cmp_pe_k, cmp_k_w1, cmp_k_b1, cmp_k_w2, cmp_pe_v, cmp_v_w1, cmp_v_b1, cmp_v_w2,
           sgu_ln_g, sgu_ln_b, sgu_ws, sgu_b, w_proj_a, w_proj_b, w_mix_out, norm_cross_g, norm_mem_g,
           w_mq, w_mkv, w_mo, norm_ffn_g):
    s, d = x.shape
    qw = NSA_HEADS * NSA_DIM
    kvw = NSA_GROUPS * NSA_DIM
    sguw = sgu_ln_g.shape[0]
    ngate = NSA_HEADS * 3
    o_kv = qw
    o_gate = o_kv + N_KV_STREAMS * kvw
    o_u = o_gate + ngate

    w_a = w_in[:, :o_gate].astype(BF16)
    w_b = jnp.pad(w_in[:, o_u:].astype(BF16), ((0, 0), (0, PROJ_TN)))
    w_b = lax.dynamic_update_slice(w_b, w_in[:, o_gate:o_u].astype(BF16), (0, w_in.shape[1] - o_u))
    tm_proj = min(1024, s)
    nat, kw, ks_aug, vs_aug_t, vw_aug_t, q = _proj_attn(x, norm_mix_g, w_a, tm_proj)
    proj_b, gates_t = _proj_gate(x, norm_mix_g, w_b, tm_proj)

    ncp = s // CMP_STRIDE
    r = nat.reshape(2, NSA_GROUPS, ncp, CMP_STRIDE * NSA_DIM)
    w1 = jnp.stack([cmp_k_w1, cmp_v_w1]).astype(BF16)
    pe = jnp.stack([cmp_pe_k.reshape(1, -1), cmp_pe_v.reshape(1, -1)])
    b1 = jnp.stack([cmp_k_b1.reshape(1, -1), cmp_v_b1.reshape(1, -1)])
    w2 = jnp.stack([cmp_k_w2, cmp_v_w2]).astype(BF16)
    cmp, cmp_t = _compress(r, w1, pe, b1, w2)
    o_a = _nsa(q, gates_t, cmp[0], cmp_t[1], ks_aug, vs_aug_t, kw, vw_aug_t)

    o_b = _sgu(proj_b, sgu_ln_g, sgu_ln_b, sgu_ws, sgu_b, min(512, s))
    x = _merge(o_a, o_b, proj_b, x, w_proj_a.astype(BF16), w_proj_b.astype(BF16), w_mix_out.astype(BF16),
               min(256, s), (2 * sguw) // d)

    m = mem.shape[0]
    mw = MEM_HEADS * MEM_DIM
    mkv = _norm_matmul(mem, norm_mem_g, w_mkv.astype(BF16), F32, m, mw)
    mk_t = mkv[:, :mw].reshape(m, MEM_HEADS, MEM_DIM).transpose(1, 2, 0).astype(BF16)
    mv = mkv[:, mw:].reshape(m, MEM_HEADS, MEM_DIM).transpose(1, 0, 2).astype(BF16)
    x = _cross(x, norm_cross_g, w_mq.astype(BF16), mk_t, mv, w_mo.astype(BF16), min(512, s))
    return x


def kernel(x, mem, norm_mix_g, w_in, cmp_pe_k, cmp_k_w1, cmp_k_b1, cmp_k_w2, cmp_pe_v, cmp_v_w1, cmp_v_b1, cmp_v_w2, sgu_ln_g, sgu_ln_b, sgu_ws, sgu_b, w_proj_a, w_proj_b, w_mix_out, norm_cross_g, norm_mem_g, w_mq, w_mkv, w_mo, norm_ffn_g, w_ffn_in, w_ffn_out, norm_final_g):
    b, s, d = x.shape
    depth = w_in.shape[0]
    outs = []
    for bi in range(b):
        xb = x[bi]
        for l in range(depth):
            last = l == depth - 1
            xb = _layer(xb, mem[bi], norm_mix_g[l], w_in[l], cmp_pe_k[l], cmp_k_w1[l], cmp_k_b1[l], cmp_k_w2[l],
                        cmp_pe_v[l], cmp_v_w1[l], cmp_v_b1[l], cmp_v_w2[l], sgu_ln_g[l], sgu_ln_b[l], sgu_ws[l],
                        sgu_b[l], w_proj_a[l], w_proj_b[l], w_mix_out[l], norm_cross_g[l], norm_mem_g[l],
                        w_mq[l], w_mkv[l], w_mo[l], norm_ffn_g[l])
            xb = _ffn(xb, norm_ffn_g[l], w_ffn_in[l].astype(BF16), w_ffn_out[l].astype(BF16), norm_final_g, last,
                      min(512, s), 512)
        outs.append(xb)
    return jnp.stack(outs)
```

```python
import functools
import math

import jax
import jax.numpy as jnp
from jax import lax
from jax.experimental import pallas as pl
from jax.experimental.pallas import tpu as pltpu

F32 = jnp.float32
BF16 = jnp.bfloat16

NORM_EPS = 1e-6
MASK_VALUE = -1e30
N_FORCED = 3
LOG2E = math.log2(math.e)

NSA_HEADS = 16
NSA_GROUPS = 4
NSA_HPG = NSA_HEADS // NSA_GROUPS
NSA_DIM = 64
N_KV_STREAMS = 6
CMP_BLOCK = 32
CMP_STRIDE = 16
SEL_BLOCK = 64
SEL_TOPK = 16
WINDOW = 512
QUERY_BLOCK = 128
SGU_GROUPS = 8
SGU_CHUNK = 128
MEM_HEADS = 4
MEM_DIM = 128

LANES = 128
SUBLANES = 8
COV_RATIO = SEL_BLOCK // CMP_STRIDE
COV_LEAD = CMP_BLOCK // CMP_STRIDE - 1
COV_BAND = tuple(
    max(min((k - COV_LEAD) * CMP_STRIDE + CMP_BLOCK, SEL_BLOCK) - max((k - COV_LEAD) * CMP_STRIDE, 0), 0) / CMP_BLOCK
    for k in range(COV_RATIO + COV_LEAD))
assert SEL_BLOCK % CMP_STRIDE == 0 and CMP_BLOCK % CMP_STRIDE == 0 and COV_LEAD <= SUBLANES
SEL_TILE = 512
SEL_BPT = SEL_TILE // SEL_BLOCK
SEL_CHUNK = 256
SEL_UNROLL = 8
NSA_BUCKETS = 4
BF16_ROWS = 16
VMEM_LIMIT = 56 * 1024 * 1024


def _params(*sem):
    return pltpu.CompilerParams(dimension_semantics=sem, vmem_limit_bytes=VMEM_LIMIT)


def _rms(x, g):
    return x * lax.rsqrt(jnp.mean(x * x, axis=-1, keepdims=True) + NORM_EPS) * g


def _dot(a, b):
    return jnp.dot(a, b, preferred_element_type=F32)


def _ones_row(n):
    return jnp.where(lax.broadcasted_iota(jnp.int32, (BF16_ROWS, n), 0) == 0, 1.0, 0.0).astype(BF16)


def _norm_matmul_kernel(x_ref, g_ref, w_ref, o_ref, h_ref):
    @pl.when(pl.program_id(1) == 0)
    def _():
        h_ref[...] = _rms(x_ref[...], g_ref[...]).astype(BF16)

    o_ref[...] = _dot(h_ref[...], w_ref[...]).astype(o_ref.dtype)


def _norm_matmul(x, g, w, out_dtype, tm, tn):
    s, d = x.shape
    n = w.shape[1]
    return pl.pallas_call(
        _norm_matmul_kernel,
        grid=(s // tm, n // tn),
        in_specs=[
            pl.BlockSpec((tm, d), lambda i, j: (i, 0)),
            pl.BlockSpec((1, d), lambda i, j: (0, 0)),
            pl.BlockSpec((d, tn), lambda i, j: (0, j)),
        ],
        out_specs=pl.BlockSpec((tm, tn), lambda i, j: (i, j)),
        out_shape=jax.ShapeDtypeStruct((s, n), out_dtype),
        scratch_shapes=[pltpu.VMEM((tm, d), BF16)],
        compiler_params=_params("parallel", "arbitrary"),
        name="norm_matmul",
    )(x, g.reshape(1, d), w)


PROJ_TN = 512


def _proj_attn_kernel(x_ref, g_ref, w_ref, nat_ref, kw_ref, ksa_ref, vsa_ref, vwa_ref, q_ref, h_ref):
    i = pl.program_id(0)
    j = pl.program_id(1)
    tm = x_ref.shape[0]
    dh = NSA_DIM
    ng = NSA_GROUPS

    @pl.when(j == 0)
    def _():
        h_ref[...] = _rms(x_ref[...], g_ref[...]).astype(BF16)

    res = _dot(h_ref[...], w_ref[...])

    def cols(c):
        return res[:, c * dh:(c + 1) * dh].astype(BF16)

    def cols_t(c):
        slab = res[:, (c // 2) * 2 * dh:(c // 2 + 1) * 2 * dh].T
        return slab[(c % 2) * dh:(c % 2 + 1) * dh].astype(BF16)

    @pl.when(j == 0)
    def _():
        for c in range(2 * ng):
            nat_ref[c // ng, c % ng] = cols(c)

    @pl.when(j == 1)
    def _():
        pos = i * tm + lax.broadcasted_iota(jnp.int32, (tm, dh), 0)
        lane = lax.broadcasted_iota(jnp.int32, (tm, dh), 1)
        onehot = jnp.where((pos // SEL_BLOCK) % SEL_BPT == lane, 1.0, 0.0).astype(BF16)
        for g in range(ng):
            ksa_ref[g, :, :dh] = cols(g)
            ksa_ref[g, :, dh:] = onehot
            vsa_ref[g, :dh, :] = cols_t(ng + g)
            vsa_ref[g, dh:, :] = _ones_row(tm)

    @pl.when(j == 2)
    def _():
        for g in range(ng):
            kw_ref[g] = cols(g)
            vwa_ref[g, :dh, :] = cols_t(ng + g)
            vwa_ref[g, dh:, :] = _ones_row(tm)

    @pl.when(j >= 3)
    def _():
        q_ref[...] = res.astype(BF16)


def _proj_attn(x, g, w, tm):
    s, d = x.shape
    dh, ng, tn = NSA_DIM, NSA_GROUPS, PROJ_TN
    assert 2 * ng * dh == tn and w.shape[1] == 3 * tn + NSA_HEADS * dh
    nq = NSA_HEADS * dh // tn
    nj = 3 + nq
    shapes = [
        jax.ShapeDtypeStruct((2, ng, s, dh), BF16),
        jax.ShapeDtypeStruct((ng, s, dh), BF16),
        jax.ShapeDtypeStruct((ng, s, 2 * dh), BF16),
        jax.ShapeDtypeStruct((ng, dh + BF16_ROWS, s), BF16),
        jax.ShapeDtypeStruct((ng, dh + BF16_ROWS, s), BF16),
        jax.ShapeDtypeStruct((s, NSA_HEADS * dh), BF16),
    ]
    return pl.pallas_call(
        _proj_attn_kernel,
        grid=(s // tm, nj),
        in_specs=[
            pl.BlockSpec((tm, d), lambda i, j: (i, 0)),
            pl.BlockSpec((1, d), lambda i, j: (0, 0)),
            pl.BlockSpec((d, tn), lambda i, j: (0, (j + nq) % nj)),
        ],
        out_specs=[
            pl.BlockSpec((2, ng, tm, dh), lambda i, j: (0, 0, i, 0)),
            pl.BlockSpec((ng, tm, dh), lambda i, j: (0, i, 0)),
            pl.BlockSpec((ng, tm, 2 * dh), lambda i, j: (0, i, 0)),
            pl.BlockSpec((ng, dh + BF16_ROWS, tm), lambda i, j: (0, 0, i)),
            pl.BlockSpec((ng, dh + BF16_ROWS, tm), lambda i, j: (0, 0, i)),
            pl.BlockSpec((tm, tn), lambda i, j: (i, jnp.maximum(j - 3, 0))),
        ],
        out_shape=shapes,
        scratch_shapes=[pltpu.VMEM((tm, d), BF16)],
        compiler_params=_params("parallel", "arbitrary"),
        name="proj_attn",
    )(x, g.reshape(1, d), w)


def _proj_gate_kernel(x_ref, g_ref, w_ref, o_ref, gt_ref, h_ref):
    j = pl.program_id(1)
    last = pl.num_programs(1) - 1

    @pl.when(j == 0)
    def _():
        h_ref[...] = _rms(x_ref[...], g_ref[...]).astype(BF16)

    res = _dot(h_ref[...], w_ref[...])

    @pl.when(j < last)
    def _():
        o_ref[...] = res.astype(o_ref.dtype)

    @pl.when(j == last)
    def _():
        gt_ref[...] = res[:, :LANES].T


def _proj_gate(x, g, w, tm):
    s, d = x.shape
    tn = PROJ_TN
    n = w.shape[1] - tn
    return pl.pallas_call(
        _proj_gate_kernel,
        grid=(s // tm, n // tn + 1),
        in_specs=[
            pl.BlockSpec((tm, d), lambda i, j: (i, 0)),
            pl.BlockSpec((1, d), lambda i, j: (0, 0)),
            pl.BlockSpec((d, tn), lambda i, j: (0, j)),
        ],
        out_specs=[
            pl.BlockSpec((tm, tn), lambda i, j: (i, jnp.minimum(j, n // tn - 1))),
            pl.BlockSpec((LANES, tm), lambda i, j: (0, i)),
        ],
        out_shape=[jax.ShapeDtypeStruct((s, n), BF16), jax.ShapeDtypeStruct((LANES, s), F32)],
        scratch_shapes=[pltpu.VMEM((tm, d), BF16)],
        compiler_params=_params("parallel", "arbitrary"),
        name="proj_gate",
    )(x, g.reshape(1, d), w)


def _compress_kernel(r_ref, w1_ref, pe_ref, b1_ref, w2_ref, o_ref, ot_ref):
    half = r_ref.shape[-1]
    r = r_ref[0, 0]
    bias = _dot(pe_ref[0].astype(BF16), w1_ref[0]) + b1_ref[0]
    top = _dot(r, w1_ref[0, :half, :])
    bot = _dot(r, w1_ref[0, half:, :])
    ncp = r.shape[0]
    hid = top + pltpu.roll(bot, ncp - 1, 0) + bias
    out = _dot(jax.nn.gelu(hid).astype(BF16), w2_ref[0])
    o_ref[0, 0] = out.astype(o_ref.dtype)
    dh = out.shape[1]
    out_t = jnp.concatenate([out, jnp.zeros((ncp, LANES - dh), F32)], axis=1).T
    ot_ref[0, 0, :dh, :] = out_t[:dh].astype(ot_ref.dtype)
    ot_ref[0, 0, dh:, :] = _ones_row(ncp)


def _compress(r, w1, pe, b1, w2):
    _, g, ncp, half = r.shape
    hidden = w1.shape[-1]
    dh = w2.shape[-1]
    return pl.pallas_call(
        _compress_kernel,
        grid=(2, g),
        in_specs=[
            pl.BlockSpec((1, 1, ncp, half), lambda s, gg: (s, gg, 0, 0)),
            pl.BlockSpec((1, 2 * half, hidden), lambda s, gg: (s, 0, 0)),
            pl.BlockSpec((1, 1, 2 * half), lambda s, gg: (s, 0, 0)),
            pl.BlockSpec((1, 1, hidden), lambda s, gg: (s, 0, 0)),
            pl.BlockSpec((1, hidden, dh), lambda s, gg: (s, 0, 0)),
        ],
        out_specs=[pl.BlockSpec((1, 1, ncp, dh), lambda s, gg: (s, gg, 0, 0)),
                   pl.BlockSpec((1, 1, dh + BF16_ROWS, ncp), lambda s, gg: (s, gg, 0, 0))],
        out_shape=[jax.ShapeDtypeStruct((2, g, ncp, dh), BF16),
                   jax.ShapeDtypeStruct((2, g, dh + BF16_ROWS, ncp), BF16)],
        compiler_params=_params("parallel", "parallel"),
        name="compress",
    )(r, w1, pe, b1, w2)


def _nsa_kernel(q_ref, gt_ref, kc_ref, vct_ref, ks_ref, vst_ref, kw_ref, vwt_ref, o_ref,
                qa_ref, sel_ref, s0_ref, s1_ref, acc_ref, part_ref, pp_ref):
    s_refs = (s0_ref, s1_ref)
    g = pl.program_id(0)
    i = pl.program_id(1)
    qb = QUERY_BLOCK
    start = i * qb
    nb = sel_ref.shape[0]
    ncp = kc_ref.shape[1]
    s_len = ks_ref.shape[1]
    hq = NSA_HPG * qb
    t_row = start + lax.broadcasted_iota(jnp.int32, (1, qb), 1)

    qt = (q_ref[...].astype(F32) * (NSA_DIM ** -0.5 * LOG2E)).T
    q_t = jnp.concatenate([qt[h * NSA_DIM:(h + 1) * NSA_DIM] for h in range(NSA_HPG)], axis=1).astype(BF16)
    qa_ref[:NSA_DIM, :] = q_t
    qa_ref[NSA_DIM:, :] = jnp.zeros((qa_ref.shape[0] - NSA_DIM, hq), BF16)

    jt = t_row // SEL_BLOCK
    picked = -2.0

    def gate(branch):
        rows = [gt_ref[pl.ds(g * (NSA_HPG * 3) + h * 3 + branch, 1), :] for h in range(NSA_HPG)]
        return jax.nn.sigmoid(jnp.concatenate(rows, axis=1))

    def exps(s, bias):
        out = []
        for h in range(NSA_HPG):
            sh = s[:, h * qb:(h + 1) * qb] + bias
            out.append(jnp.exp2(sh - jnp.max(sh, axis=0, keepdims=True)))
        return out

    def compressed_and_select(nck, nbk):
        wk = WINDOW + qb
        k0w = pl.multiple_of(jnp.maximum(start - WINDOW, 0), qb)
        sc = _dot(kc_ref[0, :nck, :], q_t)
        sw = _dot(kw_ref[0, pl.ds(k0w, wk), :], q_t)
        cmask = lax.broadcasted_iota(jnp.int32, (nck, qb), 0) * CMP_STRIDE + (CMP_BLOCK - 1) <= t_row
        e_cmp = exps(sc, jnp.where(cmask, 0.0, MASK_VALUE))
        o_cmp = _dot(vct_ref[0, :, :nck], jnp.concatenate([e.astype(BF16) for e in e_cmp], axis=1))
        kpos = k0w + lax.broadcasted_iota(jnp.int32, (wk, qb), 0)
        e_win = exps(sw, jnp.where((kpos <= t_row) & (kpos > t_row - WINDOW), 0.0, MASK_VALUE))
        o_win = _dot(vwt_ref[0, :, pl.ds(k0w, wk)], jnp.concatenate([e.astype(BF16) for e in e_win], axis=1))
        inv_cmp = jnp.where(jnp.concatenate([t_row >= CMP_BLOCK - 1] * NSA_HPG, axis=1),
                            1.0 / o_cmp[NSA_DIM:NSA_DIM + 1, :], 0.0)
        psum = e_cmp[0] * inv_cmp[:, :qb]
        for h in range(1, NSA_HPG):
            psum = psum + e_cmp[h] * inv_cmp[:, h * qb:(h + 1) * qb]
        part_ref[...] = (gate(0) * inv_cmp * o_cmp[:NSA_DIM]
                         + gate(2) * (1.0 / o_win[NSA_DIM:NSA_DIM + 1, :]) * o_win[:NSA_DIM])
        pp_ref[:SUBLANES, :] = jnp.zeros((SUBLANES, qb), F32)
        pp_ref[SUBLANES:SUBLANES + nck, :] = psum
        imp = None
        for k, w in enumerate(COV_BAND):
            tap = pp_ref[pl.ds(SUBLANES - COV_LEAD + k, nbk, stride=COV_RATIO), :]
            tap = tap if w == 1.0 else w * tap
            imp = tap if imp is None else imp + tap

        blk = lax.broadcasted_iota(jnp.int32, (nbk, qb), 0)
        forced = (blk == 0) | (blk == jt) | (blk == jt - 1)
        score = jnp.where(forced, picked, jnp.where(blk <= jt, imp, -1.0))
        blk_f = blk.astype(F32)

        def pick(_, score):
            mx = jnp.max(score, axis=0, keepdims=True)
            idx = jnp.min(jnp.where(score == mx, blk_f, float(nbk)), axis=0, keepdims=True)
            return jnp.where(blk_f == idx, picked, score)

        score = lax.fori_loop(0, min(SEL_TOPK, nbk) - N_FORCED, pick, score)
        sel_ref[:nbk, :] = jnp.where((score == picked) & (blk <= jt), 0.0, MASK_VALUE)

    bucket_ok = s_len % (NSA_BUCKETS * SEL_TILE) == 0 and nb // NSA_BUCKETS >= SEL_TOPK
    n_bucket = NSA_BUCKETS if bucket_ok else 1
    bucket = (start + qb - 1) // (s_len // n_bucket)
    for b in range(n_bucket):
        pl.when(bucket == b)(functools.partial(
            compressed_and_select, (b + 1) * (ncp // n_bucket), (b + 1) * (nb // n_bucket)))

    tk = SEL_TILE

    def scores(slot, kt):
        k0 = pl.multiple_of(kt * tk, tk)
        b8 = sel_ref[pl.ds(pl.multiple_of(kt * SEL_BPT, SEL_BPT), SEL_BPT), :]
        b16 = jnp.concatenate([b8, jnp.zeros((BF16_ROWS - SEL_BPT, qb), F32)], axis=0).astype(BF16)
        qa_ref[NSA_DIM:NSA_DIM + BF16_ROWS, :] = jnp.concatenate([b16] * NSA_HPG, axis=1)
        s = _dot(ks_ref[0, pl.ds(k0, tk), :], qa_ref[...])
        s_refs[slot][...] = s
        return jnp.max(s, axis=0, keepdims=True)

    def update(slot, kt, mt, m, causal):
        k0 = pl.multiple_of(kt * tk, tk)
        s = s_refs[slot][...]
        if causal:
            kpos = k0 + lax.broadcasted_iota(jnp.int32, (tk, qb), 0)
            s = s + jnp.concatenate([jnp.where(kpos <= t_row, 0.0, MASK_VALUE)] * NSA_HPG, axis=1)
            mt = jnp.max(s, axis=0, keepdims=True)
        m_new = jnp.maximum(m, mt)
        p = jnp.exp2(s - m_new).astype(BF16)
        acc_ref[...] = jnp.exp2(m - m_new) * acc_ref[...] + _dot(vst_ref[0, :, pl.ds(k0, tk)], p)
        return m_new

    def fused(cur, kt_cur, mt_cur, m, nxt, kt_nxt):
        k0c = pl.multiple_of(kt_cur * tk, tk)
        k0n = pl.multiple_of(kt_nxt * tk, tk)
        b8 = sel_ref[pl.ds(pl.multiple_of(kt_nxt * SEL_BPT, SEL_BPT), SEL_BPT), :]
        b16 = jnp.concatenate([b8, jnp.zeros((BF16_ROWS - SEL_BPT, qb), F32)], axis=0).astype(BF16)
        qa_ref[NSA_DIM:NSA_DIM + BF16_ROWS, :] = jnp.concatenate([b16] * NSA_HPG, axis=1)
        m_new = jnp.maximum(m, mt_cur)
        mt_nxt, pv = None, None
        for c in range(tk // SEL_CHUNK):
            rows = pl.ds(c * SEL_CHUNK, SEL_CHUNK)
            s_n = _dot(ks_ref[0, pl.ds(k0n + c * SEL_CHUNK, SEL_CHUNK), :], qa_ref[...])
            s_refs[nxt][rows, :] = s_n
            mt_c = jnp.max(s_n, axis=0, keepdims=True)
            mt_nxt = mt_c if mt_nxt is None else jnp.maximum(mt_nxt, mt_c)
            p = jnp.exp2(s_refs[cur][rows, :] - m_new).astype(BF16)
            pv_c = _dot(vst_ref[0, :, pl.ds(k0c + c * SEL_CHUNK, SEL_CHUNK)], p)
            pv = pv_c if pv is None else pv + pv_c
        acc_ref[...] = jnp.exp2(m - m_new) * acc_ref[...] + pv
        return mt_nxt, m_new

    def run(first, n_tiles, carry):
        mt, m = carry
        for k in range(n_tiles):
            mt, m = fused(k % 2, first + k, mt, m, (k + 1) % 2, first + k + 1)
        return mt, m

    acc_ref[...] = jnp.zeros(acc_ref.shape, F32)
    last = (start + qb - 1) // tk
    unroll = SEL_UNROLL
    carry = (scores(0, 0), jnp.full((1, hq), MASK_VALUE, F32))
    n_long = last // unroll
    carry = lax.fori_loop(0, n_long, lambda it, c: run(it * unroll, unroll, c), carry)
    n_short = (last - n_long * unroll) // 2
    mt_a, m = lax.fori_loop(0, n_short, lambda it, c: run(n_long * unroll + 2 * it, 2, c), carry)

    @pl.when(last % 2 == 0)
    def _():
        update(0, last, mt_a, m, True)

    @pl.when(last % 2 == 1)
    def _():
        mt_b = scores(1, last)
        update(1, last, mt_b, update(0, last - 1, mt_a, m, False), True)

    o_sel = acc_ref[:NSA_DIM, :] * (1.0 / acc_ref[NSA_DIM:NSA_DIM + 1, :])

    out_t = part_ref[...] + gate(1) * o_sel
    outs = [out_t[:, h * qb:(h + 1) * qb] for h in range(NSA_HPG)]
    o_ref[...] = jnp.concatenate(outs, axis=0).T.astype(o_ref.dtype)


def _nsa(q, gates_t, kc, vc_aug_t, ks_aug, vs_aug_t, kw, vw_aug_t):
    s = q.shape[0]
    assert s % (2 * SEL_TILE) == 0 and s >= WINDOW + QUERY_BLOCK
    gdim = NSA_HPG * NSA_DIM
    ncp = kc.shape[1]
    assert ncp * CMP_STRIDE == s and s // SEL_BLOCK * COV_RATIO == ncp
    nb = s // SEL_BLOCK
    ka = ks_aug.shape[-1]
    va = vs_aug_t.shape[1]
    hq = NSA_HPG * QUERY_BLOCK
    per_group = lambda shape: pl.BlockSpec((1,) + shape, lambda g, i: (g, 0, 0))
    return pl.pallas_call(
        _nsa_kernel,
        grid=(NSA_GROUPS, s // QUERY_BLOCK),
        in_specs=[
            pl.BlockSpec((QUERY_BLOCK, gdim), lambda g, i: (i, g)),
            pl.BlockSpec((NSA_HEADS * 3, QUERY_BLOCK), lambda g, i: (0, i)),
            per_group((ncp, NSA_DIM)),
            per_group((va, ncp)),
            per_group((s, ka)),
            per_group((va, s)),
            per_group((s, NSA_DIM)),
            per_group((va, s)),
        ],
        out_specs=pl.BlockSpec((QUERY_BLOCK, gdim), lambda g, i: (i, g)),
        out_shape=jax.ShapeDtypeStruct((s, NSA_HEADS * NSA_DIM), BF16),
        scratch_shapes=[pltpu.VMEM((ka, hq), BF16), pltpu.VMEM((nb, QUERY_BLOCK), F32),
                        pltpu.VMEM((SEL_TILE, hq), F32), pltpu.VMEM((SEL_TILE, hq), F32),
                        pltpu.VMEM((va, hq), F32),
                        pltpu.VMEM((NSA_DIM, hq), F32),
                        pltpu.VMEM((SUBLANES + ncp, QUERY_BLOCK), F32)],
        compiler_params=_params("parallel", "arbitrary"),
        name="nsa",
    )(q, gates_t, kc, vc_aug_t, ks_aug, vs_aug_t, kw, vw_aug_t)


def _sgu_kernel(u_ref, v_ref, lng_ref, lnb_ref, ws_ref, bs_ref, o_ref):
    c = SGU_CHUNK
    tm = u_ref.shape[0]
    v = jax.nn.gelu(v_ref[...].astype(F32))
    mu = jnp.mean(v, axis=-1, keepdims=True)
    var = jnp.mean(jnp.square(v - mu), axis=-1, keepdims=True)
    vn = ((v - mu) * lax.rsqrt(var + NORM_EPS) * lng_ref[...] + lnb_ref[...]).astype(BF16)
    u = jax.nn.gelu(u_ref[...].astype(F32))
    tri = lax.broadcasted_iota(jnp.int32, (c, c), 0) >= lax.broadcasted_iota(jnp.int32, (c, c), 1)
    for g in range(SGU_GROUPS):
        w = jnp.where(tri, ws_ref[g], 0.0).astype(BF16)
        cols = slice(g * c, (g + 1) * c)
        rhs = jnp.concatenate([vn[k * c:(k + 1) * c, cols] for k in range(tm // c)], axis=1)
        mixed = _dot(w, rhs)
        for k in range(tm // c):
            rows = slice(k * c, (k + 1) * c)
            o_ref[rows, cols] = (u[rows, cols] * (mixed[:, rows] + bs_ref[g])).astype(o_ref.dtype)


def _sgu(proj, lng, lnb, ws, bs, tm):
    s = proj.shape[0]
    w = lng.shape[0]
    c = SGU_CHUNK
    bs_b = jnp.broadcast_to(bs[:, :, None], (SGU_GROUPS, c, c))
    return pl.pallas_call(
        _sgu_kernel,
        grid=(s // tm,),
        in_specs=[
            pl.BlockSpec((tm, w), lambda i: (i, 0)),
            pl.BlockSpec((tm, w), lambda i: (i, 1)),
            pl.BlockSpec((1, w), lambda i: (0, 0)),
            pl.BlockSpec((1, w), lambda i: (0, 0)),
            pl.BlockSpec((SGU_GROUPS, c, c), lambda i: (0, 0, 0)),
            pl.BlockSpec((SGU_GROUPS, c, c), lambda i: (0, 0, 0)),
        ],
        out_specs=pl.BlockSpec((tm, w), lambda i: (i, 0)),
        out_shape=jax.ShapeDtypeStruct((s, w), BF16),
        compiler_params=_params("parallel"),
        name="sgu",
    )(proj, proj, lng.reshape(1, w), lnb.reshape(1, w), ws, bs_b)


def _mix_kernel(oa_ref, ob_ref, ga_ref, gb_ref, x_ref, pa_ref, pb_ref, wo_ref,
                gc_ref, wq_ref, mkt_ref, mv_ref, wmo_ref, o_ref):
    a = _dot(oa_ref[...], pa_ref[...])
    b = _dot(ob_ref[...], pb_ref[...])
    merged = jax.nn.sigmoid(ga_ref[...].astype(F32)) * a + jax.nn.sigmoid(gb_ref[...].astype(F32)) * b
    x = x_ref[...] + _dot(merged.astype(BF16), wo_ref[...])
    h = _rms(x, gc_ref[...]).astype(BF16)
    mq = (_dot(h, wq_ref[...]) * (MEM_DIM ** -0.5)).astype(BF16)
    outs = []
    for hh in range(MEM_HEADS):
        s = _dot(mq[:, hh * MEM_DIM:(hh + 1) * MEM_DIM], mkt_ref[hh])
        e = jnp.exp(s - jnp.max(s, axis=-1, keepdims=True))
        p = e / jnp.sum(e, axis=-1, keepdims=True)
        outs.append(_dot(p.astype(BF16), mv_ref[hh]).astype(BF16))
    o_ref[...] = x + _dot(jnp.concatenate(outs, axis=1), wmo_ref[...])


def _mix(o_a, o_b, proj, x, p_a, p_b, w_o, gate_block, g_cross, w_q, mk_t, mv, w_mo, tm):
    s, d = x.shape
    wa = o_a.shape[1]
    wb = o_b.shape[1]
    mw = w_q.shape[1]
    m = mv.shape[1]
    resident = lambda shape: pl.BlockSpec(shape, lambda i: (0,) * len(shape), pipeline_mode=pl.Buffered(1))
    return pl.pallas_call(
        _mix_kernel,
        grid=(s // tm,),
        in_specs=[
            pl.BlockSpec((tm, wa), lambda i: (i, 0)),
            pl.BlockSpec((tm, wb), lambda i: (i, 0)),
            pl.BlockSpec((tm, d), lambda i: (i, gate_block)),
            pl.BlockSpec((tm, d), lambda i: (i, gate_block + 1)),
            pl.BlockSpec((tm, d), lambda i: (i, 0)),
            resident((wa, d)),
            resident((wb, d)),
            resident((d, d)),
            resident((1, d)),
            resident((d, mw)),
            resident((MEM_HEADS, MEM_DIM, m)),
            resident((MEM_HEADS, m, MEM_DIM)),
            resident((mw, d)),
        ],
        out_specs=pl.BlockSpec((tm, d), lambda i: (i, 0)),
        out_shape=jax.ShapeDtypeStruct((s, d), F32),
        compiler_params=_params("parallel"),
        name="mix",
    )(o_a, o_b, proj, proj, x, p_a, p_b, w_o, g_cross.reshape(1, d), w_q, mk_t, mv, w_mo)


def _ffn_kernel(x_ref, g_ref, wg_ref, wu_ref, wo_ref, gf_ref, o_ref, h_ref, acc_ref, *, final_norm):
    j = pl.program_id(1)

    @pl.when(j == 0)
    def _():
        h_ref[...] = _rms(x_ref[...], g_ref[...]).astype(BF16)
        acc_ref[...] = jnp.zeros_like(acc_ref)

    h = h_ref[...]
    act = (jax.nn.silu(_dot(h, wg_ref[...])) * _dot(h, wu_ref[...])).astype(BF16)
    acc_ref[...] += _dot(act, wo_ref[...])

    @pl.when(j == pl.num_programs(1) - 1)
    def _():
        y = x_ref[...] + acc_ref[...]
        o_ref[...] = _rms(y, gf_ref[...]) if final_norm else y


def _ffn(x, g, w_in, w_out, g_final, final_norm, tm, th):
    s, d = x.shape
    hidden = w_out.shape[0]
    nh = hidden // th
    return pl.pallas_call(
        functools.partial(_ffn_kernel, final_norm=final_norm),
        grid=(s // tm, nh),
        in_specs=[
            pl.BlockSpec((tm, d), lambda i, j: (i, 0)),
            pl.BlockSpec((1, d), lambda i, j: (0, 0)),
            pl.BlockSpec((d, th), lambda i, j: (0, j)),
            pl.BlockSpec((d, th), lambda i, j: (0, j + nh)),
            pl.BlockSpec((th, d), lambda i, j: (j, 0)),
            pl.BlockSpec((1, d), lambda i, j: (0, 0)),
        ],
        out_specs=pl.BlockSpec((tm, d), lambda i, j: (i, 0)),
        out_shape=jax.ShapeDtypeStruct((s, d), F32),
        scratch_shapes=[pltpu.VMEM((tm, d), BF16), pltpu.VMEM((tm, d), F32)],
        compiler_params=_params("parallel", "arbitrary"),
        name="ffn",
    )(x, g.reshape(1, d), w_in, w_in, w_out, g_final.reshape(1, d))


def _layer(x, mem, norm_mix_g, w_in, cmp_pe_k, cmp_k_w1, cmp_k_b1, cmp_k_w2, cmp_pe_v, cmp_v_w1, cmp_v_b1, cmp_v_w2,
           sgu_ln_g, sgu_ln_b, sgu_ws, sgu_b, w_proj_a, w_proj_b, w_mix_out, norm_cross_g, norm_mem_g,
           w_mq, w_mkv, w_mo, norm_ffn_g):
    s, d = x.shape
    qw = NSA_HEADS * NSA_DIM
    kvw = NSA_GROUPS * NSA_DIM
    sguw = sgu_ln_g.shape[0]
    ngate = NSA_HEADS * 3
    o_kv = qw
    o_gate = o_kv + N_KV_STREAMS * kvw
    o_u = o_gate + ngate

    w_a = w_in[:, :o_gate].astype(BF16)
    w_b = jnp.pad(w_in[:, o_u:].astype(BF16), ((0, 0), (0, PROJ_TN)))
    w_b = lax.dynamic_update_slice(w_b, w_in[:, o_gate:o_u].astype(BF16), (0, w_in.shape[1] - o_u))
    tm_proj = min(1024, s)
    nat, kw, ks_aug, vs_aug_t, vw_aug_t, q = _proj_attn(x, norm_mix_g, w_a, tm_proj)
    proj_b, gates_t = _proj_gate(x, norm_mix_g, w_b, tm_proj)

    ncp = s // CMP_STRIDE
    r = nat.reshape(2, NSA_GROUPS, ncp, CMP_STRIDE * NSA_DIM)
    w1 = jnp.stack([cmp_k_w1, cmp_v_w1]).astype(BF16)
    pe = jnp.stack([cmp_pe_k.reshape(1, -1), cmp_pe_v.reshape(1, -1)])
    b1 = jnp.stack([cmp_k_b1.reshape(1, -1), cmp_v_b1.reshape(1, -1)])
    w2 = jnp.stack([cmp_k_w2, cmp_v_w2]).astype(BF16)
    cmp, cmp_t = _compress(r, w1, pe, b1, w2)
    o_a = _nsa(q, gates_t, cmp[0], cmp_t[1], ks_aug, vs_aug_t, kw, vw_aug_t)

    o_b = _sgu(proj_b, sgu_ln_g, sgu_ln_b, sgu_ws, sgu_b, min(512, s))

    m = mem.shape[0]
    mw = MEM_HEADS * MEM_DIM
    mkv = _norm_matmul(mem, norm_mem_g, w_mkv.astype(BF16), F32, m, mw)
    mk_t = mkv[:, :mw].reshape(m, MEM_HEADS, MEM_DIM).transpose(1, 2, 0).astype(BF16)
    mv = mkv[:, mw:].reshape(m, MEM_HEADS, MEM_DIM).transpose(1, 0, 2).astype(BF16)
    return _mix(o_a, o_b, proj_b, x, w_proj_a.astype(BF16), w_proj_b.astype(BF16), w_mix_out.astype(BF16),
                (2 * sguw) // d, norm_cross_g, w_mq.astype(BF16), mk_t, mv, w_mo.astype(BF16), min(256, s))


def kernel(x, mem, norm_mix_g, w_in, cmp_pe_k, cmp_k_w1, cmp_k_b1, cmp_k_w2, cmp_pe_v, cmp_v_w1, cmp_v_b1, cmp_v_w2, sgu_ln_g, sgu_ln_b, sgu_ws, sgu_b, w_proj_a, w_proj_b, w_mix_out, norm_cross_g, norm_mem_g, w_mq, w_mkv, w_mo, norm_ffn_g, w_ffn_in, w_ffn_out, norm_final_g):
    b, s, d = x.shape
    depth = w_in.shape[0]
    outs = []
    for bi in range(b):
        xb = x[bi]
        for l in range(depth):
            last = l == depth - 1
            xb = _layer(xb, mem[bi], norm_mix_g[l], w_in[l], cmp_pe_k[l], cmp_k_w1[l], cmp_k_b1[l], cmp_k_w2[l],
                        cmp_pe_v[l], cmp_v_w1[l], cmp_v_b1[l], cmp_v_w2[l], sgu_ln_g[l], sgu_ln_b[l], sgu_ws[l],
                        sgu_b[l], w_proj_a[l], w_proj_b[l], w_mix_out[l], norm_cross_g[l], norm_mem_g[l],
                        w_mq[l], w_mkv[l], w_mo[l], norm_ffn_g[l])
            xb = _ffn(xb, norm_ffn_g[l], w_ffn_in[l].astype(BF16), w_ffn_out[l].astype(BF16), norm_final_g, last,
                      min(512, s), 512)
        outs.append(xb)
    return jnp.stack(outs)
```

```python
import functools
import math

import jax
import jax.numpy as jnp
from jax import lax
from jax.experimental import pallas as pl
from jax.experimental.pallas import tpu as pltpu

F32 = jnp.float32
BF16 = jnp.bfloat16

NORM_EPS = 1e-6
MASK_VALUE = -1e30
N_FORCED = 3
LOG2E = math.log2(math.e)

NSA_HEADS = 16
NSA_GROUPS = 4
NSA_HPG = NSA_HEADS // NSA_GROUPS
NSA_DIM = 64
N_KV_STREAMS = 6
CMP_BLOCK = 32
CMP_STRIDE = 16
SEL_BLOCK = 64
SEL_TOPK = 16
WINDOW = 512
QUERY_BLOCK = 128
SGU_GROUPS = 8
SGU_CHUNK = 128
MEM_HEADS = 4
MEM_DIM = 128

LANES = 128
SUBLANES = 8
COV_RATIO = SEL_BLOCK // CMP_STRIDE
COV_LEAD = CMP_BLOCK // CMP_STRIDE - 1
COV_BAND = tuple(
    max(min((k - COV_LEAD) * CMP_STRIDE + CMP_BLOCK, SEL_BLOCK) - max((k - COV_LEAD) * CMP_STRIDE, 0), 0) / CMP_BLOCK
    for k in range(COV_RATIO + COV_LEAD))
assert SEL_BLOCK % CMP_STRIDE == 0 and CMP_BLOCK % CMP_STRIDE == 0 and COV_LEAD <= SUBLANES
SEL_TILE = 512
SEL_BPT = SEL_TILE // SEL_BLOCK
SEL_CHUNK = 256
SEL_UNROLL = 8
NSA_STEP_BLOCKS = 4
NSA_BUCKETS = 4
BF16_ROWS = 16
FFN_CHUNK = 256
VMEM_LIMIT = 56 * 1024 * 1024


def _params(*sem):
    return pltpu.CompilerParams(dimension_semantics=sem, vmem_limit_bytes=VMEM_LIMIT)


def _rms(x, g):
    return x * lax.rsqrt(jnp.mean(x * x, axis=-1, keepdims=True) + NORM_EPS) * g


def _dot(a, b):
    return jnp.dot(a, b, preferred_element_type=F32)


def _ones_row(n):
    return jnp.where(lax.broadcasted_iota(jnp.int32, (BF16_ROWS, n), 0) == 0, 1.0, 0.0).astype(BF16)


def _norm_matmul_kernel(x_ref, g_ref, w_ref, o_ref, h_ref):
    @pl.when(pl.program_id(1) == 0)
    def _():
        h_ref[...] = _rms(x_ref[...], g_ref[...]).astype(BF16)

    o_ref[...] = _dot(h_ref[...], w_ref[...]).astype(o_ref.dtype)


def _norm_matmul(x, g, w, out_dtype, tm, tn):
    s, d = x.shape
    n = w.shape[1]
    return pl.pallas_call(
        _norm_matmul_kernel,
        grid=(s // tm, n // tn),
        in_specs=[
            pl.BlockSpec((tm, d), lambda i, j: (i, 0)),
            pl.BlockSpec((1, d), lambda i, j: (0, 0)),
            pl.BlockSpec((d, tn), lambda i, j: (0, j)),
        ],
        out_specs=pl.BlockSpec((tm, tn), lambda i, j: (i, j)),
        out_shape=jax.ShapeDtypeStruct((s, n), out_dtype),
        scratch_shapes=[pltpu.VMEM((tm, d), BF16)],
        compiler_params=_params("parallel", "arbitrary"),
        name="norm_matmul",
    )(x, g.reshape(1, d), w)


PROJ_TN = 512


def _proj_attn_kernel(x_ref, g_ref, w_ref, nat_ref, kw_ref, ksa_ref, vsa_ref, vwa_ref, q_ref, h_ref):
    i = pl.program_id(0)
    j = pl.program_id(1)
    tm = x_ref.shape[0]
    dh = NSA_DIM
    ng = NSA_GROUPS

    @pl.when(j == 0)
    def _():
        h_ref[...] = _rms(x_ref[...], g_ref[...]).astype(BF16)

    res = _dot(h_ref[...], w_ref[...])

    def cols(c):
        return res[:, c * dh:(c + 1) * dh].astype(BF16)

    def cols_t(c):
        slab = res[:, (c // 2) * 2 * dh:(c // 2 + 1) * 2 * dh].T
        return slab[(c % 2) * dh:(c % 2 + 1) * dh].astype(BF16)

    @pl.when(j == 0)
    def _():
        for c in range(2 * ng):
            nat_ref[c // ng, c % ng] = cols(c)

    @pl.when(j == 1)
    def _():
        pos = i * tm + lax.broadcasted_iota(jnp.int32, (tm, dh), 0)
        lane = lax.broadcasted_iota(jnp.int32, (tm, dh), 1)
        onehot = jnp.where((pos // SEL_BLOCK) % SEL_BPT == lane, 1.0, 0.0).astype(BF16)
        for g in range(ng):
            ksa_ref[g, :, :dh] = cols(g)
            ksa_ref[g, :, dh:] = onehot
            vsa_ref[g, :dh, :] = cols_t(ng + g)
            vsa_ref[g, dh:, :] = _ones_row(tm)

    @pl.when(j == 2)
    def _():
        for g in range(ng):
            kw_ref[g] = cols(g)
            vwa_ref[g, :dh, :] = cols_t(ng + g)
            vwa_ref[g, dh:, :] = _ones_row(tm)

    @pl.when(j >= 3)
    def _():
        q_ref[...] = res.astype(BF16)


def _proj_attn(x, g, w, tm):
    s, d = x.shape
    dh, ng, tn = NSA_DIM, NSA_GROUPS, PROJ_TN
    assert 2 * ng * dh == tn and w.shape[1] == 3 * tn + NSA_HEADS * dh
    nq = NSA_HEADS * dh // tn
    nj = 3 + nq
    shapes = [
        jax.ShapeDtypeStruct((2, ng, s, dh), BF16),
        jax.ShapeDtypeStruct((ng, s, dh), BF16),
        jax.ShapeDtypeStruct((ng, s, 2 * dh), BF16),
        jax.ShapeDtypeStruct((ng, dh + BF16_ROWS, s), BF16),
        jax.ShapeDtypeStruct((ng, dh + BF16_ROWS, s), BF16),
        jax.ShapeDtypeStruct((s, NSA_HEADS * dh), BF16),
    ]
    return pl.pallas_call(
        _proj_attn_kernel,
        grid=(s // tm, nj),
        in_specs=[
            pl.BlockSpec((tm, d), lambda i, j: (i, 0)),
            pl.BlockSpec((1, d), lambda i, j: (0, 0)),
            pl.BlockSpec((d, tn), lambda i, j: (0, (j + nq) % nj)),
        ],
        out_specs=[
            pl.BlockSpec((2, ng, tm, dh), lambda i, j: (0, 0, i, 0)),
            pl.BlockSpec((ng, tm, dh), lambda i, j: (0, i, 0)),
            pl.BlockSpec((ng, tm, 2 * dh), lambda i, j: (0, i, 0)),
            pl.BlockSpec((ng, dh + BF16_ROWS, tm), lambda i, j: (0, 0, i)),
            pl.BlockSpec((ng, dh + BF16_ROWS, tm), lambda i, j: (0, 0, i)),
            pl.BlockSpec((tm, tn), lambda i, j: (i, jnp.maximum(j - 3, 0))),
        ],
        out_shape=shapes,
        scratch_shapes=[pltpu.VMEM((tm, d), BF16)],
        compiler_params=_params("parallel", "arbitrary"),
        name="proj_attn",
    )(x, g.reshape(1, d), w)


def _proj_gate_kernel(x_ref, g_ref, w_ref, o_ref, gt_ref, h_ref):
    j = pl.program_id(1)
    last = pl.num_programs(1) - 1

    @pl.when(j == 0)
    def _():
        h_ref[...] = _rms(x_ref[...], g_ref[...]).astype(BF16)

    res = _dot(h_ref[...], w_ref[...])

    @pl.when(j < last)
    def _():
        o_ref[...] = res

    @pl.when(j == last)
    def _():
        gt_ref[...] = res[:, :LANES].T


def _proj_gate(x, g, w, tm):
    s, d = x.shape
    tn = PROJ_TN
    n = w.shape[1] - tn
    return pl.pallas_call(
        _proj_gate_kernel,
        grid=(s // tm, n // tn + 1),
        in_specs=[
            pl.BlockSpec((tm, d), lambda i, j: (i, 0)),
            pl.BlockSpec((1, d), lambda i, j: (0, 0)),
            pl.BlockSpec((d, tn), lambda i, j: (0, j)),
        ],
        out_specs=[
            pl.BlockSpec((tm, tn), lambda i, j: (i, jnp.minimum(j, n // tn - 1))),
            pl.BlockSpec((LANES, tm), lambda i, j: (0, i)),
        ],
        out_shape=[jax.ShapeDtypeStruct((s, n), F32), jax.ShapeDtypeStruct((LANES, s), F32)],
        scratch_shapes=[pltpu.VMEM((tm, d), BF16)],
        compiler_params=_params("parallel", "arbitrary"),
        name="proj_gate",
    )(x, g.reshape(1, d), w)


def _compress_kernel(r_ref, w1_ref, pe_ref, b1_ref, w2_ref, o_ref, ot_ref):
    half = r_ref.shape[-1]
    r = r_ref[0, 0]
    bias = _dot(pe_ref[0].astype(BF16), w1_ref[0]) + b1_ref[0]
    top = _dot(r, w1_ref[0, :half, :])
    bot = _dot(r, w1_ref[0, half:, :])
    ncp = r.shape[0]
    hid = top + pltpu.roll(bot, ncp - 1, 0) + bias
    out = _dot(jax.nn.gelu(hid).astype(BF16), w2_ref[0])
    o_ref[0, 0] = out.astype(o_ref.dtype)
    dh = out.shape[1]
    out_t = jnp.concatenate([out, jnp.zeros((ncp, LANES - dh), F32)], axis=1).T
    ot_ref[0, 0, :dh, :] = out_t[:dh].astype(ot_ref.dtype)
    ot_ref[0, 0, dh:, :] = _ones_row(ncp)


def _compress(r, w1, pe, b1, w2):
    _, g, ncp, half = r.shape
    hidden = w1.shape[-1]
    dh = w2.shape[-1]
    return pl.pallas_call(
        _compress_kernel,
        grid=(2, g),
        in_specs=[
            pl.BlockSpec((1, 1, ncp, half), lambda s, gg: (s, gg, 0, 0)),
            pl.BlockSpec((1, 2 * half, hidden), lambda s, gg: (s, 0, 0)),
            pl.BlockSpec((1, 1, 2 * half), lambda s, gg: (s, 0, 0)),
            pl.BlockSpec((1, 1, hidden), lambda s, gg: (s, 0, 0)),
            pl.BlockSpec((1, hidden, dh), lambda s, gg: (s, 0, 0)),
        ],
        out_specs=[pl.BlockSpec((1, 1, ncp, dh), lambda s, gg: (s, gg, 0, 0)),
                   pl.BlockSpec((1, 1, dh + BF16_ROWS, ncp), lambda s, gg: (s, gg, 0, 0))],
        out_shape=[jax.ShapeDtypeStruct((2, g, ncp, dh), BF16),
                   jax.ShapeDtypeStruct((2, g, dh + BF16_ROWS, ncp), BF16)],
        compiler_params=_params("parallel", "parallel"),
        name="compress",
    )(r, w1, pe, b1, w2)


def _nsa_kernel(q_ref, gt_ref, kc_ref, vct_ref, ks_ref, vst_ref, kw_ref, vwt_ref, o_ref, *scratch):
    g = pl.program_id(0)
    step = pl.program_id(1)

    def block(sub, carry):
        rows = pl.ds(pl.multiple_of(sub * QUERY_BLOCK, QUERY_BLOCK), QUERY_BLOCK)
        _nsa_block(g, step * NSA_STEP_BLOCKS + sub, q_ref.at[rows, :], gt_ref.at[:, rows], kc_ref, vct_ref, ks_ref,
                   vst_ref, kw_ref, vwt_ref, o_ref.at[rows, :], *scratch)
        return carry

    lax.fori_loop(0, NSA_STEP_BLOCKS, block, 0)


def _nsa_block(g, i, q_ref, gt_ref, kc_ref, vct_ref, ks_ref, vst_ref, kw_ref, vwt_ref, o_ref,
               qa_ref, sel_ref, s0_ref, s1_ref, acc_ref, part_ref, pp_ref, gates_ref):
    s_refs = (s0_ref, s1_ref)
    qb = QUERY_BLOCK
    start = i * qb
    nb = sel_ref.shape[0]
    ncp = kc_ref.shape[1]
    s_len = ks_ref.shape[1]
    hq = NSA_HPG * qb
    t_row = start + lax.broadcasted_iota(jnp.int32, (1, qb), 1)

    qt = (q_ref[...].astype(F32) * (NSA_DIM ** -0.5 * LOG2E)).T
    q_t = jnp.concatenate([qt[h * NSA_DIM:(h + 1) * NSA_DIM] for h in range(NSA_HPG)], axis=1).astype(BF16)
    qa_ref[:NSA_DIM, :] = q_t
    qa_ref[NSA_DIM:, :] = jnp.zeros((qa_ref.shape[0] - NSA_DIM, hq), BF16)

    jt = t_row // SEL_BLOCK
    picked = -2.0

    gates_ref[...] = gt_ref[...]

    def gate(branch):
        rows = [gates_ref[pl.ds(g * (NSA_HPG * 3) + h * 3 + branch, 1), :] for h in range(NSA_HPG)]
        return jax.nn.sigmoid(jnp.concatenate(rows, axis=1))

    def exps(s, bias):
        out = []
        for h in range(NSA_HPG):
            sh = s[:, h * qb:(h + 1) * qb] + bias
            out.append(jnp.exp2(sh - jnp.max(sh, axis=0, keepdims=True)))
        return out

    def compressed_and_select(nck, nbk):
        wk = WINDOW + qb
        k0w = pl.multiple_of(jnp.maximum(start - WINDOW, 0), qb)
        sc = _dot(kc_ref[0, :nck, :], q_t)
        sw = _dot(kw_ref[0, pl.ds(k0w, wk), :], q_t)
        cmask = lax.broadcasted_iota(jnp.int32, (nck, qb), 0) * CMP_STRIDE + (CMP_BLOCK - 1) <= t_row
        e_cmp = exps(sc, jnp.where(cmask, 0.0, MASK_VALUE))
        o_cmp = _dot(vct_ref[0, :, :nck], jnp.concatenate([e.astype(BF16) for e in e_cmp], axis=1))
        kpos = k0w + lax.broadcasted_iota(jnp.int32, (wk, qb), 0)
        e_win = exps(sw, jnp.where((kpos <= t_row) & (kpos > t_row - WINDOW), 0.0, MASK_VALUE))
        o_win = _dot(vwt_ref[0, :, pl.ds(k0w, wk)], jnp.concatenate([e.astype(BF16) for e in e_win], axis=1))
        inv_cmp = jnp.where(jnp.concatenate([t_row >= CMP_BLOCK - 1] * NSA_HPG, axis=1),
                            1.0 / o_cmp[NSA_DIM:NSA_DIM + 1, :], 0.0)
        psum = e_cmp[0] * inv_cmp[:, :qb]
        for h in range(1, NSA_HPG):
            psum = psum + e_cmp[h] * inv_cmp[:, h * qb:(h + 1) * qb]
        part_ref[...] = (gate(0) * inv_cmp * o_cmp[:NSA_DIM]
                         + gate(2) * (1.0 / o_win[NSA_DIM:NSA_DIM + 1, :]) * o_win[:NSA_DIM])
        pp_ref[:SUBLANES, :] = jnp.zeros((SUBLANES, qb), F32)
        pp_ref[SUBLANES:SUBLANES + nck, :] = psum
        imp = None
        for k, w in enumerate(COV_BAND):
            tap = pp_ref[pl.ds(SUBLANES - COV_LEAD + k, nbk, stride=COV_RATIO), :]
            tap = tap if w == 1.0 else w * tap
            imp = tap if imp is None else imp + tap

        blk = lax.broadcasted_iota(jnp.int32, (nbk, qb), 0)
        forced = (blk == 0) | (blk == jt) | (blk == jt - 1)
        score = jnp.where(forced, picked, jnp.where(blk <= jt, imp, -1.0))
        blk_f = blk.astype(F32)

        def pick(_, score):
            mx = jnp.max(score, axis=0, keepdims=True)
            idx = jnp.min(jnp.where(score == mx, blk_f, float(nbk)), axis=0, keepdims=True)
            return jnp.where(blk_f == idx, picked, score)

        score = lax.fori_loop(0, min(SEL_TOPK, nbk) - N_FORCED, pick, score)
        sel_ref[:nbk, :] = jnp.where((score == picked) & (blk <= jt), 0.0, MASK_VALUE)

    bucket_ok = s_len % (NSA_BUCKETS * SEL_TILE) == 0 and nb // NSA_BUCKETS >= SEL_TOPK
    n_bucket = NSA_BUCKETS if bucket_ok else 1
    bucket = (start + qb - 1) // (s_len // n_bucket)
    for b in range(n_bucket):
        pl.when(bucket == b)(functools.partial(
            compressed_and_select, (b + 1) * (ncp // n_bucket), (b + 1) * (nb // n_bucket)))

    tk = SEL_TILE

    def scores(slot, kt):
        k0 = pl.multiple_of(kt * tk, tk)
        b8 = sel_ref[pl.ds(pl.multiple_of(kt * SEL_BPT, SEL_BPT), SEL_BPT), :]
        b16 = jnp.concatenate([b8, jnp.zeros((BF16_ROWS - SEL_BPT, qb), F32)], axis=0).astype(BF16)
        qa_ref[NSA_DIM:NSA_DIM + BF16_ROWS, :] = jnp.concatenate([b16] * NSA_HPG, axis=1)
        s = _dot(ks_ref[0, pl.ds(k0, tk), :], qa_ref[...])
        s_refs[slot][...] = s
        return jnp.max(s, axis=0, keepdims=True)

    def update(slot, kt, mt, m, causal):
        k0 = pl.multiple_of(kt * tk, tk)
        s = s_refs[slot][...]
        if causal:
            kpos = k0 + lax.broadcasted_iota(jnp.int32, (tk, qb), 0)
            s = s + jnp.concatenate([jnp.where(kpos <= t_row, 0.0, MASK_VALUE)] * NSA_HPG, axis=1)
            mt = jnp.max(s, axis=0, keepdims=True)
        m_new = jnp.maximum(m, mt)
        p = jnp.exp2(s - m_new).astype(BF16)
        acc_ref[...] = jnp.exp2(m - m_new) * acc_ref[...] + _dot(vst_ref[0, :, pl.ds(k0, tk)], p)
        return m_new

    def fused(cur, kt_cur, mt_cur, m, nxt, kt_nxt):
        k0c = pl.multiple_of(kt_cur * tk, tk)
        k0n = pl.multiple_of(kt_nxt * tk, tk)
        b8 = sel_ref[pl.ds(pl.multiple_of(kt_nxt * SEL_BPT, SEL_BPT), SEL_BPT), :]
        b16 = jnp.concatenate([b8, jnp.zeros((BF16_ROWS - SEL_BPT, qb), F32)], axis=0).astype(BF16)
        qa_ref[NSA_DIM:NSA_DIM + BF16_ROWS, :] = jnp.concatenate([b16] * NSA_HPG, axis=1)
        m_new = jnp.maximum(m, mt_cur)
        mt_nxt, pv = None, None
        for c in range(tk // SEL_CHUNK):
            rows = pl.ds(c * SEL_CHUNK, SEL_CHUNK)
            s_n = _dot(ks_ref[0, pl.ds(k0n + c * SEL_CHUNK, SEL_CHUNK), :], qa_ref[...])
            s_refs[nxt][rows, :] = s_n
            mt_c = jnp.max(s_n, axis=0, keepdims=True)
            mt_nxt = mt_c if mt_nxt is None else jnp.maximum(mt_nxt, mt_c)
            p = jnp.exp2(s_refs[cur][rows, :] - m_new).astype(BF16)
            pv_c = _dot(vst_ref[0, :, pl.ds(k0c + c * SEL_CHUNK, SEL_CHUNK)], p)
            pv = pv_c if pv is None else pv + pv_c
        acc_ref[...] = jnp.exp2(m - m_new) * acc_ref[...] + pv
        return mt_nxt, m_new

    def run(first, n_tiles, carry):
        mt, m = carry
        for k in range(n_tiles):
            mt, m = fused(k % 2, first + k, mt, m, (k + 1) % 2, first + k + 1)
        return mt, m

    acc_ref[...] = jnp.zeros(acc_ref.shape, F32)
    last = (start + qb - 1) // tk
    unroll = SEL_UNROLL
    carry = (scores(0, 0), jnp.full((1, hq), MASK_VALUE, F32))
    n_long = last // unroll
    carry = lax.fori_loop(0, n_long, lambda it, c: run(it * unroll, unroll, c), carry)
    n_short = (last - n_long * unroll) // 2
    mt_a, m = lax.fori_loop(0, n_short, lambda it, c: run(n_long * unroll + 2 * it, 2, c), carry)

    @pl.when(last % 2 == 0)
    def _():
        update(0, last, mt_a, m, True)

    @pl.when(last % 2 == 1)
    def _():
        mt_b = scores(1, last)
        update(1, last, mt_b, update(0, last - 1, mt_a, m, False), True)

    o_sel = acc_ref[:NSA_DIM, :] * (1.0 / acc_ref[NSA_DIM:NSA_DIM + 1, :])

    out_t = part_ref[...] + gate(1) * o_sel
    outs = [out_t[:, h * qb:(h + 1) * qb] for h in range(NSA_HPG)]
    o_ref[...] = jnp.concatenate(outs, axis=0).T.astype(o_ref.dtype)


def _nsa(q, gates_t, kc, vc_aug_t, ks_aug, vs_aug_t, kw, vw_aug_t):
    s = q.shape[0]
    assert s % (2 * SEL_TILE) == 0 and s >= WINDOW + QUERY_BLOCK
    gdim = NSA_HPG * NSA_DIM
    ncp = kc.shape[1]
    assert ncp * CMP_STRIDE == s and s // SEL_BLOCK * COV_RATIO == ncp
    nb = s // SEL_BLOCK
    ka = ks_aug.shape[-1]
    va = vs_aug_t.shape[1]
    hq = NSA_HPG * QUERY_BLOCK
    per_group = lambda shape: pl.BlockSpec((1,) + shape, lambda g, i: (g, 0, 0))
    rows = NSA_STEP_BLOCKS * QUERY_BLOCK
    assert s % rows == 0
    return pl.pallas_call(
        _nsa_kernel,
        grid=(NSA_GROUPS, s // rows),
        in_specs=[
            pl.BlockSpec((rows, gdim), lambda g, i: (i, g)),
            pl.BlockSpec((NSA_HEADS * 3, rows), lambda g, i: (0, i)),
            per_group((ncp, NSA_DIM)),
            per_group((va, ncp)),
            per_group((s, ka)),
            per_group((va, s)),
            per_group((s, NSA_DIM)),
            per_group((va, s)),
        ],
        out_specs=pl.BlockSpec((rows, gdim), lambda g, i: (i, g)),
        out_shape=jax.ShapeDtypeStruct((s, NSA_HEADS * NSA_DIM), BF16),
        scratch_shapes=[pltpu.VMEM((ka, hq), BF16), pltpu.VMEM((nb, QUERY_BLOCK), F32),
                        pltpu.VMEM((SEL_TILE, hq), F32), pltpu.VMEM((SEL_TILE, hq), F32),
                        pltpu.VMEM((va, hq), F32),
                        pltpu.VMEM((NSA_DIM, hq), F32),
                        pltpu.VMEM((SUBLANES + ncp, QUERY_BLOCK), F32),
                        pltpu.VMEM((NSA_HEADS * 3, QUERY_BLOCK), F32)],
        compiler_params=_params("parallel", "arbitrary"),
        name="nsa",
    )(q, gates_t, kc, vc_aug_t, ks_aug, vs_aug_t, kw, vw_aug_t)


def _sgu_kernel(u_ref, v_ref, lng_ref, lnb_ref, ws_ref, bs_ref, o_ref):
    c = SGU_CHUNK
    tm = u_ref.shape[0]
    v = jax.nn.gelu(v_ref[...])
    mu = jnp.mean(v, axis=-1, keepdims=True)
    var = jnp.mean(jnp.square(v - mu), axis=-1, keepdims=True)
    vn = ((v - mu) * lax.rsqrt(var + NORM_EPS) * lng_ref[...] + lnb_ref[...]).astype(BF16)
    u = jax.nn.gelu(u_ref[...])
    tri = lax.broadcasted_iota(jnp.int32, (c, c), 0) >= lax.broadcasted_iota(jnp.int32, (c, c), 1)
    for g in range(SGU_GROUPS):
        w = jnp.where(tri, ws_ref[g], 0.0).astype(BF16)
        cols = slice(g * c, (g + 1) * c)
        rhs = jnp.concatenate([vn[k * c:(k + 1) * c, cols] for k in range(tm // c)], axis=1)
        mixed = _dot(w, rhs)
        for k in range(tm // c):
            rows = slice(k * c, (k + 1) * c)
            o_ref[rows, cols] = (u[rows, cols] * (mixed[:, rows] + bs_ref[g])).astype(o_ref.dtype)


def _sgu(proj, lng, lnb, ws, bs, tm):
    s = proj.shape[0]
    w = lng.shape[0]
    c = SGU_CHUNK
    bs_b = jnp.broadcast_to(bs[:, :, None], (SGU_GROUPS, c, c))
    return pl.pallas_call(
        _sgu_kernel,
        grid=(s // tm,),
        in_specs=[
            pl.BlockSpec((tm, w), lambda i: (i, 0)),
            pl.BlockSpec((tm, w), lambda i: (i, 1)),
            pl.BlockSpec((1, w), lambda i: (0, 0)),
            pl.BlockSpec((1, w), lambda i: (0, 0)),
            pl.BlockSpec((SGU_GROUPS, c, c), lambda i: (0, 0, 0)),
            pl.BlockSpec((SGU_GROUPS, c, c), lambda i: (0, 0, 0)),
        ],
        out_specs=pl.BlockSpec((tm, w), lambda i: (i, 0)),
        out_shape=jax.ShapeDtypeStruct((s, w), BF16),
        compiler_params=_params("parallel"),
        name="sgu",
    )(proj, proj, lng.reshape(1, w), lnb.reshape(1, w), ws, bs_b)


def _mix_kernel(oa_ref, ob_ref, ga_ref, gb_ref, x_ref, pa_ref, pb_ref, wo_ref,
                gc_ref, wq_ref, mkt_ref, mv_ref, wmo_ref, o_ref):
    a = _dot(oa_ref[...], pa_ref[...])
    b = _dot(ob_ref[...], pb_ref[...])
    merged = jax.nn.sigmoid(ga_ref[...]) * a + jax.nn.sigmoid(gb_ref[...]) * b
    x = x_ref[...] + _dot(merged.astype(BF16), wo_ref[...])
    h = _rms(x, gc_ref[...]).astype(BF16)
    mq = (_dot(h, wq_ref[...]) * (MEM_DIM ** -0.5)).astype(BF16)
    outs = []
    for hh in range(MEM_HEADS):
        s = _dot(mq[:, hh * MEM_DIM:(hh + 1) * MEM_DIM], mkt_ref[hh])
        e = jnp.exp(s - jnp.max(s, axis=-1, keepdims=True))
        p = e / jnp.sum(e, axis=-1, keepdims=True)
        outs.append(_dot(p.astype(BF16), mv_ref[hh]).astype(BF16))
    o_ref[...] = x + _dot(jnp.concatenate(outs, axis=1), wmo_ref[...])


def _mix(o_a, o_b, proj, x, p_a, p_b, w_o, gate_block, g_cross, w_q, mk_t, mv, w_mo, tm):
    s, d = x.shape
    wa = o_a.shape[1]
    wb = o_b.shape[1]
    mw = w_q.shape[1]
    m = mv.shape[1]
    resident = lambda shape: pl.BlockSpec(shape, lambda i: (0,) * len(shape), pipeline_mode=pl.Buffered(1))
    return pl.pallas_call(
        _mix_kernel,
        grid=(s // tm,),
        in_specs=[
            pl.BlockSpec((tm, wa), lambda i: (i, 0)),
            pl.BlockSpec((tm, wb), lambda i: (i, 0)),
            pl.BlockSpec((tm, d), lambda i: (i, gate_block)),
            pl.BlockSpec((tm, d), lambda i: (i, gate_block + 1)),
            pl.BlockSpec((tm, d), lambda i: (i, 0)),
            resident((wa, d)),
            resident((wb, d)),
            resident((d, d)),
            resident((1, d)),
            resident((d, mw)),
            resident((MEM_HEADS, MEM_DIM, m)),
            resident((MEM_HEADS, m, MEM_DIM)),
            resident((mw, d)),
        ],
        out_specs=pl.BlockSpec((tm, d), lambda i: (i, 0)),
        out_shape=jax.ShapeDtypeStruct((s, d), F32),
        compiler_params=_params("parallel"),
        name="mix",
    )(o_a, o_b, proj, proj, x, p_a, p_b, w_o, g_cross.reshape(1, d), w_q, mk_t, mv, w_mo)


def _ffn_kernel(x_ref, g_ref, wg_ref, wu_ref, wo_ref, gf_ref, o_ref, h_ref, acc_ref, *, final_norm):
    j = pl.program_id(1)

    @pl.when(j == 0)
    def _():
        h_ref[...] = _rms(x_ref[...], g_ref[...]).astype(BF16)
        acc_ref[...] = jnp.zeros_like(acc_ref)

    h = h_ref[...]
    th = wg_ref.shape[1]
    out = None
    for c in range(th // FFN_CHUNK):
        cs = slice(c * FFN_CHUNK, (c + 1) * FFN_CHUNK)
        act = (jax.nn.silu(_dot(h, wg_ref[:, cs])) * _dot(h, wu_ref[:, cs])).astype(BF16)
        part = _dot(act, wo_ref[cs, :])
        out = part if out is None else out + part
    acc_ref[...] += out

    @pl.when(j == pl.num_programs(1) - 1)
    def _():
        y = x_ref[...] + acc_ref[...]
        o_ref[...] = _rms(y, gf_ref[...]) if final_norm else y


def _ffn(x, g, w_in, w_out, g_final, final_norm, tm, th):
    s, d = x.shape
    hidden = w_out.shape[0]
    nh = hidden // th
    return pl.pallas_call(
        functools.partial(_ffn_kernel, final_norm=final_norm),
        grid=(s // tm, nh),
        in_specs=[
            pl.BlockSpec((tm, d), lambda i, j: (i, 0)),
            pl.BlockSpec((1, d), lambda i, j: (0, 0)),
            pl.BlockSpec((d, th), lambda i, j: (0, j)),
            pl.BlockSpec((d, th), lambda i, j: (0, j + nh)),
            pl.BlockSpec((th, d), lambda i, j: (j, 0)),
            pl.BlockSpec((1, d), lambda i, j: (0, 0)),
        ],
        out_specs=pl.BlockSpec((tm, d), lambda i, j: (i, 0)),
        out_shape=jax.ShapeDtypeStruct((s, d), F32),
        scratch_shapes=[pltpu.VMEM((tm, d), BF16), pltpu.VMEM((tm, d), F32)],
        compiler_params=_params("parallel", "arbitrary"),
        name="ffn",
    )(x, g.reshape(1, d), w_in, w_in, w_out, g_final.reshape(1, d))


def _layer(x, mem, norm_mix_g, w_in, cmp_pe_k, cmp_k_w1, cmp_k_b1, cmp_k_w2, cmp_pe_v, cmp_v_w1, cmp_v_b1, cmp_v_w2,
           sgu_ln_g, sgu_ln_b, sgu_ws, sgu_b, w_proj_a, w_proj_b, w_mix_out, norm_cross_g, norm_mem_g,
           w_mq, w_mkv, w_mo, norm_ffn_g):
    s, d = x.shape
    qw = NSA_HEADS * NSA_DIM
    kvw = NSA_GROUPS * NSA_DIM
    sguw = sgu_ln_g.shape[0]
    ngate = NSA_HEADS * 3
    o_kv = qw
    o_gate = o_kv + N_KV_STREAMS * kvw
    o_u = o_gate + ngate

    w_a = w_in[:, :o_gate].astype(BF16)
    w_b = jnp.pad(w_in[:, o_u:].astype(BF16), ((0, 0), (0, PROJ_TN)))
    w_b = lax.dynamic_update_slice(w_b, w_in[:, o_gate:o_u].astype(BF16), (0, w_in.shape[1] - o_u))
    tm_proj = min(1024, s)
    nat, kw, ks_aug, vs_aug_t, vw_aug_t, q = _proj_attn(x, norm_mix_g, w_a, tm_proj)
    proj_b, gates_t = _proj_gate(x, norm_mix_g, w_b, tm_proj)

    ncp = s // CMP_STRIDE
    r = nat.reshape(2, NSA_GROUPS, ncp, CMP_STRIDE * NSA_DIM)
    w1 = jnp.stack([cmp_k_w1, cmp_v_w1]).astype(BF16)
    pe = jnp.stack([cmp_pe_k.reshape(1, -1), cmp_pe_v.reshape(1, -1)])
    b1 = jnp.stack([cmp_k_b1.reshape(1, -1), cmp_v_b1.reshape(1, -1)])
    w2 = jnp.stack([cmp_k_w2, cmp_v_w2]).astype(BF16)
    cmp, cmp_t = _compress(r, w1, pe, b1, w2)
    o_a = _nsa(q, gates_t, cmp[0], cmp_t[1], ks_aug, vs_aug_t, kw, vw_aug_t)

    o_b = _sgu(proj_b, sgu_ln_g, sgu_ln_b, sgu_ws, sgu_b, min(512, s))

    m = mem.shape[0]
    mw = MEM_HEADS * MEM_DIM
    mkv = _norm_matmul(mem, norm_mem_g, w_mkv.astype(BF16), F32, m, mw)
    mk_t = mkv[:, :mw].reshape(m, MEM_HEADS, MEM_DIM).transpose(1, 2, 0).astype(BF16)
    mv = mkv[:, mw:].reshape(m, MEM_HEADS, MEM_DIM).transpose(1, 0, 2).astype(BF16)
    return _mix(o_a, o_b, proj_b, x, w_proj_a.astype(BF16), w_proj_b.astype(BF16), w_mix_out.astype(BF16),
                (2 * sguw) // d, norm_cross_g, w_mq.astype(BF16), mk_t, mv, w_mo.astype(BF16), min(256, s))


def kernel(x, mem, norm_mix_g, w_in, cmp_pe_k, cmp_k_w1, cmp_k_b1, cmp_k_w2, cmp_pe_v, cmp_v_w1, cmp_v_b1, cmp_v_w2, sgu_ln_g, sgu_ln_b, sgu_ws, sgu_b, w_proj_a, w_proj_b, w_mix_out, norm_cross_g, norm_mem_g, w_mq, w_mkv, w_mo, norm_ffn_g, w_ffn_in, w_ffn_out, norm_final_g):
    b, s, d = x.shape
    depth = w_in.shape[0]
    outs = []
    for bi in range(b):
        xb = x[bi]
        for l in range(depth):
            last = l == depth - 1
            xb = _layer(xb, mem[bi], norm_mix_g[l], w_in[l], cmp_pe_k[l], cmp_k_w1[l], cmp_k_b1[l], cmp_k_w2[l],
                        cmp_pe_v[l], cmp_v_w1[l], cmp_v_b1[l], cmp_v_w2[l], sgu_ln_g[l], sgu_ln_b[l], sgu_ws[l],
                        sgu_b[l], w_proj_a[l], w_proj_b[l], w_mix_out[l], norm_cross_g[l], norm_mem_g[l],
                        w_mq[l], w_mkv[l], w_mo[l], norm_ffn_g[l])
            xb = _ffn(xb, norm_ffn_g[l], w_ffn_in[l].astype(BF16), w_ffn_out[l].astype(BF16), norm_final_g, last,
                      min(512, s), 512)
        outs.append(xb)
    return jnp.stack(outs)
```

```python
import functools
import math

import jax
import jax.numpy as jnp
from jax import lax
from jax.experimental import pallas as pl
from jax.experimental.pallas import tpu as pltpu

F32 = jnp.float32
BF16 = jnp.bfloat16

NORM_EPS = 1e-6
MASK_VALUE = -1e30
N_FORCED = 3
LOG2E = math.log2(math.e)

NSA_HEADS = 16
NSA_GROUPS = 4
NSA_HPG = NSA_HEADS // NSA_GROUPS
NSA_DIM = 64
N_KV_STREAMS = 6
CMP_BLOCK = 32
CMP_STRIDE = 16
SEL_BLOCK = 64
SEL_TOPK = 16
WINDOW = 512
QUERY_BLOCK = 128
SGU_GROUPS = 8
SGU_CHUNK = 128
MEM_HEADS = 4
MEM_DIM = 128

LANES = 128
SUBLANES = 8
COV_RATIO = SEL_BLOCK // CMP_STRIDE
COV_LEAD = CMP_BLOCK // CMP_STRIDE - 1
COV_BAND = tuple(
    max(min((k - COV_LEAD) * CMP_STRIDE + CMP_BLOCK, SEL_BLOCK) - max((k - COV_LEAD) * CMP_STRIDE, 0), 0) / CMP_BLOCK
    for k in range(COV_RATIO + COV_LEAD))
assert SEL_BLOCK % CMP_STRIDE == 0 and CMP_BLOCK % CMP_STRIDE == 0 and COV_LEAD <= SUBLANES
SEL_TILE = 512
SEL_BPT = SEL_TILE // SEL_BLOCK
SEL_CHUNK = 256
SEL_UNROLLS = (8, 4, 2)
NSA_STEP_BLOCKS = 4
NSA_BUCKETS = 4
BF16_ROWS = 16
FFN_CHUNK = 256
VMEM_LIMIT = 56 * 1024 * 1024


def _params(*sem):
    return pltpu.CompilerParams(dimension_semantics=sem, vmem_limit_bytes=VMEM_LIMIT)


def _rms(x, g):
    return x * lax.rsqrt(jnp.mean(x * x, axis=-1, keepdims=True) + NORM_EPS) * g


def _dot(a, b):
    return jnp.dot(a, b, preferred_element_type=F32)


def _ones_row(n):
    return jnp.where(lax.broadcasted_iota(jnp.int32, (BF16_ROWS, n), 0) == 0, 1.0, 0.0).astype(BF16)


def _norm_matmul_kernel(x_ref, g_ref, w_ref, o_ref, h_ref):
    @pl.when(pl.program_id(1) == 0)
    def _():
        h_ref[...] = _rms(x_ref[...], g_ref[...]).astype(BF16)

    o_ref[...] = _dot(h_ref[...], w_ref[...]).astype(o_ref.dtype)


def _norm_matmul(x, g, w, out_dtype, tm, tn):
    s, d = x.shape
    n = w.shape[1]
    return pl.pallas_call(
        _norm_matmul_kernel,
        grid=(s // tm, n // tn),
        in_specs=[
            pl.BlockSpec((tm, d), lambda i, j: (i, 0)),
            pl.BlockSpec((1, d), lambda i, j: (0, 0)),
            pl.BlockSpec((d, tn), lambda i, j: (0, j)),
        ],
        out_specs=pl.BlockSpec((tm, tn), lambda i, j: (i, j)),
        out_shape=jax.ShapeDtypeStruct((s, n), out_dtype),
        scratch_shapes=[pltpu.VMEM((tm, d), BF16)],
        compiler_params=_params("parallel", "arbitrary"),
        name="norm_matmul",
    )(x, g.reshape(1, d), w)


PROJ_TN = 512


def _proj_attn_kernel(x_ref, g_ref, w_ref, nat_ref, kw_ref, ksa_ref, vsa_ref, vwa_ref, q_ref, h_ref):
    i = pl.program_id(0)
    j = pl.program_id(1)
    tm = x_ref.shape[0]
    dh = NSA_DIM
    ng = NSA_GROUPS

    @pl.when(j == 0)
    def _():
        h_ref[...] = _rms(x_ref[...], g_ref[...]).astype(BF16)

    res = _dot(h_ref[...], w_ref[...])

    def cols(c):
        return res[:, c * dh:(c + 1) * dh].astype(BF16)

    def cols_t(c):
        slab = res[:, (c // 2) * 2 * dh:(c // 2 + 1) * 2 * dh].T
        return slab[(c % 2) * dh:(c % 2 + 1) * dh].astype(BF16)

    @pl.when(j == 0)
    def _():
        for c in range(2 * ng):
            nat_ref[c // ng, c % ng] = cols(c)

    @pl.when(j == 1)
    def _():
        pos = i * tm + lax.broadcasted_iota(jnp.int32, (tm, dh), 0)
        lane = lax.broadcasted_iota(jnp.int32, (tm, dh), 1)
        onehot = jnp.where((pos // SEL_BLOCK) % SEL_BPT == lane, 1.0, 0.0).astype(BF16)
        for g in range(ng):
            ksa_ref[g, :, :dh] = cols(g)
            ksa_ref[g, :, dh:] = onehot
            vsa_ref[g, :dh, :] = cols_t(ng + g)
            vsa_ref[g, dh:, :] = _ones_row(tm)

    @pl.when(j == 2)
    def _():
        for g in range(ng):
            kw_ref[g] = cols(g)
            vwa_ref[g, :dh, :] = cols_t(ng + g)
            vwa_ref[g, dh:, :] = _ones_row(tm)

    @pl.when(j >= 3)
    def _():
        q_ref[...] = res.astype(BF16)


def _proj_attn(x, g, w, tm):
    s, d = x.shape
    dh, ng, tn = NSA_DIM, NSA_GROUPS, PROJ_TN
    assert 2 * ng * dh == tn and w.shape[1] == 3 * tn + NSA_HEADS * dh
    nq = NSA_HEADS * dh // tn
    nj = 3 + nq
    shapes = [
        jax.ShapeDtypeStruct((2, ng, s, dh), BF16),
        jax.ShapeDtypeStruct((ng, s, dh), BF16),
        jax.ShapeDtypeStruct((ng, s, 2 * dh), BF16),
        jax.ShapeDtypeStruct((ng, dh + BF16_ROWS, s), BF16),
        jax.ShapeDtypeStruct((ng, dh + BF16_ROWS, s), BF16),
        jax.ShapeDtypeStruct((s, NSA_HEADS * dh), BF16),
    ]
    return pl.pallas_call(
        _proj_attn_kernel,
        grid=(s // tm, nj),
        in_specs=[
            pl.BlockSpec((tm, d), lambda i, j: (i, 0)),
            pl.BlockSpec((1, d), lambda i, j: (0, 0)),
            pl.BlockSpec((d, tn), lambda i, j: (0, (j + nq) % nj)),
        ],
        out_specs=[
            pl.BlockSpec((2, ng, tm, dh), lambda i, j: (0, 0, i, 0)),
            pl.BlockSpec((ng, tm, dh), lambda i, j: (0, i, 0)),
            pl.BlockSpec((ng, tm, 2 * dh), lambda i, j: (0, i, 0)),
            pl.BlockSpec((ng, dh + BF16_ROWS, tm), lambda i, j: (0, 0, i)),
            pl.BlockSpec((ng, dh + BF16_ROWS, tm), lambda i, j: (0, 0, i)),
            pl.BlockSpec((tm, tn), lambda i, j: (i, jnp.maximum(j - 3, 0))),
        ],
        out_shape=shapes,
        scratch_shapes=[pltpu.VMEM((tm, d), BF16)],
        compiler_params=_params("parallel", "arbitrary"),
        name="proj_attn",
    )(x, g.reshape(1, d), w)


def _proj_gate_kernel(x_ref, g_ref, w_ref, o_ref, gt_ref, h_ref):
    j = pl.program_id(1)
    last = pl.num_programs(1) - 1

    @pl.when(j == 0)
    def _():
        h_ref[...] = _rms(x_ref[...], g_ref[...]).astype(BF16)

    res = _dot(h_ref[...], w_ref[...])

    @pl.when(j < last)
    def _():
        o_ref[...] = res

    @pl.when(j == last)
    def _():
        gt_ref[...] = res[:, :LANES].T


def _proj_gate(x, g, w, tm):
    s, d = x.shape
    tn = PROJ_TN
    n = w.shape[1] - tn
    return pl.pallas_call(
        _proj_gate_kernel,
        grid=(s // tm, n // tn + 1),
        in_specs=[
            pl.BlockSpec((tm, d), lambda i, j: (i, 0)),
            pl.BlockSpec((1, d), lambda i, j: (0, 0)),
            pl.BlockSpec((d, tn), lambda i, j: (0, j)),
        ],
        out_specs=[
            pl.BlockSpec((tm, tn), lambda i, j: (i, jnp.minimum(j, n // tn - 1))),
            pl.BlockSpec((LANES, tm), lambda i, j: (0, i)),
        ],
        out_shape=[jax.ShapeDtypeStruct((s, n), F32), jax.ShapeDtypeStruct((LANES, s), F32)],
        scratch_shapes=[pltpu.VMEM((tm, d), BF16)],
        compiler_params=_params("parallel", "arbitrary"),
        name="proj_gate",
    )(x, g.reshape(1, d), w)


def _compress_kernel(r_ref, w1_ref, pe_ref, b1_ref, w2_ref, o_ref, ot_ref):
    half = r_ref.shape[-1]
    r = r_ref[0, 0]
    bias = _dot(pe_ref[0].astype(BF16), w1_ref[0]) + b1_ref[0]
    top = _dot(r, w1_ref[0, :half, :])
    bot = _dot(r, w1_ref[0, half:, :])
    ncp = r.shape[0]
    hid = top + pltpu.roll(bot, ncp - 1, 0) + bias
    out = _dot(jax.nn.gelu(hid).astype(BF16), w2_ref[0])
    o_ref[0, 0] = out.astype(o_ref.dtype)
    dh = out.shape[1]
    out_t = jnp.concatenate([out, jnp.zeros((ncp, LANES - dh), F32)], axis=1).T
    ot_ref[0, 0, :dh, :] = out_t[:dh].astype(ot_ref.dtype)
    ot_ref[0, 0, dh:, :] = _ones_row(ncp)


def _compress(r, w1, pe, b1, w2):
    _, g, ncp, half = r.shape
    hidden = w1.shape[-1]
    dh = w2.shape[-1]
    return pl.pallas_call(
        _compress_kernel,
        grid=(2, g),
        in_specs=[
            pl.BlockSpec((1, 1, ncp, half), lambda s, gg: (s, gg, 0, 0)),
            pl.BlockSpec((1, 2 * half, hidden), lambda s, gg: (s, 0, 0)),
            pl.BlockSpec((1, 1, 2 * half), lambda s, gg: (s, 0, 0)),
            pl.BlockSpec((1, 1, hidden), lambda s, gg: (s, 0, 0)),
            pl.BlockSpec((1, hidden, dh), lambda s, gg: (s, 0, 0)),
        ],
        out_specs=[pl.BlockSpec((1, 1, ncp, dh), lambda s, gg: (s, gg, 0, 0)),
                   pl.BlockSpec((1, 1, dh + BF16_ROWS, ncp), lambda s, gg: (s, gg, 0, 0))],
        out_shape=[jax.ShapeDtypeStruct((2, g, ncp, dh), BF16),
                   jax.ShapeDtypeStruct((2, g, dh + BF16_ROWS, ncp), BF16)],
        compiler_params=_params("parallel", "parallel"),
        name="compress",
    )(r, w1, pe, b1, w2)


def _nsa_kernel(q_ref, gt_ref, kc_ref, vct_ref, ks_ref, vst_ref, kw_ref, vwt_ref, o_ref, *scratch):
    g = pl.program_id(0)
    step = pl.program_id(1)

    def block(sub, carry):
        rows = pl.ds(pl.multiple_of(sub * QUERY_BLOCK, QUERY_BLOCK), QUERY_BLOCK)
        _nsa_block(g, step * NSA_STEP_BLOCKS + sub, q_ref.at[rows, :], gt_ref.at[:, rows], kc_ref, vct_ref, ks_ref,
                   vst_ref, kw_ref, vwt_ref, o_ref.at[rows, :], *scratch)
        return carry

    lax.fori_loop(0, NSA_STEP_BLOCKS, block, 0)


def _nsa_block(g, i, q_ref, gt_ref, kc_ref, vct_ref, ks_ref, vst_ref, kw_ref, vwt_ref, o_ref,
               qa_ref, sel_ref, s0_ref, s1_ref, acc_ref, part_ref, pp_ref, gates_ref):
    s_refs = (s0_ref, s1_ref)
    qb = QUERY_BLOCK
    start = i * qb
    nb = sel_ref.shape[0]
    ncp = kc_ref.shape[1]
    s_len = ks_ref.shape[1]
    hq = NSA_HPG * qb
    t_row = start + lax.broadcasted_iota(jnp.int32, (1, qb), 1)

    qt = (q_ref[...].astype(F32) * (NSA_DIM ** -0.5 * LOG2E)).T
    q_t = jnp.concatenate([qt[h * NSA_DIM:(h + 1) * NSA_DIM] for h in range(NSA_HPG)], axis=1).astype(BF16)
    qa_ref[:NSA_DIM, :] = q_t
    qa_ref[NSA_DIM:, :] = jnp.zeros((qa_ref.shape[0] - NSA_DIM, hq), BF16)

    jt = t_row // SEL_BLOCK
    picked = -2.0

    gates_ref[...] = gt_ref[...]

    def gate(branch):
        rows = [gates_ref[pl.ds(g * (NSA_HPG * 3) + h * 3 + branch, 1), :] for h in range(NSA_HPG)]
        return jax.nn.sigmoid(jnp.concatenate(rows, axis=1))

    def exps(s, bias):
        out = []
        for h in range(NSA_HPG):
            sh = s[:, h * qb:(h + 1) * qb] + bias
            out.append(jnp.exp2(sh - jnp.max(sh, axis=0, keepdims=True)))
        return out

    def compressed_and_select(nck, nbk):
        wk = WINDOW + qb
        k0w = pl.multiple_of(jnp.maximum(start - WINDOW, 0), qb)
        sc = _dot(kc_ref[0, :nck, :], q_t)
        sw = _dot(kw_ref[0, pl.ds(k0w, wk), :], q_t)
        cmask = lax.broadcasted_iota(jnp.int32, (nck, qb), 0) * CMP_STRIDE + (CMP_BLOCK - 1) <= t_row
        e_cmp = exps(sc, jnp.where(cmask, 0.0, MASK_VALUE))
        o_cmp = _dot(vct_ref[0, :, :nck], jnp.concatenate([e.astype(BF16) for e in e_cmp], axis=1))
        kpos = k0w + lax.broadcasted_iota(jnp.int32, (wk, qb), 0)
        e_win = exps(sw, jnp.where((kpos <= t_row) & (kpos > t_row - WINDOW), 0.0, MASK_VALUE))
        o_win = _dot(vwt_ref[0, :, pl.ds(k0w, wk)], jnp.concatenate([e.astype(BF16) for e in e_win], axis=1))
        inv_cmp = jnp.where(jnp.concatenate([t_row >= CMP_BLOCK - 1] * NSA_HPG, axis=1),
                            1.0 / o_cmp[NSA_DIM:NSA_DIM + 1, :], 0.0)
        psum = e_cmp[0] * inv_cmp[:, :qb]
        for h in range(1, NSA_HPG):
            psum = psum + e_cmp[h] * inv_cmp[:, h * qb:(h + 1) * qb]
        part_ref[...] = (gate(0) * inv_cmp * o_cmp[:NSA_DIM]
                         + gate(2) * (1.0 / o_win[NSA_DIM:NSA_DIM + 1, :]) * o_win[:NSA_DIM])
        pp_ref[:SUBLANES, :] = jnp.zeros((SUBLANES, qb), F32)
        pp_ref[SUBLANES:SUBLANES + nck, :] = psum
        imp = None
        for k, w in enumerate(COV_BAND):
            tap = pp_ref[pl.ds(SUBLANES - COV_LEAD + k, nbk, stride=COV_RATIO), :]
            tap = tap if w == 1.0 else w * tap
            imp = tap if imp is None else imp + tap

        blk = lax.broadcasted_iota(jnp.int32, (nbk, qb), 0)
        forced = (blk == 0) | (blk == jt) | (blk == jt - 1)
        score = jnp.where(forced, picked, jnp.where(blk <= jt, imp, -1.0))
        blk_f = blk.astype(F32)

        def pick(_, score):
            mx = jnp.max(score, axis=0, keepdims=True)
            idx = jnp.min(jnp.where(score == mx, blk_f, float(nbk)), axis=0, keepdims=True)
            return jnp.where(blk_f == idx, picked, score)

        score = lax.fori_loop(0, min(SEL_TOPK, nbk) - N_FORCED, pick, score)
        sel_ref[:nbk, :] = jnp.where((score == picked) & (blk <= jt), 0.0, MASK_VALUE)

    bucket_ok = s_len % (NSA_BUCKETS * SEL_TILE) == 0 and nb // NSA_BUCKETS >= SEL_TOPK
    n_bucket = NSA_BUCKETS if bucket_ok else 1
    bucket = (start + qb - 1) // (s_len // n_bucket)
    for b in range(n_bucket):
        pl.when(bucket == b)(functools.partial(
            compressed_and_select, (b + 1) * (ncp // n_bucket), (b + 1) * (nb // n_bucket)))

    tk = SEL_TILE

    def scores(slot, kt):
        k0 = pl.multiple_of(kt * tk, tk)
        b8 = sel_ref[pl.ds(pl.multiple_of(kt * SEL_BPT, SEL_BPT), SEL_BPT), :]
        b16 = jnp.concatenate([b8, jnp.zeros((BF16_ROWS - SEL_BPT, qb), F32)], axis=0).astype(BF16)
        qa_ref[NSA_DIM:NSA_DIM + BF16_ROWS, :] = jnp.concatenate([b16] * NSA_HPG, axis=1)
        s = _dot(ks_ref[0, pl.ds(k0, tk), :], qa_ref[...])
        s_refs[slot][...] = s
        return jnp.max(s, axis=0, keepdims=True)

    def update(slot, kt, mt, m, causal):
        k0 = pl.multiple_of(kt * tk, tk)
        s = s_refs[slot][...]
        if causal:
            kpos = k0 + lax.broadcasted_iota(jnp.int32, (tk, qb), 0)
            s = s + jnp.concatenate([jnp.where(kpos <= t_row, 0.0, MASK_VALUE)] * NSA_HPG, axis=1)
            mt = jnp.max(s, axis=0, keepdims=True)
        m_new = jnp.maximum(m, mt)
        p = jnp.exp2(s - m_new).astype(BF16)
        acc_ref[...] = jnp.exp2(m - m_new) * acc_ref[...] + _dot(vst_ref[0, :, pl.ds(k0, tk)], p)
        return m_new

    def accumulate(pend):
        p, k0, alpha = pend
        pv = _dot(vst_ref[0, :, pl.ds(k0, SEL_CHUNK)], p)
        acc_ref[...] = (acc_ref[...] if alpha is None else alpha * acc_ref[...]) + pv

    def run(first, n_tiles, carry):
        mt, m = carry
        pend = None
        for k in range(n_tiles):
            cur, nxt = s_refs[k % 2], s_refs[(k + 1) % 2]
            k0c = pl.multiple_of((first + k) * tk, tk)
            k0n = pl.multiple_of((first + k + 1) * tk, tk)
            b8 = sel_ref[pl.ds(pl.multiple_of((first + k + 1) * SEL_BPT, SEL_BPT), SEL_BPT), :]
            b16 = jnp.concatenate([b8, jnp.zeros((BF16_ROWS - SEL_BPT, qb), F32)], axis=0).astype(BF16)
            qa_ref[NSA_DIM:NSA_DIM + BF16_ROWS, :] = jnp.concatenate([b16] * NSA_HPG, axis=1)
            m_new = jnp.maximum(m, mt)
            alpha = jnp.exp2(m - m_new)
            mt = None
            for c in range(tk // SEL_CHUNK):
                rows = pl.ds(c * SEL_CHUNK, SEL_CHUNK)
                s_n = _dot(ks_ref[0, pl.ds(k0n + c * SEL_CHUNK, SEL_CHUNK), :], qa_ref[...])
                nxt[rows, :] = s_n
                mt_c = jnp.max(s_n, axis=0, keepdims=True)
                mt = mt_c if mt is None else jnp.maximum(mt, mt_c)
                p = jnp.exp2(cur[rows, :] - m_new).astype(BF16)
                if pend is not None:
                    accumulate(pend)
                pend = (p, k0c + c * SEL_CHUNK, alpha if c == 0 else None)
            m = m_new
        accumulate(pend)
        return mt, m

    acc_ref[...] = jnp.zeros(acc_ref.shape, F32)
    last = (start + qb - 1) // tk
    carry = (scores(0, 0), jnp.full((1, hq), MASK_VALUE, F32))
    done = 0
    for unroll in SEL_UNROLLS:
        trips = (last - done) // unroll
        carry = lax.fori_loop(0, trips, lambda it, c, d=done, u=unroll: run(d + it * u, u, c), carry)
        done = done + trips * unroll
    mt_a, m = carry

    @pl.when(last % 2 == 0)
    def _():
        update(0, last, mt_a, m, True)

    @pl.when(last % 2 == 1)
    def _():
        mt_b = scores(1, last)
        update(1, last, mt_b, update(0, last - 1, mt_a, m, False), True)

    o_sel = acc_ref[:NSA_DIM, :] * (1.0 / acc_ref[NSA_DIM:NSA_DIM + 1, :])

    out_t = part_ref[...] + gate(1) * o_sel
    outs = [out_t[:, h * qb:(h + 1) * qb] for h in range(NSA_HPG)]
    o_ref[...] = jnp.concatenate(outs, axis=0).T.astype(o_ref.dtype)


def _nsa(q, gates_t, kc, vc_aug_t, ks_aug, vs_aug_t, kw, vw_aug_t):
    s = q.shape[0]
    assert s % (2 * SEL_TILE) == 0 and s >= WINDOW + QUERY_BLOCK
    gdim = NSA_HPG * NSA_DIM
    ncp = kc.shape[1]
    assert ncp * CMP_STRIDE == s and s // SEL_BLOCK * COV_RATIO == ncp
    nb = s // SEL_BLOCK
    ka = ks_aug.shape[-1]
    va = vs_aug_t.shape[1]
    hq = NSA_HPG * QUERY_BLOCK
    per_group = lambda shape: pl.BlockSpec((1,) + shape, lambda g, i: (g, 0, 0))
    rows = NSA_STEP_BLOCKS * QUERY_BLOCK
    assert s % rows == 0
    return pl.pallas_call(
        _nsa_kernel,
        grid=(NSA_GROUPS, s // rows),
        in_specs=[
            pl.BlockSpec((rows, gdim), lambda g, i: (i, g)),
            pl.BlockSpec((NSA_HEADS * 3, rows), lambda g, i: (0, i)),
            per_group((ncp, NSA_DIM)),
            per_group((va, ncp)),
            per_group((s, ka)),
            per_group((va, s)),
            per_group((s, NSA_DIM)),
            per_group((va, s)),
        ],
        out_specs=pl.BlockSpec((rows, gdim), lambda g, i: (i, g)),
        out_shape=jax.ShapeDtypeStruct((s, NSA_HEADS * NSA_DIM), BF16),
        scratch_shapes=[pltpu.VMEM((ka, hq), BF16), pltpu.VMEM((nb, QUERY_BLOCK), F32),
                        pltpu.VMEM((SEL_TILE, hq), F32), pltpu.VMEM((SEL_TILE, hq), F32),
                        pltpu.VMEM((va, hq), F32),
                        pltpu.VMEM((NSA_DIM, hq), F32),
                        pltpu.VMEM((SUBLANES + ncp, QUERY_BLOCK), F32),
                        pltpu.VMEM((NSA_HEADS * 3, QUERY_BLOCK), F32)],
        compiler_params=_params("parallel", "arbitrary"),
        name="nsa",
    )(q, gates_t, kc, vc_aug_t, ks_aug, vs_aug_t, kw, vw_aug_t)


def _sgu_kernel(u_ref, v_ref, lng_ref, lnb_ref, ws_ref, bs_ref, o_ref):
    c = SGU_CHUNK
    tm = u_ref.shape[0]
    v = jax.nn.gelu(v_ref[...])
    mu = jnp.mean(v, axis=-1, keepdims=True)
    var = jnp.mean(jnp.square(v - mu), axis=-1, keepdims=True)
    vn = ((v - mu) * lax.rsqrt(var + NORM_EPS) * lng_ref[...] + lnb_ref[...]).astype(BF16)
    u = jax.nn.gelu(u_ref[...])
    tri = lax.broadcasted_iota(jnp.int32, (c, c), 0) >= lax.broadcasted_iota(jnp.int32, (c, c), 1)
    for g in range(SGU_GROUPS):
        w = jnp.where(tri, ws_ref[g], 0.0).astype(BF16)
        cols = slice(g * c, (g + 1) * c)
        rhs = jnp.concatenate([vn[k * c:(k + 1) * c, cols] for k in range(tm // c)], axis=1)
        mixed = _dot(w, rhs)
        for k in range(tm // c):
            rows = slice(k * c, (k + 1) * c)
            o_ref[rows, cols] = (u[rows, cols] * (mixed[:, rows] + bs_ref[g])).astype(o_ref.dtype)


def _sgu(proj, lng, lnb, ws, bs, tm):
    s = proj.shape[0]
    w = lng.shape[0]
    c = SGU_CHUNK
    bs_b = jnp.broadcast_to(bs[:, :, None], (SGU_GROUPS, c, c))
    return pl.pallas_call(
        _sgu_kernel,
        grid=(s // tm,),
        in_specs=[
            pl.BlockSpec((tm, w), lambda i: (i, 0)),
            pl.BlockSpec((tm, w), lambda i: (i, 1)),
            pl.BlockSpec((1, w), lambda i: (0, 0)),
            pl.BlockSpec((1, w), lambda i: (0, 0)),
            pl.BlockSpec((SGU_GROUPS, c, c), lambda i: (0, 0, 0)),
            pl.BlockSpec((SGU_GROUPS, c, c), lambda i: (0, 0, 0)),
        ],
        out_specs=pl.BlockSpec((tm, w), lambda i: (i, 0)),
        out_shape=jax.ShapeDtypeStruct((s, w), BF16),
        compiler_params=_params("parallel"),
        name="sgu",
    )(proj, proj, lng.reshape(1, w), lnb.reshape(1, w), ws, bs_b)


def _mix_kernel(oa_ref, ob_ref, ga_ref, gb_ref, x_ref, pa_ref, pb_ref, wo_ref,
                gc_ref, wq_ref, mkt_ref, mv_ref, wmo_ref, o_ref):
    a = _dot(oa_ref[...], pa_ref[...])
    b = _dot(ob_ref[...], pb_ref[...])
    merged = jax.nn.sigmoid(ga_ref[...]) * a + jax.nn.sigmoid(gb_ref[...]) * b
    x = x_ref[...] + _dot(merged.astype(BF16), wo_ref[...])
    h = _rms(x, gc_ref[...]).astype(BF16)
    mq = (_dot(h, wq_ref[...]) * (MEM_DIM ** -0.5)).astype(BF16)
    outs = []
    for hh in range(MEM_HEADS):
        s = _dot(mq[:, hh * MEM_DIM:(hh + 1) * MEM_DIM], mkt_ref[hh])
        e = jnp.exp(s - jnp.max(s, axis=-1, keepdims=True))
        p = e / jnp.sum(e, axis=-1, keepdims=True)
        outs.append(_dot(p.astype(BF16), mv_ref[hh]).astype(BF16))
    o_ref[...] = x + _dot(jnp.concatenate(outs, axis=1), wmo_ref[...])


def _mix(o_a, o_b, proj, x, p_a, p_b, w_o, gate_block, g_cross, w_q, mk_t, mv, w_mo, tm):
    s, d = x.shape
    wa = o_a.shape[1]
    wb = o_b.shape[1]
    mw = w_q.shape[1]
    m = mv.shape[1]
    resident = lambda shape: pl.BlockSpec(shape, lambda i: (0,) * len(shape), pipeline_mode=pl.Buffered(1))
    return pl.pallas_call(
        _mix_kernel,
        grid=(s // tm,),
        in_specs=[
            pl.BlockSpec((tm, wa), lambda i: (i, 0)),
            pl.BlockSpec((tm, wb), lambda i: (i, 0)),
            pl.BlockSpec((tm, d), lambda i: (i, gate_block)),
            pl.BlockSpec((tm, d), lambda i: (i, gate_block + 1)),
            pl.BlockSpec((tm, d), lambda i: (i, 0)),
            resident((wa, d)),
            resident((wb, d)),
            resident((d, d)),
            resident((1, d)),
            resident((d, mw)),
            resident((MEM_HEADS, MEM_DIM, m)),
            resident((MEM_HEADS, m, MEM_DIM)),
            resident((mw, d)),
        ],
        out_specs=pl.BlockSpec((tm, d), lambda i: (i, 0)),
        out_shape=jax.ShapeDtypeStruct((s, d), F32),
        compiler_params=_params("parallel"),
        name="mix",
    )(o_a, o_b, proj, proj, x, p_a, p_b, w_o, g_cross.reshape(1, d), w_q, mk_t, mv, w_mo)


def _ffn_kernel(x_ref, g_ref, wg_ref, wu_ref, wo_ref, gf_ref, o_ref, h_ref, acc_ref, *, final_norm):
    j = pl.program_id(1)

    @pl.when(j == 0)
    def _():
        h_ref[...] = _rms(x_ref[...], g_ref[...]).astype(BF16)
        acc_ref[...] = jnp.zeros_like(acc_ref)

    h = h_ref[...]
    th = wg_ref.shape[1]
    out = None
    for c in range(th // FFN_CHUNK):
        cs = slice(c * FFN_CHUNK, (c + 1) * FFN_CHUNK)
        act = (jax.nn.silu(_dot(h, wg_ref[:, cs])) * _dot(h, wu_ref[:, cs])).astype(BF16)
        part = _dot(act, wo_ref[cs, :])
        out = part if out is None else out + part
    acc_ref[...] += out

    @pl.when(j == pl.num_programs(1) - 1)
    def _():
        y = x_ref[...] + acc_ref[...]
        o_ref[...] = _rms(y, gf_ref[...]) if final_norm else y


def _ffn(x, g, w_in, w_out, g_final, final_norm, tm, th):
    s, d = x.shape
    hidden = w_out.shape[0]
    nh = hidden // th
    return pl.pallas_call(
        functools.partial(_ffn_kernel, final_norm=final_norm),
        grid=(s // tm, nh),
        in_specs=[
            pl.BlockSpec((tm, d), lambda i, j: (i, 0)),
            pl.BlockSpec((1, d), lambda i, j: (0, 0)),
            pl.BlockSpec((d, th), lambda i, j: (0, j)),
            pl.BlockSpec((d, th), lambda i, j: (0, j + nh)),
            pl.BlockSpec((th, d), lambda i, j: (j, 0)),
            pl.BlockSpec((1, d), lambda i, j: (0, 0)),
        ],
        out_specs=pl.BlockSpec((tm, d), lambda i, j: (i, 0)),
        out_shape=jax.ShapeDtypeStruct((s, d), F32),
        scratch_shapes=[pltpu.VMEM((tm, d), BF16), pltpu.VMEM((tm, d), F32)],
        compiler_params=_params("parallel", "arbitrary"),
        name="ffn",
    )(x, g.reshape(1, d), w_in, w_in, w_out, g_final.reshape(1, d))


def _layer(x, mem, norm_mix_g, w_in, cmp_pe_k, cmp_k_w1, cmp_k_b1, cmp_k_w2, cmp_pe_v, cmp_v_w1, cmp_v_b1, cmp_v_w2,
           sgu_ln_g, sgu_ln_b, sgu_ws, sgu_b, w_proj_a, w_proj_b, w_mix_out, norm_cross_g, norm_mem_g,
           w_mq, w_mkv, w_mo, norm_ffn_g):
    s, d = x.shape
    qw = NSA_HEADS * NSA_DIM
    kvw = NSA_GROUPS * NSA_DIM
    sguw = sgu_ln_g.shape[0]
    ngate = NSA_HEADS * 3
    o_kv = qw
    o_gate = o_kv + N_KV_STREAMS * kvw
    o_u = o_gate + ngate

    w_a = w_in[:, :o_gate].astype(BF16)
    w_b = jnp.pad(w_in[:, o_u:].astype(BF16), ((0, 0), (0, PROJ_TN)))
    w_b = lax.dynamic_update_slice(w_b, w_in[:, o_gate:o_u].astype(BF16), (0, w_in.shape[1] - o_u))
    tm_proj = min(1024, s)
    nat, kw, ks_aug, vs_aug_t, vw_aug_t, q = _proj_attn(x, norm_mix_g, w_a, tm_proj)
    proj_b, gates_t = _proj_gate(x, norm_mix_g, w_b, tm_proj)

    ncp = s // CMP_STRIDE
    r = nat.reshape(2, NSA_GROUPS, ncp, CMP_STRIDE * NSA_DIM)
    w1 = jnp.stack([cmp_k_w1, cmp_v_w1]).astype(BF16)
    pe = jnp.stack([cmp_pe_k.reshape(1, -1), cmp_pe_v.reshape(1, -1)])
    b1 = jnp.stack([cmp_k_b1.reshape(1, -1), cmp_v_b1.reshape(1, -1)])
    w2 = jnp.stack([cmp_k_w2, cmp_v_w2]).astype(BF16)
    cmp, cmp_t = _compress(r, w1, pe, b1, w2)
    o_a = _nsa(q, gates_t, cmp[0], cmp_t[1], ks_aug, vs_aug_t, kw, vw_aug_t)

    o_b = _sgu(proj_b, sgu_ln_g, sgu_ln_b, sgu_ws, sgu_b, min(512, s))

    m = mem.shape[0]
    mw = MEM_HEADS * MEM_DIM
    mkv = _norm_matmul(mem, norm_mem_g, w_mkv.astype(BF16), F32, m, mw)
    mk_t = mkv[:, :mw].reshape(m, MEM_HEADS, MEM_DIM).transpose(1, 2, 0).astype(BF16)
    mv = mkv[:, mw:].reshape(m, MEM_HEADS, MEM_DIM).transpose(1, 0, 2).astype(BF16)
    return _mix(o_a, o_b, proj_b, x, w_proj_a.astype(BF16), w_proj_b.astype(BF16), w_mix_out.astype(BF16),
                (2 * sguw) // d, norm_cross_g, w_mq.astype(BF16), mk_t, mv, w_mo.astype(BF16), min(256, s))


def kernel(x, mem, norm_mix_g, w_in, cmp_pe_k, cmp_k_w1, cmp_k_b1, cmp_k_w2, cmp_pe_v, cmp_v_w1, cmp_v_b1, cmp_v_w2, sgu_ln_g, sgu_ln_b, sgu_ws, sgu_b, w_proj_a, w_proj_b, w_mix_out, norm_cross_g, norm_mem_g, w_mq, w_mkv, w_mo, norm_ffn_g, w_ffn_in, w_ffn_out, norm_final_g):
    b, s, d = x.shape
    depth = w_in.shape[0]
    outs = []
    for bi in range(b):
        xb = x[bi]
        for l in range(depth):
            last = l == depth - 1
            xb = _layer(xb, mem[bi], norm_mix_g[l], w_in[l], cmp_pe_k[l], cmp_k_w1[l], cmp_k_b1[l], cmp_k_w2[l],
                        cmp_pe_v[l], cmp_v_w1[l], cmp_v_b1[l], cmp_v_w2[l], sgu_ln_g[l], sgu_ln_b[l], sgu_ws[l],
                        sgu_b[l], w_proj_a[l], w_proj_b[l], w_mix_out[l], norm_cross_g[l], norm_mem_g[l],
                        w_mq[l], w_mkv[l], w_mo[l], norm_ffn_g[l])
            xb = _ffn(xb, norm_ffn_g[l], w_ffn_in[l].astype(BF16), w_ffn_out[l].astype(BF16), norm_final_g, last,
                      min(512, s), 512)
        outs.append(xb)
    return jnp.stack(outs)
```

```python
import functools
import math

import jax
import jax.numpy as jnp
from jax import lax
from jax.experimental import pallas as pl
from jax.experimental.pallas import tpu as pltpu

F32 = jnp.float32
BF16 = jnp.bfloat16

NORM_EPS = 1e-6
MASK_VALUE = -1e30
N_FORCED = 3
LOG2E = math.log2(math.e)

NSA_HEADS = 16
NSA_GROUPS = 4
NSA_HPG = NSA_HEADS // NSA_GROUPS
NSA_DIM = 64
N_KV_STREAMS = 6
CMP_BLOCK = 32
CMP_STRIDE = 16
SEL_BLOCK = 64
SEL_TOPK = 16
WINDOW = 512
QUERY_BLOCK = 128
SGU_GROUPS = 8
SGU_CHUNK = 128
MEM_HEADS = 4
MEM_DIM = 128

LANES = 128
SUBLANES = 8
COV_RATIO = SEL_BLOCK // CMP_STRIDE
COV_LEAD = CMP_BLOCK // CMP_STRIDE - 1
COV_BAND = tuple(
    max(min((k - COV_LEAD) * CMP_STRIDE + CMP_BLOCK, SEL_BLOCK) - max((k - COV_LEAD) * CMP_STRIDE, 0), 0) / CMP_BLOCK
    for k in range(COV_RATIO + COV_LEAD))
assert SEL_BLOCK % CMP_STRIDE == 0 and CMP_BLOCK % CMP_STRIDE == 0 and COV_LEAD <= SUBLANES
SEL_TILE = 512
SEL_BPT = SEL_TILE // SEL_BLOCK
SEL_CHUNK = 256
SEL_UNROLLS = (8, 4, 2)
NSA_STEP_BLOCKS = 4
NSA_BUCKETS = 4
BF16_ROWS = 16
FFN_CHUNK = 256
VMEM_LIMIT = 56 * 1024 * 1024


def _params(*sem):
    return pltpu.CompilerParams(dimension_semantics=sem, vmem_limit_bytes=VMEM_LIMIT)


def _rms(x, g):
    return x * lax.rsqrt(jnp.mean(x * x, axis=-1, keepdims=True) + NORM_EPS) * g


def _dot(a, b):
    return jnp.dot(a, b, preferred_element_type=F32)


def _ones_row(n):
    return jnp.where(lax.broadcasted_iota(jnp.int32, (BF16_ROWS, n), 0) == 0, 1.0, 0.0).astype(BF16)


def _norm_matmul_kernel(x_ref, g_ref, w_ref, o_ref, h_ref):
    @pl.when(pl.program_id(1) == 0)
    def _():
        h_ref[...] = _rms(x_ref[...], g_ref[...]).astype(BF16)

    o_ref[...] = _dot(h_ref[...], w_ref[...]).astype(o_ref.dtype)


def _norm_matmul(x, g, w, out_dtype, tm, tn):
    s, d = x.shape
    n = w.shape[1]
    return pl.pallas_call(
        _norm_matmul_kernel,
        grid=(s // tm, n // tn),
        in_specs=[
            pl.BlockSpec((tm, d), lambda i, j: (i, 0)),
            pl.BlockSpec((1, d), lambda i, j: (0, 0)),
            pl.BlockSpec((d, tn), lambda i, j: (0, j)),
        ],
        out_specs=pl.BlockSpec((tm, tn), lambda i, j: (i, j)),
        out_shape=jax.ShapeDtypeStruct((s, n), out_dtype),
        scratch_shapes=[pltpu.VMEM((tm, d), BF16)],
        compiler_params=_params("parallel", "arbitrary"),
        name="norm_matmul",
    )(x, g.reshape(1, d), w)


PROJ_TN = 512


def _proj_attn_kernel(x_ref, g_ref, w_ref, nat_ref, kw_ref, ksa_ref, vsa_ref, vwa_ref, q_ref, h_ref):
    i = pl.program_id(0)
    j = pl.program_id(1)
    tm = x_ref.shape[0]
    dh = NSA_DIM
    ng = NSA_GROUPS

    @pl.when(j == 0)
    def _():
        h_ref[...] = _rms(x_ref[...], g_ref[...]).astype(BF16)

    res = _dot(h_ref[...], w_ref[...])

    def cols(c):
        return res[:, c * dh:(c + 1) * dh].astype(BF16)

    def cols_t(c):
        slab = res[:, (c // 2) * 2 * dh:(c // 2 + 1) * 2 * dh].T
        return slab[(c % 2) * dh:(c % 2 + 1) * dh].astype(BF16)

    @pl.when(j == 0)
    def _():
        for c in range(2 * ng):
            nat_ref[c // ng, c % ng] = cols(c)

    @pl.when(j == 1)
    def _():
        pos = i * tm + lax.broadcasted_iota(jnp.int32, (tm, dh), 0)
        lane = lax.broadcasted_iota(jnp.int32, (tm, dh), 1)
        onehot = jnp.where((pos // SEL_BLOCK) % SEL_BPT == lane, 1.0, 0.0).astype(BF16)
        for g in range(ng):
            ksa_ref[g, :, :dh] = cols(g)
            ksa_ref[g, :, dh:] = onehot
            vsa_ref[g, :dh, :] = cols_t(ng + g)
            vsa_ref[g, dh:, :] = _ones_row(tm)

    @pl.when(j == 2)
    def _():
        for g in range(ng):
            kw_ref[g] = cols(g)
            vwa_ref[g, :dh, :] = cols_t(ng + g)
            vwa_ref[g, dh:, :] = _ones_row(tm)

    @pl.when(j >= 3)
    def _():
        q_ref[...] = res.astype(BF16)


def _proj_attn(x, g, w, tm):
    s, d = x.shape
    dh, ng, tn = NSA_DIM, NSA_GROUPS, PROJ_TN
    assert 2 * ng * dh == tn and w.shape[1] == 3 * tn + NSA_HEADS * dh
    nq = NSA_HEADS * dh // tn
    nj = 3 + nq
    shapes = [
        jax.ShapeDtypeStruct((2, ng, s, dh), BF16),
        jax.ShapeDtypeStruct((ng, s, dh), BF16),
        jax.ShapeDtypeStruct((ng, s, 2 * dh), BF16),
        jax.ShapeDtypeStruct((ng, dh + BF16_ROWS, s), BF16),
        jax.ShapeDtypeStruct((ng, dh + BF16_ROWS, s), BF16),
        jax.ShapeDtypeStruct((s, NSA_HEADS * dh), BF16),
    ]
    return pl.pallas_call(
        _proj_attn_kernel,
        grid=(s // tm, nj),
        in_specs=[
            pl.BlockSpec((tm, d), lambda i, j: (i, 0)),
            pl.BlockSpec((1, d), lambda i, j: (0, 0)),
            pl.BlockSpec((d, tn), lambda i, j: (0, (j + nq) % nj)),
        ],
        out_specs=[
            pl.BlockSpec((2, ng, tm, dh), lambda i, j: (0, 0, i, 0)),
            pl.BlockSpec((ng, tm, dh), lambda i, j: (0, i, 0)),
            pl.BlockSpec((ng, tm, 2 * dh), lambda i, j: (0, i, 0)),
            pl.BlockSpec((ng, dh + BF16_ROWS, tm), lambda i, j: (0, 0, i)),
            pl.BlockSpec((ng, dh + BF16_ROWS, tm), lambda i, j: (0, 0, i)),
            pl.BlockSpec((tm, tn), lambda i, j: (i, jnp.maximum(j - 3, 0))),
        ],
        out_shape=shapes,
        scratch_shapes=[pltpu.VMEM((tm, d), BF16)],
        compiler_params=_params("parallel", "arbitrary"),
        name="proj_attn",
    )(x, g.reshape(1, d), w)


def _proj_gate_kernel(x_ref, g_ref, w_ref, o_ref, gt_ref, h_ref):
    j = pl.program_id(1)
    last = pl.num_programs(1) - 1

    @pl.when(j == 0)
    def _():
        h_ref[...] = _rms(x_ref[...], g_ref[...]).astype(BF16)

    res = _dot(h_ref[...], w_ref[...])

    @pl.when(j < last)
    def _():
        o_ref[...] = res

    @pl.when(j == last)
    def _():
        gt_ref[...] = res[:, :LANES].T


def _proj_gate(x, g, w, tm):
    s, d = x.shape
    tn = PROJ_TN
    n = w.shape[1] - tn
    return pl.pallas_call(
        _proj_gate_kernel,
        grid=(s // tm, n // tn + 1),
        in_specs=[
            pl.BlockSpec((tm, d), lambda i, j: (i, 0)),
            pl.BlockSpec((1, d), lambda i, j: (0, 0)),
            pl.BlockSpec((d, tn), lambda i, j: (0, j)),
        ],
        out_specs=[
            pl.BlockSpec((tm, tn), lambda i, j: (i, jnp.minimum(j, n // tn - 1))),
            pl.BlockSpec((LANES, tm), lambda i, j: (0, i)),
        ],
        out_shape=[jax.ShapeDtypeStruct((s, n), F32), jax.ShapeDtypeStruct((LANES, s), F32)],
        scratch_shapes=[pltpu.VMEM((tm, d), BF16)],
        compiler_params=_params("parallel", "arbitrary"),
        name="proj_gate",
    )(x, g.reshape(1, d), w)


def _gate_weights_kernel(a_ref, b_ref, o_ref, *, shift):
    k = pl.program_id(1)
    last = pl.num_programs(1) - 1
    rows, tn = o_ref.shape

    @pl.when(k < last)
    def _():
        o_ref[...] = jnp.concatenate([a_ref[:, shift:], b_ref[:, :shift]], axis=1).astype(BF16)

    @pl.when(k == last)
    def _():
        o_ref[...] = jnp.concatenate([a_ref[:, :shift], jnp.zeros((rows, tn - shift), F32)], axis=1).astype(BF16)


def _gate_weights(w, o_gate, o_u, tr):
    d, n = w.shape
    tn = PROJ_TN
    shift = o_u - o_gate
    assert o_gate % tn == 0 and (n - o_u) % tn == 0 and 0 < shift <= LANES
    base = o_gate // tn
    n_main = (n - o_u) // tn
    return pl.pallas_call(
        functools.partial(_gate_weights_kernel, shift=shift),
        grid=(d // tr, n_main + 1),
        in_specs=[
            pl.BlockSpec((tr, tn), lambda r, k: (r, base + k % n_main)),
            pl.BlockSpec((tr, LANES), lambda r, k: (r, (base + 1 + k % n_main) * (tn // LANES))),
        ],
        out_specs=pl.BlockSpec((tr, tn), lambda r, k: (r, k)),
        out_shape=jax.ShapeDtypeStruct((d, n - o_u + tn), BF16),
        compiler_params=_params("parallel", "arbitrary"),
        name="gate_weights",
    )(w, w)


def _compress_kernel(r_ref, w1_ref, pe_ref, b1_ref, w2_ref, o_ref, ot_ref):
    half = r_ref.shape[-1]
    r = r_ref[0, 0]
    bias = _dot(pe_ref[0].astype(BF16), w1_ref[0]) + b1_ref[0]
    top = _dot(r, w1_ref[0, :half, :])
    bot = _dot(r, w1_ref[0, half:, :])
    ncp = r.shape[0]
    hid = top + pltpu.roll(bot, ncp - 1, 0) + bias
    out = _dot(jax.nn.gelu(hid).astype(BF16), w2_ref[0])
    o_ref[0, 0] = out.astype(o_ref.dtype)
    dh = out.shape[1]
    out_t = jnp.concatenate([out, jnp.zeros((ncp, LANES - dh), F32)], axis=1).T
    ot_ref[0, 0, :dh, :] = out_t[:dh].astype(ot_ref.dtype)
    ot_ref[0, 0, dh:, :] = _ones_row(ncp)


def _compress(r, w1, pe, b1, w2):
    _, g, ncp, half = r.shape
    hidden = w1.shape[-1]
    dh = w2.shape[-1]
    return pl.pallas_call(
        _compress_kernel,
        grid=(2, g),
        in_specs=[
            pl.BlockSpec((1, 1, ncp, half), lambda s, gg: (s, gg, 0, 0)),
            pl.BlockSpec((1, 2 * half, hidden), lambda s, gg: (s, 0, 0)),
            pl.BlockSpec((1, 1, 2 * half), lambda s, gg: (s, 0, 0)),
            pl.BlockSpec((1, 1, hidden), lambda s, gg: (s, 0, 0)),
            pl.BlockSpec((1, hidden, dh), lambda s, gg: (s, 0, 0)),
        ],
        out_specs=[pl.BlockSpec((1, 1, ncp, dh), lambda s, gg: (s, gg, 0, 0)),
                   pl.BlockSpec((1, 1, dh + BF16_ROWS, ncp), lambda s, gg: (s, gg, 0, 0))],
        out_shape=[jax.ShapeDtypeStruct((2, g, ncp, dh), BF16),
                   jax.ShapeDtypeStruct((2, g, dh + BF16_ROWS, ncp), BF16)],
        compiler_params=_params("parallel", "parallel"),
        name="compress",
    )(r, w1, pe, b1, w2)


def _nsa_kernel(q_ref, gt_ref, kc_ref, vct_ref, ks_ref, vst_ref, kw_ref, vwt_ref, o_ref, *scratch):
    g = pl.program_id(0)
    step = pl.program_id(1)

    def block(sub, carry):
        rows = pl.ds(pl.multiple_of(sub * QUERY_BLOCK, QUERY_BLOCK), QUERY_BLOCK)
        _nsa_block(g, step * NSA_STEP_BLOCKS + sub, q_ref.at[rows, :], gt_ref.at[:, rows], kc_ref, vct_ref, ks_ref,
                   vst_ref, kw_ref, vwt_ref, o_ref.at[rows, :], *scratch)
        return carry

    lax.fori_loop(0, NSA_STEP_BLOCKS, block, 0)


def _nsa_block(g, i, q_ref, gt_ref, kc_ref, vct_ref, ks_ref, vst_ref, kw_ref, vwt_ref, o_ref,
               qa_ref, sel_ref, s0_ref, s1_ref, acc_ref, part_ref, pp_ref, gates_ref):
    s_refs = (s0_ref, s1_ref)
    qb = QUERY_BLOCK
    start = i * qb
    nb = sel_ref.shape[0]
    ncp = kc_ref.shape[1]
    s_len = ks_ref.shape[1]
    hq = NSA_HPG * qb
    t_row = start + lax.broadcasted_iota(jnp.int32, (1, qb), 1)

    qt = (q_ref[...].astype(F32) * (NSA_DIM ** -0.5 * LOG2E)).T
    q_t = jnp.concatenate([qt[h * NSA_DIM:(h + 1) * NSA_DIM] for h in range(NSA_HPG)], axis=1).astype(BF16)
    qa_ref[:NSA_DIM, :] = q_t
    qa_ref[NSA_DIM:, :] = jnp.zeros((qa_ref.shape[0] - NSA_DIM, hq), BF16)

    jt = t_row // SEL_BLOCK
    picked = -2.0

    gates_ref[...] = gt_ref[...]

    def gate(branch):
        rows = [gates_ref[pl.ds(g * (NSA_HPG * 3) + h * 3 + branch, 1), :] for h in range(NSA_HPG)]
        return jax.nn.sigmoid(jnp.concatenate(rows, axis=1))

    def exps(s, bias):
        out = []
        for h in range(NSA_HPG):
            sh = s[:, h * qb:(h + 1) * qb] + bias
            out.append(jnp.exp2(sh - jnp.max(sh, axis=0, keepdims=True)))
        return out

    def compressed_and_select(nck, nbk):
        wk = WINDOW + qb
        k0w = pl.multiple_of(jnp.maximum(start - WINDOW, 0), qb)
        sc = _dot(kc_ref[0, :nck, :], q_t)
        sw = _dot(kw_ref[0, pl.ds(k0w, wk), :], q_t)
        cmask = lax.broadcasted_iota(jnp.int32, (nck, qb), 0) * CMP_STRIDE + (CMP_BLOCK - 1) <= t_row
        e_cmp = exps(sc, jnp.where(cmask, 0.0, MASK_VALUE))
        o_cmp = _dot(vct_ref[0, :, :nck], jnp.concatenate([e.astype(BF16) for e in e_cmp], axis=1))
        kpos = k0w + lax.broadcasted_iota(jnp.int32, (wk, qb), 0)
        e_win = exps(sw, jnp.where((kpos <= t_row) & (kpos > t_row - WINDOW), 0.0, MASK_VALUE))
        o_win = _dot(vwt_ref[0, :, pl.ds(k0w, wk)], jnp.concatenate([e.astype(BF16) for e in e_win], axis=1))
        inv_cmp = jnp.where(jnp.concatenate([t_row >= CMP_BLOCK - 1] * NSA_HPG, axis=1),
                            1.0 / o_cmp[NSA_DIM:NSA_DIM + 1, :], 0.0)
        psum = e_cmp[0] * inv_cmp[:, :qb]
        for h in range(1, NSA_HPG):
            psum = psum + e_cmp[h] * inv_cmp[:, h * qb:(h + 1) * qb]
        part_ref[...] = (gate(0) * inv_cmp * o_cmp[:NSA_DIM]
                         + gate(2) * (1.0 / o_win[NSA_DIM:NSA_DIM + 1, :]) * o_win[:NSA_DIM])
        pp_ref[:SUBLANES, :] = jnp.zeros((SUBLANES, qb), F32)
        pp_ref[SUBLANES:SUBLANES + nck, :] = psum
        imp = None
        for k, w in enumerate(COV_BAND):
            tap = pp_ref[pl.ds(SUBLANES - COV_LEAD + k, nbk, stride=COV_RATIO), :]
            tap = tap if w == 1.0 else w * tap
            imp = tap if imp is None else imp + tap

        blk = lax.broadcasted_iota(jnp.int32, (nbk, qb), 0)
        forced = (blk == 0) | (blk == jt) | (blk == jt - 1)
        score = jnp.where(forced, picked, jnp.where(blk <= jt, imp, -1.0))
        blk_f = blk.astype(F32)

        def pick(_, score):
            mx = jnp.max(score, axis=0, keepdims=True)
            idx = jnp.min(jnp.where(score == mx, blk_f, float(nbk)), axis=0, keepdims=True)
            return jnp.where(blk_f == idx, picked, score)

        score = lax.fori_loop(0, min(SEL_TOPK, nbk) - N_FORCED, pick, score)
        sel_ref[:nbk, :] = jnp.where((score == picked) & (blk <= jt), 0.0, MASK_VALUE)

    bucket_ok = s_len % (NSA_BUCKETS * SEL_TILE) == 0 and nb // NSA_BUCKETS >= SEL_TOPK
    n_bucket = NSA_BUCKETS if bucket_ok else 1
    bucket = (start + qb - 1) // (s_len // n_bucket)
    for b in range(n_bucket):
        pl.when(bucket == b)(functools.partial(
            compressed_and_select, (b + 1) * (ncp // n_bucket), (b + 1) * (nb // n_bucket)))

    tk = SEL_TILE

    def scores(slot, kt):
        k0 = pl.multiple_of(kt * tk, tk)
        b8 = sel_ref[pl.ds(pl.multiple_of(kt * SEL_BPT, SEL_BPT), SEL_BPT), :]
        b16 = jnp.concatenate([b8, jnp.zeros((BF16_ROWS - SEL_BPT, qb), F32)], axis=0).astype(BF16)
        qa_ref[NSA_DIM:NSA_DIM + BF16_ROWS, :] = jnp.concatenate([b16] * NSA_HPG, axis=1)
        s = _dot(ks_ref[0, pl.ds(k0, tk), :], qa_ref[...])
        s_refs[slot][...] = s
        return jnp.max(s, axis=0, keepdims=True)

    def update(slot, kt, mt, m, causal):
        k0 = pl.multiple_of(kt * tk, tk)
        s = s_refs[slot][...]
        if causal:
            kpos = k0 + lax.broadcasted_iota(jnp.int32, (tk, qb), 0)
            s = s + jnp.concatenate([jnp.where(kpos <= t_row, 0.0, MASK_VALUE)] * NSA_HPG, axis=1)
            mt = jnp.max(s, axis=0, keepdims=True)
        m_new = jnp.maximum(m, mt)
        p = jnp.exp2(s - m_new).astype(BF16)
        acc_ref[...] = jnp.exp2(m - m_new) * acc_ref[...] + _dot(vst_ref[0, :, pl.ds(k0, tk)], p)
        return m_new

    def accumulate(pend):
        p, k0, alpha = pend
        pv = _dot(vst_ref[0, :, pl.ds(k0, SEL_CHUNK)], p)
        acc_ref[...] = (acc_ref[...] if alpha is None else alpha * acc_ref[...]) + pv

    def run(first, n_tiles, carry):
        mt, m = carry
        pend = None
        for k in range(n_tiles):
            cur, nxt = s_refs[k % 2], s_refs[(k + 1) % 2]
            k0c = pl.multiple_of((first + k) * tk, tk)
            k0n = pl.multiple_of((first + k + 1) * tk, tk)
            b8 = sel_ref[pl.ds(pl.multiple_of((first + k + 1) * SEL_BPT, SEL_BPT), SEL_BPT), :]
            b16 = jnp.concatenate([b8, jnp.zeros((BF16_ROWS - SEL_BPT, qb), F32)], axis=0).astype(BF16)
            qa_ref[NSA_DIM:NSA_DIM + BF16_ROWS, :] = jnp.concatenate([b16] * NSA_HPG, axis=1)
            m_new = jnp.maximum(m, mt)
            alpha = jnp.exp2(m - m_new)
            mt = None
            for c in range(tk // SEL_CHUNK):
                rows = pl.ds(c * SEL_CHUNK, SEL_CHUNK)
                s_n = _dot(ks_ref[0, pl.ds(k0n + c * SEL_CHUNK, SEL_CHUNK), :], qa_ref[...])
                nxt[rows, :] = s_n
                mt_c = jnp.max(s_n, axis=0, keepdims=True)
                mt = mt_c if mt is None else jnp.maximum(mt, mt_c)
                p = jnp.exp2(cur[rows, :] - m_new).astype(BF16)
                if pend is not None:
                    accumulate(pend)
                pend = (p, k0c + c * SEL_CHUNK, alpha if c == 0 else None)
            m = m_new
        accumulate(pend)
        return mt, m

    acc_ref[...] = jnp.zeros(acc_ref.shape, F32)
    last = (start + qb - 1) // tk
    carry = (scores(0, 0), jnp.full((1, hq), MASK_VALUE, F32))
    done = 0
    for unroll in SEL_UNROLLS:
        trips = (last - done) // unroll
        carry = lax.fori_loop(0, trips, lambda it, c, d=done, u=unroll: run(d + it * u, u, c), carry)
        done = done + trips * unroll
    mt_a, m = carry

    @pl.when(last % 2 == 0)
    def _():
        update(0, last, mt_a, m, True)

    @pl.when(last % 2 == 1)
    def _():
        mt_b = scores(1, last)
        update(1, last, mt_b, update(0, last - 1, mt_a, m, False), True)

    o_sel = acc_ref[:NSA_DIM, :] * (1.0 / acc_ref[NSA_DIM:NSA_DIM + 1, :])

    out_t = part_ref[...] + gate(1) * o_sel
    outs = [out_t[:, h * qb:(h + 1) * qb] for h in range(NSA_HPG)]
    o_ref[...] = jnp.concatenate(outs, axis=0).T.astype(o_ref.dtype)


def _nsa(q, gates_t, kc, vc_aug_t, ks_aug, vs_aug_t, kw, vw_aug_t):
    s = q.shape[0]
    assert s % (2 * SEL_TILE) == 0 and s >= WINDOW + QUERY_BLOCK
    gdim = NSA_HPG * NSA_DIM
    ncp = kc.shape[1]
    assert ncp * CMP_STRIDE == s and s // SEL_BLOCK * COV_RATIO == ncp
    nb = s // SEL_BLOCK
    ka = ks_aug.shape[-1]
    va = vs_aug_t.shape[1]
    hq = NSA_HPG * QUERY_BLOCK
    per_group = lambda shape: pl.BlockSpec((1,) + shape, lambda g, i: (g, 0, 0))
    rows = NSA_STEP_BLOCKS * QUERY_BLOCK
    assert s % rows == 0
    return pl.pallas_call(
        _nsa_kernel,
        grid=(NSA_GROUPS, s // rows),
        in_specs=[
            pl.BlockSpec((rows, gdim), lambda g, i: (i, g)),
            pl.BlockSpec((NSA_HEADS * 3, rows), lambda g, i: (0, i)),
            per_group((ncp, NSA_DIM)),
            per_group((va, ncp)),
            per_group((s, ka)),
            per_group((va, s)),
            per_group((s, NSA_DIM)),
            per_group((va, s)),
        ],
        out_specs=pl.BlockSpec((rows, gdim), lambda g, i: (i, g)),
        out_shape=jax.ShapeDtypeStruct((s, NSA_HEADS * NSA_DIM), BF16),
        scratch_shapes=[pltpu.VMEM((ka, hq), BF16), pltpu.VMEM((nb, QUERY_BLOCK), F32),
                        pltpu.VMEM((SEL_TILE, hq), F32), pltpu.VMEM((SEL_TILE, hq), F32),
                        pltpu.VMEM((va, hq), F32),
                        pltpu.VMEM((NSA_DIM, hq), F32),
                        pltpu.VMEM((SUBLANES + ncp, QUERY_BLOCK), F32),
                        pltpu.VMEM((NSA_HEADS * 3, QUERY_BLOCK), F32)],
        compiler_params=_params("parallel", "arbitrary"),
        name="nsa",
    )(q, gates_t, kc, vc_aug_t, ks_aug, vs_aug_t, kw, vw_aug_t)


def _sgu_kernel(u_ref, v_ref, lng_ref, lnb_ref, ws_ref, bs_ref, o_ref):
    c = SGU_CHUNK
    tm = u_ref.shape[0]
    v = jax.nn.gelu(v_ref[...])
    mu = jnp.mean(v, axis=-1, keepdims=True)
    var = jnp.mean(jnp.square(v - mu), axis=-1, keepdims=True)
    vn = ((v - mu) * lax.rsqrt(var + NORM_EPS) * lng_ref[...] + lnb_ref[...]).astype(BF16)
    u = jax.nn.gelu(u_ref[...])
    tri = lax.broadcasted_iota(jnp.int32, (c, c), 0) >= lax.broadcasted_iota(jnp.int32, (c, c), 1)
    for g in range(SGU_GROUPS):
        w = jnp.where(tri, ws_ref[g], 0.0).astype(BF16)
        cols = slice(g * c, (g + 1) * c)
        rhs = jnp.concatenate([vn[k * c:(k + 1) * c, cols] for k in range(tm // c)], axis=1)
        mixed = _dot(w, rhs)
        for k in range(tm // c):
            rows = slice(k * c, (k + 1) * c)
            o_ref[rows, cols] = (u[rows, cols] * (mixed[:, rows] + bs_ref[g])).astype(o_ref.dtype)


def _sgu(proj, lng, lnb, ws, bs, tm):
    s = proj.shape[0]
    w = lng.shape[0]
    c = SGU_CHUNK
    bs_b = jnp.broadcast_to(bs[:, :, None], (SGU_GROUPS, c, c))
    return pl.pallas_call(
        _sgu_kernel,
        grid=(s // tm,),
        in_specs=[
            pl.BlockSpec((tm, w), lambda i: (i, 0)),
            pl.BlockSpec((tm, w), lambda i: (i, 1)),
            pl.BlockSpec((1, w), lambda i: (0, 0)),
            pl.BlockSpec((1, w), lambda i: (0, 0)),
            pl.BlockSpec((SGU_GROUPS, c, c), lambda i: (0, 0, 0)),
            pl.BlockSpec((SGU_GROUPS, c, c), lambda i: (0, 0, 0)),
        ],
        out_specs=pl.BlockSpec((tm, w), lambda i: (i, 0)),
        out_shape=jax.ShapeDtypeStruct((s, w), BF16),
        compiler_params=_params("parallel"),
        name="sgu",
    )(proj, proj, lng.reshape(1, w), lnb.reshape(1, w), ws, bs_b)


def _mix_kernel(oa_ref, ob_ref, ga_ref, gb_ref, x_ref, pa_ref, pb_ref, wo_ref,
                gc_ref, wq_ref, mkt_ref, mv_ref, wmo_ref, o_ref):
    a = _dot(oa_ref[...], pa_ref[...])
    b = _dot(ob_ref[...], pb_ref[...])
    merged = jax.nn.sigmoid(ga_ref[...]) * a + jax.nn.sigmoid(gb_ref[...]) * b
    x = x_ref[...] + _dot(merged.astype(BF16), wo_ref[...])
    h = _rms(x, gc_ref[...]).astype(BF16)
    mq = (_dot(h, wq_ref[...]) * (MEM_DIM ** -0.5)).astype(BF16)
    outs = []
    for hh in range(MEM_HEADS):
        s = _dot(mq[:, hh * MEM_DIM:(hh + 1) * MEM_DIM], mkt_ref[hh])
        e = jnp.exp(s - jnp.max(s, axis=-1, keepdims=True))
        p = e / jnp.sum(e, axis=-1, keepdims=True)
        outs.append(_dot(p.astype(BF16), mv_ref[hh]).astype(BF16))
    o_ref[...] = x + _dot(jnp.concatenate(outs, axis=1), wmo_ref[...])


def _mix(o_a, o_b, proj, x, p_a, p_b, w_o, gate_block, g_cross, w_q, mk_t, mv, w_mo, tm):
    s, d = x.shape
    wa = o_a.shape[1]
    wb = o_b.shape[1]
    mw = w_q.shape[1]
    m = mv.shape[1]
    resident = lambda shape: pl.BlockSpec(shape, lambda i: (0,) * len(shape), pipeline_mode=pl.Buffered(1))
    return pl.pallas_call(
        _mix_kernel,
        grid=(s // tm,),
        in_specs=[
            pl.BlockSpec((tm, wa), lambda i: (i, 0)),
            pl.BlockSpec((tm, wb), lambda i: (i, 0)),
            pl.BlockSpec((tm, d), lambda i: (i, gate_block)),
            pl.BlockSpec((tm, d), lambda i: (i, gate_block + 1)),
            pl.BlockSpec((tm, d), lambda i: (i, 0)),
            resident((wa, d)),
            resident((wb, d)),
            resident((d, d)),
            resident((1, d)),
            resident((d, mw)),
            resident((MEM_HEADS, MEM_DIM, m)),
            resident((MEM_HEADS, m, MEM_DIM)),
            resident((mw, d)),
        ],
        out_specs=pl.BlockSpec((tm, d), lambda i: (i, 0)),
        out_shape=jax.ShapeDtypeStruct((s, d), F32),
        compiler_params=_params("parallel"),
        name="mix",
    )(o_a, o_b, proj, proj, x, p_a, p_b, w_o, g_cross.reshape(1, d), w_q, mk_t, mv, w_mo)


def _ffn_kernel(x_ref, g_ref, wg_ref, wu_ref, wo_ref, gf_ref, o_ref, h_ref, acc_ref, *, final_norm):
    j = pl.program_id(1)

    @pl.when(j == 0)
    def _():
        h_ref[...] = _rms(x_ref[...], g_ref[...]).astype(BF16)
        acc_ref[...] = jnp.zeros_like(acc_ref)

    h = h_ref[...]
    th = wg_ref.shape[1]
    out = None
    for c in range(th // FFN_CHUNK):
        cs = slice(c * FFN_CHUNK, (c + 1) * FFN_CHUNK)
        act = (jax.nn.silu(_dot(h, wg_ref[:, cs])) * _dot(h, wu_ref[:, cs])).astype(BF16)
        part = _dot(act, wo_ref[cs, :])
        out = part if out is None else out + part
    acc_ref[...] += out

    @pl.when(j == pl.num_programs(1) - 1)
    def _():
        y = x_ref[...] + acc_ref[...]
        o_ref[...] = _rms(y, gf_ref[...]) if final_norm else y


def _ffn(x, g, w_in, w_out, g_final, final_norm, tm, th):
    s, d = x.shape
    hidden = w_out.shape[0]
    nh = hidden // th
    return pl.pallas_call(
        functools.partial(_ffn_kernel, final_norm=final_norm),
        grid=(s // tm, nh),
        in_specs=[
            pl.BlockSpec((tm, d), lambda i, j: (i, 0)),
            pl.BlockSpec((1, d), lambda i, j: (0, 0)),
            pl.BlockSpec((d, th), lambda i, j: (0, j)),
            pl.BlockSpec((d, th), lambda i, j: (0, j + nh)),
            pl.BlockSpec((th, d), lambda i, j: (j, 0)),
            pl.BlockSpec((1, d), lambda i, j: (0, 0)),
        ],
        out_specs=pl.BlockSpec((tm, d), lambda i, j: (i, 0)),
        out_shape=jax.ShapeDtypeStruct((s, d), F32),
        scratch_shapes=[pltpu.VMEM((tm, d), BF16), pltpu.VMEM((tm, d), F32)],
        compiler_params=_params("parallel", "arbitrary"),
        name="ffn",
    )(x, g.reshape(1, d), w_in, w_in, w_out, g_final.reshape(1, d))


def _layer(x, mem, norm_mix_g, w_in, cmp_pe_k, cmp_k_w1, cmp_k_b1, cmp_k_w2, cmp_pe_v, cmp_v_w1, cmp_v_b1, cmp_v_w2,
           sgu_ln_g, sgu_ln_b, sgu_ws, sgu_b, w_proj_a, w_proj_b, w_mix_out, norm_cross_g, norm_mem_g,
           w_mq, w_mkv, w_mo, norm_ffn_g):
    s, d = x.shape
    qw = NSA_HEADS * NSA_DIM
    kvw = NSA_GROUPS * NSA_DIM
    sguw = sgu_ln_g.shape[0]
    ngate = NSA_HEADS * 3
    o_kv = qw
    o_gate = o_kv + N_KV_STREAMS * kvw
    o_u = o_gate + ngate

    w_a = w_in[:, :o_gate].astype(BF16)
    w_b = _gate_weights(w_in, o_gate, o_u, min(256, d))
    tm_proj = min(1024, s)
    nat, kw, ks_aug, vs_aug_t, vw_aug_t, q = _proj_attn(x, norm_mix_g, w_a, tm_proj)
    proj_b, gates_t = _proj_gate(x, norm_mix_g, w_b, tm_proj)

    ncp = s // CMP_STRIDE
    r = nat.reshape(2, NSA_GROUPS, ncp, CMP_STRIDE * NSA_DIM)
    w1 = jnp.stack([cmp_k_w1, cmp_v_w1]).astype(BF16)
    pe = jnp.stack([cmp_pe_k.reshape(1, -1), cmp_pe_v.reshape(1, -1)])
    b1 = jnp.stack([cmp_k_b1.reshape(1, -1), cmp_v_b1.reshape(1, -1)])
    w2 = jnp.stack([cmp_k_w2, cmp_v_w2]).astype(BF16)
    cmp, cmp_t = _compress(r, w1, pe, b1, w2)
    o_a = _nsa(q, gates_t, cmp[0], cmp_t[1], ks_aug, vs_aug_t, kw, vw_aug_t)

    o_b = _sgu(proj_b, sgu_ln_g, sgu_ln_b, sgu_ws, sgu_b, min(512, s))

    m = mem.shape[0]
    mw = MEM_HEADS * MEM_DIM
    mkv = _norm_matmul(mem, norm_mem_g, w_mkv.astype(BF16), F32, m, mw)
    mk_t = mkv[:, :mw].reshape(m, MEM_HEADS, MEM_DIM).transpose(1, 2, 0).astype(BF16)
    mv = mkv[:, mw:].reshape(m, MEM_HEADS, MEM_DIM).transpose(1, 0, 2).astype(BF16)
    return _mix(o_a, o_b, proj_b, x, w_proj_a.astype(BF16), w_proj_b.astype(BF16), w_mix_out.astype(BF16),
                (2 * sguw) // d, norm_cross_g, w_mq.astype(BF16), mk_t, mv, w_mo.astype(BF16), min(256, s))


def kernel(x, mem, norm_mix_g, w_in, cmp_pe_k, cmp_k_w1, cmp_k_b1, cmp_k_w2, cmp_pe_v, cmp_v_w1, cmp_v_b1, cmp_v_w2, sgu_ln_g, sgu_ln_b, sgu_ws, sgu_b, w_proj_a, w_proj_b, w_mix_out, norm_cross_g, norm_mem_g, w_mq, w_mkv, w_mo, norm_ffn_g, w_ffn_in, w_ffn_out, norm_final_g):
    b, s, d = x.shape
    depth = w_in.shape[0]
    outs = []
    for bi in range(b):
        xb = x[bi]
        for l in range(depth):
            last = l == depth - 1
            xb = _layer(xb, mem[bi], norm_mix_g[l], w_in[l], cmp_pe_k[l], cmp_k_w1[l], cmp_k_b1[l], cmp_k_w2[l],
                        cmp_pe_v[l], cmp_v_w1[l], cmp_v_b1[l], cmp_v_w2[l], sgu_ln_g[l], sgu_ln_b[l], sgu_ws[l],
                        sgu_b[l], w_proj_a[l], w_proj_b[l], w_mix_out[l], norm_cross_g[l], norm_mem_g[l],
                        w_mq[l], w_mkv[l], w_mo[l], norm_ffn_g[l])
            xb = _ffn(xb, norm_ffn_g[l], w_ffn_in[l].astype(BF16), w_ffn_out[l].astype(BF16), norm_final_g, last,
                      min(512, s), 512)
        outs.append(xb)
    return jnp.stack(outs)
```

```python
import functools
import math

import jax
import jax.numpy as jnp
from jax import lax
from jax.experimental import pallas as pl
from jax.experimental.pallas import tpu as pltpu

F32 = jnp.float32
BF16 = jnp.bfloat16

NORM_EPS = 1e-6
MASK_VALUE = -1e30
N_FORCED = 3
LOG2E = math.log2(math.e)

NSA_HEADS = 16
NSA_GROUPS = 4
NSA_HPG = NSA_HEADS // NSA_GROUPS
NSA_DIM = 64
N_KV_STREAMS = 6
CMP_BLOCK = 32
CMP_STRIDE = 16
SEL_BLOCK = 64
SEL_TOPK = 16
WINDOW = 512
QUERY_BLOCK = 128
SGU_GROUPS = 8
SGU_CHUNK = 128
MEM_HEADS = 4
MEM_DIM = 128

LANES = 128
SUBLANES = 8
COV_RATIO = SEL_BLOCK // CMP_STRIDE
COV_LEAD = CMP_BLOCK // CMP_STRIDE - 1
COV_BAND = tuple(
    max(min((k - COV_LEAD) * CMP_STRIDE + CMP_BLOCK, SEL_BLOCK) - max((k - COV_LEAD) * CMP_STRIDE, 0), 0) / CMP_BLOCK
    for k in range(COV_RATIO + COV_LEAD))
assert SEL_BLOCK % CMP_STRIDE == 0 and CMP_BLOCK % CMP_STRIDE == 0 and COV_LEAD <= SUBLANES
SEL_TILE = 512
SEL_BPT = SEL_TILE // SEL_BLOCK
SEL_CHUNK = 256
SEL_UNROLLS = (16, 8, 4, 2)
NSA_STEP_BLOCKS = 4
NSA_BUCKETS = 8
BF16_ROWS = 16
FFN_CHUNK = 256
VMEM_LIMIT = 56 * 1024 * 1024


def _params(*sem):
    return pltpu.CompilerParams(dimension_semantics=sem, vmem_limit_bytes=VMEM_LIMIT)


def _rms(x, g):
    return x * lax.rsqrt(jnp.mean(x * x, axis=-1, keepdims=True) + NORM_EPS) * g


def _dot(a, b):
    return jnp.dot(a, b, preferred_element_type=F32)


def _ones_row(n):
    return jnp.where(lax.broadcasted_iota(jnp.int32, (BF16_ROWS, n), 0) == 0, 1.0, 0.0).astype(BF16)


def _norm_matmul_kernel(x_ref, g_ref, w_ref, o_ref, h_ref):
    @pl.when(pl.program_id(1) == 0)
    def _():
        h_ref[...] = _rms(x_ref[...], g_ref[...]).astype(BF16)

    o_ref[...] = _dot(h_ref[...], w_ref[...]).astype(o_ref.dtype)


def _norm_matmul(x, g, w, out_dtype, tm, tn):
    s, d = x.shape
    n = w.shape[1]
    return pl.pallas_call(
        _norm_matmul_kernel,
        grid=(s // tm, n // tn),
        in_specs=[
            pl.BlockSpec((tm, d), lambda i, j: (i, 0)),
            pl.BlockSpec((1, d), lambda i, j: (0, 0)),
            pl.BlockSpec((d, tn), lambda i, j: (0, j)),
        ],
        out_specs=pl.BlockSpec((tm, tn), lambda i, j: (i, j)),
        out_shape=jax.ShapeDtypeStruct((s, n), out_dtype),
        scratch_shapes=[pltpu.VMEM((tm, d), BF16)],
        compiler_params=_params("parallel", "arbitrary"),
        name="norm_matmul",
    )(x, g.reshape(1, d), w)


PROJ_TN = 512


def _proj_attn_kernel(x_ref, g_ref, w_ref, nat_ref, kw_ref, ksa_ref, vsa_ref, vwa_ref, q_ref, h_ref):
    i = pl.program_id(0)
    j = pl.program_id(1)
    tm = x_ref.shape[0]
    dh = NSA_DIM
    ng = NSA_GROUPS

    @pl.when(j == 0)
    def _():
        h_ref[...] = _rms(x_ref[...], g_ref[...]).astype(BF16)

    res = _dot(h_ref[...], w_ref[...])

    def cols(c):
        return res[:, c * dh:(c + 1) * dh].astype(BF16)

    def cols_t(c):
        slab = res[:, (c // 2) * 2 * dh:(c // 2 + 1) * 2 * dh].T
        return slab[(c % 2) * dh:(c % 2 + 1) * dh].astype(BF16)

    @pl.when(j == 0)
    def _():
        for c in range(2 * ng):
            nat_ref[c // ng, c % ng] = cols(c)

    @pl.when(j == 1)
    def _():
        pos = i * tm + lax.broadcasted_iota(jnp.int32, (tm, dh), 0)
        lane = lax.broadcasted_iota(jnp.int32, (tm, dh), 1)
        onehot = jnp.where((pos // SEL_BLOCK) % SEL_BPT == lane, 1.0, 0.0).astype(BF16)
        for g in range(ng):
            ksa_ref[g, :, :dh] = cols(g)
            ksa_ref[g, :, dh:] = onehot
            vsa_ref[g, :dh, :] = cols_t(ng + g)
            vsa_ref[g, dh:, :] = _ones_row(tm)

    @pl.when(j == 2)
    def _():
        for g in range(ng):
            kw_ref[g] = cols(g)
            vwa_ref[g, :dh, :] = cols_t(ng + g)
            vwa_ref[g, dh:, :] = _ones_row(tm)

    @pl.when(j >= 3)
    def _():
        q_ref[...] = res.astype(BF16)


def _proj_attn(x, g, w, tm):
    s, d = x.shape
    dh, ng, tn = NSA_DIM, NSA_GROUPS, PROJ_TN
    assert 2 * ng * dh == tn and w.shape[1] == 3 * tn + NSA_HEADS * dh
    nq = NSA_HEADS * dh // tn
    nj = 3 + nq
    shapes = [
        jax.ShapeDtypeStruct((2, ng, s, dh), BF16),
        jax.ShapeDtypeStruct((ng, s, dh), BF16),
        jax.ShapeDtypeStruct((ng, s, 2 * dh), BF16),
        jax.ShapeDtypeStruct((ng, dh + BF16_ROWS, s), BF16),
        jax.ShapeDtypeStruct((ng, dh + BF16_ROWS, s), BF16),
        jax.ShapeDtypeStruct((s, NSA_HEADS * dh), BF16),
    ]
    return pl.pallas_call(
        _proj_attn_kernel,
        grid=(s // tm, nj),
        in_specs=[
            pl.BlockSpec((tm, d), lambda i, j: (i, 0)),
            pl.BlockSpec((1, d), lambda i, j: (0, 0)),
            pl.BlockSpec((d, tn), lambda i, j: (0, (j + nq) % nj)),
        ],
        out_specs=[
            pl.BlockSpec((2, ng, tm, dh), lambda i, j: (0, 0, i, 0)),
            pl.BlockSpec((ng, tm, dh), lambda i, j: (0, i, 0)),
            pl.BlockSpec((ng, tm, 2 * dh), lambda i, j: (0, i, 0)),
            pl.BlockSpec((ng, dh + BF16_ROWS, tm), lambda i, j: (0, 0, i)),
            pl.BlockSpec((ng, dh + BF16_ROWS, tm), lambda i, j: (0, 0, i)),
            pl.BlockSpec((tm, tn), lambda i, j: (i, jnp.maximum(j - 3, 0))),
        ],
        out_shape=shapes,
        scratch_shapes=[pltpu.VMEM((tm, d), BF16)],
        compiler_params=_params("parallel", "arbitrary"),
        name="proj_attn",
    )(x, g.reshape(1, d), w)


def _proj_gate_kernel(x_ref, g_ref, w_ref, o_ref, gt_ref, h_ref):
    j = pl.program_id(1)
    last = pl.num_programs(1) - 1

    @pl.when(j == 0)
    def _():
        h_ref[...] = _rms(x_ref[...], g_ref[...]).astype(BF16)

    res = _dot(h_ref[...], w_ref[...])

    @pl.when(j < last)
    def _():
        o_ref[...] = res

    @pl.when(j == last)
    def _():
        gt_ref[...] = res[:, :LANES].T


def _proj_gate(x, g, w, tm):
    s, d = x.shape
    tn = PROJ_TN
    n = w.shape[1] - tn
    return pl.pallas_call(
        _proj_gate_kernel,
        grid=(s // tm, n // tn + 1),
        in_specs=[
            pl.BlockSpec((tm, d), lambda i, j: (i, 0)),
            pl.BlockSpec((1, d), lambda i, j: (0, 0)),
            pl.BlockSpec((d, tn), lambda i, j: (0, j)),
        ],
        out_specs=[
            pl.BlockSpec((tm, tn), lambda i, j: (i, jnp.minimum(j, n // tn - 1))),
            pl.BlockSpec((LANES, tm), lambda i, j: (0, i)),
        ],
        out_shape=[jax.ShapeDtypeStruct((s, n), F32), jax.ShapeDtypeStruct((LANES, s), F32)],
        scratch_shapes=[pltpu.VMEM((tm, d), BF16)],
        compiler_params=_params("parallel", "arbitrary"),
        name="proj_gate",
    )(x, g.reshape(1, d), w)


def _compress_kernel(r_ref, w1_ref, pe_ref, b1_ref, w2_ref, o_ref, ot_ref):
    half = r_ref.shape[-1]
    r = r_ref[0, 0]
    bias = _dot(pe_ref[0].astype(BF16), w1_ref[0]) + b1_ref[0]
    top = _dot(r, w1_ref[0, :half, :])
    bot = _dot(r, w1_ref[0, half:, :])
    ncp = r.shape[0]
    hid = top + pltpu.roll(bot, ncp - 1, 0) + bias
    out = _dot(jax.nn.gelu(hid).astype(BF16), w2_ref[0])
    o_ref[0, 0] = out.astype(o_ref.dtype)
    dh = out.shape[1]
    out_t = jnp.concatenate([out, jnp.zeros((ncp, LANES - dh), F32)], axis=1).T
    ot_ref[0, 0, :dh, :] = out_t[:dh].astype(ot_ref.dtype)
    ot_ref[0, 0, dh:, :] = _ones_row(ncp)


def _compress(r, w1, pe, b1, w2):
    _, g, ncp, half = r.shape
    hidden = w1.shape[-1]
    dh = w2.shape[-1]
    return pl.pallas_call(
        _compress_kernel,
        grid=(2, g),
        in_specs=[
            pl.BlockSpec((1, 1, ncp, half), lambda s, gg: (s, gg, 0, 0)),
            pl.BlockSpec((1, 2 * half, hidden), lambda s, gg: (s, 0, 0)),
            pl.BlockSpec((1, 1, 2 * half), lambda s, gg: (s, 0, 0)),
            pl.BlockSpec((1, 1, hidden), lambda s, gg: (s, 0, 0)),
            pl.BlockSpec((1, hidden, dh), lambda s, gg: (s, 0, 0)),
        ],
        out_specs=[pl.BlockSpec((1, 1, ncp, dh), lambda s, gg: (s, gg, 0, 0)),
                   pl.BlockSpec((1, 1, dh + BF16_ROWS, ncp), lambda s, gg: (s, gg, 0, 0))],
        out_shape=[jax.ShapeDtypeStruct((2, g, ncp, dh), BF16),
                   jax.ShapeDtypeStruct((2, g, dh + BF16_ROWS, ncp), BF16)],
        compiler_params=_params("parallel", "parallel"),
        name="compress",
    )(r, w1, pe, b1, w2)


def _nsa_kernel(q_ref, gt_ref, kc_ref, vct_ref, ks_ref, vst_ref, kw_ref, vwt_ref, o_ref, *scratch):
    g = pl.program_id(0)
    step = pl.program_id(1)

    def block(sub, carry):
        rows = pl.ds(pl.multiple_of(sub * QUERY_BLOCK, QUERY_BLOCK), QUERY_BLOCK)
        _nsa_block(g, step * NSA_STEP_BLOCKS + sub, q_ref.at[rows, :], gt_ref.at[:, rows], kc_ref, vct_ref, ks_ref,
                   vst_ref, kw_ref, vwt_ref, o_ref.at[rows, :], *scratch)
        return carry

    lax.fori_loop(0, NSA_STEP_BLOCKS, block, 0)


def _nsa_block(g, i, q_ref, gt_ref, kc_ref, vct_ref, ks_ref, vst_ref, kw_ref, vwt_ref, o_ref,
               qa_ref, sel_ref, s0_ref, s1_ref, acc_ref, part_ref, pp_ref, gates_ref):
    s_refs = (s0_ref, s1_ref)
    qb = QUERY_BLOCK
    start = i * qb
    nb = sel_ref.shape[0]
    ncp = kc_ref.shape[1]
    s_len = ks_ref.shape[1]
    hq = NSA_HPG * qb
    t_row = start + lax.broadcasted_iota(jnp.int32, (1, qb), 1)

    qt = (q_ref[...].astype(F32) * (NSA_DIM ** -0.5 * LOG2E)).T
    q_t = jnp.concatenate([qt[h * NSA_DIM:(h + 1) * NSA_DIM] for h in range(NSA_HPG)], axis=1).astype(BF16)
    qa_ref[:NSA_DIM, :] = q_t
    qa_ref[NSA_DIM:, :] = jnp.zeros((qa_ref.shape[0] - NSA_DIM, hq), BF16)

    jt = t_row // SEL_BLOCK
    picked = -2.0

    gates_ref[...] = gt_ref[...]

    def gate(branch):
        rows = [gates_ref[pl.ds(g * (NSA_HPG * 3) + h * 3 + branch, 1), :] for h in range(NSA_HPG)]
        return jax.nn.sigmoid(jnp.concatenate(rows, axis=1))

    def exps(s, bias):
        out = []
        for h in range(NSA_HPG):
            sh = s[:, h * qb:(h + 1) * qb] + bias
            out.append(jnp.exp2(sh - jnp.max(sh, axis=0, keepdims=True)))
        return out

    def compressed_and_select(nck, nbk):
        wk = WINDOW + qb
        k0w = pl.multiple_of(jnp.maximum(start - WINDOW, 0), qb)
        sc = _dot(kc_ref[0, :nck, :], q_t)
        sw = _dot(kw_ref[0, pl.ds(k0w, wk), :], q_t)
        cmask = lax.broadcasted_iota(jnp.int32, (nck, qb), 0) * CMP_STRIDE + (CMP_BLOCK - 1) <= t_row
        e_cmp = exps(sc, jnp.where(cmask, 0.0, MASK_VALUE))
        o_cmp = _dot(vct_ref[0, :, :nck], jnp.concatenate([e.astype(BF16) for e in e_cmp], axis=1))
        kpos = k0w + lax.broadcasted_iota(jnp.int32, (wk, qb), 0)
        e_win = exps(sw, jnp.where((kpos <= t_row) & (kpos > t_row - WINDOW), 0.0, MASK_VALUE))
        o_win = _dot(vwt_ref[0, :, pl.ds(k0w, wk)], jnp.concatenate([e.astype(BF16) for e in e_win], axis=1))
        inv_cmp = jnp.where(jnp.concatenate([t_row >= CMP_BLOCK - 1] * NSA_HPG, axis=1),
                            1.0 / o_cmp[NSA_DIM:NSA_DIM + 1, :], 0.0)
        psum = e_cmp[0] * inv_cmp[:, :qb]
        for h in range(1, NSA_HPG):
            psum = psum + e_cmp[h] * inv_cmp[:, h * qb:(h + 1) * qb]
        part_ref[...] = (gate(0) * inv_cmp * o_cmp[:NSA_DIM]
                         + gate(2) * (1.0 / o_win[NSA_DIM:NSA_DIM + 1, :]) * o_win[:NSA_DIM])
        pp_ref[:SUBLANES, :] = jnp.zeros((SUBLANES, qb), F32)
        pp_ref[SUBLANES:SUBLANES + nck, :] = psum
        imp = None
        for k, w in enumerate(COV_BAND):
            tap = pp_ref[pl.ds(SUBLANES - COV_LEAD + k, nbk, stride=COV_RATIO), :]
            tap = tap if w == 1.0 else w * tap
            imp = tap if imp is None else imp + tap

        blk = lax.broadcasted_iota(jnp.int32, (nbk, qb), 0)
        forced = (blk == 0) | (blk == jt) | (blk == jt - 1)
        score = jnp.where(forced, picked, jnp.where(blk <= jt, imp, -1.0))
        blk_f = blk.astype(F32)

        def pick(_, score):
            mx = jnp.max(score, axis=0, keepdims=True)
            idx = jnp.min(jnp.where(score == mx, blk_f, float(nbk)), axis=0, keepdims=True)
            return jnp.where(blk_f == idx, picked, score)

        score = lax.fori_loop(0, min(SEL_TOPK, nbk) - N_FORCED, pick, score)
        sel_ref[:nbk, :] = jnp.where((score == picked) & (blk <= jt), 0.0, MASK_VALUE)

    bucket_ok = s_len % (NSA_BUCKETS * SEL_TILE) == 0 and nb // NSA_BUCKETS >= SEL_TOPK
    n_bucket = NSA_BUCKETS if bucket_ok else 1
    bucket = (start + qb - 1) // (s_len // n_bucket)
    for b in range(n_bucket):
        pl.when(bucket == b)(functools.partial(
            compressed_and_select, (b + 1) * (ncp // n_bucket), (b + 1) * (nb // n_bucket)))

    tk = SEL_TILE

    def scores(slot, kt):
        k0 = pl.multiple_of(kt * tk, tk)
        b8 = sel_ref[pl.ds(pl.multiple_of(kt * SEL_BPT, SEL_BPT), SEL_BPT), :]
        b16 = jnp.concatenate([b8, jnp.zeros((BF16_ROWS - SEL_BPT, qb), F32)], axis=0).astype(BF16)
        qa_ref[NSA_DIM:NSA_DIM + BF16_ROWS, :] = jnp.concatenate([b16] * NSA_HPG, axis=1)
        s = _dot(ks_ref[0, pl.ds(k0, tk), :], qa_ref[...])
        s_refs[slot][...] = s
        return jnp.max(s, axis=0, keepdims=True)

    def update(slot, kt, mt, m, causal):
        k0 = pl.multiple_of(kt * tk, tk)
        s = s_refs[slot][...]
        if causal:
            kpos = k0 + lax.broadcasted_iota(jnp.int32, (tk, qb), 0)
            s = s + jnp.concatenate([jnp.where(kpos <= t_row, 0.0, MASK_VALUE)] * NSA_HPG, axis=1)
            mt = jnp.max(s, axis=0, keepdims=True)
        m_new = jnp.maximum(m, mt)
        p = jnp.exp2(s - m_new).astype(BF16)
        acc_ref[...] = jnp.exp2(m - m_new) * acc_ref[...] + _dot(vst_ref[0, :, pl.ds(k0, tk)], p)
        return m_new

    def accumulate(pend):
        p, k0, alpha = pend
        pv = _dot(vst_ref[0, :, pl.ds(k0, SEL_CHUNK)], p)
        acc_ref[...] = (acc_ref[...] if alpha is None else alpha * acc_ref[...]) + pv

    def run(first, n_tiles, carry):
        mt, m = carry
        pend = None
        for k in range(n_tiles):
            cur, nxt = s_refs[k % 2], s_refs[(k + 1) % 2]
            k0c = pl.multiple_of((first + k) * tk, tk)
            k0n = pl.multiple_of((first + k + 1) * tk, tk)
            b8 = sel_ref[pl.ds(pl.multiple_of((first + k + 1) * SEL_BPT, SEL_BPT), SEL_BPT), :]
            b16 = jnp.concatenate([b8, jnp.zeros((BF16_ROWS - SEL_BPT, qb), F32)], axis=0).astype(BF16)
            qa_ref[NSA_DIM:NSA_DIM + BF16_ROWS, :] = jnp.concatenate([b16] * NSA_HPG, axis=1)
            m_new = jnp.maximum(m, mt)
            alpha = jnp.exp2(m - m_new)
            mt = None
            for c in range(tk // SEL_CHUNK):
                rows = pl.ds(c * SEL_CHUNK, SEL_CHUNK)
                s_n = _dot(ks_ref[0, pl.ds(k0n + c * SEL_CHUNK, SEL_CHUNK), :], qa_ref[...])
                nxt[rows, :] = s_n
                mt_c = jnp.max(s_n, axis=0, keepdims=True)
                mt = mt_c if mt is None else jnp.maximum(mt, mt_c)
                p = jnp.exp2(cur[rows, :] - m_new).astype(BF16)
                if pend is not None:
                    accumulate(pend)
                pend = (p, k0c + c * SEL_CHUNK, alpha if c == 0 else None)
            m = m_new
        accumulate(pend)
        return mt, m

    acc_ref[...] = jnp.zeros(acc_ref.shape, F32)
    last = (start + qb - 1) // tk
    carry = (scores(0, 0), jnp.full((1, hq), MASK_VALUE, F32))
    done = 0
    for unroll in SEL_UNROLLS:
        trips = (last - done) // unroll
        carry = lax.fori_loop(0, trips, lambda it, c, d=done, u=unroll: run(d + it * u, u, c), carry)
        done = done + trips * unroll
    mt_a, m = carry

    @pl.when(last % 2 == 0)
    def _():
        update(0, last, mt_a, m, True)

    @pl.when(last % 2 == 1)
    def _():
        mt_b = scores(1, last)
        update(1, last, mt_b, update(0, last - 1, mt_a, m, False), True)

    o_sel = acc_ref[:NSA_DIM, :] * (1.0 / acc_ref[NSA_DIM:NSA_DIM + 1, :])

    out_t = part_ref[...] + gate(1) * o_sel
    outs = [out_t[:, h * qb:(h + 1) * qb] for h in range(NSA_HPG)]
    o_ref[...] = jnp.concatenate(outs, axis=0).T.astype(o_ref.dtype)


def _nsa(q, gates_t, kc, vc_aug_t, ks_aug, vs_aug_t, kw, vw_aug_t):
    s = q.shape[0]
    assert s % (2 * SEL_TILE) == 0 and s >= WINDOW + QUERY_BLOCK
    gdim = NSA_HPG * NSA_DIM
    ncp = kc.shape[1]
    assert ncp * CMP_STRIDE == s and s // SEL_BLOCK * COV_RATIO == ncp
    nb = s // SEL_BLOCK
    ka = ks_aug.shape[-1]
    va = vs_aug_t.shape[1]
    hq = NSA_HPG * QUERY_BLOCK
    per_group = lambda shape: pl.BlockSpec((1,) + shape, lambda g, i: (g, 0, 0))
    rows = NSA_STEP_BLOCKS * QUERY_BLOCK
    assert s % rows == 0
    return pl.pallas_call(
        _nsa_kernel,
        grid=(NSA_GROUPS, s // rows),
        in_specs=[
            pl.BlockSpec((rows, gdim), lambda g, i: (i, g)),
            pl.BlockSpec((NSA_HEADS * 3, rows), lambda g, i: (0, i)),
            per_group((ncp, NSA_DIM)),
            per_group((va, ncp)),
            per_group((s, ka)),
            per_group((va, s)),
            per_group((s, NSA_DIM)),
            per_group((va, s)),
        ],
        out_specs=pl.BlockSpec((rows, gdim), lambda g, i: (i, g)),
        out_shape=jax.ShapeDtypeStruct((s, NSA_HEADS * NSA_DIM), BF16),
        scratch_shapes=[pltpu.VMEM((ka, hq), BF16), pltpu.VMEM((nb, QUERY_BLOCK), F32),
                        pltpu.VMEM((SEL_TILE, hq), F32), pltpu.VMEM((SEL_TILE, hq), F32),
                        pltpu.VMEM((va, hq), F32),
                        pltpu.VMEM((NSA_DIM, hq), F32),
                        pltpu.VMEM((SUBLANES + ncp, QUERY_BLOCK), F32),
                        pltpu.VMEM((NSA_HEADS * 3, QUERY_BLOCK), F32)],
        compiler_params=_params("parallel", "arbitrary"),
        name="nsa",
    )(q, gates_t, kc, vc_aug_t, ks_aug, vs_aug_t, kw, vw_aug_t)


def _sgu_kernel(u_ref, v_ref, lng_ref, lnb_ref, ws_ref, bs_ref, o_ref):
    c = SGU_CHUNK
    tm = u_ref.shape[0]
    v = jax.nn.gelu(v_ref[...])
    mu = jnp.mean(v, axis=-1, keepdims=True)
    var = jnp.mean(jnp.square(v - mu), axis=-1, keepdims=True)
    vn = ((v - mu) * lax.rsqrt(var + NORM_EPS) * lng_ref[...] + lnb_ref[...]).astype(BF16)
    u = jax.nn.gelu(u_ref[...])
    tri = lax.broadcasted_iota(jnp.int32, (c, c), 0) >= lax.broadcasted_iota(jnp.int32, (c, c), 1)
    for g in range(SGU_GROUPS):
        w = jnp.where(tri, ws_ref[g], 0.0).astype(BF16)
        cols = slice(g * c, (g + 1) * c)
        rhs = jnp.concatenate([vn[k * c:(k + 1) * c, cols] for k in range(tm // c)], axis=1)
        mixed = _dot(w, rhs)
        for k in range(tm // c):
            rows = slice(k * c, (k + 1) * c)
            o_ref[rows, cols] = (u[rows, cols] * (mixed[:, rows] + bs_ref[g])).astype(o_ref.dtype)


def _sgu(proj, lng, lnb, ws, bs, tm):
    s = proj.shape[0]
    w = lng.shape[0]
    c = SGU_CHUNK
    bs_b = jnp.broadcast_to(bs[:, :, None], (SGU_GROUPS, c, c))
    return pl.pallas_call(
        _sgu_kernel,
        grid=(s // tm,),
        in_specs=[
            pl.BlockSpec((tm, w), lambda i: (i, 0)),
            pl.BlockSpec((tm, w), lambda i: (i, 1)),
            pl.BlockSpec((1, w), lambda i: (0, 0)),
            pl.BlockSpec((1, w), lambda i: (0, 0)),
            pl.BlockSpec((SGU_GROUPS, c, c), lambda i: (0, 0, 0)),
            pl.BlockSpec((SGU_GROUPS, c, c), lambda i: (0, 0, 0)),
        ],
        out_specs=pl.BlockSpec((tm, w), lambda i: (i, 0)),
        out_shape=jax.ShapeDtypeStruct((s, w), BF16),
        compiler_params=_params("parallel"),
        name="sgu",
    )(proj, proj, lng.reshape(1, w), lnb.reshape(1, w), ws, bs_b)


def _mix_kernel(oa_ref, ob_ref, ga_ref, gb_ref, x_ref, pa_ref, pb_ref, wo_ref,
                gc_ref, wq_ref, mkt_ref, mv_ref, wmo_ref, o_ref):
    a = _dot(oa_ref[...], pa_ref[...])
    b = _dot(ob_ref[...], pb_ref[...])
    merged = jax.nn.sigmoid(ga_ref[...]) * a + jax.nn.sigmoid(gb_ref[...]) * b
    x = x_ref[...] + _dot(merged.astype(BF16), wo_ref[...])
    h = _rms(x, gc_ref[...]).astype(BF16)
    mq = (_dot(h, wq_ref[...]) * (MEM_DIM ** -0.5)).astype(BF16)
    outs = []
    for hh in range(MEM_HEADS):
        s = _dot(mq[:, hh * MEM_DIM:(hh + 1) * MEM_DIM], mkt_ref[hh])
        e = jnp.exp(s - jnp.max(s, axis=-1, keepdims=True))
        p = e / jnp.sum(e, axis=-1, keepdims=True)
        outs.append(_dot(p.astype(BF16), mv_ref[hh]).astype(BF16))
    o_ref[...] = x + _dot(jnp.concatenate(outs, axis=1), wmo_ref[...])


def _mix(o_a, o_b, proj, x, p_a, p_b, w_o, gate_block, g_cross, w_q, mk_t, mv, w_mo, tm):
    s, d = x.shape
    wa = o_a.shape[1]
    wb = o_b.shape[1]
    mw = w_q.shape[1]
    m = mv.shape[1]
    resident = lambda shape: pl.BlockSpec(shape, lambda i: (0,) * len(shape), pipeline_mode=pl.Buffered(1))
    return pl.pallas_call(
        _mix_kernel,
        grid=(s // tm,),
        in_specs=[
            pl.BlockSpec((tm, wa), lambda i: (i, 0)),
            pl.BlockSpec((tm, wb), lambda i: (i, 0)),
            pl.BlockSpec((tm, d), lambda i: (i, gate_block)),
            pl.BlockSpec((tm, d), lambda i: (i, gate_block + 1)),
            pl.BlockSpec((tm, d), lambda i: (i, 0)),
            resident((wa, d)),
            resident((wb, d)),
            resident((d, d)),
            resident((1, d)),
            resident((d, mw)),
            resident((MEM_HEADS, MEM_DIM, m)),
            resident((MEM_HEADS, m, MEM_DIM)),
            resident((mw, d)),
        ],
        out_specs=pl.BlockSpec((tm, d), lambda i: (i, 0)),
        out_shape=jax.ShapeDtypeStruct((s, d), F32),
        compiler_params=_params("parallel"),
        name="mix",
    )(o_a, o_b, proj, proj, x, p_a, p_b, w_o, g_cross.reshape(1, d), w_q, mk_t, mv, w_mo)


def _ffn_kernel(x_ref, g_ref, wg_ref, wu_ref, wo_ref, gf_ref, o_ref, h_ref, acc_ref, *, final_norm):
    j = pl.program_id(1)

    @pl.when(j == 0)
    def _():
        h_ref[...] = _rms(x_ref[...], g_ref[...]).astype(BF16)
        acc_ref[...] = jnp.zeros_like(acc_ref)

    h = h_ref[...]
    th = wg_ref.shape[1]
    out = None
    for c in range(th // FFN_CHUNK):
        cs = slice(c * FFN_CHUNK, (c + 1) * FFN_CHUNK)
        act = (jax.nn.silu(_dot(h, wg_ref[:, cs])) * _dot(h, wu_ref[:, cs])).astype(BF16)
        part = _dot(act, wo_ref[cs, :])
        out = part if out is None else out + part
    acc_ref[...] += out

    @pl.when(j == pl.num_programs(1) - 1)
    def _():
        y = x_ref[...] + acc_ref[...]
        o_ref[...] = _rms(y, gf_ref[...]) if final_norm else y


def _ffn(x, g, w_in, w_out, g_final, final_norm, tm, th):
    s, d = x.shape
    hidden = w_out.shape[0]
    nh = hidden // th
    return pl.pallas_call(
        functools.partial(_ffn_kernel, final_norm=final_norm),
        grid=(s // tm, nh),
        in_specs=[
            pl.BlockSpec((tm, d), lambda i, j: (i, 0)),
            pl.BlockSpec((1, d), lambda i, j: (0, 0)),
            pl.BlockSpec((d, th), lambda i, j: (0, j)),
            pl.BlockSpec((d, th), lambda i, j: (0, j + nh)),
            pl.BlockSpec((th, d), lambda i, j: (j, 0)),
            pl.BlockSpec((1, d), lambda i, j: (0, 0)),
        ],
        out_specs=pl.BlockSpec((tm, d), lambda i, j: (i, 0)),
        out_shape=jax.ShapeDtypeStruct((s, d), F32),
        scratch_shapes=[pltpu.VMEM((tm, d), BF16), pltpu.VMEM((tm, d), F32)],
        compiler_params=_params("parallel", "arbitrary"),
        name="ffn",
    )(x, g.reshape(1, d), w_in, w_in, w_out, g_final.reshape(1, d))


def _layer(x, mem, norm_mix_g, w_in, cmp_pe_k, cmp_k_w1, cmp_k_b1, cmp_k_w2, cmp_pe_v, cmp_v_w1, cmp_v_b1, cmp_v_w2,
           sgu_ln_g, sgu_ln_b, sgu_ws, sgu_b, w_proj_a, w_proj_b, w_mix_out, norm_cross_g, norm_mem_g,
           w_mq, w_mkv, w_mo, norm_ffn_g):
    s, d = x.shape
    qw = NSA_HEADS * NSA_DIM
    kvw = NSA_GROUPS * NSA_DIM
    sguw = sgu_ln_g.shape[0]
    ngate = NSA_HEADS * 3
    o_kv = qw
    o_gate = o_kv + N_KV_STREAMS * kvw
    o_u = o_gate + ngate

    w_a = w_in[:, :o_gate].astype(BF16)
    w_b = jnp.pad(w_in[:, o_u:].astype(BF16), ((0, 0), (0, PROJ_TN)))
    w_b = lax.dynamic_update_slice(w_b, w_in[:, o_gate:o_u].astype(BF16), (0, w_in.shape[1] - o_u))
    tm_proj = min(1024, s)
    nat, kw, ks_aug, vs_aug_t, vw_aug_t, q = _proj_attn(x, norm_mix_g, w_a, tm_proj)
    proj_b, gates_t = _proj_gate(x, norm_mix_g, w_b, tm_proj)

    ncp = s // CMP_STRIDE
    r = nat.reshape(2, NSA_GROUPS, ncp, CMP_STRIDE * NSA_DIM)
    w1 = jnp.stack([cmp_k_w1, cmp_v_w1]).astype(BF16)
    pe = jnp.stack([cmp_pe_k.reshape(1, -1), cmp_pe_v.reshape(1, -1)])
    b1 = jnp.stack([cmp_k_b1.reshape(1, -1), cmp_v_b1.reshape(1, -1)])
    w2 = jnp.stack([cmp_k_w2, cmp_v_w2]).astype(BF16)
    cmp, cmp_t = _compress(r, w1, pe, b1, w2)
    o_a = _nsa(q, gates_t, cmp[0], cmp_t[1], ks_aug, vs_aug_t, kw, vw_aug_t)

    o_b = _sgu(proj_b, sgu_ln_g, sgu_ln_b, sgu_ws, sgu_b, min(512, s))

    m = mem.shape[0]
    mw = MEM_HEADS * MEM_DIM
    mkv = _norm_matmul(mem, norm_mem_g, w_mkv.astype(BF16), F32, m, mw)
    mk_t = mkv[:, :mw].reshape(m, MEM_HEADS, MEM_DIM).transpose(1, 2, 0).astype(BF16)
    mv = mkv[:, mw:].reshape(m, MEM_HEADS, MEM_DIM).transpose(1, 0, 2).astype(BF16)
    return _mix(o_a, o_b, proj_b, x, w_proj_a.astype(BF16), w_proj_b.astype(BF16), w_mix_out.astype(BF16),
                (2 * sguw) // d, norm_cross_g, w_mq.astype(BF16), mk_t, mv, w_mo.astype(BF16), min(256, s))


def kernel(x, mem, norm_mix_g, w_in, cmp_pe_k, cmp_k_w1, cmp_k_b1, cmp_k_w2, cmp_pe_v, cmp_v_w1, cmp_v_b1, cmp_v_w2, sgu_ln_g, sgu_ln_b, sgu_ws, sgu_b, w_proj_a, w_proj_b, w_mix_out, norm_cross_g, norm_mem_g, w_mq, w_mkv, w_mo, norm_ffn_g, w_ffn_in, w_ffn_out, norm_final_g):
    b, s, d = x.shape
    depth = w_in.shape[0]
    outs = []
    for bi in range(b):
        xb = x[bi]
        for l in range(depth):
            last = l == depth - 1
            xb = _layer(xb, mem[bi], norm_mix_g[l], w_in[l], cmp_pe_k[l], cmp_k_w1[l], cmp_k_b1[l], cmp_k_w2[l],
                        cmp_pe_v[l], cmp_v_w1[l], cmp_v_b1[l], cmp_v_w2[l], sgu_ln_g[l], sgu_ln_b[l], sgu_ws[l],
                        sgu_b[l], w_proj_a[l], w_proj_b[l], w_mix_out[l], norm_cross_g[l], norm_mem_g[l],
                        w_mq[l], w_mkv[l], w_mo[l], norm_ffn_g[l])
            xb = _ffn(xb, norm_ffn_g[l], w_ffn_in[l].astype(BF16), w_ffn_out[l].astype(BF16), norm_final_g, last,
                      min(512, s), 512)
        outs.append(xb)
    return jnp.stack(outs)
```

```python
import functools
import math

import jax
import jax.numpy as jnp
from jax import lax
from jax.experimental import pallas as pl
from jax.experimental.pallas import tpu as pltpu

F32 = jnp.float32
BF16 = jnp.bfloat16

NORM_EPS = 1e-6
MASK_VALUE = -1e30
N_FORCED = 3
LOG2E = math.log2(math.e)

NSA_HEADS = 16
NSA_GROUPS = 4
NSA_HPG = NSA_HEADS // NSA_GROUPS
NSA_DIM = 64
N_KV_STREAMS = 6
CMP_BLOCK = 32
CMP_STRIDE = 16
SEL_BLOCK = 64
SEL_TOPK = 16
WINDOW = 512
QUERY_BLOCK = 128
SGU_GROUPS = 8
SGU_CHUNK = 128
MEM_HEADS = 4
MEM_DIM = 128

LANES = 128
SUBLANES = 8
COV_RATIO = SEL_BLOCK // CMP_STRIDE
COV_LEAD = CMP_BLOCK // CMP_STRIDE - 1
COV_BAND = tuple(
    max(min((k - COV_LEAD) * CMP_STRIDE + CMP_BLOCK, SEL_BLOCK) - max((k - COV_LEAD) * CMP_STRIDE, 0), 0) / CMP_BLOCK
    for k in range(COV_RATIO + COV_LEAD))
assert SEL_BLOCK % CMP_STRIDE == 0 and CMP_BLOCK % CMP_STRIDE == 0 and COV_LEAD <= SUBLANES
SEL_TILE = 512
SEL_BPT = SEL_TILE // SEL_BLOCK
SEL_CHUNK = 256
SEL_UNROLLS = (16, 8, 4, 2)
NSA_STEP_BLOCKS = 4
NSA_BUCKETS = 8
BF16_ROWS = 16
FFN_CHUNK = 256
VMEM_LIMIT = 56 * 1024 * 1024


def _params(*sem):
    return pltpu.CompilerParams(dimension_semantics=sem, vmem_limit_bytes=VMEM_LIMIT)


def _rms(x, g):
    return x * lax.rsqrt(jnp.mean(x * x, axis=-1, keepdims=True) + NORM_EPS) * g


def _dot(a, b):
    return jnp.dot(a, b, preferred_element_type=F32)


def _ones_row(n):
    return jnp.where(lax.broadcasted_iota(jnp.int32, (BF16_ROWS, n), 0) == 0, 1.0, 0.0).astype(BF16)


def _norm_matmul_kernel(x_ref, g_ref, w_ref, o_ref, h_ref):
    @pl.when(pl.program_id(1) == 0)
    def _():
        h_ref[...] = _rms(x_ref[...], g_ref[...]).astype(BF16)

    o_ref[...] = _dot(h_ref[...], w_ref[...]).astype(o_ref.dtype)


def _norm_matmul(x, g, w, out_dtype, tm, tn):
    s, d = x.shape
    n = w.shape[1]
    return pl.pallas_call(
        _norm_matmul_kernel,
        grid=(s // tm, n // tn),
        in_specs=[
            pl.BlockSpec((tm, d), lambda i, j: (i, 0)),
            pl.BlockSpec((1, d), lambda i, j: (0, 0)),
            pl.BlockSpec((d, tn), lambda i, j: (0, j)),
        ],
        out_specs=pl.BlockSpec((tm, tn), lambda i, j: (i, j)),
        out_shape=jax.ShapeDtypeStruct((s, n), out_dtype),
        scratch_shapes=[pltpu.VMEM((tm, d), BF16)],
        compiler_params=_params("parallel", "arbitrary"),
        name="norm_matmul",
    )(x, g.reshape(1, d), w)


PROJ_TN = 512
PROJ_KV_TILES = 3


def _proj_kernel(x_ref, g_ref, wa_ref, wb_ref, nat_ref, kw_ref, ksa_ref, vsa_ref, vwa_ref, q_ref, o_ref, gt_ref,
                 h_ref, *, n_attn):
    i = pl.program_id(0)
    j = pl.program_id(1)
    last = pl.num_programs(1) - 1
    tm = x_ref.shape[0]
    dh = NSA_DIM
    ng = NSA_GROUPS

    @pl.when(j == 0)
    def _():
        h_ref[...] = _rms(x_ref[...], g_ref[...]).astype(BF16)

    @pl.when(j < n_attn)
    def _():
        res = _dot(h_ref[...], wa_ref[...])

        def cols(c):
            return res[:, c * dh:(c + 1) * dh].astype(BF16)

        def cols_t(c):
            slab = res[:, (c // 2) * 2 * dh:(c // 2 + 1) * 2 * dh].T
            return slab[(c % 2) * dh:(c % 2 + 1) * dh].astype(BF16)

        @pl.when(j == 0)
        def _():
            for c in range(2 * ng):
                nat_ref[c // ng, c % ng] = cols(c)

        @pl.when(j == 1)
        def _():
            pos = i * tm + lax.broadcasted_iota(jnp.int32, (tm, dh), 0)
            lane = lax.broadcasted_iota(jnp.int32, (tm, dh), 1)
            onehot = jnp.where((pos // SEL_BLOCK) % SEL_BPT == lane, 1.0, 0.0).astype(BF16)
            for g in range(ng):
                ksa_ref[g, :, :dh] = cols(g)
                ksa_ref[g, :, dh:] = onehot
                vsa_ref[g, :dh, :] = cols_t(ng + g)
                vsa_ref[g, dh:, :] = _ones_row(tm)

        @pl.when(j == 2)
        def _():
            for g in range(ng):
                kw_ref[g] = cols(g)
                vwa_ref[g, :dh, :] = cols_t(ng + g)
                vwa_ref[g, dh:, :] = _ones_row(tm)

        @pl.when(j >= PROJ_KV_TILES)
        def _():
            q_ref[...] = res.astype(BF16)

    @pl.when(j >= n_attn)
    def _():
        res = _dot(h_ref[...], wb_ref[...])

        @pl.when(j < last)
        def _():
            o_ref[...] = res

        @pl.when(j == last)
        def _():
            gt_ref[...] = res[:, :LANES].T


def _proj(x, g, w_a, w_b, tm):
    s, d = x.shape
    dh, ng, tn = NSA_DIM, NSA_GROUPS, PROJ_TN
    assert 2 * ng * dh == tn and w_a.shape[1] == PROJ_KV_TILES * tn + NSA_HEADS * dh
    nq = NSA_HEADS * dh // tn
    na = PROJ_KV_TILES + nq
    n = w_b.shape[1] - tn
    nb = n // tn
    shapes = [
        jax.ShapeDtypeStruct((2, ng, s, dh), BF16),
        jax.ShapeDtypeStruct((ng, s, dh), BF16),
        jax.ShapeDtypeStruct((ng, s, 2 * dh), BF16),
        jax.ShapeDtypeStruct((ng, dh + BF16_ROWS, s), BF16),
        jax.ShapeDtypeStruct((ng, dh + BF16_ROWS, s), BF16),
        jax.ShapeDtypeStruct((s, NSA_HEADS * dh), BF16),
        jax.ShapeDtypeStruct((s, n), F32),
        jax.ShapeDtypeStruct((LANES, s), F32),
    ]
    return pl.pallas_call(
        functools.partial(_proj_kernel, n_attn=na),
        grid=(s // tm, na + nb + 1),
        in_specs=[
            pl.BlockSpec((tm, d), lambda i, j: (i, 0)),
            pl.BlockSpec((1, d), lambda i, j: (0, 0)),
            pl.BlockSpec((d, tn), lambda i, j: (0, (jnp.minimum(j, na - 1) + nq) % na)),
            pl.BlockSpec((d, tn), lambda i, j: (0, jnp.maximum(j - na, 0))),
        ],
        out_specs=[
            pl.BlockSpec((2, ng, tm, dh), lambda i, j: (0, 0, i, 0)),
            pl.BlockSpec((ng, tm, dh), lambda i, j: (0, i, 0)),
            pl.BlockSpec((ng, tm, 2 * dh), lambda i, j: (0, i, 0)),
            pl.BlockSpec((ng, dh + BF16_ROWS, tm), lambda i, j: (0, 0, i)),
            pl.BlockSpec((ng, dh + BF16_ROWS, tm), lambda i, j: (0, 0, i)),
            pl.BlockSpec((tm, tn), lambda i, j: (i, jnp.clip(j - PROJ_KV_TILES, 0, nq - 1))),
            pl.BlockSpec((tm, tn), lambda i, j: (i, jnp.clip(j - na, 0, nb - 1))),
            pl.BlockSpec((LANES, tm), lambda i, j: (0, i)),
        ],
        out_shape=shapes,
        scratch_shapes=[pltpu.VMEM((tm, d), BF16)],
        compiler_params=_params("parallel", "arbitrary"),
        name="proj",
    )(x, g.reshape(1, d), w_a, w_b)


def _compress_kernel(r_ref, w1_ref, pe_ref, b1_ref, w2_ref, o_ref, ot_ref):
    half = r_ref.shape[-1]
    r = r_ref[0, 0]
    bias = _dot(pe_ref[0].astype(BF16), w1_ref[0]) + b1_ref[0]
    top = _dot(r, w1_ref[0, :half, :])
    bot = _dot(r, w1_ref[0, half:, :])
    ncp = r.shape[0]
    hid = top + pltpu.roll(bot, ncp - 1, 0) + bias
    out = _dot(jax.nn.gelu(hid).astype(BF16), w2_ref[0])
    o_ref[0, 0] = out.astype(o_ref.dtype)
    dh = out.shape[1]
    out_t = jnp.concatenate([out, jnp.zeros((ncp, LANES - dh), F32)], axis=1).T
    ot_ref[0, 0, :dh, :] = out_t[:dh].astype(ot_ref.dtype)
    ot_ref[0, 0, dh:, :] = _ones_row(ncp)


def _compress(r, w1, pe, b1, w2):
    _, g, ncp, half = r.shape
    hidden = w1.shape[-1]
    dh = w2.shape[-1]
    return pl.pallas_call(
        _compress_kernel,
        grid=(2, g),
        in_specs=[
            pl.BlockSpec((1, 1, ncp, half), lambda s, gg: (s, gg, 0, 0)),
            pl.BlockSpec((1, 2 * half, hidden), lambda s, gg: (s, 0, 0)),
            pl.BlockSpec((1, 1, 2 * half), lambda s, gg: (s, 0, 0)),
            pl.BlockSpec((1, 1, hidden), lambda s, gg: (s, 0, 0)),
            pl.BlockSpec((1, hidden, dh), lambda s, gg: (s, 0, 0)),
        ],
        out_specs=[pl.BlockSpec((1, 1, ncp, dh), lambda s, gg: (s, gg, 0, 0)),
                   pl.BlockSpec((1, 1, dh + BF16_ROWS, ncp), lambda s, gg: (s, gg, 0, 0))],
        out_shape=[jax.ShapeDtypeStruct((2, g, ncp, dh), BF16),
                   jax.ShapeDtypeStruct((2, g, dh + BF16_ROWS, ncp), BF16)],
        compiler_params=_params("parallel", "parallel"),
        name="compress",
    )(r, w1, pe, b1, w2)


def _nsa_kernel(q_ref, gt_ref, kc_ref, vct_ref, ks_ref, vst_ref, kw_ref, vwt_ref, o_ref, *scratch):
    g = pl.program_id(0)
    step = pl.program_id(1)
    cend_ref, wrel_ref = scratch[-2:]

    @pl.when(step == 0)
    def _():
        lane = lax.broadcasted_iota(jnp.int32, cend_ref.shape, 1)
        cend_ref[...] = lax.broadcasted_iota(jnp.int32, cend_ref.shape, 0) * CMP_STRIDE + (CMP_BLOCK - 1) - lane
        wrel_ref[...] = (lax.broadcasted_iota(jnp.int32, wrel_ref.shape, 0)
                         - lax.broadcasted_iota(jnp.int32, wrel_ref.shape, 1))

    def block(sub, carry):
        rows = pl.ds(pl.multiple_of(sub * QUERY_BLOCK, QUERY_BLOCK), QUERY_BLOCK)
        _nsa_block(g, step * NSA_STEP_BLOCKS + sub, q_ref.at[rows, :], gt_ref.at[:, rows], kc_ref, vct_ref, ks_ref,
                   vst_ref, kw_ref, vwt_ref, o_ref.at[rows, :], *scratch)
        return carry

    lax.fori_loop(0, NSA_STEP_BLOCKS, block, 0)


def _nsa_block(g, i, q_ref, gt_ref, kc_ref, vct_ref, ks_ref, vst_ref, kw_ref, vwt_ref, o_ref,
               qa_ref, sel_ref, s0_ref, s1_ref, acc_ref, part_ref, pp_ref, gates_ref, blkf_ref, cend_ref, wrel_ref):
    s_refs = (s0_ref, s1_ref)
    qb = QUERY_BLOCK
    start = i * qb
    nb = sel_ref.shape[0]
    ncp = kc_ref.shape[1]
    s_len = ks_ref.shape[1]
    hq = NSA_HPG * qb
    t_row = start + lax.broadcasted_iota(jnp.int32, (1, qb), 1)

    qt = (q_ref[...].astype(F32) * (NSA_DIM ** -0.5 * LOG2E)).T
    q_t = jnp.concatenate([qt[h * NSA_DIM:(h + 1) * NSA_DIM] for h in range(NSA_HPG)], axis=1).astype(BF16)
    qa_ref[:NSA_DIM, :] = q_t
    qa_ref[NSA_DIM:, :] = jnp.zeros((qa_ref.shape[0] - NSA_DIM, hq), BF16)

    jt = t_row // SEL_BLOCK
    picked = -2.0

    gates_ref[...] = gt_ref[...]

    def gate(branch):
        rows = [gates_ref[pl.ds(g * (NSA_HPG * 3) + h * 3 + branch, 1), :] for h in range(NSA_HPG)]
        return jax.nn.sigmoid(jnp.concatenate(rows, axis=1))

    def exps(s, bias):
        out = []
        for h in range(NSA_HPG):
            sh = s[:, h * qb:(h + 1) * qb] + bias
            out.append(jnp.exp2(sh - jnp.max(sh, axis=0, keepdims=True)))
        return out

    def compressed_and_select(nck, nbk):
        wk = WINDOW + qb
        k0w = pl.multiple_of(jnp.maximum(start - WINDOW, 0), qb)
        sc = _dot(kc_ref[0, :nck, :], q_t)
        sw = _dot(kw_ref[0, pl.ds(k0w, wk), :], q_t)
        e_cmp = exps(sc, jnp.where(cend_ref[:nck, :] <= start, 0.0, MASK_VALUE))
        o_cmp = _dot(vct_ref[0, :, :nck], jnp.concatenate([e.astype(BF16) for e in e_cmp], axis=1))
        wrel = wrel_ref[...]
        e_win = exps(sw, jnp.where((wrel <= start - k0w) & (wrel > start - k0w - WINDOW), 0.0, MASK_VALUE))
        o_win = _dot(vwt_ref[0, :, pl.ds(k0w, wk)], jnp.concatenate([e.astype(BF16) for e in e_win], axis=1))
        inv_cmp = jnp.where(jnp.concatenate([t_row >= CMP_BLOCK - 1] * NSA_HPG, axis=1),
                            1.0 / o_cmp[NSA_DIM:NSA_DIM + 1, :], 0.0)
        psum = e_cmp[0] * inv_cmp[:, :qb]
        for h in range(1, NSA_HPG):
            psum = psum + e_cmp[h] * inv_cmp[:, h * qb:(h + 1) * qb]
        part_ref[...] = (gate(0) * inv_cmp * o_cmp[:NSA_DIM]
                         + gate(2) * (1.0 / o_win[NSA_DIM:NSA_DIM + 1, :]) * o_win[:NSA_DIM])
        pp_ref[:SUBLANES, :] = jnp.zeros((SUBLANES, qb), F32)
        pp_ref[SUBLANES:SUBLANES + nck, :] = psum
        imp = None
        for k, w in enumerate(COV_BAND):
            tap = pp_ref[pl.ds(SUBLANES - COV_LEAD + k, nbk, stride=COV_RATIO), :]
            tap = tap if w == 1.0 else w * tap
            imp = tap if imp is None else imp + tap

        blk = lax.broadcasted_iota(jnp.int32, (nbk, qb), 0)
        forced = (blk == 0) | (blk == jt) | (blk == jt - 1)
        score = jnp.where(forced, picked, jnp.where(blk <= jt, imp, -1.0))
        blkf_ref[:nbk, :] = blk.astype(F32)

        def pick(_, score):
            mx = jnp.max(score, axis=0, keepdims=True)
            cand = jnp.where(score == mx, blkf_ref[:nbk, :], float(nbk))
            return jnp.where(cand == jnp.min(cand, axis=0, keepdims=True), picked, score)

        score = lax.fori_loop(0, min(SEL_TOPK, nbk) - N_FORCED, pick, score)
        sel_ref[:nbk, :] = jnp.where((score == picked) & (blk <= jt), 0.0, MASK_VALUE)

    bucket_ok = s_len % (NSA_BUCKETS * SEL_TILE) == 0 and nb // NSA_BUCKETS >= SEL_TOPK
    n_bucket = NSA_BUCKETS if bucket_ok else 1
    bucket = (start + qb - 1) // (s_len // n_bucket)
    for b in range(n_bucket):
        pl.when(bucket == b)(functools.partial(
            compressed_and_select, (b + 1) * (ncp // n_bucket), (b + 1) * (nb // n_bucket)))

    tk = SEL_TILE

    def scores(slot, kt):
        k0 = pl.multiple_of(kt * tk, tk)
        b8 = sel_ref[pl.ds(pl.multiple_of(kt * SEL_BPT, SEL_BPT), SEL_BPT), :]
        b16 = jnp.concatenate([b8, jnp.zeros((BF16_ROWS - SEL_BPT, qb), F32)], axis=0).astype(BF16)
        qa_ref[NSA_DIM:NSA_DIM + BF16_ROWS, :] = jnp.concatenate([b16] * NSA_HPG, axis=1)
        s = _dot(ks_ref[0, pl.ds(k0, tk), :], qa_ref[...])
        s_refs[slot][...] = s
        return jnp.max(s, axis=0, keepdims=True)

    def update(slot, kt, mt, m, causal):
        k0 = pl.multiple_of(kt * tk, tk)
        s = s_refs[slot][...]
        if causal:
            kpos = k0 + lax.broadcasted_iota(jnp.int32, (tk, qb), 0)
            s = s + jnp.concatenate([jnp.where(kpos <= t_row, 0.0, MASK_VALUE)] * NSA_HPG, axis=1)
            mt = jnp.max(s, axis=0, keepdims=True)
        m_new = jnp.maximum(m, mt)
        p = jnp.exp2(s - m_new).astype(BF16)
        acc_ref[...] = jnp.exp2(m - m_new) * acc_ref[...] + _dot(vst_ref[0, :, pl.ds(k0, tk)], p)
        return m_new

    def accumulate(pend):
        p, k0, alpha = pend
        pv = _dot(vst_ref[0, :, pl.ds(k0, SEL_CHUNK)], p)
        acc_ref[...] = (acc_ref[...] if alpha is None else alpha * acc_ref[...]) + pv

    def run(first, n_tiles, carry):
        mt, m = carry
        pend = None
        for k in range(n_tiles):
            cur, nxt = s_refs[k % 2], s_refs[(k + 1) % 2]
            k0c = pl.multiple_of((first + k) * tk, tk)
            k0n = pl.multiple_of((first + k + 1) * tk, tk)
            b8 = sel_ref[pl.ds(pl.multiple_of((first + k + 1) * SEL_BPT, SEL_BPT), SEL_BPT), :]
            b16 = jnp.concatenate([b8, jnp.zeros((BF16_ROWS - SEL_BPT, qb), F32)], axis=0).astype(BF16)
            qa_ref[NSA_DIM:NSA_DIM + BF16_ROWS, :] = jnp.concatenate([b16] * NSA_HPG, axis=1)
            m_new = jnp.maximum(m, mt)
            alpha = jnp.exp2(m - m_new)
            mt = None
            for c in range(tk // SEL_CHUNK):
                rows = pl.ds(c * SEL_CHUNK, SEL_CHUNK)
                s_n = _dot(ks_ref[0, pl.ds(k0n + c * SEL_CHUNK, SEL_CHUNK), :], qa_ref[...])
                nxt[rows, :] = s_n
                mt_c = jnp.max(s_n, axis=0, keepdims=True)
                mt = mt_c if mt is None else jnp.maximum(mt, mt_c)
                p = jnp.exp2(cur[rows, :] - m_new).astype(BF16)
                if pend is not None:
                    accumulate(pend)
                pend = (p, k0c + c * SEL_CHUNK, alpha if c == 0 else None)
            m = m_new
        accumulate(pend)
        return mt, m

    acc_ref[...] = jnp.zeros(acc_ref.shape, F32)
    last = (start + qb - 1) // tk
    carry = (scores(0, 0), jnp.full((1, hq), MASK_VALUE, F32))
    done = 0
    for unroll in SEL_UNROLLS:
        trips = (last - done) // unroll
        carry = lax.fori_loop(0, trips, lambda it, c, d=done, u=unroll: run(d + it * u, u, c), carry)
        done = done + trips * unroll
    mt_a, m = carry

    @pl.when(last % 2 == 0)
    def _():
        update(0, last, mt_a, m, True)

    @pl.when(last % 2 == 1)
    def _():
        mt_b = scores(1, last)
        update(1, last, mt_b, update(0, last - 1, mt_a, m, False), True)

    o_sel = acc_ref[:NSA_DIM, :] * (1.0 / acc_ref[NSA_DIM:NSA_DIM + 1, :])

    out_t = part_ref[...] + gate(1) * o_sel
    outs = [out_t[:, h * qb:(h + 1) * qb] for h in range(NSA_HPG)]
    o_ref[...] = jnp.concatenate(outs, axis=0).T.astype(o_ref.dtype)


def _nsa(q, gates_t, kc, vc_aug_t, ks_aug, vs_aug_t, kw, vw_aug_t):
    s = q.shape[0]
    assert s % (2 * SEL_TILE) == 0 and s >= WINDOW + QUERY_BLOCK
    gdim = NSA_HPG * NSA_DIM
    ncp = kc.shape[1]
    assert ncp * CMP_STRIDE == s and s // SEL_BLOCK * COV_RATIO == ncp
    nb = s // SEL_BLOCK
    ka = ks_aug.shape[-1]
    va = vs_aug_t.shape[1]
    hq = NSA_HPG * QUERY_BLOCK
    per_group = lambda shape: pl.BlockSpec((1,) + shape, lambda g, i: (g, 0, 0))
    rows = NSA_STEP_BLOCKS * QUERY_BLOCK
    assert s % rows == 0
    return pl.pallas_call(
        _nsa_kernel,
        grid=(NSA_GROUPS, s // rows),
        in_specs=[
            pl.BlockSpec((rows, gdim), lambda g, i: (i, g)),
            pl.BlockSpec((NSA_HEADS * 3, rows), lambda g, i: (0, i)),
            per_group((ncp, NSA_DIM)),
            per_group((va, ncp)),
            per_group((s, ka)),
            per_group((va, s)),
            per_group((s, NSA_DIM)),
            per_group((va, s)),
        ],
        out_specs=pl.BlockSpec((rows, gdim), lambda g, i: (i, g)),
        out_shape=jax.ShapeDtypeStruct((s, NSA_HEADS * NSA_DIM), BF16),
        scratch_shapes=[pltpu.VMEM((ka, hq), BF16), pltpu.VMEM((nb, QUERY_BLOCK), F32),
                        pltpu.VMEM((SEL_TILE, hq), F32), pltpu.VMEM((SEL_TILE, hq), F32),
                        pltpu.VMEM((va, hq), F32),
                        pltpu.VMEM((NSA_DIM, hq), F32),
                        pltpu.VMEM((SUBLANES + ncp, QUERY_BLOCK), F32),
                        pltpu.VMEM((NSA_HEADS * 3, QUERY_BLOCK), F32),
                        pltpu.VMEM((nb, QUERY_BLOCK), F32),
                        pltpu.VMEM((ncp, QUERY_BLOCK), jnp.int32),
                        pltpu.VMEM((WINDOW + QUERY_BLOCK, QUERY_BLOCK), jnp.int32)],
        compiler_params=_params("parallel", "arbitrary"),
        name="nsa",
    )(q, gates_t, kc, vc_aug_t, ks_aug, vs_aug_t, kw, vw_aug_t)


def _sgu_kernel(u_ref, v_ref, lng_ref, lnb_ref, ws_ref, bs_ref, o_ref):
    c = SGU_CHUNK
    tm = u_ref.shape[0]
    v = jax.nn.gelu(v_ref[...])
    mu = jnp.mean(v, axis=-1, keepdims=True)
    var = jnp.mean(jnp.square(v - mu), axis=-1, keepdims=True)
    vn = ((v - mu) * lax.rsqrt(var + NORM_EPS) * lng_ref[...] + lnb_ref[...]).astype(BF16)
    u = jax.nn.gelu(u_ref[...])
    tri = lax.broadcasted_iota(jnp.int32, (c, c), 0) >= lax.broadcasted_iota(jnp.int32, (c, c), 1)
    for g in range(SGU_GROUPS):
        w = jnp.where(tri, ws_ref[g], 0.0).astype(BF16)
        cols = slice(g * c, (g + 1) * c)
        rhs = jnp.concatenate([vn[k * c:(k + 1) * c, cols] for k in range(tm // c)], axis=1)
        mixed = _dot(w, rhs)
        for k in range(tm // c):
            rows = slice(k * c, (k + 1) * c)
            o_ref[rows, cols] = (u[rows, cols] * (mixed[:, rows] + bs_ref[g])).astype(o_ref.dtype)


def _sgu(proj, lng, lnb, ws, bs, tm):
    s = proj.shape[0]
    w = lng.shape[0]
    c = SGU_CHUNK
    bs_b = jnp.broadcast_to(bs[:, :, None], (SGU_GROUPS, c, c))
    return pl.pallas_call(
        _sgu_kernel,
        grid=(s // tm,),
        in_specs=[
            pl.BlockSpec((tm, w), lambda i: (i, 0)),
            pl.BlockSpec((tm, w), lambda i: (i, 1)),
            pl.BlockSpec((1, w), lambda i: (0, 0)),
            pl.BlockSpec((1, w), lambda i: (0, 0)),
            pl.BlockSpec((SGU_GROUPS, c, c), lambda i: (0, 0, 0)),
            pl.BlockSpec((SGU_GROUPS, c, c), lambda i: (0, 0, 0)),
        ],
        out_specs=pl.BlockSpec((tm, w), lambda i: (i, 0)),
        out_shape=jax.ShapeDtypeStruct((s, w), BF16),
        compiler_params=_params("parallel"),
        name="sgu",
    )(proj, proj, lng.reshape(1, w), lnb.reshape(1, w), ws, bs_b)


def _mix_kernel(oa_ref, ob_ref, ga_ref, gb_ref, x_ref, pa_ref, pb_ref, wo_ref,
                gc_ref, wq_ref, mkt_ref, mv_ref, wmo_ref, o_ref):
    a = _dot(oa_ref[...], pa_ref[...])
    b = _dot(ob_ref[...], pb_ref[...])
    merged = jax.nn.sigmoid(ga_ref[...]) * a + jax.nn.sigmoid(gb_ref[...]) * b
    x = x_ref[...] + _dot(merged.astype(BF16), wo_ref[...])
    h = _rms(x, gc_ref[...]).astype(BF16)
    mq = (_dot(h, wq_ref[...]) * (MEM_DIM ** -0.5)).astype(BF16)
    outs = []
    for hh in range(MEM_HEADS):
        s = _dot(mq[:, hh * MEM_DIM:(hh + 1) * MEM_DIM], mkt_ref[hh])
        e = jnp.exp(s - jnp.max(s, axis=-1, keepdims=True))
        p = e / jnp.sum(e, axis=-1, keepdims=True)
        outs.append(_dot(p.astype(BF16), mv_ref[hh]).astype(BF16))
    o_ref[...] = x + _dot(jnp.concatenate(outs, axis=1), wmo_ref[...])


def _mix(o_a, o_b, proj, x, p_a, p_b, w_o, gate_block, g_cross, w_q, mk_t, mv, w_mo, tm):
    s, d = x.shape
    wa = o_a.shape[1]
    wb = o_b.shape[1]
    mw = w_q.shape[1]
    m = mv.shape[1]
    resident = lambda shape: pl.BlockSpec(shape, lambda i: (0,) * len(shape), pipeline_mode=pl.Buffered(1))
    return pl.pallas_call(
        _mix_kernel,
        grid=(s // tm,),
        in_specs=[
            pl.BlockSpec((tm, wa), lambda i: (i, 0)),
            pl.BlockSpec((tm, wb), lambda i: (i, 0)),
            pl.BlockSpec((tm, d), lambda i: (i, gate_block)),
            pl.BlockSpec((tm, d), lambda i: (i, gate_block + 1)),
            pl.BlockSpec((tm, d), lambda i: (i, 0)),
            resident((wa, d)),
            resident((wb, d)),
            resident((d, d)),
            resident((1, d)),
            resident((d, mw)),
            resident((MEM_HEADS, MEM_DIM, m)),
            resident((MEM_HEADS, m, MEM_DIM)),
            resident((mw, d)),
        ],
        out_specs=pl.BlockSpec((tm, d), lambda i: (i, 0)),
        out_shape=jax.ShapeDtypeStruct((s, d), F32),
        compiler_params=_params("parallel"),
        name="mix",
    )(o_a, o_b, proj, proj, x, p_a, p_b, w_o, g_cross.reshape(1, d), w_q, mk_t, mv, w_mo)


def _ffn_kernel(x_ref, g_ref, wg_ref, wu_ref, wo_ref, gf_ref, o_ref, h_ref, acc_ref, *, final_norm):
    j = pl.program_id(1)

    @pl.when(j == 0)
    def _():
        h_ref[...] = _rms(x_ref[...], g_ref[...]).astype(BF16)
        acc_ref[...] = jnp.zeros_like(acc_ref)

    h = h_ref[...]
    th = wg_ref.shape[1]
    out = None
    for c in range(th // FFN_CHUNK):
        cs = slice(c * FFN_CHUNK, (c + 1) * FFN_CHUNK)
        act = (jax.nn.silu(_dot(h, wg_ref[:, cs])) * _dot(h, wu_ref[:, cs])).astype(BF16)
        part = _dot(act, wo_ref[cs, :])
        out = part if out is None else out + part
    acc_ref[...] += out

    @pl.when(j == pl.num_programs(1) - 1)
    def _():
        y = x_ref[...] + acc_ref[...]
        o_ref[...] = _rms(y, gf_ref[...]) if final_norm else y


def _ffn(x, g, w_in, w_out, g_final, final_norm, tm, th):
    s, d = x.shape
    hidden = w_out.shape[0]
    nh = hidden // th
    return pl.pallas_call(
        functools.partial(_ffn_kernel, final_norm=final_norm),
        grid=(s // tm, nh),
        in_specs=[
            pl.BlockSpec((tm, d), lambda i, j: (i, 0)),
            pl.BlockSpec((1, d), lambda i, j: (0, 0)),
            pl.BlockSpec((d, th), lambda i, j: (0, j)),
            pl.BlockSpec((d, th), lambda i, j: (0, j + nh)),
            pl.BlockSpec((th, d), lambda i, j: (j, 0)),
            pl.BlockSpec((1, d), lambda i, j: (0, 0)),
        ],
        out_specs=pl.BlockSpec((tm, d), lambda i, j: (i, 0)),
        out_shape=jax.ShapeDtypeStruct((s, d), F32),
        scratch_shapes=[pltpu.VMEM((tm, d), BF16), pltpu.VMEM((tm, d), F32)],
        compiler_params=_params("parallel", "arbitrary"),
        name="ffn",
    )(x, g.reshape(1, d), w_in, w_in, w_out, g_final.reshape(1, d))


def _layer(x, mem, norm_mix_g, w_in, cmp_pe_k, cmp_k_w1, cmp_k_b1, cmp_k_w2, cmp_pe_v, cmp_v_w1, cmp_v_b1, cmp_v_w2,
           sgu_ln_g, sgu_ln_b, sgu_ws, sgu_b, w_proj_a, w_proj_b, w_mix_out, norm_cross_g, norm_mem_g,
           w_mq, w_mkv, w_mo, norm_ffn_g):
    s, d = x.shape
    qw = NSA_HEADS * NSA_DIM
    kvw = NSA_GROUPS * NSA_DIM
    sguw = sgu_ln_g.shape[0]
    ngate = NSA_HEADS * 3
    o_kv = qw
    o_gate = o_kv + N_KV_STREAMS * kvw
    o_u = o_gate + ngate

    w_a = w_in[:, :o_gate].astype(BF16)
    w_b = jnp.pad(w_in[:, o_u:].astype(BF16), ((0, 0), (0, PROJ_TN)))
    w_b = lax.dynamic_update_slice(w_b, w_in[:, o_gate:o_u].astype(BF16), (0, w_in.shape[1] - o_u))
    tm_proj = min(1024, s)
    nat, kw, ks_aug, vs_aug_t, vw_aug_t, q, proj_b, gates_t = _proj(x, norm_mix_g, w_a, w_b, tm_proj)

    ncp = s // CMP_STRIDE
    r = nat.reshape(2, NSA_GROUPS, ncp, CMP_STRIDE * NSA_DIM)
    w1 = jnp.stack([cmp_k_w1, cmp_v_w1]).astype(BF16)
    pe = jnp.stack([cmp_pe_k.reshape(1, -1), cmp_pe_v.reshape(1, -1)])
    b1 = jnp.stack([cmp_k_b1.reshape(1, -1), cmp_v_b1.reshape(1, -1)])
    w2 = jnp.stack([cmp_k_w2, cmp_v_w2]).astype(BF16)
    cmp, cmp_t = _compress(r, w1, pe, b1, w2)
    o_a = _nsa(q, gates_t, cmp[0], cmp_t[1], ks_aug, vs_aug_t, kw, vw_aug_t)

    o_b = _sgu(proj_b, sgu_ln_g, sgu_ln_b, sgu_ws, sgu_b, min(512, s))

    m = mem.shape[0]
    mw = MEM_HEADS * MEM_DIM
    mkv = _norm_matmul(mem, norm_mem_g, w_mkv.astype(BF16), F32, m, mw)
    mk_t = mkv[:, :mw].reshape(m, MEM_HEADS, MEM_DIM).transpose(1, 2, 0).astype(BF16)
    mv = mkv[:, mw:].reshape(m, MEM_HEADS, MEM_DIM).transpose(1, 0, 2).astype(BF16)
    return _mix(o_a, o_b, proj_b, x, w_proj_a.astype(BF16), w_proj_b.astype(BF16), w_mix_out.astype(BF16),
                (2 * sguw) // d, norm_cross_g, w_mq.astype(BF16), mk_t, mv, w_mo.astype(BF16), min(256, s))


def kernel(x, mem, norm_mix_g, w_in, cmp_pe_k, cmp_k_w1, cmp_k_b1, cmp_k_w2, cmp_pe_v, cmp_v_w1, cmp_v_b1, cmp_v_w2, sgu_ln_g, sgu_ln_b, sgu_ws, sgu_b, w_proj_a, w_proj_b, w_mix_out, norm_cross_g, norm_mem_g, w_mq, w_mkv, w_mo, norm_ffn_g, w_ffn_in, w_ffn_out, norm_final_g):
    b, s, d = x.shape
    depth = w_in.shape[0]
    outs = []
    for bi in range(b):
        xb = x[bi]
        for l in range(depth):
            last = l == depth - 1
            xb = _layer(xb, mem[bi], norm_mix_g[l], w_in[l], cmp_pe_k[l], cmp_k_w1[l], cmp_k_b1[l], cmp_k_w2[l],
                        cmp_pe_v[l], cmp_v_w1[l], cmp_v_b1[l], cmp_v_w2[l], sgu_ln_g[l], sgu_ln_b[l], sgu_ws[l],
                        sgu_b[l], w_proj_a[l], w_proj_b[l], w_mix_out[l], norm_cross_g[l], norm_mem_g[l],
                        w_mq[l], w_mkv[l], w_mo[l], norm_ffn_g[l])
            xb = _ffn(xb, norm_ffn_g[l], w_ffn_in[l].astype(BF16), w_ffn_out[l].astype(BF16), norm_final_g, last,
                      min(512, s), 512)
        outs.append(xb)
    return jnp.stack(outs)
```

```python
import functools
import math

import jax
import jax.numpy as jnp
from jax import lax
from jax.experimental import pallas as pl
from jax.experimental.pallas import tpu as pltpu

F32 = jnp.float32
BF16 = jnp.bfloat16

NORM_EPS = 1e-6
MASK_VALUE = -1e30
N_FORCED = 3
LOG2E = math.log2(math.e)

NSA_HEADS = 16
NSA_GROUPS = 4
NSA_HPG = NSA_HEADS // NSA_GROUPS
NSA_DIM = 64
N_KV_STREAMS = 6
CMP_BLOCK = 32
CMP_STRIDE = 16
SEL_BLOCK = 64
SEL_TOPK = 16
WINDOW = 512
QUERY_BLOCK = 128
SGU_GROUPS = 8
SGU_CHUNK = 128
MEM_HEADS = 4
MEM_DIM = 128

LANES = 128
SUBLANES = 8
COV_RATIO = SEL_BLOCK // CMP_STRIDE
COV_LEAD = CMP_BLOCK // CMP_STRIDE - 1
COV_BAND = tuple(
    max(min((k - COV_LEAD) * CMP_STRIDE + CMP_BLOCK, SEL_BLOCK) - max((k - COV_LEAD) * CMP_STRIDE, 0), 0) / CMP_BLOCK
    for k in range(COV_RATIO + COV_LEAD))
assert SEL_BLOCK % CMP_STRIDE == 0 and CMP_BLOCK % CMP_STRIDE == 0 and COV_LEAD <= SUBLANES
SEL_TILE = 512
SEL_BPT = SEL_TILE // SEL_BLOCK
SEL_CHUNK = 256
SEL_UNROLLS = (16, 8, 4, 2)
NSA_STEP_BLOCKS = 4
NSA_BUCKETS = 8
BF16_ROWS = 16
FFN_CHUNK = 256
VMEM_LIMIT = 56 * 1024 * 1024


def _params(*sem):
    return pltpu.CompilerParams(dimension_semantics=sem, vmem_limit_bytes=VMEM_LIMIT)


def _rms(x, g):
    return x * lax.rsqrt(jnp.mean(x * x, axis=-1, keepdims=True) + NORM_EPS) * g


def _dot(a, b):
    return jnp.dot(a, b, preferred_element_type=F32)


def _ones_row(n):
    return jnp.where(lax.broadcasted_iota(jnp.int32, (BF16_ROWS, n), 0) == 0, 1.0, 0.0).astype(BF16)


def _norm_matmul_kernel(x_ref, g_ref, w_ref, o_ref, h_ref):
    @pl.when(pl.program_id(1) == 0)
    def _():
        h_ref[...] = _rms(x_ref[...], g_ref[...]).astype(BF16)

    o_ref[...] = _dot(h_ref[...], w_ref[...]).astype(o_ref.dtype)


def _norm_matmul(x, g, w, out_dtype, tm, tn):
    s, d = x.shape
    n = w.shape[1]
    return pl.pallas_call(
        _norm_matmul_kernel,
        grid=(s // tm, n // tn),
        in_specs=[
            pl.BlockSpec((tm, d), lambda i, j: (i, 0)),
            pl.BlockSpec((1, d), lambda i, j: (0, 0)),
            pl.BlockSpec((d, tn), lambda i, j: (0, j)),
        ],
        out_specs=pl.BlockSpec((tm, tn), lambda i, j: (i, j)),
        out_shape=jax.ShapeDtypeStruct((s, n), out_dtype),
        scratch_shapes=[pltpu.VMEM((tm, d), BF16)],
        compiler_params=_params("parallel", "arbitrary"),
        name="norm_matmul",
    )(x, g.reshape(1, d), w)


PROJ_TN = 512
PROJ_KV_TILES = 3


def _proj_kernel(x_ref, g_ref, wa_ref, wb_ref, nat_ref, kw_ref, ksa_ref, vsa_ref, vwa_ref, q_ref, o_ref, gt_ref,
                 h_ref, stage_ref, *, n_attn):
    i = pl.program_id(0)
    j = pl.program_id(1)
    last = pl.num_programs(1) - 1
    tm = x_ref.shape[0]
    dh = NSA_DIM
    ng = NSA_GROUPS

    @pl.when(j == 0)
    def _():
        h_ref[...] = _rms(x_ref[...], g_ref[...]).astype(BF16)

    @pl.when(j < n_attn)
    def _():
        res = _dot(h_ref[...], wa_ref[...])

        def cols(c):
            return res[:, c * dh:(c + 1) * dh].astype(BF16)

        def cols_t(c):
            slab = res[:, (c // 2) * 2 * dh:(c // 2 + 1) * 2 * dh].T
            return slab[(c % 2) * dh:(c % 2 + 1) * dh].astype(BF16)

        @pl.when(j == 0)
        def _():
            for c in range(2 * ng):
                stage_ref[...] = res[:, c * dh:(c + 1) * dh]
                for t in range(0, CMP_STRIDE, 2):
                    pair = [stage_ref[pl.ds(t + u, tm // CMP_STRIDE, stride=CMP_STRIDE), :] for u in range(2)]
                    nat_ref[c // ng, c % ng, :, t * dh:(t + 2) * dh] = jnp.concatenate(pair, axis=1).astype(BF16)

        @pl.when(j == 1)
        def _():
            pos = i * tm + lax.broadcasted_iota(jnp.int32, (tm, dh), 0)
            lane = lax.broadcasted_iota(jnp.int32, (tm, dh), 1)
            onehot = jnp.where((pos // SEL_BLOCK) % SEL_BPT == lane, 1.0, 0.0).astype(BF16)
            for g in range(ng):
                ksa_ref[g, :, :dh] = cols(g)
                ksa_ref[g, :, dh:] = onehot
                vsa_ref[g, :dh, :] = cols_t(ng + g)
                vsa_ref[g, dh:, :] = _ones_row(tm)

        @pl.when(j == 2)
        def _():
            for g in range(ng):
                kw_ref[g] = cols(g)
                vwa_ref[g, :dh, :] = cols_t(ng + g)
                vwa_ref[g, dh:, :] = _ones_row(tm)

        @pl.when(j >= PROJ_KV_TILES)
        def _():
            q_ref[...] = res.astype(BF16)

    @pl.when(j >= n_attn)
    def _():
        res = _dot(h_ref[...], wb_ref[...])

        @pl.when(j < last)
        def _():
            o_ref[...] = res

        @pl.when(j == last)
        def _():
            gt_ref[...] = res[:, :LANES].T


def _proj(x, g, w_a, w_b, tm):
    s, d = x.shape
    dh, ng, tn = NSA_DIM, NSA_GROUPS, PROJ_TN
    assert 2 * ng * dh == tn and w_a.shape[1] == PROJ_KV_TILES * tn + NSA_HEADS * dh
    nq = NSA_HEADS * dh // tn
    na = PROJ_KV_TILES + nq
    n = w_b.shape[1] - tn
    nb = n // tn
    shapes = [
        jax.ShapeDtypeStruct((2, ng, s // CMP_STRIDE, CMP_STRIDE * dh), BF16),
        jax.ShapeDtypeStruct((ng, s, dh), BF16),
        jax.ShapeDtypeStruct((ng, s, 2 * dh), BF16),
        jax.ShapeDtypeStruct((ng, dh + BF16_ROWS, s), BF16),
        jax.ShapeDtypeStruct((ng, dh + BF16_ROWS, s), BF16),
        jax.ShapeDtypeStruct((s, NSA_HEADS * dh), BF16),
        jax.ShapeDtypeStruct((s, n), F32),
        jax.ShapeDtypeStruct((LANES, s), F32),
    ]
    return pl.pallas_call(
        functools.partial(_proj_kernel, n_attn=na),
        grid=(s // tm, na + nb + 1),
        in_specs=[
            pl.BlockSpec((tm, d), lambda i, j: (i, 0)),
            pl.BlockSpec((1, d), lambda i, j: (0, 0)),
            pl.BlockSpec((d, tn), lambda i, j: (0, (jnp.minimum(j, na - 1) + nq) % na)),
            pl.BlockSpec((d, tn), lambda i, j: (0, jnp.maximum(j - na, 0))),
        ],
        out_specs=[
            pl.BlockSpec((2, ng, tm // CMP_STRIDE, CMP_STRIDE * dh), lambda i, j: (0, 0, i, 0)),
            pl.BlockSpec((ng, tm, dh), lambda i, j: (0, i, 0)),
            pl.BlockSpec((ng, tm, 2 * dh), lambda i, j: (0, i, 0)),
            pl.BlockSpec((ng, dh + BF16_ROWS, tm), lambda i, j: (0, 0, i)),
            pl.BlockSpec((ng, dh + BF16_ROWS, tm), lambda i, j: (0, 0, i)),
            pl.BlockSpec((tm, tn), lambda i, j: (i, jnp.clip(j - PROJ_KV_TILES, 0, nq - 1))),
            pl.BlockSpec((tm, tn), lambda i, j: (i, jnp.clip(j - na, 0, nb - 1))),
            pl.BlockSpec((LANES, tm), lambda i, j: (0, i)),
        ],
        out_shape=shapes,
        scratch_shapes=[pltpu.VMEM((tm, d), BF16), pltpu.VMEM((tm, dh), F32)],
        compiler_params=_params("parallel", "arbitrary"),
        name="proj",
    )(x, g.reshape(1, d), w_a, w_b)


def _compress_kernel(r_ref, w1_ref, pe_ref, b1_ref, w2_ref, o_ref, ot_ref):
    half = r_ref.shape[-1]
    r = r_ref[0, 0]
    bias = _dot(pe_ref[0].astype(BF16), w1_ref[0]) + b1_ref[0]
    top = _dot(r, w1_ref[0, :half, :])
    bot = _dot(r, w1_ref[0, half:, :])
    ncp = r.shape[0]
    hid = top + pltpu.roll(bot, ncp - 1, 0) + bias
    out = _dot(jax.nn.gelu(hid).astype(BF16), w2_ref[0])
    o_ref[0, 0] = out.astype(o_ref.dtype)
    dh = out.shape[1]
    out_t = jnp.concatenate([out, jnp.zeros((ncp, LANES - dh), F32)], axis=1).T
    ot_ref[0, 0, :dh, :] = out_t[:dh].astype(ot_ref.dtype)
    ot_ref[0, 0, dh:, :] = _ones_row(ncp)


def _compress(r, w1, pe, b1, w2):
    _, g, ncp, half = r.shape
    hidden = w1.shape[-1]
    dh = w2.shape[-1]
    return pl.pallas_call(
        _compress_kernel,
        grid=(2, g),
        in_specs=[
            pl.BlockSpec((1, 1, ncp, half), lambda s, gg: (s, gg, 0, 0)),
            pl.BlockSpec((1, 2 * half, hidden), lambda s, gg: (s, 0, 0)),
            pl.BlockSpec((1, 1, 2 * half), lambda s, gg: (s, 0, 0)),
            pl.BlockSpec((1, 1, hidden), lambda s, gg: (s, 0, 0)),
            pl.BlockSpec((1, hidden, dh), lambda s, gg: (s, 0, 0)),
        ],
        out_specs=[pl.BlockSpec((1, 1, ncp, dh), lambda s, gg: (s, gg, 0, 0)),
                   pl.BlockSpec((1, 1, dh + BF16_ROWS, ncp), lambda s, gg: (s, gg, 0, 0))],
        out_shape=[jax.ShapeDtypeStruct((2, g, ncp, dh), BF16),
                   jax.ShapeDtypeStruct((2, g, dh + BF16_ROWS, ncp), BF16)],
        compiler_params=_params("parallel", "parallel"),
        name="compress",
    )(r, w1, pe, b1, w2)


def _nsa_kernel(q_ref, gt_ref, kc_ref, vct_ref, ks_ref, vst_ref, kw_ref, vwt_ref, o_ref, *scratch):
    g = pl.program_id(0)
    step = pl.program_id(1)
    cend_ref, wrel_ref = scratch[7:9]

    @pl.when(step == 0)
    def _():
        lane = lax.broadcasted_iota(jnp.int32, cend_ref.shape, 1)
        cend_ref[...] = lax.broadcasted_iota(jnp.int32, cend_ref.shape, 0) * CMP_STRIDE + (CMP_BLOCK - 1) - lane
        wrel_ref[...] = (lax.broadcasted_iota(jnp.int32, wrel_ref.shape, 0)
                         - lax.broadcasted_iota(jnp.int32, wrel_ref.shape, 1))

    def block(sub, carry):
        rows = pl.ds(pl.multiple_of(sub * QUERY_BLOCK, QUERY_BLOCK), QUERY_BLOCK)
        _nsa_block(g, step * NSA_STEP_BLOCKS + sub, q_ref.at[rows, :], gt_ref.at[:, rows], kc_ref, vct_ref, ks_ref,
                   vst_ref, kw_ref, vwt_ref, o_ref.at[rows, :], *scratch)
        return carry

    lax.fori_loop(0, NSA_STEP_BLOCKS, block, 0)


def _nsa_block(g, i, q_ref, gt_ref, kc_ref, vct_ref, ks_ref, vst_ref, kw_ref, vwt_ref, o_ref,
               qa_ref, sel_ref, s0_ref, s1_ref, acc_ref, part_ref, blkf_ref, cend_ref, wrel_ref, pp_ref, gates_ref):
    s_refs = (s0_ref, s1_ref)
    qb = QUERY_BLOCK
    start = i * qb
    nb = sel_ref.shape[0]
    ncp = kc_ref.shape[1]
    s_len = ks_ref.shape[1]
    hq = NSA_HPG * qb
    t_row = start + lax.broadcasted_iota(jnp.int32, (1, qb), 1)

    qt = (q_ref[...].astype(F32) * (NSA_DIM ** -0.5 * LOG2E)).T
    q_t = jnp.concatenate([qt[h * NSA_DIM:(h + 1) * NSA_DIM] for h in range(NSA_HPG)], axis=1).astype(BF16)
    qa_ref[:NSA_DIM, :] = q_t
    qa_ref[NSA_DIM:, :] = jnp.zeros((qa_ref.shape[0] - NSA_DIM, hq), BF16)

    jt = t_row // SEL_BLOCK
    picked = -2.0

    gates_ref[...] = gt_ref[...]

    def gate(branch):
        rows = [gates_ref[pl.ds(g * (NSA_HPG * 3) + h * 3 + branch, 1), :] for h in range(NSA_HPG)]
        return jax.nn.sigmoid(jnp.concatenate(rows, axis=1))

    def exps(s, bias):
        out = []
        for h in range(NSA_HPG):
            sh = s[:, h * qb:(h + 1) * qb] + bias
            out.append(jnp.exp2(sh - jnp.max(sh, axis=0, keepdims=True)))
        return out

    def compressed_and_select(nck, nbk):
        wk = WINDOW + qb
        k0w = pl.multiple_of(jnp.maximum(start - WINDOW, 0), qb)
        sc = _dot(kc_ref[0, :nck, :], q_t)
        sw = _dot(kw_ref[0, pl.ds(k0w, wk), :], q_t)
        e_cmp = exps(sc, jnp.where(cend_ref[:nck, :] <= start, 0.0, MASK_VALUE))
        o_cmp = _dot(vct_ref[0, :, :nck], jnp.concatenate([e.astype(BF16) for e in e_cmp], axis=1))
        wrel = wrel_ref[...]
        e_win = exps(sw, jnp.where((wrel <= start - k0w) & (wrel > start - k0w - WINDOW), 0.0, MASK_VALUE))
        o_win = _dot(vwt_ref[0, :, pl.ds(k0w, wk)], jnp.concatenate([e.astype(BF16) for e in e_win], axis=1))
        inv_cmp = jnp.where(jnp.concatenate([t_row >= CMP_BLOCK - 1] * NSA_HPG, axis=1),
                            1.0 / o_cmp[NSA_DIM:NSA_DIM + 1, :], 0.0)
        psum = e_cmp[0] * inv_cmp[:, :qb]
        for h in range(1, NSA_HPG):
            psum = psum + e_cmp[h] * inv_cmp[:, h * qb:(h + 1) * qb]
        part_ref[...] = (gate(0) * inv_cmp * o_cmp[:NSA_DIM]
                         + gate(2) * (1.0 / o_win[NSA_DIM:NSA_DIM + 1, :]) * o_win[:NSA_DIM])
        pp_ref[:SUBLANES, :] = jnp.zeros((SUBLANES, qb), F32)
        pp_ref[SUBLANES:SUBLANES + nck, :] = psum
        imp = None
        for k, w in enumerate(COV_BAND):
            tap = pp_ref[pl.ds(SUBLANES - COV_LEAD + k, nbk, stride=COV_RATIO), :]
            tap = tap if w == 1.0 else w * tap
            imp = tap if imp is None else imp + tap

        blk = lax.broadcasted_iota(jnp.int32, (nbk, qb), 0)
        forced = (blk == 0) | (blk == jt) | (blk == jt - 1)
        score = jnp.where(forced, picked, jnp.where(blk <= jt, imp, -1.0))
        blkf_ref[:nbk, :] = blk.astype(F32)

        def pick(_, score):
            mx = jnp.max(score, axis=0, keepdims=True)
            cand = jnp.where(score == mx, blkf_ref[:nbk, :], float(nbk))
            return jnp.where(cand == jnp.min(cand, axis=0, keepdims=True), picked, score)

        score = lax.fori_loop(0, min(SEL_TOPK, nbk) - N_FORCED, pick, score)
        sel_ref[:nbk, :] = jnp.where((score == picked) & (blk <= jt), 0.0, MASK_VALUE)

    bucket_ok = s_len % (NSA_BUCKETS * SEL_TILE) == 0 and nb // NSA_BUCKETS >= SEL_TOPK
    n_bucket = NSA_BUCKETS if bucket_ok else 1
    bucket = (start + qb - 1) // (s_len // n_bucket)
    for b in range(n_bucket):
        pl.when(bucket == b)(functools.partial(
            compressed_and_select, (b + 1) * (ncp // n_bucket), (b + 1) * (nb // n_bucket)))

    tk = SEL_TILE

    def scores(slot, kt):
        k0 = pl.multiple_of(kt * tk, tk)
        b8 = sel_ref[pl.ds(pl.multiple_of(kt * SEL_BPT, SEL_BPT), SEL_BPT), :]
        b16 = jnp.concatenate([b8, jnp.zeros((BF16_ROWS - SEL_BPT, qb), F32)], axis=0).astype(BF16)
        qa_ref[NSA_DIM:NSA_DIM + BF16_ROWS, :] = jnp.concatenate([b16] * NSA_HPG, axis=1)
        s = _dot(ks_ref[0, pl.ds(k0, tk), :], qa_ref[...])
        s_refs[slot][...] = s
        return jnp.max(s, axis=0, keepdims=True)

    def update(slot, kt, mt, m, causal):
        k0 = pl.multiple_of(kt * tk, tk)
        s = s_refs[slot][...]
        if causal:
            kpos = k0 + lax.broadcasted_iota(jnp.int32, (tk, qb), 0)
            s = s + jnp.concatenate([jnp.where(kpos <= t_row, 0.0, MASK_VALUE)] * NSA_HPG, axis=1)
            mt = jnp.max(s, axis=0, keepdims=True)
        m_new = jnp.maximum(m, mt)
        p = jnp.exp2(s - m_new).astype(BF16)
        acc_ref[...] = jnp.exp2(m - m_new) * acc_ref[...] + _dot(vst_ref[0, :, pl.ds(k0, tk)], p)
        return m_new

    def accumulate(pend):
        p, k0, alpha = pend
        pv = _dot(vst_ref[0, :, pl.ds(k0, SEL_CHUNK)], p)
        acc_ref[...] = (acc_ref[...] if alpha is None else alpha * acc_ref[...]) + pv

    def run(first, n_tiles, carry):
        mt, m = carry
        pend = None
        for k in range(n_tiles):
            cur, nxt = s_refs[k % 2], s_refs[(k + 1) % 2]
            k0c = pl.multiple_of((first + k) * tk, tk)
            k0n = pl.multiple_of((first + k + 1) * tk, tk)
            b8 = sel_ref[pl.ds(pl.multiple_of((first + k + 1) * SEL_BPT, SEL_BPT), SEL_BPT), :]
            b16 = jnp.concatenate([b8, jnp.zeros((BF16_ROWS - SEL_BPT, qb), F32)], axis=0).astype(BF16)
            qa_ref[NSA_DIM:NSA_DIM + BF16_ROWS, :] = jnp.concatenate([b16] * NSA_HPG, axis=1)
            m_new = jnp.maximum(m, mt)
            alpha = jnp.exp2(m - m_new)
            mt = None
            for c in range(tk // SEL_CHUNK):
                rows = pl.ds(c * SEL_CHUNK, SEL_CHUNK)
                s_n = _dot(ks_ref[0, pl.ds(k0n + c * SEL_CHUNK, SEL_CHUNK), :], qa_ref[...])
                nxt[rows, :] = s_n
                mt_c = jnp.max(s_n, axis=0, keepdims=True)
                mt = mt_c if mt is None else jnp.maximum(mt, mt_c)
                p = jnp.exp2(cur[rows, :] - m_new).astype(BF16)
                if pend is not None:
                    accumulate(pend)
                pend = (p, k0c + c * SEL_CHUNK, alpha if c == 0 else None)
            m = m_new
        accumulate(pend)
        return mt, m

    acc_ref[...] = jnp.zeros(acc_ref.shape, F32)
    last = (start + qb - 1) // tk
    carry = (scores(0, 0), jnp.full((1, hq), MASK_VALUE, F32))
    done = 0
    for unroll in SEL_UNROLLS:
        trips = (last - done) // unroll
        carry = lax.fori_loop(0, trips, lambda it, c, d=done, u=unroll: run(d + it * u, u, c), carry)
        done = done + trips * unroll
    mt_a, m = carry

    @pl.when(last % 2 == 0)
    def _():
        update(0, last, mt_a, m, True)

    @pl.when(last % 2 == 1)
    def _():
        mt_b = scores(1, last)
        update(1, last, mt_b, update(0, last - 1, mt_a, m, False), True)

    o_sel = acc_ref[:NSA_DIM, :] * (1.0 / acc_ref[NSA_DIM:NSA_DIM + 1, :])

    out_t = part_ref[...] + gate(1) * o_sel
    outs = [out_t[:, h * qb:(h + 1) * qb] for h in range(NSA_HPG)]
    o_ref[...] = jnp.concatenate(outs, axis=0).T.astype(o_ref.dtype)


def _nsa(q, gates_t, kc, vc_aug_t, ks_aug, vs_aug_t, kw, vw_aug_t):
    s = q.shape[0]
    assert s % (2 * SEL_TILE) == 0 and s >= WINDOW + QUERY_BLOCK
    gdim = NSA_HPG * NSA_DIM
    ncp = kc.shape[1]
    assert ncp * CMP_STRIDE == s and s // SEL_BLOCK * COV_RATIO == ncp
    nb = s // SEL_BLOCK
    ka = ks_aug.shape[-1]
    va = vs_aug_t.shape[1]
    hq = NSA_HPG * QUERY_BLOCK
    per_group = lambda shape: pl.BlockSpec((1,) + shape, lambda g, i: (g, 0, 0))
    rows = NSA_STEP_BLOCKS * QUERY_BLOCK
    assert s % rows == 0
    return pl.pallas_call(
        _nsa_kernel,
        grid=(NSA_GROUPS, s // rows),
        in_specs=[
            pl.BlockSpec((rows, gdim), lambda g, i: (i, g)),
            pl.BlockSpec((NSA_HEADS * 3, rows), lambda g, i: (0, i)),
            per_group((ncp, NSA_DIM)),
            per_group((va, ncp)),
            per_group((s, ka)),
            per_group((va, s)),
            per_group((s, NSA_DIM)),
            per_group((va, s)),
        ],
        out_specs=pl.BlockSpec((rows, gdim), lambda g, i: (i, g)),
        out_shape=jax.ShapeDtypeStruct((s, NSA_HEADS * NSA_DIM), BF16),
        scratch_shapes=[pltpu.VMEM((ka, hq), BF16), pltpu.VMEM((nb, QUERY_BLOCK), F32),
                        pltpu.VMEM((SEL_TILE, hq), F32), pltpu.VMEM((SEL_TILE, hq), F32),
                        pltpu.VMEM((va, hq), F32),
                        pltpu.VMEM((NSA_DIM, hq), F32),
                        pltpu.VMEM((nb, QUERY_BLOCK), F32),
                        pltpu.VMEM((ncp, QUERY_BLOCK), jnp.int32),
                        pltpu.VMEM((WINDOW + QUERY_BLOCK, QUERY_BLOCK), jnp.int32),
                        pltpu.VMEM((SUBLANES + ncp, QUERY_BLOCK), F32),
                        pltpu.VMEM((NSA_HEADS * 3, QUERY_BLOCK), F32)],
        compiler_params=_params("parallel", "arbitrary"),
        name="nsa",
    )(q, gates_t, kc, vc_aug_t, ks_aug, vs_aug_t, kw, vw_aug_t)


def _sgu_kernel(u_ref, v_ref, lng_ref, lnb_ref, ws_ref, bs_ref, o_ref):
    c = SGU_CHUNK
    tm = u_ref.shape[0]
    v = jax.nn.gelu(v_ref[...])
    mu = jnp.mean(v, axis=-1, keepdims=True)
    var = jnp.mean(jnp.square(v - mu), axis=-1, keepdims=True)
    vn = ((v - mu) * lax.rsqrt(var + NORM_EPS) * lng_ref[...] + lnb_ref[...]).astype(BF16)
    u = jax.nn.gelu(u_ref[...])
    tri = lax.broadcasted_iota(jnp.int32, (c, c), 0) >= lax.broadcasted_iota(jnp.int32, (c, c), 1)
    for g in range(SGU_GROUPS):
        w = jnp.where(tri, ws_ref[g], 0.0).astype(BF16)
        cols = slice(g * c, (g + 1) * c)
        rhs = jnp.concatenate([vn[k * c:(k + 1) * c, cols] for k in range(tm // c)], axis=1)
        mixed = _dot(w, rhs)
        for k in range(tm // c):
            rows = slice(k * c, (k + 1) * c)
            o_ref[rows, cols] = (u[rows, cols] * (mixed[:, rows] + bs_ref[g])).astype(o_ref.dtype)


def _sgu(proj, lng, lnb, ws, bs, tm):
    s = proj.shape[0]
    w = lng.shape[0]
    c = SGU_CHUNK
    bs_b = jnp.broadcast_to(bs[:, :, None], (SGU_GROUPS, c, c))
    return pl.pallas_call(
        _sgu_kernel,
        grid=(s // tm,),
        in_specs=[
            pl.BlockSpec((tm, w), lambda i: (i, 0)),
            pl.BlockSpec((tm, w), lambda i: (i, 1)),
            pl.BlockSpec((1, w), lambda i: (0, 0)),
            pl.BlockSpec((1, w), lambda i: (0, 0)),
            pl.BlockSpec((SGU_GROUPS, c, c), lambda i: (0, 0, 0)),
            pl.BlockSpec((SGU_GROUPS, c, c), lambda i: (0, 0, 0)),
        ],
        out_specs=pl.BlockSpec((tm, w), lambda i: (i, 0)),
        out_shape=jax.ShapeDtypeStruct((s, w), BF16),
        compiler_params=_params("parallel"),
        name="sgu",
    )(proj, proj, lng.reshape(1, w), lnb.reshape(1, w), ws, bs_b)


def _mix_kernel(oa_ref, ob_ref, ga_ref, gb_ref, x_ref, pa_ref, pb_ref, wo_ref,
                gc_ref, wq_ref, mkt_ref, mv_ref, wmo_ref, o_ref):
    a = _dot(oa_ref[...], pa_ref[...])
    b = _dot(ob_ref[...], pb_ref[...])
    merged = jax.nn.sigmoid(ga_ref[...]) * a + jax.nn.sigmoid(gb_ref[...]) * b
    x = x_ref[...] + _dot(merged.astype(BF16), wo_ref[...])
    h = _rms(x, gc_ref[...]).astype(BF16)
    mq = (_dot(h, wq_ref[...]) * (MEM_DIM ** -0.5)).astype(BF16)
    outs = []
    for hh in range(MEM_HEADS):
        s = _dot(mq[:, hh * MEM_DIM:(hh + 1) * MEM_DIM], mkt_ref[hh])
        e = jnp.exp(s - jnp.max(s, axis=-1, keepdims=True))
        p = e / jnp.sum(e, axis=-1, keepdims=True)
        outs.append(_dot(p.astype(BF16), mv_ref[hh]).astype(BF16))
    o_ref[...] = x + _dot(jnp.concatenate(outs, axis=1), wmo_ref[...])


def _mix(o_a, o_b, proj, x, p_a, p_b, w_o, gate_block, g_cross, w_q, mk_t, mv, w_mo, tm):
    s, d = x.shape
    wa = o_a.shape[1]
    wb = o_b.shape[1]
    mw = w_q.shape[1]
    m = mv.shape[1]
    resident = lambda shape: pl.BlockSpec(shape, lambda i: (0,) * len(shape), pipeline_mode=pl.Buffered(1))
    return pl.pallas_call(
        _mix_kernel,
        grid=(s // tm,),
        in_specs=[
            pl.BlockSpec((tm, wa), lambda i: (i, 0)),
            pl.BlockSpec((tm, wb), lambda i: (i, 0)),
            pl.BlockSpec((tm, d), lambda i: (i, gate_block)),
            pl.BlockSpec((tm, d), lambda i: (i, gate_block + 1)),
            pl.BlockSpec((tm, d), lambda i: (i, 0)),
            resident((wa, d)),
            resident((wb, d)),
            resident((d, d)),
            resident((1, d)),
            resident((d, mw)),
            resident((MEM_HEADS, MEM_DIM, m)),
            resident((MEM_HEADS, m, MEM_DIM)),
            resident((mw, d)),
        ],
        out_specs=pl.BlockSpec((tm, d), lambda i: (i, 0)),
        out_shape=jax.ShapeDtypeStruct((s, d), F32),
        compiler_params=_params("parallel"),
        name="mix",
    )(o_a, o_b, proj, proj, x, p_a, p_b, w_o, g_cross.reshape(1, d), w_q, mk_t, mv, w_mo)


def _ffn_kernel(x_ref, g_ref, wg_ref, wu_ref, wo_ref, gf_ref, o_ref, h_ref, acc_ref, *, final_norm):
    j = pl.program_id(1)

    @pl.when(j == 0)
    def _():
        h_ref[...] = _rms(x_ref[...], g_ref[...]).astype(BF16)
        acc_ref[...] = jnp.zeros_like(acc_ref)

    h = h_ref[...]
    th = wg_ref.shape[1]
    out = None
    for c in range(th // FFN_CHUNK):
        cs = slice(c * FFN_CHUNK, (c + 1) * FFN_CHUNK)
        act = (jax.nn.silu(_dot(h, wg_ref[:, cs])) * _dot(h, wu_ref[:, cs])).astype(BF16)
        part = _dot(act, wo_ref[cs, :])
        out = part if out is None else out + part
    acc_ref[...] += out

    @pl.when(j == pl.num_programs(1) - 1)
    def _():
        y = x_ref[...] + acc_ref[...]
        o_ref[...] = _rms(y, gf_ref[...]) if final_norm else y


def _ffn(x, g, w_in, w_out, g_final, final_norm, tm, th):
    s, d = x.shape
    hidden = w_out.shape[0]
    nh = hidden // th
    return pl.pallas_call(
        functools.partial(_ffn_kernel, final_norm=final_norm),
        grid=(s // tm, nh),
        in_specs=[
            pl.BlockSpec((tm, d), lambda i, j: (i, 0)),
            pl.BlockSpec((1, d), lambda i, j: (0, 0)),
            pl.BlockSpec((d, th), lambda i, j: (0, j)),
            pl.BlockSpec((d, th), lambda i, j: (0, j + nh)),
            pl.BlockSpec((th, d), lambda i, j: (j, 0)),
            pl.BlockSpec((1, d), lambda i, j: (0, 0)),
        ],
        out_specs=pl.BlockSpec((tm, d), lambda i, j: (i, 0)),
        out_shape=jax.ShapeDtypeStruct((s, d), F32),
        scratch_shapes=[pltpu.VMEM((tm, d), BF16), pltpu.VMEM((tm, d), F32)],
        compiler_params=_params("parallel", "arbitrary"),
        name="ffn",
    )(x, g.reshape(1, d), w_in, w_in, w_out, g_final.reshape(1, d))


def _layer(x, mem, norm_mix_g, w_in, cmp_pe_k, cmp_k_w1, cmp_k_b1, cmp_k_w2, cmp_pe_v, cmp_v_w1, cmp_v_b1, cmp_v_w2,
           sgu_ln_g, sgu_ln_b, sgu_ws, sgu_b, w_proj_a, w_proj_b, w_mix_out, norm_cross_g, norm_mem_g,
           w_mq, w_mkv, w_mo, norm_ffn_g):
    s, d = x.shape
    qw = NSA_HEADS * NSA_DIM
    kvw = NSA_GROUPS * NSA_DIM
    sguw = sgu_ln_g.shape[0]
    ngate = NSA_HEADS * 3
    o_kv = qw
    o_gate = o_kv + N_KV_STREAMS * kvw
    o_u = o_gate + ngate

    w_a = w_in[:, :o_gate].astype(BF16)
    w_b = jnp.pad(w_in[:, o_u:].astype(BF16), ((0, 0), (0, PROJ_TN)))
    w_b = lax.dynamic_update_slice(w_b, w_in[:, o_gate:o_u].astype(BF16), (0, w_in.shape[1] - o_u))
    tm_proj = min(1024, s)
    nat, kw, ks_aug, vs_aug_t, vw_aug_t, q, proj_b, gates_t = _proj(x, norm_mix_g, w_a, w_b, tm_proj)

    ncp = s // CMP_STRIDE
    w1 = jnp.stack([cmp_k_w1, cmp_v_w1]).astype(BF16)
    pe = jnp.stack([cmp_pe_k.reshape(1, -1), cmp_pe_v.reshape(1, -1)])
    b1 = jnp.stack([cmp_k_b1.reshape(1, -1), cmp_v_b1.reshape(1, -1)])
    w2 = jnp.stack([cmp_k_w2, cmp_v_w2]).astype(BF16)
    cmp, cmp_t = _compress(nat, w1, pe, b1, w2)
    o_a = _nsa(q, gates_t, cmp[0], cmp_t[1], ks_aug, vs_aug_t, kw, vw_aug_t)

    o_b = _sgu(proj_b, sgu_ln_g, sgu_ln_b, sgu_ws, sgu_b, min(512, s))

    m = mem.shape[0]
    mw = MEM_HEADS * MEM_DIM
    mkv = _norm_matmul(mem, norm_mem_g, w_mkv.astype(BF16), F32, m, mw)
    mk_t = mkv[:, :mw].reshape(m, MEM_HEADS, MEM_DIM).transpose(1, 2, 0).astype(BF16)
    mv = mkv[:, mw:].reshape(m, MEM_HEADS, MEM_DIM).transpose(1, 0, 2).astype(BF16)
    return _mix(o_a, o_b, proj_b, x, w_proj_a.astype(BF16), w_proj_b.astype(BF16), w_mix_out.astype(BF16),
                (2 * sguw) // d, norm_cross_g, w_mq.astype(BF16), mk_t, mv, w_mo.astype(BF16), min(256, s))


def kernel(x, mem, norm_mix_g, w_in, cmp_pe_k, cmp_k_w1, cmp_k_b1, cmp_k_w2, cmp_pe_v, cmp_v_w1, cmp_v_b1, cmp_v_w2, sgu_ln_g, sgu_ln_b, sgu_ws, sgu_b, w_proj_a, w_proj_b, w_mix_out, norm_cross_g, norm_mem_g, w_mq, w_mkv, w_mo, norm_ffn_g, w_ffn_in, w_ffn_out, norm_final_g):
    b, s, d = x.shape
    depth = w_in.shape[0]
    outs = []
    for bi in range(b):
        xb = x[bi]
        for l in range(depth):
            last = l == depth - 1
            xb = _layer(xb, mem[bi], norm_mix_g[l], w_in[l], cmp_pe_k[l], cmp_k_w1[l], cmp_k_b1[l], cmp_k_w2[l],
                        cmp_pe_v[l], cmp_v_w1[l], cmp_v_b1[l], cmp_v_w2[l], sgu_ln_g[l], sgu_ln_b[l], sgu_ws[l],
                        sgu_b[l], w_proj_a[l], w_proj_b[l], w_mix_out[l], norm_cross_g[l], norm_mem_g[l],
                        w_mq[l], w_mkv[l], w_mo[l], norm_ffn_g[l])
            xb = _ffn(xb, norm_ffn_g[l], w_ffn_in[l].astype(BF16), w_ffn_out[l].astype(BF16), norm_final_g, last,
                      min(512, s), 512)
        outs.append(xb)
    return jnp.stack(outs)
```

```python
import functools
import math

import jax
import jax.numpy as jnp
from jax import lax
from jax.experimental import pallas as pl
from jax.experimental.pallas import tpu as pltpu

F32 = jnp.float32
BF16 = jnp.bfloat16

NORM_EPS = 1e-6
MASK_VALUE = -1e30
N_FORCED = 3
LOG2E = math.log2(math.e)

NSA_HEADS = 16
NSA_GROUPS = 4
NSA_HPG = NSA_HEADS // NSA_GROUPS
NSA_DIM = 64
N_KV_STREAMS = 6
CMP_BLOCK = 32
CMP_STRIDE = 16
SEL_BLOCK = 64
SEL_TOPK = 16
WINDOW = 512
QUERY_BLOCK = 128
SGU_GROUPS = 8
SGU_CHUNK = 128
MEM_HEADS = 4
MEM_DIM = 128

LANES = 128
SUBLANES = 8
COV_RATIO = SEL_BLOCK // CMP_STRIDE
COV_LEAD = CMP_BLOCK // CMP_STRIDE - 1
COV_BAND = tuple(
    max(min((k - COV_LEAD) * CMP_STRIDE + CMP_BLOCK, SEL_BLOCK) - max((k - COV_LEAD) * CMP_STRIDE, 0), 0) / CMP_BLOCK
    for k in range(COV_RATIO + COV_LEAD))
assert SEL_BLOCK % CMP_STRIDE == 0 and CMP_BLOCK % CMP_STRIDE == 0 and COV_LEAD <= SUBLANES
SEL_TILE = 512
SEL_BPT = SEL_TILE // SEL_BLOCK
SEL_CHUNK = 256
SEL_UNROLLS = (16, 8, 4, 2)
NSA_STEP_BLOCKS = 4
NSA_BUCKETS = 8
BF16_ROWS = 16
FFN_CHUNK = 256
VMEM_LIMIT = 62 * 1024 * 1024


def _params(*sem):
    return pltpu.CompilerParams(dimension_semantics=sem, vmem_limit_bytes=VMEM_LIMIT)


def _rms(x, g):
    return x * lax.rsqrt(jnp.mean(x * x, axis=-1, keepdims=True) + NORM_EPS) * g


def _dot(a, b):
    return jnp.dot(a, b, preferred_element_type=F32)


def _ones_row(n):
    return jnp.where(lax.broadcasted_iota(jnp.int32, (BF16_ROWS, n), 0) == 0, 1.0, 0.0).astype(BF16)


def _norm_matmul_kernel(x_ref, g_ref, w_ref, o_ref, h_ref):
    @pl.when(pl.program_id(1) == 0)
    def _():
        h_ref[...] = _rms(x_ref[...], g_ref[...]).astype(BF16)

    o_ref[...] = _dot(h_ref[...], w_ref[...]).astype(o_ref.dtype)


def _norm_matmul(x, g, w, out_dtype, tm, tn):
    s, d = x.shape
    n = w.shape[1]
    return pl.pallas_call(
        _norm_matmul_kernel,
        grid=(s // tm, n // tn),
        in_specs=[
            pl.BlockSpec((tm, d), lambda i, j: (i, 0)),
            pl.BlockSpec((1, d), lambda i, j: (0, 0)),
            pl.BlockSpec((d, tn), lambda i, j: (0, j)),
        ],
        out_specs=pl.BlockSpec((tm, tn), lambda i, j: (i, j)),
        out_shape=jax.ShapeDtypeStruct((s, n), out_dtype),
        scratch_shapes=[pltpu.VMEM((tm, d), BF16)],
        compiler_params=_params("parallel", "arbitrary"),
        name="norm_matmul",
    )(x, g.reshape(1, d), w)


PROJ_TN = 512


def _proj_kernel(x_ref, g_ref, wa_ref, wb_ref, nat_ref, kw_ref, ksa_ref, vsa_ref, vwa_ref, q_ref, o_ref, gt_ref,
                 h_ref, stage_ref):
    i = pl.program_id(0)
    j = pl.program_id(1)
    tm = x_ref.shape[0]
    dh = NSA_DIM
    ng = NSA_GROUPS
    tn = PROJ_TN
    qw = q_ref.shape[1]

    @pl.when(j == 0)
    def _():
        h_ref[...] = _rms(x_ref[...], g_ref[...]).astype(BF16)
        h = h_ref[...]

        def tile(t):
            return _dot(h, wa_ref[:, qw + t * tn:qw + (t + 1) * tn])

        def cols(res, c):
            return res[:, c * dh:(c + 1) * dh].astype(BF16)

        def cols_t(res, c):
            slab = res[:, (c // 2) * 2 * dh:(c // 2 + 1) * 2 * dh].T
            return slab[(c % 2) * dh:(c % 2 + 1) * dh].astype(BF16)

        res = tile(0)
        for c in range(2 * ng):
            stage_ref[...] = res[:, c * dh:(c + 1) * dh]
            for t in range(0, CMP_STRIDE, 2):
                pair = [stage_ref[pl.ds(t + u, tm // CMP_STRIDE, stride=CMP_STRIDE), :] for u in range(2)]
                nat_ref[c // ng, c % ng, :, t * dh:(t + 2) * dh] = jnp.concatenate(pair, axis=1).astype(BF16)

        res = tile(1)
        pos = i * tm + lax.broadcasted_iota(jnp.int32, (tm, dh), 0)
        lane = lax.broadcasted_iota(jnp.int32, (tm, dh), 1)
        onehot = jnp.where((pos // SEL_BLOCK) % SEL_BPT == lane, 1.0, 0.0).astype(BF16)
        for g in range(ng):
            ksa_ref[g, :, :dh] = cols(res, g)
            ksa_ref[g, :, dh:] = onehot
            vsa_ref[g, :dh, :] = cols_t(res, ng + g)
            vsa_ref[g, dh:, :] = _ones_row(tm)

        res = tile(2)
        for g in range(ng):
            kw_ref[g] = cols(res, g)
            vwa_ref[g, :dh, :] = cols_t(res, ng + g)
            vwa_ref[g, dh:, :] = _ones_row(tm)

        for t in range(qw // tn):
            q_ref[:, t * tn:(t + 1) * tn] = _dot(h, wa_ref[:, t * tn:(t + 1) * tn]).astype(BF16)

    @pl.when(j >= 1)
    def _():
        res = _dot(h_ref[...], wb_ref[...])
        o_ref[...] = res

        @pl.when(j == 1)
        def _():
            gt_ref[...] = res[:, :LANES].T


def _proj(x, g, w_a, w_b, tm):
    s, d = x.shape
    dh, ng, tn = NSA_DIM, NSA_GROUPS, PROJ_TN
    qw = NSA_HEADS * dh
    assert 2 * ng * dh == tn and qw % tn == 0 and w_a.shape[1] == qw + N_KV_STREAMS // 2 * tn
    n = w_b.shape[1] - tn
    nb = n // tn
    shapes = [
        jax.ShapeDtypeStruct((2, ng, s // CMP_STRIDE, CMP_STRIDE * dh), BF16),
        jax.ShapeDtypeStruct((ng, s, dh), BF16),
        jax.ShapeDtypeStruct((ng, s, 2 * dh), BF16),
        jax.ShapeDtypeStruct((ng, dh + BF16_ROWS, s), BF16),
        jax.ShapeDtypeStruct((ng, dh + BF16_ROWS, s), BF16),
        jax.ShapeDtypeStruct((s, NSA_HEADS * dh), BF16),
        jax.ShapeDtypeStruct((s, n), F32),
        jax.ShapeDtypeStruct((LANES, s), F32),
    ]
    return pl.pallas_call(
        _proj_kernel,
        grid=(s // tm, 2 + nb),
        in_specs=[
            pl.BlockSpec((tm, d), lambda i, j: (i, 0)),
            pl.BlockSpec((1, d), lambda i, j: (0, 0)),
            pl.BlockSpec(w_a.shape, lambda i, j: (0, 0), pipeline_mode=pl.Buffered(1)),
            pl.BlockSpec((d, tn), lambda i, j: (0, jnp.where(j <= 1, nb, j - 2))),
        ],
        out_specs=[
            pl.BlockSpec((2, ng, tm // CMP_STRIDE, CMP_STRIDE * dh), lambda i, j: (0, 0, i, 0)),
            pl.BlockSpec((ng, tm, dh), lambda i, j: (0, i, 0)),
            pl.BlockSpec((ng, tm, 2 * dh), lambda i, j: (0, i, 0)),
            pl.BlockSpec((ng, dh + BF16_ROWS, tm), lambda i, j: (0, 0, i)),
            pl.BlockSpec((ng, dh + BF16_ROWS, tm), lambda i, j: (0, 0, i)),
            pl.BlockSpec((tm, qw), lambda i, j: (i, 0)),
            pl.BlockSpec((tm, tn), lambda i, j: (i, jnp.maximum(j - 2, 0))),
            pl.BlockSpec((LANES, tm), lambda i, j: (0, i)),
        ],
        out_shape=shapes,
        scratch_shapes=[pltpu.VMEM((tm, d), BF16), pltpu.VMEM((tm, dh), F32)],
        compiler_params=_params("parallel", "arbitrary"),
        name="proj",
    )(x, g.reshape(1, d), w_a, w_b)


def _compress_kernel(r_ref, w1_ref, pe_ref, b1_ref, w2_ref, o_ref, ot_ref):
    half = r_ref.shape[-1]
    r = r_ref[0, 0]
    bias = _dot(pe_ref[0].astype(BF16), w1_ref[0]) + b1_ref[0]
    top = _dot(r, w1_ref[0, :half, :])
    bot = _dot(r, w1_ref[0, half:, :])
    ncp = r.shape[0]
    hid = top + pltpu.roll(bot, ncp - 1, 0) + bias
    out = _dot(jax.nn.gelu(hid).astype(BF16), w2_ref[0])
    o_ref[0, 0] = out.astype(o_ref.dtype)
    dh = out.shape[1]
    out_t = jnp.concatenate([out, jnp.zeros((ncp, LANES - dh), F32)], axis=1).T
    ot_ref[0, 0, :dh, :] = out_t[:dh].astype(ot_ref.dtype)
    ot_ref[0, 0, dh:, :] = _ones_row(ncp)


def _compress(r, w1, pe, b1, w2):
    _, g, ncp, half = r.shape
    hidden = w1.shape[-1]
    dh = w2.shape[-1]
    return pl.pallas_call(
        _compress_kernel,
        grid=(2, g),
        in_specs=[
            pl.BlockSpec((1, 1, ncp, half), lambda s, gg: (s, gg, 0, 0)),
            pl.BlockSpec((1, 2 * half, hidden), lambda s, gg: (s, 0, 0)),
            pl.BlockSpec((1, 1, 2 * half), lambda s, gg: (s, 0, 0)),
            pl.BlockSpec((1, 1, hidden), lambda s, gg: (s, 0, 0)),
            pl.BlockSpec((1, hidden, dh), lambda s, gg: (s, 0, 0)),
        ],
        out_specs=[pl.BlockSpec((1, 1, ncp, dh), lambda s, gg: (s, gg, 0, 0)),
                   pl.BlockSpec((1, 1, dh + BF16_ROWS, ncp), lambda s, gg: (s, gg, 0, 0))],
        out_shape=[jax.ShapeDtypeStruct((2, g, ncp, dh), BF16),
                   jax.ShapeDtypeStruct((2, g, dh + BF16_ROWS, ncp), BF16)],
        compiler_params=_params("parallel", "parallel"),
        name="compress",
    )(r, w1, pe, b1, w2)


def _nsa_kernel(q_ref, gt_ref, kc_ref, vct_ref, ks_ref, vst_ref, kw_ref, vwt_ref, o_ref, *scratch):
    g = pl.program_id(0)
    step = pl.program_id(1)
    cend_ref, wrel_ref = scratch[7:9]

    @pl.when(step == 0)
    def _():
        lane = lax.broadcasted_iota(jnp.int32, cend_ref.shape, 1)
        cend_ref[...] = lax.broadcasted_iota(jnp.int32, cend_ref.shape, 0) * CMP_STRIDE + (CMP_BLOCK - 1) - lane
        wrel_ref[...] = (lax.broadcasted_iota(jnp.int32, wrel_ref.shape, 0)
                         - lax.broadcasted_iota(jnp.int32, wrel_ref.shape, 1))

    def block(sub, carry):
        rows = pl.ds(pl.multiple_of(sub * QUERY_BLOCK, QUERY_BLOCK), QUERY_BLOCK)
        _nsa_block(g, step * NSA_STEP_BLOCKS + sub, q_ref.at[rows, :], gt_ref.at[:, rows], kc_ref, vct_ref, ks_ref,
                   vst_ref, kw_ref, vwt_ref, o_ref.at[rows, :], *scratch)
        return carry

    lax.fori_loop(0, NSA_STEP_BLOCKS, block, 0)


def _nsa_block(g, i, q_ref, gt_ref, kc_ref, vct_ref, ks_ref, vst_ref, kw_ref, vwt_ref, o_ref,
               qa_ref, sel_ref, s0_ref, s1_ref, acc_ref, part_ref, blkf_ref, cend_ref, wrel_ref, pp_ref, gates_ref):
    s_refs = (s0_ref, s1_ref)
    qb = QUERY_BLOCK
    start = i * qb
    nb = sel_ref.shape[0]
    ncp = kc_ref.shape[1]
    s_len = ks_ref.shape[1]
    hq = NSA_HPG * qb
    t_row = start + lax.broadcasted_iota(jnp.int32, (1, qb), 1)

    qt = (q_ref[...].astype(F32) * (NSA_DIM ** -0.5 * LOG2E)).T
    q_t = jnp.concatenate([qt[h * NSA_DIM:(h + 1) * NSA_DIM] for h in range(NSA_HPG)], axis=1).astype(BF16)
    qa_ref[:NSA_DIM, :] = q_t
    qa_ref[NSA_DIM:, :] = jnp.zeros((qa_ref.shape[0] - NSA_DIM, hq), BF16)

    jt = t_row // SEL_BLOCK
    picked = -2.0

    gates_ref[...] = gt_ref[...]

    def gate(branch):
        rows = [gates_ref[pl.ds(g * (NSA_HPG * 3) + h * 3 + branch, 1), :] for h in range(NSA_HPG)]
        return jax.nn.sigmoid(jnp.concatenate(rows, axis=1))

    def exps(s, bias):
        out = []
        for h in range(NSA_HPG):
            sh = s[:, h * qb:(h + 1) * qb] + bias
            out.append(jnp.exp2(sh - jnp.max(sh, axis=0, keepdims=True)))
        return out

    def compressed_and_select(nck, nbk):
        wk = WINDOW + qb
        k0w = pl.multiple_of(jnp.maximum(start - WINDOW, 0), qb)
        sc = _dot(kc_ref[0, :nck, :], q_t)
        sw = _dot(kw_ref[0, pl.ds(k0w, wk), :], q_t)
        e_cmp = exps(sc, jnp.where(cend_ref[:nck, :] <= start, 0.0, MASK_VALUE))
        o_cmp = _dot(vct_ref[0, :, :nck], jnp.concatenate([e.astype(BF16) for e in e_cmp], axis=1))
        wrel = wrel_ref[...]
        e_win = exps(sw, jnp.where((wrel <= start - k0w) & (wrel > start - k0w - WINDOW), 0.0, MASK_VALUE))
        o_win = _dot(vwt_ref[0, :, pl.ds(k0w, wk)], jnp.concatenate([e.astype(BF16) for e in e_win], axis=1))
        inv_cmp = jnp.where(jnp.concatenate([t_row >= CMP_BLOCK - 1] * NSA_HPG, axis=1),
                            1.0 / o_cmp[NSA_DIM:NSA_DIM + 1, :], 0.0)
        psum = e_cmp[0] * inv_cmp[:, :qb]
        for h in range(1, NSA_HPG):
            psum = psum + e_cmp[h] * inv_cmp[:, h * qb:(h + 1) * qb]
        part_ref[...] = (gate(0) * inv_cmp * o_cmp[:NSA_DIM]
                         + gate(2) * (1.0 / o_win[NSA_DIM:NSA_DIM + 1, :]) * o_win[:NSA_DIM])
        pp_ref[:SUBLANES, :] = jnp.zeros((SUBLANES, qb), F32)
        pp_ref[SUBLANES:SUBLANES + nck, :] = psum
        imp = None
        for k, w in enumerate(COV_BAND):
            tap = pp_ref[pl.ds(SUBLANES - COV_LEAD + k, nbk, stride=COV_RATIO), :]
            tap = tap if w == 1.0 else w * tap
            imp = tap if imp is None else imp + tap

        blk = lax.broadcasted_iota(jnp.int32, (nbk, qb), 0)
        forced = (blk == 0) | (blk == jt) | (blk == jt - 1)
        score = jnp.where(forced, picked, jnp.where(blk <= jt, imp, -1.0))
        blkf_ref[:nbk, :] = blk.astype(F32)

        def pick(_, score):
            mx = jnp.max(score, axis=0, keepdims=True)
            cand = jnp.where(score == mx, blkf_ref[:nbk, :], float(nbk))
            return jnp.where(cand == jnp.min(cand, axis=0, keepdims=True), picked, score)

        score = lax.fori_loop(0, min(SEL_TOPK, nbk) - N_FORCED, pick, score)
        sel_ref[:nbk, :] = jnp.where((score == picked) & (blk <= jt), 0.0, MASK_VALUE)

    bucket_ok = s_len % (NSA_BUCKETS * SEL_TILE) == 0 and nb // NSA_BUCKETS >= SEL_TOPK
    n_bucket = NSA_BUCKETS if bucket_ok else 1
    bucket = (start + qb - 1) // (s_len // n_bucket)
    for b in range(n_bucket):
        pl.when(bucket == b)(functools.partial(
            compressed_and_select, (b + 1) * (ncp // n_bucket), (b + 1) * (nb // n_bucket)))

    tk = SEL_TILE

    def scores(slot, kt):
        k0 = pl.multiple_of(kt * tk, tk)
        b8 = sel_ref[pl.ds(pl.multiple_of(kt * SEL_BPT, SEL_BPT), SEL_BPT), :]
        b16 = jnp.concatenate([b8, jnp.zeros((BF16_ROWS - SEL_BPT, qb), F32)], axis=0).astype(BF16)
        qa_ref[NSA_DIM:NSA_DIM + BF16_ROWS, :] = jnp.concatenate([b16] * NSA_HPG, axis=1)
        s = _dot(ks_ref[0, pl.ds(k0, tk), :], qa_ref[...])
        s_refs[slot][...] = s
        return jnp.max(s, axis=0, keepdims=True)

    def update(slot, kt, mt, m, causal):
        k0 = pl.multiple_of(kt * tk, tk)
        s = s_refs[slot][...]
        if causal:
            kpos = k0 + lax.broadcasted_iota(jnp.int32, (tk, qb), 0)
            s = s + jnp.concatenate([jnp.where(kpos <= t_row, 0.0, MASK_VALUE)] * NSA_HPG, axis=1)
            mt = jnp.max(s, axis=0, keepdims=True)
        m_new = jnp.maximum(m, mt)
        p = jnp.exp2(s - m_new).astype(BF16)
        acc_ref[...] = jnp.exp2(m - m_new) * acc_ref[...] + _dot(vst_ref[0, :, pl.ds(k0, tk)], p)
        return m_new

    def accumulate(pend):
        p, k0, alpha = pend
        pv = _dot(vst_ref[0, :, pl.ds(k0, SEL_CHUNK)], p)
        acc_ref[...] = (acc_ref[...] if alpha is None else alpha * acc_ref[...]) + pv

    def run(first, n_tiles, carry):
        mt, m = carry
        pend = None
        for k in range(n_tiles):
            cur, nxt = s_refs[k % 2], s_refs[(k + 1) % 2]
            k0c = pl.multiple_of((first + k) * tk, tk)
            k0n = pl.multiple_of((first + k + 1) * tk, tk)
            b8 = sel_ref[pl.ds(pl.multiple_of((first + k + 1) * SEL_BPT, SEL_BPT), SEL_BPT), :]
            b16 = jnp.concatenate([b8, jnp.zeros((BF16_ROWS - SEL_BPT, qb), F32)], axis=0).astype(BF16)
            qa_ref[NSA_DIM:NSA_DIM + BF16_ROWS, :] = jnp.concatenate([b16] * NSA_HPG, axis=1)
            m_new = jnp.maximum(m, mt)
            alpha = jnp.exp2(m - m_new)
            mt = None
            for c in range(tk // SEL_CHUNK):
                rows = pl.ds(c * SEL_CHUNK, SEL_CHUNK)
                s_n = _dot(ks_ref[0, pl.ds(k0n + c * SEL_CHUNK, SEL_CHUNK), :], qa_ref[...])
                nxt[rows, :] = s_n
                mt_c = jnp.max(s_n, axis=0, keepdims=True)
                mt = mt_c if mt is None else jnp.maximum(mt, mt_c)
                p = jnp.exp2(cur[rows, :] - m_new).astype(BF16)
                if pend is not None:
                    accumulate(pend)
                pend = (p, k0c + c * SEL_CHUNK, alpha if c == 0 else None)
            m = m_new
        accumulate(pend)
        return mt, m

    acc_ref[...] = jnp.zeros(acc_ref.shape, F32)
    last = (start + qb - 1) // tk
    carry = (scores(0, 0), jnp.full((1, hq), MASK_VALUE, F32))
    done = 0
    for unroll in SEL_UNROLLS:
        trips = (last - done) // unroll
        carry = lax.fori_loop(0, trips, lambda it, c, d=done, u=unroll: run(d + it * u, u, c), carry)
        done = done + trips * unroll
    mt_a, m = carry

    @pl.when(last % 2 == 0)
    def _():
        update(0, last, mt_a, m, True)

    @pl.when(last % 2 == 1)
    def _():
        mt_b = scores(1, last)
        update(1, last, mt_b, update(0, last - 1, mt_a, m, False), True)

    o_sel = acc_ref[:NSA_DIM, :] * (1.0 / acc_ref[NSA_DIM:NSA_DIM + 1, :])

    out_t = part_ref[...] + gate(1) * o_sel
    outs = [out_t[:, h * qb:(h + 1) * qb] for h in range(NSA_HPG)]
    o_ref[...] = jnp.concatenate(outs, axis=0).T.astype(o_ref.dtype)


def _nsa(q, gates_t, kc, vc_aug_t, ks_aug, vs_aug_t, kw, vw_aug_t):
    s = q.shape[0]
    assert s % (2 * SEL_TILE) == 0 and s >= WINDOW + QUERY_BLOCK
    gdim = NSA_HPG * NSA_DIM
    ncp = kc.shape[1]
    assert ncp * CMP_STRIDE == s and s // SEL_BLOCK * COV_RATIO == ncp
    nb = s // SEL_BLOCK
    ka = ks_aug.shape[-1]
    va = vs_aug_t.shape[1]
    hq = NSA_HPG * QUERY_BLOCK
    per_group = lambda shape: pl.BlockSpec((1,) + shape, lambda g, i: (g, 0, 0))
    rows = NSA_STEP_BLOCKS * QUERY_BLOCK
    assert s % rows == 0
    return pl.pallas_call(
        _nsa_kernel,
        grid=(NSA_GROUPS, s // rows),
        in_specs=[
            pl.BlockSpec((rows, gdim), lambda g, i: (i, g)),
            pl.BlockSpec((NSA_HEADS * 3, rows), lambda g, i: (0, i)),
            per_group((ncp, NSA_DIM)),
            per_group((va, ncp)),
            per_group((s, ka)),
            per_group((va, s)),
            per_group((s, NSA_DIM)),
            per_group((va, s)),
        ],
        out_specs=pl.BlockSpec((rows, gdim), lambda g, i: (i, g)),
        out_shape=jax.ShapeDtypeStruct((s, NSA_HEADS * NSA_DIM), BF16),
        scratch_shapes=[pltpu.VMEM((ka, hq), BF16), pltpu.VMEM((nb, QUERY_BLOCK), F32),
                        pltpu.VMEM((SEL_TILE, hq), F32), pltpu.VMEM((SEL_TILE, hq), F32),
                        pltpu.VMEM((va, hq), F32),
                        pltpu.VMEM((NSA_DIM, hq), F32),
                        pltpu.VMEM((nb, QUERY_BLOCK), F32),
                        pltpu.VMEM((ncp, QUERY_BLOCK), jnp.int32),
                        pltpu.VMEM((WINDOW + QUERY_BLOCK, QUERY_BLOCK), jnp.int32),
                        pltpu.VMEM((SUBLANES + ncp, QUERY_BLOCK), F32),
                        pltpu.VMEM((NSA_HEADS * 3, QUERY_BLOCK), F32)],
        compiler_params=_params("parallel", "arbitrary"),
        name="nsa",
    )(q, gates_t, kc, vc_aug_t, ks_aug, vs_aug_t, kw, vw_aug_t)


def _sgu_kernel(u_ref, v_ref, lng_ref, lnb_ref, ws_ref, bs_ref, o_ref):
    c = SGU_CHUNK
    tm = u_ref.shape[0]
    v = jax.nn.gelu(v_ref[...])
    mu = jnp.mean(v, axis=-1, keepdims=True)
    var = jnp.mean(jnp.square(v - mu), axis=-1, keepdims=True)
    vn = ((v - mu) * lax.rsqrt(var + NORM_EPS) * lng_ref[...] + lnb_ref[...]).astype(BF16)
    u = jax.nn.gelu(u_ref[...])
    tri = lax.broadcasted_iota(jnp.int32, (c, c), 0) >= lax.broadcasted_iota(jnp.int32, (c, c), 1)
    for g in range(SGU_GROUPS):
        w = jnp.where(tri, ws_ref[g], 0.0).astype(BF16)
        cols = slice(g * c, (g + 1) * c)
        rhs = jnp.concatenate([vn[k * c:(k + 1) * c, cols] for k in range(tm // c)], axis=1)
        mixed = _dot(w, rhs)
        for k in range(tm // c):
            rows = slice(k * c, (k + 1) * c)
            o_ref[rows, cols] = (u[rows, cols] * (mixed[:, rows] + bs_ref[g])).astype(o_ref.dtype)


def _sgu(proj, lng, lnb, ws, bs, tm):
    s = proj.shape[0]
    w = lng.shape[0]
    c = SGU_CHUNK
    bs_b = jnp.broadcast_to(bs[:, :, None], (SGU_GROUPS, c, c))
    return pl.pallas_call(
        _sgu_kernel,
        grid=(s // tm,),
        in_specs=[
            pl.BlockSpec((tm, w), lambda i: (i, 0)),
            pl.BlockSpec((tm, w), lambda i: (i, 1)),
            pl.BlockSpec((1, w), lambda i: (0, 0)),
            pl.BlockSpec((1, w), lambda i: (0, 0)),
            pl.BlockSpec((SGU_GROUPS, c, c), lambda i: (0, 0, 0)),
            pl.BlockSpec((SGU_GROUPS, c, c), lambda i: (0, 0, 0)),
        ],
        out_specs=pl.BlockSpec((tm, w), lambda i: (i, 0)),
        out_shape=jax.ShapeDtypeStruct((s, w), BF16),
        compiler_params=_params("parallel"),
        name="sgu",
    )(proj, proj, lng.reshape(1, w), lnb.reshape(1, w), ws, bs_b)


def _mix_kernel(oa_ref, ob_ref, ga_ref, gb_ref, x_ref, pa_ref, pb_ref, wo_ref,
                gc_ref, wq_ref, mkt_ref, mv_ref, wmo_ref, o_ref):
    a = _dot(oa_ref[...], pa_ref[...])
    b = _dot(ob_ref[...], pb_ref[...])
    merged = jax.nn.sigmoid(ga_ref[...]) * a + jax.nn.sigmoid(gb_ref[...]) * b
    x = x_ref[...] + _dot(merged.astype(BF16), wo_ref[...])
    h = _rms(x, gc_ref[...]).astype(BF16)
    mq = (_dot(h, wq_ref[...]) * (MEM_DIM ** -0.5)).astype(BF16)
    outs = []
    for hh in range(MEM_HEADS):
        s = _dot(mq[:, hh * MEM_DIM:(hh + 1) * MEM_DIM], mkt_ref[hh])
        e = jnp.exp(s - jnp.max(s, axis=-1, keepdims=True))
        p = e / jnp.sum(e, axis=-1, keepdims=True)
        outs.append(_dot(p.astype(BF16), mv_ref[hh]).astype(BF16))
    o_ref[...] = x + _dot(jnp.concatenate(outs, axis=1), wmo_ref[...])


def _mix(o_a, o_b, proj, x, p_a, p_b, w_o, gate_block, g_cross, w_q, mk_t, mv, w_mo, tm):
    s, d = x.shape
    wa = o_a.shape[1]
    wb = o_b.shape[1]
    mw = w_q.shape[1]
    m = mv.shape[1]
    resident = lambda shape: pl.BlockSpec(shape, lambda i: (0,) * len(shape), pipeline_mode=pl.Buffered(1))
    return pl.pallas_call(
        _mix_kernel,
        grid=(s // tm,),
        in_specs=[
            pl.BlockSpec((tm, wa), lambda i: (i, 0)),
            pl.BlockSpec((tm, wb), lambda i: (i, 0)),
            pl.BlockSpec((tm, d), lambda i: (i, gate_block)),
            pl.BlockSpec((tm, d), lambda i: (i, gate_block + 1)),
            pl.BlockSpec((tm, d), lambda i: (i, 0)),
            resident((wa, d)),
            resident((wb, d)),
            resident((d, d)),
            resident((1, d)),
            resident((d, mw)),
            resident((MEM_HEADS, MEM_DIM, m)),
            resident((MEM_HEADS, m, MEM_DIM)),
            resident((mw, d)),
        ],
        out_specs=pl.BlockSpec((tm, d), lambda i: (i, 0)),
        out_shape=jax.ShapeDtypeStruct((s, d), F32),
        compiler_params=_params("parallel"),
        name="mix",
    )(o_a, o_b, proj, proj, x, p_a, p_b, w_o, g_cross.reshape(1, d), w_q, mk_t, mv, w_mo)


def _ffn_kernel(x_ref, g_ref, wg_ref, wu_ref, wo_ref, gf_ref, o_ref, h_ref, acc_ref, *, final_norm):
    j = pl.program_id(1)

    @pl.when(j == 0)
    def _():
        h_ref[...] = _rms(x_ref[...], g_ref[...]).astype(BF16)
        acc_ref[...] = jnp.zeros_like(acc_ref)

    h = h_ref[...]
    th = wg_ref.shape[1]
    out = None
    for c in range(th // FFN_CHUNK):
        cs = slice(c * FFN_CHUNK, (c + 1) * FFN_CHUNK)
        act = (jax.nn.silu(_dot(h, wg_ref[:, cs])) * _dot(h, wu_ref[:, cs])).astype(BF16)
        part = _dot(act, wo_ref[cs, :])
        out = part if out is None else out + part
    acc_ref[...] += out

    @pl.when(j == pl.num_programs(1) - 1)
    def _():
        y = x_ref[...] + acc_ref[...]
        o_ref[...] = _rms(y, gf_ref[...]) if final_norm else y


def _ffn(x, g, w_in, w_out, g_final, final_norm, tm, th):
    s, d = x.shape
    hidden = w_out.shape[0]
    nh = hidden // th
    return pl.pallas_call(
        functools.partial(_ffn_kernel, final_norm=final_norm),
        grid=(s // tm, nh),
        in_specs=[
            pl.BlockSpec((tm, d), lambda i, j: (i, 0)),
            pl.BlockSpec((1, d), lambda i, j: (0, 0)),
            pl.BlockSpec((d, th), lambda i, j: (0, j)),
            pl.BlockSpec((d, th), lambda i, j: (0, j + nh)),
            pl.BlockSpec((th, d), lambda i, j: (j, 0)),
            pl.BlockSpec((1, d), lambda i, j: (0, 0)),
        ],
        out_specs=pl.BlockSpec((tm, d), lambda i, j: (i, 0)),
        out_shape=jax.ShapeDtypeStruct((s, d), F32),
        scratch_shapes=[pltpu.VMEM((tm, d), BF16), pltpu.VMEM((tm, d), F32)],
        compiler_params=_params("parallel", "arbitrary"),
        name="ffn",
    )(x, g.reshape(1, d), w_in, w_in, w_out, g_final.reshape(1, d))


def _layer(x, mem, norm_mix_g, w_in, cmp_pe_k, cmp_k_w1, cmp_k_b1, cmp_k_w2, cmp_pe_v, cmp_v_w1, cmp_v_b1, cmp_v_w2,
           sgu_ln_g, sgu_ln_b, sgu_ws, sgu_b, w_proj_a, w_proj_b, w_mix_out, norm_cross_g, norm_mem_g,
           w_mq, w_mkv, w_mo, norm_ffn_g):
    s, d = x.shape
    qw = NSA_HEADS * NSA_DIM
    kvw = NSA_GROUPS * NSA_DIM
    sguw = sgu_ln_g.shape[0]
    ngate = NSA_HEADS * 3
    o_kv = qw
    o_gate = o_kv + N_KV_STREAMS * kvw
    o_u = o_gate + ngate

    w_a = w_in[:, :o_gate].astype(BF16)
    w_b = jnp.pad(w_in[:, o_u:].astype(BF16), ((0, 0), (0, PROJ_TN)))
    w_b = lax.dynamic_update_slice(w_b, w_in[:, o_gate:o_u].astype(BF16), (0, w_in.shape[1] - o_u))
    tm_proj = min(1024, s)
    nat, kw, ks_aug, vs_aug_t, vw_aug_t, q, proj_b, gates_t = _proj(x, norm_mix_g, w_a, w_b, tm_proj)

    ncp = s // CMP_STRIDE
    w1 = jnp.stack([cmp_k_w1, cmp_v_w1]).astype(BF16)
    pe = jnp.stack([cmp_pe_k.reshape(1, -1), cmp_pe_v.reshape(1, -1)])
    b1 = jnp.stack([cmp_k_b1.reshape(1, -1), cmp_v_b1.reshape(1, -1)])
    w2 = jnp.stack([cmp_k_w2, cmp_v_w2]).astype(BF16)
    cmp, cmp_t = _compress(nat, w1, pe, b1, w2)
    o_a = _nsa(q, gates_t, cmp[0], cmp_t[1], ks_aug, vs_aug_t, kw, vw_aug_t)

    o_b = _sgu(proj_b, sgu_ln_g, sgu_ln_b, sgu_ws, sgu_b, min(512, s))

    m = mem.shape[0]
    mw = MEM_HEADS * MEM_DIM
    mkv = _norm_matmul(mem, norm_mem_g, w_mkv.astype(BF16), F32, m, mw)
    mk_t = mkv[:, :mw].reshape(m, MEM_HEADS, MEM_DIM).transpose(1, 2, 0).astype(BF16)
    mv = mkv[:, mw:].reshape(m, MEM_HEADS, MEM_DIM).transpose(1, 0, 2).astype(BF16)
    return _mix(o_a, o_b, proj_b, x, w_proj_a.astype(BF16), w_proj_b.astype(BF16), w_mix_out.astype(BF16),
                (2 * sguw) // d, norm_cross_g, w_mq.astype(BF16), mk_t, mv, w_mo.astype(BF16), min(256, s))


def kernel(x, mem, norm_mix_g, w_in, cmp_pe_k, cmp_k_w1, cmp_k_b1, cmp_k_w2, cmp_pe_v, cmp_v_w1, cmp_v_b1, cmp_v_w2, sgu_ln_g, sgu_ln_b, sgu_ws, sgu_b, w_proj_a, w_proj_b, w_mix_out, norm_cross_g, norm_mem_g, w_mq, w_mkv, w_mo, norm_ffn_g, w_ffn_in, w_ffn_out, norm_final_g):
    b, s, d = x.shape
    depth = w_in.shape[0]
    outs = []
    for bi in range(b):
        xb = x[bi]
        for l in range(depth):
            last = l == depth - 1
            xb = _layer(xb, mem[bi], norm_mix_g[l], w_in[l], cmp_pe_k[l], cmp_k_w1[l], cmp_k_b1[l], cmp_k_w2[l],
                        cmp_pe_v[l], cmp_v_w1[l], cmp_v_b1[l], cmp_v_w2[l], sgu_ln_g[l], sgu_ln_b[l], sgu_ws[l],
                        sgu_b[l], w_proj_a[l], w_proj_b[l], w_mix_out[l], norm_cross_g[l], norm_mem_g[l],
                        w_mq[l], w_mkv[l], w_mo[l], norm_ffn_g[l])
            xb = _ffn(xb, norm_ffn_g[l], w_ffn_in[l].astype(BF16), w_ffn_out[l].astype(BF16), norm_final_g, last,
                      min(512, s), 512)
        outs.append(xb)
    return jnp.stack(outs)
```

```python
import functools
import math

import jax
import jax.numpy as jnp
from jax import lax
from jax.experimental import pallas as pl
from jax.experimental.pallas import tpu as pltpu

F32 = jnp.float32
BF16 = jnp.bfloat16

NORM_EPS = 1e-6
MASK_VALUE = -1e30
N_FORCED = 3
N_BRANCH = 3
LOG2E = math.log2(math.e)

NSA_HEADS = 16
NSA_GROUPS = 4
NSA_HPG = NSA_HEADS // NSA_GROUPS
NSA_DIM = 64
GATE_ROWS = NSA_HEADS * N_BRANCH
N_KV_STREAMS = 6
CMP_BLOCK = 32
CMP_STRIDE = 16
SEL_BLOCK = 64
SEL_TOPK = 16
WINDOW = 512
QUERY_BLOCK = 128
SGU_GROUPS = 8
SGU_CHUNK = 128
MEM_HEADS = 4
MEM_DIM = 128

LANES = 128
SUBLANES = 8
COV_RATIO = SEL_BLOCK // CMP_STRIDE
COV_LEAD = CMP_BLOCK // CMP_STRIDE - 1
COV_BAND = tuple(
    max(min((k - COV_LEAD) * CMP_STRIDE + CMP_BLOCK, SEL_BLOCK) - max((k - COV_LEAD) * CMP_STRIDE, 0), 0) / CMP_BLOCK
    for k in range(COV_RATIO + COV_LEAD))
assert SEL_BLOCK % CMP_STRIDE == 0 and CMP_BLOCK % CMP_STRIDE == 0 and COV_LEAD <= SUBLANES
SEL_TILE = 512
SEL_BPT = SEL_TILE // SEL_BLOCK
SEL_CHUNK = 256
SEL_UNROLLS = (16, 8, 4, 2)
NSA_STEP_BLOCKS = 4
NSA_BUCKETS = 8
BF16_ROWS = 16
FFN_CHUNK = 256
VMEM_LIMIT = 62 * 1024 * 1024


def _params(*sem):
    return pltpu.CompilerParams(dimension_semantics=sem, vmem_limit_bytes=VMEM_LIMIT)


def _rms(x, g):
    return x * lax.rsqrt(jnp.mean(x * x, axis=-1, keepdims=True) + NORM_EPS) * g


def _dot(a, b):
    return jnp.dot(a, b, preferred_element_type=F32)


def _ones_row(n):
    return jnp.where(lax.broadcasted_iota(jnp.int32, (BF16_ROWS, n), 0) == 0, 1.0, 0.0).astype(BF16)


def _norm_matmul_kernel(x_ref, g_ref, w_ref, o_ref, h_ref):
    @pl.when(pl.program_id(1) == 0)
    def _():
        h_ref[...] = _rms(x_ref[...], g_ref[...]).astype(BF16)

    o_ref[...] = _dot(h_ref[...], w_ref[...]).astype(o_ref.dtype)


def _norm_matmul(x, g, w, out_dtype, tm, tn):
    s, d = x.shape
    n = w.shape[1]
    return pl.pallas_call(
        _norm_matmul_kernel,
        grid=(s // tm, n // tn),
        in_specs=[
            pl.BlockSpec((tm, d), lambda i, j: (i, 0)),
            pl.BlockSpec((1, d), lambda i, j: (0, 0)),
            pl.BlockSpec((d, tn), lambda i, j: (0, j)),
        ],
        out_specs=pl.BlockSpec((tm, tn), lambda i, j: (i, j)),
        out_shape=jax.ShapeDtypeStruct((s, n), out_dtype),
        scratch_shapes=[pltpu.VMEM((tm, d), BF16)],
        compiler_params=_params("parallel", "arbitrary"),
        name="norm_matmul",
    )(x, g.reshape(1, d), w)


PROJ_TN = 512


def _proj_kernel(x_ref, g_ref, wa_ref, wb_ref, nat_ref, kw_ref, ksa_ref, vsa_ref, vwa_ref, q_ref, o_ref, gt_ref,
                 h_ref, stage_ref):
    i = pl.program_id(0)
    j = pl.program_id(1)
    tm = x_ref.shape[0]
    dh = NSA_DIM
    ng = NSA_GROUPS
    tn = PROJ_TN
    qw = q_ref.shape[1]

    @pl.when(j == 0)
    def _():
        h_ref[...] = _rms(x_ref[...], g_ref[...]).astype(BF16)
        h = h_ref[...]

        def tile(t):
            return _dot(h, wa_ref[:, qw + t * tn:qw + (t + 1) * tn])

        def cols(res, c):
            return res[:, c * dh:(c + 1) * dh].astype(BF16)

        def cols_t(res, c):
            slab = res[:, (c // 2) * 2 * dh:(c // 2 + 1) * 2 * dh].T
            return slab[(c % 2) * dh:(c % 2 + 1) * dh].astype(BF16)

        res = tile(0)
        for c in range(2 * ng):
            stage_ref[...] = res[:, c * dh:(c + 1) * dh]
            for t in range(0, CMP_STRIDE, 2):
                pair = [stage_ref[pl.ds(t + u, tm // CMP_STRIDE, stride=CMP_STRIDE), :] for u in range(2)]
                nat_ref[c // ng, c % ng, :, t * dh:(t + 2) * dh] = jnp.concatenate(pair, axis=1).astype(BF16)

        res = tile(1)
        pos = i * tm + lax.broadcasted_iota(jnp.int32, (tm, dh), 0)
        lane = lax.broadcasted_iota(jnp.int32, (tm, dh), 1)
        onehot = jnp.where((pos // SEL_BLOCK) % SEL_BPT == lane, 1.0, 0.0).astype(BF16)
        for g in range(ng):
            ksa_ref[g, :, :dh] = cols(res, g)
            ksa_ref[g, :, dh:] = onehot
            vsa_ref[g, :dh, :] = cols_t(res, ng + g)
            vsa_ref[g, dh:, :] = _ones_row(tm)

        res = tile(2)
        for g in range(ng):
            kw_ref[g] = cols(res, g)
            vwa_ref[g, :dh, :] = cols_t(res, ng + g)
            vwa_ref[g, dh:, :] = _ones_row(tm)

        for t in range(qw // tn):
            q_ref[:, t * tn:(t + 1) * tn] = _dot(h, wa_ref[:, t * tn:(t + 1) * tn]).astype(BF16)

    @pl.when(j >= 1)
    def _():
        res = _dot(h_ref[...], wb_ref[...])
        o_ref[...] = res

        @pl.when(j == 1)
        def _():
            gt_ref[...] = res[:, :LANES].T


def _proj(x, g, w_a, w_b, tm):
    s, d = x.shape
    dh, ng, tn = NSA_DIM, NSA_GROUPS, PROJ_TN
    qw = NSA_HEADS * dh
    assert 2 * ng * dh == tn and qw % tn == 0 and w_a.shape[1] == qw + N_KV_STREAMS // 2 * tn
    n = w_b.shape[1] - tn
    nb = n // tn
    shapes = [
        jax.ShapeDtypeStruct((2, ng, s // CMP_STRIDE, CMP_STRIDE * dh), BF16),
        jax.ShapeDtypeStruct((ng, s, dh), BF16),
        jax.ShapeDtypeStruct((ng, s, 2 * dh), BF16),
        jax.ShapeDtypeStruct((ng, dh + BF16_ROWS, s), BF16),
        jax.ShapeDtypeStruct((ng, dh + BF16_ROWS, s), BF16),
        jax.ShapeDtypeStruct((s, NSA_HEADS * dh), BF16),
        jax.ShapeDtypeStruct((s, n), F32),
        jax.ShapeDtypeStruct((LANES, s), F32),
    ]
    return pl.pallas_call(
        _proj_kernel,
        grid=(s // tm, 2 + nb),
        in_specs=[
            pl.BlockSpec((tm, d), lambda i, j: (i, 0)),
            pl.BlockSpec((1, d), lambda i, j: (0, 0)),
            pl.BlockSpec(w_a.shape, lambda i, j: (0, 0), pipeline_mode=pl.Buffered(1)),
            pl.BlockSpec((d, tn), lambda i, j: (0, jnp.where(j <= 1, nb, j - 2))),
        ],
        out_specs=[
            pl.BlockSpec((2, ng, tm // CMP_STRIDE, CMP_STRIDE * dh), lambda i, j: (0, 0, i, 0)),
            pl.BlockSpec((ng, tm, dh), lambda i, j: (0, i, 0)),
            pl.BlockSpec((ng, tm, 2 * dh), lambda i, j: (0, i, 0)),
            pl.BlockSpec((ng, dh + BF16_ROWS, tm), lambda i, j: (0, 0, i)),
            pl.BlockSpec((ng, dh + BF16_ROWS, tm), lambda i, j: (0, 0, i)),
            pl.BlockSpec((tm, qw), lambda i, j: (i, 0)),
            pl.BlockSpec((tm, tn), lambda i, j: (i, jnp.maximum(j - 2, 0))),
            pl.BlockSpec((LANES, tm), lambda i, j: (0, i)),
        ],
        out_shape=shapes,
        scratch_shapes=[pltpu.VMEM((tm, d), BF16), pltpu.VMEM((tm, dh), F32)],
        compiler_params=_params("parallel", "arbitrary"),
        name="proj",
    )(x, g.reshape(1, d), w_a, w_b)


def _compress_kernel(r_ref, w1_ref, pe_ref, b1_ref, w2_ref, o_ref, ot_ref):
    half = r_ref.shape[-1]
    r = r_ref[0, 0]
    bias = _dot(pe_ref[0].astype(BF16), w1_ref[0]) + b1_ref[0]
    top = _dot(r, w1_ref[0, :half, :])
    bot = _dot(r, w1_ref[0, half:, :])
    ncp = r.shape[0]
    hid = top + pltpu.roll(bot, ncp - 1, 0) + bias
    out = _dot(jax.nn.gelu(hid).astype(BF16), w2_ref[0])
    o_ref[0, 0] = out.astype(o_ref.dtype)
    dh = out.shape[1]
    out_t = jnp.concatenate([out, jnp.zeros((ncp, LANES - dh), F32)], axis=1).T
    ot_ref[0, 0, :dh, :] = out_t[:dh].astype(ot_ref.dtype)
    ot_ref[0, 0, dh:, :] = _ones_row(ncp)


def _compress(r, w1, pe, b1, w2):
    _, g, ncp, half = r.shape
    hidden = w1.shape[-1]
    dh = w2.shape[-1]
    return pl.pallas_call(
        _compress_kernel,
        grid=(2, g),
        in_specs=[
            pl.BlockSpec((1, 1, ncp, half), lambda s, gg: (s, gg, 0, 0)),
            pl.BlockSpec((1, 2 * half, hidden), lambda s, gg: (s, 0, 0)),
            pl.BlockSpec((1, 1, 2 * half), lambda s, gg: (s, 0, 0)),
            pl.BlockSpec((1, 1, hidden), lambda s, gg: (s, 0, 0)),
            pl.BlockSpec((1, hidden, dh), lambda s, gg: (s, 0, 0)),
        ],
        out_specs=[pl.BlockSpec((1, 1, ncp, dh), lambda s, gg: (s, gg, 0, 0)),
                   pl.BlockSpec((1, 1, dh + BF16_ROWS, ncp), lambda s, gg: (s, gg, 0, 0))],
        out_shape=[jax.ShapeDtypeStruct((2, g, ncp, dh), BF16),
                   jax.ShapeDtypeStruct((2, g, dh + BF16_ROWS, ncp), BF16)],
        compiler_params=_params("parallel", "parallel"),
        name="compress",
    )(r, w1, pe, b1, w2)


PICKED = -2.0


def _nsa_kernel(q_ref, gt_ref, kc_ref, vct_ref, ks_ref, vst_ref, kw_ref, vwt_ref, o_ref,
                qa_ref, sel_ref, part_ref, gates_ref, s0_ref, s1_ref, acc_ref, blkf_ref, cend_ref, wrel_ref, pp_ref):
    g = pl.program_id(0)
    step = pl.program_id(1)
    qb = QUERY_BLOCK
    nb = sel_ref.shape[1]
    ncp = kc_ref.shape[1]
    s_len = ks_ref.shape[1]
    first = step * NSA_STEP_BLOCKS

    @pl.when(step == 0)
    def _():
        lane = lax.broadcasted_iota(jnp.int32, cend_ref.shape, 1)
        cend_ref[...] = lax.broadcasted_iota(jnp.int32, cend_ref.shape, 0) * CMP_STRIDE + (CMP_BLOCK - 1) - lane
        wrel_ref[...] = (lax.broadcasted_iota(jnp.int32, wrel_ref.shape, 0)
                         - lax.broadcasted_iota(jnp.int32, wrel_ref.shape, 1))

    def rows(sub):
        return pl.ds(pl.multiple_of(sub * qb, qb), qb)

    bucket_ok = s_len % (NSA_BUCKETS * SEL_TILE) == 0 and nb // NSA_BUCKETS >= SEL_TOPK
    n_bucket = NSA_BUCKETS if bucket_ok else 1
    bucket = ((first + NSA_STEP_BLOCKS) * qb - 1) // (s_len // n_bucket)
    for b in range(n_bucket):
        nck, nbk = (b + 1) * (ncp // n_bucket), (b + 1) * (nb // n_bucket)

        @pl.when(bucket == b)
        def _():
            def scores(sub, carry):
                _nsa_scores(g, first + sub, q_ref.at[rows(sub), :], gt_ref.at[:, rows(sub)], kc_ref, vct_ref, kw_ref,
                            vwt_ref, qa_ref.at[sub], sel_ref.at[sub], part_ref.at[sub], gates_ref.at[sub], pp_ref,
                            cend_ref, wrel_ref, nck, nbk)
                return carry

            lax.fori_loop(0, NSA_STEP_BLOCKS, scores, 0)
            _nsa_topk(first, sel_ref, blkf_ref, nbk)

    def select(sub, carry):
        _nsa_select(g, first + sub, ks_ref, vst_ref, o_ref.at[rows(sub), :], qa_ref.at[sub], sel_ref.at[sub],
                    part_ref.at[sub], gates_ref.at[sub], s0_ref, s1_ref, acc_ref)
        return carry

    lax.fori_loop(0, NSA_STEP_BLOCKS, select, 0)


def _gate(gates_ref, g, branch):
    rows = [gates_ref[pl.ds((g * NSA_HPG + h) * N_BRANCH + branch, 1), :] for h in range(NSA_HPG)]
    return jax.nn.sigmoid(jnp.concatenate(rows, axis=1))


def _nsa_scores(g, i, q_ref, gt_ref, kc_ref, vct_ref, kw_ref, vwt_ref, qa_ref, score_ref, part_ref, gates_ref,
                pp_ref, cend_ref, wrel_ref, nck, nbk):
    qb = QUERY_BLOCK
    start = i * qb
    hq = NSA_HPG * qb
    t_row = start + lax.broadcasted_iota(jnp.int32, (1, qb), 1)
    jt = t_row // SEL_BLOCK

    qt = (q_ref[...].astype(F32) * (NSA_DIM ** -0.5 * LOG2E)).T
    q_t = jnp.concatenate([qt[h * NSA_DIM:(h + 1) * NSA_DIM] for h in range(NSA_HPG)], axis=1).astype(BF16)
    qa_ref[:NSA_DIM, :] = q_t
    qa_ref[NSA_DIM:, :] = jnp.zeros((qa_ref.shape[0] - NSA_DIM, hq), BF16)
    gates_ref[...] = gt_ref[...]

    def exps(s, bias):
        out = []
        for h in range(NSA_HPG):
            sh = s[:, h * qb:(h + 1) * qb] + bias
            out.append(jnp.exp2(sh - jnp.max(sh, axis=0, keepdims=True)))
        return out

    wk = WINDOW + qb
    k0w = pl.multiple_of(jnp.maximum(start - WINDOW, 0), qb)
    sc = _dot(kc_ref[0, :nck, :], q_t)
    sw = _dot(kw_ref[0, pl.ds(k0w, wk), :], q_t)
    e_cmp = exps(sc, jnp.where(cend_ref[:nck, :] <= start, 0.0, MASK_VALUE))
    o_cmp = _dot(vct_ref[0, :, :nck], jnp.concatenate([e.astype(BF16) for e in e_cmp], axis=1))
    wrel = wrel_ref[...]
    e_win = exps(sw, jnp.where((wrel <= start - k0w) & (wrel > start - k0w - WINDOW), 0.0, MASK_VALUE))
    o_win = _dot(vwt_ref[0, :, pl.ds(k0w, wk)], jnp.concatenate([e.astype(BF16) for e in e_win], axis=1))
    inv_cmp = jnp.where(jnp.concatenate([t_row >= CMP_BLOCK - 1] * NSA_HPG, axis=1),
                        1.0 / o_cmp[NSA_DIM:NSA_DIM + 1, :], 0.0)
    psum = e_cmp[0] * inv_cmp[:, :qb]
    for h in range(1, NSA_HPG):
        psum = psum + e_cmp[h] * inv_cmp[:, h * qb:(h + 1) * qb]
    part_ref[...] = (_gate(gates_ref, g, 0) * inv_cmp * o_cmp[:NSA_DIM]
                     + _gate(gates_ref, g, 2) * (1.0 / o_win[NSA_DIM:NSA_DIM + 1, :]) * o_win[:NSA_DIM])
    pp_ref[:SUBLANES, :] = jnp.zeros((SUBLANES, qb), F32)
    pp_ref[SUBLANES:SUBLANES + nck, :] = psum
    imp = None
    for k, w in enumerate(COV_BAND):
        tap = pp_ref[pl.ds(SUBLANES - COV_LEAD + k, nbk, stride=COV_RATIO), :]
        tap = tap if w == 1.0 else w * tap
        imp = tap if imp is None else imp + tap
    blk = lax.broadcasted_iota(jnp.int32, (nbk, qb), 0)
    forced = (blk == 0) | (blk == jt) | (blk == jt - 1)
    score_ref[:nbk, :] = jnp.where(forced, PICKED, jnp.where(blk <= jt, imp, -1.0))


def _nsa_topk(first, sel_ref, blkf_ref, nbk):
    qb = QUERY_BLOCK
    blk = lax.broadcasted_iota(jnp.int32, (nbk, qb), 0)
    blkf_ref[:nbk, :] = blk.astype(F32)

    def pick(_, scores):
        out = []
        for score in scores:
            mx = jnp.max(score, axis=0, keepdims=True)
            cand = jnp.where(score == mx, blkf_ref[:nbk, :], float(nbk))
            out.append(jnp.where(cand == jnp.min(cand, axis=0, keepdims=True), PICKED, score))
        return tuple(out)

    scores = tuple(sel_ref[b, :nbk, :] for b in range(sel_ref.shape[0]))
    scores = lax.fori_loop(0, min(SEL_TOPK, nbk) - N_FORCED, pick, scores)
    for b, score in enumerate(scores):
        jt = ((first + b) * qb + lax.broadcasted_iota(jnp.int32, (1, qb), 1)) // SEL_BLOCK
        sel_ref[b, :nbk, :] = jnp.where((score == PICKED) & (blk <= jt), 0.0, MASK_VALUE)


def _nsa_select(g, i, ks_ref, vst_ref, o_ref, qa_ref, sel_ref, part_ref, gates_ref, s0_ref, s1_ref, acc_ref):
    s_refs = (s0_ref, s1_ref)
    qb = QUERY_BLOCK
    start = i * qb
    hq = NSA_HPG * qb
    t_row = start + lax.broadcasted_iota(jnp.int32, (1, qb), 1)
    tk = SEL_TILE

    def scores(slot, kt):
        k0 = pl.multiple_of(kt * tk, tk)
        b8 = sel_ref[pl.ds(pl.multiple_of(kt * SEL_BPT, SEL_BPT), SEL_BPT), :]
        b16 = jnp.concatenate([b8, jnp.zeros((BF16_ROWS - SEL_BPT, qb), F32)], axis=0).astype(BF16)
        qa_ref[NSA_DIM:NSA_DIM + BF16_ROWS, :] = jnp.concatenate([b16] * NSA_HPG, axis=1)
        s = _dot(ks_ref[0, pl.ds(k0, tk), :], qa_ref[...])
        s_refs[slot][...] = s
        return jnp.max(s, axis=0, keepdims=True)

    def update(slot, kt, mt, m, causal):
        k0 = pl.multiple_of(kt * tk, tk)
        s = s_refs[slot][...]
        if causal:
            kpos = k0 + lax.broadcasted_iota(jnp.int32, (tk, qb), 0)
            s = s + jnp.concatenate([jnp.where(kpos <= t_row, 0.0, MASK_VALUE)] * NSA_HPG, axis=1)
            mt = jnp.max(s, axis=0, keepdims=True)
        m_new = jnp.maximum(m, mt)
        p = jnp.exp2(s - m_new).astype(BF16)
        acc_ref[...] = jnp.exp2(m - m_new) * acc_ref[...] + _dot(vst_ref[0, :, pl.ds(k0, tk)], p)
        return m_new

    def accumulate(pend):
        p, k0, alpha = pend
        pv = _dot(vst_ref[0, :, pl.ds(k0, SEL_CHUNK)], p)
        acc_ref[...] = (acc_ref[...] if alpha is None else alpha * acc_ref[...]) + pv

    def run(first, n_tiles, carry):
        mt, m = carry
        pend = None
        for k in range(n_tiles):
            cur, nxt = s_refs[k % 2], s_refs[(k + 1) % 2]
            k0c = pl.multiple_of((first + k) * tk, tk)
            k0n = pl.multiple_of((first + k + 1) * tk, tk)
            b8 = sel_ref[pl.ds(pl.multiple_of((first + k + 1) * SEL_BPT, SEL_BPT), SEL_BPT), :]
            b16 = jnp.concatenate([b8, jnp.zeros((BF16_ROWS - SEL_BPT, qb), F32)], axis=0).astype(BF16)
            qa_ref[NSA_DIM:NSA_DIM + BF16_ROWS, :] = jnp.concatenate([b16] * NSA_HPG, axis=1)
            m_new = jnp.maximum(m, mt)
            alpha = jnp.exp2(m - m_new)
            mt = None
            for c in range(tk // SEL_CHUNK):
                rows = pl.ds(c * SEL_CHUNK, SEL_CHUNK)
                s_n = _dot(ks_ref[0, pl.ds(k0n + c * SEL_CHUNK, SEL_CHUNK), :], qa_ref[...])
                nxt[rows, :] = s_n
                mt_c = jnp.max(s_n, axis=0, keepdims=True)
                mt = mt_c if mt is None else jnp.maximum(mt, mt_c)
                p = jnp.exp2(cur[rows, :] - m_new).astype(BF16)
                if pend is not None:
                    accumulate(pend)
                pend = (p, k0c + c * SEL_CHUNK, alpha if c == 0 else None)
            m = m_new
        accumulate(pend)
        return mt, m

    acc_ref[...] = jnp.zeros(acc_ref.shape, F32)
    last = (start + qb - 1) // tk
    carry = (scores(0, 0), jnp.full((1, hq), MASK_VALUE, F32))
    done = 0
    for unroll in SEL_UNROLLS:
        trips = (last - done) // unroll
        carry = lax.fori_loop(0, trips, lambda it, c, d=done, u=unroll: run(d + it * u, u, c), carry)
        done = done + trips * unroll
    mt_a, m = carry

    @pl.when(last % 2 == 0)
    def _():
        update(0, last, mt_a, m, True)

    @pl.when(last % 2 == 1)
    def _():
        mt_b = scores(1, last)
        update(1, last, mt_b, update(0, last - 1, mt_a, m, False), True)

    o_sel = acc_ref[:NSA_DIM, :] * (1.0 / acc_ref[NSA_DIM:NSA_DIM + 1, :])

    out_t = part_ref[...] + _gate(gates_ref, g, 1) * o_sel
    outs = [out_t[:, h * qb:(h + 1) * qb] for h in range(NSA_HPG)]
    o_ref[...] = jnp.concatenate(outs, axis=0).T.astype(o_ref.dtype)


def _nsa(q, gates_t, kc, vc_aug_t, ks_aug, vs_aug_t, kw, vw_aug_t):
    s = q.shape[0]
    assert s % (2 * SEL_TILE) == 0 and s >= WINDOW + QUERY_BLOCK
    gdim = NSA_HPG * NSA_DIM
    ncp = kc.shape[1]
    assert ncp * CMP_STRIDE == s and s // SEL_BLOCK * COV_RATIO == ncp
    nb = s // SEL_BLOCK
    ka = ks_aug.shape[-1]
    va = vs_aug_t.shape[1]
    hq = NSA_HPG * QUERY_BLOCK
    per_group = lambda shape: pl.BlockSpec((1,) + shape, lambda g, i: (g, 0, 0))
    rows = NSA_STEP_BLOCKS * QUERY_BLOCK
    assert s % rows == 0
    return pl.pallas_call(
        _nsa_kernel,
        grid=(NSA_GROUPS, s // rows),
        in_specs=[
            pl.BlockSpec((rows, gdim), lambda g, i: (i, g)),
            pl.BlockSpec((GATE_ROWS, rows), lambda g, i: (0, i)),
            per_group((ncp, NSA_DIM)),
            per_group((va, ncp)),
            per_group((s, ka)),
            per_group((va, s)),
            per_group((s, NSA_DIM)),
            per_group((va, s)),
        ],
        out_specs=pl.BlockSpec((rows, gdim), lambda g, i: (i, g)),
        out_shape=jax.ShapeDtypeStruct((s, NSA_HEADS * NSA_DIM), BF16),
        scratch_shapes=[pltpu.VMEM((NSA_STEP_BLOCKS, ka, hq), BF16),
                        pltpu.VMEM((NSA_STEP_BLOCKS, nb, QUERY_BLOCK), F32),
                        pltpu.VMEM((NSA_STEP_BLOCKS, NSA_DIM, hq), F32),
                        pltpu.VMEM((NSA_STEP_BLOCKS, GATE_ROWS, QUERY_BLOCK), F32),
                        pltpu.VMEM((SEL_TILE, hq), F32), pltpu.VMEM((SEL_TILE, hq), F32),
                        pltpu.VMEM((va, hq), F32),
                        pltpu.VMEM((nb, QUERY_BLOCK), F32),
                        pltpu.VMEM((ncp, QUERY_BLOCK), jnp.int32),
                        pltpu.VMEM((WINDOW + QUERY_BLOCK, QUERY_BLOCK), jnp.int32),
                        pltpu.VMEM((SUBLANES + ncp, QUERY_BLOCK), F32)],
        compiler_params=_params("parallel", "arbitrary"),
        name="nsa",
    )(q, gates_t, kc, vc_aug_t, ks_aug, vs_aug_t, kw, vw_aug_t)


def _sgu_kernel(u_ref, v_ref, lng_ref, lnb_ref, ws_ref, bs_ref, o_ref):
    c = SGU_CHUNK
    tm = u_ref.shape[0]
    v = jax.nn.gelu(v_ref[...])
    mu = jnp.mean(v, axis=-1, keepdims=True)
    var = jnp.mean(jnp.square(v - mu), axis=-1, keepdims=True)
    vn = ((v - mu) * lax.rsqrt(var + NORM_EPS) * lng_ref[...] + lnb_ref[...]).astype(BF16)
    u = jax.nn.gelu(u_ref[...])
    tri = lax.broadcasted_iota(jnp.int32, (c, c), 0) >= lax.broadcasted_iota(jnp.int32, (c, c), 1)
    for g in range(SGU_GROUPS):
        w = jnp.where(tri, ws_ref[g], 0.0).astype(BF16)
        cols = slice(g * c, (g + 1) * c)
        rhs = jnp.concatenate([vn[k * c:(k + 1) * c, cols] for k in range(tm // c)], axis=1)
        mixed = _dot(w, rhs)
        for k in range(tm // c):
            rows = slice(k * c, (k + 1) * c)
            o_ref[rows, cols] = (u[rows, cols] * (mixed[:, rows] + bs_ref[g])).astype(o_ref.dtype)


def _sgu(proj, lng, lnb, ws, bs, tm):
    s = proj.shape[0]
    w = lng.shape[0]
    c = SGU_CHUNK
    bs_b = jnp.broadcast_to(bs[:, :, None], (SGU_GROUPS, c, c))
    return pl.pallas_call(
        _sgu_kernel,
        grid=(s // tm,),
        in_specs=[
            pl.BlockSpec((tm, w), lambda i: (i, 0)),
            pl.BlockSpec((tm, w), lambda i: (i, 1)),
            pl.BlockSpec((1, w), lambda i: (0, 0)),
            pl.BlockSpec((1, w), lambda i: (0, 0)),
            pl.BlockSpec((SGU_GROUPS, c, c), lambda i: (0, 0, 0)),
            pl.BlockSpec((SGU_GROUPS, c, c), lambda i: (0, 0, 0)),
        ],
        out_specs=pl.BlockSpec((tm, w), lambda i: (i, 0)),
        out_shape=jax.ShapeDtypeStruct((s, w), BF16),
        compiler_params=_params("parallel"),
        name="sgu",
    )(proj, proj, lng.reshape(1, w), lnb.reshape(1, w), ws, bs_b)


def _mix_kernel(oa_ref, ob_ref, ga_ref, gb_ref, x_ref, pa_ref, pb_ref, wo_ref,
                gc_ref, wq_ref, mkt_ref, mv_ref, wmo_ref, o_ref):
    a = _dot(oa_ref[...], pa_ref[...])
    b = _dot(ob_ref[...], pb_ref[...])
    merged = jax.nn.sigmoid(ga_ref[...]) * a + jax.nn.sigmoid(gb_ref[...]) * b
    x = x_ref[...] + _dot(merged.astype(BF16), wo_ref[...])
    h = _rms(x, gc_ref[...]).astype(BF16)
    mq = (_dot(h, wq_ref[...]) * (MEM_DIM ** -0.5)).astype(BF16)
    outs = []
    for hh in range(MEM_HEADS):
        s = _dot(mq[:, hh * MEM_DIM:(hh + 1) * MEM_DIM], mkt_ref[hh])
        e = jnp.exp(s - jnp.max(s, axis=-1, keepdims=True))
        p = e / jnp.sum(e, axis=-1, keepdims=True)
        outs.append(_dot(p.astype(BF16), mv_ref[hh]).astype(BF16))
    o_ref[...] = x + _dot(jnp.concatenate(outs, axis=1), wmo_ref[...])


def _mix(o_a, o_b, proj, x, p_a, p_b, w_o, gate_block, g_cross, w_q, mk_t, mv, w_mo, tm):
    s, d = x.shape
    wa = o_a.shape[1]
    wb = o_b.shape[1]
    mw = w_q.shape[1]
    m = mv.shape[1]
    resident = lambda shape: pl.BlockSpec(shape, lambda i: (0,) * len(shape), pipeline_mode=pl.Buffered(1))
    return pl.pallas_call(
        _mix_kernel,
        grid=(s // tm,),
        in_specs=[
            pl.BlockSpec((tm, wa), lambda i: (i, 0)),
            pl.BlockSpec((tm, wb), lambda i: (i, 0)),
            pl.BlockSpec((tm, d), lambda i: (i, gate_block)),
            pl.BlockSpec((tm, d), lambda i: (i, gate_block + 1)),
            pl.BlockSpec((tm, d), lambda i: (i, 0)),
            resident((wa, d)),
            resident((wb, d)),
            resident((d, d)),
            resident((1, d)),
            resident((d, mw)),
            resident((MEM_HEADS, MEM_DIM, m)),
            resident((MEM_HEADS, m, MEM_DIM)),
            resident((mw, d)),
        ],
        out_specs=pl.BlockSpec((tm, d), lambda i: (i, 0)),
        out_shape=jax.ShapeDtypeStruct((s, d), F32),
        compiler_params=_params("parallel"),
        name="mix",
    )(o_a, o_b, proj, proj, x, p_a, p_b, w_o, g_cross.reshape(1, d), w_q, mk_t, mv, w_mo)


def _ffn_kernel(x_ref, g_ref, wg_ref, wu_ref, wo_ref, gf_ref, o_ref, h_ref, acc_ref, *, final_norm):
    j = pl.program_id(1)

    @pl.when(j == 0)
    def _():
        h_ref[...] = _rms(x_ref[...], g_ref[...]).astype(BF16)
        acc_ref[...] = jnp.zeros_like(acc_ref)

    h = h_ref[...]
    th = wg_ref.shape[1]
    out = None
    for c in range(th // FFN_CHUNK):
        cs = slice(c * FFN_CHUNK, (c + 1) * FFN_CHUNK)
        act = (jax.nn.silu(_dot(h, wg_ref[:, cs])) * _dot(h, wu_ref[:, cs])).astype(BF16)
        part = _dot(act, wo_ref[cs, :])
        out = part if out is None else out + part
    acc_ref[...] += out

    @pl.when(j == pl.num_programs(1) - 1)
    def _():
        y = x_ref[...] + acc_ref[...]
        o_ref[...] = _rms(y, gf_ref[...]) if final_norm else y


def _ffn(x, g, w_in, w_out, g_final, final_norm, tm, th):
    s, d = x.shape
    hidden = w_out.shape[0]
    nh = hidden // th
    return pl.pallas_call(
        functools.partial(_ffn_kernel, final_norm=final_norm),
        grid=(s // tm, nh),
        in_specs=[
            pl.BlockSpec((tm, d), lambda i, j: (i, 0)),
            pl.BlockSpec((1, d), lambda i, j: (0, 0)),
            pl.BlockSpec((d, th), lambda i, j: (0, j)),
            pl.BlockSpec((d, th), lambda i, j: (0, j + nh)),
            pl.BlockSpec((th, d), lambda i, j: (j, 0)),
            pl.BlockSpec((1, d), lambda i, j: (0, 0)),
        ],
        out_specs=pl.BlockSpec((tm, d), lambda i, j: (i, 0)),
        out_shape=jax.ShapeDtypeStruct((s, d), F32),
        scratch_shapes=[pltpu.VMEM((tm, d), BF16), pltpu.VMEM((tm, d), F32)],
        compiler_params=_params("parallel", "arbitrary"),
        name="ffn",
    )(x, g.reshape(1, d), w_in, w_in, w_out, g_final.reshape(1, d))


def _layer(x, mem, norm_mix_g, w_in, cmp_pe_k, cmp_k_w1, cmp_k_b1, cmp_k_w2, cmp_pe_v, cmp_v_w1, cmp_v_b1, cmp_v_w2,
           sgu_ln_g, sgu_ln_b, sgu_ws, sgu_b, w_proj_a, w_proj_b, w_mix_out, norm_cross_g, norm_mem_g,
           w_mq, w_mkv, w_mo, norm_ffn_g):
    s, d = x.shape
    qw = NSA_HEADS * NSA_DIM
    kvw = NSA_GROUPS * NSA_DIM
    sguw = sgu_ln_g.shape[0]
    ngate = GATE_ROWS
    o_kv = qw
    o_gate = o_kv + N_KV_STREAMS * kvw
    o_u = o_gate + ngate

    w_a = w_in[:, :o_gate].astype(BF16)
    w_b = jnp.pad(w_in[:, o_u:].astype(BF16), ((0, 0), (0, PROJ_TN)))
    w_b = lax.dynamic_update_slice(w_b, w_in[:, o_gate:o_u].astype(BF16), (0, w_in.shape[1] - o_u))
    tm_proj = min(1024, s)
    nat, kw, ks_aug, vs_aug_t, vw_aug_t, q, proj_b, gates_t = _proj(x, norm_mix_g, w_a, w_b, tm_proj)

    ncp = s // CMP_STRIDE
    w1 = jnp.stack([cmp_k_w1, cmp_v_w1]).astype(BF16)
    pe = jnp.stack([cmp_pe_k.reshape(1, -1), cmp_pe_v.reshape(1, -1)])
    b1 = jnp.stack([cmp_k_b1.reshape(1, -1), cmp_v_b1.reshape(1, -1)])
    w2 = jnp.stack([cmp_k_w2, cmp_v_w2]).astype(BF16)
    cmp, cmp_t = _compress(nat, w1, pe, b1, w2)
    o_a = _nsa(q, gates_t, cmp[0], cmp_t[1], ks_aug, vs_aug_t, kw, vw_aug_t)

    o_b = _sgu(proj_b, sgu_ln_g, sgu_ln_b, sgu_ws, sgu_b, min(512, s))

    m = mem.shape[0]
    mw = MEM_HEADS * MEM_DIM
    mkv = _norm_matmul(mem, norm_mem_g, w_mkv.astype(BF16), F32, m, mw)
    mk_t = mkv[:, :mw].reshape(m, MEM_HEADS, MEM_DIM).transpose(1, 2, 0).astype(BF16)
    mv = mkv[:, mw:].reshape(m, MEM_HEADS, MEM_DIM).transpose(1, 0, 2).astype(BF16)
    return _mix(o_a, o_b, proj_b, x, w_proj_a.astype(BF16), w_proj_b.astype(BF16), w_mix_out.astype(BF16),
                (2 * sguw) // d, norm_cross_g, w_mq.astype(BF16), mk_t, mv, w_mo.astype(BF16), min(256, s))


def kernel(x, mem, norm_mix_g, w_in, cmp_pe_k, cmp_k_w1, cmp_k_b1, cmp_k_w2, cmp_pe_v, cmp_v_w1, cmp_v_b1, cmp_v_w2, sgu_ln_g, sgu_ln_b, sgu_ws, sgu_b, w_proj_a, w_proj_b, w_mix_out, norm_cross_g, norm_mem_g, w_mq, w_mkv, w_mo, norm_ffn_g, w_ffn_in, w_ffn_out, norm_final_g):
    b, s, d = x.shape
    depth = w_in.shape[0]
    outs = []
    for bi in range(b):
        xb = x[bi]
        for l in range(depth):
            last = l == depth - 1
            xb = _layer(xb, mem[bi], norm_mix_g[l], w_in[l], cmp_pe_k[l], cmp_k_w1[l], cmp_k_b1[l], cmp_k_w2[l],
                        cmp_pe_v[l], cmp_v_w1[l], cmp_v_b1[l], cmp_v_w2[l], sgu_ln_g[l], sgu_ln_b[l], sgu_ws[l],
                        sgu_b[l], w_proj_a[l], w_proj_b[l], w_mix_out[l], norm_cross_g[l], norm_mem_g[l],
                        w_mq[l], w_mkv[l], w_mo[l], norm_ffn_g[l])
            xb = _ffn(xb, norm_ffn_g[l], w_ffn_in[l].astype(BF16), w_ffn_out[l].astype(BF16), norm_final_g, last,
                      min(512, s), 512)
        outs.append(xb)
    return jnp.stack(outs)
```

```python
import functools
import math

import jax
import jax.numpy as jnp
from jax import lax
from jax.experimental import pallas as pl
from jax.experimental.pallas import tpu as pltpu

F32 = jnp.float32
BF16 = jnp.bfloat16

NORM_EPS = 1e-6
MASK_VALUE = -1e30
N_FORCED = 3
N_BRANCH = 3
LOG2E = math.log2(math.e)

NSA_HEADS = 16
NSA_GROUPS = 4
NSA_HPG = NSA_HEADS // NSA_GROUPS
NSA_DIM = 64
GATE_ROWS = NSA_HEADS * N_BRANCH
N_KV_STREAMS = 6
CMP_BLOCK = 32
CMP_STRIDE = 16
SEL_BLOCK = 64
SEL_TOPK = 16
WINDOW = 512
QUERY_BLOCK = 128
SGU_GROUPS = 8
SGU_CHUNK = 128
MEM_HEADS = 4
MEM_DIM = 128

LANES = 128
SUBLANES = 8
COV_RATIO = SEL_BLOCK // CMP_STRIDE
COV_LEAD = CMP_BLOCK // CMP_STRIDE - 1
COV_BAND = tuple(
    max(min((k - COV_LEAD) * CMP_STRIDE + CMP_BLOCK, SEL_BLOCK) - max((k - COV_LEAD) * CMP_STRIDE, 0), 0) / CMP_BLOCK
    for k in range(COV_RATIO + COV_LEAD))
assert SEL_BLOCK % CMP_STRIDE == 0 and CMP_BLOCK % CMP_STRIDE == 0 and COV_LEAD <= SUBLANES
SEL_TILE = 512
SEL_BPT = SEL_TILE // SEL_BLOCK
SEL_CHUNK = 256
SEL_UNROLLS = (16, 8, 4, 2)
NSA_STEP_BLOCKS = 4
NSA_BUCKETS = 8
BF16_ROWS = 16
FFN_CHUNK = 256
PROJ_TM, SGU_TM, MIX_TM, FFN_TM, FFN_TH = 1024, 512, 256, 512, 512
VMEM_LIMIT = 62 * 1024 * 1024


def _params(*sem):
    return pltpu.CompilerParams(dimension_semantics=sem, vmem_limit_bytes=VMEM_LIMIT)


def _rms(x, g):
    return x * lax.rsqrt(jnp.mean(x * x, axis=-1, keepdims=True) + NORM_EPS) * g


def _dot(a, b):
    return jnp.dot(a, b, preferred_element_type=F32)


def _ones_row(n):
    return jnp.where(lax.broadcasted_iota(jnp.int32, (BF16_ROWS, n), 0) == 0, 1.0, 0.0).astype(BF16)


def _norm_matmul_kernel(x_ref, g_ref, w_ref, o_ref, h_ref):
    @pl.when(pl.program_id(1) == 0)
    def _():
        h_ref[...] = _rms(x_ref[...], g_ref[...]).astype(BF16)

    o_ref[...] = _dot(h_ref[...], w_ref[...]).astype(o_ref.dtype)


def _norm_matmul(x, g, w, out_dtype, tm, tn):
    s, d = x.shape
    n = w.shape[1]
    return pl.pallas_call(
        _norm_matmul_kernel,
        grid=(s // tm, n // tn),
        in_specs=[
            pl.BlockSpec((tm, d), lambda i, j: (i, 0)),
            pl.BlockSpec((1, d), lambda i, j: (0, 0)),
            pl.BlockSpec((d, tn), lambda i, j: (0, j)),
        ],
        out_specs=pl.BlockSpec((tm, tn), lambda i, j: (i, j)),
        out_shape=jax.ShapeDtypeStruct((s, n), out_dtype),
        scratch_shapes=[pltpu.VMEM((tm, d), BF16)],
        compiler_params=_params("parallel", "arbitrary"),
        name="norm_matmul",
    )(x, g.reshape(1, d), w)


PROJ_TN = 512


def _proj_kernel(x_ref, g_ref, wa_ref, wb_ref, nat_ref, kw_ref, ksa_ref, vsa_ref, vwa_ref, q_ref, o_ref, gt_ref,
                 h_ref, stage_ref):
    i = pl.program_id(0)
    j = pl.program_id(1)
    tm = x_ref.shape[0]
    dh = NSA_DIM
    ng = NSA_GROUPS
    tn = PROJ_TN
    qw = q_ref.shape[1]

    @pl.when(j == 0)
    def _():
        h_ref[...] = _rms(x_ref[...], g_ref[...]).astype(BF16)
        h = h_ref[...]

        def tile(t):
            return _dot(h, wa_ref[:, qw + t * tn:qw + (t + 1) * tn])

        def cols(res, c):
            return res[:, c * dh:(c + 1) * dh].astype(BF16)

        def cols_t(res, c):
            slab = res[:, (c // 2) * 2 * dh:(c // 2 + 1) * 2 * dh].T
            return slab[(c % 2) * dh:(c % 2 + 1) * dh].astype(BF16)

        res = tile(0)
        for c in range(2 * ng):
            stage_ref[...] = res[:, c * dh:(c + 1) * dh]
            for t in range(0, CMP_STRIDE, 2):
                pair = [stage_ref[pl.ds(t + u, tm // CMP_STRIDE, stride=CMP_STRIDE), :] for u in range(2)]
                nat_ref[c // ng, c % ng, :, t * dh:(t + 2) * dh] = jnp.concatenate(pair, axis=1).astype(BF16)

        res = tile(1)
        pos = i * tm + lax.broadcasted_iota(jnp.int32, (tm, dh), 0)
        lane = lax.broadcasted_iota(jnp.int32, (tm, dh), 1)
        onehot = jnp.where((pos // SEL_BLOCK) % SEL_BPT == lane, 1.0, 0.0).astype(BF16)
        for g in range(ng):
            ksa_ref[g, :, :dh] = cols(res, g)
            ksa_ref[g, :, dh:] = onehot
            vsa_ref[g, :dh, :] = cols_t(res, ng + g)
            vsa_ref[g, dh:, :] = _ones_row(tm)

        res = tile(2)
        for g in range(ng):
            kw_ref[g] = cols(res, g)
            vwa_ref[g, :dh, :] = cols_t(res, ng + g)
            vwa_ref[g, dh:, :] = _ones_row(tm)

        for t in range(qw // tn):
            q_ref[:, t * tn:(t + 1) * tn] = _dot(h, wa_ref[:, t * tn:(t + 1) * tn]).astype(BF16)

    @pl.when(j >= 1)
    def _():
        res = _dot(h_ref[...], wb_ref[...])
        o_ref[...] = res

        @pl.when(j == 1)
        def _():
            gt_ref[...] = res[:, :LANES].T


def _proj(x, g, w_a, w_b, tm):
    s, d = x.shape
    dh, ng, tn = NSA_DIM, NSA_GROUPS, PROJ_TN
    qw = NSA_HEADS * dh
    assert 2 * ng * dh == tn and qw % tn == 0 and w_a.shape[1] == qw + N_KV_STREAMS // 2 * tn
    n = w_b.shape[1] - tn
    nb = n // tn
    shapes = [
        jax.ShapeDtypeStruct((2, ng, s // CMP_STRIDE, CMP_STRIDE * dh), BF16),
        jax.ShapeDtypeStruct((ng, s, dh), BF16),
        jax.ShapeDtypeStruct((ng, s, 2 * dh), BF16),
        jax.ShapeDtypeStruct((ng, dh + BF16_ROWS, s), BF16),
        jax.ShapeDtypeStruct((ng, dh + BF16_ROWS, s), BF16),
        jax.ShapeDtypeStruct((s, NSA_HEADS * dh), BF16),
        jax.ShapeDtypeStruct((s, n), F32),
        jax.ShapeDtypeStruct((LANES, s), F32),
    ]
    return pl.pallas_call(
        _proj_kernel,
        grid=(s // tm, 2 + nb),
        in_specs=[
            pl.BlockSpec((tm, d), lambda i, j: (i, 0)),
            pl.BlockSpec((1, d), lambda i, j: (0, 0)),
            pl.BlockSpec(w_a.shape, lambda i, j: (0, 0), pipeline_mode=pl.Buffered(1)),
            pl.BlockSpec((d, tn), lambda i, j: (0, jnp.where(j <= 1, nb, j - 2))),
        ],
        out_specs=[
            pl.BlockSpec((2, ng, tm // CMP_STRIDE, CMP_STRIDE * dh), lambda i, j: (0, 0, i, 0)),
            pl.BlockSpec((ng, tm, dh), lambda i, j: (0, i, 0)),
            pl.BlockSpec((ng, tm, 2 * dh), lambda i, j: (0, i, 0)),
            pl.BlockSpec((ng, dh + BF16_ROWS, tm), lambda i, j: (0, 0, i)),
            pl.BlockSpec((ng, dh + BF16_ROWS, tm), lambda i, j: (0, 0, i)),
            pl.BlockSpec((tm, qw), lambda i, j: (i, 0)),
            pl.BlockSpec((tm, tn), lambda i, j: (i, jnp.maximum(j - 2, 0))),
            pl.BlockSpec((LANES, tm), lambda i, j: (0, i)),
        ],
        out_shape=shapes,
        scratch_shapes=[pltpu.VMEM((tm, d), BF16), pltpu.VMEM((tm, dh), F32)],
        compiler_params=_params("parallel", "arbitrary"),
        name="proj",
    )(x, g.reshape(1, d), w_a, w_b)


def _compress_kernel(r_ref, w1_ref, pe_ref, b1_ref, w2_ref, o_ref, ot_ref):
    half = r_ref.shape[-1]
    r = r_ref[0, 0]
    bias = _dot(pe_ref[0].astype(BF16), w1_ref[0]) + b1_ref[0]
    top = _dot(r, w1_ref[0, :half, :])
    bot = _dot(r, w1_ref[0, half:, :])
    ncp = r.shape[0]
    hid = top + pltpu.roll(bot, ncp - 1, 0) + bias
    out = _dot(jax.nn.gelu(hid).astype(BF16), w2_ref[0])
    o_ref[0, 0] = out.astype(o_ref.dtype)
    dh = out.shape[1]
    out_t = jnp.concatenate([out, jnp.zeros((ncp, LANES - dh), F32)], axis=1).T
    ot_ref[0, 0, :dh, :] = out_t[:dh].astype(ot_ref.dtype)
    ot_ref[0, 0, dh:, :] = _ones_row(ncp)


def _compress(r, w1, pe, b1, w2):
    _, g, ncp, half = r.shape
    hidden = w1.shape[-1]
    dh = w2.shape[-1]
    return pl.pallas_call(
        _compress_kernel,
        grid=(2, g),
        in_specs=[
            pl.BlockSpec((1, 1, ncp, half), lambda s, gg: (s, gg, 0, 0)),
            pl.BlockSpec((1, 2 * half, hidden), lambda s, gg: (s, 0, 0)),
            pl.BlockSpec((1, 1, 2 * half), lambda s, gg: (s, 0, 0)),
            pl.BlockSpec((1, 1, hidden), lambda s, gg: (s, 0, 0)),
            pl.BlockSpec((1, hidden, dh), lambda s, gg: (s, 0, 0)),
        ],
        out_specs=[pl.BlockSpec((1, 1, ncp, dh), lambda s, gg: (s, gg, 0, 0)),
                   pl.BlockSpec((1, 1, dh + BF16_ROWS, ncp), lambda s, gg: (s, gg, 0, 0))],
        out_shape=[jax.ShapeDtypeStruct((2, g, ncp, dh), BF16),
                   jax.ShapeDtypeStruct((2, g, dh + BF16_ROWS, ncp), BF16)],
        compiler_params=_params("parallel", "parallel"),
        name="compress",
    )(r, w1, pe, b1, w2)


PICKED = -2.0


def _nsa_kernel(q_ref, gt_ref, kc_ref, vct_ref, ks_ref, vst_ref, kw_ref, vwt_ref, o_ref,
                qa_ref, sel_ref, part_ref, gates_ref, s0_ref, s1_ref, acc_ref, blkf_ref, cend_ref, wrel_ref, pp_ref):
    g = pl.program_id(0)
    step = pl.program_id(1)
    qb = QUERY_BLOCK
    nb = sel_ref.shape[1]
    ncp = kc_ref.shape[1]
    s_len = ks_ref.shape[1]
    first = step * NSA_STEP_BLOCKS

    @pl.when(step == 0)
    def _():
        lane = lax.broadcasted_iota(jnp.int32, cend_ref.shape, 1)
        cend_ref[...] = lax.broadcasted_iota(jnp.int32, cend_ref.shape, 0) * CMP_STRIDE + (CMP_BLOCK - 1) - lane
        wrel_ref[...] = (lax.broadcasted_iota(jnp.int32, wrel_ref.shape, 0)
                         - lax.broadcasted_iota(jnp.int32, wrel_ref.shape, 1))

    def rows(sub):
        return pl.ds(pl.multiple_of(sub * qb, qb), qb)

    bucket_ok = s_len % (NSA_BUCKETS * SEL_TILE) == 0 and nb // NSA_BUCKETS >= SEL_TOPK
    n_bucket = NSA_BUCKETS if bucket_ok else 1
    bucket = ((first + NSA_STEP_BLOCKS) * qb - 1) // (s_len // n_bucket)
    for b in range(n_bucket):
        nck, nbk = (b + 1) * (ncp // n_bucket), (b + 1) * (nb // n_bucket)

        @pl.when(bucket == b)
        def _():
            def scores(sub, carry):
                _nsa_scores(g, first + sub, q_ref.at[rows(sub), :], gt_ref.at[:, rows(sub)], kc_ref, vct_ref, kw_ref,
                            vwt_ref, qa_ref.at[sub], sel_ref.at[sub], part_ref.at[sub], gates_ref.at[sub], pp_ref,
                            cend_ref, wrel_ref, nck, nbk)
                return carry

            lax.fori_loop(0, NSA_STEP_BLOCKS, scores, 0)
            _nsa_topk(first, sel_ref, blkf_ref, nbk)

    def select(sub, carry):
        _nsa_select(g, first + sub, ks_ref, vst_ref, o_ref.at[rows(sub), :], qa_ref.at[sub], sel_ref.at[sub],
                    part_ref.at[sub], gates_ref.at[sub], s0_ref, s1_ref, acc_ref)
        return carry

    lax.fori_loop(0, NSA_STEP_BLOCKS, select, 0)


def _gate(gates_ref, g, branch):
    rows = [gates_ref[pl.ds((g * NSA_HPG + h) * N_BRANCH + branch, 1), :] for h in range(NSA_HPG)]
    return jax.nn.sigmoid(jnp.concatenate(rows, axis=1))


def _nsa_scores(g, i, q_ref, gt_ref, kc_ref, vct_ref, kw_ref, vwt_ref, qa_ref, score_ref, part_ref, gates_ref,
                pp_ref, cend_ref, wrel_ref, nck, nbk):
    qb = QUERY_BLOCK
    start = i * qb
    hq = NSA_HPG * qb
    t_row = start + lax.broadcasted_iota(jnp.int32, (1, qb), 1)
    jt = t_row // SEL_BLOCK

    qt = (q_ref[...].astype(F32) * (NSA_DIM ** -0.5 * LOG2E)).T
    q_t = jnp.concatenate([qt[h * NSA_DIM:(h + 1) * NSA_DIM] for h in range(NSA_HPG)], axis=1).astype(BF16)
    qa_ref[:NSA_DIM, :] = q_t
    qa_ref[NSA_DIM:, :] = jnp.zeros((qa_ref.shape[0] - NSA_DIM, hq), BF16)
    gates_ref[...] = gt_ref[...]

    def exps(s, bias):
        out = []
        for h in range(NSA_HPG):
            sh = s[:, h * qb:(h + 1) * qb] + bias
            out.append(jnp.exp2(sh - jnp.max(sh, axis=0, keepdims=True)))
        return out

    wk = WINDOW + qb
    k0w = pl.multiple_of(jnp.maximum(start - WINDOW, 0), qb)
    sc = _dot(kc_ref[0, :nck, :], q_t)
    sw = _dot(kw_ref[0, pl.ds(k0w, wk), :], q_t)
    e_cmp = exps(sc, jnp.where(cend_ref[:nck, :] <= start, 0.0, MASK_VALUE))
    o_cmp = _dot(vct_ref[0, :, :nck], jnp.concatenate([e.astype(BF16) for e in e_cmp], axis=1))
    wrel = wrel_ref[...]
    e_win = exps(sw, jnp.where((wrel <= start - k0w) & (wrel > start - k0w - WINDOW), 0.0, MASK_VALUE))
    o_win = _dot(vwt_ref[0, :, pl.ds(k0w, wk)], jnp.concatenate([e.astype(BF16) for e in e_win], axis=1))
    inv_cmp = jnp.where(jnp.concatenate([t_row >= CMP_BLOCK - 1] * NSA_HPG, axis=1),
                        1.0 / o_cmp[NSA_DIM:NSA_DIM + 1, :], 0.0)
    psum = e_cmp[0] * inv_cmp[:, :qb]
    for h in range(1, NSA_HPG):
        psum = psum + e_cmp[h] * inv_cmp[:, h * qb:(h + 1) * qb]
    part_ref[...] = (_gate(gates_ref, g, 0) * inv_cmp * o_cmp[:NSA_DIM]
                     + _gate(gates_ref, g, 2) * (1.0 / o_win[NSA_DIM:NSA_DIM + 1, :]) * o_win[:NSA_DIM])
    pp_ref[:SUBLANES, :] = jnp.zeros((SUBLANES, qb), F32)
    pp_ref[SUBLANES:SUBLANES + nck, :] = psum
    imp = None
    for k, w in enumerate(COV_BAND):
        tap = pp_ref[pl.ds(SUBLANES - COV_LEAD + k, nbk, stride=COV_RATIO), :]
        tap = tap if w == 1.0 else w * tap
        imp = tap if imp is None else imp + tap
    blk = lax.broadcasted_iota(jnp.int32, (nbk, qb), 0)
    forced = (blk == 0) | (blk == jt) | (blk == jt - 1)
    score_ref[:nbk, :] = jnp.where(forced, PICKED, jnp.where(blk <= jt, imp, -1.0))


def _nsa_topk(first, sel_ref, blkf_ref, nbk):
    qb = QUERY_BLOCK
    blk = lax.broadcasted_iota(jnp.int32, (nbk, qb), 0)
    blkf_ref[:nbk, :] = blk.astype(F32)

    def pick(_, scores):
        out = []
        for score in scores:
            mx = jnp.max(score, axis=0, keepdims=True)
            cand = jnp.where(score == mx, blkf_ref[:nbk, :], float(nbk))
            out.append(jnp.where(cand == jnp.min(cand, axis=0, keepdims=True), PICKED, score))
        return tuple(out)

    scores = tuple(sel_ref[b, :nbk, :] for b in range(sel_ref.shape[0]))
    scores = lax.fori_loop(0, min(SEL_TOPK, nbk) - N_FORCED, pick, scores)
    for b, score in enumerate(scores):
        jt = ((first + b) * qb + lax.broadcasted_iota(jnp.int32, (1, qb), 1)) // SEL_BLOCK
        sel_ref[b, :nbk, :] = jnp.where((score == PICKED) & (blk <= jt), 0.0, MASK_VALUE)


def _nsa_select(g, i, ks_ref, vst_ref, o_ref, qa_ref, sel_ref, part_ref, gates_ref, s0_ref, s1_ref, acc_ref):
    s_refs = (s0_ref, s1_ref)
    qb = QUERY_BLOCK
    start = i * qb
    hq = NSA_HPG * qb
    t_row = start + lax.broadcasted_iota(jnp.int32, (1, qb), 1)
    tk = SEL_TILE

    def scores(slot, kt):
        k0 = pl.multiple_of(kt * tk, tk)
        b8 = sel_ref[pl.ds(pl.multiple_of(kt * SEL_BPT, SEL_BPT), SEL_BPT), :]
        b16 = jnp.concatenate([b8, jnp.zeros((BF16_ROWS - SEL_BPT, qb), F32)], axis=0).astype(BF16)
        qa_ref[NSA_DIM:NSA_DIM + BF16_ROWS, :] = jnp.concatenate([b16] * NSA_HPG, axis=1)
        s = _dot(ks_ref[0, pl.ds(k0, tk), :], qa_ref[...])
        s_refs[slot][...] = s
        return jnp.max(s, axis=0, keepdims=True)

    def update(slot, kt, mt, m, causal):
        k0 = pl.multiple_of(kt * tk, tk)
        s = s_refs[slot][...]
        if causal:
            kpos = k0 + lax.broadcasted_iota(jnp.int32, (tk, qb), 0)
            s = s + jnp.concatenate([jnp.where(kpos <= t_row, 0.0, MASK_VALUE)] * NSA_HPG, axis=1)
            mt = jnp.max(s, axis=0, keepdims=True)
        m_new = jnp.maximum(m, mt)
        p = jnp.exp2(s - m_new).astype(BF16)
        acc_ref[...] = jnp.exp2(m - m_new) * acc_ref[...] + _dot(vst_ref[0, :, pl.ds(k0, tk)], p)
        return m_new

    def accumulate(pend):
        p, k0, alpha = pend
        pv = _dot(vst_ref[0, :, pl.ds(k0, SEL_CHUNK)], p)
        acc_ref[...] = (acc_ref[...] if alpha is None else alpha * acc_ref[...]) + pv

    def run(first, n_tiles, carry):
        mt, m = carry
        pend = None
        for k in range(n_tiles):
            cur, nxt = s_refs[k % 2], s_refs[(k + 1) % 2]
            k0c = pl.multiple_of((first + k) * tk, tk)
            k0n = pl.multiple_of((first + k + 1) * tk, tk)
            b8 = sel_ref[pl.ds(pl.multiple_of((first + k + 1) * SEL_BPT, SEL_BPT), SEL_BPT), :]
            b16 = jnp.concatenate([b8, jnp.zeros((BF16_ROWS - SEL_BPT, qb), F32)], axis=0).astype(BF16)
            qa_ref[NSA_DIM:NSA_DIM + BF16_ROWS, :] = jnp.concatenate([b16] * NSA_HPG, axis=1)
            m_new = jnp.maximum(m, mt)
            alpha = jnp.exp2(m - m_new)
            mt = None
            for c in range(tk // SEL_CHUNK):
                rows = pl.ds(c * SEL_CHUNK, SEL_CHUNK)
                s_n = _dot(ks_ref[0, pl.ds(k0n + c * SEL_CHUNK, SEL_CHUNK), :], qa_ref[...])
                nxt[rows, :] = s_n
                mt_c = jnp.max(s_n, axis=0, keepdims=True)
                mt = mt_c if mt is None else jnp.maximum(mt, mt_c)
                p = jnp.exp2(cur[rows, :] - m_new).astype(BF16)
                if pend is not None:
                    accumulate(pend)
                pend = (p, k0c + c * SEL_CHUNK, alpha if c == 0 else None)
            m = m_new
        accumulate(pend)
        return mt, m

    acc_ref[...] = jnp.zeros(acc_ref.shape, F32)
    last = (start + qb - 1) // tk
    carry = (scores(0, 0), jnp.full((1, hq), MASK_VALUE, F32))
    done = 0
    for unroll in SEL_UNROLLS:
        trips = (last - done) // unroll
        carry = lax.fori_loop(0, trips, lambda it, c, d=done, u=unroll: run(d + it * u, u, c), carry)
        done = done + trips * unroll
    mt_a, m = carry

    @pl.when(last % 2 == 0)
    def _():
        update(0, last, mt_a, m, True)

    @pl.when(last % 2 == 1)
    def _():
        mt_b = scores(1, last)
        update(1, last, mt_b, update(0, last - 1, mt_a, m, False), True)

    o_sel = acc_ref[:NSA_DIM, :] * (1.0 / acc_ref[NSA_DIM:NSA_DIM + 1, :])

    out_t = part_ref[...] + _gate(gates_ref, g, 1) * o_sel
    outs = [out_t[:, h * qb:(h + 1) * qb] for h in range(NSA_HPG)]
    o_ref[...] = jnp.concatenate(outs, axis=0).T.astype(o_ref.dtype)


def _nsa(q, gates_t, kc, vc_aug_t, ks_aug, vs_aug_t, kw, vw_aug_t):
    s = q.shape[0]
    assert s % (2 * SEL_TILE) == 0 and s >= WINDOW + QUERY_BLOCK
    gdim = NSA_HPG * NSA_DIM
    ncp = kc.shape[1]
    assert ncp * CMP_STRIDE == s and s // SEL_BLOCK * COV_RATIO == ncp
    nb = s // SEL_BLOCK
    ka = ks_aug.shape[-1]
    va = vs_aug_t.shape[1]
    hq = NSA_HPG * QUERY_BLOCK
    per_group = lambda shape: pl.BlockSpec((1,) + shape, lambda g, i: (g, 0, 0))
    rows = NSA_STEP_BLOCKS * QUERY_BLOCK
    assert s % rows == 0
    return pl.pallas_call(
        _nsa_kernel,
        grid=(NSA_GROUPS, s // rows),
        in_specs=[
            pl.BlockSpec((rows, gdim), lambda g, i: (i, g)),
            pl.BlockSpec((GATE_ROWS, rows), lambda g, i: (0, i)),
            per_group((ncp, NSA_DIM)),
            per_group((va, ncp)),
            per_group((s, ka)),
            per_group((va, s)),
            per_group((s, NSA_DIM)),
            per_group((va, s)),
        ],
        out_specs=pl.BlockSpec((rows, gdim), lambda g, i: (i, g)),
        out_shape=jax.ShapeDtypeStruct((s, NSA_HEADS * NSA_DIM), BF16),
        scratch_shapes=[pltpu.VMEM((NSA_STEP_BLOCKS, ka, hq), BF16),
                        pltpu.VMEM((NSA_STEP_BLOCKS, nb, QUERY_BLOCK), F32),
                        pltpu.VMEM((NSA_STEP_BLOCKS, NSA_DIM, hq), F32),
                        pltpu.VMEM((NSA_STEP_BLOCKS, GATE_ROWS, QUERY_BLOCK), F32),
                        pltpu.VMEM((SEL_TILE, hq), F32), pltpu.VMEM((SEL_TILE, hq), F32),
                        pltpu.VMEM((va, hq), F32),
                        pltpu.VMEM((nb, QUERY_BLOCK), F32),
                        pltpu.VMEM((ncp, QUERY_BLOCK), jnp.int32),
                        pltpu.VMEM((WINDOW + QUERY_BLOCK, QUERY_BLOCK), jnp.int32),
                        pltpu.VMEM((SUBLANES + ncp, QUERY_BLOCK), F32)],
        compiler_params=_params("parallel", "arbitrary"),
        name="nsa",
    )(q, gates_t, kc, vc_aug_t, ks_aug, vs_aug_t, kw, vw_aug_t)


def _sgu_kernel(u_ref, v_ref, lng_ref, lnb_ref, ws_ref, bs_ref, o_ref):
    c = SGU_CHUNK
    tm = u_ref.shape[0]
    v = jax.nn.gelu(v_ref[...])
    mu = jnp.mean(v, axis=-1, keepdims=True)
    var = jnp.mean(jnp.square(v - mu), axis=-1, keepdims=True)
    vn = ((v - mu) * lax.rsqrt(var + NORM_EPS) * lng_ref[...] + lnb_ref[...]).astype(BF16)
    u = jax.nn.gelu(u_ref[...])
    tri = lax.broadcasted_iota(jnp.int32, (c, c), 0) >= lax.broadcasted_iota(jnp.int32, (c, c), 1)
    for g in range(SGU_GROUPS):
        w = jnp.where(tri, ws_ref[g], 0.0).astype(BF16)
        cols = slice(g * c, (g + 1) * c)
        rhs = jnp.concatenate([vn[k * c:(k + 1) * c, cols] for k in range(tm // c)], axis=1)
        mixed = _dot(w, rhs)
        for k in range(tm // c):
            rows = slice(k * c, (k + 1) * c)
            o_ref[rows, cols] = (u[rows, cols] * (mixed[:, rows] + bs_ref[g])).astype(o_ref.dtype)


def _sgu(proj, lng, lnb, ws, bs, tm):
    s = proj.shape[0]
    w = lng.shape[0]
    c = SGU_CHUNK
    bs_b = jnp.broadcast_to(bs[:, :, None], (SGU_GROUPS, c, c))
    return pl.pallas_call(
        _sgu_kernel,
        grid=(s // tm,),
        in_specs=[
            pl.BlockSpec((tm, w), lambda i: (i, 0)),
            pl.BlockSpec((tm, w), lambda i: (i, 1)),
            pl.BlockSpec((1, w), lambda i: (0, 0)),
            pl.BlockSpec((1, w), lambda i: (0, 0)),
            pl.BlockSpec((SGU_GROUPS, c, c), lambda i: (0, 0, 0)),
            pl.BlockSpec((SGU_GROUPS, c, c), lambda i: (0, 0, 0)),
        ],
        out_specs=pl.BlockSpec((tm, w), lambda i: (i, 0)),
        out_shape=jax.ShapeDtypeStruct((s, w), BF16),
        compiler_params=_params("parallel"),
        name="sgu",
    )(proj, proj, lng.reshape(1, w), lnb.reshape(1, w), ws, bs_b)


def _mix_kernel(oa_ref, ob_ref, ga_ref, gb_ref, x_ref, pa_ref, pb_ref, wo_ref,
                gc_ref, wq_ref, mkt_ref, mv_ref, wmo_ref, o_ref):
    a = _dot(oa_ref[...], pa_ref[...])
    b = _dot(ob_ref[...], pb_ref[...])
    merged = jax.nn.sigmoid(ga_ref[...]) * a + jax.nn.sigmoid(gb_ref[...]) * b
    x = x_ref[...] + _dot(merged.astype(BF16), wo_ref[...])
    h = _rms(x, gc_ref[...]).astype(BF16)
    mq = (_dot(h, wq_ref[...]) * (MEM_DIM ** -0.5)).astype(BF16)
    outs = []
    for hh in range(MEM_HEADS):
        s = _dot(mq[:, hh * MEM_DIM:(hh + 1) * MEM_DIM], mkt_ref[hh])
        e = jnp.exp(s - jnp.max(s, axis=-1, keepdims=True))
        p = e / jnp.sum(e, axis=-1, keepdims=True)
        outs.append(_dot(p.astype(BF16), mv_ref[hh]).astype(BF16))
    o_ref[...] = x + _dot(jnp.concatenate(outs, axis=1), wmo_ref[...])


def _mix(o_a, o_b, proj, x, p_a, p_b, w_o, gate_block, g_cross, w_q, mk_t, mv, w_mo, tm):
    s, d = x.shape
    wa = o_a.shape[1]
    wb = o_b.shape[1]
    mw = w_q.shape[1]
    m = mv.shape[1]
    resident = lambda shape: pl.BlockSpec(shape, lambda i: (0,) * len(shape), pipeline_mode=pl.Buffered(1))
    return pl.pallas_call(
        _mix_kernel,
        grid=(s // tm,),
        in_specs=[
            pl.BlockSpec((tm, wa), lambda i: (i, 0)),
            pl.BlockSpec((tm, wb), lambda i: (i, 0)),
            pl.BlockSpec((tm, d), lambda i: (i, gate_block)),
            pl.BlockSpec((tm, d), lambda i: (i, gate_block + 1)),
            pl.BlockSpec((tm, d), lambda i: (i, 0)),
            resident((wa, d)),
            resident((wb, d)),
            resident((d, d)),
            resident((1, d)),
            resident((d, mw)),
            resident((MEM_HEADS, MEM_DIM, m)),
            resident((MEM_HEADS, m, MEM_DIM)),
            resident((mw, d)),
        ],
        out_specs=pl.BlockSpec((tm, d), lambda i: (i, 0)),
        out_shape=jax.ShapeDtypeStruct((s, d), F32),
        compiler_params=_params("parallel"),
        name="mix",
    )(o_a, o_b, proj, proj, x, p_a, p_b, w_o, g_cross.reshape(1, d), w_q, mk_t, mv, w_mo)


def _ffn_kernel(x_ref, g_ref, wg_ref, wu_ref, wo_ref, gf_ref, o_ref, h_ref, acc_ref, *, final_norm):
    j = pl.program_id(1)

    @pl.when(j == 0)
    def _():
        h_ref[...] = _rms(x_ref[...], g_ref[...]).astype(BF16)
        acc_ref[...] = jnp.zeros_like(acc_ref)

    h = h_ref[...]
    th = wg_ref.shape[1]
    out = None
    for c in range(th // FFN_CHUNK):
        cs = slice(c * FFN_CHUNK, (c + 1) * FFN_CHUNK)
        act = (jax.nn.silu(_dot(h, wg_ref[:, cs])) * _dot(h, wu_ref[:, cs])).astype(BF16)
        part = _dot(act, wo_ref[cs, :])
        out = part if out is None else out + part
    acc_ref[...] += out

    @pl.when(j == pl.num_programs(1) - 1)
    def _():
        y = x_ref[...] + acc_ref[...]
        o_ref[...] = _rms(y, gf_ref[...]) if final_norm else y


def _ffn(x, g, w_in, w_out, g_final, final_norm, tm, th):
    s, d = x.shape
    hidden = w_out.shape[0]
    nh = hidden // th
    return pl.pallas_call(
        functools.partial(_ffn_kernel, final_norm=final_norm),
        grid=(s // tm, nh),
        in_specs=[
            pl.BlockSpec((tm, d), lambda i, j: (i, 0)),
            pl.BlockSpec((1, d), lambda i, j: (0, 0)),
            pl.BlockSpec((d, th), lambda i, j: (0, j)),
            pl.BlockSpec((d, th), lambda i, j: (0, j + nh)),
            pl.BlockSpec((th, d), lambda i, j: (j, 0)),
            pl.BlockSpec((1, d), lambda i, j: (0, 0)),
        ],
        out_specs=pl.BlockSpec((tm, d), lambda i, j: (i, 0)),
        out_shape=jax.ShapeDtypeStruct((s, d), F32),
        scratch_shapes=[pltpu.VMEM((tm, d), BF16), pltpu.VMEM((tm, d), F32)],
        compiler_params=_params("parallel", "arbitrary"),
        name="ffn",
    )(x, g.reshape(1, d), w_in, w_in, w_out, g_final.reshape(1, d))


def _layer(x, mem, norm_mix_g, w_in, cmp_pe_k, cmp_k_w1, cmp_k_b1, cmp_k_w2, cmp_pe_v, cmp_v_w1, cmp_v_b1, cmp_v_w2,
           sgu_ln_g, sgu_ln_b, sgu_ws, sgu_b, w_proj_a, w_proj_b, w_mix_out, norm_cross_g, norm_mem_g,
           w_mq, w_mkv, w_mo, norm_ffn_g):
    s, d = x.shape
    qw = NSA_HEADS * NSA_DIM
    kvw = NSA_GROUPS * NSA_DIM
    sguw = sgu_ln_g.shape[0]
    ngate = GATE_ROWS
    o_kv = qw
    o_gate = o_kv + N_KV_STREAMS * kvw
    o_u = o_gate + ngate

    w_a = w_in[:, :o_gate].astype(BF16)
    w_b = jnp.pad(w_in[:, o_u:].astype(BF16), ((0, 0), (0, PROJ_TN)))
    w_b = lax.dynamic_update_slice(w_b, w_in[:, o_gate:o_u].astype(BF16), (0, w_in.shape[1] - o_u))
    nat, kw, ks_aug, vs_aug_t, vw_aug_t, q, proj_b, gates_t = _proj(x, norm_mix_g, w_a, w_b, min(PROJ_TM, s))

    w1 =jnp.stack([cmp_k_w1, cmp_v_w1]).astype(BF16)
    pe = jnp.stack([cmp_pe_k.reshape(1, -1), cmp_pe_v.reshape(1, -1)])
    b1 = jnp.stack([cmp_k_b1.reshape(1, -1), cmp_v_b1.reshape(1, -1)])
    w2 = jnp.stack([cmp_k_w2, cmp_v_w2]).astype(BF16)
    cmp, cmp_t = _compress(nat, w1, pe, b1, w2)
    o_a = _nsa(q, gates_t, cmp[0], cmp_t[1], ks_aug, vs_aug_t, kw, vw_aug_t)

    o_b = _sgu(proj_b, sgu_ln_g, sgu_ln_b, sgu_ws, sgu_b, min(SGU_TM, s))

    m = mem.shape[0]
    mw = MEM_HEADS * MEM_DIM
    mkv = _norm_matmul(mem, norm_mem_g, w_mkv.astype(BF16), F32, m, mw)
    mk_t = mkv[:, :mw].reshape(m, MEM_HEADS, MEM_DIM).transpose(1, 2, 0).astype(BF16)
    mv = mkv[:, mw:].reshape(m, MEM_HEADS, MEM_DIM).transpose(1, 0, 2).astype(BF16)
    return _mix(o_a, o_b, proj_b, x, w_proj_a.astype(BF16), w_proj_b.astype(BF16), w_mix_out.astype(BF16),
                (2 * sguw) // d, norm_cross_g, w_mq.astype(BF16), mk_t, mv, w_mo.astype(BF16), min(MIX_TM, s))


def kernel(x, mem, norm_mix_g, w_in, cmp_pe_k, cmp_k_w1, cmp_k_b1, cmp_k_w2, cmp_pe_v, cmp_v_w1, cmp_v_b1, cmp_v_w2, sgu_ln_g, sgu_ln_b, sgu_ws, sgu_b, w_proj_a, w_proj_b, w_mix_out, norm_cross_g, norm_mem_g, w_mq, w_mkv, w_mo, norm_ffn_g, w_ffn_in, w_ffn_out, norm_final_g):
    b, s, d = x.shape
    depth = w_in.shape[0]
    outs = []
    for bi in range(b):
        xb = x[bi]
        for l in range(depth):
            last = l == depth - 1
            xb = _layer(xb, mem[bi], norm_mix_g[l], w_in[l], cmp_pe_k[l], cmp_k_w1[l], cmp_k_b1[l], cmp_k_w2[l],
                        cmp_pe_v[l], cmp_v_w1[l], cmp_v_b1[l], cmp_v_w2[l], sgu_ln_g[l], sgu_ln_b[l], sgu_ws[l],
                        sgu_b[l], w_proj_a[l], w_proj_b[l], w_mix_out[l], norm_cross_g[l], norm_mem_g[l],
                        w_mq[l], w_mkv[l], w_mo[l], norm_ffn_g[l])
            xb = _ffn(xb, norm_ffn_g[l], w_ffn_in[l].astype(BF16), w_ffn_out[l].astype(BF16), norm_final_g, last,
                      min(FFN_TM, s), FFN_TH)
        outs.append(xb)
    return jnp.stack(outs)
```

```python
import functools
import math

import jax
import jax.numpy as jnp
from jax import lax
from jax.experimental import pallas as pl
from jax.experimental.pallas import tpu as pltpu

F32 = jnp.float32
BF16 = jnp.bfloat16

NORM_EPS = 1e-6
MASK_VALUE = -1e30
N_FORCED = 3
N_BRANCH = 3
LOG2E = math.log2(math.e)

NSA_HEADS = 16
NSA_GROUPS = 4
NSA_HPG = NSA_HEADS // NSA_GROUPS
NSA_DIM = 64
GATE_ROWS = NSA_HEADS * N_BRANCH
N_KV_STREAMS = 6
CMP_BLOCK = 32
CMP_STRIDE = 16
SEL_BLOCK = 64
SEL_TOPK = 16
WINDOW = 512
QUERY_BLOCK = 128
SGU_GROUPS = 8
SGU_CHUNK = 128
MEM_HEADS = 4
MEM_DIM = 128

LANES = 128
SUBLANES = 8
COV_RATIO = SEL_BLOCK // CMP_STRIDE
COV_LEAD = CMP_BLOCK // CMP_STRIDE - 1
COV_BAND = tuple(
    max(min((k - COV_LEAD) * CMP_STRIDE + CMP_BLOCK, SEL_BLOCK) - max((k - COV_LEAD) * CMP_STRIDE, 0), 0) / CMP_BLOCK
    for k in range(COV_RATIO + COV_LEAD))
assert SEL_BLOCK % CMP_STRIDE == 0 and CMP_BLOCK % CMP_STRIDE == 0 and COV_LEAD <= SUBLANES
SEL_TILE = 512
SEL_BPT = SEL_TILE // SEL_BLOCK
SEL_LANES = 256
SEL_UNROLLS = (4, 2)
NSA_STEP_BLOCKS = SEL_TILE // QUERY_BLOCK
NSA_BUCKETS = 8
BF16_ROWS = 16
FFN_CHUNK = 256
PROJ_TM, SGU_TM, MIX_TM, FFN_TM, FFN_TH = 1024, 512, 256, 512, 512
VMEM_LIMIT = 62 * 1024 * 1024


def _params(*sem):
    return pltpu.CompilerParams(dimension_semantics=sem, vmem_limit_bytes=VMEM_LIMIT)


def _rms(x, g):
    return x * lax.rsqrt(jnp.mean(x * x, axis=-1, keepdims=True) + NORM_EPS) * g


def _dot(a, b):
    return jnp.dot(a, b, preferred_element_type=F32)


def _ones_row(n):
    return jnp.where(lax.broadcasted_iota(jnp.int32, (BF16_ROWS, n), 0) == 0, 1.0, 0.0).astype(BF16)


def _norm_matmul_kernel(x_ref, g_ref, w_ref, o_ref, h_ref):
    @pl.when(pl.program_id(1) == 0)
    def _():
        h_ref[...] = _rms(x_ref[...], g_ref[...]).astype(BF16)

    o_ref[...] = _dot(h_ref[...], w_ref[...]).astype(o_ref.dtype)


def _norm_matmul(x, g, w, out_dtype, tm, tn):
    s, d = x.shape
    n = w.shape[1]
    return pl.pallas_call(
        _norm_matmul_kernel,
        grid=(s // tm, n // tn),
        in_specs=[
            pl.BlockSpec((tm, d), lambda i, j: (i, 0)),
            pl.BlockSpec((1, d), lambda i, j: (0, 0)),
            pl.BlockSpec((d, tn), lambda i, j: (0, j)),
        ],
        out_specs=pl.BlockSpec((tm, tn), lambda i, j: (i, j)),
        out_shape=jax.ShapeDtypeStruct((s, n), out_dtype),
        scratch_shapes=[pltpu.VMEM((tm, d), BF16)],
        compiler_params=_params("parallel", "arbitrary"),
        name="norm_matmul",
    )(x, g.reshape(1, d), w)


PROJ_TN = 512


def _proj_kernel(x_ref, g_ref, wa_ref, wb_ref, nat_ref, kw_ref, ksa_ref, vsa_ref, vwa_ref, q_ref, o_ref, gt_ref,
                 h_ref, stage_ref):
    i = pl.program_id(0)
    j = pl.program_id(1)
    tm = x_ref.shape[0]
    dh = NSA_DIM
    ng = NSA_GROUPS
    tn = PROJ_TN
    qw = q_ref.shape[1]

    @pl.when(j == 0)
    def _():
        h_ref[...] = _rms(x_ref[...], g_ref[...]).astype(BF16)
        h = h_ref[...]

        def tile(t):
            return _dot(h, wa_ref[:, qw + t * tn:qw + (t + 1) * tn])

        def cols(res, c):
            return res[:, c * dh:(c + 1) * dh].astype(BF16)

        def cols_t(res, c):
            slab = res[:, (c // 2) * 2 * dh:(c // 2 + 1) * 2 * dh].T
            return slab[(c % 2) * dh:(c % 2 + 1) * dh].astype(BF16)

        res = tile(0)
        for c in range(2 * ng):
            stage_ref[...] = res[:, c * dh:(c + 1) * dh]
            for t in range(0, CMP_STRIDE, 2):
                pair = [stage_ref[pl.ds(t + u, tm // CMP_STRIDE, stride=CMP_STRIDE), :] for u in range(2)]
                nat_ref[c // ng, c % ng, :, t * dh:(t + 2) * dh] = jnp.concatenate(pair, axis=1).astype(BF16)

        res = tile(1)
        pos = i * tm + lax.broadcasted_iota(jnp.int32, (tm, dh), 0)
        lane = lax.broadcasted_iota(jnp.int32, (tm, dh), 1)
        onehot = jnp.where((pos // SEL_BLOCK) % SEL_BPT == lane, 1.0, 0.0).astype(BF16)
        for g in range(ng):
            ksa_ref[g, :, :dh] = cols(res, g)
            ksa_ref[g, :, dh:] = onehot
            vsa_ref[g, :dh, :] = cols_t(res, ng + g)
            vsa_ref[g, dh:, :] = _ones_row(tm)

        res = tile(2)
        for g in range(ng):
            kw_ref[g] = cols(res, g)
            vwa_ref[g, :dh, :] = cols_t(res, ng + g)
            vwa_ref[g, dh:, :] = _ones_row(tm)

        for t in range(qw // tn):
            q_ref[:, t * tn:(t + 1) * tn] = _dot(h, wa_ref[:, t * tn:(t + 1) * tn]).astype(BF16)

    @pl.when(j >= 1)
    def _():
        res = _dot(h_ref[...], wb_ref[...])
        o_ref[...] = res

        @pl.when(j == 1)
        def _():
            gt_ref[...] = res[:, :LANES].T


def _proj(x, g, w_a, w_b, tm):
    s, d = x.shape
    dh, ng, tn = NSA_DIM, NSA_GROUPS, PROJ_TN
    qw = NSA_HEADS * dh
    assert 2 * ng * dh == tn and qw % tn == 0 and w_a.shape[1] == qw + N_KV_STREAMS // 2 * tn
    n = w_b.shape[1] - tn
    nb = n // tn
    shapes = [
        jax.ShapeDtypeStruct((2, ng, s // CMP_STRIDE, CMP_STRIDE * dh), BF16),
        jax.ShapeDtypeStruct((ng, s, dh), BF16),
        jax.ShapeDtypeStruct((ng, s, 2 * dh), BF16),
        jax.ShapeDtypeStruct((ng, dh + BF16_ROWS, s), BF16),
        jax.ShapeDtypeStruct((ng, dh + BF16_ROWS, s), BF16),
        jax.ShapeDtypeStruct((s, NSA_HEADS * dh), BF16),
        jax.ShapeDtypeStruct((s, n), F32),
        jax.ShapeDtypeStruct((LANES, s), F32),
    ]
    return pl.pallas_call(
        _proj_kernel,
        grid=(s // tm, 2 + nb),
        in_specs=[
            pl.BlockSpec((tm, d), lambda i, j: (i, 0)),
            pl.BlockSpec((1, d), lambda i, j: (0, 0)),
            pl.BlockSpec(w_a.shape, lambda i, j: (0, 0), pipeline_mode=pl.Buffered(1)),
            pl.BlockSpec((d, tn), lambda i, j: (0, jnp.where(j <= 1, nb, j - 2))),
        ],
        out_specs=[
            pl.BlockSpec((2, ng, tm // CMP_STRIDE, CMP_STRIDE * dh), lambda i, j: (0, 0, i, 0)),
            pl.BlockSpec((ng, tm, dh), lambda i, j: (0, i, 0)),
            pl.BlockSpec((ng, tm, 2 * dh), lambda i, j: (0, i, 0)),
            pl.BlockSpec((ng, dh + BF16_ROWS, tm), lambda i, j: (0, 0, i)),
            pl.BlockSpec((ng, dh + BF16_ROWS, tm), lambda i, j: (0, 0, i)),
            pl.BlockSpec((tm, qw), lambda i, j: (i, 0)),
            pl.BlockSpec((tm, tn), lambda i, j: (i, jnp.maximum(j - 2, 0))),
            pl.BlockSpec((LANES, tm), lambda i, j: (0, i)),
        ],
        out_shape=shapes,
        scratch_shapes=[pltpu.VMEM((tm, d), BF16), pltpu.VMEM((tm, dh), F32)],
        compiler_params=_params("parallel", "arbitrary"),
        name="proj",
    )(x, g.reshape(1, d), w_a, w_b)


def _compress_kernel(r_ref, w1_ref, pe_ref, b1_ref, w2_ref, o_ref, ot_ref):
    half = r_ref.shape[-1]
    r = r_ref[0, 0]
    bias = _dot(pe_ref[0].astype(BF16), w1_ref[0]) + b1_ref[0]
    top = _dot(r, w1_ref[0, :half, :])
    bot = _dot(r, w1_ref[0, half:, :])
    ncp = r.shape[0]
    hid = top + pltpu.roll(bot, ncp - 1, 0) + bias
    out = _dot(jax.nn.gelu(hid).astype(BF16), w2_ref[0])
    o_ref[0, 0] = out.astype(o_ref.dtype)
    dh = out.shape[1]
    out_t = jnp.concatenate([out, jnp.zeros((ncp, LANES - dh), F32)], axis=1).T
    ot_ref[0, 0, :dh, :] = out_t[:dh].astype(ot_ref.dtype)
    ot_ref[0, 0, dh:, :] = _ones_row(ncp)


def _compress(r, w1, pe, b1, w2):
    _, g, ncp, half = r.shape
    hidden = w1.shape[-1]
    dh = w2.shape[-1]
    return pl.pallas_call(
        _compress_kernel,
        grid=(2, g),
        in_specs=[
            pl.BlockSpec((1, 1, ncp, half), lambda s, gg: (s, gg, 0, 0)),
            pl.BlockSpec((1, 2 * half, hidden), lambda s, gg: (s, 0, 0)),
            pl.BlockSpec((1, 1, 2 * half), lambda s, gg: (s, 0, 0)),
            pl.BlockSpec((1, 1, hidden), lambda s, gg: (s, 0, 0)),
            pl.BlockSpec((1, hidden, dh), lambda s, gg: (s, 0, 0)),
        ],
        out_specs=[pl.BlockSpec((1, 1, ncp, dh), lambda s, gg: (s, gg, 0, 0)),
                   pl.BlockSpec((1, 1, dh + BF16_ROWS, ncp), lambda s, gg: (s, gg, 0, 0))],
        out_shape=[jax.ShapeDtypeStruct((2, g, ncp, dh), BF16),
                   jax.ShapeDtypeStruct((2, g, dh + BF16_ROWS, ncp), BF16)],
        compiler_params=_params("parallel", "parallel"),
        name="compress",
    )(r, w1, pe, b1, w2)


PICKED = -2.0


def _nsa_kernel(q_ref, gt_ref, kc_ref, vct_ref, ks_ref, vst_ref, kw_ref, vwt_ref, o_ref,
                qa_ref, sel_ref, part_ref, gates_ref, s0_ref, s1_ref, acc_ref, blkf_ref, cend_ref, wrel_ref, pp_ref):
    g = pl.program_id(0)
    step = pl.program_id(1)
    qb = QUERY_BLOCK
    hq = NSA_HPG * qb
    nb = sel_ref.shape[1]
    ncp = kc_ref.shape[1]
    s_len = ks_ref.shape[1]
    first = step * NSA_STEP_BLOCKS

    @pl.when(step == 0)
    def _():
        lane = lax.broadcasted_iota(jnp.int32, cend_ref.shape, 1)
        cend_ref[...] = lax.broadcasted_iota(jnp.int32, cend_ref.shape, 0) * CMP_STRIDE + (CMP_BLOCK - 1) - lane
        wrel_ref[...] = (lax.broadcasted_iota(jnp.int32, wrel_ref.shape, 0)
                         - lax.broadcasted_iota(jnp.int32, wrel_ref.shape, 1))

    def rows(sub):
        return pl.ds(pl.multiple_of(sub * qb, qb), qb)

    bucket_ok = s_len % (NSA_BUCKETS * SEL_TILE) == 0 and nb // NSA_BUCKETS >= SEL_TOPK
    n_bucket = NSA_BUCKETS if bucket_ok else 1
    bucket = ((first + NSA_STEP_BLOCKS) * qb - 1) // (s_len // n_bucket)
    for b in range(n_bucket):
        nck, nbk = (b + 1) * (ncp // n_bucket), (b + 1) * (nb // n_bucket)

        @pl.when(bucket == b)
        def _():
            def scores(sub, carry):
                own = pl.ds(pl.multiple_of(sub * hq, hq), hq)
                _nsa_scores(g, first + sub, q_ref.at[rows(sub), :], gt_ref.at[:, rows(sub)], kc_ref, vct_ref, kw_ref,
                            vwt_ref, qa_ref.at[:, own], sel_ref.at[sub], part_ref.at[sub], gates_ref.at[sub], pp_ref,
                            cend_ref, wrel_ref, nck, nbk)
                return carry

            lax.fori_loop(0, NSA_STEP_BLOCKS, scores, 0)
            _nsa_topk(first, sel_ref, blkf_ref, nbk)

    _nsa_select(g, step, ks_ref, vst_ref, o_ref, qa_ref, sel_ref, part_ref, gates_ref, s0_ref, s1_ref, acc_ref)


def _gate(gates_ref, g, branch):
    rows = [gates_ref[pl.ds((g * NSA_HPG + h) * N_BRANCH + branch, 1), :] for h in range(NSA_HPG)]
    return jax.nn.sigmoid(jnp.concatenate(rows, axis=1))


def _nsa_scores(g, i, q_ref, gt_ref, kc_ref, vct_ref, kw_ref, vwt_ref, qa_ref, score_ref, part_ref, gates_ref,
                pp_ref, cend_ref, wrel_ref, nck, nbk):
    qb = QUERY_BLOCK
    start = i * qb
    hq = NSA_HPG * qb
    t_row = start + lax.broadcasted_iota(jnp.int32, (1, qb), 1)
    jt = t_row // SEL_BLOCK

    qt = (q_ref[...].astype(F32) * (NSA_DIM ** -0.5 * LOG2E)).T
    q_t = jnp.concatenate([qt[h * NSA_DIM:(h + 1) * NSA_DIM] for h in range(NSA_HPG)], axis=1).astype(BF16)
    qa_ref[:NSA_DIM, :] = q_t
    qa_ref[NSA_DIM:, :] = jnp.zeros((qa_ref.shape[0] - NSA_DIM, hq), BF16)
    gates_ref[...] = gt_ref[...]

    def exps(s, bias):
        out = []
        for h in range(NSA_HPG):
            sh = s[:, h * qb:(h + 1) * qb] + bias
            out.append(jnp.exp2(sh - jnp.max(sh, axis=0, keepdims=True)))
        return out

    wk = WINDOW + qb
    k0w = pl.multiple_of(jnp.maximum(start - WINDOW, 0), qb)
    sc = _dot(kc_ref[0, :nck, :], q_t)
    sw = _dot(kw_ref[0, pl.ds(k0w, wk), :], q_t)
    e_cmp = exps(sc, jnp.where(cend_ref[:nck, :] <= start, 0.0, MASK_VALUE))
    o_cmp = _dot(vct_ref[0, :, :nck], jnp.concatenate([e.astype(BF16) for e in e_cmp], axis=1))
    wrel = wrel_ref[...]
    e_win = exps(sw, jnp.where((wrel <= start - k0w) & (wrel > start - k0w - WINDOW), 0.0, MASK_VALUE))
    o_win = _dot(vwt_ref[0, :, pl.ds(k0w, wk)], jnp.concatenate([e.astype(BF16) for e in e_win], axis=1))
    inv_cmp = jnp.where(jnp.concatenate([t_row >= CMP_BLOCK - 1] * NSA_HPG, axis=1),
                        1.0 / o_cmp[NSA_DIM:NSA_DIM + 1, :], 0.0)
    psum = e_cmp[0] * inv_cmp[:, :qb]
    for h in range(1, NSA_HPG):
        psum = psum + e_cmp[h] * inv_cmp[:, h * qb:(h + 1) * qb]
    part_ref[...] = (_gate(gates_ref, g, 0) * inv_cmp * o_cmp[:NSA_DIM]
                     + _gate(gates_ref, g, 2) * (1.0 / o_win[NSA_DIM:NSA_DIM + 1, :]) * o_win[:NSA_DIM])
    pp_ref[:SUBLANES, :] = jnp.zeros((SUBLANES, qb), F32)
    pp_ref[SUBLANES:SUBLANES + nck, :] = psum
    imp = None
    for k, w in enumerate(COV_BAND):
        tap = pp_ref[pl.ds(SUBLANES - COV_LEAD + k, nbk, stride=COV_RATIO), :]
        tap = tap if w == 1.0 else w * tap
        imp = tap if imp is None else imp + tap
    blk = lax.broadcasted_iota(jnp.int32, (nbk, qb), 0)
    forced = (blk == 0) | (blk == jt) | (blk == jt - 1)
    score_ref[:nbk, :] = jnp.where(forced, PICKED, jnp.where(blk <= jt, imp, -1.0))


def _nsa_topk(first, sel_ref, blkf_ref, nbk):
    qb = QUERY_BLOCK
    blk = lax.broadcasted_iota(jnp.int32, (nbk, qb), 0)
    blkf_ref[:nbk, :] = blk.astype(F32)

    def pick(_, scores):
        out = []
        for score in scores:
            mx = jnp.max(score, axis=0, keepdims=True)
            cand = jnp.where(score == mx, blkf_ref[:nbk, :], float(nbk))
            out.append(jnp.where(cand == jnp.min(cand, axis=0, keepdims=True), PICKED, score))
        return tuple(out)

    scores = tuple(sel_ref[b, :nbk, :] for b in range(sel_ref.shape[0]))
    scores = lax.fori_loop(0, min(SEL_TOPK, nbk) - N_FORCED, pick, scores)
    for b, score in enumerate(scores):
        jt = ((first + b) * qb + lax.broadcasted_iota(jnp.int32, (1, qb), 1)) // SEL_BLOCK
        sel_ref[b, :nbk, :] = jnp.where((score == PICKED) & (blk <= jt), 0.0, MASK_VALUE)


def _nsa_select(g, step, ks_ref, vst_ref, o_ref, qa_ref, sel_ref, part_ref, gates_ref, s0_ref, s1_ref, acc_ref):
    s_refs = (s0_ref, s1_ref)
    qb = QUERY_BLOCK
    hq = NSA_HPG * qb
    nbq = sel_ref.shape[0]
    lanes = nbq * hq
    tk = SEL_TILE

    def bias_rows(kt):
        rows = []
        for b in range(nbq):
            b8 = sel_ref[b, pl.ds(pl.multiple_of(kt * SEL_BPT, SEL_BPT), SEL_BPT), :]
            b16 = jnp.concatenate([b8, jnp.zeros((BF16_ROWS - SEL_BPT, qb), F32)], axis=0).astype(BF16)
            rows += [b16] * NSA_HPG
        qa_ref[NSA_DIM:NSA_DIM + BF16_ROWS, :] = jnp.concatenate(rows, axis=1)

    def scores(slot, kt):
        k0 = pl.multiple_of(kt * tk, tk)
        bias_rows(kt)
        s = _dot(ks_ref[0, pl.ds(k0, tk), :], qa_ref[...])
        s_refs[slot][...] = s
        return jnp.max(s, axis=0, keepdims=True)

    def update(slot, kt, mt, m, causal):
        k0 = pl.multiple_of(kt * tk, tk)
        if causal:
            tri = jnp.where(lax.broadcasted_iota(jnp.int32, (qb, qb), 0)
                            <= lax.broadcasted_iota(jnp.int32, (qb, qb), 1), 0.0, MASK_VALUE)
            for b in range(nbq):
                own = (slice(b * qb, (b + 1) * qb), slice(b * hq, (b + 1) * hq))
                s_refs[slot][own] = s_refs[slot][own] + jnp.concatenate([tri] * NSA_HPG, axis=1)
            mt = jnp.max(s_refs[slot][...], axis=0, keepdims=True)
        s = s_refs[slot][...]
        m_new = jnp.maximum(m, mt)
        p = jnp.exp2(s - m_new).astype(BF16)
        acc_ref[...] = jnp.exp2(m - m_new) * acc_ref[...] + _dot(vst_ref[0, :, pl.ds(k0, tk)], p)
        return m_new

    def accumulate(pend):
        p, k0, alpha, ln = pend
        acc_ref[:, ln] = alpha * acc_ref[:, ln] + _dot(vst_ref[0, :, pl.ds(k0, tk)], p)

    def run(first, n_tiles, carry):
        mt, m = carry
        pend = None
        for k in range(n_tiles):
            cur, nxt = s_refs[k % 2], s_refs[(k + 1) % 2]
            k0c = pl.multiple_of((first + k) * tk, tk)
            k0n = pl.multiple_of((first + k + 1) * tk, tk)
            bias_rows(first + k + 1)
            m_new = jnp.maximum(m, mt)
            alpha = jnp.exp2(m - m_new)
            mts = []
            for n in range(lanes // SEL_LANES):
                ln = slice(n * SEL_LANES, (n + 1) * SEL_LANES)
                s_n = _dot(ks_ref[0, pl.ds(k0n, tk), :], qa_ref[:, ln])
                nxt[:, ln] = s_n
                mts.append(jnp.max(s_n, axis=0, keepdims=True))
                p = jnp.exp2(cur[:, ln] - m_new[:, ln]).astype(BF16)
                if pend is not None:
                    accumulate(pend)
                pend = (p, k0c, alpha[:, ln], ln)
            mt = jnp.concatenate(mts, axis=1)
            m = m_new
        accumulate(pend)
        return mt, m

    acc_ref[...] = jnp.zeros(acc_ref.shape, F32)
    last = step
    carry = (scores(0, 0), jnp.full((1, lanes), MASK_VALUE, F32))
    done = 0
    for unroll in SEL_UNROLLS:
        trips = (last - done) // unroll
        carry = lax.fori_loop(0, trips, lambda it, c, d=done, u=unroll: run(d + it * u, u, c), carry)
        done = done + trips * unroll
    mt_a, m = carry

    @pl.when(last % 2 == 0)
    def _():
        update(0, last, mt_a, m, True)

    @pl.when(last % 2 == 1)
    def _():
        mt_b, m_b = run(last - 1, 1, (mt_a, m))
        update(1, last, mt_b, m_b, True)

    for b in range(nbq):
        own = slice(b * hq, (b + 1) * hq)
        o_sel = acc_ref[:NSA_DIM, own] * (1.0 / acc_ref[NSA_DIM:NSA_DIM + 1, own])
        out_t = part_ref[b] + _gate(gates_ref.at[b], g, 1) * o_sel
        outs = [out_t[:, h * qb:(h + 1) * qb] for h in range(NSA_HPG)]
        o_ref[b * qb:(b + 1) * qb, :] = jnp.concatenate(outs, axis=0).T.astype(o_ref.dtype)


def _nsa(q, gates_t, kc, vc_aug_t, ks_aug, vs_aug_t, kw, vw_aug_t):
    s = q.shape[0]
    assert s % (2 * SEL_TILE) == 0 and s >= WINDOW + QUERY_BLOCK
    gdim = NSA_HPG * NSA_DIM
    ncp = kc.shape[1]
    assert ncp * CMP_STRIDE == s and s // SEL_BLOCK * COV_RATIO == ncp
    nb = s // SEL_BLOCK
    ka = ks_aug.shape[-1]
    va = vs_aug_t.shape[1]
    hq = NSA_HPG * QUERY_BLOCK
    per_group = lambda shape: pl.BlockSpec((1,) + shape, lambda g, i: (g, 0, 0))
    rows = NSA_STEP_BLOCKS * QUERY_BLOCK
    assert s % rows == 0
    return pl.pallas_call(
        _nsa_kernel,
        grid=(NSA_GROUPS, s // rows),
        in_specs=[
            pl.BlockSpec((rows, gdim), lambda g, i: (i, g)),
            pl.BlockSpec((GATE_ROWS, rows), lambda g, i: (0, i)),
            per_group((ncp, NSA_DIM)),
            per_group((va, ncp)),
            per_group((s, ka)),
            per_group((va, s)),
            per_group((s, NSA_DIM)),
            per_group((va, s)),
        ],
        out_specs=pl.BlockSpec((rows, gdim), lambda g, i: (i, g)),
        out_shape=jax.ShapeDtypeStruct((s, NSA_HEADS * NSA_DIM), BF16),
        scratch_shapes=[pltpu.VMEM((ka, NSA_STEP_BLOCKS * hq), BF16),
                        pltpu.VMEM((NSA_STEP_BLOCKS, nb, QUERY_BLOCK), F32),
                        pltpu.VMEM((NSA_STEP_BLOCKS, NSA_DIM, hq), F32),
                        pltpu.VMEM((NSA_STEP_BLOCKS, GATE_ROWS, QUERY_BLOCK), F32),
                        pltpu.VMEM((SEL_TILE, NSA_STEP_BLOCKS * hq), F32),
                        pltpu.VMEM((SEL_TILE, NSA_STEP_BLOCKS * hq), F32),
                        pltpu.VMEM((va, NSA_STEP_BLOCKS * hq), F32),
                        pltpu.VMEM((nb, QUERY_BLOCK), F32),
                        pltpu.VMEM((ncp, QUERY_BLOCK), jnp.int32),
                        pltpu.VMEM((WINDOW + QUERY_BLOCK, QUERY_BLOCK), jnp.int32),
                        pltpu.VMEM((SUBLANES + ncp, QUERY_BLOCK), F32)],
        compiler_params=_params("parallel", "arbitrary"),
        name="nsa",
    )(q, gates_t, kc, vc_aug_t, ks_aug, vs_aug_t, kw, vw_aug_t)


def _sgu_kernel(u_ref, v_ref, lng_ref, lnb_ref, ws_ref, bs_ref, o_ref):
    c = SGU_CHUNK
    tm = u_ref.shape[0]
    v = jax.nn.gelu(v_ref[...])
    mu = jnp.mean(v, axis=-1, keepdims=True)
    var = jnp.mean(jnp.square(v - mu), axis=-1, keepdims=True)
    vn = ((v - mu) * lax.rsqrt(var + NORM_EPS) * lng_ref[...] + lnb_ref[...]).astype(BF16)
    u = jax.nn.gelu(u_ref[...])
    tri = lax.broadcasted_iota(jnp.int32, (c, c), 0) >= lax.broadcasted_iota(jnp.int32, (c, c), 1)
    for g in range(SGU_GROUPS):
        w = jnp.where(tri, ws_ref[g], 0.0).astype(BF16)
        cols = slice(g * c, (g + 1) * c)
        rhs = jnp.concatenate([vn[k * c:(k + 1) * c, cols] for k in range(tm // c)], axis=1)
        mixed = _dot(w, rhs)
        for k in range(tm // c):
            rows = slice(k * c, (k + 1) * c)
            o_ref[rows, cols] = (u[rows, cols] * (mixed[:, rows] + bs_ref[g])).astype(o_ref.dtype)


def _sgu(proj, lng, lnb, ws, bs, tm):
    s = proj.shape[0]
    w = lng.shape[0]
    c = SGU_CHUNK
    bs_b = jnp.broadcast_to(bs[:, :, None], (SGU_GROUPS, c, c))
    return pl.pallas_call(
        _sgu_kernel,
        grid=(s // tm,),
        in_specs=[
            pl.BlockSpec((tm, w), lambda i: (i, 0)),
            pl.BlockSpec((tm, w), lambda i: (i, 1)),
            pl.BlockSpec((1, w), lambda i: (0, 0)),
            pl.BlockSpec((1, w), lambda i: (0, 0)),
            pl.BlockSpec((SGU_GROUPS, c, c), lambda i: (0, 0, 0)),
            pl.BlockSpec((SGU_GROUPS, c, c), lambda i: (0, 0, 0)),
        ],
        out_specs=pl.BlockSpec((tm, w), lambda i: (i, 0)),
        out_shape=jax.ShapeDtypeStruct((s, w), BF16),
        compiler_params=_params("parallel"),
        name="sgu",
    )(proj, proj, lng.reshape(1, w), lnb.reshape(1, w), ws, bs_b)


def _mix_kernel(oa_ref, ob_ref, ga_ref, gb_ref, x_ref, pa_ref, pb_ref, wo_ref,
                gc_ref, wq_ref, mkt_ref, mv_ref, wmo_ref, o_ref):
    a = _dot(oa_ref[...], pa_ref[...])
    b = _dot(ob_ref[...], pb_ref[...])
    merged = jax.nn.sigmoid(ga_ref[...]) * a + jax.nn.sigmoid(gb_ref[...]) * b
    x = x_ref[...] + _dot(merged.astype(BF16), wo_ref[...])
    h = _rms(x, gc_ref[...]).astype(BF16)
    mq = (_dot(h, wq_ref[...]) * (MEM_DIM ** -0.5)).astype(BF16)
    outs = []
    for hh in range(MEM_HEADS):
        s = _dot(mq[:, hh * MEM_DIM:(hh + 1) * MEM_DIM], mkt_ref[hh])
        e = jnp.exp(s - jnp.max(s, axis=-1, keepdims=True))
        p = e / jnp.sum(e, axis=-1, keepdims=True)
        outs.append(_dot(p.astype(BF16), mv_ref[hh]).astype(BF16))
    o_ref[...] = x + _dot(jnp.concatenate(outs, axis=1), wmo_ref[...])


def _mix(o_a, o_b, proj, x, p_a, p_b, w_o, gate_block, g_cross, w_q, mk_t, mv, w_mo, tm):
    s, d = x.shape
    wa = o_a.shape[1]
    wb = o_b.shape[1]
    mw = w_q.shape[1]
    m = mv.shape[1]
    resident = lambda shape: pl.BlockSpec(shape, lambda i: (0,) * len(shape), pipeline_mode=pl.Buffered(1))
    return pl.pallas_call(
        _mix_kernel,
        grid=(s // tm,),
        in_specs=[
            pl.BlockSpec((tm, wa), lambda i: (i, 0)),
            pl.BlockSpec((tm, wb), lambda i: (i, 0)),
            pl.BlockSpec((tm, d), lambda i: (i, gate_block)),
            pl.BlockSpec((tm, d), lambda i: (i, gate_block + 1)),
            pl.BlockSpec((tm, d), lambda i: (i, 0)),
            resident((wa, d)),
            resident((wb, d)),
            resident((d, d)),
            resident((1, d)),
            resident((d, mw)),
            resident((MEM_HEADS, MEM_DIM, m)),
            resident((MEM_HEADS, m, MEM_DIM)),
            resident((mw, d)),
        ],
        out_specs=pl.BlockSpec((tm, d), lambda i: (i, 0)),
        out_shape=jax.ShapeDtypeStruct((s, d), F32),
        compiler_params=_params("parallel"),
        name="mix",
    )(o_a, o_b, proj, proj, x, p_a, p_b, w_o, g_cross.reshape(1, d), w_q, mk_t, mv, w_mo)


def _ffn_kernel(x_ref, g_ref, wg_ref, wu_ref, wo_ref, gf_ref, o_ref, h_ref, acc_ref, *, final_norm):
    j = pl.program_id(1)

    @pl.when(j == 0)
    def _():
        h_ref[...] = _rms(x_ref[...], g_ref[...]).astype(BF16)
        acc_ref[...] = jnp.zeros_like(acc_ref)

    h = h_ref[...]
    th = wg_ref.shape[1]
    out = None
    for c in range(th // FFN_CHUNK):
        cs = slice(c * FFN_CHUNK, (c + 1) * FFN_CHUNK)
        act = (jax.nn.silu(_dot(h, wg_ref[:, cs])) * _dot(h, wu_ref[:, cs])).astype(BF16)
        part = _dot(act, wo_ref[cs, :])
        out = part if out is None else out + part
    acc_ref[...] += out

    @pl.when(j == pl.num_programs(1) - 1)
    def _():
        y = x_ref[...] + acc_ref[...]
        o_ref[...] = _rms(y, gf_ref[...]) if final_norm else y


def _ffn(x, g, w_in, w_out, g_final, final_norm, tm, th):
    s, d = x.shape
    hidden = w_out.shape[0]
    nh = hidden // th
    return pl.pallas_call(
        functools.partial(_ffn_kernel, final_norm=final_norm),
        grid=(s // tm, nh),
        in_specs=[
            pl.BlockSpec((tm, d), lambda i, j: (i, 0)),
            pl.BlockSpec((1, d), lambda i, j: (0, 0)),
            pl.BlockSpec((d, th), lambda i, j: (0, j)),
            pl.BlockSpec((d, th), lambda i, j: (0, j + nh)),
            pl.BlockSpec((th, d), lambda i, j: (j, 0)),
            pl.BlockSpec((1, d), lambda i, j: (0, 0)),
        ],
        out_specs=pl.BlockSpec((tm, d), lambda i, j: (i, 0)),
        out_shape=jax.ShapeDtypeStruct((s, d), F32),
        scratch_shapes=[pltpu.VMEM((tm, d), BF16), pltpu.VMEM((tm, d), F32)],
        compiler_params=_params("parallel", "arbitrary"),
        name="ffn",
    )(x, g.reshape(1, d), w_in, w_in, w_out, g_final.reshape(1, d))


def _layer(x, mem, norm_mix_g, w_in, cmp_pe_k, cmp_k_w1, cmp_k_b1, cmp_k_w2, cmp_pe_v, cmp_v_w1, cmp_v_b1, cmp_v_w2,
           sgu_ln_g, sgu_ln_b, sgu_ws, sgu_b, w_proj_a, w_proj_b, w_mix_out, norm_cross_g, norm_mem_g,
           w_mq, w_mkv, w_mo, norm_ffn_g):
    s, d = x.shape
    qw = NSA_HEADS * NSA_DIM
    kvw = NSA_GROUPS * NSA_DIM
    sguw = sgu_ln_g.shape[0]
    ngate = GATE_ROWS
    o_kv = qw
    o_gate = o_kv + N_KV_STREAMS * kvw
    o_u = o_gate + ngate

    w_a = w_in[:, :o_gate].astype(BF16)
    w_b = jnp.pad(w_in[:, o_u:].astype(BF16), ((0, 0), (0, PROJ_TN)))
    w_b = lax.dynamic_update_slice(w_b, w_in[:, o_gate:o_u].astype(BF16), (0, w_in.shape[1] - o_u))
    nat, kw, ks_aug, vs_aug_t, vw_aug_t, q, proj_b, gates_t = _proj(x, norm_mix_g, w_a, w_b, min(PROJ_TM, s))

    w1 =jnp.stack([cmp_k_w1, cmp_v_w1]).astype(BF16)
    pe = jnp.stack([cmp_pe_k.reshape(1, -1), cmp_pe_v.reshape(1, -1)])
    b1 = jnp.stack([cmp_k_b1.reshape(1, -1), cmp_v_b1.reshape(1, -1)])
    w2 = jnp.stack([cmp_k_w2, cmp_v_w2]).astype(BF16)
    cmp, cmp_t = _compress(nat, w1, pe, b1, w2)
    o_a = _nsa(q, gates_t, cmp[0], cmp_t[1], ks_aug, vs_aug_t, kw, vw_aug_t)

    o_b = _sgu(proj_b, sgu_ln_g, sgu_ln_b, sgu_ws, sgu_b, min(SGU_TM, s))

    m = mem.shape[0]
    mw = MEM_HEADS * MEM_DIM
    mkv = _norm_matmul(mem, norm_mem_g, w_mkv.astype(BF16), F32, m, mw)
    mk_t = mkv[:, :mw].reshape(m, MEM_HEADS, MEM_DIM).transpose(1, 2, 0).astype(BF16)
    mv = mkv[:, mw:].reshape(m, MEM_HEADS, MEM_DIM).transpose(1, 0, 2).astype(BF16)
    return _mix(o_a, o_b, proj_b, x, w_proj_a.astype(BF16), w_proj_b.astype(BF16), w_mix_out.astype(BF16),
                (2 * sguw) // d, norm_cross_g, w_mq.astype(BF16), mk_t, mv, w_mo.astype(BF16), min(MIX_TM, s))


def kernel(x, mem, norm_mix_g, w_in, cmp_pe_k, cmp_k_w1, cmp_k_b1, cmp_k_w2, cmp_pe_v, cmp_v_w1, cmp_v_b1, cmp_v_w2, sgu_ln_g, sgu_ln_b, sgu_ws, sgu_b, w_proj_a, w_proj_b, w_mix_out, norm_cross_g, norm_mem_g, w_mq, w_mkv, w_mo, norm_ffn_g, w_ffn_in, w_ffn_out, norm_final_g):
    b, s, d = x.shape
    depth = w_in.shape[0]
    outs = []
    for bi in range(b):
        xb = x[bi]
        for l in range(depth):
            last = l == depth - 1
            xb = _layer(xb, mem[bi], norm_mix_g[l], w_in[l], cmp_pe_k[l], cmp_k_w1[l], cmp_k_b1[l], cmp_k_w2[l],
                        cmp_pe_v[l], cmp_v_w1[l], cmp_v_b1[l], cmp_v_w2[l], sgu_ln_g[l], sgu_ln_b[l], sgu_ws[l],
                        sgu_b[l], w_proj_a[l], w_proj_b[l], w_mix_out[l], norm_cross_g[l], norm_mem_g[l],
                        w_mq[l], w_mkv[l], w_mo[l], norm_ffn_g[l])
            xb = _ffn(xb, norm_ffn_g[l], w_ffn_in[l].astype(BF16), w_ffn_out[l].astype(BF16), norm_final_g, last,
                      min(FFN_TM, s), FFN_TH)
        outs.append(xb)
    return jnp.stack(outs)
```

```python
import functools
import math

import jax
import jax.numpy as jnp
from jax import lax
from jax.experimental import pallas as pl
from jax.experimental.pallas import tpu as pltpu

F32 = jnp.float32
BF16 = jnp.bfloat16

NORM_EPS = 1e-6
MASK_VALUE = -1e30
N_FORCED = 3
N_BRANCH = 3
LOG2E = math.log2(math.e)

NSA_HEADS = 16
NSA_GROUPS = 4
NSA_HPG = NSA_HEADS // NSA_GROUPS
NSA_DIM = 64
GATE_ROWS = NSA_HEADS * N_BRANCH
N_KV_STREAMS = 6
CMP_BLOCK = 32
CMP_STRIDE = 16
SEL_BLOCK = 64
SEL_TOPK = 16
WINDOW = 512
QUERY_BLOCK = 128
SGU_GROUPS = 8
SGU_CHUNK = 128
MEM_HEADS = 4
MEM_DIM = 128

LANES = 128
SUBLANES = 8
COV_RATIO = SEL_BLOCK // CMP_STRIDE
COV_LEAD = CMP_BLOCK // CMP_STRIDE - 1
COV_BAND = tuple(
    max(min((k - COV_LEAD) * CMP_STRIDE + CMP_BLOCK, SEL_BLOCK) - max((k - COV_LEAD) * CMP_STRIDE, 0), 0) / CMP_BLOCK
    for k in range(COV_RATIO + COV_LEAD))
assert SEL_BLOCK % CMP_STRIDE == 0 and CMP_BLOCK % CMP_STRIDE == 0 and COV_LEAD <= SUBLANES
SEL_TILE = 512
SEL_BPT = SEL_TILE // SEL_BLOCK
SEL_LANES = 256
SEL_UNROLLS = (4, 2)
NSA_STEP_BLOCKS = SEL_TILE // QUERY_BLOCK
NSA_BUCKETS = 8
BF16_ROWS = 16
FFN_CHUNK = 256
PROJ_TM, SGU_TM, MIX_TM, FFN_TM, FFN_TH = 1024, 512, 256, 512, 512
VMEM_LIMIT = 62 * 1024 * 1024


def _params(*sem):
    return pltpu.CompilerParams(dimension_semantics=sem, vmem_limit_bytes=VMEM_LIMIT)


def _rms(x, g):
    return x * lax.rsqrt(jnp.mean(x * x, axis=-1, keepdims=True) + NORM_EPS) * g


def _dot(a, b):
    return jnp.dot(a, b, preferred_element_type=F32)


def _ones_row(n):
    return jnp.where(lax.broadcasted_iota(jnp.int32, (BF16_ROWS, n), 0) == 0, 1.0, 0.0).astype(BF16)


def _norm_matmul_kernel(x_ref, g_ref, w_ref, o_ref, h_ref):
    @pl.when(pl.program_id(1) == 0)
    def _():
        h_ref[...] = _rms(x_ref[...], g_ref[...]).astype(BF16)

    o_ref[...] = _dot(h_ref[...], w_ref[...]).astype(o_ref.dtype)


def _norm_matmul(x, g, w, out_dtype, tm, tn):
    s, d = x.shape
    n = w.shape[1]
    return pl.pallas_call(
        _norm_matmul_kernel,
        grid=(s // tm, n // tn),
        in_specs=[
            pl.BlockSpec((tm, d), lambda i, j: (i, 0)),
            pl.BlockSpec((1, d), lambda i, j: (0, 0)),
            pl.BlockSpec((d, tn), lambda i, j: (0, j)),
        ],
        out_specs=pl.BlockSpec((tm, tn), lambda i, j: (i, j)),
        out_shape=jax.ShapeDtypeStruct((s, n), out_dtype),
        scratch_shapes=[pltpu.VMEM((tm, d), BF16)],
        compiler_params=_params("parallel", "arbitrary"),
        name="norm_matmul",
    )(x, g.reshape(1, d), w)


PROJ_TN = 512


def _proj_kernel(x_ref, g_ref, wa_ref, wb_ref, nat_ref, kwa_ref, ksa_ref, vsa_ref, vwa_ref, q_ref, o_ref, gt_ref,
                 h_ref, stage_ref):
    i = pl.program_id(0)
    j = pl.program_id(1)
    tm = x_ref.shape[0]
    dh = NSA_DIM
    ng = NSA_GROUPS
    tn = PROJ_TN
    qw = q_ref.shape[1]

    @pl.when(j == 0)
    def _():
        h_ref[...] = _rms(x_ref[...], g_ref[...]).astype(BF16)
        h = h_ref[...]

        def tile(t):
            return _dot(h, wa_ref[:, qw + t * tn:qw + (t + 1) * tn])

        def cols(res, c):
            return res[:, c * dh:(c + 1) * dh].astype(BF16)

        def cols_t(res, c):
            slab = res[:, (c // 2) * 2 * dh:(c // 2 + 1) * 2 * dh].T
            return slab[(c % 2) * dh:(c % 2 + 1) * dh].astype(BF16)

        res = tile(0)
        for c in range(2 * ng):
            stage_ref[...] = res[:, c * dh:(c + 1) * dh]
            for t in range(0, CMP_STRIDE, 2):
                pair = [stage_ref[pl.ds(t + u, tm // CMP_STRIDE, stride=CMP_STRIDE), :] for u in range(2)]
                nat_ref[c // ng, c % ng, :, t * dh:(t + 2) * dh] = jnp.concatenate(pair, axis=1).astype(BF16)

        res = tile(1)
        pos = i * tm + lax.broadcasted_iota(jnp.int32, (tm, dh), 0)
        lane = lax.broadcasted_iota(jnp.int32, (tm, dh), 1)
        onehot = jnp.where((pos // SEL_BLOCK) % SEL_BPT == lane, 1.0, 0.0).astype(BF16)
        for g in range(ng):
            ksa_ref[g, :, :dh] = cols(res, g)
            ksa_ref[g, :, dh:] = onehot
            vsa_ref[g, :dh, :] = cols_t(res, ng + g)
            vsa_ref[g, dh:, :] = _ones_row(tm)

        res = tile(2)
        onehot = jnp.where((pos // QUERY_BLOCK) % NSA_STEP_BLOCKS == lane, 1.0, 0.0).astype(BF16)
        for g in range(ng):
            kwa_ref[g, :, :dh] = cols(res, g)
            kwa_ref[g, :, dh:] = onehot
            vwa_ref[g, :dh, :] = cols_t(res, ng + g)
            vwa_ref[g, dh:, :] = _ones_row(tm)

        for t in range(qw // tn):
            q_ref[:, t * tn:(t + 1) * tn] = _dot(h, wa_ref[:, t * tn:(t + 1) * tn]).astype(BF16)

    @pl.when(j >= 1)
    def _():
        res = _dot(h_ref[...], wb_ref[...])
        o_ref[...] = res

        @pl.when(j == 1)
        def _():
            gt_ref[...] = res[:, :LANES].T


def _proj(x, g, w_a, w_b, tm):
    s, d = x.shape
    dh, ng, tn = NSA_DIM, NSA_GROUPS, PROJ_TN
    qw = NSA_HEADS * dh
    assert 2 * ng * dh == tn and qw % tn == 0 and w_a.shape[1] == qw + N_KV_STREAMS // 2 * tn
    n = w_b.shape[1] - tn
    nb = n // tn
    shapes = [
        jax.ShapeDtypeStruct((2, ng, s // CMP_STRIDE, CMP_STRIDE * dh), BF16),
        jax.ShapeDtypeStruct((ng, s, 2 * dh), BF16),
        jax.ShapeDtypeStruct((ng, s, 2 * dh), BF16),
        jax.ShapeDtypeStruct((ng, dh + BF16_ROWS, s), BF16),
        jax.ShapeDtypeStruct((ng, dh + BF16_ROWS, s), BF16),
        jax.ShapeDtypeStruct((s, NSA_HEADS * dh), BF16),
        jax.ShapeDtypeStruct((s, n), F32),
        jax.ShapeDtypeStruct((LANES, s), F32),
    ]
    return pl.pallas_call(
        _proj_kernel,
        grid=(s // tm, 2 + nb),
        in_specs=[
            pl.BlockSpec((tm, d), lambda i, j: (i, 0)),
            pl.BlockSpec((1, d), lambda i, j: (0, 0)),
            pl.BlockSpec(w_a.shape, lambda i, j: (0, 0), pipeline_mode=pl.Buffered(1)),
            pl.BlockSpec((d, tn), lambda i, j: (0, jnp.where(j <= 1, nb, j - 2))),
        ],
        out_specs=[
            pl.BlockSpec((2, ng, tm // CMP_STRIDE, CMP_STRIDE * dh), lambda i, j: (0, 0, i, 0)),
            pl.BlockSpec((ng, tm, 2 * dh), lambda i, j: (0, i, 0)),
            pl.BlockSpec((ng, tm, 2 * dh), lambda i, j: (0, i, 0)),
            pl.BlockSpec((ng, dh + BF16_ROWS, tm), lambda i, j: (0, 0, i)),
            pl.BlockSpec((ng, dh + BF16_ROWS, tm), lambda i, j: (0, 0, i)),
            pl.BlockSpec((tm, qw), lambda i, j: (i, 0)),
            pl.BlockSpec((tm, tn), lambda i, j: (i, jnp.maximum(j - 2, 0))),
            pl.BlockSpec((LANES, tm), lambda i, j: (0, i)),
        ],
        out_shape=shapes,
        scratch_shapes=[pltpu.VMEM((tm, d), BF16), pltpu.VMEM((tm, dh), F32)],
        compiler_params=_params("parallel", "arbitrary"),
        name="proj",
    )(x, g.reshape(1, d), w_a, w_b)


def _compress_kernel(r_ref, w1_ref, pe_ref, b1_ref, w2_ref, o_ref, ot_ref):
    half = r_ref.shape[-1]
    r = r_ref[0, 0]
    bias = _dot(pe_ref[0].astype(BF16), w1_ref[0]) + b1_ref[0]
    top = _dot(r, w1_ref[0, :half, :])
    bot = _dot(r, w1_ref[0, half:, :])
    ncp = r.shape[0]
    hid = top + pltpu.roll(bot, ncp - 1, 0) + bias
    out = _dot(jax.nn.gelu(hid).astype(BF16), w2_ref[0])
    o_ref[0, 0] = out.astype(o_ref.dtype)
    dh = out.shape[1]
    out_t = jnp.concatenate([out, jnp.zeros((ncp, LANES - dh), F32)], axis=1).T
    ot_ref[0, 0, :dh, :] = out_t[:dh].astype(ot_ref.dtype)
    ot_ref[0, 0, dh:, :] = _ones_row(ncp)


def _compress(r, w1, pe, b1, w2):
    _, g, ncp, half = r.shape
    hidden = w1.shape[-1]
    dh = w2.shape[-1]
    return pl.pallas_call(
        _compress_kernel,
        grid=(2, g),
        in_specs=[
            pl.BlockSpec((1, 1, ncp, half), lambda s, gg: (s, gg, 0, 0)),
            pl.BlockSpec((1, 2 * half, hidden), lambda s, gg: (s, 0, 0)),
            pl.BlockSpec((1, 1, 2 * half), lambda s, gg: (s, 0, 0)),
            pl.BlockSpec((1, 1, hidden), lambda s, gg: (s, 0, 0)),
            pl.BlockSpec((1, hidden, dh), lambda s, gg: (s, 0, 0)),
        ],
        out_specs=[pl.BlockSpec((1, 1, ncp, dh), lambda s, gg: (s, gg, 0, 0)),
                   pl.BlockSpec((1, 1, dh + BF16_ROWS, ncp), lambda s, gg: (s, gg, 0, 0))],
        out_shape=[jax.ShapeDtypeStruct((2, g, ncp, dh), BF16),
                   jax.ShapeDtypeStruct((2, g, dh + BF16_ROWS, ncp), BF16)],
        compiler_params=_params("parallel", "parallel"),
        name="compress",
    )(r, w1, pe, b1, w2)


PICKED = -2.0


def _nsa_kernel(q_ref, gt_ref, kc_ref, vct_ref, ks_ref, vst_ref, kw_ref, vwt_ref, o_ref,
                qa_ref, sel_ref, part_ref, gates_ref, s0_ref, s1_ref, acc_ref, blkf_ref, cend_ref, pp_ref):
    g = pl.program_id(0)
    step = pl.program_id(1)
    qb = QUERY_BLOCK
    hq = NSA_HPG * qb
    nb = sel_ref.shape[1]
    ncp = kc_ref.shape[1]
    s_len = ks_ref.shape[1]
    first = step * NSA_STEP_BLOCKS

    @pl.when(step == 0)
    def _():
        lane = lax.broadcasted_iota(jnp.int32, cend_ref.shape, 1)
        cend_ref[...] = lax.broadcasted_iota(jnp.int32, cend_ref.shape, 0) * CMP_STRIDE + (CMP_BLOCK - 1) - lane

    def rows(sub):
        return pl.ds(pl.multiple_of(sub * qb, qb), qb)

    bucket_ok = s_len % (NSA_BUCKETS * SEL_TILE) == 0 and nb // NSA_BUCKETS >= SEL_TOPK
    n_bucket = NSA_BUCKETS if bucket_ok else 1
    bucket = ((first + NSA_STEP_BLOCKS) * qb - 1) // (s_len // n_bucket)
    for b in range(n_bucket):
        nck, nbk = (b + 1) * (ncp // n_bucket), (b + 1) * (nb // n_bucket)

        @pl.when(bucket == b)
        def _():
            def scores(sub, carry):
                own = pl.ds(pl.multiple_of(sub * hq, hq), hq)
                _nsa_scores(g, first + sub, q_ref.at[rows(sub), :], gt_ref.at[:, rows(sub)], kc_ref, vct_ref,
                            qa_ref.at[:, own], sel_ref.at[sub], part_ref.at[sub], gates_ref.at[sub], pp_ref, cend_ref,
                            nck, nbk)
                return carry

            lax.fori_loop(0, NSA_STEP_BLOCKS, scores, 0)
            _nsa_topk(first, sel_ref, blkf_ref, nbk)

    _nsa_window(g, step, kw_ref, vwt_ref, qa_ref, part_ref, gates_ref, s0_ref, s1_ref, acc_ref)
    _nsa_select(g, step, ks_ref, vst_ref, o_ref, qa_ref, sel_ref, part_ref, gates_ref, s0_ref, s1_ref, acc_ref)


def _gate(gates_ref, g, branch):
    rows = [gates_ref[pl.ds((g * NSA_HPG + h) * N_BRANCH + branch, 1), :] for h in range(NSA_HPG)]
    return jax.nn.sigmoid(jnp.concatenate(rows, axis=1))


def _nsa_scores(g, i, q_ref, gt_ref, kc_ref, vct_ref, qa_ref, score_ref, part_ref, gates_ref, pp_ref, cend_ref,
                nck, nbk):
    qb = QUERY_BLOCK
    start = i * qb
    hq = NSA_HPG * qb
    t_row = start + lax.broadcasted_iota(jnp.int32, (1, qb), 1)
    jt = t_row // SEL_BLOCK

    qt = (q_ref[...].astype(F32) * (NSA_DIM ** -0.5 * LOG2E)).T
    q_t = jnp.concatenate([qt[h * NSA_DIM:(h + 1) * NSA_DIM] for h in range(NSA_HPG)], axis=1).astype(BF16)
    qa_ref[:NSA_DIM, :] = q_t
    qa_ref[NSA_DIM:, :] = jnp.zeros((qa_ref.shape[0] - NSA_DIM, hq), BF16)
    gates_ref[...] = gt_ref[...]

    sc = _dot(kc_ref[0, :nck, :], q_t)
    bias = jnp.where(cend_ref[:nck, :] <= start, 0.0, MASK_VALUE)
    e_cmp = []
    for h in range(NSA_HPG):
        sh = sc[:, h * qb:(h + 1) * qb] + bias
        e_cmp.append(jnp.exp2(sh - jnp.max(sh, axis=0, keepdims=True)))
    o_cmp = _dot(vct_ref[0, :, :nck], jnp.concatenate([e.astype(BF16) for e in e_cmp], axis=1))
    inv_cmp = jnp.where(jnp.concatenate([t_row >= CMP_BLOCK - 1] * NSA_HPG, axis=1),
                        1.0 / o_cmp[NSA_DIM:NSA_DIM + 1, :], 0.0)
    psum = e_cmp[0] * inv_cmp[:, :qb]
    for h in range(1, NSA_HPG):
        psum = psum + e_cmp[h] * inv_cmp[:, h * qb:(h + 1) * qb]
    part_ref[...] = _gate(gates_ref, g, 0) * inv_cmp * o_cmp[:NSA_DIM]
    pp_ref[:SUBLANES, :] = jnp.zeros((SUBLANES, qb), F32)
    pp_ref[SUBLANES:SUBLANES + nck, :] = psum
    imp = None
    for k, w in enumerate(COV_BAND):
        tap = pp_ref[pl.ds(SUBLANES - COV_LEAD + k, nbk, stride=COV_RATIO), :]
        tap = tap if w == 1.0 else w * tap
        imp = tap if imp is None else imp + tap
    blk = lax.broadcasted_iota(jnp.int32, (nbk, qb), 0)
    forced = (blk == 0) | (blk == jt) | (blk == jt - 1)
    score_ref[:nbk, :] = jnp.where(forced, PICKED, jnp.where(blk <= jt, imp, -1.0))


def _nsa_topk(first, sel_ref, blkf_ref, nbk):
    qb = QUERY_BLOCK
    blk = lax.broadcasted_iota(jnp.int32, (nbk, qb), 0)
    blkf_ref[:nbk, :] = blk.astype(F32)

    def pick(_, scores):
        out = []
        for score in scores:
            mx = jnp.max(score, axis=0, keepdims=True)
            cand = jnp.where(score == mx, blkf_ref[:nbk, :], float(nbk))
            out.append(jnp.where(cand == jnp.min(cand, axis=0, keepdims=True), PICKED, score))
        return tuple(out)

    scores = tuple(sel_ref[b, :nbk, :] for b in range(sel_ref.shape[0]))
    scores = lax.fori_loop(0, min(SEL_TOPK, nbk) - N_FORCED, pick, scores)
    for b, score in enumerate(scores):
        jt = ((first + b) * qb + lax.broadcasted_iota(jnp.int32, (1, qb), 1)) // SEL_BLOCK
        sel_ref[b, :nbk, :] = jnp.where((score == PICKED) & (blk <= jt), 0.0, MASK_VALUE)


def _nsa_window(g, step, kw_ref, vwt_ref, qa_ref, part_ref, gates_ref, s0_ref, s1_ref, acc_ref):
    qb = QUERY_BLOCK
    hq = NSA_HPG * qb
    nbq = part_ref.shape[0]
    lanes = nbq * hq
    tk = SEL_TILE
    key_blk = lax.broadcasted_iota(jnp.int32, (BF16_ROWS, lanes), 0)
    qry_blk = lax.broadcasted_iota(jnp.int32, (BF16_ROWS, lanes), 1) // hq
    r_k = lax.broadcasted_iota(jnp.int32, (qb, qb), 0)
    r_q = lax.broadcasted_iota(jnp.int32, (qb, qb), 1)

    def scores(s_ref, kt, keep_blocks, keep_own):
        qa_ref[NSA_DIM:NSA_DIM + BF16_ROWS, :] = jnp.where(keep_blocks, 0.0, MASK_VALUE).astype(BF16)
        s_ref[...] = _dot(kw_ref[0, pl.ds(pl.multiple_of(kt * tk, tk), tk), :], qa_ref[...])
        own_bias = jnp.concatenate([jnp.where(keep_own, 0.0, MASK_VALUE)] * NSA_HPG, axis=1)
        for b in range(nbq):
            own = (slice(b * qb, (b + 1) * qb), slice(b * hq, (b + 1) * hq))
            s_ref[own] = s_ref[own] + own_bias

    scores(s0_ref, jnp.maximum(step - 1, 0), (key_blk >= qry_blk) & (step > 0), r_k > r_q)
    scores(s1_ref, step, key_blk <= qry_blk, r_k <= r_q)
    m = jnp.maximum(jnp.max(s0_ref[...], axis=0, keepdims=True), jnp.max(s1_ref[...], axis=0, keepdims=True))
    k_prev = pl.multiple_of(jnp.maximum(step - 1, 0) * tk, tk)
    k_own = pl.multiple_of(step * tk, tk)
    acc_ref[...] = (_dot(vwt_ref[0, :, pl.ds(k_prev, tk)], jnp.exp2(s0_ref[...] - m).astype(BF16))
                    + _dot(vwt_ref[0, :, pl.ds(k_own, tk)], jnp.exp2(s1_ref[...] - m).astype(BF16)))
    for b in range(nbq):
        own = slice(b * hq, (b + 1) * hq)
        o_win = acc_ref[:NSA_DIM, own] * (1.0 / acc_ref[NSA_DIM:NSA_DIM + 1, own])
        part_ref[b] = part_ref[b] + _gate(gates_ref.at[b], g, 2) * o_win


def _nsa_select(g, step, ks_ref, vst_ref, o_ref, qa_ref, sel_ref, part_ref, gates_ref, s0_ref, s1_ref, acc_ref):
    s_refs = (s0_ref, s1_ref)
    qb = QUERY_BLOCK
    hq = NSA_HPG * qb
    nbq = sel_ref.shape[0]
    lanes = nbq * hq
    tk = SEL_TILE

    def bias_rows(kt):
        rows = []
        for b in range(nbq):
            b8 = sel_ref[b, pl.ds(pl.multiple_of(kt * SEL_BPT, SEL_BPT), SEL_BPT), :]
            b16 = jnp.concatenate([b8, jnp.zeros((BF16_ROWS - SEL_BPT, qb), F32)], axis=0).astype(BF16)
            rows += [b16] * NSA_HPG
        qa_ref[NSA_DIM:NSA_DIM + BF16_ROWS, :] = jnp.concatenate(rows, axis=1)

    def scores(slot, kt):
        k0 = pl.multiple_of(kt * tk, tk)
        bias_rows(kt)
        s = _dot(ks_ref[0, pl.ds(k0, tk), :], qa_ref[...])
        s_refs[slot][...] = s
        return jnp.max(s, axis=0, keepdims=True)

    def update(slot, kt, mt, m, causal):
        k0 = pl.multiple_of(kt * tk, tk)
        if causal:
            tri = jnp.where(lax.broadcasted_iota(jnp.int32, (qb, qb), 0)
                            <= lax.broadcasted_iota(jnp.int32, (qb, qb), 1), 0.0, MASK_VALUE)
            for b in range(nbq):
                own = (slice(b * qb, (b + 1) * qb), slice(b * hq, (b + 1) * hq))
                s_refs[slot][own] = s_refs[slot][own] + jnp.concatenate([tri] * NSA_HPG, axis=1)
            mt = jnp.max(s_refs[slot][...], axis=0, keepdims=True)
        s = s_refs[slot][...]
        m_new = jnp.maximum(m, mt)
        p = jnp.exp2(s - m_new).astype(BF16)
        acc_ref[...] = jnp.exp2(m - m_new) * acc_ref[...] + _dot(vst_ref[0, :, pl.ds(k0, tk)], p)
        return m_new

    def accumulate(pend):
        p, k0, alpha, ln = pend
        acc_ref[:, ln] = alpha * acc_ref[:, ln] + _dot(vst_ref[0, :, pl.ds(k0, tk)], p)

    def run(first, n_tiles, carry):
        mt, m = carry
        pend = None
        for k in range(n_tiles):
            cur, nxt = s_refs[k % 2], s_refs[(k + 1) % 2]
            k0c = pl.multiple_of((first + k) * tk, tk)
            k0n = pl.multiple_of((first + k + 1) * tk, tk)
            bias_rows(first + k + 1)
            m_new = jnp.maximum(m, mt)
            alpha = jnp.exp2(m - m_new)
            mts = []
            for n in range(lanes // SEL_LANES):
                ln = slice(n * SEL_LANES, (n + 1) * SEL_LANES)
                s_n = _dot(ks_ref[0, pl.ds(k0n, tk), :], qa_ref[:, ln])
                nxt[:, ln] = s_n
                mts.append(jnp.max(s_n, axis=0, keepdims=True))
                p = jnp.exp2(cur[:, ln] - m_new[:, ln]).astype(BF16)
                if pend is not None:
                    accumulate(pend)
                pend = (p, k0c, alpha[:, ln], ln)
            mt = jnp.concatenate(mts, axis=1)
            m = m_new
        accumulate(pend)
        return mt, m

    acc_ref[...] = jnp.zeros(acc_ref.shape, F32)
    last = step
    carry = (scores(0, 0), jnp.full((1, lanes), MASK_VALUE, F32))
    done = 0
    for unroll in SEL_UNROLLS:
        trips = (last - done) // unroll
        carry = lax.fori_loop(0, trips, lambda it, c, d=done, u=unroll: run(d + it * u, u, c), carry)
        done = done + trips * unroll
    mt_a, m = carry

    @pl.when(last % 2 == 0)
    def _():
        update(0, last, mt_a, m, True)

    @pl.when(last % 2 == 1)
    def _():
        mt_b, m_b = run(last - 1, 1, (mt_a, m))
        update(1, last, mt_b, m_b, True)

    for b in range(nbq):
        own = slice(b * hq, (b + 1) * hq)
        o_sel = acc_ref[:NSA_DIM, own] * (1.0 / acc_ref[NSA_DIM:NSA_DIM + 1, own])
        out_t = part_ref[b] + _gate(gates_ref.at[b], g, 1) * o_sel
        outs = [out_t[:, h * qb:(h + 1) * qb] for h in range(NSA_HPG)]
        o_ref[b * qb:(b + 1) * qb, :] = jnp.concatenate(outs, axis=0).T.astype(o_ref.dtype)


def _nsa(q, gates_t, kc, vc_aug_t, ks_aug, vs_aug_t, kw_aug, vw_aug_t):
    s = q.shape[0]
    assert s % (2 * SEL_TILE) == 0 and WINDOW == SEL_TILE
    gdim = NSA_HPG * NSA_DIM
    ncp = kc.shape[1]
    assert ncp * CMP_STRIDE == s and s // SEL_BLOCK * COV_RATIO == ncp
    nb = s // SEL_BLOCK
    ka = ks_aug.shape[-1]
    va = vs_aug_t.shape[1]
    hq = NSA_HPG * QUERY_BLOCK
    per_group = lambda shape: pl.BlockSpec((1,) + shape, lambda g, i: (g, 0, 0))
    rows = NSA_STEP_BLOCKS * QUERY_BLOCK
    assert s % rows == 0
    return pl.pallas_call(
        _nsa_kernel,
        grid=(NSA_GROUPS, s // rows),
        in_specs=[
            pl.BlockSpec((rows, gdim), lambda g, i: (i, g)),
            pl.BlockSpec((GATE_ROWS, rows), lambda g, i: (0, i)),
            per_group((ncp, NSA_DIM)),
            per_group((va, ncp)),
            per_group((s, ka)),
            per_group((va, s)),
            per_group((s, ka)),
            per_group((va, s)),
        ],
        out_specs=pl.BlockSpec((rows, gdim), lambda g, i: (i, g)),
        out_shape=jax.ShapeDtypeStruct((s, NSA_HEADS * NSA_DIM), BF16),
        scratch_shapes=[pltpu.VMEM((ka, NSA_STEP_BLOCKS * hq), BF16),
                        pltpu.VMEM((NSA_STEP_BLOCKS, nb, QUERY_BLOCK), F32),
                        pltpu.VMEM((NSA_STEP_BLOCKS, NSA_DIM, hq), F32),
                        pltpu.VMEM((NSA_STEP_BLOCKS, GATE_ROWS, QUERY_BLOCK), F32),
                        pltpu.VMEM((SEL_TILE, NSA_STEP_BLOCKS * hq), F32),
                        pltpu.VMEM((SEL_TILE, NSA_STEP_BLOCKS * hq), F32),
                        pltpu.VMEM((va, NSA_STEP_BLOCKS * hq), F32),
                        pltpu.VMEM((nb, QUERY_BLOCK), F32),
                        pltpu.VMEM((ncp, QUERY_BLOCK), jnp.int32),
                        pltpu.VMEM((SUBLANES + ncp, QUERY_BLOCK), F32)],
        compiler_params=_params("parallel", "arbitrary"),
        name="nsa",
    )(q, gates_t, kc, vc_aug_t, ks_aug, vs_aug_t, kw_aug, vw_aug_t)


def _sgu_kernel(u_ref, v_ref, lng_ref, lnb_ref, ws_ref, bs_ref, o_ref):
    c = SGU_CHUNK
    tm = u_ref.shape[0]
    v = jax.nn.gelu(v_ref[...])
    mu = jnp.mean(v, axis=-1, keepdims=True)
    var = jnp.mean(jnp.square(v - mu), axis=-1, keepdims=True)
    vn = ((v - mu) * lax.rsqrt(var + NORM_EPS) * lng_ref[...] + lnb_ref[...]).astype(BF16)
    u = jax.nn.gelu(u_ref[...])
    tri = lax.broadcasted_iota(jnp.int32, (c, c), 0) >= lax.broadcasted_iota(jnp.int32, (c, c), 1)
    for g in range(SGU_GROUPS):
        w = jnp.where(tri, ws_ref[g], 0.0).astype(BF16)
        cols = slice(g * c, (g + 1) * c)
        rhs = jnp.concatenate([vn[k * c:(k + 1) * c, cols] for k in range(tm // c)], axis=1)
        mixed = _dot(w, rhs)
        for k in range(tm // c):
            rows = slice(k * c, (k + 1) * c)
            o_ref[rows, cols] = (u[rows, cols] * (mixed[:, rows] + bs_ref[g])).astype(o_ref.dtype)


def _sgu(proj, lng, lnb, ws, bs, tm):
    s = proj.shape[0]
    w = lng.shape[0]
    c = SGU_CHUNK
    bs_b = jnp.broadcast_to(bs[:, :, None], (SGU_GROUPS, c, c))
    return pl.pallas_call(
        _sgu_kernel,
        grid=(s // tm,),
        in_specs=[
            pl.BlockSpec((tm, w), lambda i: (i, 0)),
            pl.BlockSpec((tm, w), lambda i: (i, 1)),
            pl.BlockSpec((1, w), lambda i: (0, 0)),
            pl.BlockSpec((1, w), lambda i: (0, 0)),
            pl.BlockSpec((SGU_GROUPS, c, c), lambda i: (0, 0, 0)),
            pl.BlockSpec((SGU_GROUPS, c, c), lambda i: (0, 0, 0)),
        ],
        out_specs=pl.BlockSpec((tm, w), lambda i: (i, 0)),
        out_shape=jax.ShapeDtypeStruct((s, w), BF16),
        compiler_params=_params("parallel"),
        name="sgu",
    )(proj, proj, lng.reshape(1, w), lnb.reshape(1, w), ws, bs_b)


def _mix_kernel(oa_ref, ob_ref, ga_ref, gb_ref, x_ref, pa_ref, pb_ref, wo_ref,
                gc_ref, wq_ref, mkt_ref, mv_ref, wmo_ref, o_ref):
    a = _dot(oa_ref[...], pa_ref[...])
    b = _dot(ob_ref[...], pb_ref[...])
    merged = jax.nn.sigmoid(ga_ref[...]) * a + jax.nn.sigmoid(gb_ref[...]) * b
    x = x_ref[...] + _dot(merged.astype(BF16), wo_ref[...])
    h = _rms(x, gc_ref[...]).astype(BF16)
    mq = (_dot(h, wq_ref[...]) * (MEM_DIM ** -0.5)).astype(BF16)
    outs = []
    for hh in range(MEM_HEADS):
        s = _dot(mq[:, hh * MEM_DIM:(hh + 1) * MEM_DIM], mkt_ref[hh])
        e = jnp.exp(s - jnp.max(s, axis=-1, keepdims=True))
        p = e / jnp.sum(e, axis=-1, keepdims=True)
        outs.append(_dot(p.astype(BF16), mv_ref[hh]).astype(BF16))
    o_ref[...] = x + _dot(jnp.concatenate(outs, axis=1), wmo_ref[...])


def _mix(o_a, o_b, proj, x, p_a, p_b, w_o, gate_block, g_cross, w_q, mk_t, mv, w_mo, tm):
    s, d = x.shape
    wa = o_a.shape[1]
    wb = o_b.shape[1]
    mw = w_q.shape[1]
    m = mv.shape[1]
    resident = lambda shape: pl.BlockSpec(shape, lambda i: (0,) * len(shape), pipeline_mode=pl.Buffered(1))
    return pl.pallas_call(
        _mix_kernel,
        grid=(s // tm,),
        in_specs=[
            pl.BlockSpec((tm, wa), lambda i: (i, 0)),
            pl.BlockSpec((tm, wb), lambda i: (i, 0)),
            pl.BlockSpec((tm, d), lambda i: (i, gate_block)),
            pl.BlockSpec((tm, d), lambda i: (i, gate_block + 1)),
            pl.BlockSpec((tm, d), lambda i: (i, 0)),
            resident((wa, d)),
            resident((wb, d)),
            resident((d, d)),
            resident((1, d)),
            resident((d, mw)),
            resident((MEM_HEADS, MEM_DIM, m)),
            resident((MEM_HEADS, m, MEM_DIM)),
            resident((mw, d)),
        ],
        out_specs=pl.BlockSpec((tm, d), lambda i: (i, 0)),
        out_shape=jax.ShapeDtypeStruct((s, d), F32),
        compiler_params=_params("parallel"),
        name="mix",
    )(o_a, o_b, proj, proj, x, p_a, p_b, w_o, g_cross.reshape(1, d), w_q, mk_t, mv, w_mo)


def _ffn_kernel(x_ref, g_ref, wg_ref, wu_ref, wo_ref, gf_ref, o_ref, h_ref, acc_ref, *, final_norm):
    j = pl.program_id(1)

    @pl.when(j == 0)
    def _():
        h_ref[...] = _rms(x_ref[...], g_ref[...]).astype(BF16)
        acc_ref[...] = jnp.zeros_like(acc_ref)

    h = h_ref[...]
    th = wg_ref.shape[1]
    out = None
    for c in range(th // FFN_CHUNK):
        cs = slice(c * FFN_CHUNK, (c + 1) * FFN_CHUNK)
        act = (jax.nn.silu(_dot(h, wg_ref[:, cs])) * _dot(h, wu_ref[:, cs])).astype(BF16)
        part = _dot(act, wo_ref[cs, :])
        out = part if out is None else out + part
    acc_ref[...] += out

    @pl.when(j == pl.num_programs(1) - 1)
    def _():
        y = x_ref[...] + acc_ref[...]
        o_ref[...] = _rms(y, gf_ref[...]) if final_norm else y


def _ffn(x, g, w_in, w_out, g_final, final_norm, tm, th):
    s, d = x.shape
    hidden = w_out.shape[0]
    nh = hidden // th
    return pl.pallas_call(
        functools.partial(_ffn_kernel, final_norm=final_norm),
        grid=(s // tm, nh),
        in_specs=[
            pl.BlockSpec((tm, d), lambda i, j: (i, 0)),
            pl.BlockSpec((1, d), lambda i, j: (0, 0)),
            pl.BlockSpec((d, th), lambda i, j: (0, j)),
            pl.BlockSpec((d, th), lambda i, j: (0, j + nh)),
            pl.BlockSpec((th, d), lambda i, j: (j, 0)),
            pl.BlockSpec((1, d), lambda i, j: (0, 0)),
        ],
        out_specs=pl.BlockSpec((tm, d), lambda i, j: (i, 0)),
        out_shape=jax.ShapeDtypeStruct((s, d), F32),
        scratch_shapes=[pltpu.VMEM((tm, d), BF16), pltpu.VMEM((tm, d), F32)],
        compiler_params=_params("parallel", "arbitrary"),
        name="ffn",
    )(x, g.reshape(1, d), w_in, w_in, w_out, g_final.reshape(1, d))


def _layer(x, mem, norm_mix_g, w_in, cmp_pe_k, cmp_k_w1, cmp_k_b1, cmp_k_w2, cmp_pe_v, cmp_v_w1, cmp_v_b1, cmp_v_w2,
           sgu_ln_g, sgu_ln_b, sgu_ws, sgu_b, w_proj_a, w_proj_b, w_mix_out, norm_cross_g, norm_mem_g,
           w_mq, w_mkv, w_mo, norm_ffn_g):
    s, d = x.shape
    qw = NSA_HEADS * NSA_DIM
    kvw = NSA_GROUPS * NSA_DIM
    sguw = sgu_ln_g.shape[0]
    ngate = GATE_ROWS
    o_kv = qw
    o_gate = o_kv + N_KV_STREAMS * kvw
    o_u = o_gate + ngate

    w_a = w_in[:, :o_gate].astype(BF16)
    w_b = jnp.pad(w_in[:, o_u:].astype(BF16), ((0, 0), (0, PROJ_TN)))
    w_b = lax.dynamic_update_slice(w_b, w_in[:, o_gate:o_u].astype(BF16), (0, w_in.shape[1] - o_u))
    nat, kw_aug, ks_aug, vs_aug_t, vw_aug_t, q, proj_b, gates_t = _proj(x, norm_mix_g, w_a, w_b, min(PROJ_TM, s))

    w1 =jnp.stack([cmp_k_w1, cmp_v_w1]).astype(BF16)
    pe = jnp.stack([cmp_pe_k.reshape(1, -1), cmp_pe_v.reshape(1, -1)])
    b1 = jnp.stack([cmp_k_b1.reshape(1, -1), cmp_v_b1.reshape(1, -1)])
    w2 = jnp.stack([cmp_k_w2, cmp_v_w2]).astype(BF16)
    cmp, cmp_t = _compress(nat, w1, pe, b1, w2)
    o_a = _nsa(q, gates_t, cmp[0], cmp_t[1], ks_aug, vs_aug_t, kw_aug, vw_aug_t)

    o_b = _sgu(proj_b, sgu_ln_g, sgu_ln_b, sgu_ws, sgu_b, min(SGU_TM, s))

    m = mem.shape[0]
    mw = MEM_HEADS * MEM_DIM
    mkv = _norm_matmul(mem, norm_mem_g, w_mkv.astype(BF16), F32, m, mw)
    mk_t = mkv[:, :mw].reshape(m, MEM_HEADS, MEM_DIM).transpose(1, 2, 0).astype(BF16)
    mv = mkv[:, mw:].reshape(m, MEM_HEADS, MEM_DIM).transpose(1, 0, 2).astype(BF16)
    return _mix(o_a, o_b, proj_b, x, w_proj_a.astype(BF16), w_proj_b.astype(BF16), w_mix_out.astype(BF16),
                (2 * sguw) // d, norm_cross_g, w_mq.astype(BF16), mk_t, mv, w_mo.astype(BF16), min(MIX_TM, s))


def kernel(x, mem, norm_mix_g, w_in, cmp_pe_k, cmp_k_w1, cmp_k_b1, cmp_k_w2, cmp_pe_v, cmp_v_w1, cmp_v_b1, cmp_v_w2, sgu_ln_g, sgu_ln_b, sgu_ws, sgu_b, w_proj_a, w_proj_b, w_mix_out, norm_cross_g, norm_mem_g, w_mq, w_mkv, w_mo, norm_ffn_g, w_ffn_in, w_ffn_out, norm_final_g):
    b, s, d = x.shape
    depth = w_in.shape[0]
    outs = []
    for bi in range(b):
        xb = x[bi]
        for l in range(depth):
            last = l == depth - 1
            xb = _layer(xb, mem[bi], norm_mix_g[l], w_in[l], cmp_pe_k[l], cmp_k_w1[l], cmp_k_b1[l], cmp_k_w2[l],
                        cmp_pe_v[l], cmp_v_w1[l], cmp_v_b1[l], cmp_v_w2[l], sgu_ln_g[l], sgu_ln_b[l], sgu_ws[l],
                        sgu_b[l], w_proj_a[l], w_proj_b[l], w_mix_out[l], norm_cross_g[l], norm_mem_g[l],
                        w_mq[l], w_mkv[l], w_mo[l], norm_ffn_g[l])
            xb = _ffn(xb, norm_ffn_g[l], w_ffn_in[l].astype(BF16), w_ffn_out[l].astype(BF16), norm_final_g, last,
                      min(FFN_TM, s), FFN_TH)
        outs.append(xb)
    return jnp.stack(outs)
```

```python
import functools
import math

import jax
import jax.numpy as jnp
from jax import lax
from jax.experimental import pallas as pl
from jax.experimental.pallas import tpu as pltpu

F32 = jnp.float32
BF16 = jnp.bfloat16

NORM_EPS = 1e-6
MASK_VALUE = -1e30
N_FORCED = 3
N_BRANCH = 3
LOG2E = math.log2(math.e)

NSA_HEADS = 16
NSA_GROUPS = 4
NSA_HPG = NSA_HEADS // NSA_GROUPS
NSA_DIM = 64
GATE_ROWS = NSA_HEADS * N_BRANCH
N_KV_STREAMS = 6
CMP_BLOCK = 32
CMP_STRIDE = 16
SEL_BLOCK = 64
SEL_TOPK = 16
WINDOW = 512
QUERY_BLOCK = 128
SGU_GROUPS = 8
SGU_CHUNK = 128
MEM_HEADS = 4
MEM_DIM = 128

LANES = 128
SUBLANES = 8
COV_RATIO = SEL_BLOCK // CMP_STRIDE
COV_LEAD = CMP_BLOCK // CMP_STRIDE - 1
COV_BAND = tuple(
    max(min((k - COV_LEAD) * CMP_STRIDE + CMP_BLOCK, SEL_BLOCK) - max((k - COV_LEAD) * CMP_STRIDE, 0), 0) / CMP_BLOCK
    for k in range(COV_RATIO + COV_LEAD))
assert SEL_BLOCK % CMP_STRIDE == 0 and CMP_BLOCK % CMP_STRIDE == 0 and COV_LEAD <= SUBLANES
SEL_TILE = 512
SEL_BPT = SEL_TILE // SEL_BLOCK
SEL_LANES = 256
SEL_UNROLLS = (4, 2)
NSA_STEP_BLOCKS = SEL_TILE // QUERY_BLOCK
NSA_BUCKETS = 8
BF16_ROWS = 16
FFN_CHUNK = 256
PROJ_TM, SGU_TM, MIX_TM, FFN_TM, FFN_TH = 1024, 512, 256, 512, 512
VMEM_LIMIT = 62 * 1024 * 1024


def _params(*sem):
    return pltpu.CompilerParams(dimension_semantics=sem, vmem_limit_bytes=VMEM_LIMIT)


def _rms(x, g):
    return x * lax.rsqrt(jnp.mean(x * x, axis=-1, keepdims=True) + NORM_EPS) * g


def _dot(a, b):
    return jnp.dot(a, b, preferred_element_type=F32)


def _ones_row(n):
    return jnp.where(lax.broadcasted_iota(jnp.int32, (BF16_ROWS, n), 0) == 0, 1.0, 0.0).astype(BF16)


def _norm_matmul_kernel(x_ref, g_ref, w_ref, o_ref, h_ref):
    @pl.when(pl.program_id(1) == 0)
    def _():
        h_ref[...] = _rms(x_ref[...], g_ref[...]).astype(BF16)

    o_ref[...] = _dot(h_ref[...], w_ref[...]).astype(o_ref.dtype)


def _norm_matmul(x, g, w, out_dtype, tm, tn):
    s, d = x.shape
    n = w.shape[1]
    return pl.pallas_call(
        _norm_matmul_kernel,
        grid=(s // tm, n // tn),
        in_specs=[
            pl.BlockSpec((tm, d), lambda i, j: (i, 0)),
            pl.BlockSpec((1, d), lambda i, j: (0, 0)),
            pl.BlockSpec((d, tn), lambda i, j: (0, j)),
        ],
        out_specs=pl.BlockSpec((tm, tn), lambda i, j: (i, j)),
        out_shape=jax.ShapeDtypeStruct((s, n), out_dtype),
        scratch_shapes=[pltpu.VMEM((tm, d), BF16)],
        compiler_params=_params("parallel", "arbitrary"),
        name="norm_matmul",
    )(x, g.reshape(1, d), w)


PROJ_TN = 512
PROJ_GATE_TN = 256


def _proj_kernel(x_ref, g_ref, wa_ref, wb_ref, nat_ref, kw_ref, ksa_ref, vsa_ref, vwa_ref, q_ref, o_ref, gt_ref,
                 h_ref, stage_ref):
    i = pl.program_id(0)
    j = pl.program_id(1)
    tm = x_ref.shape[0]
    dh = NSA_DIM
    ng = NSA_GROUPS
    tn = PROJ_TN
    qw = q_ref.shape[1]

    @pl.when(j == 0)
    def _():
        h_ref[...] = _rms(x_ref[...], g_ref[...]).astype(BF16)
        h = h_ref[...]

        def tile(t):
            return _dot(h, wa_ref[:, qw + t * tn:qw + (t + 1) * tn])

        def cols(res, c):
            return res[:, c * dh:(c + 1) * dh].astype(BF16)

        def cols_t(res, c):
            slab = res[:, (c // 2) * 2 * dh:(c // 2 + 1) * 2 * dh].T
            return slab[(c % 2) * dh:(c % 2 + 1) * dh].astype(BF16)

        res = tile(0)
        for c in range(2 * ng):
            stage_ref[...] = res[:, c * dh:(c + 1) * dh]
            for t in range(0, CMP_STRIDE, 2):
                pair = [stage_ref[pl.ds(t + u, tm // CMP_STRIDE, stride=CMP_STRIDE), :] for u in range(2)]
                nat_ref[c // ng, c % ng, :, t * dh:(t + 2) * dh] = jnp.concatenate(pair, axis=1).astype(BF16)

        res = tile(1)
        pos = i * tm + lax.broadcasted_iota(jnp.int32, (tm, dh), 0)
        lane = lax.broadcasted_iota(jnp.int32, (tm, dh), 1)
        onehot = jnp.where((pos // SEL_BLOCK) % SEL_BPT == lane, 1.0, 0.0).astype(BF16)
        for g in range(ng):
            ksa_ref[g, :, :dh] = cols(res, g)
            ksa_ref[g, :, dh:] = onehot
            vsa_ref[g, :dh, :] = cols_t(res, ng + g)
            vsa_ref[g, dh:, :] = _ones_row(tm)

        res = tile(2)
        for g in range(ng):
            kw_ref[g] = cols(res, g)
            vwa_ref[g, :dh, :] = cols_t(res, ng + g)
            vwa_ref[g, dh:, :] = _ones_row(tm)

        o_gate = qw + N_KV_STREAMS // 2 * tn
        res = _dot(h, wa_ref[:, o_gate:o_gate + PROJ_GATE_TN])
        gt_ref[...] = res[:, :LANES].T

        for t in range(qw // tn):
            q_ref[:, t * tn:(t + 1) * tn] = _dot(h, wa_ref[:, t * tn:(t + 1) * tn]).astype(BF16)

    @pl.when(j >= 1)
    def _():
        o_ref[...] = _dot(h_ref[...], wb_ref[...])


def _proj(x, g, w_a, w_b, tm):
    s, d = x.shape
    dh, ng, tn = NSA_DIM, NSA_GROUPS, PROJ_TN
    qw = NSA_HEADS * dh
    assert 2 * ng * dh == tn and qw % tn == 0 and w_a.shape[1] == qw + N_KV_STREAMS // 2 * tn + PROJ_GATE_TN
    n = w_b.shape[1]
    assert n % tn == 0 and GATE_ROWS <= LANES <= PROJ_GATE_TN
    nb = n // tn
    shapes = [
        jax.ShapeDtypeStruct((2, ng, s // CMP_STRIDE, CMP_STRIDE * dh), BF16),
        jax.ShapeDtypeStruct((ng, s, dh), BF16),
        jax.ShapeDtypeStruct((ng, s, 2 * dh), BF16),
        jax.ShapeDtypeStruct((ng, dh + BF16_ROWS, s), BF16),
        jax.ShapeDtypeStruct((ng, dh + BF16_ROWS, s), BF16),
        jax.ShapeDtypeStruct((s, NSA_HEADS * dh), BF16),
        jax.ShapeDtypeStruct((s, n), F32),
        jax.ShapeDtypeStruct((LANES, s), F32),
    ]
    return pl.pallas_call(
        _proj_kernel,
        grid=(s // tm, 1 + nb),
        in_specs=[
            pl.BlockSpec((tm, d), lambda i, j: (i, 0)),
            pl.BlockSpec((1, d), lambda i, j: (0, 0)),
            pl.BlockSpec(w_a.shape, lambda i, j: (0, 0), pipeline_mode=pl.Buffered(1)),
            pl.BlockSpec((d, tn), lambda i, j: (0, jnp.maximum(j - 1, 0))),
        ],
        out_specs=[
            pl.BlockSpec((2, ng, tm // CMP_STRIDE, CMP_STRIDE * dh), lambda i, j: (0, 0, i, 0)),
            pl.BlockSpec((ng, tm, dh), lambda i, j: (0, i, 0)),
            pl.BlockSpec((ng, tm, 2 * dh), lambda i, j: (0, i, 0)),
            pl.BlockSpec((ng, dh + BF16_ROWS, tm), lambda i, j: (0, 0, i)),
            pl.BlockSpec((ng, dh + BF16_ROWS, tm), lambda i, j: (0, 0, i)),
            pl.BlockSpec((tm, qw), lambda i, j: (i, 0)),
            pl.BlockSpec((tm, tn), lambda i, j: (i, jnp.maximum(j - 1, 0))),
            pl.BlockSpec((LANES, tm), lambda i, j: (0, i)),
        ],
        out_shape=shapes,
        scratch_shapes=[pltpu.VMEM((tm, d), BF16), pltpu.VMEM((tm, dh), F32)],
        compiler_params=_params("parallel", "arbitrary"),
        name="proj",
    )(x, g.reshape(1, d), w_a, w_b)


def _compress_kernel(r_ref, w1_ref, pe_ref, b1_ref, w2_ref, o_ref, ot_ref):
    half = r_ref.shape[-1]
    r = r_ref[0, 0]
    bias = _dot(pe_ref[0].astype(BF16), w1_ref[0]) + b1_ref[0]
    top = _dot(r, w1_ref[0, :half, :])
    bot = _dot(r, w1_ref[0, half:, :])
    ncp = r.shape[0]
    hid = top + pltpu.roll(bot, ncp - 1, 0) + bias
    out = _dot(jax.nn.gelu(hid).astype(BF16), w2_ref[0])
    o_ref[0, 0] = out.astype(o_ref.dtype)
    dh = out.shape[1]
    out_t = jnp.concatenate([out, jnp.zeros((ncp, LANES - dh), F32)], axis=1).T
    ot_ref[0, 0, :dh, :] = out_t[:dh].astype(ot_ref.dtype)
    ot_ref[0, 0, dh:, :] = _ones_row(ncp)


def _compress(r, w1, pe, b1, w2):
    _, g, ncp, half = r.shape
    hidden = w1.shape[-1]
    dh = w2.shape[-1]
    return pl.pallas_call(
        _compress_kernel,
        grid=(2, g),
        in_specs=[
            pl.BlockSpec((1, 1, ncp, half), lambda s, gg: (s, gg, 0, 0)),
            pl.BlockSpec((1, 2 * half, hidden), lambda s, gg: (s, 0, 0)),
            pl.BlockSpec((1, 1, 2 * half), lambda s, gg: (s, 0, 0)),
            pl.BlockSpec((1, 1, hidden), lambda s, gg: (s, 0, 0)),
            pl.BlockSpec((1, hidden, dh), lambda s, gg: (s, 0, 0)),
        ],
        out_specs=[pl.BlockSpec((1, 1, ncp, dh), lambda s, gg: (s, gg, 0, 0)),
                   pl.BlockSpec((1, 1, dh + BF16_ROWS, ncp), lambda s, gg: (s, gg, 0, 0))],
        out_shape=[jax.ShapeDtypeStruct((2, g, ncp, dh), BF16),
                   jax.ShapeDtypeStruct((2, g, dh + BF16_ROWS, ncp), BF16)],
        compiler_params=_params("parallel", "parallel"),
        name="compress",
    )(r, w1, pe, b1, w2)


PICKED = -2.0


def _nsa_kernel(q_ref, gt_ref, kc_ref, vct_ref, ks_ref, vst_ref, kw_ref, vwt_ref, o_ref,
                qa_ref, sel_ref, part_ref, gates_ref, s0_ref, s1_ref, acc_ref, blkf_ref, cend_ref, wrel_ref, pp_ref):
    g = pl.program_id(0)
    step = pl.program_id(1)
    qb = QUERY_BLOCK
    hq = NSA_HPG * qb
    nb = sel_ref.shape[1]
    ncp = kc_ref.shape[1]
    s_len = ks_ref.shape[1]
    first = step * NSA_STEP_BLOCKS

    @pl.when(step == 0)
    def _():
        lane = lax.broadcasted_iota(jnp.int32, cend_ref.shape, 1)
        cend_ref[...] = lax.broadcasted_iota(jnp.int32, cend_ref.shape, 0) * CMP_STRIDE + (CMP_BLOCK - 1) - lane
        wrel_ref[...] = (lax.broadcasted_iota(jnp.int32, wrel_ref.shape, 0)
                         - lax.broadcasted_iota(jnp.int32, wrel_ref.shape, 1))

    def rows(sub):
        return pl.ds(pl.multiple_of(sub * qb, qb), qb)

    bucket_ok = s_len % (NSA_BUCKETS * SEL_TILE) == 0 and nb // NSA_BUCKETS >= SEL_TOPK
    n_bucket = NSA_BUCKETS if bucket_ok else 1
    bucket = ((first + NSA_STEP_BLOCKS) * qb - 1) // (s_len // n_bucket)
    for b in range(n_bucket):
        nck, nbk = (b + 1) * (ncp // n_bucket), (b + 1) * (nb // n_bucket)

        @pl.when(bucket == b)
        def _():
            def scores(sub, carry):
                own = pl.ds(pl.multiple_of(sub * hq, hq), hq)
                _nsa_scores(g, first + sub, q_ref.at[rows(sub), :], gt_ref.at[:, rows(sub)], kc_ref, vct_ref, kw_ref,
                            vwt_ref, qa_ref.at[:, own], sel_ref.at[sub], part_ref.at[sub], gates_ref.at[sub], pp_ref,
                            cend_ref, wrel_ref, nck, nbk)
                return carry

            lax.fori_loop(0, NSA_STEP_BLOCKS, scores, 0)
            _nsa_topk(first, sel_ref, blkf_ref, nbk)

    _nsa_select(g, step, ks_ref, vst_ref, o_ref, qa_ref, sel_ref, part_ref, gates_ref, s0_ref, s1_ref, acc_ref)


def _gate(gates_ref, g, branch):
    rows = [gates_ref[pl.ds((g * NSA_HPG + h) * N_BRANCH + branch, 1), :] for h in range(NSA_HPG)]
    return jax.nn.sigmoid(jnp.concatenate(rows, axis=1))


def _nsa_scores(g, i, q_ref, gt_ref, kc_ref, vct_ref, kw_ref, vwt_ref, qa_ref, score_ref, part_ref, gates_ref,
                pp_ref, cend_ref, wrel_ref, nck, nbk):
    qb = QUERY_BLOCK
    start = i * qb
    hq = NSA_HPG * qb
    t_row = start + lax.broadcasted_iota(jnp.int32, (1, qb), 1)
    jt = t_row // SEL_BLOCK

    qt = (q_ref[...].astype(F32) * (NSA_DIM ** -0.5 * LOG2E)).T
    q_t = jnp.concatenate([qt[h * NSA_DIM:(h + 1) * NSA_DIM] for h in range(NSA_HPG)], axis=1).astype(BF16)
    qa_ref[:NSA_DIM, :] = q_t
    qa_ref[NSA_DIM:, :] = jnp.zeros((qa_ref.shape[0] - NSA_DIM, hq), BF16)
    gates_ref[...] = gt_ref[...]

    def exps(s, bias):
        out = []
        for h in range(NSA_HPG):
            sh = s[:, h * qb:(h + 1) * qb] + bias
            out.append(jnp.exp2(sh - jnp.max(sh, axis=0, keepdims=True)))
        return out

    wk = WINDOW + qb
    k0w = pl.multiple_of(jnp.maximum(start - WINDOW, 0), qb)
    sc = _dot(kc_ref[0, :nck, :], q_t)
    sw = _dot(kw_ref[0, pl.ds(k0w, wk), :], q_t)
    e_cmp = exps(sc, jnp.where(cend_ref[:nck, :] <= start, 0.0, MASK_VALUE))
    o_cmp = _dot(vct_ref[0, :, :nck], jnp.concatenate([e.astype(BF16) for e in e_cmp], axis=1))
    wrel = wrel_ref[...]
    e_win = exps(sw, jnp.where((wrel <= start - k0w) & (wrel > start - k0w - WINDOW), 0.0, MASK_VALUE))
    o_win = _dot(vwt_ref[0, :, pl.ds(k0w, wk)], jnp.concatenate([e.astype(BF16) for e in e_win], axis=1))
    inv_cmp = jnp.where(jnp.concatenate([t_row >= CMP_BLOCK - 1] * NSA_HPG, axis=1),
                        1.0 / o_cmp[NSA_DIM:NSA_DIM + 1, :], 0.0)
    psum = e_cmp[0] * inv_cmp[:, :qb]
    for h in range(1, NSA_HPG):
        psum = psum + e_cmp[h] * inv_cmp[:, h * qb:(h + 1) * qb]
    part_ref[...] = (_gate(gates_ref, g, 0) * inv_cmp * o_cmp[:NSA_DIM]
                     + _gate(gates_ref, g, 2) * (1.0 / o_win[NSA_DIM:NSA_DIM + 1, :]) * o_win[:NSA_DIM])
    pp_ref[:SUBLANES, :] = jnp.zeros((SUBLANES, qb), F32)
    pp_ref[SUBLANES:SUBLANES + nck, :] = psum
    imp = None
    for k, w in enumerate(COV_BAND):
        tap = pp_ref[pl.ds(SUBLANES - COV_LEAD + k, nbk, stride=COV_RATIO), :]
        tap = tap if w == 1.0 else w * tap
        imp = tap if imp is None else imp + tap
    blk = lax.broadcasted_iota(jnp.int32, (nbk, qb), 0)
    forced = (blk == 0) | (blk == jt) | (blk == jt - 1)
    score_ref[:nbk, :] = jnp.where(forced, PICKED, jnp.where(blk <= jt, imp, -1.0))


def _nsa_topk(first, sel_ref, blkf_ref, nbk):
    qb = QUERY_BLOCK
    blk = lax.broadcasted_iota(jnp.int32, (nbk, qb), 0)
    blkf_ref[:nbk, :] = blk.astype(F32)

    def pick(_, scores):
        out = []
        for score in scores:
            mx = jnp.max(score, axis=0, keepdims=True)
            cand = jnp.where(score == mx, blkf_ref[:nbk, :], float(nbk))
            out.append(jnp.where(cand == jnp.min(cand, axis=0, keepdims=True), PICKED, score))
        return tuple(out)

    scores = tuple(sel_ref[b, :nbk, :] for b in range(sel_ref.shape[0]))
    scores = lax.fori_loop(0, min(SEL_TOPK, nbk) - N_FORCED, pick, scores)
    for b, score in enumerate(scores):
        jt = ((first + b) * qb + lax.broadcasted_iota(jnp.int32, (1, qb), 1)) // SEL_BLOCK
        sel_ref[b, :nbk, :] = jnp.where((score == PICKED) & (blk <= jt), 0.0, MASK_VALUE)


def _nsa_select(g, step, ks_ref, vst_ref, o_ref, qa_ref, sel_ref, part_ref, gates_ref, s0_ref, s1_ref, acc_ref):
    s_refs = (s0_ref, s1_ref)
    qb = QUERY_BLOCK
    hq = NSA_HPG * qb
    nbq = sel_ref.shape[0]
    lanes = nbq * hq
    tk = SEL_TILE

    def bias_rows(kt):
        rows = []
        for b in range(nbq):
            b8 = sel_ref[b, pl.ds(pl.multiple_of(kt * SEL_BPT, SEL_BPT), SEL_BPT), :]
            b16 = jnp.concatenate([b8, jnp.zeros((BF16_ROWS - SEL_BPT, qb), F32)], axis=0).astype(BF16)
            rows += [b16] * NSA_HPG
        qa_ref[NSA_DIM:NSA_DIM + BF16_ROWS, :] = jnp.concatenate(rows, axis=1)

    def scores(slot, kt):
        k0 = pl.multiple_of(kt * tk, tk)
        bias_rows(kt)
        s = _dot(ks_ref[0, pl.ds(k0, tk), :], qa_ref[...])
        s_refs[slot][...] = s
        return jnp.max(s, axis=0, keepdims=True)

    def update(slot, kt, mt, m, causal):
        k0 = pl.multiple_of(kt * tk, tk)
        if causal:
            tri = jnp.where(lax.broadcasted_iota(jnp.int32, (qb, qb), 0)
                            <= lax.broadcasted_iota(jnp.int32, (qb, qb), 1), 0.0, MASK_VALUE)
            for b in range(nbq):
                own = (slice(b * qb, (b + 1) * qb), slice(b * hq, (b + 1) * hq))
                s_refs[slot][own] = s_refs[slot][own] + jnp.concatenate([tri] * NSA_HPG, axis=1)
            mt = jnp.max(s_refs[slot][...], axis=0, keepdims=True)
        s = s_refs[slot][...]
        m_new = jnp.maximum(m, mt)
        p = jnp.exp2(s - m_new).astype(BF16)
        acc_ref[...] = jnp.exp2(m - m_new) * acc_ref[...] + _dot(vst_ref[0, :, pl.ds(k0, tk)], p)
        return m_new

    def accumulate(pend):
        p, k0, alpha, ln = pend
        acc_ref[:, ln] = alpha * acc_ref[:, ln] + _dot(vst_ref[0, :, pl.ds(k0, tk)], p)

    def run(first, n_tiles, carry):
        mt, m = carry
        pend = None
        for k in range(n_tiles):
            cur, nxt = s_refs[k % 2], s_refs[(k + 1) % 2]
            k0c = pl.multiple_of((first + k) * tk, tk)
            k0n = pl.multiple_of((first + k + 1) * tk, tk)
            bias_rows(first + k + 1)
            m_new = jnp.maximum(m, mt)
            alpha = jnp.exp2(m - m_new)
            mts = []
            for n in range(lanes // SEL_LANES):
                ln = slice(n * SEL_LANES, (n + 1) * SEL_LANES)
                s_n = _dot(ks_ref[0, pl.ds(k0n, tk), :], qa_ref[:, ln])
                nxt[:, ln] = s_n
                mts.append(jnp.max(s_n, axis=0, keepdims=True))
                p = jnp.exp2(cur[:, ln] - m_new[:, ln]).astype(BF16)
                if pend is not None:
                    accumulate(pend)
                pend = (p, k0c, alpha[:, ln], ln)
            mt = jnp.concatenate(mts, axis=1)
            m = m_new
        accumulate(pend)
        return mt, m

    acc_ref[...] = jnp.zeros(acc_ref.shape, F32)
    last = step
    carry = (scores(0, 0), jnp.full((1, lanes), MASK_VALUE, F32))
    done = 0
    for unroll in SEL_UNROLLS:
        trips = (last - done) // unroll
        carry = lax.fori_loop(0, trips, lambda it, c, d=done, u=unroll: run(d + it * u, u, c), carry)
        done = done + trips * unroll
    mt_a, m = carry

    @pl.when(last % 2 == 0)
    def _():
        update(0, last, mt_a, m, True)

    @pl.when(last % 2 == 1)
    def _():
        mt_b, m_b = run(last - 1, 1, (mt_a, m))
        update(1, last, mt_b, m_b, True)

    for b in range(nbq):
        own = slice(b * hq, (b + 1) * hq)
        o_sel = acc_ref[:NSA_DIM, own] * (1.0 / acc_ref[NSA_DIM:NSA_DIM + 1, own])
        out_t = part_ref[b] + _gate(gates_ref.at[b], g, 1) * o_sel
        outs = [out_t[:, h * qb:(h + 1) * qb] for h in range(NSA_HPG)]
        o_ref[b * qb:(b + 1) * qb, :] = jnp.concatenate(outs, axis=0).T.astype(o_ref.dtype)


def _nsa(q, gates_t, kc, vc_aug_t, ks_aug, vs_aug_t, kw, vw_aug_t):
    s = q.shape[0]
    assert s % (2 * SEL_TILE) == 0 and s >= WINDOW + QUERY_BLOCK
    gdim = NSA_HPG * NSA_DIM
    ncp = kc.shape[1]
    assert ncp * CMP_STRIDE == s and s // SEL_BLOCK * COV_RATIO == ncp
    nb = s // SEL_BLOCK
    ka = ks_aug.shape[-1]
    va = vs_aug_t.shape[1]
    hq = NSA_HPG * QUERY_BLOCK
    per_group = lambda shape: pl.BlockSpec((1,) + shape, lambda g, i: (g, 0, 0))
    rows = NSA_STEP_BLOCKS * QUERY_BLOCK
    assert s % rows == 0
    return pl.pallas_call(
        _nsa_kernel,
        grid=(NSA_GROUPS, s // rows),
        in_specs=[
            pl.BlockSpec((rows, gdim), lambda g, i: (i, g)),
            pl.BlockSpec((GATE_ROWS, rows), lambda g, i: (0, i)),
            per_group((ncp, NSA_DIM)),
            per_group((va, ncp)),
            per_group((s, ka)),
            per_group((va, s)),
            per_group((s, NSA_DIM)),
            per_group((va, s)),
        ],
        out_specs=pl.BlockSpec((rows, gdim), lambda g, i: (i, g)),
        out_shape=jax.ShapeDtypeStruct((s, NSA_HEADS * NSA_DIM), BF16),
        scratch_shapes=[pltpu.VMEM((ka, NSA_STEP_BLOCKS * hq), BF16),
                        pltpu.VMEM((NSA_STEP_BLOCKS, nb, QUERY_BLOCK), F32),
                        pltpu.VMEM((NSA_STEP_BLOCKS, NSA_DIM, hq), F32),
                        pltpu.VMEM((NSA_STEP_BLOCKS, GATE_ROWS, QUERY_BLOCK), F32),
                        pltpu.VMEM((SEL_TILE, NSA_STEP_BLOCKS * hq), F32),
                        pltpu.VMEM((SEL_TILE, NSA_STEP_BLOCKS * hq), F32),
                        pltpu.VMEM((va, NSA_STEP_BLOCKS * hq), F32),
                        pltpu.VMEM((nb, QUERY_BLOCK), F32),
                        pltpu.VMEM((ncp, QUERY_BLOCK), jnp.int32),
                        pltpu.VMEM((WINDOW + QUERY_BLOCK, QUERY_BLOCK), jnp.int32),
                        pltpu.VMEM((SUBLANES + ncp, QUERY_BLOCK), F32)],
        compiler_params=_params("parallel", "arbitrary"),
        name="nsa",
    )(q, gates_t, kc, vc_aug_t, ks_aug, vs_aug_t, kw, vw_aug_t)


def _sgu_kernel(u_ref, v_ref, lng_ref, lnb_ref, ws_ref, bs_ref, o_ref):
    c = SGU_CHUNK
    tm = u_ref.shape[0]
    v = jax.nn.gelu(v_ref[...])
    mu = jnp.mean(v, axis=-1, keepdims=True)
    var = jnp.mean(jnp.square(v - mu), axis=-1, keepdims=True)
    vn = ((v - mu) * lax.rsqrt(var + NORM_EPS) * lng_ref[...] + lnb_ref[...]).astype(BF16)
    u = jax.nn.gelu(u_ref[...])
    tri = lax.broadcasted_iota(jnp.int32, (c, c), 0) >= lax.broadcasted_iota(jnp.int32, (c, c), 1)
    for g in range(SGU_GROUPS):
        w = jnp.where(tri, ws_ref[g], 0.0).astype(BF16)
        cols = slice(g * c, (g + 1) * c)
        rhs = jnp.concatenate([vn[k * c:(k + 1) * c, cols] for k in range(tm // c)], axis=1)
        mixed = _dot(w, rhs)
        for k in range(tm // c):
            rows = slice(k * c, (k + 1) * c)
            o_ref[rows, cols] = (u[rows, cols] * (mixed[:, rows] + bs_ref[g])).astype(o_ref.dtype)


def _sgu(proj, lng, lnb, ws, bs, tm):
    s = proj.shape[0]
    w = lng.shape[0]
    c = SGU_CHUNK
    bs_b = jnp.broadcast_to(bs[:, :, None], (SGU_GROUPS, c, c))
    return pl.pallas_call(
        _sgu_kernel,
        grid=(s // tm,),
        in_specs=[
            pl.BlockSpec((tm, w), lambda i: (i, 0)),
            pl.BlockSpec((tm, w), lambda i: (i, 1)),
            pl.BlockSpec((1, w), lambda i: (0, 0)),
            pl.BlockSpec((1, w), lambda i: (0, 0)),
            pl.BlockSpec((SGU_GROUPS, c, c), lambda i: (0, 0, 0)),
            pl.BlockSpec((SGU_GROUPS, c, c), lambda i: (0, 0, 0)),
        ],
        out_specs=pl.BlockSpec((tm, w), lambda i: (i, 0)),
        out_shape=jax.ShapeDtypeStruct((s, w), BF16),
        compiler_params=_params("parallel"),
        name="sgu",
    )(proj, proj, lng.reshape(1, w), lnb.reshape(1, w), ws, bs_b)


def _mix_kernel(oa_ref, ob_ref, ga_ref, gb_ref, x_ref, pa_ref, pb_ref, wo_ref,
                gc_ref, wq_ref, mkt_ref, mv_ref, wmo_ref, o_ref):
    a = _dot(oa_ref[...], pa_ref[...])
    b = _dot(ob_ref[...], pb_ref[...])
    merged = jax.nn.sigmoid(ga_ref[...]) * a + jax.nn.sigmoid(gb_ref[...]) * b
    x = x_ref[...] + _dot(merged.astype(BF16), wo_ref[...])
    h = _rms(x, gc_ref[...]).astype(BF16)
    mq = (_dot(h, wq_ref[...]) * (MEM_DIM ** -0.5)).astype(BF16)
    outs = []
    for hh in range(MEM_HEADS):
        s = _dot(mq[:, hh * MEM_DIM:(hh + 1) * MEM_DIM], mkt_ref[hh])
        e = jnp.exp(s - jnp.max(s, axis=-1, keepdims=True))
        p = e / jnp.sum(e, axis=-1, keepdims=True)
        outs.append(_dot(p.astype(BF16), mv_ref[hh]).astype(BF16))
    o_ref[...] = x + _dot(jnp.concatenate(outs, axis=1), wmo_ref[...])


def _mix(o_a, o_b, proj, x, p_a, p_b, w_o, gate_block, g_cross, w_q, mk_t, mv, w_mo, tm):
    s, d = x.shape
    wa = o_a.shape[1]
    wb = o_b.shape[1]
    mw = w_q.shape[1]
    m = mv.shape[1]
    resident = lambda shape: pl.BlockSpec(shape, lambda i: (0,) * len(shape), pipeline_mode=pl.Buffered(1))
    return pl.pallas_call(
        _mix_kernel,
        grid=(s // tm,),
        in_specs=[
            pl.BlockSpec((tm, wa), lambda i: (i, 0)),
            pl.BlockSpec((tm, wb), lambda i: (i, 0)),
            pl.BlockSpec((tm, d), lambda i: (i, gate_block)),
            pl.BlockSpec((tm, d), lambda i: (i, gate_block + 1)),
            pl.BlockSpec((tm, d), lambda i: (i, 0)),
            resident((wa, d)),
            resident((wb, d)),
            resident((d, d)),
            resident((1, d)),
            resident((d, mw)),
            resident((MEM_HEADS, MEM_DIM, m)),
            resident((MEM_HEADS, m, MEM_DIM)),
            resident((mw, d)),
        ],
        out_specs=pl.BlockSpec((tm, d), lambda i: (i, 0)),
        out_shape=jax.ShapeDtypeStruct((s, d), F32),
        compiler_params=_params("parallel"),
        name="mix",
    )(o_a, o_b, proj, proj, x, p_a, p_b, w_o, g_cross.reshape(1, d), w_q, mk_t, mv, w_mo)


def _ffn_kernel(x_ref, g_ref, wg_ref, wu_ref, wo_ref, gf_ref, o_ref, h_ref, acc_ref, *, final_norm):
    j = pl.program_id(1)

    @pl.when(j == 0)
    def _():
        h_ref[...] = _rms(x_ref[...], g_ref[...]).astype(BF16)
        acc_ref[...] = jnp.zeros_like(acc_ref)

    h = h_ref[...]
    th = wg_ref.shape[1]
    out = None
    for c in range(th // FFN_CHUNK):
        cs = slice(c * FFN_CHUNK, (c + 1) * FFN_CHUNK)
        act = (jax.nn.silu(_dot(h, wg_ref[:, cs])) * _dot(h, wu_ref[:, cs])).astype(BF16)
        part = _dot(act, wo_ref[cs, :])
        out = part if out is None else out + part
    acc_ref[...] += out

    @pl.when(j == pl.num_programs(1) - 1)
    def _():
        y = x_ref[...] + acc_ref[...]
        o_ref[...] = _rms(y, gf_ref[...]) if final_norm else y


def _ffn(x, g, w_in, w_out, g_final, final_norm, tm, th):
    s, d = x.shape
    hidden = w_out.shape[0]
    nh = hidden // th
    return pl.pallas_call(
        functools.partial(_ffn_kernel, final_norm=final_norm),
        grid=(s // tm, nh),
        in_specs=[
            pl.BlockSpec((tm, d), lambda i, j: (i, 0)),
            pl.BlockSpec((1, d), lambda i, j: (0, 0)),
            pl.BlockSpec((d, th), lambda i, j: (0, j)),
            pl.BlockSpec((d, th), lambda i, j: (0, j + nh)),
            pl.BlockSpec((th, d), lambda i, j: (j, 0)),
            pl.BlockSpec((1, d), lambda i, j: (0, 0)),
        ],
        out_specs=pl.BlockSpec((tm, d), lambda i, j: (i, 0)),
        out_shape=jax.ShapeDtypeStruct((s, d), F32),
        scratch_shapes=[pltpu.VMEM((tm, d), BF16), pltpu.VMEM((tm, d), F32)],
        compiler_params=_params("parallel", "arbitrary"),
        name="ffn",
    )(x, g.reshape(1, d), w_in, w_in, w_out, g_final.reshape(1, d))


def _layer(x, mem, norm_mix_g, w_in, cmp_pe_k, cmp_k_w1, cmp_k_b1, cmp_k_w2, cmp_pe_v, cmp_v_w1, cmp_v_b1, cmp_v_w2,
           sgu_ln_g, sgu_ln_b, sgu_ws, sgu_b, w_proj_a, w_proj_b, w_mix_out, norm_cross_g, norm_mem_g,
           w_mq, w_mkv, w_mo, norm_ffn_g):
    s, d = x.shape
    qw = NSA_HEADS * NSA_DIM
    kvw = NSA_GROUPS * NSA_DIM
    sguw = sgu_ln_g.shape[0]
    ngate = GATE_ROWS
    o_kv = qw
    o_gate = o_kv + N_KV_STREAMS * kvw
    o_u = o_gate + ngate

    w_a = w_in[:, :o_gate + PROJ_GATE_TN].astype(BF16)
    w_b = w_in[:, o_u:].astype(BF16)
    nat, kw, ks_aug, vs_aug_t, vw_aug_t, q, proj_b, gates_t = _proj(x, norm_mix_g, w_a, w_b, min(PROJ_TM, s))

    w1 =jnp.stack([cmp_k_w1, cmp_v_w1]).astype(BF16)
    pe = jnp.stack([cmp_pe_k.reshape(1, -1), cmp_pe_v.reshape(1, -1)])
    b1 = jnp.stack([cmp_k_b1.reshape(1, -1), cmp_v_b1.reshape(1, -1)])
    w2 = jnp.stack([cmp_k_w2, cmp_v_w2]).astype(BF16)
    cmp, cmp_t = _compress(nat, w1, pe, b1, w2)
    o_a = _nsa(q, gates_t, cmp[0], cmp_t[1], ks_aug, vs_aug_t, kw, vw_aug_t)

    o_b = _sgu(proj_b, sgu_ln_g, sgu_ln_b, sgu_ws, sgu_b, min(SGU_TM, s))

    m = mem.shape[0]
    mw = MEM_HEADS * MEM_DIM
    mkv = _norm_matmul(mem, norm_mem_g, w_mkv.astype(BF16), F32, m, mw)
    mk_t = mkv[:, :mw].reshape(m, MEM_HEADS, MEM_DIM).transpose(1, 2, 0).astype(BF16)
    mv = mkv[:, mw:].reshape(m, MEM_HEADS, MEM_DIM).transpose(1, 0, 2).astype(BF16)
    return _mix(o_a, o_b, proj_b, x, w_proj_a.astype(BF16), w_proj_b.astype(BF16), w_mix_out.astype(BF16),
                (2 * sguw) // d, norm_cross_g, w_mq.astype(BF16), mk_t, mv, w_mo.astype(BF16), min(MIX_TM, s))


def kernel(x, mem, norm_mix_g, w_in, cmp_pe_k, cmp_k_w1, cmp_k_b1, cmp_k_w2, cmp_pe_v, cmp_v_w1, cmp_v_b1, cmp_v_w2, sgu_ln_g, sgu_ln_b, sgu_ws, sgu_b, w_proj_a, w_proj_b, w_mix_out, norm_cross_g, norm_mem_g, w_mq, w_mkv, w_mo, norm_ffn_g, w_ffn_in, w_ffn_out, norm_final_g):
    b, s, d = x.shape
    depth = w_in.shape[0]
    outs = []
    for bi in range(b):
        xb = x[bi]
        for l in range(depth):
            last = l == depth - 1
            xb = _layer(xb, mem[bi], norm_mix_g[l], w_in[l], cmp_pe_k[l], cmp_k_w1[l], cmp_k_b1[l], cmp_k_w2[l],
                        cmp_pe_v[l], cmp_v_w1[l], cmp_v_b1[l], cmp_v_w2[l], sgu_ln_g[l], sgu_ln_b[l], sgu_ws[l],
                        sgu_b[l], w_proj_a[l], w_proj_b[l], w_mix_out[l], norm_cross_g[l], norm_mem_g[l],
                        w_mq[l], w_mkv[l], w_mo[l], norm_ffn_g[l])
            xb = _ffn(xb, norm_ffn_g[l], w_ffn_in[l].astype(BF16), w_ffn_out[l].astype(BF16), norm_final_g, last,
                      min(FFN_TM, s), FFN_TH)
        outs.append(xb)
    return jnp.stack(outs)
```

```python
import functools
import math

import jax
import jax.numpy as jnp
from jax import lax
from jax.experimental import pallas as pl
from jax.experimental.pallas import tpu as pltpu

F32 = jnp.float32
BF16 = jnp.bfloat16

NORM_EPS = 1e-6
MASK_VALUE = -1e30
N_FORCED = 3
N_BRANCH = 3
LOG2E = math.log2(math.e)

NSA_HEADS = 16
NSA_GROUPS = 4
NSA_HPG = NSA_HEADS // NSA_GROUPS
NSA_DIM = 64
GATE_ROWS = NSA_HEADS * N_BRANCH
N_KV_STREAMS = 6
CMP_BLOCK = 32
CMP_STRIDE = 16
SEL_BLOCK = 64
SEL_TOPK = 16
WINDOW = 512
QUERY_BLOCK = 128
SGU_GROUPS = 8
SGU_CHUNK = 128
MEM_HEADS = 4
MEM_DIM = 128

LANES = 128
SUBLANES = 8
COV_RATIO = SEL_BLOCK // CMP_STRIDE
COV_LEAD = CMP_BLOCK // CMP_STRIDE - 1
COV_BAND = tuple(
    max(min((k - COV_LEAD) * CMP_STRIDE + CMP_BLOCK, SEL_BLOCK) - max((k - COV_LEAD) * CMP_STRIDE, 0), 0) / CMP_BLOCK
    for k in range(COV_RATIO + COV_LEAD))
assert SEL_BLOCK % CMP_STRIDE == 0 and CMP_BLOCK % CMP_STRIDE == 0 and COV_LEAD <= SUBLANES
SEL_TILE = 512
SEL_BPT = SEL_TILE // SEL_BLOCK
SEL_LANES = 256
SEL_UNROLLS = (4, 2)
NSA_STEP_BLOCKS = SEL_TILE // QUERY_BLOCK
NSA_BUCKETS = 8
BF16_ROWS = 16
FFN_CHUNK = 256
FFN_OUT_TN = 1024
FFN_NORM_ROWS = 256
PROJ_TM, SGU_TM, MIX_TM, FFN_TM, FFN_TH = 1024, 512, 256, 1024, 512
VMEM_LIMIT = 62 * 1024 * 1024


def _params(*sem):
    return pltpu.CompilerParams(dimension_semantics=sem, vmem_limit_bytes=VMEM_LIMIT)


def _rms(x, g):
    return x * lax.rsqrt(jnp.mean(x * x, axis=-1, keepdims=True) + NORM_EPS) * g


def _dot(a, b):
    return jnp.dot(a, b, preferred_element_type=F32)


def _ones_row(n):
    return jnp.where(lax.broadcasted_iota(jnp.int32, (BF16_ROWS, n), 0) == 0, 1.0, 0.0).astype(BF16)


def _norm_matmul_kernel(x_ref, g_ref, w_ref, o_ref, h_ref):
    @pl.when(pl.program_id(1) == 0)
    def _():
        h_ref[...] = _rms(x_ref[...], g_ref[...]).astype(BF16)

    o_ref[...] = _dot(h_ref[...], w_ref[...]).astype(o_ref.dtype)


def _norm_matmul(x, g, w, out_dtype, tm, tn):
    s, d = x.shape
    n = w.shape[1]
    return pl.pallas_call(
        _norm_matmul_kernel,
        grid=(s // tm, n // tn),
        in_specs=[
            pl.BlockSpec((tm, d), lambda i, j: (i, 0)),
            pl.BlockSpec((1, d), lambda i, j: (0, 0)),
            pl.BlockSpec((d, tn), lambda i, j: (0, j)),
        ],
        out_specs=pl.BlockSpec((tm, tn), lambda i, j: (i, j)),
        out_shape=jax.ShapeDtypeStruct((s, n), out_dtype),
        scratch_shapes=[pltpu.VMEM((tm, d), BF16)],
        compiler_params=_params("parallel", "arbitrary"),
        name="norm_matmul",
    )(x, g.reshape(1, d), w)


PROJ_TN = 512
PROJ_GATE_TN = 256


def _proj_kernel(x_ref, g_ref, wa_ref, wb_ref, nat_ref, kw_ref, ksa_ref, vsa_ref, vwa_ref, q_ref, o_ref, gt_ref,
                 h_ref, stage_ref):
    i = pl.program_id(0)
    j = pl.program_id(1)
    tm = x_ref.shape[0]
    dh = NSA_DIM
    ng = NSA_GROUPS
    tn = PROJ_TN
    qw = q_ref.shape[1]

    @pl.when(j == 0)
    def _():
        h_ref[...] = _rms(x_ref[...], g_ref[...]).astype(BF16)
        h = h_ref[...]

        def tile(t):
            return _dot(h, wa_ref[:, qw + t * tn:qw + (t + 1) * tn])

        def cols(res, c):
            return res[:, c * dh:(c + 1) * dh].astype(BF16)

        def cols_t(res, c):
            slab = res[:, (c // 2) * 2 * dh:(c // 2 + 1) * 2 * dh].T
            return slab[(c % 2) * dh:(c % 2 + 1) * dh].astype(BF16)

        res = tile(0)
        for c in range(2 * ng):
            stage_ref[...] = res[:, c * dh:(c + 1) * dh]
            for t in range(0, CMP_STRIDE, 2):
                pair = [stage_ref[pl.ds(t + u, tm // CMP_STRIDE, stride=CMP_STRIDE), :] for u in range(2)]
                nat_ref[c // ng, c % ng, :, t * dh:(t + 2) * dh] = jnp.concatenate(pair, axis=1).astype(BF16)

        res = tile(1)
        pos = i * tm + lax.broadcasted_iota(jnp.int32, (tm, dh), 0)
        lane = lax.broadcasted_iota(jnp.int32, (tm, dh), 1)
        onehot = jnp.where((pos // SEL_BLOCK) % SEL_BPT == lane, 1.0, 0.0).astype(BF16)
        for g in range(ng):
            ksa_ref[g, :, :dh] = cols(res, g)
            ksa_ref[g, :, dh:] = onehot
            vsa_ref[g, :dh, :] = cols_t(res, ng + g)
            vsa_ref[g, dh:, :] = _ones_row(tm)

        res = tile(2)
        for g in range(ng):
            kw_ref[g] = cols(res, g)
            vwa_ref[g, :dh, :] = cols_t(res, ng + g)
            vwa_ref[g, dh:, :] = _ones_row(tm)

        o_gate = qw + N_KV_STREAMS // 2 * tn
        res = _dot(h, wa_ref[:, o_gate:o_gate + PROJ_GATE_TN])
        gt_ref[...] = res[:, :LANES].T

        for t in range(qw // tn):
            q_ref[:, t * tn:(t + 1) * tn] = _dot(h, wa_ref[:, t * tn:(t + 1) * tn]).astype(BF16)

    @pl.when(j >= 1)
    def _():
        o_ref[...] = _dot(h_ref[...], wb_ref[...])


def _proj(x, g, w_a, w_b, tm):
    s, d = x.shape
    dh, ng, tn = NSA_DIM, NSA_GROUPS, PROJ_TN
    qw = NSA_HEADS * dh
    assert 2 * ng * dh == tn and qw % tn == 0 and w_a.shape[1] == qw + N_KV_STREAMS // 2 * tn + PROJ_GATE_TN
    n = w_b.shape[1]
    assert n % tn == 0 and GATE_ROWS <= LANES <= PROJ_GATE_TN
    nb = n // tn
    shapes = [
        jax.ShapeDtypeStruct((2, ng, s // CMP_STRIDE, CMP_STRIDE * dh), BF16),
        jax.ShapeDtypeStruct((ng, s, dh), BF16),
        jax.ShapeDtypeStruct((ng, s, 2 * dh), BF16),
        jax.ShapeDtypeStruct((ng, dh + BF16_ROWS, s), BF16),
        jax.ShapeDtypeStruct((ng, dh + BF16_ROWS, s), BF16),
        jax.ShapeDtypeStruct((s, NSA_HEADS * dh), BF16),
        jax.ShapeDtypeStruct((s, n), F32),
        jax.ShapeDtypeStruct((LANES, s), F32),
    ]
    return pl.pallas_call(
        _proj_kernel,
        grid=(s // tm, 1 + nb),
        in_specs=[
            pl.BlockSpec((tm, d), lambda i, j: (i, 0)),
            pl.BlockSpec((1, d), lambda i, j: (0, 0)),
            pl.BlockSpec(w_a.shape, lambda i, j: (0, 0), pipeline_mode=pl.Buffered(1)),
            pl.BlockSpec((d, tn), lambda i, j: (0, jnp.maximum(j - 1, 0))),
        ],
        out_specs=[
            pl.BlockSpec((2, ng, tm // CMP_STRIDE, CMP_STRIDE * dh), lambda i, j: (0, 0, i, 0)),
            pl.BlockSpec((ng, tm, dh), lambda i, j: (0, i, 0)),
            pl.BlockSpec((ng, tm, 2 * dh), lambda i, j: (0, i, 0)),
            pl.BlockSpec((ng, dh + BF16_ROWS, tm), lambda i, j: (0, 0, i)),
            pl.BlockSpec((ng, dh + BF16_ROWS, tm), lambda i, j: (0, 0, i)),
            pl.BlockSpec((tm, qw), lambda i, j: (i, 0)),
            pl.BlockSpec((tm, tn), lambda i, j: (i, jnp.maximum(j - 1, 0))),
            pl.BlockSpec((LANES, tm), lambda i, j: (0, i)),
        ],
        out_shape=shapes,
        scratch_shapes=[pltpu.VMEM((tm, d), BF16), pltpu.VMEM((tm, dh), F32)],
        compiler_params=_params("parallel", "arbitrary"),
        name="proj",
    )(x, g.reshape(1, d), w_a, w_b)


def _compress_kernel(r_ref, w1_ref, pe_ref, b1_ref, w2_ref, o_ref, ot_ref):
    half = r_ref.shape[-1]
    r = r_ref[0, 0]
    bias = _dot(pe_ref[0].astype(BF16), w1_ref[0]) + b1_ref[0]
    top = _dot(r, w1_ref[0, :half, :])
    bot = _dot(r, w1_ref[0, half:, :])
    ncp = r.shape[0]
    hid = top + pltpu.roll(bot, ncp - 1, 0) + bias
    out = _dot(jax.nn.gelu(hid).astype(BF16), w2_ref[0])
    o_ref[0, 0] = out.astype(o_ref.dtype)
    dh = out.shape[1]
    out_t = jnp.concatenate([out, jnp.zeros((ncp, LANES - dh), F32)], axis=1).T
    ot_ref[0, 0, :dh, :] = out_t[:dh].astype(ot_ref.dtype)
    ot_ref[0, 0, dh:, :] = _ones_row(ncp)


def _compress(r, w1, pe, b1, w2):
    _, g, ncp, half = r.shape
    hidden = w1.shape[-1]
    dh = w2.shape[-1]
    return pl.pallas_call(
        _compress_kernel,
        grid=(2, g),
        in_specs=[
            pl.BlockSpec((1, 1, ncp, half), lambda s, gg: (s, gg, 0, 0)),
            pl.BlockSpec((1, 2 * half, hidden), lambda s, gg: (s, 0, 0)),
            pl.BlockSpec((1, 1, 2 * half), lambda s, gg: (s, 0, 0)),
            pl.BlockSpec((1, 1, hidden), lambda s, gg: (s, 0, 0)),
            pl.BlockSpec((1, hidden, dh), lambda s, gg: (s, 0, 0)),
        ],
        out_specs=[pl.BlockSpec((1, 1, ncp, dh), lambda s, gg: (s, gg, 0, 0)),
                   pl.BlockSpec((1, 1, dh + BF16_ROWS, ncp), lambda s, gg: (s, gg, 0, 0))],
        out_shape=[jax.ShapeDtypeStruct((2, g, ncp, dh), BF16),
                   jax.ShapeDtypeStruct((2, g, dh + BF16_ROWS, ncp), BF16)],
        compiler_params=_params("parallel", "parallel"),
        name="compress",
    )(r, w1, pe, b1, w2)


PICKED = -2.0


def _nsa_kernel(q_ref, gt_ref, kc_ref, vct_ref, ks_ref, vst_ref, kw_ref, vwt_ref, o_ref,
                qa_ref, sel_ref, part_ref, gates_ref, s0_ref, s1_ref, acc_ref, blkf_ref, cend_ref, wrel_ref, pp_ref):
    g = pl.program_id(0)
    step = pl.program_id(1)
    qb = QUERY_BLOCK
    hq = NSA_HPG * qb
    nb = sel_ref.shape[1]
    ncp = kc_ref.shape[1]
    s_len = ks_ref.shape[1]
    first = step * NSA_STEP_BLOCKS

    @pl.when(step == 0)
    def _():
        lane = lax.broadcasted_iota(jnp.int32, cend_ref.shape, 1)
        cend_ref[...] = lax.broadcasted_iota(jnp.int32, cend_ref.shape, 0) * CMP_STRIDE + (CMP_BLOCK - 1) - lane
        wrel_ref[...] = (lax.broadcasted_iota(jnp.int32, wrel_ref.shape, 0)
                         - lax.broadcasted_iota(jnp.int32, wrel_ref.shape, 1))

    def rows(sub):
        return pl.ds(pl.multiple_of(sub * qb, qb), qb)

    bucket_ok = s_len % (NSA_BUCKETS * SEL_TILE) == 0 and nb // NSA_BUCKETS >= SEL_TOPK
    n_bucket = NSA_BUCKETS if bucket_ok else 1
    bucket = ((first + NSA_STEP_BLOCKS) * qb - 1) // (s_len // n_bucket)
    for b in range(n_bucket):
        nck, nbk = (b + 1) * (ncp // n_bucket), (b + 1) * (nb // n_bucket)

        @pl.when(bucket == b)
        def _():
            def scores(sub, carry):
                own = pl.ds(pl.multiple_of(sub * hq, hq), hq)
                _nsa_scores(g, first + sub, q_ref.at[rows(sub), :], gt_ref.at[:, rows(sub)], kc_ref, vct_ref, kw_ref,
                            vwt_ref, qa_ref.at[:, own], sel_ref.at[sub], part_ref.at[sub], gates_ref.at[sub], pp_ref,
                            cend_ref, wrel_ref, nck, nbk)
                return carry

            lax.fori_loop(0, NSA_STEP_BLOCKS, scores, 0)
            _nsa_topk(first, sel_ref, blkf_ref, nbk)

    _nsa_select(g, step, ks_ref, vst_ref, o_ref, qa_ref, sel_ref, part_ref, gates_ref, s0_ref, s1_ref, acc_ref)


def _gate(gates_ref, g, branch):
    rows = [gates_ref[pl.ds((g * NSA_HPG + h) * N_BRANCH + branch, 1), :] for h in range(NSA_HPG)]
    return jax.nn.sigmoid(jnp.concatenate(rows, axis=1))


def _nsa_scores(g, i, q_ref, gt_ref, kc_ref, vct_ref, kw_ref, vwt_ref, qa_ref, score_ref, part_ref, gates_ref,
                pp_ref, cend_ref, wrel_ref, nck, nbk):
    qb = QUERY_BLOCK
    start = i * qb
    hq = NSA_HPG * qb
    t_row = start + lax.broadcasted_iota(jnp.int32, (1, qb), 1)
    jt = t_row // SEL_BLOCK

    qt = (q_ref[...].astype(F32) * (NSA_DIM ** -0.5 * LOG2E)).T
    q_t = jnp.concatenate([qt[h * NSA_DIM:(h + 1) * NSA_DIM] for h in range(NSA_HPG)], axis=1).astype(BF16)
    qa_ref[:NSA_DIM, :] = q_t
    qa_ref[NSA_DIM:, :] = jnp.zeros((qa_ref.shape[0] - NSA_DIM, hq), BF16)
    gates_ref[...] = gt_ref[...]

    def exps(s, bias):
        out = []
        for h in range(NSA_HPG):
            sh = s[:, h * qb:(h + 1) * qb] + bias
            out.append(jnp.exp2(sh - jnp.max(sh, axis=0, keepdims=True)))
        return out

    wk = WINDOW + qb
    k0w = pl.multiple_of(jnp.maximum(start - WINDOW, 0), qb)
    sc = _dot(kc_ref[0, :nck, :], q_t)
    sw = _dot(kw_ref[0, pl.ds(k0w, wk), :], q_t)
    e_cmp = exps(sc, jnp.where(cend_ref[:nck, :] <= start, 0.0, MASK_VALUE))
    o_cmp = _dot(vct_ref[0, :, :nck], jnp.concatenate([e.astype(BF16) for e in e_cmp], axis=1))
    wrel = wrel_ref[...]
    e_win = exps(sw, jnp.where((wrel <= start - k0w) & (wrel > start - k0w - WINDOW), 0.0, MASK_VALUE))
    o_win = _dot(vwt_ref[0, :, pl.ds(k0w, wk)], jnp.concatenate([e.astype(BF16) for e in e_win], axis=1))
    inv_cmp = jnp.where(jnp.concatenate([t_row >= CMP_BLOCK - 1] * NSA_HPG, axis=1),
                        1.0 / o_cmp[NSA_DIM:NSA_DIM + 1, :], 0.0)
    psum = e_cmp[0] * inv_cmp[:, :qb]
    for h in range(1, NSA_HPG):
        psum = psum + e_cmp[h] * inv_cmp[:, h * qb:(h + 1) * qb]
    part_ref[...] = (_gate(gates_ref, g, 0) * inv_cmp * o_cmp[:NSA_DIM]
                     + _gate(gates_ref, g, 2) * (1.0 / o_win[NSA_DIM:NSA_DIM + 1, :]) * o_win[:NSA_DIM])
    pp_ref[:SUBLANES, :] = jnp.zeros((SUBLANES, qb), F32)
    pp_ref[SUBLANES:SUBLANES + nck, :] = psum
    imp = None
    for k, w in enumerate(COV_BAND):
        tap = pp_ref[pl.ds(SUBLANES - COV_LEAD + k, nbk, stride=COV_RATIO), :]
        tap = tap if w == 1.0 else w * tap
        imp = tap if imp is None else imp + tap
    blk = lax.broadcasted_iota(jnp.int32, (nbk, qb), 0)
    forced = (blk == 0) | (blk == jt) | (blk == jt - 1)
    score_ref[:nbk, :] = jnp.where(forced, PICKED, jnp.where(blk <= jt, imp, -1.0))


def _nsa_topk(first, sel_ref, blkf_ref, nbk):
    qb = QUERY_BLOCK
    blk = lax.broadcasted_iota(jnp.int32, (nbk, qb), 0)
    blkf_ref[:nbk, :] = blk.astype(F32)

    def pick(_, scores):
        out = []
        for score in scores:
            mx = jnp.max(score, axis=0, keepdims=True)
            cand = jnp.where(score == mx, blkf_ref[:nbk, :], float(nbk))
            out.append(jnp.where(cand == jnp.min(cand, axis=0, keepdims=True), PICKED, score))
        return tuple(out)

    scores = tuple(sel_ref[b, :nbk, :] for b in range(sel_ref.shape[0]))
    scores = lax.fori_loop(0, min(SEL_TOPK, nbk) - N_FORCED, pick, scores)
    for b, score in enumerate(scores):
        jt = ((first + b) * qb + lax.broadcasted_iota(jnp.int32, (1, qb), 1)) // SEL_BLOCK
        sel_ref[b, :nbk, :] = jnp.where((score == PICKED) & (blk <= jt), 0.0, MASK_VALUE)


def _nsa_select(g, step, ks_ref, vst_ref, o_ref, qa_ref, sel_ref, part_ref, gates_ref, s0_ref, s1_ref, acc_ref):
    s_refs = (s0_ref, s1_ref)
    qb = QUERY_BLOCK
    hq = NSA_HPG * qb
    nbq = sel_ref.shape[0]
    lanes = nbq * hq
    tk = SEL_TILE

    def bias_rows(kt):
        rows = []
        for b in range(nbq):
            b8 = sel_ref[b, pl.ds(pl.multiple_of(kt * SEL_BPT, SEL_BPT), SEL_BPT), :]
            b16 = jnp.concatenate([b8, jnp.zeros((BF16_ROWS - SEL_BPT, qb), F32)], axis=0).astype(BF16)
            rows += [b16] * NSA_HPG
        qa_ref[NSA_DIM:NSA_DIM + BF16_ROWS, :] = jnp.concatenate(rows, axis=1)

    def scores(slot, kt):
        k0 = pl.multiple_of(kt * tk, tk)
        bias_rows(kt)
        s = _dot(ks_ref[0, pl.ds(k0, tk), :], qa_ref[...])
        s_refs[slot][...] = s
        return jnp.max(s, axis=0, keepdims=True)

    def update(slot, kt, mt, m, causal):
        k0 = pl.multiple_of(kt * tk, tk)
        if causal:
            tri = jnp.where(lax.broadcasted_iota(jnp.int32, (qb, qb), 0)
                            <= lax.broadcasted_iota(jnp.int32, (qb, qb), 1), 0.0, MASK_VALUE)
            for b in range(nbq):
                own = (slice(b * qb, (b + 1) * qb), slice(b * hq, (b + 1) * hq))
                s_refs[slot][own] = s_refs[slot][own] + jnp.concatenate([tri] * NSA_HPG, axis=1)
            mt = jnp.max(s_refs[slot][...], axis=0, keepdims=True)
        s = s_refs[slot][...]
        m_new = jnp.maximum(m, mt)
        p = jnp.exp2(s - m_new).astype(BF16)
        acc_ref[...] = jnp.exp2(m - m_new) * acc_ref[...] + _dot(vst_ref[0, :, pl.ds(k0, tk)], p)
        return m_new

    def accumulate(pend):
        p, k0, alpha, ln = pend
        acc_ref[:, ln] = alpha * acc_ref[:, ln] + _dot(vst_ref[0, :, pl.ds(k0, tk)], p)

    def run(first, n_tiles, carry):
        mt, m = carry
        pend = None
        for k in range(n_tiles):
            cur, nxt = s_refs[k % 2], s_refs[(k + 1) % 2]
            k0c = pl.multiple_of((first + k) * tk, tk)
            k0n = pl.multiple_of((first + k + 1) * tk, tk)
            bias_rows(first + k + 1)
            m_new = jnp.maximum(m, mt)
            alpha = jnp.exp2(m - m_new)
            mts = []
            for n in range(lanes // SEL_LANES):
                ln = slice(n * SEL_LANES, (n + 1) * SEL_LANES)
                s_n = _dot(ks_ref[0, pl.ds(k0n, tk), :], qa_ref[:, ln])
                nxt[:, ln] = s_n
                mts.append(jnp.max(s_n, axis=0, keepdims=True))
                p = jnp.exp2(cur[:, ln] - m_new[:, ln]).astype(BF16)
                if pend is not None:
                    accumulate(pend)
                pend = (p, k0c, alpha[:, ln], ln)
            mt = jnp.concatenate(mts, axis=1)
            m = m_new
        accumulate(pend)
        return mt, m

    acc_ref[...] = jnp.zeros(acc_ref.shape, F32)
    last = step
    carry = (scores(0, 0), jnp.full((1, lanes), MASK_VALUE, F32))
    done = 0
    for unroll in SEL_UNROLLS:
        trips = (last - done) // unroll
        carry = lax.fori_loop(0, trips, lambda it, c, d=done, u=unroll: run(d + it * u, u, c), carry)
        done = done + trips * unroll
    mt_a, m = carry

    @pl.when(last % 2 == 0)
    def _():
        update(0, last, mt_a, m, True)

    @pl.when(last % 2 == 1)
    def _():
        mt_b, m_b = run(last - 1, 1, (mt_a, m))
        update(1, last, mt_b, m_b, True)

    for b in range(nbq):
        own = slice(b * hq, (b + 1) * hq)
        o_sel = acc_ref[:NSA_DIM, own] * (1.0 / acc_ref[NSA_DIM:NSA_DIM + 1, own])
        out_t = part_ref[b] + _gate(gates_ref.at[b], g, 1) * o_sel
        outs = [out_t[:, h * qb:(h + 1) * qb] for h in range(NSA_HPG)]
        o_ref[b * qb:(b + 1) * qb, :] = jnp.concatenate(outs, axis=0).T.astype(o_ref.dtype)


def _nsa(q, gates_t, kc, vc_aug_t, ks_aug, vs_aug_t, kw, vw_aug_t):
    s = q.shape[0]
    assert s % (2 * SEL_TILE) == 0 and s >= WINDOW + QUERY_BLOCK
    gdim = NSA_HPG * NSA_DIM
    ncp = kc.shape[1]
    assert ncp * CMP_STRIDE == s and s // SEL_BLOCK * COV_RATIO == ncp
    nb = s // SEL_BLOCK
    ka = ks_aug.shape[-1]
    va = vs_aug_t.shape[1]
    hq = NSA_HPG * QUERY_BLOCK
    per_group = lambda shape: pl.BlockSpec((1,) + shape, lambda g, i: (g, 0, 0))
    rows = NSA_STEP_BLOCKS * QUERY_BLOCK
    assert s % rows == 0
    return pl.pallas_call(
        _nsa_kernel,
        grid=(NSA_GROUPS, s // rows),
        in_specs=[
            pl.BlockSpec((rows, gdim), lambda g, i: (i, g)),
            pl.BlockSpec((GATE_ROWS, rows), lambda g, i: (0, i)),
            per_group((ncp, NSA_DIM)),
            per_group((va, ncp)),
            per_group((s, ka)),
            per_group((va, s)),
            per_group((s, NSA_DIM)),
            per_group((va, s)),
        ],
        out_specs=pl.BlockSpec((rows, gdim), lambda g, i: (i, g)),
        out_shape=jax.ShapeDtypeStruct((s, NSA_HEADS * NSA_DIM), BF16),
        scratch_shapes=[pltpu.VMEM((ka, NSA_STEP_BLOCKS * hq), BF16),
                        pltpu.VMEM((NSA_STEP_BLOCKS, nb, QUERY_BLOCK), F32),
                        pltpu.VMEM((NSA_STEP_BLOCKS, NSA_DIM, hq), F32),
                        pltpu.VMEM((NSA_STEP_BLOCKS, GATE_ROWS, QUERY_BLOCK), F32),
                        pltpu.VMEM((SEL_TILE, NSA_STEP_BLOCKS * hq), F32),
                        pltpu.VMEM((SEL_TILE, NSA_STEP_BLOCKS * hq), F32),
                        pltpu.VMEM((va, NSA_STEP_BLOCKS * hq), F32),
                        pltpu.VMEM((nb, QUERY_BLOCK), F32),
                        pltpu.VMEM((ncp, QUERY_BLOCK), jnp.int32),
                        pltpu.VMEM((WINDOW + QUERY_BLOCK, QUERY_BLOCK), jnp.int32),
                        pltpu.VMEM((SUBLANES + ncp, QUERY_BLOCK), F32)],
        compiler_params=_params("parallel", "arbitrary"),
        name="nsa",
    )(q, gates_t, kc, vc_aug_t, ks_aug, vs_aug_t, kw, vw_aug_t)


def _sgu_kernel(u_ref, v_ref, lng_ref, lnb_ref, ws_ref, bs_ref, o_ref):
    c = SGU_CHUNK
    tm = u_ref.shape[0]
    v = jax.nn.gelu(v_ref[...])
    mu = jnp.mean(v, axis=-1, keepdims=True)
    var = jnp.mean(jnp.square(v - mu), axis=-1, keepdims=True)
    vn = ((v - mu) * lax.rsqrt(var + NORM_EPS) * lng_ref[...] + lnb_ref[...]).astype(BF16)
    u = jax.nn.gelu(u_ref[...])
    tri = lax.broadcasted_iota(jnp.int32, (c, c), 0) >= lax.broadcasted_iota(jnp.int32, (c, c), 1)
    for g in range(SGU_GROUPS):
        w = jnp.where(tri, ws_ref[g], 0.0).astype(BF16)
        cols = slice(g * c, (g + 1) * c)
        rhs = jnp.concatenate([vn[k * c:(k + 1) * c, cols] for k in range(tm // c)], axis=1)
        mixed = _dot(w, rhs)
        for k in range(tm // c):
            rows = slice(k * c, (k + 1) * c)
            o_ref[rows, cols] = (u[rows, cols] * (mixed[:, rows] + bs_ref[g])).astype(o_ref.dtype)


def _sgu(proj, lng, lnb, ws, bs, tm):
    s = proj.shape[0]
    w = lng.shape[0]
    c = SGU_CHUNK
    bs_b = jnp.broadcast_to(bs[:, :, None], (SGU_GROUPS, c, c))
    return pl.pallas_call(
        _sgu_kernel,
        grid=(s // tm,),
        in_specs=[
            pl.BlockSpec((tm, w), lambda i: (i, 0)),
            pl.BlockSpec((tm, w), lambda i: (i, 1)),
            pl.BlockSpec((1, w), lambda i: (0, 0)),
            pl.BlockSpec((1, w), lambda i: (0, 0)),
            pl.BlockSpec((SGU_GROUPS, c, c), lambda i: (0, 0, 0)),
            pl.BlockSpec((SGU_GROUPS, c, c), lambda i: (0, 0, 0)),
        ],
        out_specs=pl.BlockSpec((tm, w), lambda i: (i, 0)),
        out_shape=jax.ShapeDtypeStruct((s, w), BF16),
        compiler_params=_params("parallel"),
        name="sgu",
    )(proj, proj, lng.reshape(1, w), lnb.reshape(1, w), ws, bs_b)


def _mix_kernel(oa_ref, ob_ref, ga_ref, gb_ref, x_ref, pa_ref, pb_ref, wo_ref,
                gc_ref, wq_ref, mkt_ref, mv_ref, wmo_ref, o_ref):
    a = _dot(oa_ref[...], pa_ref[...])
    b = _dot(ob_ref[...], pb_ref[...])
    merged = jax.nn.sigmoid(ga_ref[...]) * a + jax.nn.sigmoid(gb_ref[...]) * b
    x = x_ref[...] + _dot(merged.astype(BF16), wo_ref[...])
    h = _rms(x, gc_ref[...]).astype(BF16)
    mq = (_dot(h, wq_ref[...]) * (MEM_DIM ** -0.5)).astype(BF16)
    outs = []
    for hh in range(MEM_HEADS):
        s = _dot(mq[:, hh * MEM_DIM:(hh + 1) * MEM_DIM], mkt_ref[hh])
        e = jnp.exp(s - jnp.max(s, axis=-1, keepdims=True))
        p = e / jnp.sum(e, axis=-1, keepdims=True)
        outs.append(_dot(p.astype(BF16), mv_ref[hh]).astype(BF16))
    o_ref[...] = x + _dot(jnp.concatenate(outs, axis=1), wmo_ref[...])


def _mix(o_a, o_b, proj, x, p_a, p_b, w_o, gate_block, g_cross, w_q, mk_t, mv, w_mo, tm):
    s, d = x.shape
    wa = o_a.shape[1]
    wb = o_b.shape[1]
    mw = w_q.shape[1]
    m = mv.shape[1]
    resident = lambda shape: pl.BlockSpec(shape, lambda i: (0,) * len(shape), pipeline_mode=pl.Buffered(1))
    return pl.pallas_call(
        _mix_kernel,
        grid=(s // tm,),
        in_specs=[
            pl.BlockSpec((tm, wa), lambda i: (i, 0)),
            pl.BlockSpec((tm, wb), lambda i: (i, 0)),
            pl.BlockSpec((tm, d), lambda i: (i, gate_block)),
            pl.BlockSpec((tm, d), lambda i: (i, gate_block + 1)),
            pl.BlockSpec((tm, d), lambda i: (i, 0)),
            resident((wa, d)),
            resident((wb, d)),
            resident((d, d)),
            resident((1, d)),
            resident((d, mw)),
            resident((MEM_HEADS, MEM_DIM, m)),
            resident((MEM_HEADS, m, MEM_DIM)),
            resident((mw, d)),
        ],
        out_specs=pl.BlockSpec((tm, d), lambda i: (i, 0)),
        out_shape=jax.ShapeDtypeStruct((s, d), F32),
        compiler_params=_params("parallel"),
        name="mix",
    )(o_a, o_b, proj, proj, x, p_a, p_b, w_o, g_cross.reshape(1, d), w_q, mk_t, mv, w_mo)


def _ffn_kernel(x_ref, g_ref, wg_ref, wu_ref, wo_ref, gf_ref, o_ref, h_ref, *, final_norm):
    j = pl.program_id(1)
    row_chunks = [slice(r, r + FFN_NORM_ROWS) for r in range(0, x_ref.shape[0], FFN_NORM_ROWS)]

    @pl.when(j == 0)
    def _():
        for rs in row_chunks:
            x = x_ref[rs, :]
            h_ref[rs, :] = _rms(x, g_ref[...]).astype(BF16)
            o_ref[rs, :] = x

    h = h_ref[...]
    th = wg_ref.shape[1]
    d = o_ref.shape[1]
    for c in range(th // FFN_CHUNK):
        cs = slice(c * FFN_CHUNK, (c + 1) * FFN_CHUNK)
        act = (jax.nn.silu(_dot(h, wg_ref[:, cs])) * _dot(h, wu_ref[:, cs])).astype(BF16)
        for n in range(d // FFN_OUT_TN):
            ns = slice(n * FFN_OUT_TN, (n + 1) * FFN_OUT_TN)
            o_ref[:, ns] += _dot(act, wo_ref[cs, ns])

    if final_norm:
        @pl.when(j == pl.num_programs(1) - 1)
        def _():
            for rs in row_chunks:
                o_ref[rs, :] = _rms(o_ref[rs, :], gf_ref[...])


def _ffn(x, g, w_in, w_out, g_final, final_norm, tm, th):
    s, d = x.shape
    hidden = w_out.shape[0]
    nh = hidden // th
    return pl.pallas_call(
        functools.partial(_ffn_kernel, final_norm=final_norm),
        grid=(s // tm, nh),
        in_specs=[
            pl.BlockSpec((tm, d), lambda i, j: (i, 0)),
            pl.BlockSpec((1, d), lambda i, j: (0, 0)),
            pl.BlockSpec((d, th), lambda i, j: (0, j)),
            pl.BlockSpec((d, th), lambda i, j: (0, j + nh)),
            pl.BlockSpec((th, d), lambda i, j: (j, 0)),
            pl.BlockSpec((1, d), lambda i, j: (0, 0)),
        ],
        out_specs=pl.BlockSpec((tm, d), lambda i, j: (i, 0)),
        out_shape=jax.ShapeDtypeStruct((s, d), F32),
        scratch_shapes=[pltpu.VMEM((tm, d), BF16)],
        compiler_params=_params("parallel", "arbitrary"),
        name="ffn",
    )(x, g.reshape(1, d), w_in, w_in, w_out, g_final.reshape(1, d))


def _layer(x, mem, norm_mix_g, w_in, cmp_pe_k, cmp_k_w1, cmp_k_b1, cmp_k_w2, cmp_pe_v, cmp_v_w1, cmp_v_b1, cmp_v_w2,
           sgu_ln_g, sgu_ln_b, sgu_ws, sgu_b, w_proj_a, w_proj_b, w_mix_out, norm_cross_g, norm_mem_g,
           w_mq, w_mkv, w_mo, norm_ffn_g):
    s, d = x.shape
    qw = NSA_HEADS * NSA_DIM
    kvw = NSA_GROUPS * NSA_DIM
    sguw = sgu_ln_g.shape[0]
    ngate = GATE_ROWS
    o_kv = qw
    o_gate = o_kv + N_KV_STREAMS * kvw
    o_u = o_gate + ngate

    w_a = w_in[:, :o_gate + PROJ_GATE_TN].astype(BF16)
    w_b = w_in[:, o_u:].astype(BF16)
    nat, kw, ks_aug, vs_aug_t, vw_aug_t, q, proj_b, gates_t = _proj(x, norm_mix_g, w_a, w_b, min(PROJ_TM, s))

    w1 =jnp.stack([cmp_k_w1, cmp_v_w1]).astype(BF16)
    pe = jnp.stack([cmp_pe_k.reshape(1, -1), cmp_pe_v.reshape(1, -1)])
    b1 = jnp.stack([cmp_k_b1.reshape(1, -1), cmp_v_b1.reshape(1, -1)])
    w2 = jnp.stack([cmp_k_w2, cmp_v_w2]).astype(BF16)
    cmp, cmp_t = _compress(nat, w1, pe, b1, w2)
    o_a = _nsa(q, gates_t, cmp[0], cmp_t[1], ks_aug, vs_aug_t, kw, vw_aug_t)

    o_b = _sgu(proj_b, sgu_ln_g, sgu_ln_b, sgu_ws, sgu_b, min(SGU_TM, s))

    m = mem.shape[0]
    mw = MEM_HEADS * MEM_DIM
    mkv = _norm_matmul(mem, norm_mem_g, w_mkv.astype(BF16), F32, m, mw)
    mk_t = mkv[:, :mw].reshape(m, MEM_HEADS, MEM_DIM).transpose(1, 2, 0).astype(BF16)
    mv = mkv[:, mw:].reshape(m, MEM_HEADS, MEM_DIM).transpose(1, 0, 2).astype(BF16)
    return _mix(o_a, o_b, proj_b, x, w_proj_a.astype(BF16), w_proj_b.astype(BF16), w_mix_out.astype(BF16),
                (2 * sguw) // d, norm_cross_g, w_mq.astype(BF16), mk_t, mv, w_mo.astype(BF16), min(MIX_TM, s))


def kernel(x, mem, norm_mix_g, w_in, cmp_pe_k, cmp_k_w1, cmp_k_b1, cmp_k_w2, cmp_pe_v, cmp_v_w1, cmp_v_b1, cmp_v_w2, sgu_ln_g, sgu_ln_b, sgu_ws, sgu_b, w_proj_a, w_proj_b, w_mix_out, norm_cross_g, norm_mem_g, w_mq, w_mkv, w_mo, norm_ffn_g, w_ffn_in, w_ffn_out, norm_final_g):
    b, s, d = x.shape
    depth = w_in.shape[0]
    outs = []
    for bi in range(b):
        xb = x[bi]
        for l in range(depth):
            last = l == depth - 1
            xb = _layer(xb, mem[bi], norm_mix_g[l], w_in[l], cmp_pe_k[l], cmp_k_w1[l], cmp_k_b1[l], cmp_k_w2[l],
                        cmp_pe_v[l], cmp_v_w1[l], cmp_v_b1[l], cmp_v_w2[l], sgu_ln_g[l], sgu_ln_b[l], sgu_ws[l],
                        sgu_b[l], w_proj_a[l], w_proj_b[l], w_mix_out[l], norm_cross_g[l], norm_mem_g[l],
                        w_mq[l], w_mkv[l], w_mo[l], norm_ffn_g[l])
            xb = _ffn(xb, norm_ffn_g[l], w_ffn_in[l].astype(BF16), w_ffn_out[l].astype(BF16), norm_final_g, last,
                      min(FFN_TM, s), FFN_TH)
        outs.append(xb)
    return jnp.stack(outs)
```

```python
import functools
import math

import jax
import jax.numpy as jnp
from jax import lax
from jax.experimental import pallas as pl
from jax.experimental.pallas import tpu as pltpu

F32 = jnp.float32
BF16 = jnp.bfloat16

NORM_EPS = 1e-6
MASK_VALUE = -1e30
N_FORCED = 3
N_BRANCH = 3
LOG2E = math.log2(math.e)

NSA_HEADS = 16
NSA_GROUPS = 4
NSA_HPG = NSA_HEADS // NSA_GROUPS
NSA_DIM = 64
GATE_ROWS = NSA_HEADS * N_BRANCH
N_KV_STREAMS = 6
CMP_BLOCK = 32
CMP_STRIDE = 16
SEL_BLOCK = 64
SEL_TOPK = 16
WINDOW = 512
QUERY_BLOCK = 128
SGU_GROUPS = 8
SGU_CHUNK = 128
MEM_HEADS = 4
MEM_DIM = 128

LANES = 128
SUBLANES = 8
COV_RATIO = SEL_BLOCK // CMP_STRIDE
COV_LEAD = CMP_BLOCK // CMP_STRIDE - 1
COV_BAND = tuple(
    max(min((k - COV_LEAD) * CMP_STRIDE + CMP_BLOCK, SEL_BLOCK) - max((k - COV_LEAD) * CMP_STRIDE, 0), 0) / CMP_BLOCK
    for k in range(COV_RATIO + COV_LEAD))
assert SEL_BLOCK % CMP_STRIDE == 0 and CMP_BLOCK % CMP_STRIDE == 0 and COV_LEAD <= SUBLANES
SEL_TILE = 512
SEL_BPT = SEL_TILE // SEL_BLOCK
SEL_LANES = 256
SEL_UNROLLS = (4, 2)
NSA_STEP_BLOCKS = SEL_TILE // QUERY_BLOCK
NSA_BUCKETS = 8
BF16_ROWS = 16
FFN_CHUNK = 256
FFN_OUT_TN = 1024
FFN_NORM_ROWS = 256
PROJ_TM, SGU_TM, MIX_TM, FFN_TM, FFN_TH = 1024, 512, 256, 1024, 512
VMEM_LIMIT = 62 * 1024 * 1024


def _params(*sem):
    return pltpu.CompilerParams(dimension_semantics=sem, vmem_limit_bytes=VMEM_LIMIT)


def _rms(x, g):
    return x * lax.rsqrt(jnp.mean(x * x, axis=-1, keepdims=True) + NORM_EPS) * g


def _dot(a, b):
    return jnp.dot(a, b, preferred_element_type=F32)


def _ones_row(n):
    return jnp.where(lax.broadcasted_iota(jnp.int32, (BF16_ROWS, n), 0) == 0, 1.0, 0.0).astype(BF16)


def _norm_matmul_kernel(x_ref, g_ref, w_ref, o_ref, h_ref):
    @pl.when(pl.program_id(1) == 0)
    def _():
        h_ref[...] = _rms(x_ref[...], g_ref[...]).astype(BF16)

    o_ref[...] = _dot(h_ref[...], w_ref[...]).astype(o_ref.dtype)


def _norm_matmul(x, g, w, out_dtype, tm, tn):
    s, d = x.shape
    n = w.shape[1]
    return pl.pallas_call(
        _norm_matmul_kernel,
        grid=(s // tm, n // tn),
        in_specs=[
            pl.BlockSpec((tm, d), lambda i, j: (i, 0)),
            pl.BlockSpec((1, d), lambda i, j: (0, 0)),
            pl.BlockSpec((d, tn), lambda i, j: (0, j)),
        ],
        out_specs=pl.BlockSpec((tm, tn), lambda i, j: (i, j)),
        out_shape=jax.ShapeDtypeStruct((s, n), out_dtype),
        scratch_shapes=[pltpu.VMEM((tm, d), BF16)],
        compiler_params=_params("parallel", "arbitrary"),
        name="norm_matmul",
    )(x, g.reshape(1, d), w)


PROJ_TN = 512
PROJ_GATE_TN = 256


def _proj_kernel(x_ref, g_ref, wa_ref, wb_ref, nat_ref, kw_ref, ksa_ref, vsa_ref, vwa_ref, q_ref, o_ref, gt_ref,
                 h_ref, stage_ref):
    i = pl.program_id(0)
    j = pl.program_id(1)
    tm = x_ref.shape[0]
    dh = NSA_DIM
    ng = NSA_GROUPS
    tn = PROJ_TN
    qw = q_ref.shape[1]
    nb = pl.num_programs(1) - 1

    @pl.when(j == 0)
    def _():
        h_ref[...] = _rms(x_ref[...], g_ref[...]).astype(BF16)

    @pl.when(j < nb)
    def _():
        o_ref[...] = _dot(h_ref[...], wb_ref[...])

    @pl.when(j == nb)
    def _():
        h = h_ref[...]

        def tile(t):
            return _dot(h, wa_ref[:, qw + t * tn:qw + (t + 1) * tn])

        def cols(res, c):
            return res[:, c * dh:(c + 1) * dh].astype(BF16)

        def cols_t(res, c):
            slab = res[:, (c // 2) * 2 * dh:(c // 2 + 1) * 2 * dh].T
            return slab[(c % 2) * dh:(c % 2 + 1) * dh].astype(BF16)

        res = tile(0)
        for c in range(2 * ng):
            stage_ref[...] = res[:, c * dh:(c + 1) * dh]
            for t in range(0, CMP_STRIDE, 2):
                pair = [stage_ref[pl.ds(t + u, tm // CMP_STRIDE, stride=CMP_STRIDE), :] for u in range(2)]
                nat_ref[c // ng, c % ng, :, t * dh:(t + 2) * dh] = jnp.concatenate(pair, axis=1).astype(BF16)

        res = tile(1)
        pos = i * tm + lax.broadcasted_iota(jnp.int32, (tm, dh), 0)
        lane = lax.broadcasted_iota(jnp.int32, (tm, dh), 1)
        onehot = jnp.where((pos // SEL_BLOCK) % SEL_BPT == lane, 1.0, 0.0).astype(BF16)
        for g in range(ng):
            ksa_ref[g, :, :dh] = cols(res, g)
            ksa_ref[g, :, dh:] = onehot
            vsa_ref[g, :dh, :] = cols_t(res, ng + g)
            vsa_ref[g, dh:, :] = _ones_row(tm)

        res = tile(2)
        for g in range(ng):
            kw_ref[g] = cols(res, g)
            vwa_ref[g, :dh, :] = cols_t(res, ng + g)
            vwa_ref[g, dh:, :] = _ones_row(tm)

        o_gate = qw + N_KV_STREAMS // 2 * tn
        res = _dot(h, wa_ref[:, o_gate:o_gate + PROJ_GATE_TN])
        gt_ref[...] = res[:, :LANES].T

        for t in range(qw // tn):
            q_ref[:, t * tn:(t + 1) * tn] = _dot(h, wa_ref[:, t * tn:(t + 1) * tn]).astype(BF16)


def _proj(x, g, w_a, w_b, tm):
    s, d = x.shape
    dh, ng, tn = NSA_DIM, NSA_GROUPS, PROJ_TN
    qw = NSA_HEADS * dh
    assert 2 * ng * dh == tn and qw % tn == 0 and w_a.shape[1] == qw + N_KV_STREAMS // 2 * tn + PROJ_GATE_TN
    n = w_b.shape[1]
    assert n % tn == 0 and GATE_ROWS <= LANES <= PROJ_GATE_TN
    nb = n // tn
    shapes = [
        jax.ShapeDtypeStruct((2, ng, s // CMP_STRIDE, CMP_STRIDE * dh), BF16),
        jax.ShapeDtypeStruct((ng, s, dh), BF16),
        jax.ShapeDtypeStruct((ng, s, 2 * dh), BF16),
        jax.ShapeDtypeStruct((ng, dh + BF16_ROWS, s), BF16),
        jax.ShapeDtypeStruct((ng, dh + BF16_ROWS, s), BF16),
        jax.ShapeDtypeStruct((s, NSA_HEADS * dh), BF16),
        jax.ShapeDtypeStruct((s, n), F32),
        jax.ShapeDtypeStruct((LANES, s), F32),
    ]
    return pl.pallas_call(
        _proj_kernel,
        grid=(s // tm, 1 + nb),
        in_specs=[
            pl.BlockSpec((tm, d), lambda i, j: (i, 0)),
            pl.BlockSpec((1, d), lambda i, j: (0, 0)),
            pl.BlockSpec(w_a.shape, lambda i, j: (0, 0), pipeline_mode=pl.Buffered(1)),
            pl.BlockSpec((d, tn), lambda i, j: (0, jnp.minimum(j, nb - 1))),
        ],
        out_specs=[
            pl.BlockSpec((2, ng, tm // CMP_STRIDE, CMP_STRIDE * dh), lambda i, j: (0, 0, i, 0)),
            pl.BlockSpec((ng, tm, dh), lambda i, j: (0, i, 0)),
            pl.BlockSpec((ng, tm, 2 * dh), lambda i, j: (0, i, 0)),
            pl.BlockSpec((ng, dh + BF16_ROWS, tm), lambda i, j: (0, 0, i)),
            pl.BlockSpec((ng, dh + BF16_ROWS, tm), lambda i, j: (0, 0, i)),
            pl.BlockSpec((tm, qw), lambda i, j: (i, 0)),
            pl.BlockSpec((tm, tn), lambda i, j: (i, jnp.minimum(j, nb - 1))),
            pl.BlockSpec((LANES, tm), lambda i, j: (0, i)),
        ],
        out_shape=shapes,
        scratch_shapes=[pltpu.VMEM((tm, d), BF16), pltpu.VMEM((tm, dh), F32)],
        compiler_params=_params("parallel", "arbitrary"),
        name="proj",
    )(x, g.reshape(1, d), w_a, w_b)


def _compress_kernel(r_ref, w1_ref, pe_ref, b1_ref, w2_ref, o_ref, ot_ref):
    half = r_ref.shape[-1]
    r = r_ref[0, 0]
    bias = _dot(pe_ref[0].astype(BF16), w1_ref[0]) + b1_ref[0]
    top = _dot(r, w1_ref[0, :half, :])
    bot = _dot(r, w1_ref[0, half:, :])
    ncp = r.shape[0]
    hid = top + pltpu.roll(bot, ncp - 1, 0) + bias
    out = _dot(jax.nn.gelu(hid).astype(BF16), w2_ref[0])
    o_ref[0, 0] = out.astype(o_ref.dtype)
    dh = out.shape[1]
    out_t = jnp.concatenate([out, jnp.zeros((ncp, LANES - dh), F32)], axis=1).T
    ot_ref[0, 0, :dh, :] = out_t[:dh].astype(ot_ref.dtype)
    ot_ref[0, 0, dh:, :] = _ones_row(ncp)


def _compress(r, w1, pe, b1, w2):
    _, g, ncp, half = r.shape
    hidden = w1.shape[-1]
    dh = w2.shape[-1]
    return pl.pallas_call(
        _compress_kernel,
        grid=(2, g),
        in_specs=[
            pl.BlockSpec((1, 1, ncp, half), lambda s, gg: (s, gg, 0, 0)),
            pl.BlockSpec((1, 2 * half, hidden), lambda s, gg: (s, 0, 0)),
            pl.BlockSpec((1, 1, 2 * half), lambda s, gg: (s, 0, 0)),
            pl.BlockSpec((1, 1, hidden), lambda s, gg: (s, 0, 0)),
            pl.BlockSpec((1, hidden, dh), lambda s, gg: (s, 0, 0)),
        ],
        out_specs=[pl.BlockSpec((1, 1, ncp, dh), lambda s, gg: (s, gg, 0, 0)),
                   pl.BlockSpec((1, 1, dh + BF16_ROWS, ncp), lambda s, gg: (s, gg, 0, 0))],
        out_shape=[jax.ShapeDtypeStruct((2, g, ncp, dh), BF16),
                   jax.ShapeDtypeStruct((2, g, dh + BF16_ROWS, ncp), BF16)],
        compiler_params=_params("parallel", "parallel"),
        name="compress",
    )(r, w1, pe, b1, w2)


PICKED = -2.0


def _nsa_kernel(q_ref, gt_ref, kc_ref, vct_ref, ks_ref, vst_ref, kw_ref, vwt_ref, o_ref,
                qa_ref, sel_ref, part_ref, gates_ref, s0_ref, s1_ref, acc_ref, blkf_ref, cend_ref, wrel_ref, pp_ref):
    g = pl.program_id(0)
    step = pl.program_id(1)
    qb = QUERY_BLOCK
    hq = NSA_HPG * qb
    nb = sel_ref.shape[1]
    ncp = kc_ref.shape[1]
    s_len = ks_ref.shape[1]
    first = step * NSA_STEP_BLOCKS

    @pl.when(step == 0)
    def _():
        lane = lax.broadcasted_iota(jnp.int32, cend_ref.shape, 1)
        cend_ref[...] = lax.broadcasted_iota(jnp.int32, cend_ref.shape, 0) * CMP_STRIDE + (CMP_BLOCK - 1) - lane
        wrel_ref[...] = (lax.broadcasted_iota(jnp.int32, wrel_ref.shape, 0)
                         - lax.broadcasted_iota(jnp.int32, wrel_ref.shape, 1))

    def rows(sub):
        return pl.ds(pl.multiple_of(sub * qb, qb), qb)

    bucket_ok = s_len % (NSA_BUCKETS * SEL_TILE) == 0 and nb // NSA_BUCKETS >= SEL_TOPK
    n_bucket = NSA_BUCKETS if bucket_ok else 1
    bucket = ((first + NSA_STEP_BLOCKS) * qb - 1) // (s_len // n_bucket)
    for b in range(n_bucket):
        nck, nbk = (b + 1) * (ncp // n_bucket), (b + 1) * (nb // n_bucket)

        @pl.when(bucket == b)
        def _():
            def scores(sub, carry):
                own = pl.ds(pl.multiple_of(sub * hq, hq), hq)
                _nsa_scores(g, first + sub, q_ref.at[rows(sub), :], gt_ref.at[:, rows(sub)], kc_ref, vct_ref, kw_ref,
                            vwt_ref, qa_ref.at[:, own], sel_ref.at[sub], part_ref.at[sub], gates_ref.at[sub], pp_ref,
                            cend_ref, wrel_ref, nck, nbk)
                return carry

            lax.fori_loop(0, NSA_STEP_BLOCKS, scores, 0)
            _nsa_topk(first, sel_ref, blkf_ref, nbk)

    _nsa_select(g, step, ks_ref, vst_ref, o_ref, qa_ref, sel_ref, part_ref, gates_ref, s0_ref, s1_ref, acc_ref)


def _gate(gates_ref, g, branch):
    rows = [gates_ref[pl.ds((g * NSA_HPG + h) * N_BRANCH + branch, 1), :] for h in range(NSA_HPG)]
    return jax.nn.sigmoid(jnp.concatenate(rows, axis=1))


def _nsa_scores(g, i, q_ref, gt_ref, kc_ref, vct_ref, kw_ref, vwt_ref, qa_ref, score_ref, part_ref, gates_ref,
                pp_ref, cend_ref, wrel_ref, nck, nbk):
    qb = QUERY_BLOCK
    start = i * qb
    hq = NSA_HPG * qb
    t_row = start + lax.broadcasted_iota(jnp.int32, (1, qb), 1)
    jt = t_row // SEL_BLOCK

    qt = (q_ref[...].astype(F32) * (NSA_DIM ** -0.5 * LOG2E)).T
    q_t = jnp.concatenate([qt[h * NSA_DIM:(h + 1) * NSA_DIM] for h in range(NSA_HPG)], axis=1).astype(BF16)
    qa_ref[:NSA_DIM, :] = q_t
    qa_ref[NSA_DIM:, :] = jnp.zeros((qa_ref.shape[0] - NSA_DIM, hq), BF16)
    gates_ref[...] = gt_ref[...]

    def exps(s, bias):
        out = []
        for h in range(NSA_HPG):
            sh = s[:, h * qb:(h + 1) * qb] + bias
            out.append(jnp.exp2(sh - jnp.max(sh, axis=0, keepdims=True)))
        return out

    wk = WINDOW + qb
    k0w = pl.multiple_of(jnp.maximum(start - WINDOW, 0), qb)
    sc = _dot(kc_ref[0, :nck, :], q_t)
    sw = _dot(kw_ref[0, pl.ds(k0w, wk), :], q_t)
    e_cmp = exps(sc, jnp.where(cend_ref[:nck, :] <= start, 0.0, MASK_VALUE))
    o_cmp = _dot(vct_ref[0, :, :nck], jnp.concatenate([e.astype(BF16) for e in e_cmp], axis=1))
    wrel = wrel_ref[...]
    e_win = exps(sw, jnp.where((wrel <= start - k0w) & (wrel > start - k0w - WINDOW), 0.0, MASK_VALUE))
    o_win = _dot(vwt_ref[0, :, pl.ds(k0w, wk)], jnp.concatenate([e.astype(BF16) for e in e_win], axis=1))
    inv_cmp = jnp.where(jnp.concatenate([t_row >= CMP_BLOCK - 1] * NSA_HPG, axis=1),
                        1.0 / o_cmp[NSA_DIM:NSA_DIM + 1, :], 0.0)
    psum = e_cmp[0] * inv_cmp[:, :qb]
    for h in range(1, NSA_HPG):
        psum = psum + e_cmp[h] * inv_cmp[:, h * qb:(h + 1) * qb]
    part_ref[...] = (_gate(gates_ref, g, 0) * inv_cmp * o_cmp[:NSA_DIM]
                     + _gate(gates_ref, g, 2) * (1.0 / o_win[NSA_DIM:NSA_DIM + 1, :]) * o_win[:NSA_DIM])
    pp_ref[:SUBLANES, :] = jnp.zeros((SUBLANES, qb), F32)
    pp_ref[SUBLANES:SUBLANES + nck, :] = psum
    imp = None
    for k, w in enumerate(COV_BAND):
        tap = pp_ref[pl.ds(SUBLANES - COV_LEAD + k, nbk, stride=COV_RATIO), :]
        tap = tap if w == 1.0 else w * tap
        imp = tap if imp is None else imp + tap
    blk = lax.broadcasted_iota(jnp.int32, (nbk, qb), 0)
    forced = (blk == 0) | (blk == jt) | (blk == jt - 1)
    score_ref[:nbk, :] = jnp.where(forced, PICKED, jnp.where(blk <= jt, imp, -1.0))


def _nsa_topk(first, sel_ref, blkf_ref, nbk):
    qb = QUERY_BLOCK
    blk = lax.broadcasted_iota(jnp.int32, (nbk, qb), 0)
    blkf_ref[:nbk, :] = blk.astype(F32)

    def pick(_, scores):
        out = []
        for score in scores:
            mx = jnp.max(score, axis=0, keepdims=True)
            cand = jnp.where(score == mx, blkf_ref[:nbk, :], float(nbk))
            out.append(jnp.where(cand == jnp.min(cand, axis=0, keepdims=True), PICKED, score))
        return tuple(out)

    scores = tuple(sel_ref[b, :nbk, :] for b in range(sel_ref.shape[0]))
    scores = lax.fori_loop(0, min(SEL_TOPK, nbk) - N_FORCED, pick, scores)
    for b, score in enumerate(scores):
        jt = ((first + b) * qb + lax.broadcasted_iota(jnp.int32, (1, qb), 1)) // SEL_BLOCK
        sel_ref[b, :nbk, :] = jnp.where((score == PICKED) & (blk <= jt), 0.0, MASK_VALUE)


def _nsa_select(g, step, ks_ref, vst_ref, o_ref, qa_ref, sel_ref, part_ref, gates_ref, s0_ref, s1_ref, acc_ref):
    s_refs = (s0_ref, s1_ref)
    qb = QUERY_BLOCK
    hq = NSA_HPG * qb
    nbq = sel_ref.shape[0]
    lanes = nbq * hq
    tk = SEL_TILE

    def bias_rows(kt):
        rows = []
        for b in range(nbq):
            b8 = sel_ref[b, pl.ds(pl.multiple_of(kt * SEL_BPT, SEL_BPT), SEL_BPT), :]
            b16 = jnp.concatenate([b8, jnp.zeros((BF16_ROWS - SEL_BPT, qb), F32)], axis=0).astype(BF16)
            rows += [b16] * NSA_HPG
        qa_ref[NSA_DIM:NSA_DIM + BF16_ROWS, :] = jnp.concatenate(rows, axis=1)

    def scores(slot, kt):
        k0 = pl.multiple_of(kt * tk, tk)
        bias_rows(kt)
        s = _dot(ks_ref[0, pl.ds(k0, tk), :], qa_ref[...])
        s_refs[slot][...] = s
        return jnp.max(s, axis=0, keepdims=True)

    def update(slot, kt, mt, m, causal):
        k0 = pl.multiple_of(kt * tk, tk)
        if causal:
            tri = jnp.where(lax.broadcasted_iota(jnp.int32, (qb, qb), 0)
                            <= lax.broadcasted_iota(jnp.int32, (qb, qb), 1), 0.0, MASK_VALUE)
            for b in range(nbq):
                own = (slice(b * qb, (b + 1) * qb), slice(b * hq, (b + 1) * hq))
                s_refs[slot][own] = s_refs[slot][own] + jnp.concatenate([tri] * NSA_HPG, axis=1)
            mt = jnp.max(s_refs[slot][...], axis=0, keepdims=True)
        s = s_refs[slot][...]
        m_new = jnp.maximum(m, mt)
        p = jnp.exp2(s - m_new).astype(BF16)
        acc_ref[...] = jnp.exp2(m - m_new) * acc_ref[...] + _dot(vst_ref[0, :, pl.ds(k0, tk)], p)
        return m_new

    def accumulate(pend):
        p, k0, alpha, ln = pend
        acc_ref[:, ln] = alpha * acc_ref[:, ln] + _dot(vst_ref[0, :, pl.ds(k0, tk)], p)

    def run(first, n_tiles, carry):
        mt, m = carry
        pend = None
        for k in range(n_tiles):
            cur, nxt = s_refs[k % 2], s_refs[(k + 1) % 2]
            k0c = pl.multiple_of((first + k) * tk, tk)
            k0n = pl.multiple_of((first + k + 1) * tk, tk)
            bias_rows(first + k + 1)
            m_new = jnp.maximum(m, mt)
            alpha = jnp.exp2(m - m_new)
            mts = []
            for n in range(lanes // SEL_LANES):
                ln = slice(n * SEL_LANES, (n + 1) * SEL_LANES)
                s_n = _dot(ks_ref[0, pl.ds(k0n, tk), :], qa_ref[:, ln])
                nxt[:, ln] = s_n
                mts.append(jnp.max(s_n, axis=0, keepdims=True))
                p = jnp.exp2(cur[:, ln] - m_new[:, ln]).astype(BF16)
                if pend is not None:
                    accumulate(pend)
                pend = (p, k0c, alpha[:, ln], ln)
            mt = jnp.concatenate(mts, axis=1)
            m = m_new
        accumulate(pend)
        return mt, m

    acc_ref[...] = jnp.zeros(acc_ref.shape, F32)
    last = step
    carry = (scores(0, 0), jnp.full((1, lanes), MASK_VALUE, F32))
    done = 0
    for unroll in SEL_UNROLLS:
        trips = (last - done) // unroll
        carry = lax.fori_loop(0, trips, lambda it, c, d=done, u=unroll: run(d + it * u, u, c), carry)
        done = done + trips * unroll
    mt_a, m = carry

    @pl.when(last % 2 == 0)
    def _():
        update(0, last, mt_a, m, True)

    @pl.when(last % 2 == 1)
    def _():
        mt_b, m_b = run(last - 1, 1, (mt_a, m))
        update(1, last, mt_b, m_b, True)

    for b in range(nbq):
        own = slice(b * hq, (b + 1) * hq)
        o_sel = acc_ref[:NSA_DIM, own] * (1.0 / acc_ref[NSA_DIM:NSA_DIM + 1, own])
        out_t = part_ref[b] + _gate(gates_ref.at[b], g, 1) * o_sel
        outs = [out_t[:, h * qb:(h + 1) * qb] for h in range(NSA_HPG)]
        o_ref[b * qb:(b + 1) * qb, :] = jnp.concatenate(outs, axis=0).T.astype(o_ref.dtype)


def _nsa(q, gates_t, kc, vc_aug_t, ks_aug, vs_aug_t, kw, vw_aug_t):
    s = q.shape[0]
    assert s % (2 * SEL_TILE) == 0 and s >= WINDOW + QUERY_BLOCK
    gdim = NSA_HPG * NSA_DIM
    ncp = kc.shape[1]
    assert ncp * CMP_STRIDE == s and s // SEL_BLOCK * COV_RATIO == ncp
    nb = s // SEL_BLOCK
    ka = ks_aug.shape[-1]
    va = vs_aug_t.shape[1]
    hq = NSA_HPG * QUERY_BLOCK
    per_group = lambda shape: pl.BlockSpec((1,) + shape, lambda g, i: (g, 0, 0))
    rows = NSA_STEP_BLOCKS * QUERY_BLOCK
    assert s % rows == 0
    return pl.pallas_call(
        _nsa_kernel,
        grid=(NSA_GROUPS, s // rows),
        in_specs=[
            pl.BlockSpec((rows, gdim), lambda g, i: (i, g)),
            pl.BlockSpec((GATE_ROWS, rows), lambda g, i: (0, i)),
            per_group((ncp, NSA_DIM)),
            per_group((va, ncp)),
            per_group((s, ka)),
            per_group((va, s)),
            per_group((s, NSA_DIM)),
            per_group((va, s)),
        ],
        out_specs=pl.BlockSpec((rows, gdim), lambda g, i: (i, g)),
        out_shape=jax.ShapeDtypeStruct((s, NSA_HEADS * NSA_DIM), BF16),
        scratch_shapes=[pltpu.VMEM((ka, NSA_STEP_BLOCKS * hq), BF16),
                        pltpu.VMEM((NSA_STEP_BLOCKS, nb, QUERY_BLOCK), F32),
                        pltpu.VMEM((NSA_STEP_BLOCKS, NSA_DIM, hq), F32),
                        pltpu.VMEM((NSA_STEP_BLOCKS, GATE_ROWS, QUERY_BLOCK), F32),
                        pltpu.VMEM((SEL_TILE, NSA_STEP_BLOCKS * hq), F32),
                        pltpu.VMEM((SEL_TILE, NSA_STEP_BLOCKS * hq), F32),
                        pltpu.VMEM((va, NSA_STEP_BLOCKS * hq), F32),
                        pltpu.VMEM((nb, QUERY_BLOCK), F32),
                        pltpu.VMEM((ncp, QUERY_BLOCK), jnp.int32),
                        pltpu.VMEM((WINDOW + QUERY_BLOCK, QUERY_BLOCK), jnp.int32),
                        pltpu.VMEM((SUBLANES + ncp, QUERY_BLOCK), F32)],
        compiler_params=_params("parallel", "arbitrary"),
        name="nsa",
    )(q, gates_t, kc, vc_aug_t, ks_aug, vs_aug_t, kw, vw_aug_t)


def _sgu_kernel(u_ref, v_ref, lng_ref, lnb_ref, ws_ref, bs_ref, o_ref):
    c = SGU_CHUNK
    tm = u_ref.shape[0]
    v = jax.nn.gelu(v_ref[...])
    mu = jnp.mean(v, axis=-1, keepdims=True)
    var = jnp.mean(jnp.square(v - mu), axis=-1, keepdims=True)
    vn = ((v - mu) * lax.rsqrt(var + NORM_EPS) * lng_ref[...] + lnb_ref[...]).astype(BF16)
    u = jax.nn.gelu(u_ref[...])
    tri = lax.broadcasted_iota(jnp.int32, (c, c), 0) >= lax.broadcasted_iota(jnp.int32, (c, c), 1)
    for g in range(SGU_GROUPS):
        w = jnp.where(tri, ws_ref[g], 0.0).astype(BF16)
        cols = slice(g * c, (g + 1) * c)
        rhs = jnp.concatenate([vn[k * c:(k + 1) * c, cols] for k in range(tm // c)], axis=1)
        mixed = _dot(w, rhs)
        for k in range(tm // c):
            rows = slice(k * c, (k + 1) * c)
            o_ref[rows, cols] = (u[rows, cols] * (mixed[:, rows] + bs_ref[g])).astype(o_ref.dtype)


def _sgu(proj, lng, lnb, ws, bs, tm):
    s = proj.shape[0]
    w = lng.shape[0]
    c = SGU_CHUNK
    bs_b = jnp.broadcast_to(bs[:, :, None], (SGU_GROUPS, c, c))
    return pl.pallas_call(
        _sgu_kernel,
        grid=(s // tm,),
        in_specs=[
            pl.BlockSpec((tm, w), lambda i: (i, 0)),
            pl.BlockSpec((tm, w), lambda i: (i, 1)),
            pl.BlockSpec((1, w), lambda i: (0, 0)),
            pl.BlockSpec((1, w), lambda i: (0, 0)),
            pl.BlockSpec((SGU_GROUPS, c, c), lambda i: (0, 0, 0)),
            pl.BlockSpec((SGU_GROUPS, c, c), lambda i: (0, 0, 0)),
        ],
        out_specs=pl.BlockSpec((tm, w), lambda i: (i, 0)),
        out_shape=jax.ShapeDtypeStruct((s, w), BF16),
        compiler_params=_params("parallel"),
        name="sgu",
    )(proj, proj, lng.reshape(1, w), lnb.reshape(1, w), ws, bs_b)


def _mix_kernel(oa_ref, ob_ref, ga_ref, gb_ref, x_ref, pa_ref, pb_ref, wo_ref,
                gc_ref, wq_ref, mkt_ref, mv_ref, wmo_ref, o_ref):
    a = _dot(oa_ref[...], pa_ref[...])
    b = _dot(ob_ref[...], pb_ref[...])
    merged = jax.nn.sigmoid(ga_ref[...]) * a + jax.nn.sigmoid(gb_ref[...]) * b
    x = x_ref[...] + _dot(merged.astype(BF16), wo_ref[...])
    h = _rms(x, gc_ref[...]).astype(BF16)
    mq = (_dot(h, wq_ref[...]) * (MEM_DIM ** -0.5)).astype(BF16)
    outs = []
    for hh in range(MEM_HEADS):
        s = _dot(mq[:, hh * MEM_DIM:(hh + 1) * MEM_DIM], mkt_ref[hh])
        e = jnp.exp(s - jnp.max(s, axis=-1, keepdims=True))
        p = e / jnp.sum(e, axis=-1, keepdims=True)
        outs.append(_dot(p.astype(BF16), mv_ref[hh]).astype(BF16))
    o_ref[...] = x + _dot(jnp.concatenate(outs, axis=1), wmo_ref[...])


def _mix(o_a, o_b, proj, x, p_a, p_b, w_o, gate_block, g_cross, w_q, mk_t, mv, w_mo, tm):
    s, d = x.shape
    wa = o_a.shape[1]
    wb = o_b.shape[1]
    mw = w_q.shape[1]
    m = mv.shape[1]
    resident = lambda shape: pl.BlockSpec(shape, lambda i: (0,) * len(shape), pipeline_mode=pl.Buffered(1))
    return pl.pallas_call(
        _mix_kernel,
        grid=(s // tm,),
        in_specs=[
            pl.BlockSpec((tm, wa), lambda i: (i, 0)),
            pl.BlockSpec((tm, wb), lambda i: (i, 0)),
            pl.BlockSpec((tm, d), lambda i: (i, gate_block)),
            pl.BlockSpec((tm, d), lambda i: (i, gate_block + 1)),
            pl.BlockSpec((tm, d), lambda i: (i, 0)),
            resident((wa, d)),
            resident((wb, d)),
            resident((d, d)),
            resident((1, d)),
            resident((d, mw)),
            resident((MEM_HEADS, MEM_DIM, m)),
            resident((MEM_HEADS, m, MEM_DIM)),
            resident((mw, d)),
        ],
        out_specs=pl.BlockSpec((tm, d), lambda i: (i, 0)),
        out_shape=jax.ShapeDtypeStruct((s, d), F32),
        compiler_params=_params("parallel"),
        name="mix",
    )(o_a, o_b, proj, proj, x, p_a, p_b, w_o, g_cross.reshape(1, d), w_q, mk_t, mv, w_mo)


def _ffn_kernel(x_ref, g_ref, wg_ref, wu_ref, wo_ref, gf_ref, o_ref, h_ref, *, final_norm):
    j = pl.program_id(1)
    row_chunks = [slice(r, r + FFN_NORM_ROWS) for r in range(0, x_ref.shape[0], FFN_NORM_ROWS)]

    @pl.when(j == 0)
    def _():
        for rs in row_chunks:
            x = x_ref[rs, :]
            h_ref[rs, :] = _rms(x, g_ref[...]).astype(BF16)
            o_ref[rs, :] = x

    h = h_ref[...]
    th = wg_ref.shape[1]
    d = o_ref.shape[1]
    for c in range(th // FFN_CHUNK):
        cs = slice(c * FFN_CHUNK, (c + 1) * FFN_CHUNK)
        act = (jax.nn.silu(_dot(h, wg_ref[:, cs])) * _dot(h, wu_ref[:, cs])).astype(BF16)
        for n in range(d // FFN_OUT_TN):
            ns = slice(n * FFN_OUT_TN, (n + 1) * FFN_OUT_TN)
            o_ref[:, ns] += _dot(act, wo_ref[cs, ns])

    if final_norm:
        @pl.when(j == pl.num_programs(1) - 1)
        def _():
            for rs in row_chunks:
                o_ref[rs, :] = _rms(o_ref[rs, :], gf_ref[...])


def _ffn(x, g, w_in, w_out, g_final, final_norm, tm, th):
    s, d = x.shape
    hidden = w_out.shape[0]
    nh = hidden // th
    return pl.pallas_call(
        functools.partial(_ffn_kernel, final_norm=final_norm),
        grid=(s // tm, nh),
        in_specs=[
            pl.BlockSpec((tm, d), lambda i, j: (i, 0)),
            pl.BlockSpec((1, d), lambda i, j: (0, 0)),
            pl.BlockSpec((d, th), lambda i, j: (0, j)),
            pl.BlockSpec((d, th), lambda i, j: (0, j + nh)),
            pl.BlockSpec((th, d), lambda i, j: (j, 0)),
            pl.BlockSpec((1, d), lambda i, j: (0, 0)),
        ],
        out_specs=pl.BlockSpec((tm, d), lambda i, j: (i, 0)),
        out_shape=jax.ShapeDtypeStruct((s, d), F32),
        scratch_shapes=[pltpu.VMEM((tm, d), BF16)],
        compiler_params=_params("parallel", "arbitrary"),
        name="ffn",
    )(x, g.reshape(1, d), w_in, w_in, w_out, g_final.reshape(1, d))


def _layer(x, mem, norm_mix_g, w_in, cmp_pe_k, cmp_k_w1, cmp_k_b1, cmp_k_w2, cmp_pe_v, cmp_v_w1, cmp_v_b1, cmp_v_w2,
           sgu_ln_g, sgu_ln_b, sgu_ws, sgu_b, w_proj_a, w_proj_b, w_mix_out, norm_cross_g, norm_mem_g,
           w_mq, w_mkv, w_mo, norm_ffn_g):
    s, d = x.shape
    qw = NSA_HEADS * NSA_DIM
    kvw = NSA_GROUPS * NSA_DIM
    sguw = sgu_ln_g.shape[0]
    ngate = GATE_ROWS
    o_kv = qw
    o_gate = o_kv + N_KV_STREAMS * kvw
    o_u = o_gate + ngate

    w_a = w_in[:, :o_gate + PROJ_GATE_TN].astype(BF16)
    w_b = w_in[:, o_u:].astype(BF16)
    nat, kw, ks_aug, vs_aug_t, vw_aug_t, q, proj_b, gates_t = _proj(x, norm_mix_g, w_a, w_b, min(PROJ_TM, s))

    w1 =jnp.stack([cmp_k_w1, cmp_v_w1]).astype(BF16)
    pe = jnp.stack([cmp_pe_k.reshape(1, -1), cmp_pe_v.reshape(1, -1)])
    b1 = jnp.stack([cmp_k_b1.reshape(1, -1), cmp_v_b1.reshape(1, -1)])
    w2 = jnp.stack([cmp_k_w2, cmp_v_w2]).astype(BF16)
    cmp, cmp_t = _compress(nat, w1, pe, b1, w2)
    o_a = _nsa(q, gates_t, cmp[0], cmp_t[1], ks_aug, vs_aug_t, kw, vw_aug_t)

    o_b = _sgu(proj_b, sgu_ln_g, sgu_ln_b, sgu_ws, sgu_b, min(SGU_TM, s))

    m = mem.shape[0]
    mw = MEM_HEADS * MEM_DIM
    mkv = _norm_matmul(mem, norm_mem_g, w_mkv.astype(BF16), F32, m, mw)
    mk_t = mkv[:, :mw].reshape(m, MEM_HEADS, MEM_DIM).transpose(1, 2, 0).astype(BF16)
    mv = mkv[:, mw:].reshape(m, MEM_HEADS, MEM_DIM).transpose(1, 0, 2).astype(BF16)
    return _mix(o_a, o_b, proj_b, x, w_proj_a.astype(BF16), w_proj_b.astype(BF16), w_mix_out.astype(BF16),
                (2 * sguw) // d, norm_cross_g, w_mq.astype(BF16), mk_t, mv, w_mo.astype(BF16), min(MIX_TM, s))


def kernel(x, mem, norm_mix_g, w_in, cmp_pe_k, cmp_k_w1, cmp_k_b1, cmp_k_w2, cmp_pe_v, cmp_v_w1, cmp_v_b1, cmp_v_w2, sgu_ln_g, sgu_ln_b, sgu_ws, sgu_b, w_proj_a, w_proj_b, w_mix_out, norm_cross_g, norm_mem_g, w_mq, w_mkv, w_mo, norm_ffn_g, w_ffn_in, w_ffn_out, norm_final_g):
    b, s, d = x.shape
    depth = w_in.shape[0]
    outs = []
    for bi in range(b):
        xb = x[bi]
        for l in range(depth):
            last = l == depth - 1
            xb = _layer(xb, mem[bi], norm_mix_g[l], w_in[l], cmp_pe_k[l], cmp_k_w1[l], cmp_k_b1[l], cmp_k_w2[l],
                        cmp_pe_v[l], cmp_v_w1[l], cmp_v_b1[l], cmp_v_w2[l], sgu_ln_g[l], sgu_ln_b[l], sgu_ws[l],
                        sgu_b[l], w_proj_a[l], w_proj_b[l], w_mix_out[l], norm_cross_g[l], norm_mem_g[l],
                        w_mq[l], w_mkv[l], w_mo[l], norm_ffn_g[l])
            xb = _ffn(xb, norm_ffn_g[l], w_ffn_in[l].astype(BF16), w_ffn_out[l].astype(BF16), norm_final_g, last,
                      min(FFN_TM, s), FFN_TH)
        outs.append(xb)
    return jnp.stack(outs)
```

```python
import functools
import math

import jax
import jax.numpy as jnp
from jax import lax
from jax.experimental import pallas as pl
from jax.experimental.pallas import tpu as pltpu

F32 = jnp.float32
BF16 = jnp.bfloat16

NORM_EPS = 1e-6
MASK_VALUE = -1e30
N_FORCED = 3
N_BRANCH = 3
LOG2E = math.log2(math.e)

NSA_HEADS = 16
NSA_GROUPS = 4
NSA_HPG = NSA_HEADS // NSA_GROUPS
NSA_DIM = 64
GATE_ROWS = NSA_HEADS * N_BRANCH
N_KV_STREAMS = 6
CMP_BLOCK = 32
CMP_STRIDE = 16
SEL_BLOCK = 64
SEL_TOPK = 16
WINDOW = 512
QUERY_BLOCK = 128
SGU_GROUPS = 8
SGU_CHUNK = 128
MEM_HEADS = 4
MEM_DIM = 128

LANES = 128
SUBLANES = 8
COV_RATIO = SEL_BLOCK // CMP_STRIDE
COV_LEAD = CMP_BLOCK // CMP_STRIDE - 1
COV_BAND = tuple(
    max(min((k - COV_LEAD) * CMP_STRIDE + CMP_BLOCK, SEL_BLOCK) - max((k - COV_LEAD) * CMP_STRIDE, 0), 0) / CMP_BLOCK
    for k in range(COV_RATIO + COV_LEAD))
assert SEL_BLOCK % CMP_STRIDE == 0 and CMP_BLOCK % CMP_STRIDE == 0 and COV_LEAD <= SUBLANES
SEL_TILE = 512
SEL_BPT = SEL_TILE // SEL_BLOCK
SEL_LANES = 256
SEL_UNROLLS = (4, 2)
NSA_STEP_BLOCKS = SEL_TILE // QUERY_BLOCK
NSA_BUCKETS = 8
BF16_ROWS = 16
FFN_CHUNK = 256
FFN_OUT_TN = 1024
FFN_NORM_ROWS = 256
PROJ_TM, SGU_TM, MIX_TM, FFN_TM, FFN_TH = 1024, 512, 256, 1024, 512
VMEM_LIMIT = 62 * 1024 * 1024


def _params(*sem):
    return pltpu.CompilerParams(dimension_semantics=sem, vmem_limit_bytes=VMEM_LIMIT)


def _rms(x, g):
    return x * lax.rsqrt(jnp.mean(x * x, axis=-1, keepdims=True) + NORM_EPS) * g


def _dot(a, b):
    return jnp.dot(a, b, preferred_element_type=F32)


def _ones_row(n):
    return jnp.where(lax.broadcasted_iota(jnp.int32, (BF16_ROWS, n), 0) == 0, 1.0, 0.0).astype(BF16)


def _norm_matmul_kernel(x_ref, g_ref, w_ref, o_ref, h_ref):
    @pl.when(pl.program_id(1) == 0)
    def _():
        h_ref[...] = _rms(x_ref[...], g_ref[...]).astype(BF16)

    o_ref[...] = _dot(h_ref[...], w_ref[...]).astype(o_ref.dtype)


def _norm_matmul(x, g, w, out_dtype, tm, tn):
    s, d = x.shape
    n = w.shape[1]
    return pl.pallas_call(
        _norm_matmul_kernel,
        grid=(s // tm, n // tn),
        in_specs=[
            pl.BlockSpec((tm, d), lambda i, j: (i, 0)),
            pl.BlockSpec((1, d), lambda i, j: (0, 0)),
            pl.BlockSpec((d, tn), lambda i, j: (0, j)),
        ],
        out_specs=pl.BlockSpec((tm, tn), lambda i, j: (i, j)),
        out_shape=jax.ShapeDtypeStruct((s, n), out_dtype),
        scratch_shapes=[pltpu.VMEM((tm, d), BF16)],
        compiler_params=_params("parallel", "arbitrary"),
        name="norm_matmul",
    )(x, g.reshape(1, d), w)


PROJ_TN = 512
PROJ_GATE_TN = 256


def _proj_kernel(x_ref, g_ref, wa_ref, wb_ref, nat_ref, kw_ref, ksa_ref, vsa_ref, vwa_ref, q_ref, o_ref, gt_ref,
                 h_ref, stage_ref, *, attn_step):
    i = pl.program_id(0)
    j = pl.program_id(1)
    tm = x_ref.shape[0]
    dh = NSA_DIM
    ng = NSA_GROUPS
    tn = PROJ_TN
    qw = q_ref.shape[1]

    @pl.when(j == 0)
    def _():
        h_ref[...] = _rms(x_ref[...], g_ref[...]).astype(BF16)

    @pl.when(j != attn_step)
    def _():
        o_ref[...] = _dot(h_ref[...], wb_ref[...])

    @pl.when(j == attn_step)
    def _():
        h = h_ref[...]

        def tile(t):
            return _dot(h, wa_ref[:, qw + t * tn:qw + (t + 1) * tn])

        def cols(res, c):
            return res[:, c * dh:(c + 1) * dh].astype(BF16)

        def cols_t(res, c):
            slab = res[:, (c // 2) * 2 * dh:(c // 2 + 1) * 2 * dh].T
            return slab[(c % 2) * dh:(c % 2 + 1) * dh].astype(BF16)

        res = tile(0)
        for c in range(2 * ng):
            stage_ref[...] = res[:, c * dh:(c + 1) * dh]
            for t in range(0, CMP_STRIDE, 2):
                pair = [stage_ref[pl.ds(t + u, tm // CMP_STRIDE, stride=CMP_STRIDE), :] for u in range(2)]
                nat_ref[c // ng, c % ng, :, t * dh:(t + 2) * dh] = jnp.concatenate(pair, axis=1).astype(BF16)

        res = tile(1)
        pos = i * tm + lax.broadcasted_iota(jnp.int32, (tm, dh), 0)
        lane = lax.broadcasted_iota(jnp.int32, (tm, dh), 1)
        onehot = jnp.where((pos // SEL_BLOCK) % SEL_BPT == lane, 1.0, 0.0).astype(BF16)
        for g in range(ng):
            ksa_ref[g, :, :dh] = cols(res, g)
            ksa_ref[g, :, dh:] = onehot
            vsa_ref[g, :dh, :] = cols_t(res, ng + g)
            vsa_ref[g, dh:, :] = _ones_row(tm)

        res = tile(2)
        for g in range(ng):
            kw_ref[g] = cols(res, g)
            vwa_ref[g, :dh, :] = cols_t(res, ng + g)
            vwa_ref[g, dh:, :] = _ones_row(tm)

        o_gate = qw + N_KV_STREAMS // 2 * tn
        res = _dot(h, wa_ref[:, o_gate:o_gate + PROJ_GATE_TN])
        gt_ref[...] = res[:, :LANES].T

        for t in range(qw // tn):
            q_ref[:, t * tn:(t + 1) * tn] = _dot(h, wa_ref[:, t * tn:(t + 1) * tn]).astype(BF16)


def _proj(x, g, w_a, w_b, tm):
    s, d = x.shape
    dh, ng, tn = NSA_DIM, NSA_GROUPS, PROJ_TN
    qw = NSA_HEADS * dh
    assert 2 * ng * dh == tn and qw % tn == 0 and w_a.shape[1] == qw + N_KV_STREAMS // 2 * tn + PROJ_GATE_TN
    n = w_b.shape[1]
    assert n % tn == 0 and GATE_ROWS <= LANES <= PROJ_GATE_TN
    nb = n // tn
    a = nb // 2
    n_rows = s // tm
    row = lambda i, j: jnp.minimum(jnp.where(j > a, i + 1, i), n_rows - 1)
    slab = lambda j: jnp.where(j > a, j - 1, j)
    shapes = [
        jax.ShapeDtypeStruct((2, ng, s // CMP_STRIDE, CMP_STRIDE * dh), BF16),
        jax.ShapeDtypeStruct((ng, s, dh), BF16),
        jax.ShapeDtypeStruct((ng, s, 2 * dh), BF16),
        jax.ShapeDtypeStruct((ng, dh + BF16_ROWS, s), BF16),
        jax.ShapeDtypeStruct((ng, dh + BF16_ROWS, s), BF16),
        jax.ShapeDtypeStruct((s, NSA_HEADS * dh), BF16),
        jax.ShapeDtypeStruct((s, n), F32),
        jax.ShapeDtypeStruct((LANES, s), F32),
    ]
    return pl.pallas_call(
        functools.partial(_proj_kernel, attn_step=a),
        grid=(n_rows, 1 + nb),
        in_specs=[
            pl.BlockSpec((tm, d), lambda i, j: (row(i, j), 0)),
            pl.BlockSpec((1, d), lambda i, j: (0, 0)),
            pl.BlockSpec(w_a.shape, lambda i, j: (0, 0), pipeline_mode=pl.Buffered(1)),
            pl.BlockSpec((d, tn), lambda i, j: (0, slab(j))),
        ],
        out_specs=[
            pl.BlockSpec((2, ng, tm // CMP_STRIDE, CMP_STRIDE * dh), lambda i, j: (0, 0, row(i, j), 0)),
            pl.BlockSpec((ng, tm, dh), lambda i, j: (0, row(i, j), 0)),
            pl.BlockSpec((ng, tm, 2 * dh), lambda i, j: (0, row(i, j), 0)),
            pl.BlockSpec((ng, dh + BF16_ROWS, tm), lambda i, j: (0, 0, row(i, j))),
            pl.BlockSpec((ng, dh + BF16_ROWS, tm), lambda i, j: (0, 0, row(i, j))),
            pl.BlockSpec((tm, qw), lambda i, j: (row(i, j), 0)),
            pl.BlockSpec((tm, tn), lambda i, j: (i, slab(j))),
            pl.BlockSpec((LANES, tm), lambda i, j: (0, row(i, j))),
        ],
        out_shape=shapes,
        scratch_shapes=[pltpu.VMEM((tm, d), BF16), pltpu.VMEM((tm, dh), F32)],
        compiler_params=_params("arbitrary", "arbitrary"),
        name="proj",
    )(x, g.reshape(1, d), w_a, w_b)


def _compress_kernel(r_ref, w1_ref, pe_ref, b1_ref, w2_ref, o_ref, ot_ref):
    half = r_ref.shape[-1]
    r = r_ref[0, 0]
    bias = _dot(pe_ref[0].astype(BF16), w1_ref[0]) + b1_ref[0]
    top = _dot(r, w1_ref[0, :half, :])
    bot = _dot(r, w1_ref[0, half:, :])
    ncp = r.shape[0]
    hid = top + pltpu.roll(bot, ncp - 1, 0) + bias
    out = _dot(jax.nn.gelu(hid).astype(BF16), w2_ref[0])
    o_ref[0, 0] = out.astype(o_ref.dtype)
    dh = out.shape[1]
    out_t = jnp.concatenate([out, jnp.zeros((ncp, LANES - dh), F32)], axis=1).T
    ot_ref[0, 0, :dh, :] = out_t[:dh].astype(ot_ref.dtype)
    ot_ref[0, 0, dh:, :] = _ones_row(ncp)


def _compress(r, w1, pe, b1, w2):
    _, g, ncp, half = r.shape
    hidden = w1.shape[-1]
    dh = w2.shape[-1]
    return pl.pallas_call(
        _compress_kernel,
        grid=(2, g),
        in_specs=[
            pl.BlockSpec((1, 1, ncp, half), lambda s, gg: (s, gg, 0, 0)),
            pl.BlockSpec((1, 2 * half, hidden), lambda s, gg: (s, 0, 0)),
            pl.BlockSpec((1, 1, 2 * half), lambda s, gg: (s, 0, 0)),
            pl.BlockSpec((1, 1, hidden), lambda s, gg: (s, 0, 0)),
            pl.BlockSpec((1, hidden, dh), lambda s, gg: (s, 0, 0)),
        ],
        out_specs=[pl.BlockSpec((1, 1, ncp, dh), lambda s, gg: (s, gg, 0, 0)),
                   pl.BlockSpec((1, 1, dh + BF16_ROWS, ncp), lambda s, gg: (s, gg, 0, 0))],
        out_shape=[jax.ShapeDtypeStruct((2, g, ncp, dh), BF16),
                   jax.ShapeDtypeStruct((2, g, dh + BF16_ROWS, ncp), BF16)],
        compiler_params=_params("parallel", "parallel"),
        name="compress",
    )(r, w1, pe, b1, w2)


PICKED = -2.0


def _nsa_kernel(q_ref, gt_ref, kc_ref, vct_ref, ks_ref, vst_ref, kw_ref, vwt_ref, o_ref,
                qa_ref, sel_ref, part_ref, gates_ref, s0_ref, s1_ref, acc_ref, blkf_ref, cend_ref, wrel_ref, pp_ref):
    g = pl.program_id(0)
    step = pl.program_id(1)
    qb = QUERY_BLOCK
    hq = NSA_HPG * qb
    nb = sel_ref.shape[1]
    ncp = kc_ref.shape[1]
    s_len = ks_ref.shape[1]
    first = step * NSA_STEP_BLOCKS

    @pl.when(step == 0)
    def _():
        lane = lax.broadcasted_iota(jnp.int32, cend_ref.shape, 1)
        cend_ref[...] = lax.broadcasted_iota(jnp.int32, cend_ref.shape, 0) * CMP_STRIDE + (CMP_BLOCK - 1) - lane
        wrel_ref[...] = (lax.broadcasted_iota(jnp.int32, wrel_ref.shape, 0)
                         - lax.broadcasted_iota(jnp.int32, wrel_ref.shape, 1))

    def rows(sub):
        return pl.ds(pl.multiple_of(sub * qb, qb), qb)

    bucket_ok = s_len % (NSA_BUCKETS * SEL_TILE) == 0 and nb // NSA_BUCKETS >= SEL_TOPK
    n_bucket = NSA_BUCKETS if bucket_ok else 1
    bucket = ((first + NSA_STEP_BLOCKS) * qb - 1) // (s_len // n_bucket)
    for b in range(n_bucket):
        nck, nbk = (b + 1) * (ncp // n_bucket), (b + 1) * (nb // n_bucket)

        @pl.when(bucket == b)
        def _():
            def scores(sub, carry):
                own = pl.ds(pl.multiple_of(sub * hq, hq), hq)
                _nsa_scores(g, first + sub, q_ref.at[rows(sub), :], gt_ref.at[:, rows(sub)], kc_ref, vct_ref, kw_ref,
                            vwt_ref, qa_ref.at[:, own], sel_ref.at[sub], part_ref.at[sub], gates_ref.at[sub], pp_ref,
                            cend_ref, wrel_ref, nck, nbk)
                return carry

            lax.fori_loop(0, NSA_STEP_BLOCKS, scores, 0)
            _nsa_topk(first, sel_ref, blkf_ref, nbk)

    _nsa_select(g, step, ks_ref, vst_ref, o_ref, qa_ref, sel_ref, part_ref, gates_ref, s0_ref, s1_ref, acc_ref)


def _gate(gates_ref, g, branch):
    rows = [gates_ref[pl.ds((g * NSA_HPG + h) * N_BRANCH + branch, 1), :] for h in range(NSA_HPG)]
    return jax.nn.sigmoid(jnp.concatenate(rows, axis=1))


def _nsa_scores(g, i, q_ref, gt_ref, kc_ref, vct_ref, kw_ref, vwt_ref, qa_ref, score_ref, part_ref, gates_ref,
                pp_ref, cend_ref, wrel_ref, nck, nbk):
    qb = QUERY_BLOCK
    start = i * qb
    hq = NSA_HPG * qb
    t_row = start + lax.broadcasted_iota(jnp.int32, (1, qb), 1)
    jt = t_row // SEL_BLOCK

    qt = (q_ref[...].astype(F32) * (NSA_DIM ** -0.5 * LOG2E)).T
    q_t = jnp.concatenate([qt[h * NSA_DIM:(h + 1) * NSA_DIM] for h in range(NSA_HPG)], axis=1).astype(BF16)
    qa_ref[:NSA_DIM, :] = q_t
    qa_ref[NSA_DIM:, :] = jnp.zeros((qa_ref.shape[0] - NSA_DIM, hq), BF16)
    gates_ref[...] = gt_ref[...]

    def exps(s, bias):
        out = []
        for h in range(NSA_HPG):
            sh = s[:, h * qb:(h + 1) * qb] + bias
            out.append(jnp.exp2(sh - jnp.max(sh, axis=0, keepdims=True)))
        return out

    wk = WINDOW + qb
    k0w = pl.multiple_of(jnp.maximum(start - WINDOW, 0), qb)
    sc = _dot(kc_ref[0, :nck, :], q_t)
    sw = _dot(kw_ref[0, pl.ds(k0w, wk), :], q_t)
    e_cmp = exps(sc, jnp.where(cend_ref[:nck, :] <= start, 0.0, MASK_VALUE))
    o_cmp = _dot(vct_ref[0, :, :nck], jnp.concatenate([e.astype(BF16) for e in e_cmp], axis=1))
    wrel = wrel_ref[...]
    e_win = exps(sw, jnp.where((wrel <= start - k0w) & (wrel > start - k0w - WINDOW), 0.0, MASK_VALUE))
    o_win = _dot(vwt_ref[0, :, pl.ds(k0w, wk)], jnp.concatenate([e.astype(BF16) for e in e_win], axis=1))
    inv_cmp = jnp.where(jnp.concatenate([t_row >= CMP_BLOCK - 1] * NSA_HPG, axis=1),
                        1.0 / o_cmp[NSA_DIM:NSA_DIM + 1, :], 0.0)
    psum = e_cmp[0] * inv_cmp[:, :qb]
    for h in range(1, NSA_HPG):
        psum = psum + e_cmp[h] * inv_cmp[:, h * qb:(h + 1) * qb]
    part_ref[...] = (_gate(gates_ref, g, 0) * inv_cmp * o_cmp[:NSA_DIM]
                     + _gate(gates_ref, g, 2) * (1.0 / o_win[NSA_DIM:NSA_DIM + 1, :]) * o_win[:NSA_DIM])
    pp_ref[:SUBLANES, :] = jnp.zeros((SUBLANES, qb), F32)
    pp_ref[SUBLANES:SUBLANES + nck, :] = psum
    imp = None
    for k, w in enumerate(COV_BAND):
        tap = pp_ref[pl.ds(SUBLANES - COV_LEAD + k, nbk, stride=COV_RATIO), :]
        tap = tap if w == 1.0 else w * tap
        imp = tap if imp is None else imp + tap
    blk = lax.broadcasted_iota(jnp.int32, (nbk, qb), 0)
    forced = (blk == 0) | (blk == jt) | (blk == jt - 1)
    score_ref[:nbk, :] = jnp.where(forced, PICKED, jnp.where(blk <= jt, imp, -1.0))


def _nsa_topk(first, sel_ref, blkf_ref, nbk):
    qb = QUERY_BLOCK
    blk = lax.broadcasted_iota(jnp.int32, (nbk, qb), 0)
    blkf_ref[:nbk, :] = blk.astype(F32)

    def pick(_, scores):
        out = []
        for score in scores:
            mx = jnp.max(score, axis=0, keepdims=True)
            cand = jnp.where(score == mx, blkf_ref[:nbk, :], float(nbk))
            out.append(jnp.where(cand == jnp.min(cand, axis=0, keepdims=True), PICKED, score))
        return tuple(out)

    scores = tuple(sel_ref[b, :nbk, :] for b in range(sel_ref.shape[0]))
    scores = lax.fori_loop(0, min(SEL_TOPK, nbk) - N_FORCED, pick, scores)
    for b, score in enumerate(scores):
        jt = ((first + b) * qb + lax.broadcasted_iota(jnp.int32, (1, qb), 1)) // SEL_BLOCK
        sel_ref[b, :nbk, :] = jnp.where((score == PICKED) & (blk <= jt), 0.0, MASK_VALUE)


def _nsa_select(g, step, ks_ref, vst_ref, o_ref, qa_ref, sel_ref, part_ref, gates_ref, s0_ref, s1_ref, acc_ref):
    s_refs = (s0_ref, s1_ref)
    qb = QUERY_BLOCK
    hq = NSA_HPG * qb
    nbq = sel_ref.shape[0]
    lanes = nbq * hq
    tk = SEL_TILE

    def bias_rows(kt):
        rows = []
        for b in range(nbq):
            b8 = sel_ref[b, pl.ds(pl.multiple_of(kt * SEL_BPT, SEL_BPT), SEL_BPT), :]
            b16 = jnp.concatenate([b8, jnp.zeros((BF16_ROWS - SEL_BPT, qb), F32)], axis=0).astype(BF16)
            rows += [b16] * NSA_HPG
        qa_ref[NSA_DIM:NSA_DIM + BF16_ROWS, :] = jnp.concatenate(rows, axis=1)

    def scores(slot, kt):
        k0 = pl.multiple_of(kt * tk, tk)
        bias_rows(kt)
        s = _dot(ks_ref[0, pl.ds(k0, tk), :], qa_ref[...])
        s_refs[slot][...] = s
        return jnp.max(s, axis=0, keepdims=True)

    def update(slot, kt, mt, m, causal):
        k0 = pl.multiple_of(kt * tk, tk)
        if causal:
            tri = jnp.where(lax.broadcasted_iota(jnp.int32, (qb, qb), 0)
                            <= lax.broadcasted_iota(jnp.int32, (qb, qb), 1), 0.0, MASK_VALUE)
            for b in range(nbq):
                own = (slice(b * qb, (b + 1) * qb), slice(b * hq, (b + 1) * hq))
                s_refs[slot][own] = s_refs[slot][own] + jnp.concatenate([tri] * NSA_HPG, axis=1)
            mt = jnp.max(s_refs[slot][...], axis=0, keepdims=True)
        s = s_refs[slot][...]
        m_new = jnp.maximum(m, mt)
        p = jnp.exp2(s - m_new).astype(BF16)
        acc_ref[...] = jnp.exp2(m - m_new) * acc_ref[...] + _dot(vst_ref[0, :, pl.ds(k0, tk)], p)
        return m_new

    def accumulate(pend):
        p, k0, alpha, ln = pend
        acc_ref[:, ln] = alpha * acc_ref[:, ln] + _dot(vst_ref[0, :, pl.ds(k0, tk)], p)

    def run(first, n_tiles, carry):
        mt, m = carry
        pend = None
        for k in range(n_tiles):
            cur, nxt = s_refs[k % 2], s_refs[(k + 1) % 2]
            k0c = pl.multiple_of((first + k) * tk, tk)
            k0n = pl.multiple_of((first + k + 1) * tk, tk)
            bias_rows(first + k + 1)
            m_new = jnp.maximum(m, mt)
            alpha = jnp.exp2(m - m_new)
            mts = []
            for n in range(lanes // SEL_LANES):
                ln = slice(n * SEL_LANES, (n + 1) * SEL_LANES)
                s_n = _dot(ks_ref[0, pl.ds(k0n, tk), :], qa_ref[:, ln])
                nxt[:, ln] = s_n
                mts.append(jnp.max(s_n, axis=0, keepdims=True))
                p = jnp.exp2(cur[:, ln] - m_new[:, ln]).astype(BF16)
                if pend is not None:
                    accumulate(pend)
                pend = (p, k0c, alpha[:, ln], ln)
            mt = jnp.concatenate(mts, axis=1)
            m = m_new
        accumulate(pend)
        return mt, m

    acc_ref[...] = jnp.zeros(acc_ref.shape, F32)
    last = step
    carry = (scores(0, 0), jnp.full((1, lanes), MASK_VALUE, F32))
    done = 0
    for unroll in SEL_UNROLLS:
        trips = (last - done) // unroll
        carry = lax.fori_loop(0, trips, lambda it, c, d=done, u=unroll: run(d + it * u, u, c), carry)
        done = done + trips * unroll
    mt_a, m = carry

    @pl.when(last % 2 == 0)
    def _():
        update(0, last, mt_a, m, True)

    @pl.when(last % 2 == 1)
    def _():
        mt_b, m_b = run(last - 1, 1, (mt_a, m))
        update(1, last, mt_b, m_b, True)

    for b in range(nbq):
        own = slice(b * hq, (b + 1) * hq)
        o_sel = acc_ref[:NSA_DIM, own] * (1.0 / acc_ref[NSA_DIM:NSA_DIM + 1, own])
        out_t = part_ref[b] + _gate(gates_ref.at[b], g, 1) * o_sel
        outs = [out_t[:, h * qb:(h + 1) * qb] for h in range(NSA_HPG)]
        o_ref[b * qb:(b + 1) * qb, :] = jnp.concatenate(outs, axis=0).T.astype(o_ref.dtype)


def _nsa(q, gates_t, kc, vc_aug_t, ks_aug, vs_aug_t, kw, vw_aug_t):
    s = q.shape[0]
    assert s % (2 * SEL_TILE) == 0 and s >= WINDOW + QUERY_BLOCK
    gdim = NSA_HPG * NSA_DIM
    ncp = kc.shape[1]
    assert ncp * CMP_STRIDE == s and s // SEL_BLOCK * COV_RATIO == ncp
    nb = s // SEL_BLOCK
    ka = ks_aug.shape[-1]
    va = vs_aug_t.shape[1]
    hq = NSA_HPG * QUERY_BLOCK
    per_group = lambda shape: pl.BlockSpec((1,) + shape, lambda g, i: (g, 0, 0))
    rows = NSA_STEP_BLOCKS * QUERY_BLOCK
    assert s % rows == 0
    return pl.pallas_call(
        _nsa_kernel,
        grid=(NSA_GROUPS, s // rows),
        in_specs=[
            pl.BlockSpec((rows, gdim), lambda g, i: (i, g)),
            pl.BlockSpec((GATE_ROWS, rows), lambda g, i: (0, i)),
            per_group((ncp, NSA_DIM)),
            per_group((va, ncp)),
            per_group((s, ka)),
            per_group((va, s)),
            per_group((s, NSA_DIM)),
            per_group((va, s)),
        ],
        out_specs=pl.BlockSpec((rows, gdim), lambda g, i: (i, g)),
        out_shape=jax.ShapeDtypeStruct((s, NSA_HEADS * NSA_DIM), BF16),
        scratch_shapes=[pltpu.VMEM((ka, NSA_STEP_BLOCKS * hq), BF16),
                        pltpu.VMEM((NSA_STEP_BLOCKS, nb, QUERY_BLOCK), F32),
                        pltpu.VMEM((NSA_STEP_BLOCKS, NSA_DIM, hq), F32),
                        pltpu.VMEM((NSA_STEP_BLOCKS, GATE_ROWS, QUERY_BLOCK), F32),
                        pltpu.VMEM((SEL_TILE, NSA_STEP_BLOCKS * hq), F32),
                        pltpu.VMEM((SEL_TILE, NSA_STEP_BLOCKS * hq), F32),
                        pltpu.VMEM((va, NSA_STEP_BLOCKS * hq), F32),
                        pltpu.VMEM((nb, QUERY_BLOCK), F32),
                        pltpu.VMEM((ncp, QUERY_BLOCK), jnp.int32),
                        pltpu.VMEM((WINDOW + QUERY_BLOCK, QUERY_BLOCK), jnp.int32),
                        pltpu.VMEM((SUBLANES + ncp, QUERY_BLOCK), F32)],
        compiler_params=_params("parallel", "arbitrary"),
        name="nsa",
    )(q, gates_t, kc, vc_aug_t, ks_aug, vs_aug_t, kw, vw_aug_t)


def _sgu_kernel(u_ref, v_ref, lng_ref, lnb_ref, ws_ref, bs_ref, o_ref):
    c = SGU_CHUNK
    tm = u_ref.shape[0]
    v = jax.nn.gelu(v_ref[...])
    mu = jnp.mean(v, axis=-1, keepdims=True)
    var = jnp.mean(jnp.square(v - mu), axis=-1, keepdims=True)
    vn = ((v - mu) * lax.rsqrt(var + NORM_EPS) * lng_ref[...] + lnb_ref[...]).astype(BF16)
    u = jax.nn.gelu(u_ref[...])
    tri = lax.broadcasted_iota(jnp.int32, (c, c), 0) >= lax.broadcasted_iota(jnp.int32, (c, c), 1)
    for g in range(SGU_GROUPS):
        w = jnp.where(tri, ws_ref[g], 0.0).astype(BF16)
        cols = slice(g * c, (g + 1) * c)
        rhs = jnp.concatenate([vn[k * c:(k + 1) * c, cols] for k in range(tm // c)], axis=1)
        mixed = _dot(w, rhs)
        for k in range(tm // c):
            rows = slice(k * c, (k + 1) * c)
            o_ref[rows, cols] = (u[rows, cols] * (mixed[:, rows] + bs_ref[g])).astype(o_ref.dtype)


def _sgu(proj, lng, lnb, ws, bs, tm):
    s = proj.shape[0]
    w = lng.shape[0]
    c = SGU_CHUNK
    bs_b = jnp.broadcast_to(bs[:, :, None], (SGU_GROUPS, c, c))
    return pl.pallas_call(
        _sgu_kernel,
        grid=(s // tm,),
        in_specs=[
            pl.BlockSpec((tm, w), lambda i: (i, 0)),
            pl.BlockSpec((tm, w), lambda i: (i, 1)),
            pl.BlockSpec((1, w), lambda i: (0, 0)),
            pl.BlockSpec((1, w), lambda i: (0, 0)),
            pl.BlockSpec((SGU_GROUPS, c, c), lambda i: (0, 0, 0)),
            pl.BlockSpec((SGU_GROUPS, c, c), lambda i: (0, 0, 0)),
        ],
        out_specs=pl.BlockSpec((tm, w), lambda i: (i, 0)),
        out_shape=jax.ShapeDtypeStruct((s, w), BF16),
        compiler_params=_params("parallel"),
        name="sgu",
    )(proj, proj, lng.reshape(1, w), lnb.reshape(1, w), ws, bs_b)


def _mix_kernel(oa_ref, ob_ref, ga_ref, gb_ref, x_ref, pa_ref, pb_ref, wo_ref,
                gc_ref, wq_ref, mkt_ref, mv_ref, wmo_ref, o_ref):
    a = _dot(oa_ref[...], pa_ref[...])
    b = _dot(ob_ref[...], pb_ref[...])
    merged = jax.nn.sigmoid(ga_ref[...]) * a + jax.nn.sigmoid(gb_ref[...]) * b
    x = x_ref[...] + _dot(merged.astype(BF16), wo_ref[...])
    h = _rms(x, gc_ref[...]).astype(BF16)
    mq = (_dot(h, wq_ref[...]) * (MEM_DIM ** -0.5)).astype(BF16)
    outs = []
    for hh in range(MEM_HEADS):
        s = _dot(mq[:, hh * MEM_DIM:(hh + 1) * MEM_DIM], mkt_ref[hh])
        e = jnp.exp(s - jnp.max(s, axis=-1, keepdims=True))
        p = e / jnp.sum(e, axis=-1, keepdims=True)
        outs.append(_dot(p.astype(BF16), mv_ref[hh]).astype(BF16))
    o_ref[...] = x + _dot(jnp.concatenate(outs, axis=1), wmo_ref[...])


def _mix(o_a, o_b, proj, x, p_a, p_b, w_o, gate_block, g_cross, w_q, mk_t, mv, w_mo, tm):
    s, d = x.shape
    wa = o_a.shape[1]
    wb = o_b.shape[1]
    mw = w_q.shape[1]
    m = mv.shape[1]
    resident = lambda shape: pl.BlockSpec(shape, lambda i: (0,) * len(shape), pipeline_mode=pl.Buffered(1))
    return pl.pallas_call(
        _mix_kernel,
        grid=(s // tm,),
        in_specs=[
            pl.BlockSpec((tm, wa), lambda i: (i, 0)),
            pl.BlockSpec((tm, wb), lambda i: (i, 0)),
            pl.BlockSpec((tm, d), lambda i: (i, gate_block)),
            pl.BlockSpec((tm, d), lambda i: (i, gate_block + 1)),
            pl.BlockSpec((tm, d), lambda i: (i, 0)),
            resident((wa, d)),
            resident((wb, d)),
            resident((d, d)),
            resident((1, d)),
            resident((d, mw)),
            resident((MEM_HEADS, MEM_DIM, m)),
            resident((MEM_HEADS, m, MEM_DIM)),
            resident((mw, d)),
        ],
        out_specs=pl.BlockSpec((tm, d), lambda i: (i, 0)),
        out_shape=jax.ShapeDtypeStruct((s, d), F32),
        compiler_params=_params("parallel"),
        name="mix",
    )(o_a, o_b, proj, proj, x, p_a, p_b, w_o, g_cross.reshape(1, d), w_q, mk_t, mv, w_mo)


def _ffn_kernel(x_ref, g_ref, wg_ref, wu_ref, wo_ref, gf_ref, o_ref, h_ref, *, final_norm):
    j = pl.program_id(1)
    row_chunks = [slice(r, r + FFN_NORM_ROWS) for r in range(0, x_ref.shape[0], FFN_NORM_ROWS)]

    @pl.when(j == 0)
    def _():
        for rs in row_chunks:
            x = x_ref[rs, :]
            h_ref[rs, :] = _rms(x, g_ref[...]).astype(BF16)
            o_ref[rs, :] = x

    h = h_ref[...]
    th = wg_ref.shape[1]
    d = o_ref.shape[1]
    for c in range(th // FFN_CHUNK):
        cs = slice(c * FFN_CHUNK, (c + 1) * FFN_CHUNK)
        act = (jax.nn.silu(_dot(h, wg_ref[:, cs])) * _dot(h, wu_ref[:, cs])).astype(BF16)
        for n in range(d // FFN_OUT_TN):
            ns = slice(n * FFN_OUT_TN, (n + 1) * FFN_OUT_TN)
            o_ref[:, ns] += _dot(act, wo_ref[cs, ns])

    if final_norm:
        @pl.when(j == pl.num_programs(1) - 1)
        def _():
            for rs in row_chunks:
                o_ref[rs, :] = _rms(o_ref[rs, :], gf_ref[...])


def _ffn(x, g, w_in, w_out, g_final, final_norm, tm, th):
    s, d = x.shape
    hidden = w_out.shape[0]
    nh = hidden // th
    return pl.pallas_call(
        functools.partial(_ffn_kernel, final_norm=final_norm),
        grid=(s // tm, nh),
        in_specs=[
            pl.BlockSpec((tm, d), lambda i, j: (i, 0)),
            pl.BlockSpec((1, d), lambda i, j: (0, 0)),
            pl.BlockSpec((d, th), lambda i, j: (0, j)),
            pl.BlockSpec((d, th), lambda i, j: (0, j + nh)),
            pl.BlockSpec((th, d), lambda i, j: (j, 0)),
            pl.BlockSpec((1, d), lambda i, j: (0, 0)),
        ],
        out_specs=pl.BlockSpec((tm, d), lambda i, j: (i, 0)),
        out_shape=jax.ShapeDtypeStruct((s, d), F32),
        scratch_shapes=[pltpu.VMEM((tm, d), BF16)],
        compiler_params=_params("parallel", "arbitrary"),
        name="ffn",
    )(x, g.reshape(1, d), w_in, w_in, w_out, g_final.reshape(1, d))


def _layer(x, mem, norm_mix_g, w_in, cmp_pe_k, cmp_k_w1, cmp_k_b1, cmp_k_w2, cmp_pe_v, cmp_v_w1, cmp_v_b1, cmp_v_w2,
           sgu_ln_g, sgu_ln_b, sgu_ws, sgu_b, w_proj_a, w_proj_b, w_mix_out, norm_cross_g, norm_mem_g,
           w_mq, w_mkv, w_mo, norm_ffn_g):
    s, d = x.shape
    qw = NSA_HEADS * NSA_DIM
    kvw = NSA_GROUPS * NSA_DIM
    sguw = sgu_ln_g.shape[0]
    ngate = GATE_ROWS
    o_kv = qw
    o_gate = o_kv + N_KV_STREAMS * kvw
    o_u = o_gate + ngate

    w_a = w_in[:, :o_gate + PROJ_GATE_TN].astype(BF16)
    w_b = w_in[:, o_u:].astype(BF16)
    nat, kw, ks_aug, vs_aug_t, vw_aug_t, q, proj_b, gates_t = _proj(x, norm_mix_g, w_a, w_b, min(PROJ_TM, s))

    w1 =jnp.stack([cmp_k_w1, cmp_v_w1]).astype(BF16)
    pe = jnp.stack([cmp_pe_k.reshape(1, -1), cmp_pe_v.reshape(1, -1)])
    b1 = jnp.stack([cmp_k_b1.reshape(1, -1), cmp_v_b1.reshape(1, -1)])
    w2 = jnp.stack([cmp_k_w2, cmp_v_w2]).astype(BF16)
    cmp, cmp_t = _compress(nat, w1, pe, b1, w2)
    o_a = _nsa(q, gates_t, cmp[0], cmp_t[1], ks_aug, vs_aug_t, kw, vw_aug_t)

    o_b = _sgu(proj_b, sgu_ln_g, sgu_ln_b, sgu_ws, sgu_b, min(SGU_TM, s))

    m = mem.shape[0]
    mw = MEM_HEADS * MEM_DIM
    mkv = _norm_matmul(mem, norm_mem_g, w_mkv.astype(BF16), F32, m, mw)
    mk_t = mkv[:, :mw].reshape(m, MEM_HEADS, MEM_DIM).transpose(1, 2, 0).astype(BF16)
    mv = mkv[:, mw:].reshape(m, MEM_HEADS, MEM_DIM).transpose(1, 0, 2).astype(BF16)
    return _mix(o_a, o_b, proj_b, x, w_proj_a.astype(BF16), w_proj_b.astype(BF16), w_mix_out.astype(BF16),
                (2 * sguw) // d, norm_cross_g, w_mq.astype(BF16), mk_t, mv, w_mo.astype(BF16), min(MIX_TM, s))


def kernel(x, mem, norm_mix_g, w_in, cmp_pe_k, cmp_k_w1, cmp_k_b1, cmp_k_w2, cmp_pe_v, cmp_v_w1, cmp_v_b1, cmp_v_w2, sgu_ln_g, sgu_ln_b, sgu_ws, sgu_b, w_proj_a, w_proj_b, w_mix_out, norm_cross_g, norm_mem_g, w_mq, w_mkv, w_mo, norm_ffn_g, w_ffn_in, w_ffn_out, norm_final_g):
    b, s, d = x.shape
    depth = w_in.shape[0]
    outs = []
    for bi in range(b):
        xb = x[bi]
        for l in range(depth):
            last = l == depth - 1
            xb = _layer(xb, mem[bi], norm_mix_g[l], w_in[l], cmp_pe_k[l], cmp_k_w1[l], cmp_k_b1[l], cmp_k_w2[l],
                        cmp_pe_v[l], cmp_v_w1[l], cmp_v_b1[l], cmp_v_w2[l], sgu_ln_g[l], sgu_ln_b[l], sgu_ws[l],
                        sgu_b[l], w_proj_a[l], w_proj_b[l], w_mix_out[l], norm_cross_g[l], norm_mem_g[l],
                        w_mq[l], w_mkv[l], w_mo[l], norm_ffn_g[l])
            xb = _ffn(xb, norm_ffn_g[l], w_ffn_in[l].astype(BF16), w_ffn_out[l].astype(BF16), norm_final_g, last,
                      min(FFN_TM, s), FFN_TH)
        outs.append(xb)
    return jnp.stack(outs)
```

```python
import functools
import math

import jax
import jax.numpy as jnp
from jax import lax
from jax.experimental import pallas as pl
from jax.experimental.pallas import tpu as pltpu

F32 = jnp.float32
BF16 = jnp.bfloat16

NORM_EPS = 1e-6
MASK_VALUE = -1e30
N_FORCED = 3
N_BRANCH = 3
LOG2E = math.log2(math.e)

NSA_HEADS = 16
NSA_GROUPS = 4
NSA_HPG = NSA_HEADS // NSA_GROUPS
NSA_DIM = 64
GATE_ROWS = NSA_HEADS * N_BRANCH
N_KV_STREAMS = 6
CMP_BLOCK = 32
CMP_STRIDE = 16
SEL_BLOCK = 64
SEL_TOPK = 16
WINDOW = 512
QUERY_BLOCK = 128
SGU_GROUPS = 8
SGU_CHUNK = 128
MEM_HEADS = 4
MEM_DIM = 128

LANES = 128
SUBLANES = 8
COV_RATIO = SEL_BLOCK // CMP_STRIDE
COV_LEAD = CMP_BLOCK // CMP_STRIDE - 1
COV_BAND = tuple(
    max(min((k - COV_LEAD) * CMP_STRIDE + CMP_BLOCK, SEL_BLOCK) - max((k - COV_LEAD) * CMP_STRIDE, 0), 0) / CMP_BLOCK
    for k in range(COV_RATIO + COV_LEAD))
assert SEL_BLOCK % CMP_STRIDE == 0 and CMP_BLOCK % CMP_STRIDE == 0 and COV_LEAD <= SUBLANES
SEL_TILE = 512
SEL_BPT = SEL_TILE // SEL_BLOCK
SEL_LANES = 256
SEL_UNROLLS = (4, 2)
NSA_STEP_BLOCKS = SEL_TILE // QUERY_BLOCK
NSA_BUCKETS = 8
BF16_ROWS = 16
FFN_CHUNK = 256
FFN_OUT_TN = 1024
FFN_NORM_ROWS = 256
PROJ_TM, SGU_TM, MIX_TM, FFN_TM, FFN_TH = 1024, 512, 256, 1024, 512
VMEM_LIMIT = 62 * 1024 * 1024


def _params(*sem):
    return pltpu.CompilerParams(dimension_semantics=sem, vmem_limit_bytes=VMEM_LIMIT)


def _rms(x, g):
    return x * lax.rsqrt(jnp.mean(x * x, axis=-1, keepdims=True) + NORM_EPS) * g


def _dot(a, b):
    return jnp.dot(a, b, preferred_element_type=F32)


def _ones_row(n):
    return jnp.where(lax.broadcasted_iota(jnp.int32, (BF16_ROWS, n), 0) == 0, 1.0, 0.0).astype(BF16)


def _norm_matmul_kernel(x_ref, g_ref, w_ref, o_ref, h_ref):
    @pl.when(pl.program_id(1) == 0)
    def _():
        h_ref[...] = _rms(x_ref[...], g_ref[...]).astype(BF16)

    o_ref[...] = _dot(h_ref[...], w_ref[...]).astype(o_ref.dtype)


def _norm_matmul(x, g, w, out_dtype, tm, tn):
    s, d = x.shape
    n = w.shape[1]
    return pl.pallas_call(
        _norm_matmul_kernel,
        grid=(s // tm, n // tn),
        in_specs=[
            pl.BlockSpec((tm, d), lambda i, j: (i, 0)),
            pl.BlockSpec((1, d), lambda i, j: (0, 0)),
            pl.BlockSpec((d, tn), lambda i, j: (0, j)),
        ],
        out_specs=pl.BlockSpec((tm, tn), lambda i, j: (i, j)),
        out_shape=jax.ShapeDtypeStruct((s, n), out_dtype),
        scratch_shapes=[pltpu.VMEM((tm, d), BF16)],
        compiler_params=_params("parallel", "arbitrary"),
        name="norm_matmul",
    )(x, g.reshape(1, d), w)


PROJ_TN = 512
PROJ_GATE_TN = 256


def _proj_kernel(x_ref, g_ref, wa_ref, wb_ref, nat_ref, kw_ref, ksa_ref, vsa_ref, vwa_ref, q_ref, o_ref, gt_ref,
                 h_ref, stage_ref):
    i = pl.program_id(0)
    j = pl.program_id(1)
    tm = x_ref.shape[0]
    dh = NSA_DIM
    ng = NSA_GROUPS
    tn = PROJ_TN
    qw = q_ref.shape[1]
    nb = pl.num_programs(1) - 1

    @pl.when(j == 0)
    def _():
        h_ref[...] = _rms(x_ref[...], g_ref[...]).astype(BF16)

    @pl.when(j < nb)
    def _():
        o_ref[...] = _dot(h_ref[...], wb_ref[...])

    @pl.when(j == nb)
    def _():
        h = h_ref[...]

        def tile(t):
            return _dot(h, wa_ref[:, qw + t * tn:qw + (t + 1) * tn])

        def cols(res, c):
            return res[:, c * dh:(c + 1) * dh].astype(BF16)

        def cols_t(res, c):
            slab = res[:, (c // 2) * 2 * dh:(c // 2 + 1) * 2 * dh].T
            return slab[(c % 2) * dh:(c % 2 + 1) * dh].astype(BF16)

        res = tile(0)
        for c in range(2 * ng):
            stage_ref[...] = res[:, c * dh:(c + 1) * dh]
            for t in range(0, CMP_STRIDE, 2):
                pair = [stage_ref[pl.ds(t + u, tm // CMP_STRIDE, stride=CMP_STRIDE), :] for u in range(2)]
                nat_ref[c // ng, c % ng, :, t * dh:(t + 2) * dh] = jnp.concatenate(pair, axis=1).astype(BF16)

        res = tile(1)
        pos = i * tm + lax.broadcasted_iota(jnp.int32, (tm, dh), 0)
        lane = lax.broadcasted_iota(jnp.int32, (tm, dh), 1)
        onehot = jnp.where((pos // SEL_BLOCK) % SEL_BPT == lane, 1.0, 0.0).astype(BF16)
        for g in range(ng):
            ksa_ref[g, :, :dh] = cols(res, g)
            ksa_ref[g, :, dh:] = onehot
            vsa_ref[g, :dh, :] = cols_t(res, ng + g)
            vsa_ref[g, dh:, :] = _ones_row(tm)

        res = tile(2)
        for g in range(ng):
            kw_ref[g] = cols(res, g)
            vwa_ref[g, :dh, :] = cols_t(res, ng + g)
            vwa_ref[g, dh:, :] = _ones_row(tm)

        o_gate = qw + N_KV_STREAMS // 2 * tn
        res = _dot(h, wa_ref[:, o_gate:o_gate + PROJ_GATE_TN])
        gt_ref[...] = res[:, :LANES].T

        for t in range(qw // tn):
            q_ref[:, t * tn:(t + 1) * tn] = _dot(h, wa_ref[:, t * tn:(t + 1) * tn]).astype(BF16)


def _proj(x, g, w_a, w_b, tm):
    s, d = x.shape
    dh, ng, tn = NSA_DIM, NSA_GROUPS, PROJ_TN
    qw = NSA_HEADS * dh
    assert 2 * ng * dh == tn and qw % tn == 0 and w_a.shape[1] == qw + N_KV_STREAMS // 2 * tn + PROJ_GATE_TN
    n = w_b.shape[1]
    assert n % tn == 0 and GATE_ROWS <= LANES <= PROJ_GATE_TN
    nb = n // tn
    shapes = [
        jax.ShapeDtypeStruct((2, ng, s // CMP_STRIDE, CMP_STRIDE * dh), BF16),
        jax.ShapeDtypeStruct((ng, s, dh), BF16),
        jax.ShapeDtypeStruct((ng, s, 2 * dh), BF16),
        jax.ShapeDtypeStruct((ng, dh + BF16_ROWS, s), BF16),
        jax.ShapeDtypeStruct((ng, dh + BF16_ROWS, s), BF16),
        jax.ShapeDtypeStruct((s, NSA_HEADS * dh), BF16),
        jax.ShapeDtypeStruct((s, n), F32),
        jax.ShapeDtypeStruct((LANES, s), F32),
    ]
    return pl.pallas_call(
        _proj_kernel,
        grid=(s // tm, 1 + nb),
        in_specs=[
            pl.BlockSpec((tm, d), lambda i, j: (i, 0)),
            pl.BlockSpec((1, d), lambda i, j: (0, 0)),
            pl.BlockSpec(w_a.shape, lambda i, j: (0, 0), pipeline_mode=pl.Buffered(1)),
            pl.BlockSpec((d, tn), lambda i, j: (0, jnp.minimum(j, nb - 1))),
        ],
        out_specs=[
            pl.BlockSpec((2, ng, tm // CMP_STRIDE, CMP_STRIDE * dh), lambda i, j: (0, 0, i, 0)),
            pl.BlockSpec((ng, tm, dh), lambda i, j: (0, i, 0)),
            pl.BlockSpec((ng, tm, 2 * dh), lambda i, j: (0, i, 0)),
            pl.BlockSpec((ng, dh + BF16_ROWS, tm), lambda i, j: (0, 0, i)),
            pl.BlockSpec((ng, dh + BF16_ROWS, tm), lambda i, j: (0, 0, i)),
            pl.BlockSpec((tm, qw), lambda i, j: (i, 0)),
            pl.BlockSpec((tm, tn), lambda i, j: (i, jnp.minimum(j, nb - 1))),
            pl.BlockSpec((LANES, tm), lambda i, j: (0, i)),
        ],
        out_shape=shapes,
        scratch_shapes=[pltpu.VMEM((tm, d), BF16), pltpu.VMEM((tm, dh), F32)],
        compiler_params=_params("parallel", "arbitrary"),
        name="proj",
    )(x, g.reshape(1, d), w_a, w_b)


def _compress_kernel(r_ref, w1_ref, pe_ref, b1_ref, w2_ref, o_ref, ot_ref):
    half = r_ref.shape[-1]
    r = r_ref[0, 0]
    bias = _dot(pe_ref[0].astype(BF16), w1_ref[0]) + b1_ref[0]
    top = _dot(r, w1_ref[0, :half, :])
    bot = _dot(r, w1_ref[0, half:, :])
    ncp = r.shape[0]
    hid = top + pltpu.roll(bot, ncp - 1, 0) + bias
    out = _dot(jax.nn.gelu(hid).astype(BF16), w2_ref[0])
    o_ref[0, 0] = out.astype(o_ref.dtype)
    dh = out.shape[1]
    out_t = jnp.concatenate([out, jnp.zeros((ncp, LANES - dh), F32)], axis=1).T
    ot_ref[0, 0, :dh, :] = out_t[:dh].astype(ot_ref.dtype)
    ot_ref[0, 0, dh:, :] = _ones_row(ncp)


def _compress(r, w1, pe, b1, w2):
    _, g, ncp, half = r.shape
    hidden = w1.shape[-1]
    dh = w2.shape[-1]
    return pl.pallas_call(
        _compress_kernel,
        grid=(2, g),
        in_specs=[
            pl.BlockSpec((1, 1, ncp, half), lambda s, gg: (s, gg, 0, 0)),
            pl.BlockSpec((1, 2 * half, hidden), lambda s, gg: (s, 0, 0)),
            pl.BlockSpec((1, 1, 2 * half), lambda s, gg: (s, 0, 0)),
            pl.BlockSpec((1, 1, hidden), lambda s, gg: (s, 0, 0)),
            pl.BlockSpec((1, hidden, dh), lambda s, gg: (s, 0, 0)),
        ],
        out_specs=[pl.BlockSpec((1, 1, ncp, dh), lambda s, gg: (s, gg, 0, 0)),
                   pl.BlockSpec((1, 1, dh + BF16_ROWS, ncp), lambda s, gg: (s, gg, 0, 0))],
        out_shape=[jax.ShapeDtypeStruct((2, g, ncp, dh), BF16),
                   jax.ShapeDtypeStruct((2, g, dh + BF16_ROWS, ncp), BF16)],
        compiler_params=_params("parallel", "parallel"),
        name="compress",
    )(r, w1, pe, b1, w2)


PICKED = -2.0


def _nsa_kernel(q_ref, gt_ref, kc_ref, vct_ref, ks_ref, vst_ref, kw_ref, vwt_ref, c0_ref, c1_ref, o_ref, c0o_ref, c1o_ref,
                qa_ref, sel_ref, part_ref, gates_ref, s0_ref, s1_ref, acc_ref, blkf_ref, cend_ref, wrel_ref, pp_ref):
    c0o_ref[...] = c0_ref[...].astype(BF16)
    c1o_ref[...] = c1_ref[...].astype(BF16)
    g = pl.program_id(0)
    step = pl.program_id(1)
    qb = QUERY_BLOCK
    hq = NSA_HPG * qb
    nb = sel_ref.shape[1]
    ncp = kc_ref.shape[1]
    s_len = ks_ref.shape[1]
    first = step * NSA_STEP_BLOCKS

    @pl.when(step == 0)
    def _():
        lane = lax.broadcasted_iota(jnp.int32, cend_ref.shape, 1)
        cend_ref[...] = lax.broadcasted_iota(jnp.int32, cend_ref.shape, 0) * CMP_STRIDE + (CMP_BLOCK - 1) - lane
        wrel_ref[...] = (lax.broadcasted_iota(jnp.int32, wrel_ref.shape, 0)
                         - lax.broadcasted_iota(jnp.int32, wrel_ref.shape, 1))

    def rows(sub):
        return pl.ds(pl.multiple_of(sub * qb, qb), qb)

    bucket_ok = s_len % (NSA_BUCKETS * SEL_TILE) == 0 and nb // NSA_BUCKETS >= SEL_TOPK
    n_bucket = NSA_BUCKETS if bucket_ok else 1
    bucket = ((first + NSA_STEP_BLOCKS) * qb - 1) // (s_len // n_bucket)
    for b in range(n_bucket):
        nck, nbk = (b + 1) * (ncp // n_bucket), (b + 1) * (nb // n_bucket)

        @pl.when(bucket == b)
        def _():
            def scores(sub, carry):
                own = pl.ds(pl.multiple_of(sub * hq, hq), hq)
                _nsa_scores(g, first + sub, q_ref.at[rows(sub), :], gt_ref.at[:, rows(sub)], kc_ref, vct_ref, kw_ref,
                            vwt_ref, qa_ref.at[:, own], sel_ref.at[sub], part_ref.at[sub], gates_ref.at[sub], pp_ref,
                            cend_ref, wrel_ref, nck, nbk)
                return carry

            lax.fori_loop(0, NSA_STEP_BLOCKS, scores, 0)
            _nsa_topk(first, sel_ref, blkf_ref, nbk)

    _nsa_select(g, step, ks_ref, vst_ref, o_ref, qa_ref, sel_ref, part_ref, gates_ref, s0_ref, s1_ref, acc_ref)


def _gate(gates_ref, g, branch):
    rows = [gates_ref[pl.ds((g * NSA_HPG + h) * N_BRANCH + branch, 1), :] for h in range(NSA_HPG)]
    return jax.nn.sigmoid(jnp.concatenate(rows, axis=1))


def _nsa_scores(g, i, q_ref, gt_ref, kc_ref, vct_ref, kw_ref, vwt_ref, qa_ref, score_ref, part_ref, gates_ref,
                pp_ref, cend_ref, wrel_ref, nck, nbk):
    qb = QUERY_BLOCK
    start = i * qb
    hq = NSA_HPG * qb
    t_row = start + lax.broadcasted_iota(jnp.int32, (1, qb), 1)
    jt = t_row // SEL_BLOCK

    qt = (q_ref[...].astype(F32) * (NSA_DIM ** -0.5 * LOG2E)).T
    q_t = jnp.concatenate([qt[h * NSA_DIM:(h + 1) * NSA_DIM] for h in range(NSA_HPG)], axis=1).astype(BF16)
    qa_ref[:NSA_DIM, :] = q_t
    qa_ref[NSA_DIM:, :] = jnp.zeros((qa_ref.shape[0] - NSA_DIM, hq), BF16)
    gates_ref[...] = gt_ref[...]

    def exps(s, bias):
        out = []
        for h in range(NSA_HPG):
            sh = s[:, h * qb:(h + 1) * qb] + bias
            out.append(jnp.exp2(sh - jnp.max(sh, axis=0, keepdims=True)))
        return out

    wk = WINDOW + qb
    k0w = pl.multiple_of(jnp.maximum(start - WINDOW, 0), qb)
    sc = _dot(kc_ref[0, :nck, :], q_t)
    sw = _dot(kw_ref[0, pl.ds(k0w, wk), :], q_t)
    e_cmp = exps(sc, jnp.where(cend_ref[:nck, :] <= start, 0.0, MASK_VALUE))
    o_cmp = _dot(vct_ref[0, :, :nck], jnp.concatenate([e.astype(BF16) for e in e_cmp], axis=1))
    wrel = wrel_ref[...]
    e_win = exps(sw, jnp.where((wrel <= start - k0w) & (wrel > start - k0w - WINDOW), 0.0, MASK_VALUE))
    o_win = _dot(vwt_ref[0, :, pl.ds(k0w, wk)], jnp.concatenate([e.astype(BF16) for e in e_win], axis=1))
    inv_cmp = jnp.where(jnp.concatenate([t_row >= CMP_BLOCK - 1] * NSA_HPG, axis=1),
                        1.0 / o_cmp[NSA_DIM:NSA_DIM + 1, :], 0.0)
    psum = e_cmp[0] * inv_cmp[:, :qb]
    for h in range(1, NSA_HPG):
        psum = psum + e_cmp[h] * inv_cmp[:, h * qb:(h + 1) * qb]
    part_ref[...] = (_gate(gates_ref, g, 0) * inv_cmp * o_cmp[:NSA_DIM]
                     + _gate(gates_ref, g, 2) * (1.0 / o_win[NSA_DIM:NSA_DIM + 1, :]) * o_win[:NSA_DIM])
    pp_ref[:SUBLANES, :] = jnp.zeros((SUBLANES, qb), F32)
    pp_ref[SUBLANES:SUBLANES + nck, :] = psum
    imp = None
    for k, w in enumerate(COV_BAND):
        tap = pp_ref[pl.ds(SUBLANES - COV_LEAD + k, nbk, stride=COV_RATIO), :]
        tap = tap if w == 1.0 else w * tap
        imp = tap if imp is None else imp + tap
    blk = lax.broadcasted_iota(jnp.int32, (nbk, qb), 0)
    forced = (blk == 0) | (blk == jt) | (blk == jt - 1)
    score_ref[:nbk, :] = jnp.where(forced, PICKED, jnp.where(blk <= jt, imp, -1.0))


def _nsa_topk(first, sel_ref, blkf_ref, nbk):
    qb = QUERY_BLOCK
    blk = lax.broadcasted_iota(jnp.int32, (nbk, qb), 0)
    blkf_ref[:nbk, :] = blk.astype(F32)

    def pick(_, scores):
        out = []
        for score in scores:
            mx = jnp.max(score, axis=0, keepdims=True)
            cand = jnp.where(score == mx, blkf_ref[:nbk, :], float(nbk))
            out.append(jnp.where(cand == jnp.min(cand, axis=0, keepdims=True), PICKED, score))
        return tuple(out)

    scores = tuple(sel_ref[b, :nbk, :] for b in range(sel_ref.shape[0]))
    scores = lax.fori_loop(0, min(SEL_TOPK, nbk) - N_FORCED, pick, scores)
    for b, score in enumerate(scores):
        jt = ((first + b) * qb + lax.broadcasted_iota(jnp.int32, (1, qb), 1)) // SEL_BLOCK
        sel_ref[b, :nbk, :] = jnp.where((score == PICKED) & (blk <= jt), 0.0, MASK_VALUE)


def _nsa_select(g, step, ks_ref, vst_ref, o_ref, qa_ref, sel_ref, part_ref, gates_ref, s0_ref, s1_ref, acc_ref):
    s_refs = (s0_ref, s1_ref)
    qb = QUERY_BLOCK
    hq = NSA_HPG * qb
    nbq = sel_ref.shape[0]
    lanes = nbq * hq
    tk = SEL_TILE

    def bias_rows(kt):
        rows = []
        for b in range(nbq):
            b8 = sel_ref[b, pl.ds(pl.multiple_of(kt * SEL_BPT, SEL_BPT), SEL_BPT), :]
            b16 = jnp.concatenate([b8, jnp.zeros((BF16_ROWS - SEL_BPT, qb), F32)], axis=0).astype(BF16)
            rows += [b16] * NSA_HPG
        qa_ref[NSA_DIM:NSA_DIM + BF16_ROWS, :] = jnp.concatenate(rows, axis=1)

    def scores(slot, kt):
        k0 = pl.multiple_of(kt * tk, tk)
        bias_rows(kt)
        s = _dot(ks_ref[0, pl.ds(k0, tk), :], qa_ref[...])
        s_refs[slot][...] = s
        return jnp.max(s, axis=0, keepdims=True)

    def update(slot, kt, mt, m, causal):
        k0 = pl.multiple_of(kt * tk, tk)
        if causal:
            tri = jnp.where(lax.broadcasted_iota(jnp.int32, (qb, qb), 0)
                            <= lax.broadcasted_iota(jnp.int32, (qb, qb), 1), 0.0, MASK_VALUE)
            for b in range(nbq):
                own = (slice(b * qb, (b + 1) * qb), slice(b * hq, (b + 1) * hq))
                s_refs[slot][own] = s_refs[slot][own] + jnp.concatenate([tri] * NSA_HPG, axis=1)
            mt = jnp.max(s_refs[slot][...], axis=0, keepdims=True)
        s = s_refs[slot][...]
        m_new = jnp.maximum(m, mt)
        p = jnp.exp2(s - m_new).astype(BF16)
        acc_ref[...] = jnp.exp2(m - m_new) * acc_ref[...] + _dot(vst_ref[0, :, pl.ds(k0, tk)], p)
        return m_new

    def accumulate(pend):
        p, k0, alpha, ln = pend
        acc_ref[:, ln] = alpha * acc_ref[:, ln] + _dot(vst_ref[0, :, pl.ds(k0, tk)], p)

    def run(first, n_tiles, carry):
        mt, m = carry
        pend = None
        for k in range(n_tiles):
            cur, nxt = s_refs[k % 2], s_refs[(k + 1) % 2]
            k0c = pl.multiple_of((first + k) * tk, tk)
            k0n = pl.multiple_of((first + k + 1) * tk, tk)
            bias_rows(first + k + 1)
            m_new = jnp.maximum(m, mt)
            alpha = jnp.exp2(m - m_new)
            mts = []
            for n in range(lanes // SEL_LANES):
                ln = slice(n * SEL_LANES, (n + 1) * SEL_LANES)
                s_n = _dot(ks_ref[0, pl.ds(k0n, tk), :], qa_ref[:, ln])
                nxt[:, ln] = s_n
                mts.append(jnp.max(s_n, axis=0, keepdims=True))
                p = jnp.exp2(cur[:, ln] - m_new[:, ln]).astype(BF16)
                if pend is not None:
                    accumulate(pend)
                pend = (p, k0c, alpha[:, ln], ln)
            mt = jnp.concatenate(mts, axis=1)
            m = m_new
        accumulate(pend)
        return mt, m

    acc_ref[...] = jnp.zeros(acc_ref.shape, F32)
    last = step
    carry = (scores(0, 0), jnp.full((1, lanes), MASK_VALUE, F32))
    done = 0
    for unroll in SEL_UNROLLS:
        trips = (last - done) // unroll
        carry = lax.fori_loop(0, trips, lambda it, c, d=done, u=unroll: run(d + it * u, u, c), carry)
        done = done + trips * unroll
    mt_a, m = carry

    @pl.when(last % 2 == 0)
    def _():
        update(0, last, mt_a, m, True)

    @pl.when(last % 2 == 1)
    def _():
        mt_b, m_b = run(last - 1, 1, (mt_a, m))
        update(1, last, mt_b, m_b, True)

    for b in range(nbq):
        own = slice(b * hq, (b + 1) * hq)
        o_sel = acc_ref[:NSA_DIM, own] * (1.0 / acc_ref[NSA_DIM:NSA_DIM + 1, own])
        out_t = part_ref[b] + _gate(gates_ref.at[b], g, 1) * o_sel
        outs = [out_t[:, h * qb:(h + 1) * qb] for h in range(NSA_HPG)]
        o_ref[b * qb:(b + 1) * qb, :] = jnp.concatenate(outs, axis=0).T.astype(o_ref.dtype)


def _cast_rows(rows, n_steps):
    for k in range(1, rows // BF16_ROWS + 1):
        if rows % (k * BF16_ROWS) == 0 and rows // (k * BF16_ROWS) <= n_steps:
            return k * BF16_ROWS
    raise ValueError(f"no row block for a side cast of {rows} rows in {n_steps} steps")


def _nsa(q, gates_t, kc, vc_aug_t, ks_aug, vs_aug_t, kw, vw_aug_t, cast0, cast1):
    s = q.shape[0]
    assert s % (2 * SEL_TILE) == 0 and s >= WINDOW + QUERY_BLOCK
    gdim = NSA_HPG * NSA_DIM
    ncp = kc.shape[1]
    assert ncp * CMP_STRIDE == s and s // SEL_BLOCK * COV_RATIO == ncp
    nb = s // SEL_BLOCK
    ka = ks_aug.shape[-1]
    va = vs_aug_t.shape[1]
    hq = NSA_HPG * QUERY_BLOCK
    per_group = lambda shape: pl.BlockSpec((1,) + shape, lambda g, i: (g, 0, 0))
    rows = NSA_STEP_BLOCKS * QUERY_BLOCK
    assert s % rows == 0
    n_i = s // rows

    def cast_spec(w):
        rb = _cast_rows(w.shape[0], NSA_GROUPS * n_i)
        return pl.BlockSpec((rb, w.shape[1]), lambda g, i: (jnp.minimum(g * n_i + i, w.shape[0] // rb - 1), 0))

    return pl.pallas_call(
        _nsa_kernel,
        grid=(NSA_GROUPS, n_i),
        in_specs=[
            pl.BlockSpec((rows, gdim), lambda g, i: (i, g)),
            pl.BlockSpec((GATE_ROWS, rows), lambda g, i: (0, i)),
            per_group((ncp, NSA_DIM)),
            per_group((va, ncp)),
            per_group((s, ka)),
            per_group((va, s)),
            per_group((s, NSA_DIM)),
            per_group((va, s)),
            cast_spec(cast0),
            cast_spec(cast1),
        ],
        out_specs=[pl.BlockSpec((rows, gdim), lambda g, i: (i, g)), cast_spec(cast0), cast_spec(cast1)],
        out_shape=[jax.ShapeDtypeStruct((s, NSA_HEADS * NSA_DIM), BF16),
                   jax.ShapeDtypeStruct(cast0.shape, BF16), jax.ShapeDtypeStruct(cast1.shape, BF16)],
        scratch_shapes=[pltpu.VMEM((ka, NSA_STEP_BLOCKS * hq), BF16),
                        pltpu.VMEM((NSA_STEP_BLOCKS, nb, QUERY_BLOCK), F32),
                        pltpu.VMEM((NSA_STEP_BLOCKS, NSA_DIM, hq), F32),
                        pltpu.VMEM((NSA_STEP_BLOCKS, GATE_ROWS, QUERY_BLOCK), F32),
                        pltpu.VMEM((SEL_TILE, NSA_STEP_BLOCKS * hq), F32),
                        pltpu.VMEM((SEL_TILE, NSA_STEP_BLOCKS * hq), F32),
                        pltpu.VMEM((va, NSA_STEP_BLOCKS * hq), F32),
                        pltpu.VMEM((nb, QUERY_BLOCK), F32),
                        pltpu.VMEM((ncp, QUERY_BLOCK), jnp.int32),
                        pltpu.VMEM((WINDOW + QUERY_BLOCK, QUERY_BLOCK), jnp.int32),
                        pltpu.VMEM((SUBLANES + ncp, QUERY_BLOCK), F32)],
        compiler_params=_params("parallel", "arbitrary"),
        name="nsa",
    )(q, gates_t, kc, vc_aug_t, ks_aug, vs_aug_t, kw, vw_aug_t, cast0, cast1)


def _sgu_kernel(u_ref, v_ref, lng_ref, lnb_ref, ws_ref, bs_ref, o_ref):
    c = SGU_CHUNK
    tm = u_ref.shape[0]
    v = jax.nn.gelu(v_ref[...])
    mu = jnp.mean(v, axis=-1, keepdims=True)
    var = jnp.mean(jnp.square(v - mu), axis=-1, keepdims=True)
    vn = ((v - mu) * lax.rsqrt(var + NORM_EPS) * lng_ref[...] + lnb_ref[...]).astype(BF16)
    u = jax.nn.gelu(u_ref[...])
    tri = lax.broadcasted_iota(jnp.int32, (c, c), 0) >= lax.broadcasted_iota(jnp.int32, (c, c), 1)
    for g in range(SGU_GROUPS):
        w = jnp.where(tri, ws_ref[g], 0.0).astype(BF16)
        cols = slice(g * c, (g + 1) * c)
        rhs = jnp.concatenate([vn[k * c:(k + 1) * c, cols] for k in range(tm // c)], axis=1)
        mixed = _dot(w, rhs)
        for k in range(tm // c):
            rows = slice(k * c, (k + 1) * c)
            o_ref[rows, cols] = (u[rows, cols] * (mixed[:, rows] + bs_ref[g])).astype(o_ref.dtype)


def _sgu(proj, lng, lnb, ws, bs, tm):
    s = proj.shape[0]
    w = lng.shape[0]
    c = SGU_CHUNK
    bs_b = jnp.broadcast_to(bs[:, :, None], (SGU_GROUPS, c, c))
    return pl.pallas_call(
        _sgu_kernel,
        grid=(s // tm,),
        in_specs=[
            pl.BlockSpec((tm, w), lambda i: (i, 0)),
            pl.BlockSpec((tm, w), lambda i: (i, 1)),
            pl.BlockSpec((1, w), lambda i: (0, 0)),
            pl.BlockSpec((1, w), lambda i: (0, 0)),
            pl.BlockSpec((SGU_GROUPS, c, c), lambda i: (0, 0, 0)),
            pl.BlockSpec((SGU_GROUPS, c, c), lambda i: (0, 0, 0)),
        ],
        out_specs=pl.BlockSpec((tm, w), lambda i: (i, 0)),
        out_shape=jax.ShapeDtypeStruct((s, w), BF16),
        compiler_params=_params("parallel"),
        name="sgu",
    )(proj, proj, lng.reshape(1, w), lnb.reshape(1, w), ws, bs_b)


def _mix_kernel(oa_ref, ob_ref, ga_ref, gb_ref, x_ref, pa_ref, pb_ref, wo_ref,
                gc_ref, wq_ref, mkt_ref, mv_ref, wmo_ref, o_ref):
    a = _dot(oa_ref[...], pa_ref[...])
    b = _dot(ob_ref[...], pb_ref[...])
    merged = jax.nn.sigmoid(ga_ref[...]) * a + jax.nn.sigmoid(gb_ref[...]) * b
    x = x_ref[...] + _dot(merged.astype(BF16), wo_ref[...])
    h = _rms(x, gc_ref[...]).astype(BF16)
    mq = (_dot(h, wq_ref[...]) * (MEM_DIM ** -0.5)).astype(BF16)
    outs = []
    for hh in range(MEM_HEADS):
        s = _dot(mq[:, hh * MEM_DIM:(hh + 1) * MEM_DIM], mkt_ref[hh])
        e = jnp.exp(s - jnp.max(s, axis=-1, keepdims=True))
        p = e / jnp.sum(e, axis=-1, keepdims=True)
        outs.append(_dot(p.astype(BF16), mv_ref[hh]).astype(BF16))
    o_ref[...] = x + _dot(jnp.concatenate(outs, axis=1), wmo_ref[...])


def _mix(o_a, o_b, proj, x, p_a, p_b, w_o, gate_block, g_cross, w_q, mk_t, mv, w_mo, tm):
    s, d = x.shape
    wa = o_a.shape[1]
    wb = o_b.shape[1]
    mw = w_q.shape[1]
    m = mv.shape[1]
    resident = lambda shape: pl.BlockSpec(shape, lambda i: (0,) * len(shape), pipeline_mode=pl.Buffered(1))
    return pl.pallas_call(
        _mix_kernel,
        grid=(s // tm,),
        in_specs=[
            pl.BlockSpec((tm, wa), lambda i: (i, 0)),
            pl.BlockSpec((tm, wb), lambda i: (i, 0)),
            pl.BlockSpec((tm, d), lambda i: (i, gate_block)),
            pl.BlockSpec((tm, d), lambda i: (i, gate_block + 1)),
            pl.BlockSpec((tm, d), lambda i: (i, 0)),
            resident((wa, d)),
            resident((wb, d)),
            resident((d, d)),
            resident((1, d)),
            resident((d, mw)),
            resident((MEM_HEADS, MEM_DIM, m)),
            resident((MEM_HEADS, m, MEM_DIM)),
            resident((mw, d)),
        ],
        out_specs=pl.BlockSpec((tm, d), lambda i: (i, 0)),
        out_shape=jax.ShapeDtypeStruct((s, d), F32),
        compiler_params=_params("parallel"),
        name="mix",
    )(o_a, o_b, proj, proj, x, p_a, p_b, w_o, g_cross.reshape(1, d), w_q, mk_t, mv, w_mo)


def _ffn_kernel(x_ref, g_ref, wg_ref, wu_ref, wo_ref, gf_ref, o_ref, h_ref, *, final_norm):
    j = pl.program_id(1)
    row_chunks = [slice(r, r + FFN_NORM_ROWS) for r in range(0, x_ref.shape[0], FFN_NORM_ROWS)]

    @pl.when(j == 0)
    def _():
        for rs in row_chunks:
            x = x_ref[rs, :]
            h_ref[rs, :] = _rms(x, g_ref[...]).astype(BF16)
            o_ref[rs, :] = x

    h = h_ref[...]
    th = wg_ref.shape[1]
    d = o_ref.shape[1]
    for c in range(th // FFN_CHUNK):
        cs = slice(c * FFN_CHUNK, (c + 1) * FFN_CHUNK)
        act = (jax.nn.silu(_dot(h, wg_ref[:, cs])) * _dot(h, wu_ref[:, cs])).astype(BF16)
        for n in range(d // FFN_OUT_TN):
            ns = slice(n * FFN_OUT_TN, (n + 1) * FFN_OUT_TN)
            o_ref[:, ns] += _dot(act, wo_ref[cs, ns])

    if final_norm:
        @pl.when(j == pl.num_programs(1) - 1)
        def _():
            for rs in row_chunks:
                o_ref[rs, :] = _rms(o_ref[rs, :], gf_ref[...])


def _ffn(x, g, w_in, w_out, g_final, final_norm, tm, th):
    s, d = x.shape
    hidden = w_out.shape[0]
    nh = hidden // th
    return pl.pallas_call(
        functools.partial(_ffn_kernel, final_norm=final_norm),
        grid=(s // tm, nh),
        in_specs=[
            pl.BlockSpec((tm, d), lambda i, j: (i, 0)),
            pl.BlockSpec((1, d), lambda i, j: (0, 0)),
            pl.BlockSpec((d, th), lambda i, j: (0, j)),
            pl.BlockSpec((d, th), lambda i, j: (0, j + nh)),
            pl.BlockSpec((th, d), lambda i, j: (j, 0)),
            pl.BlockSpec((1, d), lambda i, j: (0, 0)),
        ],
        out_specs=pl.BlockSpec((tm, d), lambda i, j: (i, 0)),
        out_shape=jax.ShapeDtypeStruct((s, d), F32),
        scratch_shapes=[pltpu.VMEM((tm, d), BF16)],
        compiler_params=_params("parallel", "arbitrary"),
        name="ffn",
    )(x, g.reshape(1, d), w_in, w_in, w_out, g_final.reshape(1, d))


def _layer(x, mem, norm_mix_g, w_in, cmp_pe_k, cmp_k_w1, cmp_k_b1, cmp_k_w2, cmp_pe_v, cmp_v_w1, cmp_v_b1, cmp_v_w2,
           sgu_ln_g, sgu_ln_b, sgu_ws, sgu_b, w_proj_a, w_proj_b, w_mix_out, norm_cross_g, norm_mem_g,
           w_mq, w_mkv, w_mo, w_ffn_in, w_ffn_out):
    s, d = x.shape
    qw = NSA_HEADS * NSA_DIM
    kvw = NSA_GROUPS * NSA_DIM
    sguw = sgu_ln_g.shape[0]
    ngate = GATE_ROWS
    o_kv = qw
    o_gate = o_kv + N_KV_STREAMS * kvw
    o_u = o_gate + ngate

    w_a = w_in[:, :o_gate + PROJ_GATE_TN].astype(BF16)
    w_b = w_in[:, o_u:].astype(BF16)
    nat, kw, ks_aug, vs_aug_t, vw_aug_t, q, proj_b, gates_t = _proj(x, norm_mix_g, w_a, w_b, min(PROJ_TM, s))

    w1 =jnp.stack([cmp_k_w1, cmp_v_w1]).astype(BF16)
    pe = jnp.stack([cmp_pe_k.reshape(1, -1), cmp_pe_v.reshape(1, -1)])
    b1 = jnp.stack([cmp_k_b1.reshape(1, -1), cmp_v_b1.reshape(1, -1)])
    w2 = jnp.stack([cmp_k_w2, cmp_v_w2]).astype(BF16)
    cmp, cmp_t = _compress(nat, w1, pe, b1, w2)
    o_a, w_ffn_in, w_ffn_out = _nsa(q, gates_t, cmp[0], cmp_t[1], ks_aug, vs_aug_t, kw, vw_aug_t, w_ffn_in, w_ffn_out)

    o_b = _sgu(proj_b, sgu_ln_g, sgu_ln_b, sgu_ws, sgu_b, min(SGU_TM, s))

    m = mem.shape[0]
    mw = MEM_HEADS * MEM_DIM
    mkv = _norm_matmul(mem, norm_mem_g, w_mkv.astype(BF16), F32, m, mw)
    mk_t = mkv[:, :mw].reshape(m, MEM_HEADS, MEM_DIM).transpose(1, 2, 0).astype(BF16)
    mv = mkv[:, mw:].reshape(m, MEM_HEADS, MEM_DIM).transpose(1, 0, 2).astype(BF16)
    x = _mix(o_a, o_b, proj_b, x, w_proj_a.astype(BF16), w_proj_b.astype(BF16), w_mix_out.astype(BF16),
             (2 * sguw) // d, norm_cross_g, w_mq.astype(BF16), mk_t, mv, w_mo.astype(BF16), min(MIX_TM, s))
    return x, w_ffn_in, w_ffn_out


def kernel(x, mem, norm_mix_g, w_in, cmp_pe_k, cmp_k_w1, cmp_k_b1, cmp_k_w2, cmp_pe_v, cmp_v_w1, cmp_v_b1, cmp_v_w2, sgu_ln_g, sgu_ln_b, sgu_ws, sgu_b, w_proj_a, w_proj_b, w_mix_out, norm_cross_g, norm_mem_g, w_mq, w_mkv, w_mo, norm_ffn_g, w_ffn_in, w_ffn_out, norm_final_g):
    b, s, d = x.shape
    depth = w_in.shape[0]
    outs = []
    for bi in range(b):
        xb = x[bi]
        for l in range(depth):
            last = l == depth - 1
            xb, wi, wo = _layer(xb, mem[bi], norm_mix_g[l], w_in[l], cmp_pe_k[l], cmp_k_w1[l], cmp_k_b1[l],
                                cmp_k_w2[l], cmp_pe_v[l], cmp_v_w1[l], cmp_v_b1[l], cmp_v_w2[l], sgu_ln_g[l],
                                sgu_ln_b[l], sgu_ws[l], sgu_b[l], w_proj_a[l], w_proj_b[l], w_mix_out[l],
                                norm_cross_g[l], norm_mem_g[l], w_mq[l], w_mkv[l], w_mo[l], w_ffn_in[l], w_ffn_out[l])
            xb = _ffn(xb, norm_ffn_g[l], wi, wo, norm_final_g, last, min(FFN_TM, s), FFN_TH)
        outs.append(xb)
    return jnp.stack(outs)
```

```python
import functools
import math

import jax
import jax.numpy as jnp
from jax import lax
from jax.experimental import pallas as pl
from jax.experimental.pallas import tpu as pltpu

F32 = jnp.float32
BF16 = jnp.bfloat16

NORM_EPS = 1e-6
MASK_VALUE = -1e30
N_FORCED = 3
N_BRANCH = 3
LOG2E = math.log2(math.e)

NSA_HEADS = 16
NSA_GROUPS = 4
NSA_HPG = NSA_HEADS // NSA_GROUPS
NSA_DIM = 64
GATE_ROWS = NSA_HEADS * N_BRANCH
N_KV_STREAMS = 6
CMP_BLOCK = 32
CMP_STRIDE = 16
SEL_BLOCK = 64
SEL_TOPK = 16
WINDOW = 512
QUERY_BLOCK = 128
SGU_GROUPS = 8
SGU_CHUNK = 128
MEM_HEADS = 4
MEM_DIM = 128

LANES = 128
SUBLANES = 8
COV_RATIO = SEL_BLOCK // CMP_STRIDE
COV_LEAD = CMP_BLOCK // CMP_STRIDE - 1
COV_BAND = tuple(
    max(min((k - COV_LEAD) * CMP_STRIDE + CMP_BLOCK, SEL_BLOCK) - max((k - COV_LEAD) * CMP_STRIDE, 0), 0) / CMP_BLOCK
    for k in range(COV_RATIO + COV_LEAD))
assert SEL_BLOCK % CMP_STRIDE == 0 and CMP_BLOCK % CMP_STRIDE == 0 and COV_LEAD <= SUBLANES
SEL_TILE = 512
SEL_BPT = SEL_TILE // SEL_BLOCK
SEL_LANES = 256
SEL_UNROLLS = (4, 2)
NSA_STEP_BLOCKS = SEL_TILE // QUERY_BLOCK
NSA_BUCKETS = 8
BF16_ROWS = 16
FFN_CHUNK = 256
FFN_OUT_TN = 1024
FFN_NORM_ROWS = 256
PROJ_TM, SGU_TM, MIX_TM, FFN_TM, FFN_TH = 1024, 512, 256, 1024, 512
VMEM_LIMIT = 62 * 1024 * 1024


def _params(*sem):
    return pltpu.CompilerParams(dimension_semantics=sem, vmem_limit_bytes=VMEM_LIMIT)


def _rms(x, g):
    return x * lax.rsqrt(jnp.mean(x * x, axis=-1, keepdims=True) + NORM_EPS) * g


def _dot(a, b):
    return jnp.dot(a, b, preferred_element_type=F32)


def _ones_row(n):
    return jnp.where(lax.broadcasted_iota(jnp.int32, (BF16_ROWS, n), 0) == 0, 1.0, 0.0).astype(BF16)


def _norm_matmul_kernel(x_ref, g_ref, w_ref, o_ref, h_ref):
    @pl.when(pl.program_id(1) == 0)
    def _():
        h_ref[...] = _rms(x_ref[...], g_ref[...]).astype(BF16)

    o_ref[...] = _dot(h_ref[...], w_ref[...]).astype(o_ref.dtype)


def _norm_matmul(x, g, w, out_dtype, tm, tn):
    s, d = x.shape
    n = w.shape[1]
    return pl.pallas_call(
        _norm_matmul_kernel,
        grid=(s // tm, n // tn),
        in_specs=[
            pl.BlockSpec((tm, d), lambda i, j: (i, 0)),
            pl.BlockSpec((1, d), lambda i, j: (0, 0)),
            pl.BlockSpec((d, tn), lambda i, j: (0, j)),
        ],
        out_specs=pl.BlockSpec((tm, tn), lambda i, j: (i, j)),
        out_shape=jax.ShapeDtypeStruct((s, n), out_dtype),
        scratch_shapes=[pltpu.VMEM((tm, d), BF16)],
        compiler_params=_params("parallel", "arbitrary"),
        name="norm_matmul",
    )(x, g.reshape(1, d), w)


PROJ_TN = 512
PROJ_GATE_TN = 256


def _proj_kernel(x_ref, g_ref, wa_ref, wb_ref, nat_ref, kw_ref, ksa_ref, vsa_ref, vwa_ref, q_ref, o_ref, gt_ref,
                 h_ref, stage_ref):
    i = pl.program_id(0)
    j = pl.program_id(1)
    tm = x_ref.shape[0]
    dh = NSA_DIM
    ng = NSA_GROUPS
    tn = PROJ_TN
    qw = q_ref.shape[1]
    nb = pl.num_programs(1) - 1

    @pl.when(j == 0)
    def _():
        h_ref[...] = _rms(x_ref[...], g_ref[...]).astype(BF16)

    @pl.when(j < nb)
    def _():
        o_ref[...] = _dot(h_ref[...], wb_ref[...])

    @pl.when(j == nb)
    def _():
        h = h_ref[...]

        def tile(t):
            return _dot(h, wa_ref[:, qw + t * tn:qw + (t + 1) * tn])

        def cols(res, c):
            return res[:, c * dh:(c + 1) * dh].astype(BF16)

        def cols_t(res, c):
            slab = res[:, (c // 2) * 2 * dh:(c // 2 + 1) * 2 * dh].T
            return slab[(c % 2) * dh:(c % 2 + 1) * dh].astype(BF16)

        res = tile(0)
        for c in range(2 * ng):
            stage_ref[...] = res[:, c * dh:(c + 1) * dh]
            for t in range(0, CMP_STRIDE, 2):
                pair = [stage_ref[pl.ds(t + u, tm // CMP_STRIDE, stride=CMP_STRIDE), :] for u in range(2)]
                nat_ref[c // ng, c % ng, :, t * dh:(t + 2) * dh] = jnp.concatenate(pair, axis=1).astype(BF16)

        res = tile(1)
        pos = i * tm + lax.broadcasted_iota(jnp.int32, (tm, dh), 0)
        lane = lax.broadcasted_iota(jnp.int32, (tm, dh), 1)
        onehot = jnp.where((pos // SEL_BLOCK) % SEL_BPT == lane, 1.0, 0.0).astype(BF16)
        for g in range(ng):
            ksa_ref[g, :, :dh] = cols(res, g)
            ksa_ref[g, :, dh:] = onehot
            vsa_ref[g, :dh, :] = cols_t(res, ng + g)
            vsa_ref[g, dh:, :] = _ones_row(tm)

        res = tile(2)
        for g in range(ng):
            kw_ref[g] = cols(res, g)
            vwa_ref[g, :dh, :] = cols_t(res, ng + g)
            vwa_ref[g, dh:, :] = _ones_row(tm)

        o_gate = qw + N_KV_STREAMS // 2 * tn
        res = _dot(h, wa_ref[:, o_gate:o_gate + PROJ_GATE_TN])
        gt_ref[...] = res[:, :LANES].T

        for t in range(qw // tn):
            q_ref[:, t * tn:(t + 1) * tn] = _dot(h, wa_ref[:, t * tn:(t + 1) * tn]).astype(BF16)


def _proj(x, g, w_a, w_b, tm):
    s, d = x.shape
    dh, ng, tn = NSA_DIM, NSA_GROUPS, PROJ_TN
    qw = NSA_HEADS * dh
    assert 2 * ng * dh == tn and qw % tn == 0 and w_a.shape[1] == qw + N_KV_STREAMS // 2 * tn + PROJ_GATE_TN
    n = w_b.shape[1]
    assert n % tn == 0 and GATE_ROWS <= LANES <= PROJ_GATE_TN
    nb = n // tn
    shapes = [
        jax.ShapeDtypeStruct((2, ng, s // CMP_STRIDE, CMP_STRIDE * dh), BF16),
        jax.ShapeDtypeStruct((ng, s, dh), BF16),
        jax.ShapeDtypeStruct((ng, s, 2 * dh), BF16),
        jax.ShapeDtypeStruct((ng, dh + BF16_ROWS, s), BF16),
        jax.ShapeDtypeStruct((ng, dh + BF16_ROWS, s), BF16),
        jax.ShapeDtypeStruct((s, NSA_HEADS * dh), BF16),
        jax.ShapeDtypeStruct((s, n), F32),
        jax.ShapeDtypeStruct((LANES, s), F32),
    ]
    return pl.pallas_call(
        _proj_kernel,
        grid=(s // tm, 1 + nb),
        in_specs=[
            pl.BlockSpec((tm, d), lambda i, j: (i, 0)),
            pl.BlockSpec((1, d), lambda i, j: (0, 0)),
            pl.BlockSpec(w_a.shape, lambda i, j: (0, 0), pipeline_mode=pl.Buffered(1)),
            pl.BlockSpec((d, tn), lambda i, j: (0, jnp.minimum(j, nb - 1))),
        ],
        out_specs=[
            pl.BlockSpec((2, ng, tm // CMP_STRIDE, CMP_STRIDE * dh), lambda i, j: (0, 0, i, 0)),
            pl.BlockSpec((ng, tm, dh), lambda i, j: (0, i, 0)),
            pl.BlockSpec((ng, tm, 2 * dh), lambda i, j: (0, i, 0)),
            pl.BlockSpec((ng, dh + BF16_ROWS, tm), lambda i, j: (0, 0, i)),
            pl.BlockSpec((ng, dh + BF16_ROWS, tm), lambda i, j: (0, 0, i)),
            pl.BlockSpec((tm, qw), lambda i, j: (i, 0)),
            pl.BlockSpec((tm, tn), lambda i, j: (i, jnp.minimum(j, nb - 1))),
            pl.BlockSpec((LANES, tm), lambda i, j: (0, i)),
        ],
        out_shape=shapes,
        scratch_shapes=[pltpu.VMEM((tm, d), BF16), pltpu.VMEM((tm, dh), F32)],
        compiler_params=_params("parallel", "arbitrary"),
        name="proj",
    )(x, g.reshape(1, d), w_a, w_b)


def _compress_kernel(r_ref, w1_ref, pe_ref, b1_ref, w2_ref, o_ref, ot_ref):
    half = r_ref.shape[-1]
    r = r_ref[0, 0]
    bias = _dot(pe_ref[0].astype(BF16), w1_ref[0]) + b1_ref[0]
    top = _dot(r, w1_ref[0, :half, :])
    bot = _dot(r, w1_ref[0, half:, :])
    ncp = r.shape[0]
    hid = top + pltpu.roll(bot, ncp - 1, 0) + bias
    out = _dot(jax.nn.gelu(hid).astype(BF16), w2_ref[0])
    o_ref[0, 0] = out.astype(o_ref.dtype)
    dh = out.shape[1]
    out_t = jnp.concatenate([out, jnp.zeros((ncp, LANES - dh), F32)], axis=1).T
    ot_ref[0, 0, :dh, :] = out_t[:dh].astype(ot_ref.dtype)
    ot_ref[0, 0, dh:, :] = _ones_row(ncp)


def _compress(r, w1, pe, b1, w2):
    _, g, ncp, half = r.shape
    hidden = w1.shape[-1]
    dh = w2.shape[-1]
    return pl.pallas_call(
        _compress_kernel,
        grid=(2, g),
        in_specs=[
            pl.BlockSpec((1, 1, ncp, half), lambda s, gg: (s, gg, 0, 0)),
            pl.BlockSpec((1, 2 * half, hidden), lambda s, gg: (s, 0, 0)),
            pl.BlockSpec((1, 1, 2 * half), lambda s, gg: (s, 0, 0)),
            pl.BlockSpec((1, 1, hidden), lambda s, gg: (s, 0, 0)),
            pl.BlockSpec((1, hidden, dh), lambda s, gg: (s, 0, 0)),
        ],
        out_specs=[pl.BlockSpec((1, 1, ncp, dh), lambda s, gg: (s, gg, 0, 0)),
                   pl.BlockSpec((1, 1, dh + BF16_ROWS, ncp), lambda s, gg: (s, gg, 0, 0))],
        out_shape=[jax.ShapeDtypeStruct((2, g, ncp, dh), BF16),
                   jax.ShapeDtypeStruct((2, g, dh + BF16_ROWS, ncp), BF16)],
        compiler_params=_params("parallel", "parallel"),
        name="compress",
    )(r, w1, pe, b1, w2)


PICKED = -2.0


def _nsa_kernel(*refs, n_cast):
    q_ref, gt_ref, kc_ref, vct_ref, ks_ref, vst_ref, kw_ref, vwt_ref = refs[:8]
    cast_refs, refs = refs[8:8 + n_cast], refs[8 + n_cast:]
    o_ref, cast_out_refs, refs = refs[0], refs[1:1 + n_cast], refs[1 + n_cast:]
    qa_ref, sel_ref, part_ref, gates_ref, s0_ref, s1_ref, acc_ref, blkf_ref, cend_ref, wrel_ref, pp_ref = refs
    g = pl.program_id(0)
    step = pl.program_id(1)
    qb = QUERY_BLOCK
    hq = NSA_HPG * qb
    nb = sel_ref.shape[1]
    ncp = kc_ref.shape[1]
    s_len = ks_ref.shape[1]
    first = step * NSA_STEP_BLOCKS

    @pl.when(step == 0)
    def _():
        lane = lax.broadcasted_iota(jnp.int32, cend_ref.shape, 1)
        cend_ref[...] = lax.broadcasted_iota(jnp.int32, cend_ref.shape, 0) * CMP_STRIDE + (CMP_BLOCK - 1) - lane
        wrel_ref[...] = (lax.broadcasted_iota(jnp.int32, wrel_ref.shape, 0)
                         - lax.broadcasted_iota(jnp.int32, wrel_ref.shape, 1))

    def rows(sub):
        return pl.ds(pl.multiple_of(sub * qb, qb), qb)

    bucket_ok = s_len % (NSA_BUCKETS * SEL_TILE) == 0 and nb // NSA_BUCKETS >= SEL_TOPK
    n_bucket = NSA_BUCKETS if bucket_ok else 1
    bucket = ((first + NSA_STEP_BLOCKS) * qb - 1) // (s_len // n_bucket)
    for b in range(n_bucket):
        nck, nbk = (b + 1) * (ncp // n_bucket), (b + 1) * (nb // n_bucket)

        @pl.when(bucket == b)
        def _():
            def scores(sub, carry):
                own = pl.ds(pl.multiple_of(sub * hq, hq), hq)
                _nsa_scores(g, first + sub, q_ref.at[rows(sub), :], gt_ref.at[:, rows(sub)], kc_ref, vct_ref, kw_ref,
                            vwt_ref, qa_ref.at[:, own], sel_ref.at[sub], part_ref.at[sub], gates_ref.at[sub], pp_ref,
                            cend_ref, wrel_ref, nck, nbk)
                return carry

            lax.fori_loop(0, NSA_STEP_BLOCKS, scores, 0)
            _nsa_topk(first, sel_ref, blkf_ref, nbk)

    for src, dst in zip(cast_refs, cast_out_refs):
        dst[...] = src[...].astype(BF16)
    _nsa_select(g, step, ks_ref, vst_ref, o_ref, qa_ref, sel_ref, part_ref, gates_ref, s0_ref, s1_ref, acc_ref)


def _gate(gates_ref, g, branch):
    rows = [gates_ref[pl.ds((g * NSA_HPG + h) * N_BRANCH + branch, 1), :] for h in range(NSA_HPG)]
    return jax.nn.sigmoid(jnp.concatenate(rows, axis=1))


def _nsa_scores(g, i, q_ref, gt_ref, kc_ref, vct_ref, kw_ref, vwt_ref, qa_ref, score_ref, part_ref, gates_ref,
                pp_ref, cend_ref, wrel_ref, nck, nbk):
    qb = QUERY_BLOCK
    start = i * qb
    hq = NSA_HPG * qb
    t_row = start + lax.broadcasted_iota(jnp.int32, (1, qb), 1)
    jt = t_row // SEL_BLOCK

    qt = (q_ref[...].astype(F32) * (NSA_DIM ** -0.5 * LOG2E)).T
    q_t = jnp.concatenate([qt[h * NSA_DIM:(h + 1) * NSA_DIM] for h in range(NSA_HPG)], axis=1).astype(BF16)
    qa_ref[:NSA_DIM, :] = q_t
    qa_ref[NSA_DIM:, :] = jnp.zeros((qa_ref.shape[0] - NSA_DIM, hq), BF16)
    gates_ref[...] = gt_ref[...]

    def exps(s, bias):
        out = []
        for h in range(NSA_HPG):
            sh = s[:, h * qb:(h + 1) * qb] + bias
            out.append(jnp.exp2(sh - jnp.max(sh, axis=0, keepdims=True)))
        return out

    wk = WINDOW + qb
    k0w = pl.multiple_of(jnp.maximum(start - WINDOW, 0), qb)
    sc = _dot(kc_ref[0, :nck, :], q_t)
    sw = _dot(kw_ref[0, pl.ds(k0w, wk), :], q_t)
    e_cmp = exps(sc, jnp.where(cend_ref[:nck, :] <= start, 0.0, MASK_VALUE))
    o_cmp = _dot(vct_ref[0, :, :nck], jnp.concatenate([e.astype(BF16) for e in e_cmp], axis=1))
    wrel = wrel_ref[...]
    e_win = exps(sw, jnp.where((wrel <= start - k0w) & (wrel > start - k0w - WINDOW), 0.0, MASK_VALUE))
    o_win = _dot(vwt_ref[0, :, pl.ds(k0w, wk)], jnp.concatenate([e.astype(BF16) for e in e_win], axis=1))
    inv_cmp = jnp.where(jnp.concatenate([t_row >= CMP_BLOCK - 1] * NSA_HPG, axis=1),
                        1.0 / o_cmp[NSA_DIM:NSA_DIM + 1, :], 0.0)
    psum = e_cmp[0] * inv_cmp[:, :qb]
    for h in range(1, NSA_HPG):
        psum = psum + e_cmp[h] * inv_cmp[:, h * qb:(h + 1) * qb]
    part_ref[...] = (_gate(gates_ref, g, 0) * inv_cmp * o_cmp[:NSA_DIM]
                     + _gate(gates_ref, g, 2) * (1.0 / o_win[NSA_DIM:NSA_DIM + 1, :]) * o_win[:NSA_DIM])
    pp_ref[:SUBLANES, :] = jnp.zeros((SUBLANES, qb), F32)
    pp_ref[SUBLANES:SUBLANES + nck, :] = psum
    imp = None
    for k, w in enumerate(COV_BAND):
        tap = pp_ref[pl.ds(SUBLANES - COV_LEAD + k, nbk, stride=COV_RATIO), :]
        tap = tap if w == 1.0 else w * tap
        imp = tap if imp is None else imp + tap
    blk = lax.broadcasted_iota(jnp.int32, (nbk, qb), 0)
    forced = (blk == 0) | (blk == jt) | (blk == jt - 1)
    score_ref[:nbk, :] = jnp.where(forced, PICKED, jnp.where(blk <= jt, imp, -1.0))


def _nsa_topk(first, sel_ref, blkf_ref, nbk):
    qb = QUERY_BLOCK
    blk = lax.broadcasted_iota(jnp.int32, (nbk, qb), 0)
    blkf_ref[:nbk, :] = blk.astype(F32)

    def pick(_, scores):
        out = []
        for score in scores:
            mx = jnp.max(score, axis=0, keepdims=True)
            cand = jnp.where(score == mx, blkf_ref[:nbk, :], float(nbk))
            out.append(jnp.where(cand == jnp.min(cand, axis=0, keepdims=True), PICKED, score))
        return tuple(out)

    scores = tuple(sel_ref[b, :nbk, :] for b in range(sel_ref.shape[0]))
    scores = lax.fori_loop(0, min(SEL_TOPK, nbk) - N_FORCED, pick, scores)
    for b, score in enumerate(scores):
        jt = ((first + b) * qb + lax.broadcasted_iota(jnp.int32, (1, qb), 1)) // SEL_BLOCK
        sel_ref[b, :nbk, :] = jnp.where((score == PICKED) & (blk <= jt), 0.0, MASK_VALUE)


def _nsa_select(g, step, ks_ref, vst_ref, o_ref, qa_ref, sel_ref, part_ref, gates_ref, s0_ref, s1_ref, acc_ref):
    s_refs = (s0_ref, s1_ref)
    qb = QUERY_BLOCK
    hq = NSA_HPG * qb
    nbq = sel_ref.shape[0]
    lanes = nbq * hq
    tk = SEL_TILE

    def bias_rows(kt):
        rows = []
        for b in range(nbq):
            b8 = sel_ref[b, pl.ds(pl.multiple_of(kt * SEL_BPT, SEL_BPT), SEL_BPT), :]
            b16 = jnp.concatenate([b8, jnp.zeros((BF16_ROWS - SEL_BPT, qb), F32)], axis=0).astype(BF16)
            rows += [b16] * NSA_HPG
        qa_ref[NSA_DIM:NSA_DIM + BF16_ROWS, :] = jnp.concatenate(rows, axis=1)

    def scores(slot, kt):
        k0 = pl.multiple_of(kt * tk, tk)
        bias_rows(kt)
        s = _dot(ks_ref[0, pl.ds(k0, tk), :], qa_ref[...])
        s_refs[slot][...] = s
        return jnp.max(s, axis=0, keepdims=True)

    def update(slot, kt, mt, m, causal):
        k0 = pl.multiple_of(kt * tk, tk)
        if causal:
            tri = jnp.where(lax.broadcasted_iota(jnp.int32, (qb, qb), 0)
                            <= lax.broadcasted_iota(jnp.int32, (qb, qb), 1), 0.0, MASK_VALUE)
            for b in range(nbq):
                own = (slice(b * qb, (b + 1) * qb), slice(b * hq, (b + 1) * hq))
                s_refs[slot][own] = s_refs[slot][own] + jnp.concatenate([tri] * NSA_HPG, axis=1)
            mt = jnp.max(s_refs[slot][...], axis=0, keepdims=True)
        s = s_refs[slot][...]
        m_new = jnp.maximum(m, mt)
        p = jnp.exp2(s - m_new).astype(BF16)
        acc_ref[...] = jnp.exp2(m - m_new) * acc_ref[...] + _dot(vst_ref[0, :, pl.ds(k0, tk)], p)
        return m_new

    def accumulate(pend):
        p, k0, alpha, ln = pend
        acc_ref[:, ln] = alpha * acc_ref[:, ln] + _dot(vst_ref[0, :, pl.ds(k0, tk)], p)

    def run(first, n_tiles, carry):
        mt, m = carry
        pend = None
        for k in range(n_tiles):
            cur, nxt = s_refs[k % 2], s_refs[(k + 1) % 2]
            k0c = pl.multiple_of((first + k) * tk, tk)
            k0n = pl.multiple_of((first + k + 1) * tk, tk)
            bias_rows(first + k + 1)
            m_new = jnp.maximum(m, mt)
            alpha = jnp.exp2(m - m_new)
            mts = []
            for n in range(lanes // SEL_LANES):
                ln = slice(n * SEL_LANES, (n + 1) * SEL_LANES)
                s_n = _dot(ks_ref[0, pl.ds(k0n, tk), :], qa_ref[:, ln])
                nxt[:, ln] = s_n
                mts.append(jnp.max(s_n, axis=0, keepdims=True))
                p = jnp.exp2(cur[:, ln] - m_new[:, ln]).astype(BF16)
                if pend is not None:
                    accumulate(pend)
                pend = (p, k0c, alpha[:, ln], ln)
            mt = jnp.concatenate(mts, axis=1)
            m = m_new
        accumulate(pend)
        return mt, m

    acc_ref[...] = jnp.zeros(acc_ref.shape, F32)
    last = step
    carry = (scores(0, 0), jnp.full((1, lanes), MASK_VALUE, F32))
    done = 0
    for unroll in SEL_UNROLLS:
        trips = (last - done) // unroll
        carry = lax.fori_loop(0, trips, lambda it, c, d=done, u=unroll: run(d + it * u, u, c), carry)
        done = done + trips * unroll
    mt_a, m = carry

    @pl.when(last % 2 == 0)
    def _():
        update(0, last, mt_a, m, True)

    @pl.when(last % 2 == 1)
    def _():
        mt_b, m_b = run(last - 1, 1, (mt_a, m))
        update(1, last, mt_b, m_b, True)

    for b in range(nbq):
        own = slice(b * hq, (b + 1) * hq)
        o_sel = acc_ref[:NSA_DIM, own] * (1.0 / acc_ref[NSA_DIM:NSA_DIM + 1, own])
        out_t = part_ref[b] + _gate(gates_ref.at[b], g, 1) * o_sel
        outs = [out_t[:, h * qb:(h + 1) * qb] for h in range(NSA_HPG)]
        o_ref[b * qb:(b + 1) * qb, :] = jnp.concatenate(outs, axis=0).T.astype(o_ref.dtype)


def _cast_rows(rows, n_steps):
    for k in range(1, rows // BF16_ROWS + 1):
        if rows % (k * BF16_ROWS) == 0 and rows // (k * BF16_ROWS) <= n_steps:
            return k * BF16_ROWS
    raise ValueError(f"no row block for a side cast of {rows} rows in {n_steps} steps")


def _nsa(q, gates_t, kc, vc_aug_t, ks_aug, vs_aug_t, kw, vw_aug_t, casts):
    s = q.shape[0]
    assert s % (2 * SEL_TILE) == 0 and s >= WINDOW + QUERY_BLOCK
    gdim = NSA_HPG * NSA_DIM
    ncp = kc.shape[1]
    assert ncp * CMP_STRIDE == s and s // SEL_BLOCK * COV_RATIO == ncp
    nb = s // SEL_BLOCK
    ka = ks_aug.shape[-1]
    va = vs_aug_t.shape[1]
    hq = NSA_HPG * QUERY_BLOCK
    per_group = lambda shape: pl.BlockSpec((1,) + shape, lambda g, i: (g, 0, 0))
    rows = NSA_STEP_BLOCKS * QUERY_BLOCK
    assert s % rows == 0
    n_i = s // rows

    def cast_spec(w):
        rb = _cast_rows(w.shape[0], NSA_GROUPS * n_i)
        return pl.BlockSpec((rb, w.shape[1]), lambda g, i: (jnp.minimum(g * n_i + i, w.shape[0] // rb - 1), 0))

    return pl.pallas_call(
        functools.partial(_nsa_kernel, n_cast=len(casts)),
        grid=(NSA_GROUPS, n_i),
        in_specs=[
            pl.BlockSpec((rows, gdim), lambda g, i: (i, g)),
            pl.BlockSpec((GATE_ROWS, rows), lambda g, i: (0, i)),
            per_group((ncp, NSA_DIM)),
            per_group((va, ncp)),
            per_group((s, ka)),
            per_group((va, s)),
            per_group((s, NSA_DIM)),
            per_group((va, s)),
        ] + [cast_spec(w) for w in casts],
        out_specs=[pl.BlockSpec((rows, gdim), lambda g, i: (i, g))] + [cast_spec(w) for w in casts],
        out_shape=[jax.ShapeDtypeStruct((s, NSA_HEADS * NSA_DIM), BF16)]
        + [jax.ShapeDtypeStruct(w.shape, BF16) for w in casts],
        scratch_shapes=[pltpu.VMEM((ka, NSA_STEP_BLOCKS * hq), BF16),
                        pltpu.VMEM((NSA_STEP_BLOCKS, nb, QUERY_BLOCK), F32),
                        pltpu.VMEM((NSA_STEP_BLOCKS, NSA_DIM, hq), F32),
                        pltpu.VMEM((NSA_STEP_BLOCKS, GATE_ROWS, QUERY_BLOCK), F32),
                        pltpu.VMEM((SEL_TILE, NSA_STEP_BLOCKS * hq), F32),
                        pltpu.VMEM((SEL_TILE, NSA_STEP_BLOCKS * hq), F32),
                        pltpu.VMEM((va, NSA_STEP_BLOCKS * hq), F32),
                        pltpu.VMEM((nb, QUERY_BLOCK), F32),
                        pltpu.VMEM((ncp, QUERY_BLOCK), jnp.int32),
                        pltpu.VMEM((WINDOW + QUERY_BLOCK, QUERY_BLOCK), jnp.int32),
                        pltpu.VMEM((SUBLANES + ncp, QUERY_BLOCK), F32)],
        compiler_params=_params("parallel", "arbitrary"),
        name="nsa",
    )(q, gates_t, kc, vc_aug_t, ks_aug, vs_aug_t, kw, vw_aug_t, *casts)


def _sgu_kernel(u_ref, v_ref, lng_ref, lnb_ref, ws_ref, bs_ref, o_ref):
    c = SGU_CHUNK
    tm = u_ref.shape[0]
    v = jax.nn.gelu(v_ref[...])
    mu = jnp.mean(v, axis=-1, keepdims=True)
    var = jnp.mean(jnp.square(v - mu), axis=-1, keepdims=True)
    vn = ((v - mu) * lax.rsqrt(var + NORM_EPS) * lng_ref[...] + lnb_ref[...]).astype(BF16)
    u = jax.nn.gelu(u_ref[...])
    tri = lax.broadcasted_iota(jnp.int32, (c, c), 0) >= lax.broadcasted_iota(jnp.int32, (c, c), 1)
    for g in range(SGU_GROUPS):
        w = jnp.where(tri, ws_ref[g], 0.0).astype(BF16)
        cols = slice(g * c, (g + 1) * c)
        rhs = jnp.concatenate([vn[k * c:(k + 1) * c, cols] for k in range(tm // c)], axis=1)
        mixed = _dot(w, rhs)
        for k in range(tm // c):
            rows = slice(k * c, (k + 1) * c)
            o_ref[rows, cols] = (u[rows, cols] * (mixed[:, rows] + bs_ref[g])).astype(o_ref.dtype)


def _sgu(proj, lng, lnb, ws, bs, tm):
    s = proj.shape[0]
    w = lng.shape[0]
    c = SGU_CHUNK
    bs_b = jnp.broadcast_to(bs[:, :, None], (SGU_GROUPS, c, c))
    return pl.pallas_call(
        _sgu_kernel,
        grid=(s // tm,),
        in_specs=[
            pl.BlockSpec((tm, w), lambda i: (i, 0)),
            pl.BlockSpec((tm, w), lambda i: (i, 1)),
            pl.BlockSpec((1, w), lambda i: (0, 0)),
            pl.BlockSpec((1, w), lambda i: (0, 0)),
            pl.BlockSpec((SGU_GROUPS, c, c), lambda i: (0, 0, 0)),
            pl.BlockSpec((SGU_GROUPS, c, c), lambda i: (0, 0, 0)),
        ],
        out_specs=pl.BlockSpec((tm, w), lambda i: (i, 0)),
        out_shape=jax.ShapeDtypeStruct((s, w), BF16),
        compiler_params=_params("parallel"),
        name="sgu",
    )(proj, proj, lng.reshape(1, w), lnb.reshape(1, w), ws, bs_b)


def _mix_kernel(oa_ref, ob_ref, ga_ref, gb_ref, x_ref, pa_ref, pb_ref, wo_ref,
                gc_ref, wq_ref, mkt_ref, mv_ref, wmo_ref, o_ref):
    a = _dot(oa_ref[...], pa_ref[...])
    b = _dot(ob_ref[...], pb_ref[...])
    merged = jax.nn.sigmoid(ga_ref[...]) * a + jax.nn.sigmoid(gb_ref[...]) * b
    x = x_ref[...] + _dot(merged.astype(BF16), wo_ref[...])
    h = _rms(x, gc_ref[...]).astype(BF16)
    mq = (_dot(h, wq_ref[...]) * (MEM_DIM ** -0.5)).astype(BF16)
    outs = []
    for hh in range(MEM_HEADS):
        s = _dot(mq[:, hh * MEM_DIM:(hh + 1) * MEM_DIM], mkt_ref[hh])
        e = jnp.exp(s - jnp.max(s, axis=-1, keepdims=True))
        p = e / jnp.sum(e, axis=-1, keepdims=True)
        outs.append(_dot(p.astype(BF16), mv_ref[hh]).astype(BF16))
    o_ref[...] = x + _dot(jnp.concatenate(outs, axis=1), wmo_ref[...])


def _mix(o_a, o_b, proj, x, p_a, p_b, w_o, gate_block, g_cross, w_q, mk_t, mv, w_mo, tm):
    s, d = x.shape
    wa = o_a.shape[1]
    wb = o_b.shape[1]
    mw = w_q.shape[1]
    m = mv.shape[1]
    resident = lambda shape: pl.BlockSpec(shape, lambda i: (0,) * len(shape), pipeline_mode=pl.Buffered(1))
    return pl.pallas_call(
        _mix_kernel,
        grid=(s // tm,),
        in_specs=[
            pl.BlockSpec((tm, wa), lambda i: (i, 0)),
            pl.BlockSpec((tm, wb), lambda i: (i, 0)),
            pl.BlockSpec((tm, d), lambda i: (i, gate_block)),
            pl.BlockSpec((tm, d), lambda i: (i, gate_block + 1)),
            pl.BlockSpec((tm, d), lambda i: (i, 0)),
            resident((wa, d)),
            resident((wb, d)),
            resident((d, d)),
            resident((1, d)),
            resident((d, mw)),
            resident((MEM_HEADS, MEM_DIM, m)),
            resident((MEM_HEADS, m, MEM_DIM)),
            resident((mw, d)),
        ],
        out_specs=pl.BlockSpec((tm, d), lambda i: (i, 0)),
        out_shape=jax.ShapeDtypeStruct((s, d), F32),
        compiler_params=_params("parallel"),
        name="mix",
    )(o_a, o_b, proj, proj, x, p_a, p_b, w_o, g_cross.reshape(1, d), w_q, mk_t, mv, w_mo)


def _ffn_kernel(x_ref, g_ref, wg_ref, wu_ref, wo_ref, gf_ref, o_ref, h_ref, *, final_norm):
    j = pl.program_id(1)
    row_chunks = [slice(r, r + FFN_NORM_ROWS) for r in range(0, x_ref.shape[0], FFN_NORM_ROWS)]

    @pl.when(j == 0)
    def _():
        for rs in row_chunks:
            x = x_ref[rs, :]
            h_ref[rs, :] = _rms(x, g_ref[...]).astype(BF16)
            o_ref[rs, :] = x

    h = h_ref[...]
    th = wg_ref.shape[1]
    d = o_ref.shape[1]
    for c in range(th // FFN_CHUNK):
        cs = slice(c * FFN_CHUNK, (c + 1) * FFN_CHUNK)
        act = (jax.nn.silu(_dot(h, wg_ref[:, cs])) * _dot(h, wu_ref[:, cs])).astype(BF16)
        for n in range(d // FFN_OUT_TN):
            ns = slice(n * FFN_OUT_TN, (n + 1) * FFN_OUT_TN)
            o_ref[:, ns] += _dot(act, wo_ref[cs, ns])

    if final_norm:
        @pl.when(j == pl.num_programs(1) - 1)
        def _():
            for rs in row_chunks:
                o_ref[rs, :] = _rms(o_ref[rs, :], gf_ref[...])


def _ffn(x, g, w_in, w_out, g_final, final_norm, tm, th):
    s, d = x.shape
    hidden = w_out.shape[0]
    nh = hidden // th
    return pl.pallas_call(
        functools.partial(_ffn_kernel, final_norm=final_norm),
        grid=(s // tm, nh),
        in_specs=[
            pl.BlockSpec((tm, d), lambda i, j: (i, 0)),
            pl.BlockSpec((1, d), lambda i, j: (0, 0)),
            pl.BlockSpec((d, th), lambda i, j: (0, j)),
            pl.BlockSpec((d, th), lambda i, j: (0, j + nh)),
            pl.BlockSpec((th, d), lambda i, j: (j, 0)),
            pl.BlockSpec((1, d), lambda i, j: (0, 0)),
        ],
        out_specs=pl.BlockSpec((tm, d), lambda i, j: (i, 0)),
        out_shape=jax.ShapeDtypeStruct((s, d), F32),
        scratch_shapes=[pltpu.VMEM((tm, d), BF16)],
        compiler_params=_params("parallel", "arbitrary"),
        name="ffn",
    )(x, g.reshape(1, d), w_in, w_in, w_out, g_final.reshape(1, d))


def _layer(x, mem, norm_mix_g, w_in, cmp_pe_k, cmp_k_w1, cmp_k_b1, cmp_k_w2, cmp_pe_v, cmp_v_w1, cmp_v_b1, cmp_v_w2,
           sgu_ln_g, sgu_ln_b, sgu_ws, sgu_b, w_proj_a, w_proj_b, w_mix_out, norm_cross_g, norm_mem_g,
           w_mq, w_mkv, w_mo, w_ffn_in, w_ffn_out):
    s, d = x.shape
    qw = NSA_HEADS * NSA_DIM
    kvw = NSA_GROUPS * NSA_DIM
    sguw = sgu_ln_g.shape[0]
    ngate = GATE_ROWS
    o_kv = qw
    o_gate = o_kv + N_KV_STREAMS * kvw
    o_u = o_gate + ngate

    w_a = w_in[:, :o_gate + PROJ_GATE_TN].astype(BF16)
    w_b = w_in[:, o_u:].astype(BF16)
    nat, kw, ks_aug, vs_aug_t, vw_aug_t, q, proj_b, gates_t = _proj(x, norm_mix_g, w_a, w_b, min(PROJ_TM, s))

    w1 =jnp.stack([cmp_k_w1, cmp_v_w1]).astype(BF16)
    pe = jnp.stack([cmp_pe_k.reshape(1, -1), cmp_pe_v.reshape(1, -1)])
    b1 = jnp.stack([cmp_k_b1.reshape(1, -1), cmp_v_b1.reshape(1, -1)])
    w2 = jnp.stack([cmp_k_w2, cmp_v_w2]).astype(BF16)
    cmp, cmp_t = _compress(nat, w1, pe, b1, w2)
    o_a, w_ffn_in, w_ffn_out, w_proj_a, w_proj_b, w_mix_out, w_mq, w_mo = _nsa(
        q, gates_t, cmp[0], cmp_t[1], ks_aug, vs_aug_t, kw, vw_aug_t,
        [w_ffn_in, w_ffn_out, w_proj_a, w_proj_b, w_mix_out, w_mq, w_mo])

    o_b = _sgu(proj_b, sgu_ln_g, sgu_ln_b, sgu_ws, sgu_b, min(SGU_TM, s))

    m = mem.shape[0]
    mw = MEM_HEADS * MEM_DIM
    mkv = _norm_matmul(mem, norm_mem_g, w_mkv.astype(BF16), F32, m, mw)
    mk_t = mkv[:, :mw].reshape(m, MEM_HEADS, MEM_DIM).transpose(1, 2, 0).astype(BF16)
    mv = mkv[:, mw:].reshape(m, MEM_HEADS, MEM_DIM).transpose(1, 0, 2).astype(BF16)
    x = _mix(o_a, o_b, proj_b, x, w_proj_a, w_proj_b, w_mix_out, (2 * sguw) // d, norm_cross_g, w_mq, mk_t, mv, w_mo,
             min(MIX_TM, s))
    return x, w_ffn_in, w_ffn_out


def kernel(x, mem, norm_mix_g, w_in, cmp_pe_k, cmp_k_w1, cmp_k_b1, cmp_k_w2, cmp_pe_v, cmp_v_w1, cmp_v_b1, cmp_v_w2, sgu_ln_g, sgu_ln_b, sgu_ws, sgu_b, w_proj_a, w_proj_b, w_mix_out, norm_cross_g, norm_mem_g, w_mq, w_mkv, w_mo, norm_ffn_g, w_ffn_in, w_ffn_out, norm_final_g):
    b, s, d = x.shape
    depth = w_in.shape[0]
    outs = []
    for bi in range(b):
        xb = x[bi]
        for l in range(depth):
            last = l == depth - 1
            xb, wi, wo = _layer(xb, mem[bi], norm_mix_g[l], w_in[l], cmp_pe_k[l], cmp_k_w1[l], cmp_k_b1[l],
                                cmp_k_w2[l], cmp_pe_v[l], cmp_v_w1[l], cmp_v_b1[l], cmp_v_w2[l], sgu_ln_g[l],
                                sgu_ln_b[l], sgu_ws[l], sgu_b[l], w_proj_a[l], w_proj_b[l], w_mix_out[l],
                                norm_cross_g[l], norm_mem_g[l], w_mq[l], w_mkv[l], w_mo[l], w_ffn_in[l], w_ffn_out[l])
            xb = _ffn(xb, norm_ffn_g[l], wi, wo, norm_final_g, last, min(FFN_TM, s), FFN_TH)
        outs.append(xb)
    return jnp.stack(outs)
```

```python
import functools
import math

import jax
import jax.numpy as jnp
from jax import lax
from jax.experimental import pallas as pl
from jax.experimental.pallas import tpu as pltpu

F32 = jnp.float32
BF16 = jnp.bfloat16

NORM_EPS = 1e-6
MASK_VALUE = -1e30
N_FORCED = 3
N_BRANCH = 3
LOG2E = math.log2(math.e)

NSA_HEADS = 16
NSA_GROUPS = 4
NSA_HPG = NSA_HEADS // NSA_GROUPS
NSA_DIM = 64
GATE_ROWS = NSA_HEADS * N_BRANCH
N_KV_STREAMS = 6
CMP_BLOCK = 32
CMP_STRIDE = 16
SEL_BLOCK = 64
SEL_TOPK = 16
WINDOW = 512
QUERY_BLOCK = 128
SGU_GROUPS = 8
SGU_CHUNK = 128
MEM_HEADS = 4
MEM_DIM = 128

LANES = 128
SUBLANES = 8
COV_RATIO = SEL_BLOCK // CMP_STRIDE
COV_LEAD = CMP_BLOCK // CMP_STRIDE - 1
COV_BAND = tuple(
    max(min((k - COV_LEAD) * CMP_STRIDE + CMP_BLOCK, SEL_BLOCK) - max((k - COV_LEAD) * CMP_STRIDE, 0), 0) / CMP_BLOCK
    for k in range(COV_RATIO + COV_LEAD))
assert SEL_BLOCK % CMP_STRIDE == 0 and CMP_BLOCK % CMP_STRIDE == 0 and COV_LEAD <= SUBLANES
SEL_TILE = 512
SEL_BPT = SEL_TILE // SEL_BLOCK
SEL_LANES = 256
SEL_UNROLLS = (4, 2)
NSA_STEP_BLOCKS = SEL_TILE // QUERY_BLOCK
NSA_BUCKETS = 16
BF16_ROWS = 16
FFN_CHUNK = 256
FFN_OUT_TN = 1024
FFN_NORM_ROWS = 256
PROJ_TM, SGU_TM, MIX_TM, FFN_TM, FFN_TH = 1024, 512, 256, 1024, 512
VMEM_LIMIT = 62 * 1024 * 1024


def _params(*sem):
    return pltpu.CompilerParams(dimension_semantics=sem, vmem_limit_bytes=VMEM_LIMIT)


def _rms(x, g):
    return x * lax.rsqrt(jnp.mean(x * x, axis=-1, keepdims=True) + NORM_EPS) * g


def _dot(a, b):
    return jnp.dot(a, b, preferred_element_type=F32)


def _ones_row(n):
    return jnp.where(lax.broadcasted_iota(jnp.int32, (BF16_ROWS, n), 0) == 0, 1.0, 0.0).astype(BF16)


def _norm_matmul_kernel(x_ref, g_ref, w_ref, o_ref, h_ref):
    @pl.when(pl.program_id(1) == 0)
    def _():
        h_ref[...] = _rms(x_ref[...], g_ref[...]).astype(BF16)

    o_ref[...] = _dot(h_ref[...], w_ref[...]).astype(o_ref.dtype)


def _norm_matmul(x, g, w, out_dtype, tm, tn):
    s, d = x.shape
    n = w.shape[1]
    return pl.pallas_call(
        _norm_matmul_kernel,
        grid=(s // tm, n // tn),
        in_specs=[
            pl.BlockSpec((tm, d), lambda i, j: (i, 0)),
            pl.BlockSpec((1, d), lambda i, j: (0, 0)),
            pl.BlockSpec((d, tn), lambda i, j: (0, j)),
        ],
        out_specs=pl.BlockSpec((tm, tn), lambda i, j: (i, j)),
        out_shape=jax.ShapeDtypeStruct((s, n), out_dtype),
        scratch_shapes=[pltpu.VMEM((tm, d), BF16)],
        compiler_params=_params("parallel", "arbitrary"),
        name="norm_matmul",
    )(x, g.reshape(1, d), w)


PROJ_TN = 512
PROJ_GATE_TN = 256


def _proj_kernel(x_ref, g_ref, wa_ref, wb_ref, nat_ref, kw_ref, ksa_ref, vsa_ref, vwa_ref, q_ref, o_ref, gt_ref,
                 h_ref, stage_ref):
    i = pl.program_id(0)
    j = pl.program_id(1)
    tm = x_ref.shape[0]
    dh = NSA_DIM
    ng = NSA_GROUPS
    tn = PROJ_TN
    qw = q_ref.shape[1]
    nb = pl.num_programs(1) - 1

    @pl.when(j == 0)
    def _():
        h_ref[...] = _rms(x_ref[...], g_ref[...]).astype(BF16)

    @pl.when(j < nb)
    def _():
        o_ref[...] = _dot(h_ref[...], wb_ref[...])

    @pl.when(j == nb)
    def _():
        h = h_ref[...]

        def tile(t):
            return _dot(h, wa_ref[:, qw + t * tn:qw + (t + 1) * tn])

        def cols(res, c):
            return res[:, c * dh:(c + 1) * dh].astype(BF16)

        def cols_t(res, c):
            slab = res[:, (c // 2) * 2 * dh:(c // 2 + 1) * 2 * dh].T
            return slab[(c % 2) * dh:(c % 2 + 1) * dh].astype(BF16)

        res = tile(0)
        for c in range(2 * ng):
            stage_ref[...] = res[:, c * dh:(c + 1) * dh]
            for t in range(0, CMP_STRIDE, 2):
                pair = [stage_ref[pl.ds(t + u, tm // CMP_STRIDE, stride=CMP_STRIDE), :] for u in range(2)]
                nat_ref[c // ng, c % ng, :, t * dh:(t + 2) * dh] = jnp.concatenate(pair, axis=1).astype(BF16)

        res = tile(1)
        pos = i * tm + lax.broadcasted_iota(jnp.int32, (tm, dh), 0)
        lane = lax.broadcasted_iota(jnp.int32, (tm, dh), 1)
        onehot = jnp.where((pos // SEL_BLOCK) % SEL_BPT == lane, 1.0, 0.0).astype(BF16)
        for g in range(ng):
            ksa_ref[g, :, :dh] = cols(res, g)
            ksa_ref[g, :, dh:] = onehot
            vsa_ref[g, :dh, :] = cols_t(res, ng + g)
            vsa_ref[g, dh:, :] = _ones_row(tm)

        res = tile(2)
        for g in range(ng):
            kw_ref[g] = cols(res, g)
            vwa_ref[g, :dh, :] = cols_t(res, ng + g)
            vwa_ref[g, dh:, :] = _ones_row(tm)

        o_gate = qw + N_KV_STREAMS // 2 * tn
        res = _dot(h, wa_ref[:, o_gate:o_gate + PROJ_GATE_TN])
        gt_ref[...] = res[:, :LANES].T

        for t in range(qw // tn):
            q_ref[:, t * tn:(t + 1) * tn] = _dot(h, wa_ref[:, t * tn:(t + 1) * tn]).astype(BF16)


def _proj(x, g, w_a, w_b, tm):
    s, d = x.shape
    dh, ng, tn = NSA_DIM, NSA_GROUPS, PROJ_TN
    qw = NSA_HEADS * dh
    assert 2 * ng * dh == tn and qw % tn == 0 and w_a.shape[1] == qw + N_KV_STREAMS // 2 * tn + PROJ_GATE_TN
    n = w_b.shape[1]
    assert n % tn == 0 and GATE_ROWS <= LANES <= PROJ_GATE_TN
    nb = n // tn
    shapes = [
        jax.ShapeDtypeStruct((2, ng, s // CMP_STRIDE, CMP_STRIDE * dh), BF16),
        jax.ShapeDtypeStruct((ng, s, dh), BF16),
        jax.ShapeDtypeStruct((ng, s, 2 * dh), BF16),
        jax.ShapeDtypeStruct((ng, dh + BF16_ROWS, s), BF16),
        jax.ShapeDtypeStruct((ng, dh + BF16_ROWS, s), BF16),
        jax.ShapeDtypeStruct((s, NSA_HEADS * dh), BF16),
        jax.ShapeDtypeStruct((s, n), F32),
        jax.ShapeDtypeStruct((LANES, s), F32),
    ]
    return pl.pallas_call(
        _proj_kernel,
        grid=(s // tm, 1 + nb),
        in_specs=[
            pl.BlockSpec((tm, d), lambda i, j: (i, 0)),
            pl.BlockSpec((1, d), lambda i, j: (0, 0)),
            pl.BlockSpec(w_a.shape, lambda i, j: (0, 0), pipeline_mode=pl.Buffered(1)),
            pl.BlockSpec((d, tn), lambda i, j: (0, jnp.minimum(j, nb - 1))),
        ],
        out_specs=[
            pl.BlockSpec((2, ng, tm // CMP_STRIDE, CMP_STRIDE * dh), lambda i, j: (0, 0, i, 0)),
            pl.BlockSpec((ng, tm, dh), lambda i, j: (0, i, 0)),
            pl.BlockSpec((ng, tm, 2 * dh), lambda i, j: (0, i, 0)),
            pl.BlockSpec((ng, dh + BF16_ROWS, tm), lambda i, j: (0, 0, i)),
            pl.BlockSpec((ng, dh + BF16_ROWS, tm), lambda i, j: (0, 0, i)),
            pl.BlockSpec((tm, qw), lambda i, j: (i, 0)),
            pl.BlockSpec((tm, tn), lambda i, j: (i, jnp.minimum(j, nb - 1))),
            pl.BlockSpec((LANES, tm), lambda i, j: (0, i)),
        ],
        out_shape=shapes,
        scratch_shapes=[pltpu.VMEM((tm, d), BF16), pltpu.VMEM((tm, dh), F32)],
        compiler_params=_params("parallel", "arbitrary"),
        name="proj",
    )(x, g.reshape(1, d), w_a, w_b)


def _compress_kernel(r_ref, w1_ref, pe_ref, b1_ref, w2_ref, o_ref, ot_ref):
    half = r_ref.shape[-1]
    r = r_ref[0, 0]
    bias = _dot(pe_ref[0].astype(BF16), w1_ref[0]) + b1_ref[0]
    top = _dot(r, w1_ref[0, :half, :])
    bot = _dot(r, w1_ref[0, half:, :])
    ncp = r.shape[0]
    hid = top + pltpu.roll(bot, ncp - 1, 0) + bias
    out = _dot(jax.nn.gelu(hid).astype(BF16), w2_ref[0])
    o_ref[0, 0] = out.astype(o_ref.dtype)
    dh = out.shape[1]
    out_t = jnp.concatenate([out, jnp.zeros((ncp, LANES - dh), F32)], axis=1).T
    ot_ref[0, 0, :dh, :] = out_t[:dh].astype(ot_ref.dtype)
    ot_ref[0, 0, dh:, :] = _ones_row(ncp)


def _compress(r, w1, pe, b1, w2):
    _, g, ncp, half = r.shape
    hidden = w1.shape[-1]
    dh = w2.shape[-1]
    return pl.pallas_call(
        _compress_kernel,
        grid=(2, g),
        in_specs=[
            pl.BlockSpec((1, 1, ncp, half), lambda s, gg: (s, gg, 0, 0)),
            pl.BlockSpec((1, 2 * half, hidden), lambda s, gg: (s, 0, 0)),
            pl.BlockSpec((1, 1, 2 * half), lambda s, gg: (s, 0, 0)),
            pl.BlockSpec((1, 1, hidden), lambda s, gg: (s, 0, 0)),
            pl.BlockSpec((1, hidden, dh), lambda s, gg: (s, 0, 0)),
        ],
        out_specs=[pl.BlockSpec((1, 1, ncp, dh), lambda s, gg: (s, gg, 0, 0)),
                   pl.BlockSpec((1, 1, dh + BF16_ROWS, ncp), lambda s, gg: (s, gg, 0, 0))],
        out_shape=[jax.ShapeDtypeStruct((2, g, ncp, dh), BF16),
                   jax.ShapeDtypeStruct((2, g, dh + BF16_ROWS, ncp), BF16)],
        compiler_params=_params("parallel", "parallel"),
        name="compress",
    )(r, w1, pe, b1, w2)


PICKED = -2.0


def _nsa_kernel(*refs, n_cast):
    q_ref, gt_ref, kc_ref, vct_ref, ks_ref, vst_ref, kw_ref, vwt_ref = refs[:8]
    cast_refs, refs = refs[8:8 + n_cast], refs[8 + n_cast:]
    o_ref, cast_out_refs, refs = refs[0], refs[1:1 + n_cast], refs[1 + n_cast:]
    qa_ref, sel_ref, part_ref, gates_ref, s0_ref, s1_ref, acc_ref, blkf_ref, cend_ref, wrel_ref, pp_ref = refs
    g = pl.program_id(0)
    step = pl.program_id(1)
    qb = QUERY_BLOCK
    hq = NSA_HPG * qb
    nb = sel_ref.shape[1]
    ncp = kc_ref.shape[1]
    s_len = ks_ref.shape[1]
    first = step * NSA_STEP_BLOCKS

    @pl.when(step == 0)
    def _():
        lane = lax.broadcasted_iota(jnp.int32, cend_ref.shape, 1)
        cend_ref[...] = lax.broadcasted_iota(jnp.int32, cend_ref.shape, 0) * CMP_STRIDE + (CMP_BLOCK - 1) - lane
        wrel_ref[...] = (lax.broadcasted_iota(jnp.int32, wrel_ref.shape, 0)
                         - lax.broadcasted_iota(jnp.int32, wrel_ref.shape, 1))

    def rows(sub):
        return pl.ds(pl.multiple_of(sub * qb, qb), qb)

    bucket_ok = s_len % (NSA_BUCKETS * SEL_TILE) == 0 and nb // NSA_BUCKETS >= SEL_TOPK
    n_bucket = NSA_BUCKETS if bucket_ok else 1
    bucket = ((first + NSA_STEP_BLOCKS) * qb - 1) // (s_len // n_bucket)
    for b in range(n_bucket):
        nck, nbk = (b + 1) * (ncp // n_bucket), (b + 1) * (nb // n_bucket)

        @pl.when(bucket == b)
        def _():
            def scores(sub, carry):
                own = pl.ds(pl.multiple_of(sub * hq, hq), hq)
                _nsa_scores(g, first + sub, q_ref.at[rows(sub), :], gt_ref.at[:, rows(sub)], kc_ref, vct_ref, kw_ref,
                            vwt_ref, qa_ref.at[:, own], sel_ref.at[sub], part_ref.at[sub], gates_ref.at[sub], pp_ref,
                            cend_ref, wrel_ref, nck, nbk)
                return carry

            lax.fori_loop(0, NSA_STEP_BLOCKS, scores, 0)
            _nsa_topk(first, sel_ref, blkf_ref, nbk)

    for src, dst in zip(cast_refs, cast_out_refs):
        dst[...] = src[...].astype(BF16)
    _nsa_select(g, step, ks_ref, vst_ref, o_ref, qa_ref, sel_ref, part_ref, gates_ref, s0_ref, s1_ref, acc_ref)


def _gate(gates_ref, g, branch):
    rows = [gates_ref[pl.ds((g * NSA_HPG + h) * N_BRANCH + branch, 1), :] for h in range(NSA_HPG)]
    return jax.nn.sigmoid(jnp.concatenate(rows, axis=1))


def _nsa_scores(g, i, q_ref, gt_ref, kc_ref, vct_ref, kw_ref, vwt_ref, qa_ref, score_ref, part_ref, gates_ref,
                pp_ref, cend_ref, wrel_ref, nck, nbk):
    qb = QUERY_BLOCK
    start = i * qb
    hq = NSA_HPG * qb
    t_row = start + lax.broadcasted_iota(jnp.int32, (1, qb), 1)
    jt = t_row // SEL_BLOCK

    qt = (q_ref[...].astype(F32) * (NSA_DIM ** -0.5 * LOG2E)).T
    q_t = jnp.concatenate([qt[h * NSA_DIM:(h + 1) * NSA_DIM] for h in range(NSA_HPG)], axis=1).astype(BF16)
    qa_ref[:NSA_DIM, :] = q_t
    qa_ref[NSA_DIM:, :] = jnp.zeros((qa_ref.shape[0] - NSA_DIM, hq), BF16)
    gates_ref[...] = gt_ref[...]

    def exps(s, bias):
        out = []
        for h in range(NSA_HPG):
            sh = s[:, h * qb:(h + 1) * qb] + bias
            out.append(jnp.exp2(sh - jnp.max(sh, axis=0, keepdims=True)))
        return out

    wk = WINDOW + qb
    k0w = pl.multiple_of(jnp.maximum(start - WINDOW, 0), qb)
    sc = _dot(kc_ref[0, :nck, :], q_t)
    sw = _dot(kw_ref[0, pl.ds(k0w, wk), :], q_t)
    e_cmp = exps(sc, jnp.where(cend_ref[:nck, :] <= start, 0.0, MASK_VALUE))
    o_cmp = _dot(vct_ref[0, :, :nck], jnp.concatenate([e.astype(BF16) for e in e_cmp], axis=1))
    wrel = wrel_ref[...]
    e_win = exps(sw, jnp.where((wrel <= start - k0w) & (wrel > start - k0w - WINDOW), 0.0, MASK_VALUE))
    o_win = _dot(vwt_ref[0, :, pl.ds(k0w, wk)], jnp.concatenate([e.astype(BF16) for e in e_win], axis=1))
    inv_cmp = jnp.where(jnp.concatenate([t_row >= CMP_BLOCK - 1] * NSA_HPG, axis=1),
                        1.0 / o_cmp[NSA_DIM:NSA_DIM + 1, :], 0.0)
    psum = e_cmp[0] * inv_cmp[:, :qb]
    for h in range(1, NSA_HPG):
        psum = psum + e_cmp[h] * inv_cmp[:, h * qb:(h + 1) * qb]
    part_ref[...] = (_gate(gates_ref, g, 0) * inv_cmp * o_cmp[:NSA_DIM]
                     + _gate(gates_ref, g, 2) * (1.0 / o_win[NSA_DIM:NSA_DIM + 1, :]) * o_win[:NSA_DIM])
    pp_ref[:SUBLANES, :] = jnp.zeros((SUBLANES, qb), F32)
    pp_ref[SUBLANES:SUBLANES + nck, :] = psum
    imp = None
    for k, w in enumerate(COV_BAND):
        tap = pp_ref[pl.ds(SUBLANES - COV_LEAD + k, nbk, stride=COV_RATIO), :]
        tap = tap if w == 1.0 else w * tap
        imp = tap if imp is None else imp + tap
    blk = lax.broadcasted_iota(jnp.int32, (nbk, qb), 0)
    forced = (blk == 0) | (blk == jt) | (blk == jt - 1)
    score_ref[:nbk, :] = jnp.where(forced, PICKED, jnp.where(blk <= jt, imp, -1.0))


def _nsa_topk(first, sel_ref, blkf_ref, nbk):
    qb = QUERY_BLOCK
    blk = lax.broadcasted_iota(jnp.int32, (nbk, qb), 0)
    blkf_ref[:nbk, :] = blk.astype(F32)

    def pick(_, scores):
        out = []
        for score in scores:
            mx = jnp.max(score, axis=0, keepdims=True)
            cand = jnp.where(score == mx, blkf_ref[:nbk, :], float(nbk))
            out.append(jnp.where(cand == jnp.min(cand, axis=0, keepdims=True), PICKED, score))
        return tuple(out)

    scores = tuple(sel_ref[b, :nbk, :] for b in range(sel_ref.shape[0]))
    scores = lax.fori_loop(0, min(SEL_TOPK, nbk) - N_FORCED, pick, scores)
    for b, score in enumerate(scores):
        jt = ((first + b) * qb + lax.broadcasted_iota(jnp.int32, (1, qb), 1)) // SEL_BLOCK
        sel_ref[b, :nbk, :] = jnp.where((score == PICKED) & (blk <= jt), 0.0, MASK_VALUE)


def _nsa_select(g, step, ks_ref, vst_ref, o_ref, qa_ref, sel_ref, part_ref, gates_ref, s0_ref, s1_ref, acc_ref):
    s_refs = (s0_ref, s1_ref)
    qb = QUERY_BLOCK
    hq = NSA_HPG * qb
    nbq = sel_ref.shape[0]
    lanes = nbq * hq
    tk = SEL_TILE

    def bias_rows(kt):
        rows = []
        for b in range(nbq):
            b8 = sel_ref[b, pl.ds(pl.multiple_of(kt * SEL_BPT, SEL_BPT), SEL_BPT), :]
            b16 = jnp.concatenate([b8, jnp.zeros((BF16_ROWS - SEL_BPT, qb), F32)], axis=0).astype(BF16)
            rows += [b16] * NSA_HPG
        qa_ref[NSA_DIM:NSA_DIM + BF16_ROWS, :] = jnp.concatenate(rows, axis=1)

    def scores(slot, kt):
        k0 = pl.multiple_of(kt * tk, tk)
        bias_rows(kt)
        s = _dot(ks_ref[0, pl.ds(k0, tk), :], qa_ref[...])
        s_refs[slot][...] = s
        return jnp.max(s, axis=0, keepdims=True)

    def update(slot, kt, mt, m, causal):
        k0 = pl.multiple_of(kt * tk, tk)
        if causal:
            tri = jnp.where(lax.broadcasted_iota(jnp.int32, (qb, qb), 0)
                            <= lax.broadcasted_iota(jnp.int32, (qb, qb), 1), 0.0, MASK_VALUE)
            for b in range(nbq):
                own = (slice(b * qb, (b + 1) * qb), slice(b * hq, (b + 1) * hq))
                s_refs[slot][own] = s_refs[slot][own] + jnp.concatenate([tri] * NSA_HPG, axis=1)
            mt = jnp.max(s_refs[slot][...], axis=0, keepdims=True)
        s = s_refs[slot][...]
        m_new = jnp.maximum(m, mt)
        p = jnp.exp2(s - m_new).astype(BF16)
        acc_ref[...] = jnp.exp2(m - m_new) * acc_ref[...] + _dot(vst_ref[0, :, pl.ds(k0, tk)], p)
        return m_new

    def accumulate(pend):
        p, k0, alpha, ln = pend
        acc_ref[:, ln] = alpha * acc_ref[:, ln] + _dot(vst_ref[0, :, pl.ds(k0, tk)], p)

    def run(first, n_tiles, carry):
        mt, m = carry
        pend = None
        for k in range(n_tiles):
            cur, nxt = s_refs[k % 2], s_refs[(k + 1) % 2]
            k0c = pl.multiple_of((first + k) * tk, tk)
            k0n = pl.multiple_of((first + k + 1) * tk, tk)
            bias_rows(first + k + 1)
            m_new = jnp.maximum(m, mt)
            alpha = jnp.exp2(m - m_new)
            mts = []
            for n in range(lanes // SEL_LANES):
                ln = slice(n * SEL_LANES, (n + 1) * SEL_LANES)
                s_n = _dot(ks_ref[0, pl.ds(k0n, tk), :], qa_ref[:, ln])
                nxt[:, ln] = s_n
                mts.append(jnp.max(s_n, axis=0, keepdims=True))
                p = jnp.exp2(cur[:, ln] - m_new[:, ln]).astype(BF16)
                if pend is not None:
                    accumulate(pend)
                pend = (p, k0c, alpha[:, ln], ln)
            mt = jnp.concatenate(mts, axis=1)
            m = m_new
        accumulate(pend)
        return mt, m

    acc_ref[...] = jnp.zeros(acc_ref.shape, F32)
    last = step
    carry = (scores(0, 0), jnp.full((1, lanes), MASK_VALUE, F32))
    done = 0
    for unroll in SEL_UNROLLS:
        trips = (last - done) // unroll
        carry = lax.fori_loop(0, trips, lambda it, c, d=done, u=unroll: run(d + it * u, u, c), carry)
        done = done + trips * unroll
    mt_a, m = carry

    @pl.when(last % 2 == 0)
    def _():
        update(0, last, mt_a, m, True)

    @pl.when(last % 2 == 1)
    def _():
        mt_b, m_b = run(last - 1, 1, (mt_a, m))
        update(1, last, mt_b, m_b, True)

    for b in range(nbq):
        own = slice(b * hq, (b + 1) * hq)
        o_sel = acc_ref[:NSA_DIM, own] * (1.0 / acc_ref[NSA_DIM:NSA_DIM + 1, own])
        out_t = part_ref[b] + _gate(gates_ref.at[b], g, 1) * o_sel
        outs = [out_t[:, h * qb:(h + 1) * qb] for h in range(NSA_HPG)]
        o_ref[b * qb:(b + 1) * qb, :] = jnp.concatenate(outs, axis=0).T.astype(o_ref.dtype)


def _cast_rows(rows, n_steps):
    for k in range(1, rows // BF16_ROWS + 1):
        if rows % (k * BF16_ROWS) == 0 and rows // (k * BF16_ROWS) <= n_steps:
            return k * BF16_ROWS
    raise ValueError(f"no row block for a side cast of {rows} rows in {n_steps} steps")


def _nsa(q, gates_t, kc, vc_aug_t, ks_aug, vs_aug_t, kw, vw_aug_t, casts):
    s = q.shape[0]
    assert s % (2 * SEL_TILE) == 0 and s >= WINDOW + QUERY_BLOCK
    gdim = NSA_HPG * NSA_DIM
    ncp = kc.shape[1]
    assert ncp * CMP_STRIDE == s and s // SEL_BLOCK * COV_RATIO == ncp
    nb = s // SEL_BLOCK
    ka = ks_aug.shape[-1]
    va = vs_aug_t.shape[1]
    hq = NSA_HPG * QUERY_BLOCK
    per_group = lambda shape: pl.BlockSpec((1,) + shape, lambda g, i: (g, 0, 0))
    rows = NSA_STEP_BLOCKS * QUERY_BLOCK
    assert s % rows == 0
    n_i = s // rows

    def cast_spec(w):
        rb = _cast_rows(w.shape[0], NSA_GROUPS * n_i)
        return pl.BlockSpec((rb, w.shape[1]), lambda g, i: (jnp.minimum(g * n_i + i, w.shape[0] // rb - 1), 0))

    return pl.pallas_call(
        functools.partial(_nsa_kernel, n_cast=len(casts)),
        grid=(NSA_GROUPS, n_i),
        in_specs=[
            pl.BlockSpec((rows, gdim), lambda g, i: (i, g)),
            pl.BlockSpec((GATE_ROWS, rows), lambda g, i: (0, i)),
            per_group((ncp, NSA_DIM)),
            per_group((va, ncp)),
            per_group((s, ka)),
            per_group((va, s)),
            per_group((s, NSA_DIM)),
            per_group((va, s)),
        ] + [cast_spec(w) for w in casts],
        out_specs=[pl.BlockSpec((rows, gdim), lambda g, i: (i, g))] + [cast_spec(w) for w in casts],
        out_shape=[jax.ShapeDtypeStruct((s, NSA_HEADS * NSA_DIM), BF16)]
        + [jax.ShapeDtypeStruct(w.shape, BF16) for w in casts],
        scratch_shapes=[pltpu.VMEM((ka, NSA_STEP_BLOCKS * hq), BF16),
                        pltpu.VMEM((NSA_STEP_BLOCKS, nb, QUERY_BLOCK), F32),
                        pltpu.VMEM((NSA_STEP_BLOCKS, NSA_DIM, hq), F32),
                        pltpu.VMEM((NSA_STEP_BLOCKS, GATE_ROWS, QUERY_BLOCK), F32),
                        pltpu.VMEM((SEL_TILE, NSA_STEP_BLOCKS * hq), F32),
                        pltpu.VMEM((SEL_TILE, NSA_STEP_BLOCKS * hq), F32),
                        pltpu.VMEM((va, NSA_STEP_BLOCKS * hq), F32),
                        pltpu.VMEM((nb, QUERY_BLOCK), F32),
                        pltpu.VMEM((ncp, QUERY_BLOCK), jnp.int32),
                        pltpu.VMEM((WINDOW + QUERY_BLOCK, QUERY_BLOCK), jnp.int32),
                        pltpu.VMEM((SUBLANES + ncp, QUERY_BLOCK), F32)],
        compiler_params=_params("parallel", "arbitrary"),
        name="nsa",
    )(q, gates_t, kc, vc_aug_t, ks_aug, vs_aug_t, kw, vw_aug_t, *casts)


def _sgu_kernel(u_ref, v_ref, lng_ref, lnb_ref, ws_ref, bs_ref, o_ref):
    c = SGU_CHUNK
    tm = u_ref.shape[0]
    v = jax.nn.gelu(v_ref[...])
    mu = jnp.mean(v, axis=-1, keepdims=True)
    var = jnp.mean(jnp.square(v - mu), axis=-1, keepdims=True)
    vn = ((v - mu) * lax.rsqrt(var + NORM_EPS) * lng_ref[...] + lnb_ref[...]).astype(BF16)
    u = jax.nn.gelu(u_ref[...])
    tri = lax.broadcasted_iota(jnp.int32, (c, c), 0) >= lax.broadcasted_iota(jnp.int32, (c, c), 1)
    for g in range(SGU_GROUPS):
        w = jnp.where(tri, ws_ref[g], 0.0).astype(BF16)
        cols = slice(g * c, (g + 1) * c)
        rhs = jnp.concatenate([vn[k * c:(k + 1) * c, cols] for k in range(tm // c)], axis=1)
        mixed = _dot(w, rhs)
        for k in range(tm // c):
            rows = slice(k * c, (k + 1) * c)
            o_ref[rows, cols] = (u[rows, cols] * (mixed[:, rows] + bs_ref[g])).astype(o_ref.dtype)


def _sgu(proj, lng, lnb, ws, bs, tm):
    s = proj.shape[0]
    w = lng.shape[0]
    c = SGU_CHUNK
    bs_b = jnp.broadcast_to(bs[:, :, None], (SGU_GROUPS, c, c))
    return pl.pallas_call(
        _sgu_kernel,
        grid=(s // tm,),
        in_specs=[
            pl.BlockSpec((tm, w), lambda i: (i, 0)),
            pl.BlockSpec((tm, w), lambda i: (i, 1)),
            pl.BlockSpec((1, w), lambda i: (0, 0)),
            pl.BlockSpec((1, w), lambda i: (0, 0)),
            pl.BlockSpec((SGU_GROUPS, c, c), lambda i: (0, 0, 0)),
            pl.BlockSpec((SGU_GROUPS, c, c), lambda i: (0, 0, 0)),
        ],
        out_specs=pl.BlockSpec((tm, w), lambda i: (i, 0)),
        out_shape=jax.ShapeDtypeStruct((s, w), BF16),
        compiler_params=_params("parallel"),
        name="sgu",
    )(proj, proj, lng.reshape(1, w), lnb.reshape(1, w), ws, bs_b)


def _mix_kernel(oa_ref, ob_ref, ga_ref, gb_ref, x_ref, pa_ref, pb_ref, wo_ref,
                gc_ref, wq_ref, mkt_ref, mv_ref, wmo_ref, o_ref):
    a = _dot(oa_ref[...], pa_ref[...])
    b = _dot(ob_ref[...], pb_ref[...])
    merged = jax.nn.sigmoid(ga_ref[...]) * a + jax.nn.sigmoid(gb_ref[...]) * b
    x = x_ref[...] + _dot(merged.astype(BF16), wo_ref[...])
    h = _rms(x, gc_ref[...]).astype(BF16)
    mq = (_dot(h, wq_ref[...]) * (MEM_DIM ** -0.5)).astype(BF16)
    outs = []
    for hh in range(MEM_HEADS):
        s = _dot(mq[:, hh * MEM_DIM:(hh + 1) * MEM_DIM], mkt_ref[hh])
        e = jnp.exp(s - jnp.max(s, axis=-1, keepdims=True))
        p = e / jnp.sum(e, axis=-1, keepdims=True)
        outs.append(_dot(p.astype(BF16), mv_ref[hh]).astype(BF16))
    o_ref[...] = x + _dot(jnp.concatenate(outs, axis=1), wmo_ref[...])


def _mix(o_a, o_b, proj, x, p_a, p_b, w_o, gate_block, g_cross, w_q, mk_t, mv, w_mo, tm):
    s, d = x.shape
    wa = o_a.shape[1]
    wb = o_b.shape[1]
    mw = w_q.shape[1]
    m = mv.shape[1]
    resident = lambda shape: pl.BlockSpec(shape, lambda i: (0,) * len(shape), pipeline_mode=pl.Buffered(1))
    return pl.pallas_call(
        _mix_kernel,
        grid=(s // tm,),
        in_specs=[
            pl.BlockSpec((tm, wa), lambda i: (i, 0)),
            pl.BlockSpec((tm, wb), lambda i: (i, 0)),
            pl.BlockSpec((tm, d), lambda i: (i, gate_block)),
            pl.BlockSpec((tm, d), lambda i: (i, gate_block + 1)),
            pl.BlockSpec((tm, d), lambda i: (i, 0)),
            resident((wa, d)),
            resident((wb, d)),
            resident((d, d)),
            resident((1, d)),
            resident((d, mw)),
            resident((MEM_HEADS, MEM_DIM, m)),
            resident((MEM_HEADS, m, MEM_DIM)),
            resident((mw, d)),
        ],
        out_specs=pl.BlockSpec((tm, d), lambda i: (i, 0)),
        out_shape=jax.ShapeDtypeStruct((s, d), F32),
        compiler_params=_params("parallel"),
        name="mix",
    )(o_a, o_b, proj, proj, x, p_a, p_b, w_o, g_cross.reshape(1, d), w_q, mk_t, mv, w_mo)


def _ffn_kernel(x_ref, g_ref, wg_ref, wu_ref, wo_ref, gf_ref, o_ref, h_ref, *, final_norm):
    j = pl.program_id(1)
    row_chunks = [slice(r, r + FFN_NORM_ROWS) for r in range(0, x_ref.shape[0], FFN_NORM_ROWS)]

    @pl.when(j == 0)
    def _():
        for rs in row_chunks:
            x = x_ref[rs, :]
            h_ref[rs, :] = _rms(x, g_ref[...]).astype(BF16)
            o_ref[rs, :] = x

    h = h_ref[...]
    th = wg_ref.shape[1]
    d = o_ref.shape[1]
    for c in range(th // FFN_CHUNK):
        cs = slice(c * FFN_CHUNK, (c + 1) * FFN_CHUNK)
        act = (jax.nn.silu(_dot(h, wg_ref[:, cs])) * _dot(h, wu_ref[:, cs])).astype(BF16)
        for n in range(d // FFN_OUT_TN):
            ns = slice(n * FFN_OUT_TN, (n + 1) * FFN_OUT_TN)
            o_ref[:, ns] += _dot(act, wo_ref[cs, ns])

    if final_norm:
        @pl.when(j == pl.num_programs(1) - 1)
        def _():
            for rs in row_chunks:
                o_ref[rs, :] = _rms(o_ref[rs, :], gf_ref[...])


def _ffn(x, g, w_in, w_out, g_final, final_norm, tm, th):
    s, d = x.shape
    hidden = w_out.shape[0]
    nh = hidden // th
    return pl.pallas_call(
        functools.partial(_ffn_kernel, final_norm=final_norm),
        grid=(s // tm, nh),
        in_specs=[
            pl.BlockSpec((tm, d), lambda i, j: (i, 0)),
            pl.BlockSpec((1, d), lambda i, j: (0, 0)),
            pl.BlockSpec((d, th), lambda i, j: (0, j)),
            pl.BlockSpec((d, th), lambda i, j: (0, j + nh)),
            pl.BlockSpec((th, d), lambda i, j: (j, 0)),
            pl.BlockSpec((1, d), lambda i, j: (0, 0)),
        ],
        out_specs=pl.BlockSpec((tm, d), lambda i, j: (i, 0)),
        out_shape=jax.ShapeDtypeStruct((s, d), F32),
        scratch_shapes=[pltpu.VMEM((tm, d), BF16)],
        compiler_params=_params("parallel", "arbitrary"),
        name="ffn",
    )(x, g.reshape(1, d), w_in, w_in, w_out, g_final.reshape(1, d))


def _layer(x, mem, norm_mix_g, w_in, cmp_pe_k, cmp_k_w1, cmp_k_b1, cmp_k_w2, cmp_pe_v, cmp_v_w1, cmp_v_b1, cmp_v_w2,
           sgu_ln_g, sgu_ln_b, sgu_ws, sgu_b, w_proj_a, w_proj_b, w_mix_out, norm_cross_g, norm_mem_g,
           w_mq, w_mkv, w_mo, w_ffn_in, w_ffn_out):
    s, d = x.shape
    qw = NSA_HEADS * NSA_DIM
    kvw = NSA_GROUPS * NSA_DIM
    sguw = sgu_ln_g.shape[0]
    ngate = GATE_ROWS
    o_kv = qw
    o_gate = o_kv + N_KV_STREAMS * kvw
    o_u = o_gate + ngate

    w_a = w_in[:, :o_gate + PROJ_GATE_TN].astype(BF16)
    w_b = w_in[:, o_u:].astype(BF16)
    nat, kw, ks_aug, vs_aug_t, vw_aug_t, q, proj_b, gates_t = _proj(x, norm_mix_g, w_a, w_b, min(PROJ_TM, s))

    w1 =jnp.stack([cmp_k_w1, cmp_v_w1]).astype(BF16)
    pe = jnp.stack([cmp_pe_k.reshape(1, -1), cmp_pe_v.reshape(1, -1)])
    b1 = jnp.stack([cmp_k_b1.reshape(1, -1), cmp_v_b1.reshape(1, -1)])
    w2 = jnp.stack([cmp_k_w2, cmp_v_w2]).astype(BF16)
    cmp, cmp_t = _compress(nat, w1, pe, b1, w2)
    o_a, w_ffn_in, w_ffn_out, w_proj_a, w_proj_b, w_mix_out, w_mq, w_mo = _nsa(
        q, gates_t, cmp[0], cmp_t[1], ks_aug, vs_aug_t, kw, vw_aug_t,
        [w_ffn_in, w_ffn_out, w_proj_a, w_proj_b, w_mix_out, w_mq, w_mo])

    o_b = _sgu(proj_b, sgu_ln_g, sgu_ln_b, sgu_ws, sgu_b, min(SGU_TM, s))

    m = mem.shape[0]
    mw = MEM_HEADS * MEM_DIM
    mkv = _norm_matmul(mem, norm_mem_g, w_mkv.astype(BF16), F32, m, mw)
    mk_t = mkv[:, :mw].reshape(m, MEM_HEADS, MEM_DIM).transpose(1, 2, 0).astype(BF16)
    mv = mkv[:, mw:].reshape(m, MEM_HEADS, MEM_DIM).transpose(1, 0, 2).astype(BF16)
    x = _mix(o_a, o_b, proj_b, x, w_proj_a, w_proj_b, w_mix_out, (2 * sguw) // d, norm_cross_g, w_mq, mk_t, mv, w_mo,
             min(MIX_TM, s))
    return x, w_ffn_in, w_ffn_out


def kernel(x, mem, norm_mix_g, w_in, cmp_pe_k, cmp_k_w1, cmp_k_b1, cmp_k_w2, cmp_pe_v, cmp_v_w1, cmp_v_b1, cmp_v_w2, sgu_ln_g, sgu_ln_b, sgu_ws, sgu_b, w_proj_a, w_proj_b, w_mix_out, norm_cross_g, norm_mem_g, w_mq, w_mkv, w_mo, norm_ffn_g, w_ffn_in, w_ffn_out, norm_final_g):
    b, s, d = x.shape
    depth = w_in.shape[0]
    outs = []
    for bi in range(b):
        xb = x[bi]
        for l in range(depth):
            last = l == depth - 1
            xb, wi, wo = _layer(xb, mem[bi], norm_mix_g[l], w_in[l], cmp_pe_k[l], cmp_k_w1[l], cmp_k_b1[l],
                                cmp_k_w2[l], cmp_pe_v[l], cmp_v_w1[l], cmp_v_b1[l], cmp_v_w2[l], sgu_ln_g[l],
                                sgu_ln_b[l], sgu_ws[l], sgu_b[l], w_proj_a[l], w_proj_b[l], w_mix_out[l],
                                norm_cross_g[l], norm_mem_g[l], w_mq[l], w_mkv[l], w_mo[l], w_ffn_in[l], w_ffn_out[l])
            xb = _ffn(xb, norm_ffn_g[l], wi, wo, norm_final_g, last, min(FFN_TM, s), FFN_TH)
        outs.append(xb)
    return jnp.stack(outs)
```

```python
import functools
import math

import jax
import jax.numpy as jnp
from jax import lax
from jax.experimental import pallas as pl
from jax.experimental.pallas import tpu as pltpu

F32 = jnp.float32
BF16 = jnp.bfloat16

NORM_EPS = 1e-6
MASK_VALUE = -1e30
N_FORCED = 3
N_BRANCH = 3
LOG2E = math.log2(math.e)

NSA_HEADS = 16
NSA_GROUPS = 4
NSA_HPG = NSA_HEADS // NSA_GROUPS
NSA_DIM = 64
GATE_ROWS = NSA_HEADS * N_BRANCH
N_KV_STREAMS = 6
CMP_BLOCK = 32
CMP_STRIDE = 16
SEL_BLOCK = 64
SEL_TOPK = 16
WINDOW = 512
QUERY_BLOCK = 128
SGU_GROUPS = 8
SGU_CHUNK = 128
MEM_HEADS = 4
MEM_DIM = 128

LANES = 128
SUBLANES = 8
COV_RATIO = SEL_BLOCK // CMP_STRIDE
COV_LEAD = CMP_BLOCK // CMP_STRIDE - 1
COV_BAND = tuple(
    max(min((k - COV_LEAD) * CMP_STRIDE + CMP_BLOCK, SEL_BLOCK) - max((k - COV_LEAD) * CMP_STRIDE, 0), 0) / CMP_BLOCK
    for k in range(COV_RATIO + COV_LEAD))
assert SEL_BLOCK % CMP_STRIDE == 0 and CMP_BLOCK % CMP_STRIDE == 0 and COV_LEAD <= SUBLANES
SEL_TILE = 512
SEL_BPT = SEL_TILE // SEL_BLOCK
SEL_LANES = 256
SEL_UNROLLS = (8, 4, 2)
NSA_STEP_BLOCKS = SEL_TILE // QUERY_BLOCK
NSA_BUCKETS = 8
BF16_ROWS = 16
FFN_CHUNK = 256
FFN_OUT_TN = 1024
FFN_NORM_ROWS = 256
PROJ_TM, SGU_TM, MIX_TM, FFN_TM, FFN_TH = 1024, 512, 256, 1024, 512
VMEM_LIMIT = 62 * 1024 * 1024


def _params(*sem):
    return pltpu.CompilerParams(dimension_semantics=sem, vmem_limit_bytes=VMEM_LIMIT)


def _rms(x, g):
    return x * lax.rsqrt(jnp.mean(x * x, axis=-1, keepdims=True) + NORM_EPS) * g


def _dot(a, b):
    return jnp.dot(a, b, preferred_element_type=F32)


def _ones_row(n):
    return jnp.where(lax.broadcasted_iota(jnp.int32, (BF16_ROWS, n), 0) == 0, 1.0, 0.0).astype(BF16)


def _norm_matmul_kernel(x_ref, g_ref, w_ref, o_ref, h_ref):
    @pl.when(pl.program_id(1) == 0)
    def _():
        h_ref[...] = _rms(x_ref[...], g_ref[...]).astype(BF16)

    o_ref[...] = _dot(h_ref[...], w_ref[...]).astype(o_ref.dtype)


def _norm_matmul(x, g, w, out_dtype, tm, tn):
    s, d = x.shape
    n = w.shape[1]
    return pl.pallas_call(
        _norm_matmul_kernel,
        grid=(s // tm, n // tn),
        in_specs=[
            pl.BlockSpec((tm, d), lambda i, j: (i, 0)),
            pl.BlockSpec((1, d), lambda i, j: (0, 0)),
            pl.BlockSpec((d, tn), lambda i, j: (0, j)),
        ],
        out_specs=pl.BlockSpec((tm, tn), lambda i, j: (i, j)),
        out_shape=jax.ShapeDtypeStruct((s, n), out_dtype),
        scratch_shapes=[pltpu.VMEM((tm, d), BF16)],
        compiler_params=_params("parallel", "arbitrary"),
        name="norm_matmul",
    )(x, g.reshape(1, d), w)


PROJ_TN = 512
PROJ_GATE_TN = 256


def _proj_kernel(x_ref, g_ref, wa_ref, wb_ref, nat_ref, kw_ref, ksa_ref, vsa_ref, vwa_ref, q_ref, o_ref, gt_ref,
                 h_ref, stage_ref):
    i = pl.program_id(0)
    j = pl.program_id(1)
    tm = x_ref.shape[0]
    dh = NSA_DIM
    ng = NSA_GROUPS
    tn = PROJ_TN
    qw = q_ref.shape[1]
    nb = pl.num_programs(1) - 1

    @pl.when(j == 0)
    def _():
        h_ref[...] = _rms(x_ref[...], g_ref[...]).astype(BF16)

    @pl.when(j < nb)
    def _():
        o_ref[...] = _dot(h_ref[...], wb_ref[...])

    @pl.when(j == nb)
    def _():
        h = h_ref[...]

        def tile(t):
            return _dot(h, wa_ref[:, qw + t * tn:qw + (t + 1) * tn])

        def cols(res, c):
            return res[:, c * dh:(c + 1) * dh].astype(BF16)

        def cols_t(res, c):
            slab = res[:, (c // 2) * 2 * dh:(c // 2 + 1) * 2 * dh].T
            return slab[(c % 2) * dh:(c % 2 + 1) * dh].astype(BF16)

        res = tile(0)
        for c in range(2 * ng):
            stage_ref[...] = res[:, c * dh:(c + 1) * dh]
            for t in range(0, CMP_STRIDE, 2):
                pair = [stage_ref[pl.ds(t + u, tm // CMP_STRIDE, stride=CMP_STRIDE), :] for u in range(2)]
                nat_ref[c // ng, c % ng, :, t * dh:(t + 2) * dh] = jnp.concatenate(pair, axis=1).astype(BF16)

        res = tile(1)
        pos = i * tm + lax.broadcasted_iota(jnp.int32, (tm, dh), 0)
        lane = lax.broadcasted_iota(jnp.int32, (tm, dh), 1)
        onehot = jnp.where((pos // SEL_BLOCK) % SEL_BPT == lane, 1.0, 0.0).astype(BF16)
        for g in range(ng):
            ksa_ref[g, :, :dh] = cols(res, g)
            ksa_ref[g, :, dh:] = onehot
            vsa_ref[g, :dh, :] = cols_t(res, ng + g)
            vsa_ref[g, dh:, :] = _ones_row(tm)

        res = tile(2)
        for g in range(ng):
            kw_ref[g] = cols(res, g)
            vwa_ref[g, :dh, :] = cols_t(res, ng + g)
            vwa_ref[g, dh:, :] = _ones_row(tm)

        o_gate = qw + N_KV_STREAMS // 2 * tn
        res = _dot(h, wa_ref[:, o_gate:o_gate + PROJ_GATE_TN])
        gt_ref[...] = res[:, :LANES].T

        for t in range(qw // tn):
            q_ref[:, t * tn:(t + 1) * tn] = _dot(h, wa_ref[:, t * tn:(t + 1) * tn]).astype(BF16)


def _proj(x, g, w_a, w_b, tm):
    s, d = x.shape
    dh, ng, tn = NSA_DIM, NSA_GROUPS, PROJ_TN
    qw = NSA_HEADS * dh
    assert 2 * ng * dh == tn and qw % tn == 0 and w_a.shape[1] == qw + N_KV_STREAMS // 2 * tn + PROJ_GATE_TN
    n = w_b.shape[1]
    assert n % tn == 0 and GATE_ROWS <= LANES <= PROJ_GATE_TN
    nb = n // tn
    shapes = [
        jax.ShapeDtypeStruct((2, ng, s // CMP_STRIDE, CMP_STRIDE * dh), BF16),
        jax.ShapeDtypeStruct((ng, s, dh), BF16),
        jax.ShapeDtypeStruct((ng, s, 2 * dh), BF16),
        jax.ShapeDtypeStruct((ng, dh + BF16_ROWS, s), BF16),
        jax.ShapeDtypeStruct((ng, dh + BF16_ROWS, s), BF16),
        jax.ShapeDtypeStruct((s, NSA_HEADS * dh), BF16),
        jax.ShapeDtypeStruct((s, n), F32),
        jax.ShapeDtypeStruct((LANES, s), F32),
    ]
    return pl.pallas_call(
        _proj_kernel,
        grid=(s // tm, 1 + nb),
        in_specs=[
            pl.BlockSpec((tm, d), lambda i, j: (i, 0)),
            pl.BlockSpec((1, d), lambda i, j: (0, 0)),
            pl.BlockSpec(w_a.shape, lambda i, j: (0, 0), pipeline_mode=pl.Buffered(1)),
            pl.BlockSpec((d, tn), lambda i, j: (0, jnp.minimum(j, nb - 1))),
        ],
        out_specs=[
            pl.BlockSpec((2, ng, tm // CMP_STRIDE, CMP_STRIDE * dh), lambda i, j: (0, 0, i, 0)),
            pl.BlockSpec((ng, tm, dh), lambda i, j: (0, i, 0)),
            pl.BlockSpec((ng, tm, 2 * dh), lambda i, j: (0, i, 0)),
            pl.BlockSpec((ng, dh + BF16_ROWS, tm), lambda i, j: (0, 0, i)),
            pl.BlockSpec((ng, dh + BF16_ROWS, tm), lambda i, j: (0, 0, i)),
            pl.BlockSpec((tm, qw), lambda i, j: (i, 0)),
            pl.BlockSpec((tm, tn), lambda i, j: (i, jnp.minimum(j, nb - 1))),
            pl.BlockSpec((LANES, tm), lambda i, j: (0, i)),
        ],
        out_shape=shapes,
        scratch_shapes=[pltpu.VMEM((tm, d), BF16), pltpu.VMEM((tm, dh), F32)],
        compiler_params=_params("parallel", "arbitrary"),
        name="proj",
    )(x, g.reshape(1, d), w_a, w_b)


def _compress_kernel(r_ref, w1_ref, pe_ref, b1_ref, w2_ref, o_ref, ot_ref):
    half = r_ref.shape[-1]
    r = r_ref[0, 0]
    bias = _dot(pe_ref[0].astype(BF16), w1_ref[0]) + b1_ref[0]
    top = _dot(r, w1_ref[0, :half, :])
    bot = _dot(r, w1_ref[0, half:, :])
    ncp = r.shape[0]
    hid = top + pltpu.roll(bot, ncp - 1, 0) + bias
    out = _dot(jax.nn.gelu(hid).astype(BF16), w2_ref[0])
    o_ref[0, 0] = out.astype(o_ref.dtype)
    dh = out.shape[1]
    out_t = jnp.concatenate([out, jnp.zeros((ncp, LANES - dh), F32)], axis=1).T
    ot_ref[0, 0, :dh, :] = out_t[:dh].astype(ot_ref.dtype)
    ot_ref[0, 0, dh:, :] = _ones_row(ncp)


def _compress(r, w1, pe, b1, w2):
    _, g, ncp, half = r.shape
    hidden = w1.shape[-1]
    dh = w2.shape[-1]
    return pl.pallas_call(
        _compress_kernel,
        grid=(2, g),
        in_specs=[
            pl.BlockSpec((1, 1, ncp, half), lambda s, gg: (s, gg, 0, 0)),
            pl.BlockSpec((1, 2 * half, hidden), lambda s, gg: (s, 0, 0)),
            pl.BlockSpec((1, 1, 2 * half), lambda s, gg: (s, 0, 0)),
            pl.BlockSpec((1, 1, hidden), lambda s, gg: (s, 0, 0)),
            pl.BlockSpec((1, hidden, dh), lambda s, gg: (s, 0, 0)),
        ],
        out_specs=[pl.BlockSpec((1, 1, ncp, dh), lambda s, gg: (s, gg, 0, 0)),
                   pl.BlockSpec((1, 1, dh + BF16_ROWS, ncp), lambda s, gg: (s, gg, 0, 0))],
        out_shape=[jax.ShapeDtypeStruct((2, g, ncp, dh), BF16),
                   jax.ShapeDtypeStruct((2, g, dh + BF16_ROWS, ncp), BF16)],
        compiler_params=_params("parallel", "parallel"),
        name="compress",
    )(r, w1, pe, b1, w2)


PICKED = -2.0


def _nsa_kernel(*refs, n_cast):
    q_ref, gt_ref, kc_ref, vct_ref, ks_ref, vst_ref, kw_ref, vwt_ref = refs[:8]
    cast_refs, refs = refs[8:8 + n_cast], refs[8 + n_cast:]
    o_ref, cast_out_refs, refs = refs[0], refs[1:1 + n_cast], refs[1 + n_cast:]
    qa_ref, sel_ref, part_ref, gates_ref, s0_ref, s1_ref, acc_ref, blkf_ref, cend_ref, wrel_ref, pp_ref = refs
    g = pl.program_id(0)
    step = pl.program_id(1)
    qb = QUERY_BLOCK
    hq = NSA_HPG * qb
    nb = sel_ref.shape[1]
    ncp = kc_ref.shape[1]
    s_len = ks_ref.shape[1]
    first = step * NSA_STEP_BLOCKS

    @pl.when(step == 0)
    def _():
        lane = lax.broadcasted_iota(jnp.int32, cend_ref.shape, 1)
        cend_ref[...] = lax.broadcasted_iota(jnp.int32, cend_ref.shape, 0) * CMP_STRIDE + (CMP_BLOCK - 1) - lane
        wrel_ref[...] = (lax.broadcasted_iota(jnp.int32, wrel_ref.shape, 0)
                         - lax.broadcasted_iota(jnp.int32, wrel_ref.shape, 1))

    def rows(sub):
        return pl.ds(pl.multiple_of(sub * qb, qb), qb)

    bucket_ok = s_len % (NSA_BUCKETS * SEL_TILE) == 0 and nb // NSA_BUCKETS >= SEL_TOPK
    n_bucket = NSA_BUCKETS if bucket_ok else 1
    bucket = ((first + NSA_STEP_BLOCKS) * qb - 1) // (s_len // n_bucket)
    for b in range(n_bucket):
        nck, nbk = (b + 1) * (ncp // n_bucket), (b + 1) * (nb // n_bucket)

        @pl.when(bucket == b)
        def _():
            def scores(sub, carry):
                own = pl.ds(pl.multiple_of(sub * hq, hq), hq)
                _nsa_scores(g, first + sub, q_ref.at[rows(sub), :], gt_ref.at[:, rows(sub)], kc_ref, vct_ref, kw_ref,
                            vwt_ref, qa_ref.at[:, own], sel_ref.at[sub], part_ref.at[sub], gates_ref.at[sub], pp_ref,
                            cend_ref, wrel_ref, nck, nbk)
                return carry

            lax.fori_loop(0, NSA_STEP_BLOCKS, scores, 0)
            _nsa_topk(first, sel_ref, blkf_ref, nbk)

    for src, dst in zip(cast_refs, cast_out_refs):
        dst[...] = src[...].astype(BF16)
    _nsa_select(g, step, ks_ref, vst_ref, o_ref, qa_ref, sel_ref, part_ref, gates_ref, s0_ref, s1_ref, acc_ref)


def _gate(gates_ref, g, branch):
    rows = [gates_ref[pl.ds((g * NSA_HPG + h) * N_BRANCH + branch, 1), :] for h in range(NSA_HPG)]
    return jax.nn.sigmoid(jnp.concatenate(rows, axis=1))


def _nsa_scores(g, i, q_ref, gt_ref, kc_ref, vct_ref, kw_ref, vwt_ref, qa_ref, score_ref, part_ref, gates_ref,
                pp_ref, cend_ref, wrel_ref, nck, nbk):
    qb = QUERY_BLOCK
    start = i * qb
    hq = NSA_HPG * qb
    t_row = start + lax.broadcasted_iota(jnp.int32, (1, qb), 1)
    jt = t_row // SEL_BLOCK

    qt = (q_ref[...].astype(F32) * (NSA_DIM ** -0.5 * LOG2E)).T
    q_t = jnp.concatenate([qt[h * NSA_DIM:(h + 1) * NSA_DIM] for h in range(NSA_HPG)], axis=1).astype(BF16)
    qa_ref[:NSA_DIM, :] = q_t
    qa_ref[NSA_DIM:, :] = jnp.zeros((qa_ref.shape[0] - NSA_DIM, hq), BF16)
    gates_ref[...] = gt_ref[...]

    def exps(s, bias):
        out = []
        for h in range(NSA_HPG):
            sh = s[:, h * qb:(h + 1) * qb] + bias
            out.append(jnp.exp2(sh - jnp.max(sh, axis=0, keepdims=True)))
        return out

    wk = WINDOW + qb
    k0w = pl.multiple_of(jnp.maximum(start - WINDOW, 0), qb)
    sc = _dot(kc_ref[0, :nck, :], q_t)
    sw = _dot(kw_ref[0, pl.ds(k0w, wk), :], q_t)
    e_cmp = exps(sc, jnp.where(cend_ref[:nck, :] <= start, 0.0, MASK_VALUE))
    o_cmp = _dot(vct_ref[0, :, :nck], jnp.concatenate([e.astype(BF16) for e in e_cmp], axis=1))
    wrel = wrel_ref[...]
    e_win = exps(sw, jnp.where((wrel <= start - k0w) & (wrel > start - k0w - WINDOW), 0.0, MASK_VALUE))
    o_win = _dot(vwt_ref[0, :, pl.ds(k0w, wk)], jnp.concatenate([e.astype(BF16) for e in e_win], axis=1))
    inv_cmp = jnp.where(jnp.concatenate([t_row >= CMP_BLOCK - 1] * NSA_HPG, axis=1),
                        1.0 / o_cmp[NSA_DIM:NSA_DIM + 1, :], 0.0)
    psum = e_cmp[0] * inv_cmp[:, :qb]
    for h in range(1, NSA_HPG):
        psum = psum + e_cmp[h] * inv_cmp[:, h * qb:(h + 1) * qb]
    part_ref[...] = (_gate(gates_ref, g, 0) * inv_cmp * o_cmp[:NSA_DIM]
                     + _gate(gates_ref, g, 2) * (1.0 / o_win[NSA_DIM:NSA_DIM + 1, :]) * o_win[:NSA_DIM])
    pp_ref[:SUBLANES, :] = jnp.zeros((SUBLANES, qb), F32)
    pp_ref[SUBLANES:SUBLANES + nck, :] = psum
    imp = None
    for k, w in enumerate(COV_BAND):
        tap = pp_ref[pl.ds(SUBLANES - COV_LEAD + k, nbk, stride=COV_RATIO), :]
        tap = tap if w == 1.0 else w * tap
        imp = tap if imp is None else imp + tap
    blk = lax.broadcasted_iota(jnp.int32, (nbk, qb), 0)
    forced = (blk == 0) | (blk == jt) | (blk == jt - 1)
    score_ref[:nbk, :] = jnp.where(forced, PICKED, jnp.where(blk <= jt, imp, -1.0))


def _nsa_topk(first, sel_ref, blkf_ref, nbk):
    qb = QUERY_BLOCK
    blk = lax.broadcasted_iota(jnp.int32, (nbk, qb), 0)
    blkf_ref[:nbk, :] = blk.astype(F32)

    def pick(_, scores):
        out = []
        for score in scores:
            mx = jnp.max(score, axis=0, keepdims=True)
            cand = jnp.where(score == mx, blkf_ref[:nbk, :], float(nbk))
            out.append(jnp.where(cand == jnp.min(cand, axis=0, keepdims=True), PICKED, score))
        return tuple(out)

    scores = tuple(sel_ref[b, :nbk, :] for b in range(sel_ref.shape[0]))
    scores = lax.fori_loop(0, min(SEL_TOPK, nbk) - N_FORCED, pick, scores)
    for b, score in enumerate(scores):
        jt = ((first + b) * qb + lax.broadcasted_iota(jnp.int32, (1, qb), 1)) // SEL_BLOCK
        sel_ref[b, :nbk, :] = jnp.where((score == PICKED) & (blk <= jt), 0.0, MASK_VALUE)


def _nsa_select(g, step, ks_ref, vst_ref, o_ref, qa_ref, sel_ref, part_ref, gates_ref, s0_ref, s1_ref, acc_ref):
    s_refs = (s0_ref, s1_ref)
    qb = QUERY_BLOCK
    hq = NSA_HPG * qb
    nbq = sel_ref.shape[0]
    lanes = nbq * hq
    tk = SEL_TILE

    def bias_rows(kt):
        rows = []
        for b in range(nbq):
            b8 = sel_ref[b, pl.ds(pl.multiple_of(kt * SEL_BPT, SEL_BPT), SEL_BPT), :]
            b16 = jnp.concatenate([b8, jnp.zeros((BF16_ROWS - SEL_BPT, qb), F32)], axis=0).astype(BF16)
            rows += [b16] * NSA_HPG
        qa_ref[NSA_DIM:NSA_DIM + BF16_ROWS, :] = jnp.concatenate(rows, axis=1)

    def scores(slot, kt):
        k0 = pl.multiple_of(kt * tk, tk)
        bias_rows(kt)
        s = _dot(ks_ref[0, pl.ds(k0, tk), :], qa_ref[...])
        s_refs[slot][...] = s
        return jnp.max(s, axis=0, keepdims=True)

    def update(slot, kt, mt, m, causal):
        k0 = pl.multiple_of(kt * tk, tk)
        if causal:
            tri = jnp.where(lax.broadcasted_iota(jnp.int32, (qb, qb), 0)
                            <= lax.broadcasted_iota(jnp.int32, (qb, qb), 1), 0.0, MASK_VALUE)
            for b in range(nbq):
                own = (slice(b * qb, (b + 1) * qb), slice(b * hq, (b + 1) * hq))
                s_refs[slot][own] = s_refs[slot][own] + jnp.concatenate([tri] * NSA_HPG, axis=1)
            mt = jnp.max(s_refs[slot][...], axis=0, keepdims=True)
        s = s_refs[slot][...]
        m_new = jnp.maximum(m, mt)
        p = jnp.exp2(s - m_new).astype(BF16)
        acc_ref[...] = jnp.exp2(m - m_new) * acc_ref[...] + _dot(vst_ref[0, :, pl.ds(k0, tk)], p)
        return m_new

    def accumulate(pend):
        p, k0, alpha, ln = pend
        acc_ref[:, ln] = alpha * acc_ref[:, ln] + _dot(vst_ref[0, :, pl.ds(k0, tk)], p)

    def run(first, n_tiles, carry):
        mt, m = carry
        pend = None
        for k in range(n_tiles):
            cur, nxt = s_refs[k % 2], s_refs[(k + 1) % 2]
            k0c = pl.multiple_of((first + k) * tk, tk)
            k0n = pl.multiple_of((first + k + 1) * tk, tk)
            bias_rows(first + k + 1)
            m_new = jnp.maximum(m, mt)
            alpha = jnp.exp2(m - m_new)
            mts = []
            for n in range(lanes // SEL_LANES):
                ln = slice(n * SEL_LANES, (n + 1) * SEL_LANES)
                s_n = _dot(ks_ref[0, pl.ds(k0n, tk), :], qa_ref[:, ln])
                nxt[:, ln] = s_n
                mts.append(jnp.max(s_n, axis=0, keepdims=True))
                p = jnp.exp2(cur[:, ln] - m_new[:, ln]).astype(BF16)
                if pend is not None:
                    accumulate(pend)
                pend = (p, k0c, alpha[:, ln], ln)
            mt = jnp.concatenate(mts, axis=1)
            m = m_new
        accumulate(pend)
        return mt, m

    acc_ref[...] = jnp.zeros(acc_ref.shape, F32)
    last = step
    carry = (scores(0, 0), jnp.full((1, lanes), MASK_VALUE, F32))
    done = 0
    for unroll in SEL_UNROLLS:
        trips = (last - done) // unroll
        carry = lax.fori_loop(0, trips, lambda it, c, d=done, u=unroll: run(d + it * u, u, c), carry)
        done = done + trips * unroll
    mt_a, m = carry

    @pl.when(last % 2 == 0)
    def _():
        update(0, last, mt_a, m, True)

    @pl.when(last % 2 == 1)
    def _():
        mt_b, m_b = run(last - 1, 1, (mt_a, m))
        update(1, last, mt_b, m_b, True)

    for b in range(nbq):
        own = slice(b * hq, (b + 1) * hq)
        o_sel = acc_ref[:NSA_DIM, own] * (1.0 / acc_ref[NSA_DIM:NSA_DIM + 1, own])
        out_t = part_ref[b] + _gate(gates_ref.at[b], g, 1) * o_sel
        outs = [out_t[:, h * qb:(h + 1) * qb] for h in range(NSA_HPG)]
        o_ref[b * qb:(b + 1) * qb, :] = jnp.concatenate(outs, axis=0).T.astype(o_ref.dtype)


def _cast_rows(rows, n_steps):
    for k in range(1, rows // BF16_ROWS + 1):
        if rows % (k * BF16_ROWS) == 0 and rows // (k * BF16_ROWS) <= n_steps:
            return k * BF16_ROWS
    raise ValueError(f"no row block for a side cast of {rows} rows in {n_steps} steps")


def _nsa(q, gates_t, kc, vc_aug_t, ks_aug, vs_aug_t, kw, vw_aug_t, casts):
    s = q.shape[0]
    assert s % (2 * SEL_TILE) == 0 and s >= WINDOW + QUERY_BLOCK
    gdim = NSA_HPG * NSA_DIM
    ncp = kc.shape[1]
    assert ncp * CMP_STRIDE == s and s // SEL_BLOCK * COV_RATIO == ncp
    nb = s // SEL_BLOCK
    ka = ks_aug.shape[-1]
    va = vs_aug_t.shape[1]
    hq = NSA_HPG * QUERY_BLOCK
    per_group = lambda shape: pl.BlockSpec((1,) + shape, lambda g, i: (g, 0, 0))
    rows = NSA_STEP_BLOCKS * QUERY_BLOCK
    assert s % rows == 0
    n_i = s // rows

    def cast_spec(w):
        rb = _cast_rows(w.shape[0], NSA_GROUPS * n_i)
        return pl.BlockSpec((rb, w.shape[1]), lambda g, i: (jnp.minimum(g * n_i + i, w.shape[0] // rb - 1), 0))

    return pl.pallas_call(
        functools.partial(_nsa_kernel, n_cast=len(casts)),
        grid=(NSA_GROUPS, n_i),
        in_specs=[
            pl.BlockSpec((rows, gdim), lambda g, i: (i, g)),
            pl.BlockSpec((GATE_ROWS, rows), lambda g, i: (0, i)),
            per_group((ncp, NSA_DIM)),
            per_group((va, ncp)),
            per_group((s, ka)),
            per_group((va, s)),
            per_group((s, NSA_DIM)),
            per_group((va, s)),
        ] + [cast_spec(w) for w in casts],
        out_specs=[pl.BlockSpec((rows, gdim), lambda g, i: (i, g))] + [cast_spec(w) for w in casts],
        out_shape=[jax.ShapeDtypeStruct((s, NSA_HEADS * NSA_DIM), BF16)]
        + [jax.ShapeDtypeStruct(w.shape, BF16) for w in casts],
        scratch_shapes=[pltpu.VMEM((ka, NSA_STEP_BLOCKS * hq), BF16),
                        pltpu.VMEM((NSA_STEP_BLOCKS, nb, QUERY_BLOCK), F32),
                        pltpu.VMEM((NSA_STEP_BLOCKS, NSA_DIM, hq), F32),
                        pltpu.VMEM((NSA_STEP_BLOCKS, GATE_ROWS, QUERY_BLOCK), F32),
                        pltpu.VMEM((SEL_TILE, NSA_STEP_BLOCKS * hq), F32),
                        pltpu.VMEM((SEL_TILE, NSA_STEP_BLOCKS * hq), F32),
                        pltpu.VMEM((va, NSA_STEP_BLOCKS * hq), F32),
                        pltpu.VMEM((nb, QUERY_BLOCK), F32),
                        pltpu.VMEM((ncp, QUERY_BLOCK), jnp.int32),
                        pltpu.VMEM((WINDOW + QUERY_BLOCK, QUERY_BLOCK), jnp.int32),
                        pltpu.VMEM((SUBLANES + ncp, QUERY_BLOCK), F32)],
        compiler_params=_params("parallel", "arbitrary"),
        name="nsa",
    )(q, gates_t, kc, vc_aug_t, ks_aug, vs_aug_t, kw, vw_aug_t, *casts)


def _sgu_kernel(u_ref, v_ref, lng_ref, lnb_ref, ws_ref, bs_ref, o_ref):
    c = SGU_CHUNK
    tm = u_ref.shape[0]
    v = jax.nn.gelu(v_ref[...])
    mu = jnp.mean(v, axis=-1, keepdims=True)
    var = jnp.mean(jnp.square(v - mu), axis=-1, keepdims=True)
    vn = ((v - mu) * lax.rsqrt(var + NORM_EPS) * lng_ref[...] + lnb_ref[...]).astype(BF16)
    u = jax.nn.gelu(u_ref[...])
    tri = lax.broadcasted_iota(jnp.int32, (c, c), 0) >= lax.broadcasted_iota(jnp.int32, (c, c), 1)
    for g in range(SGU_GROUPS):
        w = jnp.where(tri, ws_ref[g], 0.0).astype(BF16)
        cols = slice(g * c, (g + 1) * c)
        rhs = jnp.concatenate([vn[k * c:(k + 1) * c, cols] for k in range(tm // c)], axis=1)
        mixed = _dot(w, rhs)
        for k in range(tm // c):
            rows = slice(k * c, (k + 1) * c)
            o_ref[rows, cols] = (u[rows, cols] * (mixed[:, rows] + bs_ref[g])).astype(o_ref.dtype)


def _sgu(proj, lng, lnb, ws, bs, tm):
    s = proj.shape[0]
    w = lng.shape[0]
    c = SGU_CHUNK
    bs_b = jnp.broadcast_to(bs[:, :, None], (SGU_GROUPS, c, c))
    return pl.pallas_call(
        _sgu_kernel,
        grid=(s // tm,),
        in_specs=[
            pl.BlockSpec((tm, w), lambda i: (i, 0)),
            pl.BlockSpec((tm, w), lambda i: (i, 1)),
            pl.BlockSpec((1, w), lambda i: (0, 0)),
            pl.BlockSpec((1, w), lambda i: (0, 0)),
            pl.BlockSpec((SGU_GROUPS, c, c), lambda i: (0, 0, 0)),
            pl.BlockSpec((SGU_GROUPS, c, c), lambda i: (0, 0, 0)),
        ],
        out_specs=pl.BlockSpec((tm, w), lambda i: (i, 0)),
        out_shape=jax.ShapeDtypeStruct((s, w), BF16),
        compiler_params=_params("parallel"),
        name="sgu",
    )(proj, proj, lng.reshape(1, w), lnb.reshape(1, w), ws, bs_b)


def _mix_kernel(oa_ref, ob_ref, ga_ref, gb_ref, x_ref, pa_ref, pb_ref, wo_ref,
                gc_ref, wq_ref, mkt_ref, mv_ref, wmo_ref, o_ref):
    a = _dot(oa_ref[...], pa_ref[...])
    b = _dot(ob_ref[...], pb_ref[...])
    merged = jax.nn.sigmoid(ga_ref[...]) * a + jax.nn.sigmoid(gb_ref[...]) * b
    x = x_ref[...] + _dot(merged.astype(BF16), wo_ref[...])
    h = _rms(x, gc_ref[...]).astype(BF16)
    mq = (_dot(h, wq_ref[...]) * (MEM_DIM ** -0.5)).astype(BF16)
    outs = []
    for hh in range(MEM_HEADS):
        s = _dot(mq[:, hh * MEM_DIM:(hh + 1) * MEM_DIM], mkt_ref[hh])
        e = jnp.exp(s - jnp.max(s, axis=-1, keepdims=True))
        p = e / jnp.sum(e, axis=-1, keepdims=True)
        outs.append(_dot(p.astype(BF16), mv_ref[hh]).astype(BF16))
    o_ref[...] = x + _dot(jnp.concatenate(outs, axis=1), wmo_ref[...])


def _mix(o_a, o_b, proj, x, p_a, p_b, w_o, gate_block, g_cross, w_q, mk_t, mv, w_mo, tm):
    s, d = x.shape
    wa = o_a.shape[1]
    wb = o_b.shape[1]
    mw = w_q.shape[1]
    m = mv.shape[1]
    resident = lambda shape: pl.BlockSpec(shape, lambda i: (0,) * len(shape), pipeline_mode=pl.Buffered(1))
    return pl.pallas_call(
        _mix_kernel,
        grid=(s // tm,),
        in_specs=[
            pl.BlockSpec((tm, wa), lambda i: (i, 0)),
            pl.BlockSpec((tm, wb), lambda i: (i, 0)),
            pl.BlockSpec((tm, d), lambda i: (i, gate_block)),
            pl.BlockSpec((tm, d), lambda i: (i, gate_block + 1)),
            pl.BlockSpec((tm, d), lambda i: (i, 0)),
            resident((wa, d)),
            resident((wb, d)),
            resident((d, d)),
            resident((1, d)),
            resident((d, mw)),
            resident((MEM_HEADS, MEM_DIM, m)),
            resident((MEM_HEADS, m, MEM_DIM)),
            resident((mw, d)),
        ],
        out_specs=pl.BlockSpec((tm, d), lambda i: (i, 0)),
        out_shape=jax.ShapeDtypeStruct((s, d), F32),
        compiler_params=_params("parallel"),
        name="mix",
    )(o_a, o_b, proj, proj, x, p_a, p_b, w_o, g_cross.reshape(1, d), w_q, mk_t, mv, w_mo)


def _ffn_kernel(x_ref, g_ref, wg_ref, wu_ref, wo_ref, gf_ref, o_ref, h_ref, *, final_norm):
    j = pl.program_id(1)
    row_chunks = [slice(r, r + FFN_NORM_ROWS) for r in range(0, x_ref.shape[0], FFN_NORM_ROWS)]

    @pl.when(j == 0)
    def _():
        for rs in row_chunks:
            x = x_ref[rs, :]
            h_ref[rs, :] = _rms(x, g_ref[...]).astype(BF16)
            o_ref[rs, :] = x

    h = h_ref[...]
    th = wg_ref.shape[1]
    d = o_ref.shape[1]
    for c in range(th // FFN_CHUNK):
        cs = slice(c * FFN_CHUNK, (c + 1) * FFN_CHUNK)
        act = (jax.nn.silu(_dot(h, wg_ref[:, cs])) * _dot(h, wu_ref[:, cs])).astype(BF16)
        for n in range(d // FFN_OUT_TN):
            ns = slice(n * FFN_OUT_TN, (n + 1) * FFN_OUT_TN)
            o_ref[:, ns] += _dot(act, wo_ref[cs, ns])

    if final_norm:
        @pl.when(j == pl.num_programs(1) - 1)
        def _():
            for rs in row_chunks:
                o_ref[rs, :] = _rms(o_ref[rs, :], gf_ref[...])


def _ffn(x, g, w_in, w_out, g_final, final_norm, tm, th):
    s, d = x.shape
    hidden = w_out.shape[0]
    nh = hidden // th
    return pl.pallas_call(
        functools.partial(_ffn_kernel, final_norm=final_norm),
        grid=(s // tm, nh),
        in_specs=[
            pl.BlockSpec((tm, d), lambda i, j: (i, 0)),
            pl.BlockSpec((1, d), lambda i, j: (0, 0)),
            pl.BlockSpec((d, th), lambda i, j: (0, j)),
            pl.BlockSpec((d, th), lambda i, j: (0, j + nh)),
            pl.BlockSpec((th, d), lambda i, j: (j, 0)),
            pl.BlockSpec((1, d), lambda i, j: (0, 0)),
        ],
        out_specs=pl.BlockSpec((tm, d), lambda i, j: (i, 0)),
        out_shape=jax.ShapeDtypeStruct((s, d), F32),
        scratch_shapes=[pltpu.VMEM((tm, d), BF16)],
        compiler_params=_params("parallel", "arbitrary"),
        name="ffn",
    )(x, g.reshape(1, d), w_in, w_in, w_out, g_final.reshape(1, d))


def _layer(x, mem, norm_mix_g, w_in, cmp_pe_k, cmp_k_w1, cmp_k_b1, cmp_k_w2, cmp_pe_v, cmp_v_w1, cmp_v_b1, cmp_v_w2,
           sgu_ln_g, sgu_ln_b, sgu_ws, sgu_b, w_proj_a, w_proj_b, w_mix_out, norm_cross_g, norm_mem_g,
           w_mq, w_mkv, w_mo, w_ffn_in, w_ffn_out):
    s, d = x.shape
    qw = NSA_HEADS * NSA_DIM
    kvw = NSA_GROUPS * NSA_DIM
    sguw = sgu_ln_g.shape[0]
    ngate = GATE_ROWS
    o_kv = qw
    o_gate = o_kv + N_KV_STREAMS * kvw
    o_u = o_gate + ngate

    w_a = w_in[:, :o_gate + PROJ_GATE_TN].astype(BF16)
    w_b = w_in[:, o_u:].astype(BF16)
    nat, kw, ks_aug, vs_aug_t, vw_aug_t, q, proj_b, gates_t = _proj(x, norm_mix_g, w_a, w_b, min(PROJ_TM, s))

    w1 =jnp.stack([cmp_k_w1, cmp_v_w1]).astype(BF16)
    pe = jnp.stack([cmp_pe_k.reshape(1, -1), cmp_pe_v.reshape(1, -1)])
    b1 = jnp.stack([cmp_k_b1.reshape(1, -1), cmp_v_b1.reshape(1, -1)])
    w2 = jnp.stack([cmp_k_w2, cmp_v_w2]).astype(BF16)
    cmp, cmp_t = _compress(nat, w1, pe, b1, w2)
    o_a, w_ffn_in, w_ffn_out, w_proj_a, w_proj_b, w_mix_out, w_mq, w_mo = _nsa(
        q, gates_t, cmp[0], cmp_t[1], ks_aug, vs_aug_t, kw, vw_aug_t,
        [w_ffn_in, w_ffn_out, w_proj_a, w_proj_b, w_mix_out, w_mq, w_mo])

    o_b = _sgu(proj_b, sgu_ln_g, sgu_ln_b, sgu_ws, sgu_b, min(SGU_TM, s))

    m = mem.shape[0]
    mw = MEM_HEADS * MEM_DIM
    mkv = _norm_matmul(mem, norm_mem_g, w_mkv.astype(BF16), F32, m, mw)
    mk_t = mkv[:, :mw].reshape(m, MEM_HEADS, MEM_DIM).transpose(1, 2, 0).astype(BF16)
    mv = mkv[:, mw:].reshape(m, MEM_HEADS, MEM_DIM).transpose(1, 0, 2).astype(BF16)
    x = _mix(o_a, o_b, proj_b, x, w_proj_a, w_proj_b, w_mix_out, (2 * sguw) // d, norm_cross_g, w_mq, mk_t, mv, w_mo,
             min(MIX_TM, s))
    return x, w_ffn_in, w_ffn_out


def kernel(x, mem, norm_mix_g, w_in, cmp_pe_k, cmp_k_w1, cmp_k_b1, cmp_k_w2, cmp_pe_v, cmp_v_w1, cmp_v_b1, cmp_v_w2, sgu_ln_g, sgu_ln_b, sgu_ws, sgu_b, w_proj_a, w_proj_b, w_mix_out, norm_cross_g, norm_mem_g, w_mq, w_mkv, w_mo, norm_ffn_g, w_ffn_in, w_ffn_out, norm_final_g):
    b, s, d = x.shape
    depth = w_in.shape[0]
    outs = []
    for bi in range(b):
        xb = x[bi]
        for l in range(depth):
            last = l == depth - 1
            xb, wi, wo = _layer(xb, mem[bi], norm_mix_g[l], w_in[l], cmp_pe_k[l], cmp_k_w1[l], cmp_k_b1[l],
                                cmp_k_w2[l], cmp_pe_v[l], cmp_v_w1[l], cmp_v_b1[l], cmp_v_w2[l], sgu_ln_g[l],
                                sgu_ln_b[l], sgu_ws[l], sgu_b[l], w_proj_a[l], w_proj_b[l], w_mix_out[l],
                                norm_cross_g[l], norm_mem_g[l], w_mq[l], w_mkv[l], w_mo[l], w_ffn_in[l], w_ffn_out[l])
            xb = _ffn(xb, norm_ffn_g[l], wi, wo, norm_final_g, last, min(FFN_TM, s), FFN_TH)
        outs.append(xb)
    return jnp.stack(outs)
```

```python
import functools
import math

import jax
import jax.numpy as jnp
from jax import lax
from jax.experimental import pallas as pl
from jax.experimental.pallas import tpu as pltpu

F32 = jnp.float32
BF16 = jnp.bfloat16

NORM_EPS = 1e-6
MASK_VALUE = -1e30
N_FORCED = 3
N_BRANCH = 3
LOG2E = math.log2(math.e)

NSA_HEADS = 16
NSA_GROUPS = 4
NSA_HPG = NSA_HEADS // NSA_GROUPS
NSA_DIM = 64
GATE_ROWS = NSA_HEADS * N_BRANCH
N_KV_STREAMS = 6
CMP_BLOCK = 32
CMP_STRIDE = 16
SEL_BLOCK = 64
SEL_TOPK = 16
WINDOW = 512
QUERY_BLOCK = 128
SGU_GROUPS = 8
SGU_CHUNK = 128
MEM_HEADS = 4
MEM_DIM = 128

LANES = 128
SUBLANES = 8
COV_RATIO = SEL_BLOCK // CMP_STRIDE
COV_LEAD = CMP_BLOCK // CMP_STRIDE - 1
COV_BAND = tuple(
    max(min((k - COV_LEAD) * CMP_STRIDE + CMP_BLOCK, SEL_BLOCK) - max((k - COV_LEAD) * CMP_STRIDE, 0), 0) / CMP_BLOCK
    for k in range(COV_RATIO + COV_LEAD))
assert SEL_BLOCK % CMP_STRIDE == 0 and CMP_BLOCK % CMP_STRIDE == 0 and COV_LEAD <= SUBLANES
SEL_TILE = 512
SEL_BPT = SEL_TILE // SEL_BLOCK
SEL_LANES = 256
SEL_UNROLLS = (16, 8, 4, 2)
NSA_STEP_BLOCKS = SEL_TILE // QUERY_BLOCK
NSA_BUCKETS = 8
BF16_ROWS = 16
FFN_CHUNK = 256
FFN_OUT_TN = 1024
FFN_NORM_ROWS = 256
PROJ_TM, SGU_TM, MIX_TM, FFN_TM, FFN_TH = 1024, 512, 256, 1024, 512
VMEM_LIMIT = 62 * 1024 * 1024


def _params(*sem):
    return pltpu.CompilerParams(dimension_semantics=sem, vmem_limit_bytes=VMEM_LIMIT)


def _rms(x, g):
    return x * lax.rsqrt(jnp.mean(x * x, axis=-1, keepdims=True) + NORM_EPS) * g


def _dot(a, b):
    return jnp.dot(a, b, preferred_element_type=F32)


def _ones_row(n):
    return jnp.where(lax.broadcasted_iota(jnp.int32, (BF16_ROWS, n), 0) == 0, 1.0, 0.0).astype(BF16)


def _norm_matmul_kernel(x_ref, g_ref, w_ref, o_ref, h_ref):
    @pl.when(pl.program_id(1) == 0)
    def _():
        h_ref[...] = _rms(x_ref[...], g_ref[...]).astype(BF16)

    o_ref[...] = _dot(h_ref[...], w_ref[...]).astype(o_ref.dtype)


def _norm_matmul(x, g, w, out_dtype, tm, tn):
    s, d = x.shape
    n = w.shape[1]
    return pl.pallas_call(
        _norm_matmul_kernel,
        grid=(s // tm, n // tn),
        in_specs=[
            pl.BlockSpec((tm, d), lambda i, j: (i, 0)),
            pl.BlockSpec((1, d), lambda i, j: (0, 0)),
            pl.BlockSpec((d, tn), lambda i, j: (0, j)),
        ],
        out_specs=pl.BlockSpec((tm, tn), lambda i, j: (i, j)),
        out_shape=jax.ShapeDtypeStruct((s, n), out_dtype),
        scratch_shapes=[pltpu.VMEM((tm, d), BF16)],
        compiler_params=_params("parallel", "arbitrary"),
        name="norm_matmul",
    )(x, g.reshape(1, d), w)


PROJ_TN = 512
PROJ_GATE_TN = 256


def _proj_kernel(x_ref, g_ref, wa_ref, wb_ref, nat_ref, kw_ref, ksa_ref, vsa_ref, vwa_ref, q_ref, o_ref, gt_ref,
                 h_ref, stage_ref):
    i = pl.program_id(0)
    j = pl.program_id(1)
    tm = x_ref.shape[0]
    dh = NSA_DIM
    ng = NSA_GROUPS
    tn = PROJ_TN
    qw = q_ref.shape[1]
    nb = pl.num_programs(1) - 1

    @pl.when(j == 0)
    def _():
        h_ref[...] = _rms(x_ref[...], g_ref[...]).astype(BF16)

    @pl.when(j < nb)
    def _():
        o_ref[...] = _dot(h_ref[...], wb_ref[...])

    @pl.when(j == nb)
    def _():
        h = h_ref[...]

        def tile(t):
            return _dot(h, wa_ref[:, qw + t * tn:qw + (t + 1) * tn])

        def cols(res, c):
            return res[:, c * dh:(c + 1) * dh].astype(BF16)

        def cols_t(res, c):
            slab = res[:, (c // 2) * 2 * dh:(c // 2 + 1) * 2 * dh].T
            return slab[(c % 2) * dh:(c % 2 + 1) * dh].astype(BF16)

        res = tile(0)
        for c in range(2 * ng):
            stage_ref[...] = res[:, c * dh:(c + 1) * dh]
            for t in range(0, CMP_STRIDE, 2):
                pair = [stage_ref[pl.ds(t + u, tm // CMP_STRIDE, stride=CMP_STRIDE), :] for u in range(2)]
                nat_ref[c // ng, c % ng, :, t * dh:(t + 2) * dh] = jnp.concatenate(pair, axis=1).astype(BF16)

        res = tile(1)
        pos = i * tm + lax.broadcasted_iota(jnp.int32, (tm, dh), 0)
        lane = lax.broadcasted_iota(jnp.int32, (tm, dh), 1)
        onehot = jnp.where((pos // SEL_BLOCK) % SEL_BPT == lane, 1.0, 0.0).astype(BF16)
        for g in range(ng):
            ksa_ref[g, :, :dh] = cols(res, g)
            ksa_ref[g, :, dh:] = onehot
            vsa_ref[g, :dh, :] = cols_t(res, ng + g)
            vsa_ref[g, dh:, :] = _ones_row(tm)

        res = tile(2)
        for g in range(ng):
            kw_ref[g] = cols(res, g)
            vwa_ref[g, :dh, :] = cols_t(res, ng + g)
            vwa_ref[g, dh:, :] = _ones_row(tm)

        o_gate = qw + N_KV_STREAMS // 2 * tn
        res = _dot(h, wa_ref[:, o_gate:o_gate + PROJ_GATE_TN])
        gt_ref[...] = res[:, :LANES].T

        for t in range(qw // tn):
            q_ref[:, t * tn:(t + 1) * tn] = _dot(h, wa_ref[:, t * tn:(t + 1) * tn]).astype(BF16)


def _proj(x, g, w_a, w_b, tm):
    s, d = x.shape
    dh, ng, tn = NSA_DIM, NSA_GROUPS, PROJ_TN
    qw = NSA_HEADS * dh
    assert 2 * ng * dh == tn and qw % tn == 0 and w_a.shape[1] == qw + N_KV_STREAMS // 2 * tn + PROJ_GATE_TN
    n = w_b.shape[1]
    assert n % tn == 0 and GATE_ROWS <= LANES <= PROJ_GATE_TN
    nb = n // tn
    shapes = [
        jax.ShapeDtypeStruct((2, ng, s // CMP_STRIDE, CMP_STRIDE * dh), BF16),
        jax.ShapeDtypeStruct((ng, s, dh), BF16),
        jax.ShapeDtypeStruct((ng, s, 2 * dh), BF16),
        jax.ShapeDtypeStruct((ng, dh + BF16_ROWS, s), BF16),
        jax.ShapeDtypeStruct((ng, dh + BF16_ROWS, s), BF16),
        jax.ShapeDtypeStruct((s, NSA_HEADS * dh), BF16),
        jax.ShapeDtypeStruct((s, n), F32),
        jax.ShapeDtypeStruct((LANES, s), F32),
    ]
    return pl.pallas_call(
        _proj_kernel,
        grid=(s // tm, 1 + nb),
        in_specs=[
            pl.BlockSpec((tm, d), lambda i, j: (i, 0)),
            pl.BlockSpec((1, d), lambda i, j: (0, 0)),
            pl.BlockSpec(w_a.shape, lambda i, j: (0, 0), pipeline_mode=pl.Buffered(1)),
            pl.BlockSpec((d, tn), lambda i, j: (0, jnp.minimum(j, nb - 1))),
        ],
        out_specs=[
            pl.BlockSpec((2, ng, tm // CMP_STRIDE, CMP_STRIDE * dh), lambda i, j: (0, 0, i, 0)),
            pl.BlockSpec((ng, tm, dh), lambda i, j: (0, i, 0)),
            pl.BlockSpec((ng, tm, 2 * dh), lambda i, j: (0, i, 0)),
            pl.BlockSpec((ng, dh + BF16_ROWS, tm), lambda i, j: (0, 0, i)),
            pl.BlockSpec((ng, dh + BF16_ROWS, tm), lambda i, j: (0, 0, i)),
            pl.BlockSpec((tm, qw), lambda i, j: (i, 0)),
            pl.BlockSpec((tm, tn), lambda i, j: (i, jnp.minimum(j, nb - 1))),
            pl.BlockSpec((LANES, tm), lambda i, j: (0, i)),
        ],
        out_shape=shapes,
        scratch_shapes=[pltpu.VMEM((tm, d), BF16), pltpu.VMEM((tm, dh), F32)],
        compiler_params=_params("parallel", "arbitrary"),
        name="proj",
    )(x, g.reshape(1, d), w_a, w_b)


def _compress_kernel(r_ref, w1_ref, pe_ref, b1_ref, w2_ref, o_ref, ot_ref):
    half = r_ref.shape[-1]
    r = r_ref[0, 0]
    bias = _dot(pe_ref[0].astype(BF16), w1_ref[0]) + b1_ref[0]
    top = _dot(r, w1_ref[0, :half, :])
    bot = _dot(r, w1_ref[0, half:, :])
    ncp = r.shape[0]
    hid = top + pltpu.roll(bot, ncp - 1, 0) + bias
    out = _dot(jax.nn.gelu(hid).astype(BF16), w2_ref[0])
    o_ref[0, 0] = out.astype(o_ref.dtype)
    dh = out.shape[1]
    out_t = jnp.concatenate([out, jnp.zeros((ncp, LANES - dh), F32)], axis=1).T
    ot_ref[0, 0, :dh, :] = out_t[:dh].astype(ot_ref.dtype)
    ot_ref[0, 0, dh:, :] = _ones_row(ncp)


def _compress(r, w1, pe, b1, w2):
    _, g, ncp, half = r.shape
    hidden = w1.shape[-1]
    dh = w2.shape[-1]
    return pl.pallas_call(
        _compress_kernel,
        grid=(2, g),
        in_specs=[
            pl.BlockSpec((1, 1, ncp, half), lambda s, gg: (s, gg, 0, 0)),
            pl.BlockSpec((1, 2 * half, hidden), lambda s, gg: (s, 0, 0)),
            pl.BlockSpec((1, 1, 2 * half), lambda s, gg: (s, 0, 0)),
            pl.BlockSpec((1, 1, hidden), lambda s, gg: (s, 0, 0)),
            pl.BlockSpec((1, hidden, dh), lambda s, gg: (s, 0, 0)),
        ],
        out_specs=[pl.BlockSpec((1, 1, ncp, dh), lambda s, gg: (s, gg, 0, 0)),
                   pl.BlockSpec((1, 1, dh + BF16_ROWS, ncp), lambda s, gg: (s, gg, 0, 0))],
        out_shape=[jax.ShapeDtypeStruct((2, g, ncp, dh), BF16),
                   jax.ShapeDtypeStruct((2, g, dh + BF16_ROWS, ncp), BF16)],
        compiler_params=_params("parallel", "parallel"),
        name="compress",
    )(r, w1, pe, b1, w2)


PICKED = -2.0


def _nsa_kernel(*refs, n_cast):
    q_ref, gt_ref, kc_ref, vct_ref, ks_ref, vst_ref, kw_ref, vwt_ref = refs[:8]
    cast_refs, refs = refs[8:8 + n_cast], refs[8 + n_cast:]
    o_ref, cast_out_refs, refs = refs[0], refs[1:1 + n_cast], refs[1 + n_cast:]
    qa_ref, sel_ref, part_ref, gates_ref, s0_ref, s1_ref, acc_ref, blkf_ref, cend_ref, wrel_ref, pp_ref = refs
    g = pl.program_id(0)
    step = pl.program_id(1)
    qb = QUERY_BLOCK
    hq = NSA_HPG * qb
    nb = sel_ref.shape[1]
    ncp = kc_ref.shape[1]
    s_len = ks_ref.shape[1]
    first = step * NSA_STEP_BLOCKS

    @pl.when(step == 0)
    def _():
        lane = lax.broadcasted_iota(jnp.int32, cend_ref.shape, 1)
        cend_ref[...] = lax.broadcasted_iota(jnp.int32, cend_ref.shape, 0) * CMP_STRIDE + (CMP_BLOCK - 1) - lane
        wrel_ref[...] = (lax.broadcasted_iota(jnp.int32, wrel_ref.shape, 0)
                         - lax.broadcasted_iota(jnp.int32, wrel_ref.shape, 1))

    def rows(sub):
        return pl.ds(pl.multiple_of(sub * qb, qb), qb)

    bucket_ok = s_len % (NSA_BUCKETS * SEL_TILE) == 0 and nb // NSA_BUCKETS >= SEL_TOPK
    n_bucket = NSA_BUCKETS if bucket_ok else 1
    bucket = ((first + NSA_STEP_BLOCKS) * qb - 1) // (s_len // n_bucket)
    for b in range(n_bucket):
        nck, nbk = (b + 1) * (ncp // n_bucket), (b + 1) * (nb // n_bucket)

        @pl.when(bucket == b)
        def _():
            def scores(sub, carry):
                own = pl.ds(pl.multiple_of(sub * hq, hq), hq)
                _nsa_scores(g, first + sub, q_ref.at[rows(sub), :], gt_ref.at[:, rows(sub)], kc_ref, vct_ref, kw_ref,
                            vwt_ref, qa_ref.at[:, own], sel_ref.at[sub], part_ref.at[sub], gates_ref.at[sub], pp_ref,
                            cend_ref, wrel_ref, nck, nbk)
                return carry

            lax.fori_loop(0, NSA_STEP_BLOCKS, scores, 0)
            _nsa_topk(first, sel_ref, blkf_ref, nbk)

    for src, dst in zip(cast_refs, cast_out_refs):
        dst[...] = src[...].astype(BF16)
    _nsa_select(g, step, ks_ref, vst_ref, o_ref, qa_ref, sel_ref, part_ref, gates_ref, s0_ref, s1_ref, acc_ref)


def _gate(gates_ref, g, branch):
    rows = [gates_ref[pl.ds((g * NSA_HPG + h) * N_BRANCH + branch, 1), :] for h in range(NSA_HPG)]
    return jax.nn.sigmoid(jnp.concatenate(rows, axis=1))


def _nsa_scores(g, i, q_ref, gt_ref, kc_ref, vct_ref, kw_ref, vwt_ref, qa_ref, score_ref, part_ref, gates_ref,
                pp_ref, cend_ref, wrel_ref, nck, nbk):
    qb = QUERY_BLOCK
    start = i * qb
    hq = NSA_HPG * qb
    t_row = start + lax.broadcasted_iota(jnp.int32, (1, qb), 1)
    jt = t_row // SEL_BLOCK

    qt = (q_ref[...].astype(F32) * (NSA_DIM ** -0.5 * LOG2E)).T
    q_t = jnp.concatenate([qt[h * NSA_DIM:(h + 1) * NSA_DIM] for h in range(NSA_HPG)], axis=1).astype(BF16)
    qa_ref[:NSA_DIM, :] = q_t
    qa_ref[NSA_DIM:, :] = jnp.zeros((qa_ref.shape[0] - NSA_DIM, hq), BF16)
    gates_ref[...] = gt_ref[...]

    def exps(s, bias):
        out = []
        for h in range(NSA_HPG):
            sh = s[:, h * qb:(h + 1) * qb] + bias
            out.append(jnp.exp2(sh - jnp.max(sh, axis=0, keepdims=True)))
        return out

    wk = WINDOW + qb
    k0w = pl.multiple_of(jnp.maximum(start - WINDOW, 0), qb)
    sc = _dot(kc_ref[0, :nck, :], q_t)
    sw = _dot(kw_ref[0, pl.ds(k0w, wk), :], q_t)
    e_cmp = exps(sc, jnp.where(cend_ref[:nck, :] <= start, 0.0, MASK_VALUE))
    o_cmp = _dot(vct_ref[0, :, :nck], jnp.concatenate([e.astype(BF16) for e in e_cmp], axis=1))
    wrel = wrel_ref[...]
    e_win = exps(sw, jnp.where((wrel <= start - k0w) & (wrel > start - k0w - WINDOW), 0.0, MASK_VALUE))
    o_win = _dot(vwt_ref[0, :, pl.ds(k0w, wk)], jnp.concatenate([e.astype(BF16) for e in e_win], axis=1))
    inv_cmp = jnp.where(jnp.concatenate([t_row >= CMP_BLOCK - 1] * NSA_HPG, axis=1),
                        1.0 / o_cmp[NSA_DIM:NSA_DIM + 1, :], 0.0)
    psum = e_cmp[0] * inv_cmp[:, :qb]
    for h in range(1, NSA_HPG):
        psum = psum + e_cmp[h] * inv_cmp[:, h * qb:(h + 1) * qb]
    part_ref[...] = (_gate(gates_ref, g, 0) * inv_cmp * o_cmp[:NSA_DIM]
                     + _gate(gates_ref, g, 2) * (1.0 / o_win[NSA_DIM:NSA_DIM + 1, :]) * o_win[:NSA_DIM])
    pp_ref[:SUBLANES, :] = jnp.zeros((SUBLANES, qb), F32)
    pp_ref[SUBLANES:SUBLANES + nck, :] = psum
    imp = None
    for k, w in enumerate(COV_BAND):
        tap = pp_ref[pl.ds(SUBLANES - COV_LEAD + k, nbk, stride=COV_RATIO), :]
        tap = tap if w == 1.0 else w * tap
        imp = tap if imp is None else imp + tap
    blk = lax.broadcasted_iota(jnp.int32, (nbk, qb), 0)
    forced = (blk == 0) | (blk == jt) | (blk == jt - 1)
    score_ref[:nbk, :] = jnp.where(forced, PICKED, jnp.where(blk <= jt, imp, -1.0))


def _nsa_topk(first, sel_ref, blkf_ref, nbk):
    qb = QUERY_BLOCK
    blk = lax.broadcasted_iota(jnp.int32, (nbk, qb), 0)
    blkf_ref[:nbk, :] = blk.astype(F32)

    def pick(_, scores):
        out = []
        for score in scores:
            mx = jnp.max(score, axis=0, keepdims=True)
            cand = jnp.where(score == mx, blkf_ref[:nbk, :], float(nbk))
            out.append(jnp.where(cand == jnp.min(cand, axis=0, keepdims=True), PICKED, score))
        return tuple(out)

    scores = tuple(sel_ref[b, :nbk, :] for b in range(sel_ref.shape[0]))
    scores = lax.fori_loop(0, min(SEL_TOPK, nbk) - N_FORCED, pick, scores)
    for b, score in enumerate(scores):
        jt = ((first + b) * qb + lax.broadcasted_iota(jnp.int32, (1, qb), 1)) // SEL_BLOCK
        sel_ref[b, :nbk, :] = jnp.where((score == PICKED) & (blk <= jt), 0.0, MASK_VALUE)


def _nsa_select(g, step, ks_ref, vst_ref, o_ref, qa_ref, sel_ref, part_ref, gates_ref, s0_ref, s1_ref, acc_ref):
    s_refs = (s0_ref, s1_ref)
    qb = QUERY_BLOCK
    hq = NSA_HPG * qb
    nbq = sel_ref.shape[0]
    lanes = nbq * hq
    tk = SEL_TILE

    def bias_rows(kt):
        rows = []
        for b in range(nbq):
            b8 = sel_ref[b, pl.ds(pl.multiple_of(kt * SEL_BPT, SEL_BPT), SEL_BPT), :]
            b16 = jnp.concatenate([b8, jnp.zeros((BF16_ROWS - SEL_BPT, qb), F32)], axis=0).astype(BF16)
            rows += [b16] * NSA_HPG
        qa_ref[NSA_DIM:NSA_DIM + BF16_ROWS, :] = jnp.concatenate(rows, axis=1)

    def scores(slot, kt):
        k0 = pl.multiple_of(kt * tk, tk)
        bias_rows(kt)
        s = _dot(ks_ref[0, pl.ds(k0, tk), :], qa_ref[...])
        s_refs[slot][...] = s
        return jnp.max(s, axis=0, keepdims=True)

    def update(slot, kt, mt, m, causal):
        k0 = pl.multiple_of(kt * tk, tk)
        if causal:
            tri = jnp.where(lax.broadcasted_iota(jnp.int32, (qb, qb), 0)
                            <= lax.broadcasted_iota(jnp.int32, (qb, qb), 1), 0.0, MASK_VALUE)
            for b in range(nbq):
                own = (slice(b * qb, (b + 1) * qb), slice(b * hq, (b + 1) * hq))
                s_refs[slot][own] = s_refs[slot][own] + jnp.concatenate([tri] * NSA_HPG, axis=1)
            mt = jnp.max(s_refs[slot][...], axis=0, keepdims=True)
        s = s_refs[slot][...]
        m_new = jnp.maximum(m, mt)
        p = jnp.exp2(s - m_new).astype(BF16)
        acc_ref[...] = jnp.exp2(m - m_new) * acc_ref[...] + _dot(vst_ref[0, :, pl.ds(k0, tk)], p)
        return m_new

    def accumulate(pend):
        p, k0, alpha, ln = pend
        acc_ref[:, ln] = alpha * acc_ref[:, ln] + _dot(vst_ref[0, :, pl.ds(k0, tk)], p)

    def run(first, n_tiles, carry):
        mt, m = carry
        pend = None
        for k in range(n_tiles):
            cur, nxt = s_refs[k % 2], s_refs[(k + 1) % 2]
            k0c = pl.multiple_of((first + k) * tk, tk)
            k0n = pl.multiple_of((first + k + 1) * tk, tk)
            bias_rows(first + k + 1)
            m_new = jnp.maximum(m, mt)
            alpha = jnp.exp2(m - m_new)
            mts = []
            for n in range(lanes // SEL_LANES):
                ln = slice(n * SEL_LANES, (n + 1) * SEL_LANES)
                s_n = _dot(ks_ref[0, pl.ds(k0n, tk), :], qa_ref[:, ln])
                nxt[:, ln] = s_n
                mts.append(jnp.max(s_n, axis=0, keepdims=True))
                p = jnp.exp2(cur[:, ln] - m_new[:, ln]).astype(BF16)
                if pend is not None:
                    accumulate(pend)
                pend = (p, k0c, alpha[:, ln], ln)
            mt = jnp.concatenate(mts, axis=1)
            m = m_new
        accumulate(pend)
        return mt, m

    acc_ref[...] = jnp.zeros(acc_ref.shape, F32)
    last = step
    carry = (scores(0, 0), jnp.full((1, lanes), MASK_VALUE, F32))
    done = 0
    for unroll in SEL_UNROLLS:
        trips = (last - done) // unroll
        carry = lax.fori_loop(0, trips, lambda it, c, d=done, u=unroll: run(d + it * u, u, c), carry)
        done = done + trips * unroll
    mt_a, m = carry

    @pl.when(last % 2 == 0)
    def _():
        update(0, last, mt_a, m, True)

    @pl.when(last % 2 == 1)
    def _():
        mt_b, m_b = run(last - 1, 1, (mt_a, m))
        update(1, last, mt_b, m_b, True)

    for b in range(nbq):
        own = slice(b * hq, (b + 1) * hq)
        o_sel = acc_ref[:NSA_DIM, own] * (1.0 / acc_ref[NSA_DIM:NSA_DIM + 1, own])
        out_t = part_ref[b] + _gate(gates_ref.at[b], g, 1) * o_sel
        outs = [out_t[:, h * qb:(h + 1) * qb] for h in range(NSA_HPG)]
        o_ref[b * qb:(b + 1) * qb, :] = jnp.concatenate(outs, axis=0).T.astype(o_ref.dtype)


def _cast_rows(rows, n_steps):
    for k in range(1, rows // BF16_ROWS + 1):
        if rows % (k * BF16_ROWS) == 0 and rows // (k * BF16_ROWS) <= n_steps:
            return k * BF16_ROWS
    raise ValueError(f"no row block for a side cast of {rows} rows in {n_steps} steps")


def _nsa(q, gates_t, kc, vc_aug_t, ks_aug, vs_aug_t, kw, vw_aug_t, casts):
    s = q.shape[0]
    assert s % (2 * SEL_TILE) == 0 and s >= WINDOW + QUERY_BLOCK
    gdim = NSA_HPG * NSA_DIM
    ncp = kc.shape[1]
    assert ncp * CMP_STRIDE == s and s // SEL_BLOCK * COV_RATIO == ncp
    nb = s // SEL_BLOCK
    ka = ks_aug.shape[-1]
    va = vs_aug_t.shape[1]
    hq = NSA_HPG * QUERY_BLOCK
    per_group = lambda shape: pl.BlockSpec((1,) + shape, lambda g, i: (g, 0, 0))
    rows = NSA_STEP_BLOCKS * QUERY_BLOCK
    assert s % rows == 0
    n_i = s // rows

    def cast_spec(w):
        rb = _cast_rows(w.shape[0], NSA_GROUPS * n_i)
        return pl.BlockSpec((rb, w.shape[1]), lambda g, i: (jnp.minimum(g * n_i + i, w.shape[0] // rb - 1), 0))

    return pl.pallas_call(
        functools.partial(_nsa_kernel, n_cast=len(casts)),
        grid=(NSA_GROUPS, n_i),
        in_specs=[
            pl.BlockSpec((rows, gdim), lambda g, i: (i, g)),
            pl.BlockSpec((GATE_ROWS, rows), lambda g, i: (0, i)),
            per_group((ncp, NSA_DIM)),
            per_group((va, ncp)),
            per_group((s, ka)),
            per_group((va, s)),
            per_group((s, NSA_DIM)),
            per_group((va, s)),
        ] + [cast_spec(w) for w in casts],
        out_specs=[pl.BlockSpec((rows, gdim), lambda g, i: (i, g))] + [cast_spec(w) for w in casts],
        out_shape=[jax.ShapeDtypeStruct((s, NSA_HEADS * NSA_DIM), BF16)]
        + [jax.ShapeDtypeStruct(w.shape, BF16) for w in casts],
        scratch_shapes=[pltpu.VMEM((ka, NSA_STEP_BLOCKS * hq), BF16),
                        pltpu.VMEM((NSA_STEP_BLOCKS, nb, QUERY_BLOCK), F32),
                        pltpu.VMEM((NSA_STEP_BLOCKS, NSA_DIM, hq), F32),
                        pltpu.VMEM((NSA_STEP_BLOCKS, GATE_ROWS, QUERY_BLOCK), F32),
                        pltpu.VMEM((SEL_TILE, NSA_STEP_BLOCKS * hq), F32),
                        pltpu.VMEM((SEL_TILE, NSA_STEP_BLOCKS * hq), F32),
                        pltpu.VMEM((va, NSA_STEP_BLOCKS * hq), F32),
                        pltpu.VMEM((nb, QUERY_BLOCK), F32),
                        pltpu.VMEM((ncp, QUERY_BLOCK), jnp.int32),
                        pltpu.VMEM((WINDOW + QUERY_BLOCK, QUERY_BLOCK), jnp.int32),
                        pltpu.VMEM((SUBLANES + ncp, QUERY_BLOCK), F32)],
        compiler_params=_params("parallel", "arbitrary"),
        name="nsa",
    )(q, gates_t, kc, vc_aug_t, ks_aug, vs_aug_t, kw, vw_aug_t, *casts)


def _sgu_kernel(u_ref, v_ref, lng_ref, lnb_ref, ws_ref, bs_ref, o_ref):
    c = SGU_CHUNK
    tm = u_ref.shape[0]
    v = jax.nn.gelu(v_ref[...])
    mu = jnp.mean(v, axis=-1, keepdims=True)
    var = jnp.mean(jnp.square(v - mu), axis=-1, keepdims=True)
    vn = ((v - mu) * lax.rsqrt(var + NORM_EPS) * lng_ref[...] + lnb_ref[...]).astype(BF16)
    u = jax.nn.gelu(u_ref[...])
    tri = lax.broadcasted_iota(jnp.int32, (c, c), 0) >= lax.broadcasted_iota(jnp.int32, (c, c), 1)
    for g in range(SGU_GROUPS):
        w = jnp.where(tri, ws_ref[g], 0.0).astype(BF16)
        cols = slice(g * c, (g + 1) * c)
        rhs = jnp.concatenate([vn[k * c:(k + 1) * c, cols] for k in range(tm // c)], axis=1)
        mixed = _dot(w, rhs)
        for k in range(tm // c):
            rows = slice(k * c, (k + 1) * c)
            o_ref[rows, cols] = (u[rows, cols] * (mixed[:, rows] + bs_ref[g])).astype(o_ref.dtype)


def _sgu(proj, lng, lnb, ws, bs, tm):
    s = proj.shape[0]
    w = lng.shape[0]
    c = SGU_CHUNK
    bs_b = jnp.broadcast_to(bs[:, :, None], (SGU_GROUPS, c, c))
    return pl.pallas_call(
        _sgu_kernel,
        grid=(s // tm,),
        in_specs=[
            pl.BlockSpec((tm, w), lambda i: (i, 0)),
            pl.BlockSpec((tm, w), lambda i: (i, 1)),
            pl.BlockSpec((1, w), lambda i: (0, 0)),
            pl.BlockSpec((1, w), lambda i: (0, 0)),
            pl.BlockSpec((SGU_GROUPS, c, c), lambda i: (0, 0, 0)),
            pl.BlockSpec((SGU_GROUPS, c, c), lambda i: (0, 0, 0)),
        ],
        out_specs=pl.BlockSpec((tm, w), lambda i: (i, 0)),
        out_shape=jax.ShapeDtypeStruct((s, w), BF16),
        compiler_params=_params("parallel"),
        name="sgu",
    )(proj, proj, lng.reshape(1, w), lnb.reshape(1, w), ws, bs_b)


def _mix_kernel(oa_ref, ob_ref, ga_ref, gb_ref, x_ref, pa_ref, pb_ref, wo_ref,
                gc_ref, wq_ref, mkt_ref, mv_ref, wmo_ref, o_ref):
    a = _dot(oa_ref[...], pa_ref[...])
    b = _dot(ob_ref[...], pb_ref[...])
    merged = jax.nn.sigmoid(ga_ref[...]) * a + jax.nn.sigmoid(gb_ref[...]) * b
    x = x_ref[...] + _dot(merged.astype(BF16), wo_ref[...])
    h = _rms(x, gc_ref[...]).astype(BF16)
    mq = (_dot(h, wq_ref[...]) * (MEM_DIM ** -0.5)).astype(BF16)
    outs = []
    for hh in range(MEM_HEADS):
        s = _dot(mq[:, hh * MEM_DIM:(hh + 1) * MEM_DIM], mkt_ref[hh])
        e = jnp.exp(s - jnp.max(s, axis=-1, keepdims=True))
        p = e / jnp.sum(e, axis=-1, keepdims=True)
        outs.append(_dot(p.astype(BF16), mv_ref[hh]).astype(BF16))
    o_ref[...] = x + _dot(jnp.concatenate(outs, axis=1), wmo_ref[...])


def _mix(o_a, o_b, proj, x, p_a, p_b, w_o, gate_block, g_cross, w_q, mk_t, mv, w_mo, tm):
    s, d = x.shape
    wa = o_a.shape[1]
    wb = o_b.shape[1]
    mw = w_q.shape[1]
    m = mv.shape[1]
    resident = lambda shape: pl.BlockSpec(shape, lambda i: (0,) * len(shape), pipeline_mode=pl.Buffered(1))
    return pl.pallas_call(
        _mix_kernel,
        grid=(s // tm,),
        in_specs=[
            pl.BlockSpec((tm, wa), lambda i: (i, 0)),
            pl.BlockSpec((tm, wb), lambda i: (i, 0)),
            pl.BlockSpec((tm, d), lambda i: (i, gate_block)),
            pl.BlockSpec((tm, d), lambda i: (i, gate_block + 1)),
            pl.BlockSpec((tm, d), lambda i: (i, 0)),
            resident((wa, d)),
            resident((wb, d)),
            resident((d, d)),
            resident((1, d)),
            resident((d, mw)),
            resident((MEM_HEADS, MEM_DIM, m)),
            resident((MEM_HEADS, m, MEM_DIM)),
            resident((mw, d)),
        ],
        out_specs=pl.BlockSpec((tm, d), lambda i: (i, 0)),
        out_shape=jax.ShapeDtypeStruct((s, d), F32),
        compiler_params=_params("parallel"),
        name="mix",
    )(o_a, o_b, proj, proj, x, p_a, p_b, w_o, g_cross.reshape(1, d), w_q, mk_t, mv, w_mo)


def _ffn_kernel(x_ref, g_ref, wg_ref, wu_ref, wo_ref, gf_ref, o_ref, h_ref, *, final_norm):
    j = pl.program_id(1)
    row_chunks = [slice(r, r + FFN_NORM_ROWS) for r in range(0, x_ref.shape[0], FFN_NORM_ROWS)]

    @pl.when(j == 0)
    def _():
        for rs in row_chunks:
            x = x_ref[rs, :]
            h_ref[rs, :] = _rms(x, g_ref[...]).astype(BF16)
            o_ref[rs, :] = x

    h = h_ref[...]
    th = wg_ref.shape[1]
    d = o_ref.shape[1]
    for c in range(th // FFN_CHUNK):
        cs = slice(c * FFN_CHUNK, (c + 1) * FFN_CHUNK)
        act = (jax.nn.silu(_dot(h, wg_ref[:, cs])) * _dot(h, wu_ref[:, cs])).astype(BF16)
        for n in range(d // FFN_OUT_TN):
            ns = slice(n * FFN_OUT_TN, (n + 1) * FFN_OUT_TN)
            o_ref[:, ns] += _dot(act, wo_ref[cs, ns])

    if final_norm:
        @pl.when(j == pl.num_programs(1) - 1)
        def _():
            for rs in row_chunks:
                o_ref[rs, :] = _rms(o_ref[rs, :], gf_ref[...])


def _ffn(x, g, w_in, w_out, g_final, final_norm, tm, th):
    s, d = x.shape
    hidden = w_out.shape[0]
    nh = hidden // th
    return pl.pallas_call(
        functools.partial(_ffn_kernel, final_norm=final_norm),
        grid=(s // tm, nh),
        in_specs=[
            pl.BlockSpec((tm, d), lambda i, j: (i, 0)),
            pl.BlockSpec((1, d), lambda i, j: (0, 0)),
            pl.BlockSpec((d, th), lambda i, j: (0, j)),
            pl.BlockSpec((d, th), lambda i, j: (0, j + nh)),
            pl.BlockSpec((th, d), lambda i, j: (j, 0)),
            pl.BlockSpec((1, d), lambda i, j: (0, 0)),
        ],
        out_specs=pl.BlockSpec((tm, d), lambda i, j: (i, 0)),
        out_shape=jax.ShapeDtypeStruct((s, d), F32),
        scratch_shapes=[pltpu.VMEM((tm, d), BF16)],
        compiler_params=_params("parallel", "arbitrary"),
        name="ffn",
    )(x, g.reshape(1, d), w_in, w_in, w_out, g_final.reshape(1, d))


def _layer(x, mem, norm_mix_g, w_in, cmp_pe_k, cmp_k_w1, cmp_k_b1, cmp_k_w2, cmp_pe_v, cmp_v_w1, cmp_v_b1, cmp_v_w2,
           sgu_ln_g, sgu_ln_b, sgu_ws, sgu_b, w_proj_a, w_proj_b, w_mix_out, norm_cross_g, norm_mem_g,
           w_mq, w_mkv, w_mo, w_ffn_in, w_ffn_out):
    s, d = x.shape
    qw = NSA_HEADS * NSA_DIM
    kvw = NSA_GROUPS * NSA_DIM
    sguw = sgu_ln_g.shape[0]
    ngate = GATE_ROWS
    o_kv = qw
    o_gate = o_kv + N_KV_STREAMS * kvw
    o_u = o_gate + ngate

    w_a = w_in[:, :o_gate + PROJ_GATE_TN].astype(BF16)
    w_b = w_in[:, o_u:].astype(BF16)
    nat, kw, ks_aug, vs_aug_t, vw_aug_t, q, proj_b, gates_t = _proj(x, norm_mix_g, w_a, w_b, min(PROJ_TM, s))

    w1 =jnp.stack([cmp_k_w1, cmp_v_w1]).astype(BF16)
    pe = jnp.stack([cmp_pe_k.reshape(1, -1), cmp_pe_v.reshape(1, -1)])
    b1 = jnp.stack([cmp_k_b1.reshape(1, -1), cmp_v_b1.reshape(1, -1)])
    w2 = jnp.stack([cmp_k_w2, cmp_v_w2]).astype(BF16)
    cmp, cmp_t = _compress(nat, w1, pe, b1, w2)
    o_a, w_ffn_in, w_ffn_out, w_proj_a, w_proj_b, w_mix_out, w_mq, w_mo = _nsa(
        q, gates_t, cmp[0], cmp_t[1], ks_aug, vs_aug_t, kw, vw_aug_t,
        [w_ffn_in, w_ffn_out, w_proj_a, w_proj_b, w_mix_out, w_mq, w_mo])

    o_b = _sgu(proj_b, sgu_ln_g, sgu_ln_b, sgu_ws, sgu_b, min(SGU_TM, s))

    m = mem.shape[0]
    mw = MEM_HEADS * MEM_DIM
    mkv = _norm_matmul(mem, norm_mem_g, w_mkv.astype(BF16), F32, m, mw)
    mk_t = mkv[:, :mw].reshape(m, MEM_HEADS, MEM_DIM).transpose(1, 2, 0).astype(BF16)
    mv = mkv[:, mw:].reshape(m, MEM_HEADS, MEM_DIM).transpose(1, 0, 2).astype(BF16)
    x = _mix(o_a, o_b, proj_b, x, w_proj_a, w_proj_b, w_mix_out, (2 * sguw) // d, norm_cross_g, w_mq, mk_t, mv, w_mo,
             min(MIX_TM, s))
    return x, w_ffn_in, w_ffn_out


def kernel(x, mem, norm_mix_g, w_in, cmp_pe_k, cmp_k_w1, cmp_k_b1, cmp_k_w2, cmp_pe_v, cmp_v_w1, cmp_v_b1, cmp_v_w2, sgu_ln_g, sgu_ln_b, sgu_ws, sgu_b, w_proj_a, w_proj_b, w_mix_out, norm_cross_g, norm_mem_g, w_mq, w_mkv, w_mo, norm_ffn_g, w_ffn_in, w_ffn_out, norm_final_g):
    b, s, d = x.shape
    depth = w_in.shape[0]
    outs = []
    for bi in range(b):
        xb = x[bi]
        for l in range(depth):
            last = l == depth - 1
            xb, wi, wo = _layer(xb, mem[bi], norm_mix_g[l], w_in[l], cmp_pe_k[l], cmp_k_w1[l], cmp_k_b1[l],
                                cmp_k_w2[l], cmp_pe_v[l], cmp_v_w1[l], cmp_v_b1[l], cmp_v_w2[l], sgu_ln_g[l],
                                sgu_ln_b[l], sgu_ws[l], sgu_b[l], w_proj_a[l], w_proj_b[l], w_mix_out[l],
                                norm_cross_g[l], norm_mem_g[l], w_mq[l], w_mkv[l], w_mo[l], w_ffn_in[l], w_ffn_out[l])
            xb = _ffn(xb, norm_ffn_g[l], wi, wo, norm_final_g, last, min(FFN_TM, s), FFN_TH)
        outs.append(xb)
    return jnp.stack(outs)
```
